```python
import math
import jax, jax.numpy as jnp
from jax import lax
import numpy as np

D_MODEL = 2048
BATCH = 4
SEQ = 2048
DEPTH = 2
DEC_BATCH = 32
DEC_SEQ = 1
PAST_LEN = 16384
PAGE_SIZE = 128

N_A_LAYERS = DEPTH // 2
N_B_LAYERS = DEPTH - N_A_LAYERS
LRU_WIDTH = D_MODEL
LRU_BLOCKS = 8
LRU_BLOCK_W = LRU_WIDTH // LRU_BLOCKS
CONV_W = 4
LRU_C = 8.0
HEAD_DIM = 64
N_Q_HEADS = D_MODEL // HEAD_DIM
N_KV_HEADS = 8
GROUP = N_Q_HEADS // N_KV_HEADS
ATT_WIDTH = N_Q_HEADS * HEAD_DIM
KV_WIDTH = N_KV_HEADS * HEAD_DIM
WINDOW = 128
BLOCK = WINDOW
N_BUCKETS = 32
MAX_DISTANCE = 128
RMS_EPS = 1e-6
NEG_INF = -1e30

kernel_name = "yoco_rglru_swa_sink_decoder_step"


def rmsnorm(x, g):
    xf = x.astype(jnp.float32)
    y = xf * lax.rsqrt(jnp.mean(xf * xf, axis=-1, keepdims=True) + RMS_EPS) * g.astype(jnp.float32)
    return y.astype(x.dtype)


def rel_buckets(dist):
    n = jnp.maximum(dist, 0)
    max_exact = N_BUCKETS // 2
    nf = jnp.maximum(n, 1).astype(jnp.float32)
    large = max_exact + (jnp.log(nf / max_exact) / math.log(MAX_DISTANCE / max_exact)
                         * (N_BUCKETS - max_exact)).astype(jnp.int32)
    large = jnp.minimum(large, N_BUCKETS - 1)
    return jnp.where(n < max_exact, n, large)


def rel_bias(dist, table):
    b = table.astype(jnp.float32)[rel_buckets(dist)]
    b = jnp.transpose(b, (2, 0, 1))
    return b.reshape(N_KV_HEADS, GROUP, b.shape[1], b.shape[2])


def lru_combine(left, right):
    a_l, b_l = left
    a_r, b_r = right
    return a_l * a_r, a_r * b_l + b_r


def rglru_layer(x, conv_prev, h0, g_pre, g_post, w_in, conv_w, conv_b, w_r, b_r, w_i, b_i, lam, w_out):
    B, T, _ = x.shape
    xn = rmsnorm(x, g_pre)
    u, gate = jnp.split(xn @ w_in, 2, axis=-1)
    ext = jnp.concatenate([conv_prev.astype(u.dtype), u], axis=1)
    conv = conv_b
    for tap in range(CONV_W):
        conv = conv + ext[:, tap:tap + T] * conv_w[tap]
    new_conv = ext[:, T:]
    ub = conv.reshape(B, T, LRU_BLOCKS, LRU_BLOCK_W)
    r = jax.nn.sigmoid(jnp.einsum('btnc,ncd->btnd', ub, w_r).reshape(B, T, LRU_WIDTH) + b_r)
    i = jax.nn.sigmoid(jnp.einsum('btnc,ncd->btnd', ub, w_i).reshape(B, T, LRU_WIDTH) + b_i)
    log_a = -LRU_C * r.astype(jnp.float32) * jax.nn.softplus(-lam.astype(jnp.float32))
    a = jnp.exp(log_a)
    mult = jnp.sqrt(-jnp.expm1(2.0 * log_a))
    bterm = mult * (i * conv).astype(jnp.float32)
    bterm = bterm.at[:, 0].add(a[:, 0] * h0.astype(jnp.float32))
    _, h = lax.associative_scan(lru_combine, (a, bterm), axis=1)
    y = h.astype(x.dtype) * jax.nn.silu(gate)
    out = y @ w_out
    return x + rmsnorm(out, g_post), new_conv, h[:, -1]


def shared_kv(x, g_kv, w_kv):
    B, T, _ = x.shape
    k, v = jnp.split(rmsnorm(x, g_kv) @ w_kv, 2, axis=-1)
    return (k.reshape(B, T, N_KV_HEADS, HEAD_DIM), v.reshape(B, T, N_KV_HEADS, HEAD_DIM))


def sink_attend(q, k, v, dist, valid, sinks, table):
    s = jnp.einsum('...qhgd,...khd->...hgqk', q, k).astype(jnp.float32) * (1.0 / math.sqrt(HEAD_DIM))
    s = s + rel_bias(dist, table)
    s = jnp.where(valid, s, NEG_INF)
    sk = sinks.astype(jnp.float32).reshape(N_KV_HEADS, GROUP, 1, 1)
    m = jnp.maximum(jnp.max(s, axis=-1, keepdims=True), sk)
    p = jnp.exp(s - m)
    p = p / (jnp.sum(p, axis=-1, keepdims=True) + jnp.exp(sk - m))
    o = jnp.einsum('...hgqk,...khd->...qhgd', p, v.astype(jnp.float32))
    return o.astype(q.dtype)


def banded_attention(q, k, v, sinks, table):
    B, T = q.shape[0], q.shape[1]
    nblk = T // BLOCK
    qb = q.reshape(B, nblk, BLOCK, N_KV_HEADS, GROUP, HEAD_DIM)
    pad = jnp.zeros((B, BLOCK, N_KV_HEADS, HEAD_DIM), k.dtype)
    kp = jnp.concatenate([pad, k], axis=1).reshape(B, nblk + 1, BLOCK, N_KV_HEADS, HEAD_DIM)
    vp = jnp.concatenate([pad, v], axis=1).reshape(B, nblk + 1, BLOCK, N_KV_HEADS, HEAD_DIM)
    kb = jnp.concatenate([kp[:, :-1], kp[:, 1:]], axis=2)
    vb = jnp.concatenate([vp[:, :-1], vp[:, 1:]], axis=2)
    qi = jnp.arange(BLOCK)[:, None]
    kj = jnp.arange(2 * BLOCK)[None, :]
    dist = qi + BLOCK - kj
    blk = jnp.arange(nblk)[:, None, None]
    valid = (dist >= 0) & (dist < WINDOW) & ((blk * BLOCK - BLOCK + kj) >= 0)
    o = sink_attend(qb, kb, vb, dist, valid[:, None, None], sinks, table)
    return o.reshape(B, T, N_KV_HEADS, GROUP, HEAD_DIM)


def cached_attention(q, k_new, v_new, k_past, v_past, sinks, table):
    T = q.shape[1]
    w_rows = k_past.shape[1]
    k_all = jnp.concatenate([k_past.astype(k_new.dtype), k_new], axis=1)
    v_all = jnp.concatenate([v_past.astype(v_new.dtype), v_new], axis=1)
    kpos = jnp.concatenate([PAST_LEN - w_rows + jnp.arange(w_rows), PAST_LEN + jnp.arange(T)])
    qpos = PAST_LEN + jnp.arange(T)
    dist = qpos[:, None] - kpos[None, :]
    valid = (dist >= 0) & (dist < WINDOW)
    return sink_attend(q, k_all, v_all, dist, valid[None, None], sinks, table)


def swa_layer(x, k, v, k_past, v_past, g_pre, g_post, w_qg, sinks, w_out, table):
    B, T, _ = x.shape
    xn = rmsnorm(x, g_pre)
    q, gate = jnp.split(xn @ w_qg, 2, axis=-1)
    q = q.reshape(B, T, N_KV_HEADS, GROUP, HEAD_DIM)
    if k_past is None:
        o = banded_attention(q, k, v, sinks, table)
    else:
        o = cached_attention(q, k, v, k_past, v_past, sinks, table)
    y = o.reshape(B, T, ATT_WIDTH) * jax.nn.silu(gate)
    return x + rmsnorm(y @ w_out, g_post)


def trunk(x, conv_prev, h_prev, k_past, v_past,
          a_norm_pre, a_norm_post, a_w_in, a_conv_w, a_conv_b, a_w_r, a_b_r, a_w_i, a_b_i,
          a_lambda, a_w_out, kv_norm, w_kv, b_norm_pre, b_norm_post, b_w_qg, b_sinks, b_w_out,
          rel_bias_table):
    conv_out, h_out = [], []
    k = v = None
    for layer in range(DEPTH):
        if layer < N_A_LAYERS:
            i = layer
            x, c, h = rglru_layer(x, conv_prev[i], h_prev[i], a_norm_pre[i], a_norm_post[i],
                                  a_w_in[i], a_conv_w[i], a_conv_b[i], a_w_r[i], a_b_r[i],
                                  a_w_i[i], a_b_i[i], a_lambda[i], a_w_out[i])
            conv_out.append(c)
            h_out.append(h)
        else:
            j = layer - N_A_LAYERS
            if j == 0:
                k, v = shared_kv(x, kv_norm, w_kv)
            x = swa_layer(x, k, v, k_past, v_past, b_norm_pre[j], b_norm_post[j], b_w_qg[j],
                          b_sinks[j], b_w_out[j], rel_bias_table)
    return x, jnp.stack(conv_out), jnp.stack(h_out), k, v


def setup_inputs(seed: int = 0) -> dict:
    key = jax.random.key(seed)
    ks = jax.random.split(key, 32)

    def nrm(k, shape, scale):
        return jax.random.normal(k, shape, jnp.float32) * scale

    win_rows = min(WINDOW, PAST_LEN)
    u = jax.random.uniform(ks[10], (N_A_LAYERS, LRU_WIDTH), jnp.float32, 0.9, 0.999)
    s = u ** (1.0 / LRU_C)
    a_lambda = jnp.log(s) - jnp.log1p(-s)
    return {
        "x_prompt": nrm(ks[0], (BATCH, SEQ, D_MODEL), 1.0),
        "x_sample": nrm(ks[1], (DEC_BATCH, DEC_SEQ, D_MODEL), 1.0),
        "state_conv": nrm(ks[2], (N_A_LAYERS, DEC_BATCH, CONV_W - 1, LRU_WIDTH), 1.0),
        "state_h": nrm(ks[3], (N_A_LAYERS, DEC_BATCH, LRU_WIDTH), 0.5),
        "cache_k": nrm(ks[4], (DEC_BATCH, win_rows, N_KV_HEADS, HEAD_DIM), 1.0),
        "cache_v": nrm(ks[5], (DEC_BATCH, win_rows, N_KV_HEADS, HEAD_DIM), 1.0),
        "a_norm_pre": 1.0 + nrm(ks[6], (N_A_LAYERS, D_MODEL), 0.05),
        "a_norm_post": 1.0 + nrm(ks[7], (N_A_LAYERS, D_MODEL), 0.05),
        "a_w_in": nrm(ks[8], (N_A_LAYERS, D_MODEL, 2 * LRU_WIDTH), D_MODEL ** -0.5),
        "a_conv_w": nrm(ks[9], (N_A_LAYERS, CONV_W, LRU_WIDTH), CONV_W ** -0.5),
        "a_conv_b": nrm(ks[11], (N_A_LAYERS, LRU_WIDTH), 0.02),
        "a_w_r": nrm(ks[12], (N_A_LAYERS, LRU_BLOCKS, LRU_BLOCK_W, LRU_BLOCK_W), LRU_BLOCK_W ** -0.5),
        "a_b_r": nrm(ks[13], (N_A_LAYERS, LRU_WIDTH), 0.02),
        "a_w_i": nrm(ks[14], (N_A_LAYERS, LRU_BLOCKS, LRU_BLOCK_W, LRU_BLOCK_W), LRU_BLOCK_W ** -0.5),
        "a_b_i": nrm(ks[15], (N_A_LAYERS, LRU_WIDTH), 0.02),
        "a_lambda": a_lambda,
        "a_w_out": nrm(ks[16], (N_A_LAYERS, LRU_WIDTH, D_MODEL), LRU_WIDTH ** -0.5),
        "kv_norm": 1.0 + nrm(ks[17], (D_MODEL,), 0.05),
        "w_kv": nrm(ks[18], (D_MODEL, 2 * KV_WIDTH), D_MODEL ** -0.5),
        "b_norm_pre": 1.0 + nrm(ks[19], (N_B_LAYERS, D_MODEL), 0.05),
        "b_norm_post": 1.0 + nrm(ks[20], (N_B_LAYERS, D_MODEL), 0.05),
        "b_w_qg": nrm(ks[21], (N_B_LAYERS, D_MODEL, 2 * ATT_WIDTH), D_MODEL ** -0.5),
        "b_sinks": nrm(ks[22], (N_B_LAYERS, N_Q_HEADS), 1.0),
        "b_w_out": nrm(ks[23], (N_B_LAYERS, ATT_WIDTH, D_MODEL), ATT_WIDTH ** -0.5),
        "rel_bias_table": nrm(ks[24], (N_BUCKETS, N_Q_HEADS), 0.5),
    }


def reference(x_prompt, x_sample, state_conv, state_h, cache_k, cache_v,
              a_norm_pre, a_norm_post, a_w_in, a_conv_w, a_conv_b, a_w_r, a_b_r, a_w_i, a_b_i,
              a_lambda, a_w_out, kv_norm, w_kv, b_norm_pre, b_norm_post, b_w_qg, b_sinks, b_w_out,
              rel_bias_table):
    weights = (a_norm_pre, a_norm_post, a_w_in, a_conv_w, a_conv_b, a_w_r, a_b_r, a_w_i, a_b_i,
               a_lambda, a_w_out, kv_norm, w_kv, b_norm_pre, b_norm_post, b_w_qg, b_sinks, b_w_out,
               rel_bias_table)
    B = x_prompt.shape[0]
    conv0 = jnp.zeros((N_A_LAYERS, B, CONV_W - 1, LRU_WIDTH), x_prompt.dtype)
    h0 = jnp.zeros((N_A_LAYERS, B, LRU_WIDTH), jnp.float32)
    y_prompt, p_conv, p_h, p_k, p_v = trunk(x_prompt, conv0, h0, None, None, *weights)
    keep = min(WINDOW, x_prompt.shape[1])
    p_k = p_k[:, -keep:]
    p_v = p_v[:, -keep:]
    y_sample, s_conv, s_h, s_k, s_v = trunk(x_sample, state_conv, state_h, cache_k, cache_v, *weights)
    return (y_prompt, y_sample, p_conv, p_h, p_k, p_v, s_conv, s_h, s_k, s_v)
```

```python
import functools
import math

import jax
import jax.numpy as jnp
from jax import lax
from jax.experimental import pallas as pl
from jax.experimental.pallas import tpu as pltpu

F32 = jnp.float32
BF16 = jnp.bfloat16

D_MODEL = 2048
LRU_WIDTH = 2048
LRU_BLOCKS = 8
LRU_BLOCK_W = LRU_WIDTH // LRU_BLOCKS
CONV_W = 4
LRU_C = 8.0
HEAD_DIM = 64
N_Q_HEADS = 32
N_KV_HEADS = 8
GROUP = N_Q_HEADS // N_KV_HEADS
ATT_WIDTH = N_Q_HEADS * HEAD_DIM
KV_WIDTH = N_KV_HEADS * HEAD_DIM
WINDOW = 128
BLOCK = WINDOW
N_BUCKETS = 32
MAX_DISTANCE = 128
RMS_EPS = 1e-6
NEG_INF = -1e30
PAST_LEN = 16384

V7X_VMEM_BYTES = 64 * 1024 * 1024
VMEM_LIMIT = V7X_VMEM_BYTES - 8 * 1024 * 1024
SUBLANES = 8


def _params(*semantics):
    return pltpu.CompilerParams(dimension_semantics=semantics, vmem_limit_bytes=VMEM_LIMIT)


def _resident(shape):
    zeros = (0,) * len(shape)
    return pl.BlockSpec(shape, lambda *_: zeros, pipeline_mode=pl.Buffered(1))


def _rms_scale(x):
    return lax.rsqrt(jnp.mean(x * x, axis=-1, keepdims=True) + RMS_EPS)


def _silu(x):
    return x * jax.nn.sigmoid(x)


def _norm_proj_kernel(x_ref, g_ref, w_ref, o1_ref, o2_ref):
    x = x_ref[...]
    xn = (x * _rms_scale(x) * g_ref[...]).astype(BF16)
    r = jnp.dot(xn, w_ref[...], preferred_element_type=F32)
    n1 = o1_ref.shape[-1]
    o1_ref[...] = r[:, :n1].astype(o1_ref.dtype)
    o2_ref[...] = r[:, n1:].astype(o2_ref.dtype)


def norm_proj(x, g, w, tm):
    m, d = x.shape
    n = w.shape[1]
    n1 = n // 2
    return pl.pallas_call(
        _norm_proj_kernel,
        grid=(m // tm,),
        in_specs=[
            pl.BlockSpec((tm, d), lambda i: (i, 0)),
            _resident((1, d)),
            _resident((d, n)),
        ],
        out_specs=[
            pl.BlockSpec((tm, n1), lambda i: (i, 0)),
            pl.BlockSpec((tm, n - n1), lambda i: (i, 0)),
        ],
        out_shape=[
            jax.ShapeDtypeStruct((m, n1), F32),
            jax.ShapeDtypeStruct((m, n - n1), F32),
        ],
        compiler_params=_params("parallel"),
        name="norm_proj",
    )(x, g.reshape(1, d), w)


def _norm_proj_kvq_kernel(x_ref, gkv_ref, gq_ref, wkv_ref, wqg_ref, k_ref, v_ref, q_ref, gate_ref):
    x = x_ref[...]
    xh = x * _rms_scale(x)
    kv = jnp.dot((xh * gkv_ref[...]).astype(BF16), wkv_ref[...], preferred_element_type=F32)
    qg = jnp.dot((xh * gq_ref[...]).astype(BF16), wqg_ref[...], preferred_element_type=F32)
    k_ref[...] = kv[:, :KV_WIDTH]
    v_ref[...] = kv[:, KV_WIDTH:]
    q_ref[...] = (qg[:, :ATT_WIDTH] * (1.0 / math.sqrt(HEAD_DIM))).astype(q_ref.dtype)
    gate_ref[...] = qg[:, ATT_WIDTH:]


def norm_proj_kvq(x, g_kv, g_q, w_kv, w_qg, tm, q_dtype):
    m, d = x.shape
    row = lambda n: pl.BlockSpec((tm, n), lambda i: (i, 0))
    return pl.pallas_call(
        _norm_proj_kvq_kernel,
        grid=(m // tm,),
        in_specs=[
            row(d),
            _resident((1, d)),
            _resident((1, d)),
            _resident(w_kv.shape),
            _resident(w_qg.shape),
        ],
        out_specs=[row(KV_WIDTH), row(KV_WIDTH), row(ATT_WIDTH), row(ATT_WIDTH)],
        out_shape=[
            jax.ShapeDtypeStruct((m, KV_WIDTH), F32),
            jax.ShapeDtypeStruct((m, KV_WIDTH), F32),
            jax.ShapeDtypeStruct((m, ATT_WIDTH), q_dtype),
            jax.ShapeDtypeStruct((m, ATT_WIDTH), F32),
        ],
        compiler_params=_params("parallel"),
        name="norm_proj_kvq",
    )(x, g_kv.reshape(1, d), g_q.reshape(1, d), w_kv, w_qg)


def _proj_norm_res_kernel(gated, *refs):
    if gated:
        a_ref, gate_ref, w_ref, g_ref, x_ref, o_ref = refs
        y = (a_ref[...] * _silu(gate_ref[...])).astype(BF16)
    else:
        a_ref, w_ref, g_ref, x_ref, o_ref = refs
        y = a_ref[...]
    o = jnp.dot(y, w_ref[...], preferred_element_type=F32)
    o_ref[...] = x_ref[...] + o * _rms_scale(o) * g_ref[...]


def proj_norm_res(a, gate, w, g, x, tm):
    m, k = a.shape
    d = w.shape[1]
    row = lambda n: pl.BlockSpec((tm, n), lambda i: (i, 0))
    gated = gate is not None
    ins = [a] + ([gate] if gated else []) + [w, g.reshape(1, d), x]
    in_specs = [row(k)] + ([row(k)] if gated else []) + [_resident(w.shape), _resident((1, d)), row(d)]
    return pl.pallas_call(
        functools.partial(_proj_norm_res_kernel, gated),
        grid=(m // tm,),
        in_specs=in_specs,
        out_specs=row(d),
        out_shape=jax.ShapeDtypeStruct((m, d), F32),
        compiler_params=_params("parallel"),
        name="proj_norm_res",
    )(*ins)


def _lru_gates(conv, wr, br, wi, bi, lam):
    cb = conv.astype(BF16)
    r = jax.nn.sigmoid(jnp.dot(cb, wr, preferred_element_type=F32) + br)
    i = jax.nn.sigmoid(jnp.dot(cb, wi, preferred_element_type=F32) + bi)
    nl = -lam
    softplus = jnp.maximum(nl, 0.0) + jnp.log1p(jnp.exp(-jnp.abs(nl)))
    log_a = -LRU_C * r * softplus
    a = jnp.exp(log_a)
    mult = jnp.sqrt(-jnp.tanh(log_a) * (a * a + 1.0))
    return a, mult * (i * conv)


def _rglru_scan_kernel(u_ref, gate_ref, cprev_ref, h0_ref, cw_ref, cb_ref, wr_ref, br_ref,
                       wi_ref, bi_ref, lam_ref, y_ref, cnew_ref, hlast_ref, h_scr, tail_scr):
    t = pl.program_id(2)
    tc = u_ref.shape[1]
    ntaps = CONV_W - 1

    @pl.when(t == 0)
    def _():
        h_scr[...] = h0_ref[0]
        tail_scr[...] = jnp.zeros_like(tail_scr)
        tail_scr[SUBLANES - ntaps:, :] = cprev_ref[0]

    u = u_ref[0]
    ext = jnp.concatenate([tail_scr[...], u], axis=0)
    cw = cw_ref[...]
    conv = cb_ref[...]
    for tap in range(CONV_W):
        off = SUBLANES - ntaps + tap
        conv = conv + ext[off:off + tc] * cw[tap:tap + 1]

    a, b = _lru_gates(conv, wr_ref[0], br_ref[...], wi_ref[0], bi_ref[...], lam_ref[...])

    sub = lax.broadcasted_iota(jnp.int32, a.shape, 0) % SUBLANES
    step = 1
    while step < SUBLANES:
        keep = sub >= step
        b = jnp.where(keep, a * pltpu.roll(b, step, axis=0) + b, b)
        a = jnp.where(keep, a * pltpu.roll(a, step, axis=0), a)
        step *= 2

    h = h_scr[...]
    rows = []
    for gi in range(tc // SUBLANES):
        sl = slice(gi * SUBLANES, (gi + 1) * SUBLANES)
        hg = a[sl] * h + b[sl]
        rows.append(hg)
        h = hg[SUBLANES - 1:]
    hs = jnp.concatenate(rows, axis=0)
    h_scr[...] = h
    tail_scr[...] = u[tc - SUBLANES:]

    y_ref[0] = (hs * _silu(gate_ref[0])).astype(y_ref.dtype)

    @pl.when(t == pl.num_programs(2) - 1)
    def _():
        hlast_ref[0] = h
        cnew_ref[0] = u[tc - ntaps:]


def rglru_scan(u, gate, conv_prev, h0, conv_w, conv_b, w_r, b_r, w_i, b_i, lam, tc):
    bsz, t, w = u.shape
    assert t % tc == 0 and tc % SUBLANES == 0 and tc >= SUBLANES
    bw = LRU_BLOCK_W
    seq = pl.BlockSpec((1, tc, bw), lambda b, n, s: (b, s, n))
    chan = lambda rows: pl.BlockSpec((rows, bw), lambda b, n, s: (0, n))
    blockw = pl.BlockSpec((1, bw, bw), lambda b, n, s: (n, 0, 0))
    state = lambda rows: pl.BlockSpec((1, rows, bw), lambda b, n, s: (b, 0, n))
    return pl.pallas_call(
        _rglru_scan_kernel,
        grid=(bsz, LRU_BLOCKS, t // tc),
        in_specs=[seq, seq, state(CONV_W - 1), state(1), chan(CONV_W), chan(1),
                  blockw, chan(1), blockw, chan(1), chan(1)],
        out_specs=[seq, state(CONV_W - 1), state(1)],
        out_shape=[
            jax.ShapeDtypeStruct((bsz, t, w), BF16),
            jax.ShapeDtypeStruct((bsz, CONV_W - 1, w), F32),
            jax.ShapeDtypeStruct((bsz, 1, w), F32),
        ],
        scratch_shapes=[pltpu.VMEM((1, bw), F32), pltpu.VMEM((SUBLANES, bw), F32)],
        compiler_params=_params("parallel", "parallel", "arbitrary"),
        name="rglru_scan",
    )(u, gate, conv_prev, h0.reshape(bsz, 1, w), conv_w, conv_b.reshape(1, w),
      w_r, b_r.reshape(1, w), w_i, b_i.reshape(1, w), lam.reshape(1, w))


def _rglru_step_kernel(u_ref, cprev_ref, h0_ref, cw_ref, cb_ref, wr_ref, br_ref, wi_ref, bi_ref,
                       lam_ref, h_ref, cnew_ref):
    u = u_ref[...]
    cw = cw_ref[...]
    conv = cb_ref[...]
    for tap in range(CONV_W - 1):
        conv = conv + cprev_ref[tap] * cw[tap:tap + 1]
        if tap > 0:
            cnew_ref[tap - 1] = cprev_ref[tap]
    conv = conv + u * cw[CONV_W - 1:]
    cnew_ref[CONV_W - 2] = u
    a, b = _lru_gates(conv, wr_ref[0], br_ref[...], wi_ref[0], bi_ref[...], lam_ref[...])
    h_ref[...] = a * h0_ref[...] + b


def rglru_step(u, conv_prev_t, h0, conv_w, conv_b, w_r, b_r, w_i, b_i, lam):
    bsz, w = u.shape
    bw = LRU_BLOCK_W
    rows = pl.BlockSpec((bsz, bw), lambda n: (0, n))
    taps = pl.BlockSpec((CONV_W - 1, bsz, bw), lambda n: (0, 0, n))
    chan = lambda r: pl.BlockSpec((r, bw), lambda n: (0, n))
    blockw = pl.BlockSpec((1, bw, bw), lambda n: (n, 0, 0))
    return pl.pallas_call(
        _rglru_step_kernel,
        grid=(LRU_BLOCKS,),
        in_specs=[rows, taps, rows, chan(CONV_W), chan(1), blockw, chan(1), blockw, chan(1), chan(1)],
        out_specs=[rows, taps],
        out_shape=[
            jax.ShapeDtypeStruct((bsz, w), F32),
            jax.ShapeDtypeStruct((CONV_W - 1, bsz, w), F32),
        ],
        compiler_params=_params("parallel"),
        name="rglru_step",
    )(u, conv_prev_t, h0, conv_w, conv_b.reshape(1, w), w_r, b_r.reshape(1, w),
      w_i, b_i.reshape(1, w), lam.reshape(1, w))


def _bucket_bias(dist, valid, table_ref, head):
    n = jnp.maximum(dist, 0)
    max_exact = N_BUCKETS // 2
    nf = jnp.maximum(n, 1).astype(F32)
    large = max_exact + (jnp.log(nf / max_exact) / math.log(MAX_DISTANCE / max_exact)
                         * (N_BUCKETS - max_exact)).astype(jnp.int32)
    large = jnp.minimum(large, N_BUCKETS - 1)
    bucket = jnp.where(n < max_exact, n, large)
    bias = jnp.zeros(dist.shape, F32)
    for b in range(N_BUCKETS):
        bias = jnp.where(bucket == b, table_ref[b, head], bias)
    return jnp.where(valid, bias, NEG_INF)


def _bias_kernel(table_ref, prev_ref, cur_ref, past_ref):
    head = pl.program_id(0)
    qi = lax.broadcasted_iota(jnp.int32, (BLOCK, BLOCK), 0)
    kj = lax.broadcasted_iota(jnp.int32, (BLOCK, BLOCK), 1)
    d_prev = qi + BLOCK - kj
    prev_ref[0] = _bucket_bias(d_prev, (d_prev >= 0) & (d_prev < WINDOW), table_ref, head)
    d_cur = qi - kj
    cur_ref[0] = _bucket_bias(d_cur, (d_cur >= 0) & (d_cur < WINDOW), table_ref, head)
    rows = past_ref.shape[2]
    d_past = rows - lax.broadcasted_iota(jnp.int32, (1, rows), 1)
    past_ref[0] = _bucket_bias(d_past, (d_past >= 0) & (d_past < WINDOW), table_ref, head)


def bias_tables(table, past_rows):
    return pl.pallas_call(
        _bias_kernel,
        grid=(N_Q_HEADS,),
        in_specs=[pl.BlockSpec(memory_space=pltpu.SMEM)],
        out_specs=[
            pl.BlockSpec((1, BLOCK, BLOCK), lambda h: (h, 0, 0)),
            pl.BlockSpec((1, BLOCK, BLOCK), lambda h: (h, 0, 0)),
            pl.BlockSpec((1, 1, past_rows), lambda h: (h, 0, 0)),
        ],
        out_shape=[
            jax.ShapeDtypeStruct((N_Q_HEADS, BLOCK, BLOCK), F32),
            jax.ShapeDtypeStruct((N_Q_HEADS, BLOCK, BLOCK), F32),
            jax.ShapeDtypeStruct((N_Q_HEADS, 1, past_rows), F32),
        ],
        compiler_params=_params("parallel"),
        name="bias_tables",
    )(table)


def _band_attn_kernel(sinks_ref, q_ref, kp_ref, kc_ref, vp_ref, vc_ref, gate_ref, bp_ref, bc_ref, y_ref):
    first = pl.program_id(1) == 0
    no_prev = jnp.where(first, NEG_INF, 0.0).astype(F32)
    nt = (((1,), (1,)), ((), ()))
    for hk in range(N_KV_HEADS):
        ks = slice(hk * HEAD_DIM, (hk + 1) * HEAD_DIM)
        kp = kp_ref[:, ks].astype(BF16)
        kc = kc_ref[:, ks].astype(BF16)
        vp = vp_ref[:, ks].astype(BF16)
        vc = vc_ref[:, ks].astype(BF16)
        for g in range(GROUP):
            h = hk * GROUP + g
            hs = slice(h * HEAD_DIM, (h + 1) * HEAD_DIM)
            q = q_ref[:, hs]
            sp = lax.dot_general(q, kp, nt, preferred_element_type=F32) + bp_ref[h] + no_prev
            sc = lax.dot_general(q, kc, nt, preferred_element_type=F32) + bc_ref[h]
            sink = sinks_ref[h]
            m = jnp.maximum(jnp.maximum(jnp.max(sp, axis=-1, keepdims=True),
                                        jnp.max(sc, axis=-1, keepdims=True)), sink)
            pp = jnp.exp(sp - m)
            pc = jnp.exp(sc - m)
            denom = (jnp.sum(pp, axis=-1, keepdims=True) + jnp.sum(pc, axis=-1, keepdims=True)
                     + jnp.exp(sink - m))
            o = (jnp.dot(pp.astype(BF16), vp, preferred_element_type=F32)
                 + jnp.dot(pc.astype(BF16), vc, preferred_element_type=F32)) / denom
            y_ref[:, hs] = (o * _silu(gate_ref[:, hs])).astype(y_ref.dtype)


def band_attention(q, k, v, gate, sinks, bias_prev, bias_cur, bsz, t):
    m = q.shape[0]
    nblk = t // BLOCK
    cur = lambda n: pl.BlockSpec((BLOCK, n), lambda b, i: (b * nblk + i, 0))
    prev = lambda n: pl.BlockSpec((BLOCK, n), lambda b, i: (b * nblk + jnp.maximum(i - 1, 0), 0))
    return pl.pallas_call(
        _band_attn_kernel,
        grid=(bsz, nblk),
        in_specs=[
            pl.BlockSpec(memory_space=pltpu.SMEM),
            cur(ATT_WIDTH), prev(KV_WIDTH), cur(KV_WIDTH), prev(KV_WIDTH), cur(KV_WIDTH),
            cur(ATT_WIDTH), _resident(bias_prev.shape), _resident(bias_cur.shape),
        ],
        out_specs=cur(ATT_WIDTH),
        out_shape=jax.ShapeDtypeStruct((m, ATT_WIDTH), BF16),
        compiler_params=_params("parallel", "parallel"),
        name="band_attention",
    )(sinks, q, k, k, v, v, gate, bias_prev, bias_cur)


def _cached_attn_kernel(q_ref, ck_ref, cv_ref, kn_ref, vn_ref, sinks_ref, bpast_ref, bnew_ref, o_ref):
    q = q_ref[0]
    qt = jnp.concatenate([q] * N_KV_HEADS, axis=1)
    lane_kv = lax.broadcasted_iota(jnp.int32, qt.shape, 1) // HEAD_DIM
    row_kv = lax.broadcasted_iota(jnp.int32, qt.shape, 0) // GROUP
    own = lane_kv == row_kv
    qm = jnp.where(own, qt, 0.0).astype(BF16)
    kpast = ck_ref[0].astype(BF16)
    vpast = cv_ref[0].astype(BF16)
    knew = kn_ref[0].astype(BF16).astype(F32)
    vnew = vn_ref[0].astype(BF16).astype(F32)
    nt = (((1,), (1,)), ((), ()))
    s = lax.dot_general(qm, kpast, nt, preferred_element_type=F32) + bpast_ref[...]
    s_new = jnp.sum(qm.astype(F32) * knew, axis=-1, keepdims=True) + bnew_ref[...]
    sink = sinks_ref[...]
    m = jnp.maximum(jnp.maximum(jnp.max(s, axis=-1, keepdims=True), s_new), sink)
    p = jnp.exp(s - m)
    p_new = jnp.exp(s_new - m)
    denom = jnp.sum(p, axis=-1, keepdims=True) + p_new + jnp.exp(sink - m)
    o_all = (jnp.dot(p.astype(BF16), vpast, preferred_element_type=F32)
             + p_new.astype(BF16).astype(F32) * vnew)
    o_all = jnp.where(own, o_all, 0.0)
    o = o_all[:, :HEAD_DIM]
    for hk in range(1, N_KV_HEADS):
        o = o + o_all[:, hk * HEAD_DIM:(hk + 1) * HEAD_DIM]
    o_ref[0] = o / denom


def cached_attention(q, cache_k, cache_v, k_new, v_new, sinks, bias_past, bias_new):
    bsz, rows = cache_k.shape[0], cache_k.shape[1]
    per_seq = lambda r, n: pl.BlockSpec((1, r, n), lambda b: (b, 0, 0))
    return pl.pallas_call(
        _cached_attn_kernel,
        grid=(bsz,),
        in_specs=[
            per_seq(N_Q_HEADS, HEAD_DIM), per_seq(rows, KV_WIDTH), per_seq(rows, KV_WIDTH),
            per_seq(1, KV_WIDTH), per_seq(1, KV_WIDTH),
            _resident((N_Q_HEADS, 1)), _resident((N_Q_HEADS, rows)), _resident((N_Q_HEADS, 1)),
        ],
        out_specs=per_seq(N_Q_HEADS, HEAD_DIM),
        out_shape=jax.ShapeDtypeStruct((bsz, N_Q_HEADS, HEAD_DIM), F32),
        compiler_params=_params("parallel"),
        name="cached_attention",
    )(q, cache_k, cache_v, k_new, v_new, sinks, bias_past, bias_new)


def kernel(x_prompt, x_sample, state_conv, state_h, cache_k, cache_v, a_norm_pre, a_norm_post,
           a_w_in, a_conv_w, a_conv_b, a_w_r, a_b_r, a_w_i, a_b_i, a_lambda, a_w_out, kv_norm, w_kv,
           b_norm_pre, b_norm_post, b_w_qg, b_sinks, b_w_out, rel_bias_table):
    bsz, t, d = x_prompt.shape
    dbsz, dt, _ = x_sample.shape
    assert a_w_in.shape[0] == 1 and b_w_qg.shape[0] == 1 and dt == 1
    assert t % BLOCK == 0 and t >= WINDOW
    past_rows = cache_k.shape[1]
    assert past_rows == min(WINDOW, PAST_LEN)

    w_in = a_w_in[0].astype(BF16)
    w_r = a_w_r[0].astype(BF16)
    w_i = a_w_i[0].astype(BF16)
    w_aout = a_w_out[0].astype(BF16)
    w_kvb = w_kv.astype(BF16)
    w_qg = b_w_qg[0].astype(BF16)
    w_bout = b_w_out[0].astype(BF16)
    sinks = b_sinks[0]

    bias_prev, bias_cur, bias_past = bias_tables(rel_bias_table, past_rows)
    lru = (a_conv_w[0], a_conv_b[0], w_r, a_b_r[0], w_i, a_b_i[0], a_lambda[0])

    tm = 256
    xp = x_prompt.reshape(bsz * t, d)
    u, gate = norm_proj(xp, a_norm_pre[0], w_in, tm)
    conv0 = jnp.zeros((bsz, CONV_W - 1, LRU_WIDTH), F32)
    h0 = jnp.zeros((bsz, LRU_WIDTH), F32)
    y, p_conv, p_h = rglru_scan(u.reshape(bsz, t, LRU_WIDTH), gate.reshape(bsz, t, LRU_WIDTH),
                                conv0, h0, *lru, tc=256)
    x1 = proj_norm_res(y.reshape(bsz * t, LRU_WIDTH), None, w_aout, a_norm_post[0], xp, tm)
    k, v, q, gate_b = norm_proj_kvq(x1, kv_norm, b_norm_pre[0], w_kvb, w_qg, tm, BF16)
    yb = band_attention(q, k, v, gate_b, sinks, bias_prev, bias_cur, bsz, t)
    y_prompt = proj_norm_res(yb, None, w_bout, b_norm_post[0], x1, tm).reshape(bsz, t, d)
    keep = min(WINDOW, t)
    p_k = k.reshape(bsz, t, N_KV_HEADS, HEAD_DIM)[:, -keep:]
    p_v = v.reshape(bsz, t, N_KV_HEADS, HEAD_DIM)[:, -keep:]

    xs = x_sample.reshape(dbsz, d)
    us, gate_s = norm_proj(xs, a_norm_pre[0], w_in, dbsz)
    hs, s_conv_t = rglru_step(us, jnp.transpose(state_conv[0], (1, 0, 2)), state_h[0], *lru)
    xs1 = proj_norm_res(hs, gate_s, w_aout, a_norm_post[0], xs, dbsz)
    ks, vs, qs, gate_sb = norm_proj_kvq(xs1, kv_norm, b_norm_pre[0], w_kvb, w_qg, dbsz, F32)
    bias_new = rel_bias_table[0].reshape(N_Q_HEADS, 1)
    os_ = cached_attention(qs.reshape(dbsz, N_Q_HEADS, HEAD_DIM), cache_k.reshape(dbsz, past_rows, KV_WIDTH),
                           cache_v.reshape(dbsz, past_rows, KV_WIDTH), ks.reshape(dbsz, 1, KV_WIDTH),
                           vs.reshape(dbsz, 1, KV_WIDTH), sinks.reshape(N_Q_HEADS, 1),
                           bias_past.reshape(N_Q_HEADS, past_rows), bias_new)
    y_sample = proj_norm_res(os_.reshape(dbsz, ATT_WIDTH), gate_sb, w_bout, b_norm_post[0], xs1, dbsz)

    return (y_prompt, y_sample.reshape(dbsz, 1, d),
            p_conv[None], p_h.reshape(1, bsz, LRU_WIDTH), p_k, p_v,
            jnp.transpose(s_conv_t, (1, 0, 2))[None], hs[None],
            ks.reshape(dbsz, 1, N_KV_HEADS, HEAD_DIM), vs.reshape(dbsz, 1, N_KV_HEADS, HEAD_DIM))
```

```python
import functools
import math

import jax
import jax.numpy as jnp
from jax import lax
from jax.experimental import pallas as pl
from jax.experimental.pallas import tpu as pltpu

F32 = jnp.float32
BF16 = jnp.bfloat16

D_MODEL = 2048
LRU_WIDTH = 2048
LRU_BLOCKS = 8
LRU_BLOCK_W = LRU_WIDTH // LRU_BLOCKS
CONV_W = 4
LRU_C = 8.0
HEAD_DIM = 64
N_Q_HEADS = 32
N_KV_HEADS = 8
GROUP = N_Q_HEADS // N_KV_HEADS
ATT_WIDTH = N_Q_HEADS * HEAD_DIM
KV_WIDTH = N_KV_HEADS * HEAD_DIM
WINDOW = 128
BLOCK = WINDOW
N_BUCKETS = 32
MAX_DISTANCE = 128
RMS_EPS = 1e-6
NEG_INF = -1e30
PAST_LEN = 16384

V7X_VMEM_BYTES = 64 * 1024 * 1024
VMEM_LIMIT = V7X_VMEM_BYTES - 8 * 1024 * 1024
SUBLANES = 8
LANES = 128
HEADS_PER_TILE = LANES // HEAD_DIM
SLABS = GROUP // HEADS_PER_TILE


def _params(*semantics):
    return pltpu.CompilerParams(dimension_semantics=semantics, vmem_limit_bytes=VMEM_LIMIT)


def _resident(shape):
    zeros = (0,) * len(shape)
    return pl.BlockSpec(shape, lambda *_: zeros, pipeline_mode=pl.Buffered(1))


def _rms_scale(x):
    return lax.rsqrt(jnp.mean(x * x, axis=-1, keepdims=True) + RMS_EPS)


def _silu(x):
    return x * jax.nn.sigmoid(x)


def _norm_proj_kernel(x_ref, g_ref, w_ref, o1_ref, o2_ref):
    x = x_ref[...]
    xn = (x * _rms_scale(x) * g_ref[...]).astype(BF16)
    r = jnp.dot(xn, w_ref[...], preferred_element_type=F32)
    n1 = o1_ref.shape[-1]
    o1_ref[...] = r[:, :n1].astype(o1_ref.dtype)
    o2_ref[...] = r[:, n1:].astype(o2_ref.dtype)


def norm_proj(x, g, w, tm):
    m, d = x.shape
    n = w.shape[1]
    n1 = n // 2
    return pl.pallas_call(
        _norm_proj_kernel,
        grid=(m // tm,),
        in_specs=[
            pl.BlockSpec((tm, d), lambda i: (i, 0)),
            _resident((1, d)),
            _resident((d, n)),
        ],
        out_specs=[
            pl.BlockSpec((tm, n1), lambda i: (i, 0)),
            pl.BlockSpec((tm, n - n1), lambda i: (i, 0)),
        ],
        out_shape=[
            jax.ShapeDtypeStruct((m, n1), F32),
            jax.ShapeDtypeStruct((m, n - n1), F32),
        ],
        compiler_params=_params("parallel"),
        name="norm_proj",
    )(x, g.reshape(1, d), w)


def _dup_heads(x):
    low = lax.broadcasted_iota(jnp.int32, (x.shape[0], LANES), 1) < HEAD_DIM
    out = []
    for c in range(x.shape[1] // LANES):
        col = x[:, c * LANES:(c + 1) * LANES]
        swapped = pltpu.roll(col, HEAD_DIM, axis=1)
        out += [jnp.where(low, col, swapped), jnp.where(low, swapped, col)]
    return jnp.concatenate(out, axis=1)


def _norm_proj_kvq_kernel(banded, x_ref, gkv_ref, gq_ref, wkv_ref, wqg_ref, k_ref, v_ref, q_ref, gate_ref,
                          *dup_refs):
    x = x_ref[...]
    xh = x * _rms_scale(x)
    kv = jnp.dot((xh * gkv_ref[...]).astype(BF16), wkv_ref[...], preferred_element_type=F32)
    qg = jnp.dot((xh * gq_ref[...]).astype(BF16), wqg_ref[...], preferred_element_type=F32)
    k = kv[:, :KV_WIDTH]
    v = kv[:, KV_WIDTH:]
    k_ref[...] = k
    v_ref[...] = v
    q_ref[...] = (qg[:, :ATT_WIDTH] * (1.0 / math.sqrt(HEAD_DIM))).astype(q_ref.dtype)
    gate_ref[...] = qg[:, ATT_WIDTH:]
    if banded:
        kdup_ref, vt_ref = dup_refs
        kdup_ref[...] = _dup_heads(k).astype(BF16)
        vt_ref[...] = _dup_heads(v).T.astype(BF16)


def norm_proj_kvq(x, g_kv, g_q, w_kv, w_qg, tm, banded):
    m, d = x.shape
    row = lambda n: pl.BlockSpec((tm, n), lambda i: (i, 0))
    out_specs = [row(KV_WIDTH), row(KV_WIDTH), row(ATT_WIDTH), row(ATT_WIDTH)]
    out_shape = [
        jax.ShapeDtypeStruct((m, KV_WIDTH), F32),
        jax.ShapeDtypeStruct((m, KV_WIDTH), F32),
        jax.ShapeDtypeStruct((m, ATT_WIDTH), BF16 if banded else F32),
        jax.ShapeDtypeStruct((m, ATT_WIDTH), F32),
    ]
    if banded:
        out_specs += [row(2 * KV_WIDTH), pl.BlockSpec((2 * KV_WIDTH, tm), lambda i: (0, i))]
        out_shape += [jax.ShapeDtypeStruct((m, 2 * KV_WIDTH), BF16),
                      jax.ShapeDtypeStruct((2 * KV_WIDTH, m), BF16)]
    return pl.pallas_call(
        functools.partial(_norm_proj_kvq_kernel, banded),
        grid=(m // tm,),
        in_specs=[
            row(d),
            _resident((1, d)),
            _resident((1, d)),
            _resident(w_kv.shape),
            _resident(w_qg.shape),
        ],
        out_specs=out_specs,
        out_shape=out_shape,
        compiler_params=_params("parallel"),
        name="norm_proj_kvq",
    )(x, g_kv.reshape(1, d), g_q.reshape(1, d), w_kv, w_qg)


def _proj_norm_res_kernel(gated, *refs):
    if gated:
        a_ref, gate_ref, w_ref, g_ref, x_ref, o_ref = refs
        y = (a_ref[...] * _silu(gate_ref[...])).astype(BF16)
    else:
        a_ref, w_ref, g_ref, x_ref, o_ref = refs
        y = a_ref[...]
    o = jnp.dot(y, w_ref[...], preferred_element_type=F32)
    o_ref[...] = x_ref[...] + o * _rms_scale(o) * g_ref[...]


def proj_norm_res(a, gate, w, g, x, tm):
    m, k = a.shape
    d = w.shape[1]
    row = lambda n: pl.BlockSpec((tm, n), lambda i: (i, 0))
    gated = gate is not None
    ins = [a] + ([gate] if gated else []) + [w, g.reshape(1, d), x]
    in_specs = [row(k)] + ([row(k)] if gated else []) + [_resident(w.shape), _resident((1, d)), row(d)]
    return pl.pallas_call(
        functools.partial(_proj_norm_res_kernel, gated),
        grid=(m // tm,),
        in_specs=in_specs,
        out_specs=row(d),
        out_shape=jax.ShapeDtypeStruct((m, d), F32),
        compiler_params=_params("parallel"),
        name="proj_norm_res",
    )(*ins)


def _lru_gates(conv, wr, br, wi, bi, lam):
    cb = conv.astype(BF16)
    r = jax.nn.sigmoid(jnp.dot(cb, wr, preferred_element_type=F32) + br)
    i = jax.nn.sigmoid(jnp.dot(cb, wi, preferred_element_type=F32) + bi)
    nl = -lam
    softplus = jnp.maximum(nl, 0.0) + jnp.log1p(jnp.exp(-jnp.abs(nl)))
    log_a = -LRU_C * r * softplus
    a = jnp.exp(log_a)
    mult = jnp.sqrt(-jnp.tanh(log_a) * (a * a + 1.0))
    return a, mult * (i * conv)


def _rglru_scan_kernel(u_ref, gate_ref, cprev_ref, h0_ref, cw_ref, cb_ref, wr_ref, br_ref,
                       wi_ref, bi_ref, lam_ref, y_ref, cnew_ref, hlast_ref, h_scr, tail_scr):
    t = pl.program_id(2)
    tc = u_ref.shape[1]
    ntaps = CONV_W - 1

    @pl.when(t == 0)
    def _():
        h_scr[...] = h0_ref[0]
        tail_scr[...] = jnp.zeros_like(tail_scr)
        tail_scr[SUBLANES - ntaps:, :] = cprev_ref[0]

    u = u_ref[0]
    ext = jnp.concatenate([tail_scr[...], u], axis=0)
    cw = cw_ref[...]
    conv = cb_ref[...]
    for tap in range(CONV_W):
        off = SUBLANES - ntaps + tap
        conv = conv + ext[off:off + tc] * cw[tap:tap + 1]

    a, b = _lru_gates(conv, wr_ref[0], br_ref[...], wi_ref[0], bi_ref[...], lam_ref[...])

    sub = lax.broadcasted_iota(jnp.int32, a.shape, 0) % SUBLANES
    step = 1
    while step < SUBLANES:
        keep = sub >= step
        b = jnp.where(keep, a * pltpu.roll(b, step, axis=0) + b, b)
        a = jnp.where(keep, a * pltpu.roll(a, step, axis=0), a)
        step *= 2

    h = h_scr[...]
    rows = []
    for gi in range(tc // SUBLANES):
        sl = slice(gi * SUBLANES, (gi + 1) * SUBLANES)
        hg = a[sl] * h + b[sl]
        rows.append(hg)
        h = hg[SUBLANES - 1:]
    hs = jnp.concatenate(rows, axis=0)
    h_scr[...] = h
    tail_scr[...] = u[tc - SUBLANES:]

    y_ref[0] = (hs * _silu(gate_ref[0])).astype(y_ref.dtype)

    @pl.when(t == pl.num_programs(2) - 1)
    def _():
        hlast_ref[0] = h
        cnew_ref[0] = u[tc - ntaps:]


def rglru_scan(u, gate, conv_prev, h0, conv_w, conv_b, w_r, b_r, w_i, b_i, lam, tc):
    bsz, t, w = u.shape
    assert t % tc == 0 and tc % SUBLANES == 0 and tc >= SUBLANES
    bw = LRU_BLOCK_W
    seq = pl.BlockSpec((1, tc, bw), lambda b, n, s: (b, s, n))
    chan = lambda rows: pl.BlockSpec((rows, bw), lambda b, n, s: (0, n))
    blockw = pl.BlockSpec((1, bw, bw), lambda b, n, s: (n, 0, 0))
    state = lambda rows: pl.BlockSpec((1, rows, bw), lambda b, n, s: (b, 0, n))
    return pl.pallas_call(
        _rglru_scan_kernel,
        grid=(bsz, LRU_BLOCKS, t // tc),
        in_specs=[seq, seq, state(CONV_W - 1), state(1), chan(CONV_W), chan(1),
                  blockw, chan(1), blockw, chan(1), chan(1)],
        out_specs=[seq, state(CONV_W - 1), state(1)],
        out_shape=[
            jax.ShapeDtypeStruct((bsz, t, w), BF16),
            jax.ShapeDtypeStruct((bsz, CONV_W - 1, w), F32),
            jax.ShapeDtypeStruct((bsz, 1, w), F32),
        ],
        scratch_shapes=[pltpu.VMEM((1, bw), F32), pltpu.VMEM((SUBLANES, bw), F32)],
        compiler_params=_params("parallel", "parallel", "arbitrary"),
        name="rglru_scan",
    )(u, gate, conv_prev, h0.reshape(bsz, 1, w), conv_w, conv_b.reshape(1, w),
      w_r, b_r.reshape(1, w), w_i, b_i.reshape(1, w), lam.reshape(1, w))


def _rglru_step_kernel(u_ref, cprev_ref, h0_ref, cw_ref, cb_ref, wr_ref, br_ref, wi_ref, bi_ref,
                       lam_ref, h_ref, cnew_ref):
    u = u_ref[...]
    cw = cw_ref[...]
    conv = cb_ref[...]
    for tap in range(CONV_W - 1):
        conv = conv + cprev_ref[tap] * cw[tap:tap + 1]
        if tap > 0:
            cnew_ref[tap - 1] = cprev_ref[tap]
    conv = conv + u * cw[CONV_W - 1:]
    cnew_ref[CONV_W - 2] = u
    a, b = _lru_gates(conv, wr_ref[0], br_ref[...], wi_ref[0], bi_ref[...], lam_ref[...])
    h_ref[...] = a * h0_ref[...] + b


def rglru_step(u, conv_prev_t, h0, conv_w, conv_b, w_r, b_r, w_i, b_i, lam):
    bsz, w = u.shape
    bw = LRU_BLOCK_W
    rows = pl.BlockSpec((bsz, bw), lambda n: (0, n))
    taps = pl.BlockSpec((CONV_W - 1, bsz, bw), lambda n: (0, 0, n))
    chan = lambda r: pl.BlockSpec((r, bw), lambda n: (0, n))
    blockw = pl.BlockSpec((1, bw, bw), lambda n: (n, 0, 0))
    return pl.pallas_call(
        _rglru_step_kernel,
        grid=(LRU_BLOCKS,),
        in_specs=[rows, taps, rows, chan(CONV_W), chan(1), blockw, chan(1), blockw, chan(1), chan(1)],
        out_specs=[rows, taps],
        out_shape=[
            jax.ShapeDtypeStruct((bsz, w), F32),
            jax.ShapeDtypeStruct((CONV_W - 1, bsz, w), F32),
        ],
        compiler_params=_params("parallel"),
        name="rglru_step",
    )(u, conv_prev_t, h0, conv_w, conv_b.reshape(1, w), w_r, b_r.reshape(1, w),
      w_i, b_i.reshape(1, w), lam.reshape(1, w))


def _buckets(dist):
    n = jnp.maximum(dist, 0)
    max_exact = N_BUCKETS // 2
    nf = jnp.maximum(n, 1).astype(F32)
    large = max_exact + (jnp.log(nf / max_exact) / math.log(MAX_DISTANCE / max_exact)
                         * (N_BUCKETS - max_exact)).astype(jnp.int32)
    large = jnp.minimum(large, N_BUCKETS - 1)
    return jnp.where(n < max_exact, n, large)


def _lookup(bucket, valid, table_ref, head):
    bias = jnp.zeros(bucket.shape, F32)
    for b in range(N_BUCKETS):
        bias = jnp.where(bucket == b, table_ref[b, head], bias)
    return jnp.where(valid, bias, NEG_INF)


def _bias_kernel(table_ref, sinks_ref, band_ref, sinkt_ref, past_ref, new_ref):
    hk = pl.program_id(0)
    kj = lax.broadcasted_iota(jnp.int32, (2 * BLOCK, BLOCK), 0)
    qi = lax.broadcasted_iota(jnp.int32, (2 * BLOCK, BLOCK), 1)
    dist = qi + BLOCK - kj
    bucket = _buckets(dist)
    in_window = (dist >= 0) & (dist < WINDOW)
    rows = past_ref.shape[2] // N_KV_HEADS
    col = lax.broadcasted_iota(jnp.int32, (1, rows * N_KV_HEADS), 1)
    d_past = rows - col // N_KV_HEADS
    b_past = _buckets(d_past)
    own_past = (d_past >= 0) & (d_past < WINDOW) & (col % N_KV_HEADS == hk)
    own_new = lax.broadcasted_iota(jnp.int32, (1, N_KV_HEADS), 1) == hk
    b_new = _buckets(jnp.zeros((1, N_KV_HEADS), jnp.int32))
    for par in range(HEADS_PER_TILE):
        for slab in range(SLABS):
            g = slab * HEADS_PER_TILE + par
            head = hk * GROUP + g
            rs = slice(par * 2 * BLOCK, (par + 1) * 2 * BLOCK)
            cs = slice(slab * BLOCK, (slab + 1) * BLOCK)
            band_ref[0, 0, rs, cs] = _lookup(bucket, in_window, table_ref, head)
            band_ref[1, 0, rs, cs] = _lookup(bucket, in_window & (kj >= BLOCK), table_ref, head)
            sinkt_ref[0, par, :, cs] = jnp.full((1, BLOCK), sinks_ref[head], F32)
            past_ref[0, g:g + 1, :] = _lookup(b_past, own_past, table_ref, head)
            new_ref[0, g:g + 1, :] = _lookup(b_new, own_new, table_ref, head)


def bias_tables(table, sinks, past_rows):
    smem = pl.BlockSpec(memory_space=pltpu.SMEM)
    ncol = past_rows * N_KV_HEADS
    return pl.pallas_call(
        _bias_kernel,
        grid=(N_KV_HEADS,),
        in_specs=[smem, smem],
        out_specs=[
            pl.BlockSpec((2, 1, HEADS_PER_TILE * 2 * BLOCK, SLABS * BLOCK), lambda h: (0, h, 0, 0)),
            pl.BlockSpec((1, HEADS_PER_TILE, 1, SLABS * BLOCK), lambda h: (h, 0, 0, 0)),
            pl.BlockSpec((1, GROUP, ncol), lambda h: (h, 0, 0)),
            pl.BlockSpec((1, GROUP, N_KV_HEADS), lambda h: (h, 0, 0)),
        ],
        out_shape=[
            jax.ShapeDtypeStruct((2, N_KV_HEADS, HEADS_PER_TILE * 2 * BLOCK, SLABS * BLOCK), F32),
            jax.ShapeDtypeStruct((N_KV_HEADS, HEADS_PER_TILE, 1, SLABS * BLOCK), F32),
            jax.ShapeDtypeStruct((N_KV_HEADS, GROUP, ncol), F32),
            jax.ShapeDtypeStruct((N_KV_HEADS, GROUP, N_KV_HEADS), F32),
        ],
        compiler_params=_params("parallel"),
        name="bias_tables",
    )(table, sinks)


def _band_attn_kernel(q_ref, kp_ref, kc_ref, vp_ref, vc_ref, gate_ref, bias_ref, sink_ref, y_ref):
    first = (pl.program_id(1) == 0).astype(jnp.int32)
    nt = (((1,), (1,)), ((), ()))
    low = (lax.broadcasted_iota(jnp.int32, (1, LANES), 1) < HEAD_DIM)
    keep_low = low.astype(BF16)
    keep_high = 1 - keep_low
    nkeys = 2 * BLOCK
    zeros_v = jnp.zeros((HEAD_DIM, nkeys), BF16)
    for hk in range(N_KV_HEADS):
        cs = slice(hk * LANES, (hk + 1) * LANES)
        kd = jnp.concatenate([kp_ref[:, cs], kc_ref[:, cs]], axis=0)
        lhs = jnp.concatenate([kd * keep_low, kd * keep_high], axis=0)
        qs = jnp.concatenate([q_ref[:, (hk * SLABS + s) * LANES:(hk * SLABS + s + 1) * LANES]
                              for s in range(SLABS)], axis=0)
        s = lax.dot_general(lhs, qs, nt, preferred_element_type=F32) + bias_ref[first, hk]
        s = s.reshape(HEADS_PER_TILE, nkeys, SLABS * BLOCK)
        sink = sink_ref[hk]
        m = jnp.maximum(jnp.max(s, axis=1, keepdims=True), sink)
        p = jnp.exp(s - m)
        denom = jnp.sum(p, axis=1, keepdims=True) + jnp.exp(sink - m)
        vt = jnp.concatenate([vp_ref[cs, :], vc_ref[cs, :]], axis=1)
        lhs_v = jnp.concatenate([
            jnp.concatenate([vt[:HEAD_DIM], zeros_v], axis=0),
            jnp.concatenate([zeros_v, vt[HEAD_DIM:]], axis=0)], axis=1)
        ot = jnp.dot(lhs_v, p.reshape(HEADS_PER_TILE * nkeys, SLABS * BLOCK).astype(BF16),
                     preferred_element_type=F32)
        inv = 1.0 / denom
        ot = jnp.concatenate([ot[:HEAD_DIM] * inv[0], ot[HEAD_DIM:] * inv[1]], axis=0)
        o = ot.T
        for sl in range(SLABS):
            c0 = (hk * SLABS + sl) * LANES
            y_ref[:, c0:c0 + LANES] = (o[sl * BLOCK:(sl + 1) * BLOCK]
                                       * _silu(gate_ref[:, c0:c0 + LANES])).astype(y_ref.dtype)


def band_attention(q, kdup, vt, gate, bias_band, sink_t, bsz, t):
    m = q.shape[0]
    nblk = t // BLOCK
    cur = lambda n: pl.BlockSpec((BLOCK, n), lambda b, i: (b * nblk + i, 0))
    prev = lambda n: pl.BlockSpec((BLOCK, n), lambda b, i: (b * nblk + jnp.maximum(i - 1, 0), 0))
    cur_t = pl.BlockSpec((2 * KV_WIDTH, BLOCK), lambda b, i: (0, b * nblk + i))
    prev_t = pl.BlockSpec((2 * KV_WIDTH, BLOCK), lambda b, i: (0, b * nblk + jnp.maximum(i - 1, 0)))
    return pl.pallas_call(
        _band_attn_kernel,
        grid=(bsz, nblk),
        in_specs=[
            cur(ATT_WIDTH), prev(2 * KV_WIDTH), cur(2 * KV_WIDTH), prev_t, cur_t,
            cur(ATT_WIDTH), _resident(bias_band.shape), _resident(sink_t.shape),
        ],
        out_specs=cur(ATT_WIDTH),
        out_shape=jax.ShapeDtypeStruct((m, ATT_WIDTH), BF16),
        compiler_params=_params("parallel", "parallel"),
        name="band_attention",
    )(q, kdup, kdup, vt, vt, gate, bias_band, sink_t)


def _cached_attn_kernel(q_ref, ck_ref, cv_ref, kn_ref, vn_ref, sinks_ref, bpast_ref, bnew_ref, o_ref):
    nt = (((1,), (1,)), ((), ()))
    q = q_ref[0].astype(BF16)
    s = lax.dot_general(q, ck_ref[0].astype(BF16), nt, preferred_element_type=F32) + bpast_ref[...]
    s_new = lax.dot_general(q, kn_ref[0].astype(BF16), nt, preferred_element_type=F32) + bnew_ref[...]
    sink = sinks_ref[...]
    m = jnp.maximum(jnp.maximum(jnp.max(s, axis=-1, keepdims=True),
                                jnp.max(s_new, axis=-1, keepdims=True)), sink)
    p = jnp.exp(s - m)
    p_new = jnp.exp(s_new - m)
    denom = (jnp.sum(p, axis=-1, keepdims=True) + jnp.sum(p_new, axis=-1, keepdims=True)
             + jnp.exp(sink - m))
    o = (jnp.dot(p.astype(BF16), cv_ref[0].astype(BF16), preferred_element_type=F32)
         + jnp.dot(p_new.astype(BF16), vn_ref[0].astype(BF16), preferred_element_type=F32))
    o_ref[0] = o / denom


def cached_attention(q, cache_k, cache_v, k_new, v_new, sinks, bias_past, bias_new):
    bsz, ncol = cache_k.shape[0], cache_k.shape[1]
    per_seq = lambda r, n: pl.BlockSpec((1, r, n), lambda b: (b, 0, 0))
    return pl.pallas_call(
        _cached_attn_kernel,
        grid=(bsz,),
        in_specs=[
            per_seq(N_Q_HEADS, HEAD_DIM), per_seq(ncol, HEAD_DIM), per_seq(ncol, HEAD_DIM),
            per_seq(N_KV_HEADS, HEAD_DIM), per_seq(N_KV_HEADS, HEAD_DIM),
            _resident((N_Q_HEADS, 1)), _resident((N_Q_HEADS, ncol)), _resident((N_Q_HEADS, N_KV_HEADS)),
        ],
        out_specs=per_seq(N_Q_HEADS, HEAD_DIM),
        out_shape=jax.ShapeDtypeStruct((bsz, N_Q_HEADS, HEAD_DIM), F32),
        compiler_params=_params("parallel"),
        name="cached_attention",
    )(q, cache_k, cache_v, k_new, v_new, sinks, bias_past, bias_new)


def kernel(x_prompt, x_sample, state_conv, state_h, cache_k, cache_v, a_norm_pre, a_norm_post,
           a_w_in, a_conv_w, a_conv_b, a_w_r, a_b_r, a_w_i, a_b_i, a_lambda, a_w_out, kv_norm, w_kv,
           b_norm_pre, b_norm_post, b_w_qg, b_sinks, b_w_out, rel_bias_table):
    bsz, t, d = x_prompt.shape
    dbsz, dt, _ = x_sample.shape
    assert a_w_in.shape[0] == 1 and b_w_qg.shape[0] == 1 and dt == 1
    assert t % BLOCK == 0 and t >= WINDOW
    past_rows = cache_k.shape[1]
    assert past_rows == min(WINDOW, PAST_LEN)

    w_in = a_w_in[0].astype(BF16)
    w_r = a_w_r[0].astype(BF16)
    w_i = a_w_i[0].astype(BF16)
    w_aout = a_w_out[0].astype(BF16)
    w_kvb = w_kv.astype(BF16)
    w_qg = b_w_qg[0].astype(BF16)
    w_bout = b_w_out[0].astype(BF16)
    sinks = b_sinks[0]

    bias_band, sink_t, bias_past, bias_new = bias_tables(rel_bias_table, sinks, past_rows)
    lru = (a_conv_w[0], a_conv_b[0], w_r, a_b_r[0], w_i, a_b_i[0], a_lambda[0])

    tm = 256
    xp = x_prompt.reshape(bsz * t, d)
    u, gate = norm_proj(xp, a_norm_pre[0], w_in, tm)
    conv0 = jnp.zeros((bsz, CONV_W - 1, LRU_WIDTH), F32)
    h0 = jnp.zeros((bsz, LRU_WIDTH), F32)
    y, p_conv, p_h = rglru_scan(u.reshape(bsz, t, LRU_WIDTH), gate.reshape(bsz, t, LRU_WIDTH),
                                conv0, h0, *lru, tc=256)
    x1 = proj_norm_res(y.reshape(bsz * t, LRU_WIDTH), None, w_aout, a_norm_post[0], xp, tm)
    k, v, q, gate_b, kdup, vt = norm_proj_kvq(x1, kv_norm, b_norm_pre[0], w_kvb, w_qg, tm, True)
    yb = band_attention(q, kdup, vt, gate_b, bias_band, sink_t, bsz, t)
    y_prompt = proj_norm_res(yb, None, w_bout, b_norm_post[0], x1, tm).reshape(bsz, t, d)
    keep = min(WINDOW, t)
    p_k = k.reshape(bsz, t, N_KV_HEADS, HEAD_DIM)[:, -keep:]
    p_v = v.reshape(bsz, t, N_KV_HEADS, HEAD_DIM)[:, -keep:]

    xs = x_sample.reshape(dbsz, d)
    us, gate_s = norm_proj(xs, a_norm_pre[0], w_in, dbsz)
    hs, s_conv_t = rglru_step(us, jnp.transpose(state_conv[0], (1, 0, 2)), state_h[0], *lru)
    xs1 = proj_norm_res(hs, gate_s, w_aout, a_norm_post[0], xs, dbsz)
    ks, vs, qs, gate_sb = norm_proj_kvq(xs1, kv_norm, b_norm_pre[0], w_kvb, w_qg, dbsz, False)
    ks = ks.reshape(dbsz, N_KV_HEADS, HEAD_DIM)
    vs = vs.reshape(dbsz, N_KV_HEADS, HEAD_DIM)
    ncol = past_rows * N_KV_HEADS
    os_ = cached_attention(qs.reshape(dbsz, N_Q_HEADS, HEAD_DIM),
                           cache_k.reshape(dbsz, ncol, HEAD_DIM), cache_v.reshape(dbsz, ncol, HEAD_DIM),
                           ks, vs, sinks.reshape(N_Q_HEADS, 1),
                           bias_past.reshape(N_Q_HEADS, ncol), bias_new.reshape(N_Q_HEADS, N_KV_HEADS))
    y_sample = proj_norm_res(os_.reshape(dbsz, ATT_WIDTH), gate_sb, w_bout, b_norm_post[0], xs1, dbsz)

    return (y_prompt, y_sample.reshape(dbsz, 1, d),
            p_conv[None], p_h.reshape(1, bsz, LRU_WIDTH), p_k, p_v,
            jnp.transpose(s_conv_t, (1, 0, 2))[None], hs[None],
            ks.reshape(dbsz, 1, N_KV_HEADS, HEAD_DIM), vs.reshape(dbsz, 1, N_KV_HEADS, HEAD_DIM))
```

```python
import functools
import math

import jax
import jax.numpy as jnp
from jax import lax
from jax.experimental import pallas as pl
from jax.experimental.pallas import tpu as pltpu

F32 = jnp.float32
BF16 = jnp.bfloat16

D_MODEL = 2048
LRU_WIDTH = 2048
LRU_BLOCKS = 8
LRU_BLOCK_W = LRU_WIDTH // LRU_BLOCKS
CONV_W = 4
LRU_C = 8.0
HEAD_DIM = 64
N_Q_HEADS = 32
N_KV_HEADS = 8
GROUP = N_Q_HEADS // N_KV_HEADS
ATT_WIDTH = N_Q_HEADS * HEAD_DIM
KV_WIDTH = N_KV_HEADS * HEAD_DIM
WINDOW = 128
BLOCK = WINDOW
N_BUCKETS = 32
MAX_DISTANCE = 128
RMS_EPS = 1e-6
NEG_INF = -1e30
LOG2_E = 1.4426950408889634
PAST_LEN = 16384

V7X_VMEM_BYTES = 64 * 1024 * 1024
VMEM_LIMIT = V7X_VMEM_BYTES - 8 * 1024 * 1024
SUBLANES = 8
LANES = 128
HEADS_PER_TILE = LANES // HEAD_DIM
SLABS = GROUP // HEADS_PER_TILE


def _params(*semantics):
    return pltpu.CompilerParams(dimension_semantics=semantics, vmem_limit_bytes=VMEM_LIMIT)


def _resident(shape):
    zeros = (0,) * len(shape)
    return pl.BlockSpec(shape, lambda *_: zeros, pipeline_mode=pl.Buffered(1))


def _rms_scale(x):
    return lax.rsqrt(jnp.mean(x * x, axis=-1, keepdims=True) + RMS_EPS)


def _silu(x):
    h = 0.5 * x
    return h * jnp.tanh(h) + h


def _segment_major(rows, inverse=False):
    seg = rows // SUBLANES
    r = lax.broadcasted_iota(jnp.int32, (rows, rows), 0)
    c = lax.broadcasted_iota(jnp.int32, (rows, rows), 1)
    if inverse:
        src = (r % seg) * SUBLANES + r // seg
    else:
        src = (r % SUBLANES) * seg + r // SUBLANES
    return jnp.where(c == src, 1.0, 0.0).astype(BF16)


def _norm_proj_kernel(segment_major, x_ref, g_ref, w_ref, o1_ref, o2_ref):
    x = x_ref[...]
    xn = (x * _rms_scale(x) * g_ref[...]).astype(BF16)
    if segment_major:
        xn = jnp.dot(_segment_major(xn.shape[0]), xn, preferred_element_type=F32).astype(BF16)
    r = jnp.dot(xn, w_ref[...], preferred_element_type=F32)
    n1 = o1_ref.shape[-1]
    o1_ref[...] = r[:, :n1]
    o2_ref[...] = r[:, n1:]


def norm_proj(x, g, w, tm, segment_major=False):
    m, d = x.shape
    n = w.shape[1]
    n1 = n // 2
    return pl.pallas_call(
        functools.partial(_norm_proj_kernel, segment_major),
        grid=(m // tm,),
        in_specs=[
            pl.BlockSpec((tm, d), lambda i: (i, 0)),
            _resident((1, d)),
            _resident(w.shape),
        ],
        out_specs=[
            pl.BlockSpec((tm, n1), lambda i: (i, 0)),
            pl.BlockSpec((tm, n - n1), lambda i: (i, 0)),
        ],
        out_shape=[
            jax.ShapeDtypeStruct((m, n1), F32),
            jax.ShapeDtypeStruct((m, n - n1), F32),
        ],
        compiler_params=_params("parallel"),
        name="norm_proj",
    )(x, g.reshape(1, d), w)


def _dup_heads(x):
    low = lax.broadcasted_iota(jnp.int32, (x.shape[0], LANES), 1) < HEAD_DIM
    out = []
    for c in range(x.shape[1] // LANES):
        col = x[:, c * LANES:(c + 1) * LANES]
        swapped = pltpu.roll(col, HEAD_DIM, axis=1)
        out += [jnp.where(low, col, swapped), jnp.where(low, swapped, col)]
    return jnp.concatenate(out, axis=1)


def _norm_proj_kvq_kernel(banded, x_ref, gkv_ref, gq_ref, wkv_ref, wqg_ref, q_ref, gate_ref, *kv_refs):
    x = x_ref[...]
    xh = x * _rms_scale(x)
    kv = jnp.dot((xh * gkv_ref[...]).astype(BF16), wkv_ref[...], preferred_element_type=F32)
    qg = jnp.dot((xh * gq_ref[...]).astype(BF16), wqg_ref[...], preferred_element_type=F32)
    k = kv[:, :KV_WIDTH]
    v = kv[:, KV_WIDTH:]
    q_ref[...] = (qg[:, :ATT_WIDTH] * (1.0 / math.sqrt(HEAD_DIM))).astype(q_ref.dtype)
    gate_ref[...] = qg[:, ATT_WIDTH:]
    if banded:
        kdup_ref, vt_ref, ktail_ref, vtail_ref = kv_refs
        kdup_ref[...] = _dup_heads(k).astype(BF16)
        vt_ref[...] = _dup_heads(v).T.astype(BF16)
        tm = x.shape[0]
        ktail_ref[0] = k[tm - WINDOW:].T
        vtail_ref[0] = v[tm - WINDOW:].T
    else:
        k_ref, v_ref = kv_refs
        k_ref[...] = k
        v_ref[...] = v


def norm_proj_kvq(x, g_kv, g_q, w_kv, w_qg, tm, seq_len=None):
    m, d = x.shape
    banded = seq_len is not None
    row = lambda n: pl.BlockSpec((tm, n), lambda i: (i, 0))
    out_specs = [row(ATT_WIDTH), row(ATT_WIDTH)]
    out_shape = [
        jax.ShapeDtypeStruct((m, ATT_WIDTH), BF16 if banded else F32),
        jax.ShapeDtypeStruct((m, ATT_WIDTH), F32),
    ]
    if banded:
        assert seq_len % tm == 0 and tm >= WINDOW
        tiles = seq_len // tm
        tail = pl.BlockSpec((1, KV_WIDTH, WINDOW), lambda i: (i // tiles, 0, 0))
        out_specs += [row(2 * KV_WIDTH), pl.BlockSpec((2 * KV_WIDTH, tm), lambda i: (0, i)), tail, tail]
        out_shape += [jax.ShapeDtypeStruct((m, 2 * KV_WIDTH), BF16),
                      jax.ShapeDtypeStruct((2 * KV_WIDTH, m), BF16),
                      jax.ShapeDtypeStruct((m // seq_len, KV_WIDTH, WINDOW), F32),
                      jax.ShapeDtypeStruct((m // seq_len, KV_WIDTH, WINDOW), F32)]
    else:
        out_specs += [row(KV_WIDTH), row(KV_WIDTH)]
        out_shape += [jax.ShapeDtypeStruct((m, KV_WIDTH), F32)] * 2
    return pl.pallas_call(
        functools.partial(_norm_proj_kvq_kernel, banded),
        grid=(m // tm,),
        in_specs=[
            row(d),
            _resident((1, d)),
            _resident((1, d)),
            _resident(w_kv.shape),
            _resident(w_qg.shape),
        ],
        out_specs=out_specs,
        out_shape=out_shape,
        compiler_params=_params("arbitrary"),
        name="norm_proj_kvq",
    )(x, g_kv.reshape(1, d), g_q.reshape(1, d), w_kv, w_qg)


def _proj_norm_res_kernel(gated, *refs):
    if gated:
        a_ref, gate_ref, w_ref, g_ref, x_ref, o_ref = refs
        y = (a_ref[...] * _silu(gate_ref[...])).astype(BF16)
    else:
        a_ref, w_ref, g_ref, x_ref, o_ref = refs
        y = a_ref[...]
    o = jnp.dot(y, w_ref[...], preferred_element_type=F32)
    o_ref[...] = x_ref[...] + o * _rms_scale(o) * g_ref[...]


def proj_norm_res(a, gate, w, g, x, tm):
    k, d = w.shape
    m = x.shape[0]
    row = lambda n: pl.BlockSpec((tm, n), lambda i: (i, 0))
    gated = gate is not None
    ins = [a] + ([gate] if gated else []) + [w, g.reshape(1, d), x]
    in_specs = [row(k)] + ([row(k)] if gated else []) + [_resident(w.shape), _resident((1, d)), row(d)]
    return pl.pallas_call(
        functools.partial(_proj_norm_res_kernel, gated),
        grid=(m // tm,),
        in_specs=in_specs,
        out_specs=row(d),
        out_shape=jax.ShapeDtypeStruct((m, d), F32),
        compiler_params=_params("parallel"),
        name="proj_norm_res",
    )(*ins)


def _lru_gates(conv, wr_half, br, wi_half, bi, lam):
    cb = conv.astype(BF16)
    th_r = jnp.tanh(jnp.dot(cb, wr_half, preferred_element_type=F32) + 0.5 * br)
    th_i = jnp.tanh(jnp.dot(cb, wi_half, preferred_element_type=F32) + 0.5 * bi)
    nl = -lam
    softplus = jnp.maximum(nl, 0.0) + jnp.log1p(jnp.exp(-jnp.abs(nl)))
    half = (0.5 * LRU_C) * softplus
    x = th_r * half + half
    a = jnp.exp2(x * -LOG2_E)
    z = jnp.tanh(x) * (a * a + 1.0)
    mult = z * lax.rsqrt(jnp.maximum(z, 1e-30))
    hc = 0.5 * conv
    return a, mult * (hc * th_i + hc)


def _rglru_scan_kernel(u_ref, gate_ref, cprev_ref, h0_ref, cw_ref, cb_ref, wr_ref, br_ref,
                       wi_ref, bi_ref, lam_ref, y_ref, cnew_ref, hlast_ref, h_scr, tail_scr):
    t = pl.program_id(1)
    tc = u_ref.shape[0]
    seg = tc // SUBLANES
    ntaps = CONV_W - 1
    bw = LRU_BLOCK_W

    @pl.when(t == 0)
    def _():
        h_scr[...] = h0_ref[0]
        tail_scr[...] = cprev_ref[0]

    sub = lax.broadcasted_iota(jnp.int32, (SUBLANES, bw), 0)
    first = sub == 0
    time_order = _segment_major(tc, inverse=True)

    def shift_in(x, row0):
        return jnp.where(first, row0, pltpu.roll(x, 1, axis=0))

    def group(x, j):
        return x[j * SUBLANES:(j + 1) * SUBLANES]

    for n in range(LRU_BLOCKS):
        cs = slice(n * bw, (n + 1) * bw)
        u = u_ref[:, cs]
        tail = tail_scr[:, cs]
        before = [shift_in(group(u, seg - m), tail[ntaps - m:ntaps - m + 1])
                  for m in range(ntaps, 0, -1)]
        ext = jnp.concatenate(before + [u], axis=0)
        tail_scr[:, cs] = jnp.concatenate(
            [group(u, seg - m)[SUBLANES - 1:] for m in range(ntaps, 0, -1)], axis=0)
        cw = cw_ref[:, cs]
        conv = cb_ref[:, cs]
        for tap in range(CONV_W):
            conv = conv + ext[tap * SUBLANES:tap * SUBLANES + tc] * cw[tap:tap + 1]

        a, b = _lru_gates(conv, wr_ref[n], br_ref[:, cs], wi_ref[n], bi_ref[:, cs], lam_ref[:, cs])

        h = b[:SUBLANES]
        acc = a[:SUBLANES]
        h_loc, a_cum = [h], [acc]
        for j in range(1, seg):
            sl = slice(j * SUBLANES, (j + 1) * SUBLANES)
            h = a[sl] * h + b[sl]
            acc = a[sl] * acc
            h_loc.append(h)
            a_cum.append(acc)

        step = 1
        while step < SUBLANES:
            keep = sub >= step
            h = jnp.where(keep, acc * pltpu.roll(h, step, axis=0) + h, h)
            acc = jnp.where(keep, acc * pltpu.roll(acc, step, axis=0), acc)
            step *= 2
        h_prev = h_scr[:, cs]
        after = h + acc * h_prev
        h_in = shift_in(after, h_prev)
        h_scr[:, cs] = after[SUBLANES - 1:]

        hs = jnp.concatenate([h_loc[j] + a_cum[j] * h_in for j in range(seg)], axis=0)
        y = (hs * _silu(gate_ref[:, cs])).astype(BF16)
        y_ref[:, cs] = jnp.dot(time_order, y, preferred_element_type=F32).astype(y_ref.dtype)

    @pl.when(t == pl.num_programs(1) - 1)
    def _():
        hlast_ref[0] = h_scr[...]
        cnew_ref[0] = tail_scr[...]


def rglru_scan(u, gate, conv_prev, h0, conv_w, conv_b, w_r, b_r, w_i, b_i, lam, seq_len, tc):
    m, w = u.shape
    bsz = m // seq_len
    chunks = seq_len // tc
    assert seq_len % tc == 0 and tc % (SUBLANES * SUBLANES) == 0 and tc // SUBLANES > CONV_W
    seq = pl.BlockSpec((tc, w), lambda b, s: (b * chunks + s, 0))
    state = lambda rows: pl.BlockSpec((1, rows, w), lambda b, s: (b, 0, 0))
    return pl.pallas_call(
        _rglru_scan_kernel,
        grid=(bsz, chunks),
        in_specs=[seq, seq, state(CONV_W - 1), state(1), _resident((CONV_W, w)), _resident((1, w)),
                  _resident(w_r.shape), _resident((1, w)), _resident(w_i.shape), _resident((1, w)),
                  _resident((1, w))],
        out_specs=[seq, state(CONV_W - 1), state(1)],
        out_shape=[
            jax.ShapeDtypeStruct((m, w), BF16),
            jax.ShapeDtypeStruct((bsz, CONV_W - 1, w), F32),
            jax.ShapeDtypeStruct((bsz, 1, w), F32),
        ],
        scratch_shapes=[pltpu.VMEM((1, w), F32), pltpu.VMEM((CONV_W - 1, w), F32)],
        compiler_params=_params("parallel", "arbitrary"),
        name="rglru_scan",
    )(u, gate, conv_prev, h0.reshape(bsz, 1, w), conv_w, conv_b.reshape(1, w),
      w_r, b_r.reshape(1, w), w_i, b_i.reshape(1, w), lam.reshape(1, w))


def _rglru_step_kernel(u_ref, cprev_ref, h0_ref, cw_ref, cb_ref, wr_ref, br_ref, wi_ref, bi_ref,
                       lam_ref, h_ref, cnew_ref):
    u = u_ref[...]
    cw = cw_ref[...]
    conv = cb_ref[...]
    for tap in range(CONV_W - 1):
        conv = conv + cprev_ref[tap] * cw[tap:tap + 1]
        if tap > 0:
            cnew_ref[tap - 1] = cprev_ref[tap]
    conv = conv + u * cw[CONV_W - 1:]
    cnew_ref[CONV_W - 2] = u
    a, b = _lru_gates(conv, wr_ref[0], br_ref[...], wi_ref[0], bi_ref[...], lam_ref[...])
    h_ref[...] = a * h0_ref[...] + b


def rglru_step(u, conv_prev_t, h0, conv_w, conv_b, w_r, b_r, w_i, b_i, lam):
    bsz, w = u.shape
    bw = LRU_BLOCK_W
    rows = pl.BlockSpec((bsz, bw), lambda n: (0, n))
    taps = pl.BlockSpec((CONV_W - 1, bsz, bw), lambda n: (0, 0, n))
    chan = lambda r: pl.BlockSpec((r, bw), lambda n: (0, n))
    blockw = pl.BlockSpec((1, bw, bw), lambda n: (n, 0, 0))
    return pl.pallas_call(
        _rglru_step_kernel,
        grid=(LRU_BLOCKS,),
        in_specs=[rows, taps, rows, chan(CONV_W), chan(1), blockw, chan(1), blockw, chan(1), chan(1)],
        out_specs=[rows, taps],
        out_shape=[
            jax.ShapeDtypeStruct((bsz, w), F32),
            jax.ShapeDtypeStruct((CONV_W - 1, bsz, w), F32),
        ],
        compiler_params=_params("parallel"),
        name="rglru_step",
    )(u, conv_prev_t, h0, conv_w, conv_b.reshape(1, w), w_r, b_r.reshape(1, w),
      w_i, b_i.reshape(1, w), lam.reshape(1, w))


def _buckets(dist):
    n = jnp.maximum(dist, 0)
    max_exact = N_BUCKETS // 2
    nf = jnp.maximum(n, 1).astype(F32)
    large = max_exact + (jnp.log(nf / max_exact) / math.log(MAX_DISTANCE / max_exact)
                         * (N_BUCKETS - max_exact)).astype(jnp.int32)
    large = jnp.minimum(large, N_BUCKETS - 1)
    return jnp.where(n < max_exact, n, large)


def _lookup(bucket, valid, table_ref, head):
    bias = jnp.zeros(bucket.shape, F32)
    for b in range(N_BUCKETS):
        bias = jnp.where(bucket == b, table_ref[b, head], bias)
    return jnp.where(valid, bias, NEG_INF)


def _bias_kernel(table_ref, sinks_ref, band_ref, sinkt_ref, past_ref, new_ref):
    hk = pl.program_id(0)
    kj = lax.broadcasted_iota(jnp.int32, (2 * BLOCK, BLOCK), 0)
    qi = lax.broadcasted_iota(jnp.int32, (2 * BLOCK, BLOCK), 1)
    dist = qi + BLOCK - kj
    bucket = _buckets(dist)
    in_window = (dist >= 0) & (dist < WINDOW)
    rows = past_ref.shape[2]
    d_past = rows - lax.broadcasted_iota(jnp.int32, (1, rows), 1)
    b_past = _buckets(d_past)
    ok_past = (d_past >= 0) & (d_past < WINDOW)
    d_new = jnp.zeros((1, LANES), jnp.int32)
    b_new = _buckets(d_new)
    for par in range(HEADS_PER_TILE):
        for slab in range(SLABS):
            g = slab * HEADS_PER_TILE + par
            head = hk * GROUP + g
            rs = slice(par * 2 * BLOCK, (par + 1) * 2 * BLOCK)
            cs = slice(slab * BLOCK, (slab + 1) * BLOCK)
            band_ref[0, 0, rs, cs] = _lookup(bucket, in_window, table_ref, head)
            band_ref[1, 0, rs, cs] = _lookup(bucket, in_window & (kj >= BLOCK), table_ref, head)
            sinkt_ref[0, par, :, cs] = jnp.full((1, BLOCK), sinks_ref[head], F32)
            past_ref[0, g:g + 1, :] = _lookup(b_past, ok_past, table_ref, head)
            new_ref[0, g:g + 1, :] = _lookup(b_new, d_new == 0, table_ref, head)


def bias_tables(table, sinks, past_rows):
    smem = pl.BlockSpec(memory_space=pltpu.SMEM)
    return pl.pallas_call(
        _bias_kernel,
        grid=(N_KV_HEADS,),
        in_specs=[smem, smem],
        out_specs=[
            pl.BlockSpec((2, 1, HEADS_PER_TILE * 2 * BLOCK, SLABS * BLOCK), lambda h: (0, h, 0, 0)),
            pl.BlockSpec((1, HEADS_PER_TILE, 1, SLABS * BLOCK), lambda h: (h, 0, 0, 0)),
            pl.BlockSpec((1, GROUP, past_rows), lambda h: (h, 0, 0)),
            pl.BlockSpec((1, GROUP, LANES), lambda h: (h, 0, 0)),
        ],
        out_shape=[
            jax.ShapeDtypeStruct((2, N_KV_HEADS, HEADS_PER_TILE * 2 * BLOCK, SLABS * BLOCK), F32),
            jax.ShapeDtypeStruct((N_KV_HEADS, HEADS_PER_TILE, 1, SLABS * BLOCK), F32),
            jax.ShapeDtypeStruct((N_KV_HEADS, GROUP, past_rows), F32),
            jax.ShapeDtypeStruct((N_KV_HEADS, GROUP, LANES), F32),
        ],
        compiler_params=_params("parallel"),
        name="bias_tables",
    )(table, sinks)


def _band_attn_kernel(q_ref, kp_ref, kc_ref, vp_ref, vc_ref, gate_ref, bias_ref, sink_ref, y_ref):
    first = (pl.program_id(1) == 0).astype(jnp.int32)
    nt = (((1,), (1,)), ((), ()))
    low = (lax.broadcasted_iota(jnp.int32, (1, LANES), 1) < HEAD_DIM)
    keep_low = low.astype(BF16)
    keep_high = 1 - keep_low
    nkeys = 2 * BLOCK
    zeros_v = jnp.zeros((HEAD_DIM, nkeys), BF16)
    for hk in range(N_KV_HEADS):
        cs = slice(hk * LANES, (hk + 1) * LANES)
        kd = jnp.concatenate([kp_ref[:, cs], kc_ref[:, cs]], axis=0)
        lhs = jnp.concatenate([kd * keep_low, kd * keep_high], axis=0)
        qs = jnp.concatenate([q_ref[:, (hk * SLABS + s) * LANES:(hk * SLABS + s + 1) * LANES]
                              for s in range(SLABS)], axis=0)
        s = lax.dot_general(lhs, qs, nt, preferred_element_type=F32) + bias_ref[first, hk]
        s = s.reshape(HEADS_PER_TILE, nkeys, SLABS * BLOCK)
        sink = sink_ref[hk]
        m = jnp.maximum(jnp.max(s, axis=1, keepdims=True), sink)
        p = jnp.exp(s - m)
        denom = jnp.sum(p, axis=1, keepdims=True) + jnp.exp(sink - m)
        vt = jnp.concatenate([vp_ref[cs, :], vc_ref[cs, :]], axis=1)
        lhs_v = jnp.concatenate([
            jnp.concatenate([vt[:HEAD_DIM], zeros_v], axis=0),
            jnp.concatenate([zeros_v, vt[HEAD_DIM:]], axis=0)], axis=1)
        ot = jnp.dot(lhs_v, p.reshape(HEADS_PER_TILE * nkeys, SLABS * BLOCK).astype(BF16),
                     preferred_element_type=F32)
        inv = 1.0 / denom
        ot = jnp.concatenate([ot[:HEAD_DIM] * inv[0], ot[HEAD_DIM:] * inv[1]], axis=0)
        o = ot.T
        for sl in range(SLABS):
            c0 = (hk * SLABS + sl) * LANES
            y_ref[:, c0:c0 + LANES] = (o[sl * BLOCK:(sl + 1) * BLOCK]
                                       * _silu(gate_ref[:, c0:c0 + LANES])).astype(y_ref.dtype)


def band_attention(q, kdup, vt, gate, bias_band, sink_t, bsz, t):
    m = q.shape[0]
    nblk = t // BLOCK
    cur = lambda n: pl.BlockSpec((BLOCK, n), lambda b, i: (b * nblk + i, 0))
    prev = lambda n: pl.BlockSpec((BLOCK, n), lambda b, i: (b * nblk + jnp.maximum(i - 1, 0), 0))
    cur_t = pl.BlockSpec((2 * KV_WIDTH, BLOCK), lambda b, i: (0, b * nblk + i))
    prev_t = pl.BlockSpec((2 * KV_WIDTH, BLOCK), lambda b, i: (0, b * nblk + jnp.maximum(i - 1, 0)))
    return pl.pallas_call(
        _band_attn_kernel,
        grid=(bsz, nblk),
        in_specs=[
            cur(ATT_WIDTH), prev(2 * KV_WIDTH), cur(2 * KV_WIDTH), prev_t, cur_t,
            cur(ATT_WIDTH), _resident(bias_band.shape), _resident(sink_t.shape),
        ],
        out_specs=cur(ATT_WIDTH),
        out_shape=jax.ShapeDtypeStruct((m, ATT_WIDTH), BF16),
        compiler_params=_params("parallel", "parallel"),
        name="band_attention",
    )(q, kdup, kdup, vt, vt, gate, bias_band, sink_t)


def _cached_attn_kernel(q_ref, ckt_ref, cvt_ref, kn_ref, vn_ref, sinks_ref, bpast_ref, bnew_ref, o_ref):
    q = q_ref[0]
    qt = jnp.concatenate([q] * N_KV_HEADS, axis=1)
    lane_kv = lax.broadcasted_iota(jnp.int32, qt.shape, 1) // HEAD_DIM
    row_kv = lax.broadcasted_iota(jnp.int32, qt.shape, 0) // GROUP
    own = lane_kv == row_kv
    qm = jnp.where(own, qt, 0.0).astype(BF16)
    knew = kn_ref[0].astype(BF16).astype(F32)
    vnew = vn_ref[0].astype(BF16).astype(F32)
    s = jnp.dot(qm, ckt_ref[0].astype(BF16), preferred_element_type=F32) + bpast_ref[...]
    s_new = jnp.sum(qm.astype(F32) * knew, axis=-1, keepdims=True) + bnew_ref[:, :1]
    sink = sinks_ref[...]
    m = jnp.maximum(jnp.maximum(jnp.max(s, axis=-1, keepdims=True), s_new), sink)
    p = jnp.exp(s - m)
    p_new = jnp.exp(s_new - m)
    denom = jnp.sum(p, axis=-1, keepdims=True) + p_new + jnp.exp(sink - m)
    nt = (((1,), (1,)), ((), ()))
    o_all = (lax.dot_general(p.astype(BF16), cvt_ref[0].astype(BF16), nt, preferred_element_type=F32)
             + p_new.astype(BF16).astype(F32) * vnew)
    o_all = jnp.where(own, o_all, 0.0)
    o = o_all[:, :HEAD_DIM]
    for hk in range(1, N_KV_HEADS):
        o = o + o_all[:, hk * HEAD_DIM:(hk + 1) * HEAD_DIM]
    o_ref[0] = o / denom


def cached_attention(q, cache_kt, cache_vt, k_new, v_new, sinks, bias_past, bias_new):
    bsz, _, rows = cache_kt.shape
    per_seq = lambda r, n: pl.BlockSpec((1, r, n), lambda b: (b, 0, 0))
    return pl.pallas_call(
        _cached_attn_kernel,
        grid=(bsz,),
        in_specs=[
            per_seq(N_Q_HEADS, HEAD_DIM), per_seq(KV_WIDTH, rows), per_seq(KV_WIDTH, rows),
            per_seq(1, KV_WIDTH), per_seq(1, KV_WIDTH),
            _resident((N_Q_HEADS, 1)), _resident((N_Q_HEADS, rows)), _resident((N_Q_HEADS, LANES)),
        ],
        out_specs=per_seq(N_Q_HEADS, HEAD_DIM),
        out_shape=jax.ShapeDtypeStruct((bsz, N_Q_HEADS, HEAD_DIM), F32),
        compiler_params=_params("parallel"),
        name="cached_attention",
    )(q, cache_kt, cache_vt, k_new, v_new, sinks, bias_past, bias_new)


def kernel(x_prompt, x_sample, state_conv, state_h, cache_k, cache_v, a_norm_pre, a_norm_post,
           a_w_in, a_conv_w, a_conv_b, a_w_r, a_b_r, a_w_i, a_b_i, a_lambda, a_w_out, kv_norm, w_kv,
           b_norm_pre, b_norm_post, b_w_qg, b_sinks, b_w_out, rel_bias_table):
    bsz, t, d = x_prompt.shape
    dbsz, dt, _ = x_sample.shape
    assert a_w_in.shape[0] == 1 and b_w_qg.shape[0] == 1 and dt == 1
    assert t % BLOCK == 0 and t >= WINDOW
    past_rows = cache_k.shape[1]
    assert past_rows == min(WINDOW, PAST_LEN)

    w_in = a_w_in[0].astype(BF16)
    w_r = (0.5 * a_w_r[0]).astype(BF16)
    w_i = (0.5 * a_w_i[0]).astype(BF16)
    w_aout = a_w_out[0].astype(BF16)
    w_kvb = w_kv.astype(BF16)
    w_qg = b_w_qg[0].astype(BF16)
    w_bout = b_w_out[0].astype(BF16)
    sinks = b_sinks[0]

    bias_band, sink_t, bias_past, bias_new = bias_tables(rel_bias_table, sinks, past_rows)
    lru = (a_conv_w[0], a_conv_b[0], w_r, a_b_r[0], w_i, a_b_i[0], a_lambda[0])

    tm = 256
    xp = x_prompt.reshape(bsz * t, d)
    u, gate = norm_proj(xp, a_norm_pre[0], w_in, tm, segment_major=True)
    conv0 = jnp.zeros((bsz, CONV_W - 1, LRU_WIDTH), F32)
    h0 = jnp.zeros((bsz, LRU_WIDTH), F32)
    y, p_conv, p_h = rglru_scan(u, gate, conv0, h0, *lru, seq_len=t, tc=256)
    x1 = proj_norm_res(y, None, w_aout, a_norm_post[0], xp, tm)
    q, gate_b, kdup, vt, k_tail, v_tail = norm_proj_kvq(x1, kv_norm, b_norm_pre[0], w_kvb, w_qg, tm, seq_len=t)
    yb = band_attention(q, kdup, vt, gate_b, bias_band, sink_t, bsz, t)
    y_prompt = proj_norm_res(yb, None, w_bout, b_norm_post[0], x1, tm).reshape(bsz, t, d)
    p_k = jnp.transpose(k_tail.reshape(bsz, N_KV_HEADS, HEAD_DIM, WINDOW), (0, 3, 1, 2))
    p_v = jnp.transpose(v_tail.reshape(bsz, N_KV_HEADS, HEAD_DIM, WINDOW), (0, 3, 1, 2))

    xs = x_sample.reshape(dbsz, d)
    us, gate_s = norm_proj(xs, a_norm_pre[0], w_in, dbsz)
    hs, s_conv_t = rglru_step(us, jnp.transpose(state_conv[0], (1, 0, 2)), state_h[0], *lru)
    xs1 = proj_norm_res(hs, gate_s, w_aout, a_norm_post[0], xs, dbsz)
    qs, gate_sb, ks, vs = norm_proj_kvq(xs1, kv_norm, b_norm_pre[0], w_kvb, w_qg, dbsz)
    cache_kt = jnp.transpose(cache_k, (0, 2, 3, 1)).reshape(dbsz, KV_WIDTH, past_rows)
    cache_vt = jnp.transpose(cache_v, (0, 2, 3, 1)).reshape(dbsz, KV_WIDTH, past_rows)
    os_ = cached_attention(qs.reshape(dbsz, N_Q_HEADS, HEAD_DIM), cache_kt, cache_vt,
                           ks.reshape(dbsz, 1, KV_WIDTH), vs.reshape(dbsz, 1, KV_WIDTH),
                           sinks.reshape(N_Q_HEADS, 1), bias_past.reshape(N_Q_HEADS, past_rows),
                           bias_new.reshape(N_Q_HEADS, LANES))
    y_sample = proj_norm_res(os_.reshape(dbsz, ATT_WIDTH), gate_sb, w_bout, b_norm_post[0], xs1, dbsz)

    return (y_prompt, y_sample.reshape(dbsz, 1, d),
            p_conv[None], p_h.reshape(1, bsz, LRU_WIDTH), p_k, p_v,
            jnp.transpose(s_conv_t, (1, 0, 2))[None], hs[None],
            ks.reshape(dbsz, 1, N_KV_HEADS, HEAD_DIM), vs.reshape(dbsz, 1, N_KV_HEADS, HEAD_DIM))
```

```python
import functools
import math

import jax
import jax.numpy as jnp
from jax import lax
from jax.experimental import pallas as pl
from jax.experimental.pallas import tpu as pltpu

F32 = jnp.float32
BF16 = jnp.bfloat16

D_MODEL = 2048
LRU_WIDTH = 2048
LRU_BLOCKS = 8
LRU_BLOCK_W = LRU_WIDTH // LRU_BLOCKS
CONV_W = 4
LRU_C = 8.0
HEAD_DIM = 64
N_Q_HEADS = 32
N_KV_HEADS = 8
GROUP = N_Q_HEADS // N_KV_HEADS
ATT_WIDTH = N_Q_HEADS * HEAD_DIM
KV_WIDTH = N_KV_HEADS * HEAD_DIM
WINDOW = 128
BLOCK = WINDOW
N_BUCKETS = 32
MAX_DISTANCE = 128
RMS_EPS = 1e-6
NEG_INF = -1e30
LOG2_E = 1.4426950408889634
PAST_LEN = 16384

V7X_VMEM_BYTES = 64 * 1024 * 1024
VMEM_LIMIT = V7X_VMEM_BYTES - 8 * 1024 * 1024
SUBLANES = 8
LANES = 128
HEADS_PER_TILE = LANES // HEAD_DIM
SLABS = GROUP // HEADS_PER_TILE
ATT_ROWS = 32


def _params(*semantics):
    return pltpu.CompilerParams(dimension_semantics=semantics, vmem_limit_bytes=VMEM_LIMIT)


def _resident(shape):
    zeros = (0,) * len(shape)
    return pl.BlockSpec(shape, lambda *_: zeros, pipeline_mode=pl.Buffered(1))


def _rms_scale(x):
    return lax.rsqrt(jnp.mean(x * x, axis=-1, keepdims=True) + RMS_EPS)


def _silu(x):
    h = 0.5 * x
    return h * jnp.tanh(h) + h


def _segment_major(rows, inverse=False):
    seg = rows // SUBLANES
    r = lax.broadcasted_iota(jnp.int32, (rows, rows), 0)
    c = lax.broadcasted_iota(jnp.int32, (rows, rows), 1)
    if inverse:
        src = (r % seg) * SUBLANES + r // seg
    else:
        src = (r % SUBLANES) * seg + r // SUBLANES
    return jnp.where(c == src, 1.0, 0.0).astype(BF16)


def _norm_proj_kernel(segment_major, x_ref, g_ref, w_ref, o1_ref, o2_ref):
    x = x_ref[...]
    xn = (x * _rms_scale(x) * g_ref[...]).astype(BF16)
    if segment_major:
        xn = jnp.dot(_segment_major(xn.shape[0]), xn, preferred_element_type=F32).astype(BF16)
    r = jnp.dot(xn, w_ref[...], preferred_element_type=F32)
    n1 = o1_ref.shape[-1]
    o1_ref[...] = r[:, :n1]
    o2_ref[...] = r[:, n1:]


def norm_proj(x, g, w, tm, segment_major=False):
    m, d = x.shape
    n = w.shape[1]
    n1 = n // 2
    return pl.pallas_call(
        functools.partial(_norm_proj_kernel, segment_major),
        grid=(m // tm,),
        in_specs=[
            pl.BlockSpec((tm, d), lambda i: (i, 0)),
            _resident((1, d)),
            _resident(w.shape),
        ],
        out_specs=[
            pl.BlockSpec((tm, n1), lambda i: (i, 0)),
            pl.BlockSpec((tm, n - n1), lambda i: (i, 0)),
        ],
        out_shape=[
            jax.ShapeDtypeStruct((m, n1), F32),
            jax.ShapeDtypeStruct((m, n - n1), F32),
        ],
        compiler_params=_params("parallel"),
        name="norm_proj",
    )(x, g.reshape(1, d), w)


def _dup_heads(x):
    low = lax.broadcasted_iota(jnp.int32, (x.shape[0], LANES), 1) < HEAD_DIM
    out = []
    for c in range(x.shape[1] // LANES):
        col = x[:, c * LANES:(c + 1) * LANES]
        swapped = pltpu.roll(col, HEAD_DIM, axis=1)
        out += [jnp.where(low, col, swapped), jnp.where(low, swapped, col)]
    return jnp.concatenate(out, axis=1)


def _norm_proj_kvq_kernel(banded, x_ref, gkv_ref, gq_ref, wkv_ref, wqg_ref, q_ref, gate_ref, *kv_refs):
    x = x_ref[...]
    xh = x * _rms_scale(x)
    kv = jnp.dot((xh * gkv_ref[...]).astype(BF16), wkv_ref[...], preferred_element_type=F32)
    qg = jnp.dot((xh * gq_ref[...]).astype(BF16), wqg_ref[...], preferred_element_type=F32)
    k = kv[:, :KV_WIDTH]
    v = kv[:, KV_WIDTH:]
    q_ref[...] = (qg[:, :ATT_WIDTH] * (1.0 / math.sqrt(HEAD_DIM))).astype(q_ref.dtype)
    gate_ref[...] = qg[:, ATT_WIDTH:]
    if banded:
        kdup_ref, vt_ref, ktail_ref, vtail_ref = kv_refs
        kdup_ref[...] = _dup_heads(k).astype(BF16)
        vt_ref[...] = _dup_heads(v).T.astype(BF16)
        tm = x.shape[0]
        ktail_ref[0] = k[tm - WINDOW:].T
        vtail_ref[0] = v[tm - WINDOW:].T
    else:
        k_ref, v_ref = kv_refs
        k_ref[...] = k
        v_ref[...] = v


def norm_proj_kvq(x, g_kv, g_q, w_kv, w_qg, tm, seq_len=None):
    m, d = x.shape
    banded = seq_len is not None
    row = lambda n: pl.BlockSpec((tm, n), lambda i: (i, 0))
    out_specs = [row(ATT_WIDTH), row(ATT_WIDTH)]
    out_shape = [
        jax.ShapeDtypeStruct((m, ATT_WIDTH), BF16 if banded else F32),
        jax.ShapeDtypeStruct((m, ATT_WIDTH), F32),
    ]
    if banded:
        assert seq_len % tm == 0 and tm >= WINDOW
        tiles = seq_len // tm
        tail = pl.BlockSpec((1, KV_WIDTH, WINDOW), lambda i: (i // tiles, 0, 0))
        out_specs += [row(2 * KV_WIDTH), pl.BlockSpec((2 * KV_WIDTH, tm), lambda i: (0, i)), tail, tail]
        out_shape += [jax.ShapeDtypeStruct((m, 2 * KV_WIDTH), BF16),
                      jax.ShapeDtypeStruct((2 * KV_WIDTH, m), BF16),
                      jax.ShapeDtypeStruct((m // seq_len, KV_WIDTH, WINDOW), F32),
                      jax.ShapeDtypeStruct((m // seq_len, KV_WIDTH, WINDOW), F32)]
    else:
        out_specs += [row(KV_WIDTH), row(KV_WIDTH)]
        out_shape += [jax.ShapeDtypeStruct((m, KV_WIDTH), F32)] * 2
    return pl.pallas_call(
        functools.partial(_norm_proj_kvq_kernel, banded),
        grid=(m // tm,),
        in_specs=[
            row(d),
            _resident((1, d)),
            _resident((1, d)),
            _resident(w_kv.shape),
            _resident(w_qg.shape),
        ],
        out_specs=out_specs,
        out_shape=out_shape,
        compiler_params=_params("arbitrary"),
        name="norm_proj_kvq",
    )(x, g_kv.reshape(1, d), g_q.reshape(1, d), w_kv, w_qg)


def _proj_norm_res_kernel(gated, *refs):
    if gated:
        a_ref, gate_ref, w_ref, g_ref, x_ref, o_ref = refs
        y = (a_ref[...] * _silu(gate_ref[...])).astype(BF16)
    else:
        a_ref, w_ref, g_ref, x_ref, o_ref = refs
        y = a_ref[...]
    o = jnp.dot(y, w_ref[...], preferred_element_type=F32)
    o_ref[...] = x_ref[...] + o * _rms_scale(o) * g_ref[...]


def proj_norm_res(a, gate, w, g, x, tm):
    k, d = w.shape
    m = x.shape[0]
    row = lambda n: pl.BlockSpec((tm, n), lambda i: (i, 0))
    gated = gate is not None
    ins = [a] + ([gate] if gated else []) + [w, g.reshape(1, d), x]
    in_specs = [row(k)] + ([row(k)] if gated else []) + [_resident(w.shape), _resident((1, d)), row(d)]
    return pl.pallas_call(
        functools.partial(_proj_norm_res_kernel, gated),
        grid=(m // tm,),
        in_specs=in_specs,
        out_specs=row(d),
        out_shape=jax.ShapeDtypeStruct((m, d), F32),
        compiler_params=_params("parallel"),
        name="proj_norm_res",
    )(*ins)


def _lru_gates(conv, wr_half, br, wi_half, bi, lam):
    cb = conv.astype(BF16)
    th_r = jnp.tanh(jnp.dot(cb, wr_half, preferred_element_type=F32) + 0.5 * br)
    th_i = jnp.tanh(jnp.dot(cb, wi_half, preferred_element_type=F32) + 0.5 * bi)
    nl = -lam
    softplus = jnp.maximum(nl, 0.0) + jnp.log1p(jnp.exp(-jnp.abs(nl)))
    half = (0.5 * LRU_C) * softplus
    x = th_r * half + half
    a = jnp.exp2(x * -LOG2_E)
    z = jnp.tanh(x) * (a * a + 1.0)
    mult = z * lax.rsqrt(jnp.maximum(z, 1e-30))
    hc = 0.5 * conv
    return a, mult * (hc * th_i + hc)


def _rglru_scan_kernel(u_ref, gate_ref, cprev_ref, h0_ref, cw_ref, cb_ref, wr_ref, br_ref,
                       wi_ref, bi_ref, lam_ref, y_ref, cnew_ref, hlast_ref, h_scr, tail_scr):
    t = pl.program_id(1)
    tc = u_ref.shape[0]
    seg = tc // SUBLANES
    ntaps = CONV_W - 1
    bw = LRU_BLOCK_W

    @pl.when(t == 0)
    def _():
        h_scr[...] = h0_ref[0]
        tail_scr[...] = cprev_ref[0]

    sub = lax.broadcasted_iota(jnp.int32, (SUBLANES, bw), 0)
    first = sub == 0
    time_order = _segment_major(tc, inverse=True)

    def shift_in(x, row0):
        return jnp.where(first, row0, pltpu.roll(x, 1, axis=0))

    def group(x, j):
        return x[j * SUBLANES:(j + 1) * SUBLANES]

    for n in range(LRU_BLOCKS):
        cs = slice(n * bw, (n + 1) * bw)
        u = u_ref[:, cs]
        tail = tail_scr[:, cs]
        before = [shift_in(group(u, seg - m), tail[ntaps - m:ntaps - m + 1])
                  for m in range(ntaps, 0, -1)]
        ext = jnp.concatenate(before + [u], axis=0)
        tail_scr[:, cs] = jnp.concatenate(
            [group(u, seg - m)[SUBLANES - 1:] for m in range(ntaps, 0, -1)], axis=0)
        cw = cw_ref[:, cs]
        conv = cb_ref[:, cs]
        for tap in range(CONV_W):
            conv = conv + ext[tap * SUBLANES:tap * SUBLANES + tc] * cw[tap:tap + 1]

        a, b = _lru_gates(conv, wr_ref[n], br_ref[:, cs], wi_ref[n], bi_ref[:, cs], lam_ref[:, cs])

        h = b[:SUBLANES]
        acc = a[:SUBLANES]
        h_loc, a_cum = [h], [acc]
        for j in range(1, seg):
            sl = slice(j * SUBLANES, (j + 1) * SUBLANES)
            h = a[sl] * h + b[sl]
            acc = a[sl] * acc
            h_loc.append(h)
            a_cum.append(acc)

        step = 1
        while step < SUBLANES:
            keep = sub >= step
            h = jnp.where(keep, acc * pltpu.roll(h, step, axis=0) + h, h)
            acc = jnp.where(keep, acc * pltpu.roll(acc, step, axis=0), acc)
            step *= 2
        h_prev = h_scr[:, cs]
        after = h + acc * h_prev
        h_in = shift_in(after, h_prev)
        h_scr[:, cs] = after[SUBLANES - 1:]

        hs = jnp.concatenate([h_loc[j] + a_cum[j] * h_in for j in range(seg)], axis=0)
        y = (hs * _silu(gate_ref[:, cs])).astype(BF16)
        y_ref[:, cs] = jnp.dot(time_order, y, preferred_element_type=F32).astype(y_ref.dtype)

    @pl.when(t == pl.num_programs(1) - 1)
    def _():
        hlast_ref[0] = h_scr[...]
        cnew_ref[0] = tail_scr[...]


def rglru_scan(u, gate, conv_prev, h0, conv_w, conv_b, w_r, b_r, w_i, b_i, lam, seq_len, tc):
    m, w = u.shape
    bsz = m // seq_len
    chunks = seq_len // tc
    assert seq_len % tc == 0 and tc % (SUBLANES * SUBLANES) == 0 and tc // SUBLANES > CONV_W
    seq = pl.BlockSpec((tc, w), lambda b, s: (b * chunks + s, 0))
    state = lambda rows: pl.BlockSpec((1, rows, w), lambda b, s: (b, 0, 0))
    return pl.pallas_call(
        _rglru_scan_kernel,
        grid=(bsz, chunks),
        in_specs=[seq, seq, state(CONV_W - 1), state(1), _resident((CONV_W, w)), _resident((1, w)),
                  _resident(w_r.shape), _resident((1, w)), _resident(w_i.shape), _resident((1, w)),
                  _resident((1, w))],
        out_specs=[seq, state(CONV_W - 1), state(1)],
        out_shape=[
            jax.ShapeDtypeStruct((m, w), BF16),
            jax.ShapeDtypeStruct((bsz, CONV_W - 1, w), F32),
            jax.ShapeDtypeStruct((bsz, 1, w), F32),
        ],
        scratch_shapes=[pltpu.VMEM((1, w), F32), pltpu.VMEM((CONV_W - 1, w), F32)],
        compiler_params=_params("parallel", "arbitrary"),
        name="rglru_scan",
    )(u, gate, conv_prev, h0.reshape(bsz, 1, w), conv_w, conv_b.reshape(1, w),
      w_r, b_r.reshape(1, w), w_i, b_i.reshape(1, w), lam.reshape(1, w))


def _rglru_step_kernel(u_ref, cprev_ref, h0_ref, cw_ref, cb_ref, wr_ref, br_ref, wi_ref, bi_ref,
                       lam_ref, h_ref, cnew_ref):
    u = u_ref[...]
    cw = cw_ref[...]
    conv = cb_ref[...]
    for tap in range(CONV_W - 1):
        conv = conv + cprev_ref[tap] * cw[tap:tap + 1]
        if tap > 0:
            cnew_ref[tap - 1] = cprev_ref[tap]
    conv = conv + u * cw[CONV_W - 1:]
    cnew_ref[CONV_W - 2] = u
    a, b = _lru_gates(conv, wr_ref[0], br_ref[...], wi_ref[0], bi_ref[...], lam_ref[...])
    h_ref[...] = a * h0_ref[...] + b


def rglru_step(u, conv_prev_t, h0, conv_w, conv_b, w_r, b_r, w_i, b_i, lam):
    bsz, w = u.shape
    bw = LRU_BLOCK_W
    rows = pl.BlockSpec((bsz, bw), lambda n: (0, n))
    taps = pl.BlockSpec((CONV_W - 1, bsz, bw), lambda n: (0, 0, n))
    chan = lambda r: pl.BlockSpec((r, bw), lambda n: (0, n))
    blockw = pl.BlockSpec((1, bw, bw), lambda n: (n, 0, 0))
    return pl.pallas_call(
        _rglru_step_kernel,
        grid=(LRU_BLOCKS,),
        in_specs=[rows, taps, rows, chan(CONV_W), chan(1), blockw, chan(1), blockw, chan(1), chan(1)],
        out_specs=[rows, taps],
        out_shape=[
            jax.ShapeDtypeStruct((bsz, w), F32),
            jax.ShapeDtypeStruct((CONV_W - 1, bsz, w), F32),
        ],
        compiler_params=_params("parallel"),
        name="rglru_step",
    )(u, conv_prev_t, h0, conv_w, conv_b.reshape(1, w), w_r, b_r.reshape(1, w),
      w_i, b_i.reshape(1, w), lam.reshape(1, w))


def _buckets(dist):
    n = jnp.maximum(dist, 0)
    max_exact = N_BUCKETS // 2
    nf = jnp.maximum(n, 1).astype(F32)
    large = max_exact + (jnp.log(nf / max_exact) / math.log(MAX_DISTANCE / max_exact)
                         * (N_BUCKETS - max_exact)).astype(jnp.int32)
    large = jnp.minimum(large, N_BUCKETS - 1)
    return jnp.where(n < max_exact, n, large)


def _lookup(bucket, valid, table_ref, head):
    bias = jnp.zeros(bucket.shape, F32)
    for b in range(N_BUCKETS):
        bias = jnp.where(bucket == b, table_ref[b, head], bias)
    return jnp.where(valid, bias, NEG_INF)


def _bias_kernel(table_ref, sinks_ref, band_ref, sinkt_ref, past_ref, new_ref):
    hk = pl.program_id(0)
    kj = lax.broadcasted_iota(jnp.int32, (2 * BLOCK, BLOCK), 0)
    qi = lax.broadcasted_iota(jnp.int32, (2 * BLOCK, BLOCK), 1)
    dist = qi + BLOCK - kj
    bucket = _buckets(dist)
    in_window = (dist >= 0) & (dist < WINDOW)
    rows = past_ref.shape[2]
    d_past = rows - lax.broadcasted_iota(jnp.int32, (1, rows), 1)
    b_past = _buckets(d_past)
    ok_past = (d_past >= 0) & (d_past < WINDOW)
    d_new = jnp.zeros((1, LANES), jnp.int32)
    b_new = _buckets(d_new)
    for par in range(HEADS_PER_TILE):
        for slab in range(SLABS):
            g = slab * HEADS_PER_TILE + par
            head = hk * GROUP + g
            rs = slice(par * 2 * BLOCK, (par + 1) * 2 * BLOCK)
            cs = slice(slab * BLOCK, (slab + 1) * BLOCK)
            band_ref[0, 0, rs, cs] = _lookup(bucket, in_window, table_ref, head)
            band_ref[1, 0, rs, cs] = _lookup(bucket, in_window & (kj >= BLOCK), table_ref, head)
            sinkt_ref[0, par, :, cs] = jnp.full((1, BLOCK), sinks_ref[head], F32)
            past_ref[0, g:g + 1, :] = _lookup(b_past, ok_past, table_ref, head)
            new_ref[0, g:g + 1, :] = _lookup(b_new, d_new == 0, table_ref, head)


def bias_tables(table, sinks, past_rows):
    smem = pl.BlockSpec(memory_space=pltpu.SMEM)
    return pl.pallas_call(
        _bias_kernel,
        grid=(N_KV_HEADS,),
        in_specs=[smem, smem],
        out_specs=[
            pl.BlockSpec((2, 1, HEADS_PER_TILE * 2 * BLOCK, SLABS * BLOCK), lambda h: (0, h, 0, 0)),
            pl.BlockSpec((1, HEADS_PER_TILE, 1, SLABS * BLOCK), lambda h: (h, 0, 0, 0)),
            pl.BlockSpec((1, GROUP, past_rows), lambda h: (h, 0, 0)),
            pl.BlockSpec((1, GROUP, LANES), lambda h: (h, 0, 0)),
        ],
        out_shape=[
            jax.ShapeDtypeStruct((2, N_KV_HEADS, HEADS_PER_TILE * 2 * BLOCK, SLABS * BLOCK), F32),
            jax.ShapeDtypeStruct((N_KV_HEADS, HEADS_PER_TILE, 1, SLABS * BLOCK), F32),
            jax.ShapeDtypeStruct((N_KV_HEADS, GROUP, past_rows), F32),
            jax.ShapeDtypeStruct((N_KV_HEADS, GROUP, LANES), F32),
        ],
        compiler_params=_params("parallel"),
        name="bias_tables",
    )(table, sinks)


def _band_attn_kernel(q_ref, kp_ref, kc_ref, vp_ref, vc_ref, gate_ref, bias_ref, sink_ref, y_ref,
                      s_scr, p_scr):
    first = (pl.program_id(1) == 0).astype(jnp.int32)
    nt = (((1,), (1,)), ((), ()))
    low = (lax.broadcasted_iota(jnp.int32, (1, LANES), 1) < HEAD_DIM)
    keep_low = low.astype(BF16)
    keep_high = 1 - keep_low
    nkeys = 2 * BLOCK
    zeros_v = jnp.zeros((HEAD_DIM, nkeys), BF16)
    rows = ATT_ROWS

    def scores(hk):
        cs = slice(hk * LANES, (hk + 1) * LANES)
        kd = jnp.concatenate([kp_ref[:, cs], kc_ref[:, cs]], axis=0)
        lhs = jnp.concatenate([kd * keep_low, kd * keep_high], axis=0)
        qs = jnp.concatenate([q_ref[:, (hk * SLABS + s) * LANES:(hk * SLABS + s + 1) * LANES]
                              for s in range(SLABS)], axis=0)
        s_scr[hk % 2] = lax.dot_general(lhs, qs, nt, preferred_element_type=F32)

    scores(0)
    for hk in range(N_KV_HEADS):
        slot = hk % 2
        if hk + 1 < N_KV_HEADS:
            scores(hk + 1)
        inv = []
        for par in range(HEADS_PER_TILE):
            base = par * nkeys
            sink = sink_ref[hk, par]
            top = None
            for r in range(base, base + nkeys, rows):
                sb = s_scr[slot, r:r + rows, :] + bias_ref[first, hk, r:r + rows, :]
                s_scr[slot, r:r + rows, :] = sb
                top = sb if top is None else jnp.maximum(top, sb)
            m = jnp.maximum(jnp.max(top, axis=0, keepdims=True), sink)
            tot = None
            for r in range(base, base + nkeys, rows):
                p = jnp.exp(s_scr[slot, r:r + rows, :] - m)
                p_scr[slot, r:r + rows, :] = p.astype(BF16)
                tot = p if tot is None else tot + p
            inv.append(1.0 / (jnp.sum(tot, axis=0, keepdims=True) + jnp.exp(sink - m)))
        cs = slice(hk * LANES, (hk + 1) * LANES)
        vt = jnp.concatenate([vp_ref[cs, :], vc_ref[cs, :]], axis=1)
        lhs_v = jnp.concatenate([
            jnp.concatenate([vt[:HEAD_DIM], zeros_v], axis=0),
            jnp.concatenate([zeros_v, vt[HEAD_DIM:]], axis=0)], axis=1)
        ot = jnp.dot(lhs_v, p_scr[slot], preferred_element_type=F32)
        ot = jnp.concatenate([ot[:HEAD_DIM] * inv[0], ot[HEAD_DIM:] * inv[1]], axis=0)
        o = ot.T
        for sl in range(SLABS):
            c0 = (hk * SLABS + sl) * LANES
            y_ref[:, c0:c0 + LANES] = (o[sl * BLOCK:(sl + 1) * BLOCK]
                                       * _silu(gate_ref[:, c0:c0 + LANES])).astype(y_ref.dtype)


def band_attention(q, kdup, vt, gate, bias_band, sink_t, bsz, t):
    m = q.shape[0]
    nblk = t // BLOCK
    cur = lambda n: pl.BlockSpec((BLOCK, n), lambda b, i: (b * nblk + i, 0))
    prev = lambda n: pl.BlockSpec((BLOCK, n), lambda b, i: (b * nblk + jnp.maximum(i - 1, 0), 0))
    cur_t = pl.BlockSpec((2 * KV_WIDTH, BLOCK), lambda b, i: (0, b * nblk + i))
    prev_t = pl.BlockSpec((2 * KV_WIDTH, BLOCK), lambda b, i: (0, b * nblk + jnp.maximum(i - 1, 0)))
    score_tile = (HEADS_PER_TILE * 2 * BLOCK, SLABS * BLOCK)
    return pl.pallas_call(
        _band_attn_kernel,
        grid=(bsz, nblk),
        in_specs=[
            cur(ATT_WIDTH), prev(2 * KV_WIDTH), cur(2 * KV_WIDTH), prev_t, cur_t,
            cur(ATT_WIDTH), _resident(bias_band.shape), _resident(sink_t.shape),
        ],
        out_specs=cur(ATT_WIDTH),
        out_shape=jax.ShapeDtypeStruct((m, ATT_WIDTH), BF16),
        scratch_shapes=[pltpu.VMEM((2,) + score_tile, F32), pltpu.VMEM((2,) + score_tile, BF16)],
        compiler_params=_params("parallel", "parallel"),
        name="band_attention",
    )(q, kdup, kdup, vt, vt, gate, bias_band, sink_t)


def _cached_attn_kernel(q_ref, ckt_ref, cvt_ref, kn_ref, vn_ref, sinks_ref, bpast_ref, bnew_ref, o_ref):
    q = q_ref[0]
    qt = jnp.concatenate([q] * N_KV_HEADS, axis=1)
    lane_kv = lax.broadcasted_iota(jnp.int32, qt.shape, 1) // HEAD_DIM
    row_kv = lax.broadcasted_iota(jnp.int32, qt.shape, 0) // GROUP
    own = lane_kv == row_kv
    qm = jnp.where(own, qt, 0.0).astype(BF16)
    knew = kn_ref[0].astype(BF16).astype(F32)
    vnew = vn_ref[0].astype(BF16).astype(F32)
    s = jnp.dot(qm, ckt_ref[0].astype(BF16), preferred_element_type=F32) + bpast_ref[...]
    s_new = jnp.sum(qm.astype(F32) * knew, axis=-1, keepdims=True) + bnew_ref[:, :1]
    sink = sinks_ref[...]
    m = jnp.maximum(jnp.maximum(jnp.max(s, axis=-1, keepdims=True), s_new), sink)
    p = jnp.exp(s - m)
    p_new = jnp.exp(s_new - m)
    denom = jnp.sum(p, axis=-1, keepdims=True) + p_new + jnp.exp(sink - m)
    nt = (((1,), (1,)), ((), ()))
    o_all = (lax.dot_general(p.astype(BF16), cvt_ref[0].astype(BF16), nt, preferred_element_type=F32)
             + p_new.astype(BF16).astype(F32) * vnew)
    o_all = jnp.where(own, o_all, 0.0)
    o = o_all[:, :HEAD_DIM]
    for hk in range(1, N_KV_HEADS):
        o = o + o_all[:, hk * HEAD_DIM:(hk + 1) * HEAD_DIM]
    o_ref[0] = o / denom


def cached_attention(q, cache_kt, cache_vt, k_new, v_new, sinks, bias_past, bias_new):
    bsz, _, rows = cache_kt.shape
    per_seq = lambda r, n: pl.BlockSpec((1, r, n), lambda b: (b, 0, 0))
    return pl.pallas_call(
        _cached_attn_kernel,
        grid=(bsz,),
        in_specs=[
            per_seq(N_Q_HEADS, HEAD_DIM), per_seq(KV_WIDTH, rows), per_seq(KV_WIDTH, rows),
            per_seq(1, KV_WIDTH), per_seq(1, KV_WIDTH),
            _resident((N_Q_HEADS, 1)), _resident((N_Q_HEADS, rows)), _resident((N_Q_HEADS, LANES)),
        ],
        out_specs=per_seq(N_Q_HEADS, HEAD_DIM),
        out_shape=jax.ShapeDtypeStruct((bsz, N_Q_HEADS, HEAD_DIM), F32),
        compiler_params=_params("parallel"),
        name="cached_attention",
    )(q, cache_kt, cache_vt, k_new, v_new, sinks, bias_past, bias_new)


def kernel(x_prompt, x_sample, state_conv, state_h, cache_k, cache_v, a_norm_pre, a_norm_post,
           a_w_in, a_conv_w, a_conv_b, a_w_r, a_b_r, a_w_i, a_b_i, a_lambda, a_w_out, kv_norm, w_kv,
           b_norm_pre, b_norm_post, b_w_qg, b_sinks, b_w_out, rel_bias_table):
    bsz, t, d = x_prompt.shape
    dbsz, dt, _ = x_sample.shape
    assert a_w_in.shape[0] == 1 and b_w_qg.shape[0] == 1 and dt == 1
    assert t % BLOCK == 0 and t >= WINDOW
    past_rows = cache_k.shape[1]
    assert past_rows == min(WINDOW, PAST_LEN)

    w_in = a_w_in[0].astype(BF16)
    w_r = (0.5 * a_w_r[0]).astype(BF16)
    w_i = (0.5 * a_w_i[0]).astype(BF16)
    w_aout = a_w_out[0].astype(BF16)
    w_kvb = w_kv.astype(BF16)
    w_qg = b_w_qg[0].astype(BF16)
    w_bout = b_w_out[0].astype(BF16)
    sinks = b_sinks[0]

    bias_band, sink_t, bias_past, bias_new = bias_tables(rel_bias_table, sinks, past_rows)
    lru = (a_conv_w[0], a_conv_b[0], w_r, a_b_r[0], w_i, a_b_i[0], a_lambda[0])

    tm = 256
    xp = x_prompt.reshape(bsz * t, d)
    u, gate = norm_proj(xp, a_norm_pre[0], w_in, tm, segment_major=True)
    conv0 = jnp.zeros((bsz, CONV_W - 1, LRU_WIDTH), F32)
    h0 = jnp.zeros((bsz, LRU_WIDTH), F32)
    y, p_conv, p_h = rglru_scan(u, gate, conv0, h0, *lru, seq_len=t, tc=256)
    x1 = proj_norm_res(y, None, w_aout, a_norm_post[0], xp, tm)
    q, gate_b, kdup, vt, k_tail, v_tail = norm_proj_kvq(x1, kv_norm, b_norm_pre[0], w_kvb, w_qg, tm, seq_len=t)
    yb = band_attention(q, kdup, vt, gate_b, bias_band, sink_t, bsz, t)
    y_prompt = proj_norm_res(yb, None, w_bout, b_norm_post[0], x1, tm).reshape(bsz, t, d)
    p_k = jnp.transpose(k_tail.reshape(bsz, N_KV_HEADS, HEAD_DIM, WINDOW), (0, 3, 1, 2))
    p_v = jnp.transpose(v_tail.reshape(bsz, N_KV_HEADS, HEAD_DIM, WINDOW), (0, 3, 1, 2))

    xs = x_sample.reshape(dbsz, d)
    us, gate_s = norm_proj(xs, a_norm_pre[0], w_in, dbsz)
    hs, s_conv_t = rglru_step(us, jnp.transpose(state_conv[0], (1, 0, 2)), state_h[0], *lru)
    xs1 = proj_norm_res(hs, gate_s, w_aout, a_norm_post[0], xs, dbsz)
    qs, gate_sb, ks, vs = norm_proj_kvq(xs1, kv_norm, b_norm_pre[0], w_kvb, w_qg, dbsz)
    cache_kt = jnp.transpose(cache_k, (0, 2, 3, 1)).reshape(dbsz, KV_WIDTH, past_rows)
    cache_vt = jnp.transpose(cache_v, (0, 2, 3, 1)).reshape(dbsz, KV_WIDTH, past_rows)
    os_ = cached_attention(qs.reshape(dbsz, N_Q_HEADS, HEAD_DIM), cache_kt, cache_vt,
                           ks.reshape(dbsz, 1, KV_WIDTH), vs.reshape(dbsz, 1, KV_WIDTH),
                           sinks.reshape(N_Q_HEADS, 1), bias_past.reshape(N_Q_HEADS, past_rows),
                           bias_new.reshape(N_Q_HEADS, LANES))
    y_sample = proj_norm_res(os_.reshape(dbsz, ATT_WIDTH), gate_sb, w_bout, b_norm_post[0], xs1, dbsz)

    return (y_prompt, y_sample.reshape(dbsz, 1, d),
            p_conv[None], p_h.reshape(1, bsz, LRU_WIDTH), p_k, p_v,
            jnp.transpose(s_conv_t, (1, 0, 2))[None], hs[None],
            ks.reshape(dbsz, 1, N_KV_HEADS, HEAD_DIM), vs.reshape(dbsz, 1, N_KV_HEADS, HEAD_DIM))
```

```python
import functools
import math

import jax
import jax.numpy as jnp
from jax import lax
from jax.experimental import pallas as pl
from jax.experimental.pallas import tpu as pltpu

F32 = jnp.float32
BF16 = jnp.bfloat16

D_MODEL = 2048
LRU_WIDTH = 2048
LRU_BLOCKS = 8
LRU_BLOCK_W = LRU_WIDTH // LRU_BLOCKS
CONV_W = 4
LRU_C = 8.0
HEAD_DIM = 64
N_Q_HEADS = 32
N_KV_HEADS = 8
GROUP = N_Q_HEADS // N_KV_HEADS
ATT_WIDTH = N_Q_HEADS * HEAD_DIM
KV_WIDTH = N_KV_HEADS * HEAD_DIM
WINDOW = 128
BLOCK = WINDOW
N_BUCKETS = 32
MAX_DISTANCE = 128
RMS_EPS = 1e-6
NEG_INF = -1e30
LOG2_E = 1.4426950408889634
PAST_LEN = 16384

V7X_VMEM_BYTES = 64 * 1024 * 1024
VMEM_LIMIT = V7X_VMEM_BYTES - 8 * 1024 * 1024
SUBLANES = 8
LANES = 128
HEADS_PER_TILE = LANES // HEAD_DIM
SLABS = GROUP // HEADS_PER_TILE
ATT_ROWS = 32
SUB_ROWS = 256
W_CHUNK = 512
SEQS_PER_STEP = 8


def _params(*semantics):
    return pltpu.CompilerParams(dimension_semantics=semantics, vmem_limit_bytes=VMEM_LIMIT)


def _resident(shape):
    zeros = (0,) * len(shape)
    return pl.BlockSpec(shape, lambda *_: zeros, pipeline_mode=pl.Buffered(1))


def _rms_scale(x):
    return lax.rsqrt(jnp.mean(x * x, axis=-1, keepdims=True) + RMS_EPS)


def _silu(x):
    h = 0.5 * x
    return h * jnp.tanh(h) + h


def _segment_major(rows, inverse=False):
    seg = rows // SUBLANES
    r = lax.broadcasted_iota(jnp.int32, (rows, rows), 0)
    c = lax.broadcasted_iota(jnp.int32, (rows, rows), 1)
    if inverse:
        src = (r % seg) * SUBLANES + r // seg
    else:
        src = (r % SUBLANES) * seg + r // SUBLANES
    return jnp.where(c == src, 1.0, 0.0).astype(BF16)


def _phase_specs(nchunk, tm, k):
    chunk_w = pl.BlockSpec((k, W_CHUNK), lambda i: (0, jnp.minimum(i, nchunk - 1)))
    chunk_o = lambda rows: pl.BlockSpec((rows, W_CHUNK), lambda i: (0, jnp.minimum(i, nchunk - 1)))
    tile = lambda n: pl.BlockSpec((tm, n), lambda i: (jnp.maximum(i - nchunk, 0), 0))
    return chunk_w, chunk_o, tile


def _norm_proj_kernel(nchunk, xs_ref, x_ref, g_ref, w_ref, os_ref, o1_ref, o2_ref, w_scr):
    i = pl.program_id(0)

    @pl.when(i < nchunk)
    def _():
        wb = w_ref[...].astype(BF16)
        w_scr[i] = wb
        xs = xs_ref[...]
        xsn = (xs * _rms_scale(xs) * g_ref[...]).astype(BF16)
        os_ref[...] = jnp.dot(xsn, wb, preferred_element_type=F32)

    @pl.when(i >= nchunk)
    def _():
        half = nchunk // 2
        for rs in _row_blocks(x_ref.shape[0]):
            x = x_ref[rs, :]
            xn = (x * _rms_scale(x) * g_ref[...]).astype(BF16)
            xn = jnp.dot(_segment_major(xn.shape[0]), xn, preferred_element_type=F32).astype(BF16)
            for c in range(nchunk):
                dst = o1_ref if c < half else o2_ref
                cs = slice((c % half) * W_CHUNK, (c % half + 1) * W_CHUNK)
                dst[rs, cs] = jnp.dot(xn, w_scr[c], preferred_element_type=F32)


def norm_proj(xs, x, g, w, tm):
    m, d = x.shape
    n = w.shape[1]
    nchunk = n // W_CHUNK
    chunk_w, chunk_o, tile = _phase_specs(nchunk, tm, d)
    return pl.pallas_call(
        functools.partial(_norm_proj_kernel, nchunk),
        grid=(nchunk + m // tm,),
        in_specs=[_resident(xs.shape), tile(d), _resident((1, d)), chunk_w],
        out_specs=[chunk_o(xs.shape[0]), tile(n // 2), tile(n // 2)],
        out_shape=[
            jax.ShapeDtypeStruct((xs.shape[0], n), F32),
            jax.ShapeDtypeStruct((m, n // 2), F32),
            jax.ShapeDtypeStruct((m, n // 2), F32),
        ],
        scratch_shapes=[pltpu.VMEM((nchunk, d, W_CHUNK), BF16)],
        compiler_params=_params("arbitrary"),
        name="norm_proj",
    )(xs, x, g.reshape(1, d), w)


def _dup_heads(x):
    low = lax.broadcasted_iota(jnp.int32, (x.shape[0], LANES), 1) < HEAD_DIM
    out = []
    for c in range(x.shape[1] // LANES):
        col = x[:, c * LANES:(c + 1) * LANES]
        swapped = pltpu.roll(col, HEAD_DIM, axis=1)
        out += [jnp.where(low, col, swapped), jnp.where(low, swapped, col)]
    return jnp.concatenate(out, axis=1)


def _norm_proj_kvq_kernel(nkv, nqg, xs_ref, x_ref, gkv_ref, gq_ref, wkv_ref, wqg_ref,
                          kvs_ref, qgs_ref, q_ref, gate_ref, kdup_ref, vt_ref, ktail_ref, vtail_ref,
                          wkv_scr, wqg_scr):
    i = pl.program_id(0)
    nchunk = nkv + nqg
    q_chunks = ATT_WIDTH // W_CHUNK
    q_scale = 1.0 / math.sqrt(HEAD_DIM)

    def sample_rows(g_ref):
        xs = xs_ref[...]
        return (xs * _rms_scale(xs) * g_ref[...]).astype(BF16)

    @pl.when(i < nkv)
    def _():
        wb = wkv_ref[...].astype(BF16)
        wkv_scr[i] = wb
        kvs_ref[...] = jnp.dot(sample_rows(gkv_ref), wb, preferred_element_type=F32)

    @pl.when((i >= nkv) & (i < nchunk))
    def _():
        c = i - nkv
        wb = wqg_ref[...].astype(BF16)
        wqg_scr[c] = wb
        r = jnp.dot(sample_rows(gq_ref), wb, preferred_element_type=F32)
        qgs_ref[...] = r * jnp.where(c < q_chunks, q_scale, 1.0)

    @pl.when(i >= nchunk)
    def _():
        tm = x_ref.shape[0]
        for rs in _row_blocks(tm):
            x = x_ref[rs, :]
            xh = x * _rms_scale(x)
            xkv = (xh * gkv_ref[...]).astype(BF16)
            xq = (xh * gq_ref[...]).astype(BF16)
            k = jnp.dot(xkv, wkv_scr[0], preferred_element_type=F32)
            v = jnp.dot(xkv, wkv_scr[1], preferred_element_type=F32)
            kdup_ref[rs, :] = _dup_heads(k).astype(BF16)
            vt_ref[:, rs] = _dup_heads(v).T.astype(BF16)
            for c in range(nqg):
                r = jnp.dot(xq, wqg_scr[c], preferred_element_type=F32)
                if c < q_chunks:
                    q_ref[rs, c * W_CHUNK:(c + 1) * W_CHUNK] = (r * q_scale).astype(q_ref.dtype)
                else:
                    cc = c - q_chunks
                    gate_ref[rs, cc * W_CHUNK:(cc + 1) * W_CHUNK] = r
        ktail_ref[0] = k[k.shape[0] - WINDOW:].T
        vtail_ref[0] = v[v.shape[0] - WINDOW:].T


def norm_proj_kvq(xs, x, g_kv, g_q, w_kv, w_qg, tm, seq_len):
    m, d = x.shape
    ns = xs.shape[0]
    assert w_kv.shape[1] == 2 * KV_WIDTH == 2 * W_CHUNK and seq_len % tm == 0 and tm >= WINDOW
    nkv, nqg = w_kv.shape[1] // W_CHUNK, w_qg.shape[1] // W_CHUNK
    nchunk = nkv + nqg
    tiles = seq_len // tm
    tile = lambda n: pl.BlockSpec((tm, n), lambda i: (jnp.maximum(i - nchunk, 0), 0))
    tail = pl.BlockSpec((1, KV_WIDTH, WINDOW), lambda i: (jnp.maximum(i - nchunk, 0) // tiles, 0, 0))
    kv_chunk = lambda i: (0, jnp.minimum(i, nkv - 1))
    qg_chunk = lambda i: (0, jnp.clip(i - nkv, 0, nqg - 1))
    return pl.pallas_call(
        functools.partial(_norm_proj_kvq_kernel, nkv, nqg),
        grid=(nchunk + m // tm,),
        in_specs=[
            _resident(xs.shape), tile(d), _resident((1, d)), _resident((1, d)),
            pl.BlockSpec((d, W_CHUNK), kv_chunk), pl.BlockSpec((d, W_CHUNK), qg_chunk),
        ],
        out_specs=[
            pl.BlockSpec((ns, W_CHUNK), kv_chunk), pl.BlockSpec((ns, W_CHUNK), qg_chunk),
            tile(ATT_WIDTH), tile(ATT_WIDTH), tile(2 * KV_WIDTH),
            pl.BlockSpec((2 * KV_WIDTH, tm), lambda i: (0, jnp.maximum(i - nchunk, 0))), tail, tail,
        ],
        out_shape=[
            jax.ShapeDtypeStruct((ns, w_kv.shape[1]), F32),
            jax.ShapeDtypeStruct((ns, w_qg.shape[1]), F32),
            jax.ShapeDtypeStruct((m, ATT_WIDTH), BF16),
            jax.ShapeDtypeStruct((m, ATT_WIDTH), F32),
            jax.ShapeDtypeStruct((m, 2 * KV_WIDTH), BF16),
            jax.ShapeDtypeStruct((2 * KV_WIDTH, m), BF16),
            jax.ShapeDtypeStruct((m // seq_len, KV_WIDTH, WINDOW), F32),
            jax.ShapeDtypeStruct((m // seq_len, KV_WIDTH, WINDOW), F32),
        ],
        scratch_shapes=[pltpu.VMEM((nkv, d, W_CHUNK), BF16), pltpu.VMEM((nqg, d, W_CHUNK), BF16)],
        compiler_params=_params("arbitrary"),
        name="norm_proj_kvq",
    )(xs, x, g_kv.reshape(1, d), g_q.reshape(1, d), w_kv, w_qg)


def _row_blocks(rows):
    sub = min(rows, SUB_ROWS)
    return [slice(r, r + sub) for r in range(0, rows, sub)]


def _proj_norm_res_kernel(nchunk, as_ref, gs_ref, xs_ref, y_ref, w_ref, g_ref, x_ref, os_ref, o_ref,
                          w_scr, raw_scr):
    i = pl.program_id(0)

    @pl.when(i < nchunk)
    def _():
        wb = w_ref[...].astype(BF16)
        w_scr[i] = wb
        ys = (as_ref[...] * _silu(gs_ref[...])).astype(BF16)
        raw_scr[i] = jnp.dot(ys, wb, preferred_element_type=F32)

    @pl.when(i == nchunk - 1)
    def _():
        o = jnp.concatenate([raw_scr[c] for c in range(nchunk)], axis=1)
        os_ref[...] = xs_ref[...] + o * _rms_scale(o) * g_ref[...]

    @pl.when(i >= nchunk)
    def _():
        for rs in _row_blocks(x_ref.shape[0]):
            y = y_ref[rs, :]
            o = jnp.concatenate([jnp.dot(y, w_scr[c], preferred_element_type=F32)
                                 for c in range(nchunk)], axis=1)
            o_ref[rs, :] = x_ref[rs, :] + o * _rms_scale(o) * g_ref[...]


def proj_norm_res(a_s, gate_s, x_s, y, w, g, x, tm):
    k, d = w.shape
    m = x.shape[0]
    nchunk = d // W_CHUNK
    chunk_w, _, tile = _phase_specs(nchunk, tm, k)
    return pl.pallas_call(
        functools.partial(_proj_norm_res_kernel, nchunk),
        grid=(nchunk + m // tm,),
        in_specs=[_resident(a_s.shape), _resident(gate_s.shape), _resident(x_s.shape),
                  tile(k), chunk_w, _resident((1, d)), tile(d)],
        out_specs=[pl.BlockSpec(x_s.shape, lambda i: (0, 0)), tile(d)],
        out_shape=[jax.ShapeDtypeStruct(x_s.shape, F32), jax.ShapeDtypeStruct((m, d), F32)],
        scratch_shapes=[pltpu.VMEM((nchunk, k, W_CHUNK), BF16),
                        pltpu.VMEM((nchunk, x_s.shape[0], W_CHUNK), F32)],
        compiler_params=_params("arbitrary"),
        name="proj_norm_res",
    )(a_s, gate_s, x_s, y, w, g.reshape(1, d), x)


def _lru_gates(conv, wr_half, br, wi_half, bi, lam):
    cb = conv.astype(BF16)
    th_r = jnp.tanh(jnp.dot(cb, wr_half, preferred_element_type=F32) + 0.5 * br)
    th_i = jnp.tanh(jnp.dot(cb, wi_half, preferred_element_type=F32) + 0.5 * bi)
    nl = -lam
    softplus = jnp.maximum(nl, 0.0) + jnp.log1p(jnp.exp(-jnp.abs(nl)))
    half = (0.5 * LRU_C) * softplus
    x = th_r * half + half
    a = jnp.exp2(x * -LOG2_E)
    z = jnp.tanh(x) * (a * a + 1.0)
    mult = z * lax.rsqrt(jnp.maximum(z, 1e-30))
    hc = 0.5 * conv
    return a, mult * (hc * th_i + hc)


def _rglru_scan_kernel(u_ref, gate_ref, cprev_ref, h0_ref, cw_ref, cb_ref, wr_ref, br_ref,
                       wi_ref, bi_ref, lam_ref, y_ref, cnew_ref, hlast_ref, h_scr, tail_scr):
    t = pl.program_id(1)
    tc = u_ref.shape[0]
    seg = tc // SUBLANES
    ntaps = CONV_W - 1
    bw = LRU_BLOCK_W

    @pl.when(t == 0)
    def _():
        h_scr[...] = h0_ref[0]
        tail_scr[...] = cprev_ref[0]

    sub = lax.broadcasted_iota(jnp.int32, (SUBLANES, bw), 0)
    first = sub == 0
    time_order = _segment_major(tc, inverse=True)

    def shift_in(x, row0):
        return jnp.where(first, row0, pltpu.roll(x, 1, axis=0))

    def group(x, j):
        return x[j * SUBLANES:(j + 1) * SUBLANES]

    for n in range(LRU_BLOCKS):
        cs = slice(n * bw, (n + 1) * bw)
        u = u_ref[:, cs]
        tail = tail_scr[:, cs]
        before = [shift_in(group(u, seg - m), tail[ntaps - m:ntaps - m + 1])
                  for m in range(ntaps, 0, -1)]
        ext = jnp.concatenate(before + [u], axis=0)
        tail_scr[:, cs] = jnp.concatenate(
            [group(u, seg - m)[SUBLANES - 1:] for m in range(ntaps, 0, -1)], axis=0)
        cw = cw_ref[:, cs]
        conv = cb_ref[:, cs]
        for tap in range(CONV_W):
            conv = conv + ext[tap * SUBLANES:tap * SUBLANES + tc] * cw[tap:tap + 1]

        a, b = _lru_gates(conv, wr_ref[n], br_ref[:, cs], wi_ref[n], bi_ref[:, cs], lam_ref[:, cs])

        h = b[:SUBLANES]
        acc = a[:SUBLANES]
        h_loc, a_cum = [h], [acc]
        for j in range(1, seg):
            sl = slice(j * SUBLANES, (j + 1) * SUBLANES)
            h = a[sl] * h + b[sl]
            acc = a[sl] * acc
            h_loc.append(h)
            a_cum.append(acc)

        step = 1
        while step < SUBLANES:
            keep = sub >= step
            h = jnp.where(keep, acc * pltpu.roll(h, step, axis=0) + h, h)
            acc = jnp.where(keep, acc * pltpu.roll(acc, step, axis=0), acc)
            step *= 2
        h_prev = h_scr[:, cs]
        after = h + acc * h_prev
        h_in = shift_in(after, h_prev)
        h_scr[:, cs] = after[SUBLANES - 1:]

        hs = jnp.concatenate([h_loc[j] + a_cum[j] * h_in for j in range(seg)], axis=0)
        y = (hs * _silu(gate_ref[:, cs])).astype(BF16)
        y_ref[:, cs] = jnp.dot(time_order, y, preferred_element_type=F32).astype(y_ref.dtype)

    @pl.when(t == pl.num_programs(1) - 1)
    def _():
        hlast_ref[0] = h_scr[...]
        cnew_ref[0] = tail_scr[...]


def rglru_scan(u, gate, conv_prev, h0, conv_w, conv_b, w_r, b_r, w_i, b_i, lam, seq_len, tc):
    m, w = u.shape
    bsz = m // seq_len
    chunks = seq_len // tc
    assert seq_len % tc == 0 and tc % (SUBLANES * SUBLANES) == 0 and tc // SUBLANES > CONV_W
    seq = pl.BlockSpec((tc, w), lambda b, s: (b * chunks + s, 0))
    state = lambda rows: pl.BlockSpec((1, rows, w), lambda b, s: (b, 0, 0))
    return pl.pallas_call(
        _rglru_scan_kernel,
        grid=(bsz, chunks),
        in_specs=[seq, seq, state(CONV_W - 1), state(1), _resident((CONV_W, w)), _resident((1, w)),
                  _resident(w_r.shape), _resident((1, w)), _resident(w_i.shape), _resident((1, w)),
                  _resident((1, w))],
        out_specs=[seq, state(CONV_W - 1), state(1)],
        out_shape=[
            jax.ShapeDtypeStruct((m, w), BF16),
            jax.ShapeDtypeStruct((bsz, CONV_W - 1, w), F32),
            jax.ShapeDtypeStruct((bsz, 1, w), F32),
        ],
        scratch_shapes=[pltpu.VMEM((1, w), F32), pltpu.VMEM((CONV_W - 1, w), F32)],
        compiler_params=_params("parallel", "arbitrary"),
        name="rglru_scan",
    )(u, gate, conv_prev, h0.reshape(bsz, 1, w), conv_w, conv_b.reshape(1, w),
      w_r, b_r.reshape(1, w), w_i, b_i.reshape(1, w), lam.reshape(1, w))


def _rglru_step_kernel(u_ref, cprev_ref, h0_ref, cw_ref, cb_ref, wr_ref, br_ref, wi_ref, bi_ref,
                       lam_ref, h_ref, cnew_ref):
    u = u_ref[...]
    cw = cw_ref[...]
    conv = cb_ref[...]
    for tap in range(CONV_W - 1):
        conv = conv + cprev_ref[tap] * cw[tap:tap + 1]
        if tap > 0:
            cnew_ref[tap - 1] = cprev_ref[tap]
    conv = conv + u * cw[CONV_W - 1:]
    cnew_ref[CONV_W - 2] = u
    a, b = _lru_gates(conv, wr_ref[0], br_ref[...], wi_ref[0], bi_ref[...], lam_ref[...])
    h_ref[...] = a * h0_ref[...] + b


def rglru_step(u, conv_prev_t, h0, conv_w, conv_b, w_r, b_r, w_i, b_i, lam):
    bsz, w = u.shape
    bw = LRU_BLOCK_W
    rows = pl.BlockSpec((bsz, bw), lambda n: (0, n))
    taps = pl.BlockSpec((CONV_W - 1, bsz, bw), lambda n: (0, 0, n))
    chan = lambda r: pl.BlockSpec((r, bw), lambda n: (0, n))
    blockw = pl.BlockSpec((1, bw, bw), lambda n: (n, 0, 0))
    return pl.pallas_call(
        _rglru_step_kernel,
        grid=(LRU_BLOCKS,),
        in_specs=[rows, taps, rows, chan(CONV_W), chan(1), blockw, chan(1), blockw, chan(1), chan(1)],
        out_specs=[rows, taps],
        out_shape=[
            jax.ShapeDtypeStruct((bsz, w), F32),
            jax.ShapeDtypeStruct((CONV_W - 1, bsz, w), F32),
        ],
        compiler_params=_params("parallel"),
        name="rglru_step",
    )(u, conv_prev_t, h0, conv_w, conv_b.reshape(1, w), w_r, b_r.reshape(1, w),
      w_i, b_i.reshape(1, w), lam.reshape(1, w))


def _buckets(dist):
    n = jnp.maximum(dist, 0)
    max_exact = N_BUCKETS // 2
    nf = jnp.maximum(n, 1).astype(F32)
    large = max_exact + jnp.floor(jnp.log(nf / max_exact) / math.log(MAX_DISTANCE / max_exact)
                                  * (N_BUCKETS - max_exact)).astype(jnp.int32)
    large = jnp.minimum(large, N_BUCKETS - 1)
    return jnp.where(n < max_exact, n, large)


def _lookup(bucket, valid, table_ref, head):
    bias = jnp.zeros(bucket.shape, F32)
    for b in range(N_BUCKETS):
        bias = jnp.where(bucket == b, table_ref[b, head], bias)
    return jnp.where(valid, bias, NEG_INF)


def _bias_kernel(table_ref, sinks_ref, band_ref, sinkt_ref, past_ref, new_ref):
    hk = pl.program_id(0)
    kj = lax.broadcasted_iota(jnp.int32, (2 * BLOCK, BLOCK), 0)
    qi = lax.broadcasted_iota(jnp.int32, (2 * BLOCK, BLOCK), 1)
    dist = qi + BLOCK - kj
    bucket = _buckets(dist)
    in_window = (dist >= 0) & (dist < WINDOW)
    rows = past_ref.shape[2]
    d_past = rows - lax.broadcasted_iota(jnp.int32, (1, rows), 1)
    b_past = _buckets(d_past)
    ok_past = (d_past >= 0) & (d_past < WINDOW)
    d_new = jnp.zeros((1, LANES), jnp.int32)
    b_new = _buckets(d_new)
    for par in range(HEADS_PER_TILE):
        for slab in range(SLABS):
            g = slab * HEADS_PER_TILE + par
            head = hk * GROUP + g
            rs = slice(par * 2 * BLOCK, (par + 1) * 2 * BLOCK)
            cs = slice(slab * BLOCK, (slab + 1) * BLOCK)
            band_ref[0, 0, rs, cs] = _lookup(bucket, in_window, table_ref, head)
            band_ref[1, 0, rs, cs] = _lookup(bucket, in_window & (kj >= BLOCK), table_ref, head)
            sinkt_ref[0, par, :, cs] = jnp.full((1, BLOCK), sinks_ref[head], F32)
            past_ref[0, g:g + 1, :] = _lookup(b_past, ok_past, table_ref, head)
            new_ref[0, g:g + 1, :] = _lookup(b_new, d_new == 0, table_ref, head)


def bias_tables(table, sinks, past_rows):
    smem = pl.BlockSpec(memory_space=pltpu.SMEM)
    return pl.pallas_call(
        _bias_kernel,
        grid=(N_KV_HEADS,),
        in_specs=[smem, smem],
        out_specs=[
            pl.BlockSpec((2, 1, HEADS_PER_TILE * 2 * BLOCK, SLABS * BLOCK), lambda h: (0, h, 0, 0)),
            pl.BlockSpec((1, HEADS_PER_TILE, 1, SLABS * BLOCK), lambda h: (h, 0, 0, 0)),
            pl.BlockSpec((1, GROUP, past_rows), lambda h: (h, 0, 0)),
            pl.BlockSpec((1, GROUP, LANES), lambda h: (h, 0, 0)),
        ],
        out_shape=[
            jax.ShapeDtypeStruct((2, N_KV_HEADS, HEADS_PER_TILE * 2 * BLOCK, SLABS * BLOCK), F32),
            jax.ShapeDtypeStruct((N_KV_HEADS, HEADS_PER_TILE, 1, SLABS * BLOCK), F32),
            jax.ShapeDtypeStruct((N_KV_HEADS, GROUP, past_rows), F32),
            jax.ShapeDtypeStruct((N_KV_HEADS, GROUP, LANES), F32),
        ],
        compiler_params=_params("parallel"),
        name="bias_tables",
    )(table, sinks)


def _band_attn_kernel(q_ref, kp_ref, kc_ref, vp_ref, vc_ref, gate_ref, bias_ref, sink_ref, y_ref,
                      s_scr, p_scr):
    first = (pl.program_id(1) == 0).astype(jnp.int32)
    nt = (((1,), (1,)), ((), ()))
    low = (lax.broadcasted_iota(jnp.int32, (1, LANES), 1) < HEAD_DIM)
    keep_low = low.astype(BF16)
    keep_high = 1 - keep_low
    nkeys = 2 * BLOCK
    zeros_v = jnp.zeros((HEAD_DIM, nkeys), BF16)
    rows = ATT_ROWS

    def scores(hk):
        cs = slice(hk * LANES, (hk + 1) * LANES)
        kd = jnp.concatenate([kp_ref[:, cs], kc_ref[:, cs]], axis=0)
        lhs = jnp.concatenate([kd * keep_low, kd * keep_high], axis=0)
        qs = jnp.concatenate([q_ref[:, (hk * SLABS + s) * LANES:(hk * SLABS + s + 1) * LANES]
                              for s in range(SLABS)], axis=0)
        s_scr[hk % 2] = lax.dot_general(lhs, qs, nt, preferred_element_type=F32)

    scores(0)
    for hk in range(N_KV_HEADS):
        slot = hk % 2
        if hk + 1 < N_KV_HEADS:
            scores(hk + 1)
        inv = []
        for par in range(HEADS_PER_TILE):
            base = par * nkeys
            sink = sink_ref[hk, par]
            top = None
            for r in range(base, base + nkeys, rows):
                sb = s_scr[slot, r:r + rows, :] + bias_ref[first, hk, r:r + rows, :]
                s_scr[slot, r:r + rows, :] = sb
                top = sb if top is None else jnp.maximum(top, sb)
            m = jnp.maximum(jnp.max(top, axis=0, keepdims=True), sink)
            tot = None
            for r in range(base, base + nkeys, rows):
                p = jnp.exp(s_scr[slot, r:r + rows, :] - m)
                p_scr[slot, r:r + rows, :] = p.astype(BF16)
                tot = p if tot is None else tot + p
            inv.append(1.0 / (jnp.sum(tot, axis=0, keepdims=True) + jnp.exp(sink - m)))
        cs = slice(hk * LANES, (hk + 1) * LANES)
        vt = jnp.concatenate([vp_ref[cs, :], vc_ref[cs, :]], axis=1)
        lhs_v = jnp.concatenate([
            jnp.concatenate([vt[:HEAD_DIM], zeros_v], axis=0),
            jnp.concatenate([zeros_v, vt[HEAD_DIM:]], axis=0)], axis=1)
        ot = jnp.dot(lhs_v, p_scr[slot], preferred_element_type=F32)
        ot = jnp.concatenate([ot[:HEAD_DIM] * inv[0], ot[HEAD_DIM:] * inv[1]], axis=0)
        o = ot.T
        for sl in range(SLABS):
            c0 = (hk * SLABS + sl) * LANES
            y_ref[:, c0:c0 + LANES] = (o[sl * BLOCK:(sl + 1) * BLOCK]
                                       * _silu(gate_ref[:, c0:c0 + LANES])).astype(y_ref.dtype)


def band_attention(q, kdup, vt, gate, bias_band, sink_t, bsz, t):
    m = q.shape[0]
    nblk = t // BLOCK
    cur = lambda n: pl.BlockSpec((BLOCK, n), lambda b, i: (b * nblk + i, 0))
    prev = lambda n: pl.BlockSpec((BLOCK, n), lambda b, i: (b * nblk + jnp.maximum(i - 1, 0), 0))
    cur_t = pl.BlockSpec((2 * KV_WIDTH, BLOCK), lambda b, i: (0, b * nblk + i))
    prev_t = pl.BlockSpec((2 * KV_WIDTH, BLOCK), lambda b, i: (0, b * nblk + jnp.maximum(i - 1, 0)))
    score_tile = (HEADS_PER_TILE * 2 * BLOCK, SLABS * BLOCK)
    return pl.pallas_call(
        _band_attn_kernel,
        grid=(bsz, nblk),
        in_specs=[
            cur(ATT_WIDTH), prev(2 * KV_WIDTH), cur(2 * KV_WIDTH), prev_t, cur_t,
            cur(ATT_WIDTH), _resident(bias_band.shape), _resident(sink_t.shape),
        ],
        out_specs=cur(ATT_WIDTH),
        out_shape=jax.ShapeDtypeStruct((m, ATT_WIDTH), BF16),
        scratch_shapes=[pltpu.VMEM((2,) + score_tile, F32), pltpu.VMEM((2,) + score_tile, BF16)],
        compiler_params=_params("parallel", "parallel"),
        name="band_attention",
    )(q, kdup, kdup, vt, vt, gate, bias_band, sink_t)


def _cached_attn_kernel(q_ref, ckt_ref, cvt_ref, kn_ref, vn_ref, sinks_ref, bpast_ref, bnew_ref, o_ref):
    shape = (N_Q_HEADS, KV_WIDTH)
    lane_kv = lax.broadcasted_iota(jnp.int32, shape, 1) // HEAD_DIM
    row_kv = lax.broadcasted_iota(jnp.int32, shape, 0) // GROUP
    own = lane_kv == row_kv
    sink = sinks_ref[...]
    nt = (((1,), (1,)), ((), ()))
    for b in range(q_ref.shape[0]):
        q = q_ref[b]
        qt = jnp.concatenate([q] * N_KV_HEADS, axis=1)
        qm = jnp.where(own, qt, 0.0).astype(BF16)
        knew = kn_ref[b].astype(BF16).astype(F32)
        vnew = vn_ref[b].astype(BF16).astype(F32)
        s = jnp.dot(qm, ckt_ref[b].astype(BF16), preferred_element_type=F32) + bpast_ref[...]
        s_new = jnp.sum(qm.astype(F32) * knew, axis=-1, keepdims=True) + bnew_ref[:, :1]
        m = jnp.maximum(jnp.maximum(jnp.max(s, axis=-1, keepdims=True), s_new), sink)
        p = jnp.exp(s - m)
        p_new = jnp.exp(s_new - m)
        denom = jnp.sum(p, axis=-1, keepdims=True) + p_new + jnp.exp(sink - m)
        o_all = (lax.dot_general(p.astype(BF16), cvt_ref[b].astype(BF16), nt, preferred_element_type=F32)
                 + p_new.astype(BF16).astype(F32) * vnew)
        o_all = jnp.where(own, o_all, 0.0)
        o = o_all[:, :HEAD_DIM]
        for hk in range(1, N_KV_HEADS):
            o = o + o_all[:, hk * HEAD_DIM:(hk + 1) * HEAD_DIM]
        o_ref[b] = o / denom


def cached_attention(q, cache_kt, cache_vt, k_new, v_new, sinks, bias_past, bias_new):
    bsz, _, rows = cache_kt.shape
    nseq = math.gcd(bsz, SEQS_PER_STEP)
    per_seq = lambda r, n: pl.BlockSpec((nseq, r, n), lambda b: (b, 0, 0))
    return pl.pallas_call(
        _cached_attn_kernel,
        grid=(bsz // nseq,),
        in_specs=[
            per_seq(N_Q_HEADS, HEAD_DIM), per_seq(KV_WIDTH, rows), per_seq(KV_WIDTH, rows),
            per_seq(1, KV_WIDTH), per_seq(1, KV_WIDTH),
            _resident((N_Q_HEADS, 1)), _resident((N_Q_HEADS, rows)), _resident((N_Q_HEADS, LANES)),
        ],
        out_specs=per_seq(N_Q_HEADS, HEAD_DIM),
        out_shape=jax.ShapeDtypeStruct((bsz, N_Q_HEADS, HEAD_DIM), F32),
        compiler_params=_params("parallel"),
        name="cached_attention",
    )(q, cache_kt, cache_vt, k_new, v_new, sinks, bias_past, bias_new)


def kernel(x_prompt, x_sample, state_conv, state_h, cache_k, cache_v, a_norm_pre, a_norm_post,
           a_w_in, a_conv_w, a_conv_b, a_w_r, a_b_r, a_w_i, a_b_i, a_lambda, a_w_out, kv_norm, w_kv,
           b_norm_pre, b_norm_post, b_w_qg, b_sinks, b_w_out, rel_bias_table):
    bsz, t, d = x_prompt.shape
    dbsz, dt, _ = x_sample.shape
    assert a_w_in.shape[0] == 1 and b_w_qg.shape[0] == 1 and dt == 1
    assert t % BLOCK == 0 and t >= WINDOW
    past_rows = cache_k.shape[1]
    assert past_rows == min(WINDOW, PAST_LEN)

    w_r = (0.5 * a_w_r[0]).astype(BF16)
    w_i = (0.5 * a_w_i[0]).astype(BF16)
    sinks = b_sinks[0]
    bias_band, sink_t, bias_past, bias_new = bias_tables(rel_bias_table, sinks, past_rows)
    lru = (a_conv_w[0], a_conv_b[0], w_r, a_b_r[0], w_i, a_b_i[0], a_lambda[0])

    tm = 2 * SUB_ROWS
    xp = x_prompt.reshape(bsz * t, d)
    xs = x_sample.reshape(dbsz, d)

    ug_s, u, gate = norm_proj(xs, xp, a_norm_pre[0], a_w_in[0], tm)
    hs, s_conv_t = rglru_step(ug_s[:, :LRU_WIDTH], jnp.transpose(state_conv[0], (1, 0, 2)), state_h[0], *lru)
    conv0 = jnp.zeros((bsz, CONV_W - 1, LRU_WIDTH), F32)
    h0 = jnp.zeros((bsz, LRU_WIDTH), F32)
    y, p_conv, p_h = rglru_scan(u, gate, conv0, h0, *lru, seq_len=t, tc=SUB_ROWS)
    xs1, x1 = proj_norm_res(hs, ug_s[:, LRU_WIDTH:], xs, y, a_w_out[0], a_norm_post[0], xp, tm)

    kv_s, qg_s, q, gate_b, kdup, vt, k_tail, v_tail = norm_proj_kvq(
        xs1, x1, kv_norm, b_norm_pre[0], w_kv, b_w_qg[0], SUB_ROWS, seq_len=t)
    ks, vs = kv_s[:, :KV_WIDTH], kv_s[:, KV_WIDTH:]
    cache_kt = jnp.transpose(cache_k, (0, 2, 3, 1)).reshape(dbsz, KV_WIDTH, past_rows)
    cache_vt = jnp.transpose(cache_v, (0, 2, 3, 1)).reshape(dbsz, KV_WIDTH, past_rows)
    os_ = cached_attention(qg_s[:, :ATT_WIDTH].reshape(dbsz, N_Q_HEADS, HEAD_DIM), cache_kt, cache_vt,
                           ks.reshape(dbsz, 1, KV_WIDTH), vs.reshape(dbsz, 1, KV_WIDTH),
                           sinks.reshape(N_Q_HEADS, 1), bias_past.reshape(N_Q_HEADS, past_rows),
                           bias_new.reshape(N_Q_HEADS, LANES))
    yb = band_attention(q, kdup, vt, gate_b, bias_band, sink_t, bsz, t)
    y_sample, y_prompt = proj_norm_res(os_.reshape(dbsz, ATT_WIDTH), qg_s[:, ATT_WIDTH:], xs1, yb,
                                       b_w_out[0], b_norm_post[0], x1, tm)
    y_prompt = y_prompt.reshape(bsz, t, d)
    p_k = jnp.transpose(k_tail.reshape(bsz, N_KV_HEADS, HEAD_DIM, WINDOW), (0, 3, 1, 2))
    p_v = jnp.transpose(v_tail.reshape(bsz, N_KV_HEADS, HEAD_DIM, WINDOW), (0, 3, 1, 2))

    return (y_prompt, y_sample.reshape(dbsz, 1, d),
            p_conv[None], p_h.reshape(1, bsz, LRU_WIDTH), p_k, p_v,
            jnp.transpose(s_conv_t, (1, 0, 2))[None], hs[None],
            ks.reshape(dbsz, 1, N_KV_HEADS, HEAD_DIM), vs.reshape(dbsz, 1, N_KV_HEADS, HEAD_DIM))
```

```python
import functools
import math

import jax
import jax.numpy as jnp
from jax import lax
from jax.experimental import pallas as pl
from jax.experimental.pallas import tpu as pltpu

F32 = jnp.float32
BF16 = jnp.bfloat16

D_MODEL = 2048
LRU_WIDTH = 2048
LRU_BLOCKS = 8
LRU_BLOCK_W = LRU_WIDTH // LRU_BLOCKS
CONV_W = 4
LRU_C = 8.0
HEAD_DIM = 64
N_Q_HEADS = 32
N_KV_HEADS = 8
GROUP = N_Q_HEADS // N_KV_HEADS
ATT_WIDTH = N_Q_HEADS * HEAD_DIM
KV_WIDTH = N_KV_HEADS * HEAD_DIM
WINDOW = 128
BLOCK = WINDOW
N_BUCKETS = 32
MAX_DISTANCE = 128
RMS_EPS = 1e-6
NEG_INF = -1e30
LOG2_E = 1.4426950408889634
PAST_LEN = 16384

V7X_VMEM_BYTES = 64 * 1024 * 1024
VMEM_LIMIT = V7X_VMEM_BYTES - 8 * 1024 * 1024
SUBLANES = 8
LANES = 128
HEADS_PER_TILE = LANES // HEAD_DIM
SLABS = GROUP // HEADS_PER_TILE
ATT_ROWS = 32
SUB_ROWS = 256
W_CHUNK = 512
SEQS_PER_STEP = 8
MXU_COLS = 256


def _params(*semantics):
    return pltpu.CompilerParams(dimension_semantics=semantics, vmem_limit_bytes=VMEM_LIMIT)


def _resident(shape):
    zeros = (0,) * len(shape)
    return pl.BlockSpec(shape, lambda *_: zeros, pipeline_mode=pl.Buffered(1))


def _rms_scale(x):
    return lax.rsqrt(jnp.mean(x * x, axis=-1, keepdims=True) + RMS_EPS)


def _silu(x):
    h = 0.5 * x
    return h * jnp.tanh(h) + h


def _segment_major(rows, inverse=False):
    seg = rows // SUBLANES
    r = lax.broadcasted_iota(jnp.int32, (rows, rows), 0)
    c = lax.broadcasted_iota(jnp.int32, (rows, rows), 1)
    if inverse:
        src = (r % seg) * SUBLANES + r // seg
    else:
        src = (r % SUBLANES) * seg + r // SUBLANES
    return jnp.where(c == src, 1.0, 0.0).astype(BF16)


def _phase_specs(nchunk, tm, k):
    chunk_w = pl.BlockSpec((k, W_CHUNK), lambda i: (0, jnp.minimum(i, nchunk - 1)))
    chunk_o = lambda rows: pl.BlockSpec((rows, W_CHUNK), lambda i: (0, jnp.minimum(i, nchunk - 1)))
    tile = lambda n: pl.BlockSpec((tm, n), lambda i: (jnp.maximum(i - nchunk, 0), 0))
    return chunk_w, chunk_o, tile


def _dup_heads(x):
    low = lax.broadcasted_iota(jnp.int32, (x.shape[0], LANES), 1) < HEAD_DIM
    out = []
    for c in range(x.shape[1] // LANES):
        col = x[:, c * LANES:(c + 1) * LANES]
        swapped = pltpu.roll(col, HEAD_DIM, axis=1)
        out += [jnp.where(low, col, swapped), jnp.where(low, swapped, col)]
    return jnp.concatenate(out, axis=1)


def _norm_proj_kvq_kernel(nkv, nqg, xs_ref, x_ref, gkv_ref, gq_ref, wkv_ref, wqg_ref,
                          kvs_ref, qgs_ref, q_ref, gate_ref, kdup_ref, vt_ref, ktail_ref, vtail_ref,
                          wkv_scr, wqg_scr):
    i = pl.program_id(0)
    nchunk = nkv + nqg
    q_chunks = ATT_WIDTH // W_CHUNK
    q_scale = 1.0 / math.sqrt(HEAD_DIM)

    def sample_rows(g_ref):
        xs = xs_ref[...]
        return (xs * _rms_scale(xs) * g_ref[...]).astype(BF16)

    @pl.when(i < nkv)
    def _():
        wb = wkv_ref[...].astype(BF16)
        wkv_scr[i] = wb
        kvs_ref[...] = jnp.dot(sample_rows(gkv_ref), wb, preferred_element_type=F32)

    @pl.when((i >= nkv) & (i < nchunk))
    def _():
        c = i - nkv
        wb = wqg_ref[...].astype(BF16)
        wqg_scr[c] = wb
        r = jnp.dot(sample_rows(gq_ref), wb, preferred_element_type=F32)
        qgs_ref[...] = r * jnp.where(c < q_chunks, q_scale, 1.0)

    @pl.when(i >= nchunk)
    def _():
        tm = x_ref.shape[0]
        for rs in _row_blocks(tm):
            x = x_ref[rs, :]
            xh = x * _rms_scale(x)
            xkv = (xh * gkv_ref[...]).astype(BF16)
            xq = (xh * gq_ref[...]).astype(BF16)
            k = jnp.dot(xkv, wkv_scr[0], preferred_element_type=F32)
            v = jnp.dot(xkv, wkv_scr[1], preferred_element_type=F32)
            kdup_ref[rs, :] = _dup_heads(k).astype(BF16)
            vt_ref[:, rs] = _dup_heads(v).T.astype(BF16)
            for c in range(nqg):
                r = jnp.dot(xq, wqg_scr[c], preferred_element_type=F32)
                if c < q_chunks:
                    q_ref[rs, c * W_CHUNK:(c + 1) * W_CHUNK] = (r * q_scale).astype(q_ref.dtype)
                else:
                    cc = c - q_chunks
                    gate_ref[rs, cc * W_CHUNK:(cc + 1) * W_CHUNK] = r
        ktail_ref[0] = k[k.shape[0] - WINDOW:].T
        vtail_ref[0] = v[v.shape[0] - WINDOW:].T


def norm_proj_kvq(xs, x, g_kv, g_q, w_kv, w_qg, tm, seq_len):
    m, d = x.shape
    ns = xs.shape[0]
    assert w_kv.shape[1] == 2 * KV_WIDTH == 2 * W_CHUNK and seq_len % tm == 0 and tm >= WINDOW
    nkv, nqg = w_kv.shape[1] // W_CHUNK, w_qg.shape[1] // W_CHUNK
    nchunk = nkv + nqg
    tiles = seq_len // tm
    tile = lambda n: pl.BlockSpec((tm, n), lambda i: (jnp.maximum(i - nchunk, 0), 0))
    tail = pl.BlockSpec((1, KV_WIDTH, WINDOW), lambda i: (jnp.maximum(i - nchunk, 0) // tiles, 0, 0))
    kv_chunk = lambda i: (0, jnp.minimum(i, nkv - 1))
    qg_chunk = lambda i: (0, jnp.clip(i - nkv, 0, nqg - 1))
    return pl.pallas_call(
        functools.partial(_norm_proj_kvq_kernel, nkv, nqg),
        grid=(nchunk + m // tm,),
        in_specs=[
            _resident(xs.shape), tile(d), _resident((1, d)), _resident((1, d)),
            pl.BlockSpec((d, W_CHUNK), kv_chunk), pl.BlockSpec((d, W_CHUNK), qg_chunk),
        ],
        out_specs=[
            pl.BlockSpec((ns, W_CHUNK), kv_chunk), pl.BlockSpec((ns, W_CHUNK), qg_chunk),
            tile(ATT_WIDTH), tile(ATT_WIDTH), tile(2 * KV_WIDTH),
            pl.BlockSpec((2 * KV_WIDTH, tm), lambda i: (0, jnp.maximum(i - nchunk, 0))), tail, tail,
        ],
        out_shape=[
            jax.ShapeDtypeStruct((ns, w_kv.shape[1]), F32),
            jax.ShapeDtypeStruct((ns, w_qg.shape[1]), F32),
            jax.ShapeDtypeStruct((m, ATT_WIDTH), BF16),
            jax.ShapeDtypeStruct((m, ATT_WIDTH), F32),
            jax.ShapeDtypeStruct((m, 2 * KV_WIDTH), BF16),
            jax.ShapeDtypeStruct((2 * KV_WIDTH, m), BF16),
            jax.ShapeDtypeStruct((m // seq_len, KV_WIDTH, WINDOW), F32),
            jax.ShapeDtypeStruct((m // seq_len, KV_WIDTH, WINDOW), F32),
        ],
        scratch_shapes=[pltpu.VMEM((nkv, d, W_CHUNK), BF16), pltpu.VMEM((nqg, d, W_CHUNK), BF16)],
        compiler_params=_params("arbitrary"),
        name="norm_proj_kvq",
    )(xs, x, g_kv.reshape(1, d), g_q.reshape(1, d), w_kv, w_qg)


def _row_blocks(rows):
    sub = min(rows, SUB_ROWS)
    return [slice(r, r + sub) for r in range(0, rows, sub)]


def _proj_norm_res_kernel(nchunk, nparts, as_ref, gs_ref, xs_ref, *refs):
    y_refs = refs[:nparts]
    w_ref, g_ref, x_ref, os_ref, o_ref, w_scr, raw_scr = refs[nparts:]
    i = pl.program_id(0)

    @pl.when(i < nchunk)
    def _():
        wb = w_ref[...].astype(BF16)
        w_scr[i] = wb
        ys = (as_ref[...] * _silu(gs_ref[...])).astype(BF16)
        raw_scr[i] = jnp.dot(ys, wb, preferred_element_type=F32)

    @pl.when(i == nchunk - 1)
    def _():
        o = jnp.concatenate([raw_scr[c] for c in range(nchunk)], axis=1)
        os_ref[...] = xs_ref[...] + o * _rms_scale(o) * g_ref[...]

    @pl.when(i >= nchunk)
    def _():
        tm = x_ref.shape[0]
        part_rows = tm // nparts
        for rs in _row_blocks(tm):
            part, off = divmod(rs.start, part_rows)
            y = y_refs[part][off:off + rs.stop - rs.start, :]
            o = jnp.concatenate([jnp.dot(y, w_scr[c], preferred_element_type=F32)
                                 for c in range(nchunk)], axis=1)
            o_ref[rs, :] = x_ref[rs, :] + o * _rms_scale(o) * g_ref[...]


def proj_norm_res(a_s, gate_s, x_s, y_parts, w, g, x, tm):
    k, d = w.shape
    m = x.shape[0]
    nchunk = d // W_CHUNK
    nparts = len(y_parts)
    assert (tm // nparts) % min(tm, SUB_ROWS) == 0
    chunk_w, _, tile = _phase_specs(nchunk, tm, k)
    part = pl.BlockSpec((tm // nparts, k), lambda i: (jnp.maximum(i - nchunk, 0), 0))
    return pl.pallas_call(
        functools.partial(_proj_norm_res_kernel, nchunk, nparts),
        grid=(nchunk + m // tm,),
        in_specs=[_resident(a_s.shape), _resident(gate_s.shape), _resident(x_s.shape)]
        + [part] * nparts + [chunk_w, _resident((1, d)), tile(d)],
        out_specs=[pl.BlockSpec(x_s.shape, lambda i: (0, 0)), tile(d)],
        out_shape=[jax.ShapeDtypeStruct(x_s.shape, F32), jax.ShapeDtypeStruct((m, d), F32)],
        scratch_shapes=[pltpu.VMEM((nchunk, k, W_CHUNK), BF16),
                        pltpu.VMEM((nchunk, x_s.shape[0], W_CHUNK), F32)],
        compiler_params=_params("arbitrary"),
        name="proj_norm_res",
    )(a_s, gate_s, x_s, *y_parts, w, g.reshape(1, d), x)


def _lru_gates(conv, wr_half, br, wi_half, bi, lam):
    cb = conv.astype(BF16)
    th_r = jnp.tanh(jnp.dot(cb, wr_half, preferred_element_type=F32) + 0.5 * br)
    th_i = jnp.tanh(jnp.dot(cb, wi_half, preferred_element_type=F32) + 0.5 * bi)
    nl = -lam
    softplus = jnp.maximum(nl, 0.0) + jnp.log1p(jnp.exp(-jnp.abs(nl)))
    half = (0.5 * LRU_C) * softplus
    x = th_r * half + half
    a = jnp.exp2(x * -LOG2_E)
    z = jnp.tanh(x) * (a * a + 1.0)
    mult = z * lax.rsqrt(jnp.maximum(z, 1e-30))
    hc = 0.5 * conv
    return a, mult * (hc * th_i + hc)


def _interleave(*stages):
    live = [[stage, share] for stage, share in stages]
    while live:
        for entry in list(live):
            try:
                for _ in range(entry[1]):
                    next(entry[0])
            except StopIteration:
                live.remove(entry)


def _in_proj_tile(x_ref, rs, g_ref, w_scr, ug_ref):
    x = x_ref[rs, :]
    xn = (x * _rms_scale(x) * g_ref[...]).astype(BF16)
    xn = jnp.dot(_segment_major(xn.shape[0]), xn, preferred_element_type=F32).astype(BF16)
    for c in range(w_scr.shape[0]):
        for n0 in range(0, W_CHUNK, MXU_COLS):
            ug_ref[:, c * W_CHUNK + n0:c * W_CHUNK + n0 + MXU_COLS] = jnp.dot(
                xn, w_scr[c, :, n0:n0 + MXU_COLS], preferred_element_type=F32)
            yield


def _rglru_tile(ug_ref, y_ref, cw_ref, cb_ref, wr_ref, br_ref, wi_ref, bi_ref, lam_ref, h_scr, tail_scr):
    tc = ug_ref.shape[0]
    seg = tc // SUBLANES
    ntaps = CONV_W - 1
    bw = LRU_BLOCK_W
    sub = lax.broadcasted_iota(jnp.int32, (SUBLANES, bw), 0)
    first = sub == 0
    time_order = _segment_major(tc, inverse=True)

    def shift_in(x, row0):
        return jnp.where(first, row0, pltpu.roll(x, 1, axis=0))

    def group(x, j):
        return x[j * SUBLANES:(j + 1) * SUBLANES]

    for n in range(LRU_BLOCKS):
        cs = slice(n * bw, (n + 1) * bw)
        u = ug_ref[:, cs]
        tail = tail_scr[:, cs]
        before = [shift_in(group(u, seg - m), tail[ntaps - m:ntaps - m + 1])
                  for m in range(ntaps, 0, -1)]
        ext = jnp.concatenate(before + [u], axis=0)
        tail_scr[:, cs] = jnp.concatenate(
            [group(u, seg - m)[SUBLANES - 1:] for m in range(ntaps, 0, -1)], axis=0)
        cw = cw_ref[:, cs]
        conv = cb_ref[:, cs]
        for tap in range(CONV_W):
            conv = conv + ext[tap * SUBLANES:tap * SUBLANES + tc] * cw[tap:tap + 1]
        yield

        a, b = _lru_gates(conv, wr_ref[n], br_ref[:, cs], wi_ref[n], bi_ref[:, cs], lam_ref[:, cs])
        yield

        h = b[:SUBLANES]
        acc = a[:SUBLANES]
        h_loc, a_cum = [h], [acc]
        for j in range(1, seg):
            sl = slice(j * SUBLANES, (j + 1) * SUBLANES)
            h = a[sl] * h + b[sl]
            acc = a[sl] * acc
            h_loc.append(h)
            a_cum.append(acc)

        step = 1
        while step < SUBLANES:
            keep = sub >= step
            h = jnp.where(keep, acc * pltpu.roll(h, step, axis=0) + h, h)
            acc = jnp.where(keep, acc * pltpu.roll(acc, step, axis=0), acc)
            step *= 2
        h_prev = h_scr[:, cs]
        after = h + acc * h_prev
        h_in = shift_in(after, h_prev)
        h_scr[:, cs] = after[SUBLANES - 1:]
        yield

        hs = jnp.concatenate([h_loc[j] + a_cum[j] * h_in for j in range(seg)], axis=0)
        y = (hs * _silu(ug_ref[:, LRU_WIDTH + n * bw:LRU_WIDTH + (n + 1) * bw])).astype(BF16)
        y_ref[:, cs] = jnp.dot(time_order, y, preferred_element_type=F32).astype(y_ref.dtype)
        yield


def _rglru_front_kernel(nchunk, npairs, chunks, xs_ref, x_ref, g_ref, w_ref, cprev_ref, h0_ref,
                        cw_ref, cb_ref, wr_ref, br_ref, wi_ref, bi_ref, lam_ref,
                        os_ref, y_even_ref, y_odd_ref, cnew_ref, hlast_ref,
                        w_scr, ug0_scr, ug1_scr, h_scr, tail_scr):
    i = pl.program_id(0)
    p = i - nchunk
    tc = SUB_ROWS
    lru = (cw_ref, cb_ref, wr_ref, br_ref, wi_ref, bi_ref, lam_ref, h_scr, tail_scr)

    @pl.when(i < nchunk)
    def _():
        wb = w_ref[...].astype(BF16)
        w_scr[i] = wb
        xs = xs_ref[...]
        xsn = (xs * _rms_scale(xs) * g_ref[...]).astype(BF16)
        os_ref[...] = jnp.dot(xsn, wb, preferred_element_type=F32)

    @pl.when(i == nchunk - 1)
    def _():
        ug1_scr[...] = jnp.zeros_like(ug1_scr)
        h_scr[...] = jnp.zeros_like(h_scr)
        tail_scr[...] = jnp.zeros_like(tail_scr)

    @pl.when(p >= 0)
    def _():
        _interleave((_in_proj_tile(x_ref, slice(0, tc), g_ref, w_scr, ug0_scr), 1),
                    (_rglru_tile(ug1_scr, y_odd_ref, *lru), 2))
        hlast_ref[0] = h_scr[...]
        cnew_ref[0] = tail_scr[...]

    @pl.when((p >= 0) & (p < npairs))
    def _():
        @pl.when((2 * p) % chunks == 0)
        def _():
            h_scr[...] = h0_ref[0]
            tail_scr[...] = cprev_ref[0]

        _interleave((_in_proj_tile(x_ref, slice(tc, 2 * tc), g_ref, w_scr, ug1_scr), 1),
                    (_rglru_tile(ug0_scr, y_even_ref, *lru), 2))


def rglru_front(xs, x, g, w_in, conv_prev, h0, conv_w, conv_b, w_r, b_r, w_i, b_i, lam, seq_len):
    m, d = x.shape
    w = w_in.shape[1] // 2
    tc = SUB_ROWS
    nchunk = w_in.shape[1] // W_CHUNK
    bsz = m // seq_len
    chunks = seq_len // tc
    npairs = m // (2 * tc)
    assert seq_len % (2 * tc) == 0 and tc % (SUBLANES * SUBLANES) == 0 and tc // SUBLANES > CONV_W
    pair = lambda i: jnp.clip(i - nchunk, 0, npairs - 1)
    last = npairs * 2 - 1
    seq_in = lambda i: (jnp.clip(2 * (i - nchunk), 0, last) // chunks, 0, 0)
    seq_out = lambda i: (jnp.clip(2 * (i - nchunk) - 1, 0, last) // chunks, 0, 0)
    state_in = lambda rows: pl.BlockSpec((1, rows, w), seq_in)
    state_out = lambda rows: pl.BlockSpec((1, rows, w), seq_out)
    chunk = lambda i: (0, jnp.minimum(i, nchunk - 1))
    return pl.pallas_call(
        functools.partial(_rglru_front_kernel, nchunk, npairs, chunks),
        grid=(nchunk + npairs + 1,),
        in_specs=[_resident(xs.shape), pl.BlockSpec((2 * tc, d), lambda i: (pair(i), 0)), _resident((1, d)),
                  pl.BlockSpec((d, W_CHUNK), chunk), state_in(CONV_W - 1), state_in(1),
                  _resident((CONV_W, w)), _resident((1, w)), _resident(w_r.shape), _resident((1, w)),
                  _resident(w_i.shape), _resident((1, w)), _resident((1, w))],
        out_specs=[
            pl.BlockSpec((xs.shape[0], W_CHUNK), chunk),
            pl.BlockSpec((tc, w), lambda i: (pair(i), 0)),
            pl.BlockSpec((tc, w), lambda i: (jnp.clip(i - nchunk - 1, 0, npairs - 1), 0)),
            state_out(CONV_W - 1), state_out(1),
        ],
        out_shape=[
            jax.ShapeDtypeStruct((xs.shape[0], 2 * w), F32),
            jax.ShapeDtypeStruct((m // 2, w), BF16),
            jax.ShapeDtypeStruct((m // 2, w), BF16),
            jax.ShapeDtypeStruct((bsz, CONV_W - 1, w), F32),
            jax.ShapeDtypeStruct((bsz, 1, w), F32),
        ],
        scratch_shapes=[pltpu.VMEM((nchunk, d, W_CHUNK), BF16),
                        pltpu.VMEM((tc, 2 * w), F32), pltpu.VMEM((tc, 2 * w), F32),
                        pltpu.VMEM((1, w), F32), pltpu.VMEM((CONV_W - 1, w), F32)],
        compiler_params=_params("arbitrary"),
        name="rglru_front",
    )(xs, x, g.reshape(1, d), w_in, conv_prev, h0.reshape(bsz, 1, w), conv_w, conv_b.reshape(1, w),
      w_r, b_r.reshape(1, w), w_i, b_i.reshape(1, w), lam.reshape(1, w))


def _rglru_step_kernel(u_ref, cprev_ref, h0_ref, cw_ref, cb_ref, wr_ref, br_ref, wi_ref, bi_ref,
                       lam_ref, h_ref, cnew_ref):
    u = u_ref[...]
    cw = cw_ref[...]
    conv = cb_ref[...]
    for tap in range(CONV_W - 1):
        conv = conv + cprev_ref[tap] * cw[tap:tap + 1]
        if tap > 0:
            cnew_ref[tap - 1] = cprev_ref[tap]
    conv = conv + u * cw[CONV_W - 1:]
    cnew_ref[CONV_W - 2] = u
    a, b = _lru_gates(conv, wr_ref[0], br_ref[...], wi_ref[0], bi_ref[...], lam_ref[...])
    h_ref[...] = a * h0_ref[...] + b


def rglru_step(u, conv_prev_t, h0, conv_w, conv_b, w_r, b_r, w_i, b_i, lam):
    bsz, w = u.shape
    bw = LRU_BLOCK_W
    rows = pl.BlockSpec((bsz, bw), lambda n: (0, n))
    taps = pl.BlockSpec((CONV_W - 1, bsz, bw), lambda n: (0, 0, n))
    chan = lambda r: pl.BlockSpec((r, bw), lambda n: (0, n))
    blockw = pl.BlockSpec((1, bw, bw), lambda n: (n, 0, 0))
    return pl.pallas_call(
        _rglru_step_kernel,
        grid=(LRU_BLOCKS,),
        in_specs=[rows, taps, rows, chan(CONV_W), chan(1), blockw, chan(1), blockw, chan(1), chan(1)],
        out_specs=[rows, taps],
        out_shape=[
            jax.ShapeDtypeStruct((bsz, w), F32),
            jax.ShapeDtypeStruct((CONV_W - 1, bsz, w), F32),
        ],
        compiler_params=_params("parallel"),
        name="rglru_step",
    )(u, conv_prev_t, h0, conv_w, conv_b.reshape(1, w), w_r, b_r.reshape(1, w),
      w_i, b_i.reshape(1, w), lam.reshape(1, w))


def _buckets(dist):
    n = jnp.maximum(dist, 0)
    max_exact = N_BUCKETS // 2
    nf = jnp.maximum(n, 1).astype(F32)
    large = max_exact + jnp.floor(jnp.log(nf / max_exact) / math.log(MAX_DISTANCE / max_exact)
                                  * (N_BUCKETS - max_exact)).astype(jnp.int32)
    large = jnp.minimum(large, N_BUCKETS - 1)
    return jnp.where(n < max_exact, n, large)


def _lookup(bucket, valid, table_ref, head):
    bias = jnp.zeros(bucket.shape, F32)
    for b in range(N_BUCKETS):
        bias = jnp.where(bucket == b, table_ref[b, head], bias)
    return jnp.where(valid, bias, NEG_INF)


def _bias_kernel(table_ref, sinks_ref, band_ref, sinkt_ref, past_ref, new_ref):
    hk = pl.program_id(0)
    kj = lax.broadcasted_iota(jnp.int32, (2 * BLOCK, BLOCK), 0)
    qi = lax.broadcasted_iota(jnp.int32, (2 * BLOCK, BLOCK), 1)
    dist = qi + BLOCK - kj
    bucket = _buckets(dist)
    in_window = (dist >= 0) & (dist < WINDOW)
    rows = past_ref.shape[2]
    d_past = rows - lax.broadcasted_iota(jnp.int32, (1, rows), 1)
    b_past = _buckets(d_past)
    ok_past = (d_past >= 0) & (d_past < WINDOW)
    d_new = jnp.zeros((1, LANES), jnp.int32)
    b_new = _buckets(d_new)
    for par in range(HEADS_PER_TILE):
        for slab in range(SLABS):
            g = slab * HEADS_PER_TILE + par
            head = hk * GROUP + g
            rs = slice(par * 2 * BLOCK, (par + 1) * 2 * BLOCK)
            cs = slice(slab * BLOCK, (slab + 1) * BLOCK)
            band_ref[0, 0, rs, cs] = _lookup(bucket, in_window, table_ref, head)
            band_ref[1, 0, rs, cs] = _lookup(bucket, in_window & (kj >= BLOCK), table_ref, head)
            sinkt_ref[0, par, :, cs] = jnp.full((1, BLOCK), sinks_ref[head], F32)
            past_ref[0, g:g + 1, :] = _lookup(b_past, ok_past, table_ref, head)
            new_ref[0, g:g + 1, :] = _lookup(b_new, d_new == 0, table_ref, head)


def bias_tables(table, sinks, past_rows):
    smem = pl.BlockSpec(memory_space=pltpu.SMEM)
    return pl.pallas_call(
        _bias_kernel,
        grid=(N_KV_HEADS,),
        in_specs=[smem, smem],
        out_specs=[
            pl.BlockSpec((2, 1, HEADS_PER_TILE * 2 * BLOCK, SLABS * BLOCK), lambda h: (0, h, 0, 0)),
            pl.BlockSpec((1, HEADS_PER_TILE, 1, SLABS * BLOCK), lambda h: (h, 0, 0, 0)),
            pl.BlockSpec((1, GROUP, past_rows), lambda h: (h, 0, 0)),
            pl.BlockSpec((1, GROUP, LANES), lambda h: (h, 0, 0)),
        ],
        out_shape=[
            jax.ShapeDtypeStruct((2, N_KV_HEADS, HEADS_PER_TILE * 2 * BLOCK, SLABS * BLOCK), F32),
            jax.ShapeDtypeStruct((N_KV_HEADS, HEADS_PER_TILE, 1, SLABS * BLOCK), F32),
            jax.ShapeDtypeStruct((N_KV_HEADS, GROUP, past_rows), F32),
            jax.ShapeDtypeStruct((N_KV_HEADS, GROUP, LANES), F32),
        ],
        compiler_params=_params("parallel"),
        name="bias_tables",
    )(table, sinks)


def _band_attn_kernel(q_ref, kp_ref, kc_ref, vp_ref, vc_ref, gate_ref, bias_ref, sink_ref, y_ref,
                      s_scr, p_scr):
    first = (pl.program_id(1) == 0).astype(jnp.int32)
    nt = (((1,), (1,)), ((), ()))
    low = (lax.broadcasted_iota(jnp.int32, (1, LANES), 1) < HEAD_DIM)
    keep_low = low.astype(BF16)
    keep_high = 1 - keep_low
    nkeys = 2 * BLOCK
    zeros_v = jnp.zeros((HEAD_DIM, nkeys), BF16)
    rows = ATT_ROWS

    def scores(hk):
        cs = slice(hk * LANES, (hk + 1) * LANES)
        kd = jnp.concatenate([kp_ref[:, cs], kc_ref[:, cs]], axis=0)
        lhs = jnp.concatenate([kd * keep_low, kd * keep_high], axis=0)
        qs = jnp.concatenate([q_ref[:, (hk * SLABS + s) * LANES:(hk * SLABS + s + 1) * LANES]
                              for s in range(SLABS)], axis=0)
        s_scr[hk % 2] = lax.dot_general(lhs, qs, nt, preferred_element_type=F32)

    scores(0)
    for hk in range(N_KV_HEADS):
        slot = hk % 2
        if hk + 1 < N_KV_HEADS:
            scores(hk + 1)
        inv = []
        for par in range(HEADS_PER_TILE):
            base = par * nkeys
            sink = sink_ref[hk, par]
            top = None
            for r in range(base, base + nkeys, rows):
                sb = s_scr[slot, r:r + rows, :] + bias_ref[first, hk, r:r + rows, :]
                s_scr[slot, r:r + rows, :] = sb
                top = sb if top is None else jnp.maximum(top, sb)
            m = jnp.maximum(jnp.max(top, axis=0, keepdims=True), sink)
            tot = None
            for r in range(base, base + nkeys, rows):
                p = jnp.exp(s_scr[slot, r:r + rows, :] - m)
                p_scr[slot, r:r + rows, :] = p.astype(BF16)
                tot = p if tot is None else tot + p
            inv.append(1.0 / (jnp.sum(tot, axis=0, keepdims=True) + jnp.exp(sink - m)))
        cs = slice(hk * LANES, (hk + 1) * LANES)
        vt = jnp.concatenate([vp_ref[cs, :], vc_ref[cs, :]], axis=1)
        lhs_v = jnp.concatenate([
            jnp.concatenate([vt[:HEAD_DIM], zeros_v], axis=0),
            jnp.concatenate([zeros_v, vt[HEAD_DIM:]], axis=0)], axis=1)
        ot = jnp.dot(lhs_v, p_scr[slot], preferred_element_type=F32)
        ot = jnp.concatenate([ot[:HEAD_DIM] * inv[0], ot[HEAD_DIM:] * inv[1]], axis=0)
        o = ot.T
        for sl in range(SLABS):
            c0 = (hk * SLABS + sl) * LANES
            y_ref[:, c0:c0 + LANES] = (o[sl * BLOCK:(sl + 1) * BLOCK]
                                       * _silu(gate_ref[:, c0:c0 + LANES])).astype(y_ref.dtype)


def band_attention(q, kdup, vt, gate, bias_band, sink_t, bsz, t):
    m = q.shape[0]
    nblk = t // BLOCK
    cur = lambda n: pl.BlockSpec((BLOCK, n), lambda b, i: (b * nblk + i, 0))
    prev = lambda n: pl.BlockSpec((BLOCK, n), lambda b, i: (b * nblk + jnp.maximum(i - 1, 0), 0))
    cur_t = pl.BlockSpec((2 * KV_WIDTH, BLOCK), lambda b, i: (0, b * nblk + i))
    prev_t = pl.BlockSpec((2 * KV_WIDTH, BLOCK), lambda b, i: (0, b * nblk + jnp.maximum(i - 1, 0)))
    score_tile = (HEADS_PER_TILE * 2 * BLOCK, SLABS * BLOCK)
    return pl.pallas_call(
        _band_attn_kernel,
        grid=(bsz, nblk),
        in_specs=[
            cur(ATT_WIDTH), prev(2 * KV_WIDTH), cur(2 * KV_WIDTH), prev_t, cur_t,
            cur(ATT_WIDTH), _resident(bias_band.shape), _resident(sink_t.shape),
        ],
        out_specs=cur(ATT_WIDTH),
        out_shape=jax.ShapeDtypeStruct((m, ATT_WIDTH), BF16),
        scratch_shapes=[pltpu.VMEM((2,) + score_tile, F32), pltpu.VMEM((2,) + score_tile, BF16)],
        compiler_params=_params("parallel", "parallel"),
        name="band_attention",
    )(q, kdup, kdup, vt, vt, gate, bias_band, sink_t)


def _cached_attn_kernel(q_ref, ckt_ref, cvt_ref, kn_ref, vn_ref, sinks_ref, bpast_ref, bnew_ref, o_ref):
    shape = (N_Q_HEADS, KV_WIDTH)
    lane_kv = lax.broadcasted_iota(jnp.int32, shape, 1) // HEAD_DIM
    row_kv = lax.broadcasted_iota(jnp.int32, shape, 0) // GROUP
    own = lane_kv == row_kv
    sink = sinks_ref[...]
    nt = (((1,), (1,)), ((), ()))
    for b in range(q_ref.shape[0]):
        q = q_ref[b]
        qt = jnp.concatenate([q] * N_KV_HEADS, axis=1)
        qm = jnp.where(own, qt, 0.0).astype(BF16)
        knew = kn_ref[b].astype(BF16).astype(F32)
        vnew = vn_ref[b].astype(BF16).astype(F32)
        s = jnp.dot(qm, ckt_ref[b].astype(BF16), preferred_element_type=F32) + bpast_ref[...]
        s_new = jnp.sum(qm.astype(F32) * knew, axis=-1, keepdims=True) + bnew_ref[:, :1]
        m = jnp.maximum(jnp.maximum(jnp.max(s, axis=-1, keepdims=True), s_new), sink)
        p = jnp.exp(s - m)
        p_new = jnp.exp(s_new - m)
        denom = jnp.sum(p, axis=-1, keepdims=True) + p_new + jnp.exp(sink - m)
        o_all = (lax.dot_general(p.astype(BF16), cvt_ref[b].astype(BF16), nt, preferred_element_type=F32)
                 + p_new.astype(BF16).astype(F32) * vnew)
        o_all = jnp.where(own, o_all, 0.0)
        o = o_all[:, :HEAD_DIM]
        for hk in range(1, N_KV_HEADS):
            o = o + o_all[:, hk * HEAD_DIM:(hk + 1) * HEAD_DIM]
        o_ref[b] = o / denom


def cached_attention(q, cache_kt, cache_vt, k_new, v_new, sinks, bias_past, bias_new):
    bsz, _, rows = cache_kt.shape
    nseq = math.gcd(bsz, SEQS_PER_STEP)
    per_seq = lambda r, n: pl.BlockSpec((nseq, r, n), lambda b: (b, 0, 0))
    return pl.pallas_call(
        _cached_attn_kernel,
        grid=(bsz // nseq,),
        in_specs=[
            per_seq(N_Q_HEADS, HEAD_DIM), per_seq(KV_WIDTH, rows), per_seq(KV_WIDTH, rows),
            per_seq(1, KV_WIDTH), per_seq(1, KV_WIDTH),
            _resident((N_Q_HEADS, 1)), _resident((N_Q_HEADS, rows)), _resident((N_Q_HEADS, LANES)),
        ],
        out_specs=per_seq(N_Q_HEADS, HEAD_DIM),
        out_shape=jax.ShapeDtypeStruct((bsz, N_Q_HEADS, HEAD_DIM), F32),
        compiler_params=_params("parallel"),
        name="cached_attention",
    )(q, cache_kt, cache_vt, k_new, v_new, sinks, bias_past, bias_new)


def kernel(x_prompt, x_sample, state_conv, state_h, cache_k, cache_v, a_norm_pre, a_norm_post,
           a_w_in, a_conv_w, a_conv_b, a_w_r, a_b_r, a_w_i, a_b_i, a_lambda, a_w_out, kv_norm, w_kv,
           b_norm_pre, b_norm_post, b_w_qg, b_sinks, b_w_out, rel_bias_table):
    bsz, t, d = x_prompt.shape
    dbsz, dt, _ = x_sample.shape
    assert a_w_in.shape[0] == 1 and b_w_qg.shape[0] == 1 and dt == 1
    assert t % BLOCK == 0 and t >= WINDOW
    past_rows = cache_k.shape[1]
    assert past_rows == min(WINDOW, PAST_LEN)

    w_r = (0.5 * a_w_r[0]).astype(BF16)
    w_i = (0.5 * a_w_i[0]).astype(BF16)
    sinks = b_sinks[0]
    bias_band, sink_t, bias_past, bias_new = bias_tables(rel_bias_table, sinks, past_rows)
    lru = (a_conv_w[0], a_conv_b[0], w_r, a_b_r[0], w_i, a_b_i[0], a_lambda[0])

    tm = 2 * SUB_ROWS
    xp = x_prompt.reshape(bsz * t, d)
    xs = x_sample.reshape(dbsz, d)

    conv0 = jnp.zeros((bsz, CONV_W - 1, LRU_WIDTH), F32)
    h0 = jnp.zeros((bsz, LRU_WIDTH), F32)
    ug_s, y_even, y_odd, p_conv, p_h = rglru_front(xs, xp, a_norm_pre[0], a_w_in[0], conv0, h0, *lru, seq_len=t)
    hs, s_conv_t = rglru_step(ug_s[:, :LRU_WIDTH], jnp.transpose(state_conv[0], (1, 0, 2)), state_h[0], *lru)
    xs1, x1 = proj_norm_res(hs, ug_s[:, LRU_WIDTH:], xs, (y_even, y_odd), a_w_out[0], a_norm_post[0], xp, tm)

    kv_s, qg_s, q, gate_b, kdup, vt, k_tail, v_tail = norm_proj_kvq(
        xs1, x1, kv_norm, b_norm_pre[0], w_kv, b_w_qg[0], SUB_ROWS, seq_len=t)
    ks, vs = kv_s[:, :KV_WIDTH], kv_s[:, KV_WIDTH:]
    cache_kt = jnp.transpose(cache_k, (0, 2, 3, 1)).reshape(dbsz, KV_WIDTH, past_rows)
    cache_vt = jnp.transpose(cache_v, (0, 2, 3, 1)).reshape(dbsz, KV_WIDTH, past_rows)
    os_ = cached_attention(qg_s[:, :ATT_WIDTH].reshape(dbsz, N_Q_HEADS, HEAD_DIM), cache_kt, cache_vt,
                           ks.reshape(dbsz, 1, KV_WIDTH), vs.reshape(dbsz, 1, KV_WIDTH),
                           sinks.reshape(N_Q_HEADS, 1), bias_past.reshape(N_Q_HEADS, past_rows),
                           bias_new.reshape(N_Q_HEADS, LANES))
    yb = band_attention(q, kdup, vt, gate_b, bias_band, sink_t, bsz, t)
    y_sample, y_prompt = proj_norm_res(os_.reshape(dbsz, ATT_WIDTH), qg_s[:, ATT_WIDTH:], xs1, (yb,),
                                       b_w_out[0], b_norm_post[0], x1, tm)
    y_prompt = y_prompt.reshape(bsz, t, d)
    p_k = jnp.transpose(k_tail.reshape(bsz, N_KV_HEADS, HEAD_DIM, WINDOW), (0, 3, 1, 2))
    p_v = jnp.transpose(v_tail.reshape(bsz, N_KV_HEADS, HEAD_DIM, WINDOW), (0, 3, 1, 2))

    return (y_prompt, y_sample.reshape(dbsz, 1, d),
            p_conv[None], p_h.reshape(1, bsz, LRU_WIDTH), p_k, p_v,
            jnp.transpose(s_conv_t, (1, 0, 2))[None], hs[None],
            ks.reshape(dbsz, 1, N_KV_HEADS, HEAD_DIM), vs.reshape(dbsz, 1, N_KV_HEADS, HEAD_DIM))
```

```python
import functools
import math

import jax
import jax.numpy as jnp
from jax import lax
from jax.experimental import pallas as pl
from jax.experimental.pallas import tpu as pltpu

F32 = jnp.float32
BF16 = jnp.bfloat16

D_MODEL = 2048
LRU_WIDTH = 2048
LRU_BLOCKS = 8
LRU_BLOCK_W = LRU_WIDTH // LRU_BLOCKS
CONV_W = 4
LRU_C = 8.0
HEAD_DIM = 64
N_Q_HEADS = 32
N_KV_HEADS = 8
GROUP = N_Q_HEADS // N_KV_HEADS
ATT_WIDTH = N_Q_HEADS * HEAD_DIM
KV_WIDTH = N_KV_HEADS * HEAD_DIM
WINDOW = 128
BLOCK = WINDOW
N_BUCKETS = 32
MAX_DISTANCE = 128
RMS_EPS = 1e-6
NEG_INF = -1e30
LOG2_E = 1.4426950408889634
PAST_LEN = 16384

V7X_VMEM_BYTES = 64 * 1024 * 1024
VMEM_LIMIT = V7X_VMEM_BYTES - 8 * 1024 * 1024
SUBLANES = 8
LANES = 128
HEADS_PER_TILE = LANES // HEAD_DIM
SLABS = GROUP // HEADS_PER_TILE
ATT_ROWS = 32
ATT_TILE = 2 * BLOCK
SUB_ROWS = 256
W_CHUNK = 512
SEQS_PER_STEP = 8
MXU_COLS = 256


def _params(*semantics):
    return pltpu.CompilerParams(dimension_semantics=semantics, vmem_limit_bytes=VMEM_LIMIT)


def _resident(shape):
    zeros = (0,) * len(shape)
    return pl.BlockSpec(shape, lambda *_: zeros, pipeline_mode=pl.Buffered(1))


def _rms_scale(x):
    return lax.rsqrt(jnp.mean(x * x, axis=-1, keepdims=True) + RMS_EPS)


def _silu(x):
    h = 0.5 * x
    return h * jnp.tanh(h) + h


def _segment_major(rows, inverse=False):
    seg = rows // SUBLANES
    r = lax.broadcasted_iota(jnp.int32, (rows, rows), 0)
    c = lax.broadcasted_iota(jnp.int32, (rows, rows), 1)
    if inverse:
        src = (r % seg) * SUBLANES + r // seg
    else:
        src = (r % SUBLANES) * seg + r // SUBLANES
    return jnp.where(c == src, 1.0, 0.0).astype(BF16)


def _phase_specs(nchunk, tm, k):
    chunk_w = pl.BlockSpec((k, W_CHUNK), lambda i: (0, jnp.minimum(i, nchunk - 1)))
    chunk_o = lambda rows: pl.BlockSpec((rows, W_CHUNK), lambda i: (0, jnp.minimum(i, nchunk - 1)))
    tile = lambda n: pl.BlockSpec((tm, n), lambda i: (jnp.maximum(i - nchunk, 0), 0))
    return chunk_w, chunk_o, tile


def _dup_heads(x):
    low = lax.broadcasted_iota(jnp.int32, (x.shape[0], LANES), 1) < HEAD_DIM
    out = []
    for c in range(x.shape[1] // LANES):
        col = x[:, c * LANES:(c + 1) * LANES]
        swapped = pltpu.roll(col, HEAD_DIM, axis=1)
        out += [jnp.where(low, col, swapped), jnp.where(low, swapped, col)]
    return jnp.concatenate(out, axis=1)


def _norm_proj_kvq_kernel(nkv, nqg, xs_ref, x_ref, gkv_ref, gq_ref, wkv_ref, wqg_ref,
                          kvs_ref, qgs_ref, q_ref, gate_ref, kdup_ref, vt_ref, ktail_ref, vtail_ref,
                          wkv_scr, wqg_scr):
    i = pl.program_id(0)
    nchunk = nkv + nqg
    q_chunks = ATT_WIDTH // W_CHUNK
    q_scale = 1.0 / math.sqrt(HEAD_DIM)

    def sample_rows(g_ref):
        xs = xs_ref[...]
        return (xs * _rms_scale(xs) * g_ref[...]).astype(BF16)

    @pl.when(i < nkv)
    def _():
        wb = wkv_ref[...].astype(BF16)
        wkv_scr[i] = wb
        kvs_ref[...] = jnp.dot(sample_rows(gkv_ref), wb, preferred_element_type=F32)

    @pl.when((i >= nkv) & (i < nchunk))
    def _():
        c = i - nkv
        wb = wqg_ref[...].astype(BF16)
        wqg_scr[c] = wb
        r = jnp.dot(sample_rows(gq_ref), wb, preferred_element_type=F32)
        qgs_ref[...] = r * jnp.where(c < q_chunks, q_scale, 1.0)

    @pl.when(i >= nchunk)
    def _():
        tm = x_ref.shape[0]
        for rs in _row_blocks(tm):
            x = x_ref[rs, :]
            xh = x * _rms_scale(x)
            xkv = (xh * gkv_ref[...]).astype(BF16)
            xq = (xh * gq_ref[...]).astype(BF16)
            k = jnp.dot(xkv, wkv_scr[0], preferred_element_type=F32)
            v = jnp.dot(xkv, wkv_scr[1], preferred_element_type=F32)
            kdup_ref[rs, :] = _dup_heads(k).astype(BF16)
            vt_ref[:, rs] = _dup_heads(v).T.astype(BF16)
            for c in range(nqg):
                r = jnp.dot(xq, wqg_scr[c], preferred_element_type=F32)
                if c < q_chunks:
                    q_ref[rs, c * W_CHUNK:(c + 1) * W_CHUNK] = (r * q_scale).astype(q_ref.dtype)
                else:
                    cc = c - q_chunks
                    gate_ref[rs, cc * W_CHUNK:(cc + 1) * W_CHUNK] = r
        ktail_ref[0] = k[k.shape[0] - WINDOW:].T
        vtail_ref[0] = v[v.shape[0] - WINDOW:].T


def norm_proj_kvq(xs, x, g_kv, g_q, w_kv, w_qg, tm, seq_len):
    m, d = x.shape
    ns = xs.shape[0]
    assert w_kv.shape[1] == 2 * KV_WIDTH == 2 * W_CHUNK and seq_len % tm == 0 and tm >= WINDOW
    nkv, nqg = w_kv.shape[1] // W_CHUNK, w_qg.shape[1] // W_CHUNK
    nchunk = nkv + nqg
    tiles = seq_len // tm
    tile = lambda n: pl.BlockSpec((tm, n), lambda i: (jnp.maximum(i - nchunk, 0), 0))
    tail = pl.BlockSpec((1, KV_WIDTH, WINDOW), lambda i: (jnp.maximum(i - nchunk, 0) // tiles, 0, 0))
    kv_chunk = lambda i: (0, jnp.minimum(i, nkv - 1))
    qg_chunk = lambda i: (0, jnp.clip(i - nkv, 0, nqg - 1))
    return pl.pallas_call(
        functools.partial(_norm_proj_kvq_kernel, nkv, nqg),
        grid=(nchunk + m // tm,),
        in_specs=[
            _resident(xs.shape), tile(d), _resident((1, d)), _resident((1, d)),
            pl.BlockSpec((d, W_CHUNK), kv_chunk), pl.BlockSpec((d, W_CHUNK), qg_chunk),
        ],
        out_specs=[
            pl.BlockSpec((ns, W_CHUNK), kv_chunk), pl.BlockSpec((ns, W_CHUNK), qg_chunk),
            tile(ATT_WIDTH), tile(ATT_WIDTH), tile(2 * KV_WIDTH),
            pl.BlockSpec((2 * KV_WIDTH, tm), lambda i: (0, jnp.maximum(i - nchunk, 0))), tail, tail,
        ],
        out_shape=[
            jax.ShapeDtypeStruct((ns, w_kv.shape[1]), F32),
            jax.ShapeDtypeStruct((ns, w_qg.shape[1]), F32),
            jax.ShapeDtypeStruct((m, ATT_WIDTH), BF16),
            jax.ShapeDtypeStruct((m, ATT_WIDTH), F32),
            jax.ShapeDtypeStruct((m, 2 * KV_WIDTH), BF16),
            jax.ShapeDtypeStruct((2 * KV_WIDTH, m), BF16),
            jax.ShapeDtypeStruct((m // seq_len, KV_WIDTH, WINDOW), F32),
            jax.ShapeDtypeStruct((m // seq_len, KV_WIDTH, WINDOW), F32),
        ],
        scratch_shapes=[pltpu.VMEM((nkv, d, W_CHUNK), BF16), pltpu.VMEM((nqg, d, W_CHUNK), BF16)],
        compiler_params=_params("arbitrary"),
        name="norm_proj_kvq",
    )(xs, x, g_kv.reshape(1, d), g_q.reshape(1, d), w_kv, w_qg)


def _row_blocks(rows):
    sub = min(rows, SUB_ROWS)
    return [slice(r, r + sub) for r in range(0, rows, sub)]


def _proj_norm_res_kernel(nchunk, nparts, as_ref, gs_ref, xs_ref, *refs):
    y_refs = refs[:nparts]
    w_ref, g_ref, x_ref, os_ref, o_ref, w_scr, raw_scr = refs[nparts:]
    i = pl.program_id(0)

    @pl.when(i < nchunk)
    def _():
        wb = w_ref[...].astype(BF16)
        w_scr[i] = wb
        ys = (as_ref[...] * _silu(gs_ref[...])).astype(BF16)
        raw_scr[i] = jnp.dot(ys, wb, preferred_element_type=F32)

    @pl.when(i == nchunk - 1)
    def _():
        o = jnp.concatenate([raw_scr[c] for c in range(nchunk)], axis=1)
        os_ref[...] = xs_ref[...] + o * _rms_scale(o) * g_ref[...]

    @pl.when(i >= nchunk)
    def _():
        tm = x_ref.shape[0]
        part_rows = tm // nparts
        for rs in _row_blocks(tm):
            part, off = divmod(rs.start, part_rows)
            y = y_refs[part][off:off + rs.stop - rs.start, :]
            o = jnp.concatenate([jnp.dot(y, w_scr[c], preferred_element_type=F32)
                                 for c in range(nchunk)], axis=1)
            o_ref[rs, :] = x_ref[rs, :] + o * _rms_scale(o) * g_ref[...]


def proj_norm_res(a_s, gate_s, x_s, y_parts, w, g, x, tm):
    k, d = w.shape
    m = x.shape[0]
    nchunk = d // W_CHUNK
    nparts = len(y_parts)
    assert (tm // nparts) % min(tm, SUB_ROWS) == 0
    chunk_w, _, tile = _phase_specs(nchunk, tm, k)
    part = pl.BlockSpec((tm // nparts, k), lambda i: (jnp.maximum(i - nchunk, 0), 0))
    return pl.pallas_call(
        functools.partial(_proj_norm_res_kernel, nchunk, nparts),
        grid=(nchunk + m // tm,),
        in_specs=[_resident(a_s.shape), _resident(gate_s.shape), _resident(x_s.shape)]
        + [part] * nparts + [chunk_w, _resident((1, d)), tile(d)],
        out_specs=[pl.BlockSpec(x_s.shape, lambda i: (0, 0)), tile(d)],
        out_shape=[jax.ShapeDtypeStruct(x_s.shape, F32), jax.ShapeDtypeStruct((m, d), F32)],
        scratch_shapes=[pltpu.VMEM((nchunk, k, W_CHUNK), BF16),
                        pltpu.VMEM((nchunk, x_s.shape[0], W_CHUNK), F32)],
        compiler_params=_params("arbitrary"),
        name="proj_norm_res",
    )(a_s, gate_s, x_s, *y_parts, w, g.reshape(1, d), x)


def _lru_gates(conv, wr_half, br, wi_half, bi, lam):
    cb = conv.astype(BF16)
    th_r = jnp.tanh(jnp.dot(cb, wr_half, preferred_element_type=F32) + 0.5 * br)
    th_i = jnp.tanh(jnp.dot(cb, wi_half, preferred_element_type=F32) + 0.5 * bi)
    nl = -lam
    softplus = jnp.maximum(nl, 0.0) + jnp.log1p(jnp.exp(-jnp.abs(nl)))
    half = (0.5 * LRU_C) * softplus
    x = th_r * half + half
    a = jnp.exp2(x * -LOG2_E)
    z = jnp.tanh(x) * (a * a + 1.0)
    mult = z * lax.rsqrt(jnp.maximum(z, 1e-30))
    hc = 0.5 * conv
    return a, mult * (hc * th_i + hc)


def _interleave(*stages):
    live = [[stage, share] for stage, share in stages]
    while live:
        for entry in list(live):
            try:
                for _ in range(entry[1]):
                    next(entry[0])
            except StopIteration:
                live.remove(entry)


def _in_proj_tile(x_ref, rs, g_ref, w_scr, ug_ref):
    x = x_ref[rs, :]
    xn = (x * _rms_scale(x) * g_ref[...]).astype(BF16)
    xn = jnp.dot(_segment_major(xn.shape[0]), xn, preferred_element_type=F32).astype(BF16)
    for c in range(w_scr.shape[0]):
        for n0 in range(0, W_CHUNK, MXU_COLS):
            ug_ref[:, c * W_CHUNK + n0:c * W_CHUNK + n0 + MXU_COLS] = jnp.dot(
                xn, w_scr[c, :, n0:n0 + MXU_COLS], preferred_element_type=F32)
            yield


def _rglru_tile(ug_ref, y_ref, cw_ref, cb_ref, wr_ref, br_ref, wi_ref, bi_ref, lam_ref, h_scr, tail_scr):
    tc = ug_ref.shape[0]
    seg = tc // SUBLANES
    ntaps = CONV_W - 1
    bw = LRU_BLOCK_W
    sub = lax.broadcasted_iota(jnp.int32, (SUBLANES, bw), 0)
    first = sub == 0
    time_order = _segment_major(tc, inverse=True)

    def shift_in(x, row0):
        return jnp.where(first, row0, pltpu.roll(x, 1, axis=0))

    def group(x, j):
        return x[j * SUBLANES:(j + 1) * SUBLANES]

    for n in range(LRU_BLOCKS):
        cs = slice(n * bw, (n + 1) * bw)
        u = ug_ref[:, cs]
        tail = tail_scr[:, cs]
        before = [shift_in(group(u, seg - m), tail[ntaps - m:ntaps - m + 1])
                  for m in range(ntaps, 0, -1)]
        ext = jnp.concatenate(before + [u], axis=0)
        tail_scr[:, cs] = jnp.concatenate(
            [group(u, seg - m)[SUBLANES - 1:] for m in range(ntaps, 0, -1)], axis=0)
        cw = cw_ref[:, cs]
        conv = cb_ref[:, cs]
        for tap in range(CONV_W):
            conv = conv + ext[tap * SUBLANES:tap * SUBLANES + tc] * cw[tap:tap + 1]
        yield

        a, b = _lru_gates(conv, wr_ref[n], br_ref[:, cs], wi_ref[n], bi_ref[:, cs], lam_ref[:, cs])
        yield

        h = b[:SUBLANES]
        acc = a[:SUBLANES]
        h_loc, a_cum = [h], [acc]
        for j in range(1, seg):
            sl = slice(j * SUBLANES, (j + 1) * SUBLANES)
            h = a[sl] * h + b[sl]
            acc = a[sl] * acc
            h_loc.append(h)
            a_cum.append(acc)

        step = 1
        while step < SUBLANES:
            keep = sub >= step
            h = jnp.where(keep, acc * pltpu.roll(h, step, axis=0) + h, h)
            acc = jnp.where(keep, acc * pltpu.roll(acc, step, axis=0), acc)
            step *= 2
        h_prev = h_scr[:, cs]
        after = h + acc * h_prev
        h_in = shift_in(after, h_prev)
        h_scr[:, cs] = after[SUBLANES - 1:]
        yield

        hs = jnp.concatenate([h_loc[j] + a_cum[j] * h_in for j in range(seg)], axis=0)
        y = (hs * _silu(ug_ref[:, LRU_WIDTH + n * bw:LRU_WIDTH + (n + 1) * bw])).astype(BF16)
        y_ref[:, cs] = jnp.dot(time_order, y, preferred_element_type=F32).astype(y_ref.dtype)
        yield


def _rglru_front_kernel(nchunk, npairs, chunks, xs_ref, x_ref, g_ref, w_ref, cprev_ref, h0_ref,
                        cw_ref, cb_ref, wr_ref, br_ref, wi_ref, bi_ref, lam_ref,
                        os_ref, y_even_ref, y_odd_ref, cnew_ref, hlast_ref,
                        w_scr, ug0_scr, ug1_scr, h_scr, tail_scr):
    i = pl.program_id(0)
    p = i - nchunk
    tc = SUB_ROWS
    lru = (cw_ref, cb_ref, wr_ref, br_ref, wi_ref, bi_ref, lam_ref, h_scr, tail_scr)

    @pl.when(i < nchunk)
    def _():
        wb = w_ref[...].astype(BF16)
        w_scr[i] = wb
        xs = xs_ref[...]
        xsn = (xs * _rms_scale(xs) * g_ref[...]).astype(BF16)
        os_ref[...] = jnp.dot(xsn, wb, preferred_element_type=F32)

    @pl.when(i == nchunk - 1)
    def _():
        ug1_scr[...] = jnp.zeros_like(ug1_scr)
        h_scr[...] = jnp.zeros_like(h_scr)
        tail_scr[...] = jnp.zeros_like(tail_scr)

    @pl.when(p >= 0)
    def _():
        _interleave((_in_proj_tile(x_ref, slice(0, tc), g_ref, w_scr, ug0_scr), 1),
                    (_rglru_tile(ug1_scr, y_odd_ref, *lru), 2))
        hlast_ref[0] = h_scr[...]
        cnew_ref[0] = tail_scr[...]

    @pl.when((p >= 0) & (p < npairs))
    def _():
        @pl.when((2 * p) % chunks == 0)
        def _():
            h_scr[...] = h0_ref[0]
            tail_scr[...] = cprev_ref[0]

        _interleave((_in_proj_tile(x_ref, slice(tc, 2 * tc), g_ref, w_scr, ug1_scr), 1),
                    (_rglru_tile(ug0_scr, y_even_ref, *lru), 2))


def rglru_front(xs, x, g, w_in, conv_prev, h0, conv_w, conv_b, w_r, b_r, w_i, b_i, lam, seq_len):
    m, d = x.shape
    w = w_in.shape[1] // 2
    tc = SUB_ROWS
    nchunk = w_in.shape[1] // W_CHUNK
    bsz = m // seq_len
    chunks = seq_len // tc
    npairs = m // (2 * tc)
    assert seq_len % (2 * tc) == 0 and tc % (SUBLANES * SUBLANES) == 0 and tc // SUBLANES > CONV_W
    pair = lambda i: jnp.clip(i - nchunk, 0, npairs - 1)
    last = npairs * 2 - 1
    seq_in = lambda i: (jnp.clip(2 * (i - nchunk), 0, last) // chunks, 0, 0)
    seq_out = lambda i: (jnp.clip(2 * (i - nchunk) - 1, 0, last) // chunks, 0, 0)
    state_in = lambda rows: pl.BlockSpec((1, rows, w), seq_in)
    state_out = lambda rows: pl.BlockSpec((1, rows, w), seq_out)
    chunk = lambda i: (0, jnp.minimum(i, nchunk - 1))
    return pl.pallas_call(
        functools.partial(_rglru_front_kernel, nchunk, npairs, chunks),
        grid=(nchunk + npairs + 1,),
        in_specs=[_resident(xs.shape), pl.BlockSpec((2 * tc, d), lambda i: (pair(i), 0)), _resident((1, d)),
                  pl.BlockSpec((d, W_CHUNK), chunk), state_in(CONV_W - 1), state_in(1),
                  _resident((CONV_W, w)), _resident((1, w)), _resident(w_r.shape), _resident((1, w)),
                  _resident(w_i.shape), _resident((1, w)), _resident((1, w))],
        out_specs=[
            pl.BlockSpec((xs.shape[0], W_CHUNK), chunk),
            pl.BlockSpec((tc, w), lambda i: (pair(i), 0)),
            pl.BlockSpec((tc, w), lambda i: (jnp.clip(i - nchunk - 1, 0, npairs - 1), 0)),
            state_out(CONV_W - 1), state_out(1),
        ],
        out_shape=[
            jax.ShapeDtypeStruct((xs.shape[0], 2 * w), F32),
            jax.ShapeDtypeStruct((m // 2, w), BF16),
            jax.ShapeDtypeStruct((m // 2, w), BF16),
            jax.ShapeDtypeStruct((bsz, CONV_W - 1, w), F32),
            jax.ShapeDtypeStruct((bsz, 1, w), F32),
        ],
        scratch_shapes=[pltpu.VMEM((nchunk, d, W_CHUNK), BF16),
                        pltpu.VMEM((tc, 2 * w), F32), pltpu.VMEM((tc, 2 * w), F32),
                        pltpu.VMEM((1, w), F32), pltpu.VMEM((CONV_W - 1, w), F32)],
        compiler_params=_params("arbitrary"),
        name="rglru_front",
    )(xs, x, g.reshape(1, d), w_in, conv_prev, h0.reshape(bsz, 1, w), conv_w, conv_b.reshape(1, w),
      w_r, b_r.reshape(1, w), w_i, b_i.reshape(1, w), lam.reshape(1, w))


def _rglru_step_kernel(u_ref, cprev_ref, h0_ref, cw_ref, cb_ref, wr_ref, br_ref, wi_ref, bi_ref,
                       lam_ref, h_ref, cnew_ref):
    u = u_ref[...]
    cw = cw_ref[...]
    conv = cb_ref[...]
    for tap in range(CONV_W - 1):
        conv = conv + cprev_ref[tap] * cw[tap:tap + 1]
        if tap > 0:
            cnew_ref[tap - 1] = cprev_ref[tap]
    conv = conv + u * cw[CONV_W - 1:]
    cnew_ref[CONV_W - 2] = u
    a, b = _lru_gates(conv, wr_ref[0], br_ref[...], wi_ref[0], bi_ref[...], lam_ref[...])
    h_ref[...] = a * h0_ref[...] + b


def rglru_step(u, conv_prev_t, h0, conv_w, conv_b, w_r, b_r, w_i, b_i, lam):
    bsz, w = u.shape
    bw = LRU_BLOCK_W
    rows = pl.BlockSpec((bsz, bw), lambda n: (0, n))
    taps = pl.BlockSpec((CONV_W - 1, bsz, bw), lambda n: (0, 0, n))
    chan = lambda r: pl.BlockSpec((r, bw), lambda n: (0, n))
    blockw = pl.BlockSpec((1, bw, bw), lambda n: (n, 0, 0))
    return pl.pallas_call(
        _rglru_step_kernel,
        grid=(LRU_BLOCKS,),
        in_specs=[rows, taps, rows, chan(CONV_W), chan(1), blockw, chan(1), blockw, chan(1), chan(1)],
        out_specs=[rows, taps],
        out_shape=[
            jax.ShapeDtypeStruct((bsz, w), F32),
            jax.ShapeDtypeStruct((CONV_W - 1, bsz, w), F32),
        ],
        compiler_params=_params("parallel"),
        name="rglru_step",
    )(u, conv_prev_t, h0, conv_w, conv_b.reshape(1, w), w_r, b_r.reshape(1, w),
      w_i, b_i.reshape(1, w), lam.reshape(1, w))


def _buckets(dist):
    n = jnp.maximum(dist, 0)
    max_exact = N_BUCKETS // 2
    nf = jnp.maximum(n, 1).astype(F32)
    large = max_exact + jnp.floor(jnp.log(nf / max_exact) / math.log(MAX_DISTANCE / max_exact)
                                  * (N_BUCKETS - max_exact)).astype(jnp.int32)
    large = jnp.minimum(large, N_BUCKETS - 1)
    return jnp.where(n < max_exact, n, large)


def _lookup(bucket, valid, table_ref, head):
    bias = jnp.zeros(bucket.shape, F32)
    for b in range(N_BUCKETS):
        bias = jnp.where(bucket == b, table_ref[b, head], bias)
    return jnp.where(valid, bias, NEG_INF)


def _bias_kernel(table_ref, sinks_ref, band_ref, sinkt_ref, past_ref, new_ref):
    hk = pl.program_id(0)
    span = 3 * BLOCK
    dist = (lax.broadcasted_iota(jnp.int32, (1, span), 1) + BLOCK) % span
    bucket = _buckets(dist)
    in_window = (dist >= 0) & (dist < WINDOW)
    no_prev = lax.broadcasted_iota(jnp.int32, (2 * BLOCK, BLOCK), 0) < BLOCK

    def band(head):
        row = _lookup(bucket, in_window, table_ref, head)
        full = pltpu.roll(jnp.broadcast_to(row, (2 * BLOCK, span)), 0, axis=1, stride=1, stride_axis=0)
        return full[:, :BLOCK]

    rows = past_ref.shape[2]
    d_past = rows - lax.broadcasted_iota(jnp.int32, (1, rows), 1)
    b_past = _buckets(d_past)
    ok_past = (d_past >= 0) & (d_past < WINDOW)
    d_new = jnp.zeros((1, LANES), jnp.int32)
    b_new = _buckets(d_new)
    for par in range(HEADS_PER_TILE):
        for slab in range(SLABS):
            g = slab * HEADS_PER_TILE + par
            head = hk * GROUP + g
            rs = slice(par * 2 * BLOCK, (par + 1) * 2 * BLOCK)
            cs = slice(slab * BLOCK, (slab + 1) * BLOCK)
            bias = band(head)
            band_ref[0, 0, rs, cs] = bias
            band_ref[1, 0, rs, cs] = jnp.where(no_prev, NEG_INF, bias)
            sinkt_ref[0, par, :, cs] = jnp.full((1, BLOCK), sinks_ref[head], F32)
            past_ref[0, g:g + 1, :] = _lookup(b_past, ok_past, table_ref, head)
            new_ref[0, g:g + 1, :] = _lookup(b_new, d_new == 0, table_ref, head)


def bias_tables(table, sinks, past_rows):
    smem = pl.BlockSpec(memory_space=pltpu.SMEM)
    return pl.pallas_call(
        _bias_kernel,
        grid=(N_KV_HEADS,),
        in_specs=[smem, smem],
        out_specs=[
            pl.BlockSpec((2, 1, HEADS_PER_TILE * 2 * BLOCK, SLABS * BLOCK), lambda h: (0, h, 0, 0)),
            pl.BlockSpec((1, HEADS_PER_TILE, 1, SLABS * BLOCK), lambda h: (h, 0, 0, 0)),
            pl.BlockSpec((1, GROUP, past_rows), lambda h: (h, 0, 0)),
            pl.BlockSpec((1, GROUP, LANES), lambda h: (h, 0, 0)),
        ],
        out_shape=[
            jax.ShapeDtypeStruct((2, N_KV_HEADS, HEADS_PER_TILE * 2 * BLOCK, SLABS * BLOCK), F32),
            jax.ShapeDtypeStruct((N_KV_HEADS, HEADS_PER_TILE, 1, SLABS * BLOCK), F32),
            jax.ShapeDtypeStruct((N_KV_HEADS, GROUP, past_rows), F32),
            jax.ShapeDtypeStruct((N_KV_HEADS, GROUP, LANES), F32),
        ],
        compiler_params=_params("parallel"),
        name="bias_tables",
    )(table, sinks)


def _band_attn_kernel(q_ref, kp_ref, kc_ref, vp_ref, vc_ref, gate_ref, bias_ref, sink_ref, y_ref,
                      s_scr, p_scr):
    first_tile = (pl.program_id(1) == 0).astype(jnp.int32)
    nt = (((1,), (1,)), ((), ()))
    low = (lax.broadcasted_iota(jnp.int32, (1, LANES), 1) < HEAD_DIM)
    keep_low = low.astype(BF16)
    keep_high = 1 - keep_low
    nkeys = 2 * BLOCK
    zeros_v = jnp.zeros((HEAD_DIM, nkeys), BF16)
    rows = ATT_ROWS
    items = [(blk, hk) for blk in range(q_ref.shape[0] // BLOCK) for hk in range(N_KV_HEADS)]

    def rows_of(blk):
        return slice(blk * BLOCK, (blk + 1) * BLOCK)

    def scores(idx):
        blk, hk = items[idx]
        cs = slice(hk * LANES, (hk + 1) * LANES)
        k_prev = kp_ref[:, cs] if blk == 0 else kc_ref[rows_of(blk - 1), cs]
        kd = jnp.concatenate([k_prev, kc_ref[rows_of(blk), cs]], axis=0)
        lhs = jnp.concatenate([kd * keep_low, kd * keep_high], axis=0)
        qs = jnp.concatenate([q_ref[rows_of(blk), (hk * SLABS + s) * LANES:(hk * SLABS + s + 1) * LANES]
                              for s in range(SLABS)], axis=0)
        s_scr[idx % 2] = lax.dot_general(lhs, qs, nt, preferred_element_type=F32)

    scores(0)
    for idx, (blk, hk) in enumerate(items):
        slot = idx % 2
        first = first_tile if blk == 0 else 0
        if idx + 1 < len(items):
            scores(idx + 1)
        inv = []
        for par in range(HEADS_PER_TILE):
            base = par * nkeys
            sink = sink_ref[hk, par]
            top = None
            for r in range(base, base + nkeys, rows):
                sb = s_scr[slot, r:r + rows, :] + bias_ref[first, hk, r:r + rows, :]
                s_scr[slot, r:r + rows, :] = sb
                top = sb if top is None else jnp.maximum(top, sb)
            m = jnp.maximum(jnp.max(top, axis=0, keepdims=True), sink)
            tot = None
            for r in range(base, base + nkeys, rows):
                p = jnp.exp(s_scr[slot, r:r + rows, :] - m)
                p_scr[slot, r:r + rows, :] = p.astype(BF16)
                tot = p if tot is None else tot + p
            inv.append(1.0 / (jnp.sum(tot, axis=0, keepdims=True) + jnp.exp(sink - m)))
        cs = slice(hk * LANES, (hk + 1) * LANES)
        v_prev = vp_ref[cs, :] if blk == 0 else vc_ref[cs, rows_of(blk - 1)]
        vt = jnp.concatenate([v_prev, vc_ref[cs, rows_of(blk)]], axis=1)
        lhs_v = jnp.concatenate([
            jnp.concatenate([vt[:HEAD_DIM], zeros_v], axis=0),
            jnp.concatenate([zeros_v, vt[HEAD_DIM:]], axis=0)], axis=1)
        ot = jnp.dot(lhs_v, p_scr[slot], preferred_element_type=F32)
        ot = jnp.concatenate([ot[:HEAD_DIM] * inv[0], ot[HEAD_DIM:] * inv[1]], axis=0)
        o = ot.T
        for sl in range(SLABS):
            c0 = (hk * SLABS + sl) * LANES
            y_ref[rows_of(blk), c0:c0 + LANES] = (
                o[sl * BLOCK:(sl + 1) * BLOCK]
                * _silu(gate_ref[rows_of(blk), c0:c0 + LANES])).astype(y_ref.dtype)


def band_attention(q, kdup, vt, gate, bias_band, sink_t, bsz, t):
    m = q.shape[0]
    nblk = t // BLOCK
    per_tile = ATT_TILE // BLOCK
    ntile = t // ATT_TILE
    assert t % ATT_TILE == 0
    before = lambda b, i: b * nblk + jnp.maximum(per_tile * i - 1, 0)
    cur = lambda n: pl.BlockSpec((ATT_TILE, n), lambda b, i: (b * ntile + i, 0))
    prev = lambda n: pl.BlockSpec((BLOCK, n), lambda b, i: (before(b, i), 0))
    cur_t = pl.BlockSpec((2 * KV_WIDTH, ATT_TILE), lambda b, i: (0, b * ntile + i))
    prev_t = pl.BlockSpec((2 * KV_WIDTH, BLOCK), lambda b, i: (0, before(b, i)))
    score_tile = (HEADS_PER_TILE * 2 * BLOCK, SLABS * BLOCK)
    return pl.pallas_call(
        _band_attn_kernel,
        grid=(bsz, ntile),
        in_specs=[
            cur(ATT_WIDTH), prev(2 * KV_WIDTH), cur(2 * KV_WIDTH), prev_t, cur_t,
            cur(ATT_WIDTH), _resident(bias_band.shape), _resident(sink_t.shape),
        ],
        out_specs=cur(ATT_WIDTH),
        out_shape=jax.ShapeDtypeStruct((m, ATT_WIDTH), BF16),
        scratch_shapes=[pltpu.VMEM((2,) + score_tile, F32), pltpu.VMEM((2,) + score_tile, BF16)],
        compiler_params=_params("parallel", "parallel"),
        name="band_attention",
    )(q, kdup, kdup, vt, vt, gate, bias_band, sink_t)


def _cached_attn_kernel(q_ref, ckt_ref, cvt_ref, kn_ref, vn_ref, sinks_ref, bpast_ref, bnew_ref, o_ref):
    shape = (N_Q_HEADS, KV_WIDTH)
    lane_kv = lax.broadcasted_iota(jnp.int32, shape, 1) // HEAD_DIM
    row_kv = lax.broadcasted_iota(jnp.int32, shape, 0) // GROUP
    own = lane_kv == row_kv
    sink = sinks_ref[...]
    nt = (((1,), (1,)), ((), ()))
    for b in range(q_ref.shape[0]):
        q = q_ref[b]
        qt = jnp.concatenate([q] * N_KV_HEADS, axis=1)
        qm = jnp.where(own, qt, 0.0).astype(BF16)
        knew = kn_ref[b].astype(BF16).astype(F32)
        vnew = vn_ref[b].astype(BF16).astype(F32)
        s = jnp.dot(qm, ckt_ref[b].astype(BF16), preferred_element_type=F32) + bpast_ref[...]
        s_new = jnp.sum(qm.astype(F32) * knew, axis=-1, keepdims=True) + bnew_ref[:, :1]
        m = jnp.maximum(jnp.maximum(jnp.max(s, axis=-1, keepdims=True), s_new), sink)
        p = jnp.exp(s - m)
        p_new = jnp.exp(s_new - m)
        denom = jnp.sum(p, axis=-1, keepdims=True) + p_new + jnp.exp(sink - m)
        o_all = (lax.dot_general(p.astype(BF16), cvt_ref[b].astype(BF16), nt, preferred_element_type=F32)
                 + p_new.astype(BF16).astype(F32) * vnew)
        o_all = jnp.where(own, o_all, 0.0)
        o = o_all[:, :HEAD_DIM]
        for hk in range(1, N_KV_HEADS):
            o = o + o_all[:, hk * HEAD_DIM:(hk + 1) * HEAD_DIM]
        o_ref[b] = o / denom


def cached_attention(q, cache_kt, cache_vt, k_new, v_new, sinks, bias_past, bias_new):
    bsz, _, rows = cache_kt.shape
    nseq = math.gcd(bsz, SEQS_PER_STEP)
    per_seq = lambda r, n: pl.BlockSpec((nseq, r, n), lambda b: (b, 0, 0))
    return pl.pallas_call(
        _cached_attn_kernel,
        grid=(bsz // nseq,),
        in_specs=[
            per_seq(N_Q_HEADS, HEAD_DIM), per_seq(KV_WIDTH, rows), per_seq(KV_WIDTH, rows),
            per_seq(1, KV_WIDTH), per_seq(1, KV_WIDTH),
            _resident((N_Q_HEADS, 1)), _resident((N_Q_HEADS, rows)), _resident((N_Q_HEADS, LANES)),
        ],
        out_specs=per_seq(N_Q_HEADS, HEAD_DIM),
        out_shape=jax.ShapeDtypeStruct((bsz, N_Q_HEADS, HEAD_DIM), F32),
        compiler_params=_params("parallel"),
        name="cached_attention",
    )(q, cache_kt, cache_vt, k_new, v_new, sinks, bias_past, bias_new)


def kernel(x_prompt, x_sample, state_conv, state_h, cache_k, cache_v, a_norm_pre, a_norm_post,
           a_w_in, a_conv_w, a_conv_b, a_w_r, a_b_r, a_w_i, a_b_i, a_lambda, a_w_out, kv_norm, w_kv,
           b_norm_pre, b_norm_post, b_w_qg, b_sinks, b_w_out, rel_bias_table):
    bsz, t, d = x_prompt.shape
    dbsz, dt, _ = x_sample.shape
    assert a_w_in.shape[0] == 1 and b_w_qg.shape[0] == 1 and dt == 1
    assert t % BLOCK == 0 and t >= WINDOW
    past_rows = cache_k.shape[1]
    assert past_rows == min(WINDOW, PAST_LEN)

    w_r = (0.5 * a_w_r[0]).astype(BF16)
    w_i = (0.5 * a_w_i[0]).astype(BF16)
    sinks = b_sinks[0]
    bias_band, sink_t, bias_past, bias_new = bias_tables(rel_bias_table, sinks, past_rows)
    lru = (a_conv_w[0], a_conv_b[0], w_r, a_b_r[0], w_i, a_b_i[0], a_lambda[0])

    tm = 2 * SUB_ROWS
    xp = x_prompt.reshape(bsz * t, d)
    xs = x_sample.reshape(dbsz, d)

    conv0 = jnp.zeros((bsz, CONV_W - 1, LRU_WIDTH), F32)
    h0 = jnp.zeros((bsz, LRU_WIDTH), F32)
    ug_s, y_even, y_odd, p_conv, p_h = rglru_front(xs, xp, a_norm_pre[0], a_w_in[0], conv0, h0, *lru, seq_len=t)
    hs, s_conv_t = rglru_step(ug_s[:, :LRU_WIDTH], jnp.transpose(state_conv[0], (1, 0, 2)), state_h[0], *lru)
    xs1, x1 = proj_norm_res(hs, ug_s[:, LRU_WIDTH:], xs, (y_even, y_odd), a_w_out[0], a_norm_post[0], xp, tm)

    kv_s, qg_s, q, gate_b, kdup, vt, k_tail, v_tail = norm_proj_kvq(
        xs1, x1, kv_norm, b_norm_pre[0], w_kv, b_w_qg[0], SUB_ROWS, seq_len=t)
    ks, vs = kv_s[:, :KV_WIDTH], kv_s[:, KV_WIDTH:]
    cache_kt = jnp.transpose(cache_k, (0, 2, 3, 1)).reshape(dbsz, KV_WIDTH, past_rows)
    cache_vt = jnp.transpose(cache_v, (0, 2, 3, 1)).reshape(dbsz, KV_WIDTH, past_rows)
    os_ = cached_attention(qg_s[:, :ATT_WIDTH].reshape(dbsz, N_Q_HEADS, HEAD_DIM), cache_kt, cache_vt,
                           ks.reshape(dbsz, 1, KV_WIDTH), vs.reshape(dbsz, 1, KV_WIDTH),
                           sinks.reshape(N_Q_HEADS, 1), bias_past.reshape(N_Q_HEADS, past_rows),
                           bias_new.reshape(N_Q_HEADS, LANES))
    yb = band_attention(q, kdup, vt, gate_b, bias_band, sink_t, bsz, t)
    y_sample, y_prompt = proj_norm_res(os_.reshape(dbsz, ATT_WIDTH), qg_s[:, ATT_WIDTH:], xs1, (yb,),
                                       b_w_out[0], b_norm_post[0], x1, tm)
    y_prompt = y_prompt.reshape(bsz, t, d)
    p_k = jnp.transpose(k_tail.reshape(bsz, N_KV_HEADS, HEAD_DIM, WINDOW), (0, 3, 1, 2))
    p_v = jnp.transpose(v_tail.reshape(bsz, N_KV_HEADS, HEAD_DIM, WINDOW), (0, 3, 1, 2))

    return (y_prompt, y_sample.reshape(dbsz, 1, d),
            p_conv[None], p_h.reshape(1, bsz, LRU_WIDTH), p_k, p_v,
            jnp.transpose(s_conv_t, (1, 0, 2))[None], hs[None],
            ks.reshape(dbsz, 1, N_KV_HEADS, HEAD_DIM), vs.reshape(dbsz, 1, N_KV_HEADS, HEAD_DIM))
```

```python
import functools
import math

import jax
import jax.numpy as jnp
from jax import lax
from jax.experimental import pallas as pl
from jax.experimental.pallas import tpu as pltpu

F32 = jnp.float32
BF16 = jnp.bfloat16

D_MODEL = 2048
LRU_WIDTH = 2048
LRU_BLOCKS = 8
LRU_BLOCK_W = LRU_WIDTH // LRU_BLOCKS
CONV_W = 4
LRU_C = 8.0
HEAD_DIM = 64
N_Q_HEADS = 32
N_KV_HEADS = 8
GROUP = N_Q_HEADS // N_KV_HEADS
ATT_WIDTH = N_Q_HEADS * HEAD_DIM
KV_WIDTH = N_KV_HEADS * HEAD_DIM
WINDOW = 128
BLOCK = WINDOW
N_BUCKETS = 32
MAX_DISTANCE = 128
RMS_EPS = 1e-6
NEG_INF = -1e30
LOG2_E = 1.4426950408889634
PAST_LEN = 16384

V7X_VMEM_BYTES = 64 * 1024 * 1024
VMEM_LIMIT = V7X_VMEM_BYTES - 8 * 1024 * 1024
SUBLANES = 8
LANES = 128
HEADS_PER_TILE = LANES // HEAD_DIM
SLABS = GROUP // HEADS_PER_TILE
ATT_ROWS = 32
ATT_TILE = 2 * BLOCK
SUB_ROWS = 256
W_CHUNK = 512
SEQS_PER_STEP = 8
MXU_COLS = 256


def _params(*semantics):
    return pltpu.CompilerParams(dimension_semantics=semantics, vmem_limit_bytes=VMEM_LIMIT)


def _resident(shape):
    zeros = (0,) * len(shape)
    return pl.BlockSpec(shape, lambda *_: zeros, pipeline_mode=pl.Buffered(1))


def _rms_scale(x):
    return lax.rsqrt(jnp.mean(x * x, axis=-1, keepdims=True) + RMS_EPS)


def _silu(x):
    h = 0.5 * x
    return h * jnp.tanh(h) + h


def _segment_major(rows, inverse=False):
    seg = rows // SUBLANES
    r = lax.broadcasted_iota(jnp.int32, (rows, rows), 0)
    c = lax.broadcasted_iota(jnp.int32, (rows, rows), 1)
    if inverse:
        src = (r % seg) * SUBLANES + r // seg
    else:
        src = (r % SUBLANES) * seg + r // SUBLANES
    return jnp.where(c == src, 1.0, 0.0).astype(BF16)


def _phase_specs(nchunk, tm, k):
    chunk_w = pl.BlockSpec((k, W_CHUNK), lambda i: (0, jnp.minimum(i, nchunk - 1)))
    chunk_o = lambda rows: pl.BlockSpec((rows, W_CHUNK), lambda i: (0, jnp.minimum(i, nchunk - 1)))
    tile = lambda n: pl.BlockSpec((tm, n), lambda i: (jnp.maximum(i - nchunk, 0), 0))
    return chunk_w, chunk_o, tile


def _dup_heads(x):
    low = lax.broadcasted_iota(jnp.int32, (x.shape[0], LANES), 1) < HEAD_DIM
    out = []
    for c in range(x.shape[1] // LANES):
        col = x[:, c * LANES:(c + 1) * LANES]
        swapped = pltpu.roll(col, HEAD_DIM, axis=1)
        out += [jnp.where(low, col, swapped), jnp.where(low, swapped, col)]
    return jnp.concatenate(out, axis=1)


def _norm_proj_kvq_kernel(nkv, nqg, xs_ref, x_ref, gkv_ref, gq_ref, wkv_ref, wqg_ref,
                          kvs_ref, qgs_ref, q_ref, gate_ref, kdup_ref, vt_ref, ktail_ref, vtail_ref,
                          wkv_scr, wqg_scr):
    i = pl.program_id(0)
    nchunk = nkv + nqg
    q_chunks = ATT_WIDTH // W_CHUNK
    q_scale = 1.0 / math.sqrt(HEAD_DIM)
    q_scale_log2 = q_scale * LOG2_E

    def sample_rows(g_ref):
        xs = xs_ref[...]
        return (xs * _rms_scale(xs) * g_ref[...]).astype(BF16)

    @pl.when(i < nkv)
    def _():
        wb = wkv_ref[...].astype(BF16)
        wkv_scr[i] = wb
        kvs_ref[...] = jnp.dot(sample_rows(gkv_ref), wb, preferred_element_type=F32)

    @pl.when((i >= nkv) & (i < nchunk))
    def _():
        c = i - nkv
        wb = wqg_ref[...].astype(BF16)
        wqg_scr[c] = wb
        r = jnp.dot(sample_rows(gq_ref), wb, preferred_element_type=F32)
        qgs_ref[...] = r * jnp.where(c < q_chunks, q_scale, 1.0)

    @pl.when(i >= nchunk)
    def _():
        tm = x_ref.shape[0]
        for rs in _row_blocks(tm):
            x = x_ref[rs, :]
            xh = x * _rms_scale(x)
            xkv = (xh * gkv_ref[...]).astype(BF16)
            xq = (xh * gq_ref[...]).astype(BF16)
            k = jnp.dot(xkv, wkv_scr[0], preferred_element_type=F32)
            v = jnp.dot(xkv, wkv_scr[1], preferred_element_type=F32)
            kdup_ref[rs, :] = _dup_heads(k).astype(BF16)
            vt_ref[:, rs] = _dup_heads(v).T.astype(BF16)
            for c in range(nqg):
                r = jnp.dot(xq, wqg_scr[c], preferred_element_type=F32)
                if c < q_chunks:
                    q_ref[rs, c * W_CHUNK:(c + 1) * W_CHUNK] = (r * q_scale_log2).astype(q_ref.dtype)
                else:
                    cc = c - q_chunks
                    gate_ref[rs, cc * W_CHUNK:(cc + 1) * W_CHUNK] = r
        ktail_ref[0] = k[k.shape[0] - WINDOW:].T
        vtail_ref[0] = v[v.shape[0] - WINDOW:].T


def norm_proj_kvq(xs, x, g_kv, g_q, w_kv, w_qg, tm, seq_len):
    m, d = x.shape
    ns = xs.shape[0]
    assert w_kv.shape[1] == 2 * KV_WIDTH == 2 * W_CHUNK and seq_len % tm == 0 and tm >= WINDOW
    nkv, nqg = w_kv.shape[1] // W_CHUNK, w_qg.shape[1] // W_CHUNK
    nchunk = nkv + nqg
    tiles = seq_len // tm
    tile = lambda n: pl.BlockSpec((tm, n), lambda i: (jnp.maximum(i - nchunk, 0), 0))
    tail = pl.BlockSpec((1, KV_WIDTH, WINDOW), lambda i: (jnp.maximum(i - nchunk, 0) // tiles, 0, 0))
    kv_chunk = lambda i: (0, jnp.minimum(i, nkv - 1))
    qg_chunk = lambda i: (0, jnp.clip(i - nkv, 0, nqg - 1))
    return pl.pallas_call(
        functools.partial(_norm_proj_kvq_kernel, nkv, nqg),
        grid=(nchunk + m // tm,),
        in_specs=[
            _resident(xs.shape), tile(d), _resident((1, d)), _resident((1, d)),
            pl.BlockSpec((d, W_CHUNK), kv_chunk), pl.BlockSpec((d, W_CHUNK), qg_chunk),
        ],
        out_specs=[
            pl.BlockSpec((ns, W_CHUNK), kv_chunk), pl.BlockSpec((ns, W_CHUNK), qg_chunk),
            tile(ATT_WIDTH), tile(ATT_WIDTH), tile(2 * KV_WIDTH),
            pl.BlockSpec((2 * KV_WIDTH, tm), lambda i: (0, jnp.maximum(i - nchunk, 0))), tail, tail,
        ],
        out_shape=[
            jax.ShapeDtypeStruct((ns, w_kv.shape[1]), F32),
            jax.ShapeDtypeStruct((ns, w_qg.shape[1]), F32),
            jax.ShapeDtypeStruct((m, ATT_WIDTH), BF16),
            jax.ShapeDtypeStruct((m, ATT_WIDTH), F32),
            jax.ShapeDtypeStruct((m, 2 * KV_WIDTH), BF16),
            jax.ShapeDtypeStruct((2 * KV_WIDTH, m), BF16),
            jax.ShapeDtypeStruct((m // seq_len, KV_WIDTH, WINDOW), F32),
            jax.ShapeDtypeStruct((m // seq_len, KV_WIDTH, WINDOW), F32),
        ],
        scratch_shapes=[pltpu.VMEM((nkv, d, W_CHUNK), BF16), pltpu.VMEM((nqg, d, W_CHUNK), BF16)],
        compiler_params=_params("arbitrary"),
        name="norm_proj_kvq",
    )(xs, x, g_kv.reshape(1, d), g_q.reshape(1, d), w_kv, w_qg)


def _row_blocks(rows):
    sub = min(rows, SUB_ROWS)
    return [slice(r, r + sub) for r in range(0, rows, sub)]


def _proj_norm_res_kernel(nchunk, nparts, as_ref, gs_ref, xs_ref, *refs):
    y_refs = refs[:nparts]
    w_ref, g_ref, x_ref, os_ref, o_ref, w_scr, raw_scr = refs[nparts:]
    i = pl.program_id(0)

    @pl.when(i < nchunk)
    def _():
        wb = w_ref[...].astype(BF16)
        w_scr[i] = wb
        ys = (as_ref[...] * _silu(gs_ref[...])).astype(BF16)
        raw_scr[i] = jnp.dot(ys, wb, preferred_element_type=F32)

    @pl.when(i == nchunk - 1)
    def _():
        o = jnp.concatenate([raw_scr[c] for c in range(nchunk)], axis=1)
        os_ref[...] = xs_ref[...] + o * _rms_scale(o) * g_ref[...]

    @pl.when(i >= nchunk)
    def _():
        tm = x_ref.shape[0]
        part_rows = tm // nparts
        for rs in _row_blocks(tm):
            part, off = divmod(rs.start, part_rows)
            y = y_refs[part][off:off + rs.stop - rs.start, :]
            o = jnp.concatenate([jnp.dot(y, w_scr[c], preferred_element_type=F32)
                                 for c in range(nchunk)], axis=1)
            o_ref[rs, :] = x_ref[rs, :] + o * _rms_scale(o) * g_ref[...]


def proj_norm_res(a_s, gate_s, x_s, y_parts, w, g, x, tm):
    k, d = w.shape
    m = x.shape[0]
    nchunk = d // W_CHUNK
    nparts = len(y_parts)
    assert (tm // nparts) % min(tm, SUB_ROWS) == 0
    chunk_w, _, tile = _phase_specs(nchunk, tm, k)
    part = pl.BlockSpec((tm // nparts, k), lambda i: (jnp.maximum(i - nchunk, 0), 0))
    return pl.pallas_call(
        functools.partial(_proj_norm_res_kernel, nchunk, nparts),
        grid=(nchunk + m // tm,),
        in_specs=[_resident(a_s.shape), _resident(gate_s.shape), _resident(x_s.shape)]
        + [part] * nparts + [chunk_w, _resident((1, d)), tile(d)],
        out_specs=[pl.BlockSpec(x_s.shape, lambda i: (0, 0)), tile(d)],
        out_shape=[jax.ShapeDtypeStruct(x_s.shape, F32), jax.ShapeDtypeStruct((m, d), F32)],
        scratch_shapes=[pltpu.VMEM((nchunk, k, W_CHUNK), BF16),
                        pltpu.VMEM((nchunk, x_s.shape[0], W_CHUNK), F32)],
        compiler_params=_params("arbitrary"),
        name="proj_norm_res",
    )(a_s, gate_s, x_s, *y_parts, w, g.reshape(1, d), x)


def _lru_gates(conv, wr_half, br, wi_half, bi, lam):
    cb = conv.astype(BF16)
    th_r = jnp.tanh(jnp.dot(cb, wr_half, preferred_element_type=F32) + 0.5 * br)
    th_i = jnp.tanh(jnp.dot(cb, wi_half, preferred_element_type=F32) + 0.5 * bi)
    nl = -lam
    softplus = jnp.maximum(nl, 0.0) + jnp.log1p(jnp.exp(-jnp.abs(nl)))
    half = (0.5 * LRU_C) * softplus
    x = th_r * half + half
    a = jnp.exp2(x * -LOG2_E)
    z = jnp.tanh(x) * (a * a + 1.0)
    mult = z * lax.rsqrt(jnp.maximum(z, 1e-30))
    hc = 0.5 * conv
    return a, mult * (hc * th_i + hc)


def _interleave(*stages):
    live = [[stage, share] for stage, share in stages]
    while live:
        for entry in list(live):
            try:
                for _ in range(entry[1]):
                    next(entry[0])
            except StopIteration:
                live.remove(entry)


def _in_proj_tile(x_ref, rs, g_ref, w_scr, ug_ref):
    x = x_ref[rs, :]
    xn = (x * _rms_scale(x) * g_ref[...]).astype(BF16)
    xn = jnp.dot(_segment_major(xn.shape[0]), xn, preferred_element_type=F32).astype(BF16)
    for c in range(w_scr.shape[0]):
        for n0 in range(0, W_CHUNK, MXU_COLS):
            ug_ref[:, c * W_CHUNK + n0:c * W_CHUNK + n0 + MXU_COLS] = jnp.dot(
                xn, w_scr[c, :, n0:n0 + MXU_COLS], preferred_element_type=F32)
            yield


def _rglru_tile(ug_ref, y_ref, cw_ref, cb_ref, wr_ref, br_ref, wi_ref, bi_ref, lam_ref, h_scr, tail_scr):
    tc = ug_ref.shape[0]
    seg = tc // SUBLANES
    ntaps = CONV_W - 1
    bw = LRU_BLOCK_W
    sub = lax.broadcasted_iota(jnp.int32, (SUBLANES, bw), 0)
    first = sub == 0
    time_order = _segment_major(tc, inverse=True)

    def shift_in(x, row0):
        return jnp.where(first, row0, pltpu.roll(x, 1, axis=0))

    def group(x, j):
        return x[j * SUBLANES:(j + 1) * SUBLANES]

    for n in range(LRU_BLOCKS):
        cs = slice(n * bw, (n + 1) * bw)
        u = ug_ref[:, cs]
        tail = tail_scr[:, cs]
        before = [shift_in(group(u, seg - m), tail[ntaps - m:ntaps - m + 1])
                  for m in range(ntaps, 0, -1)]
        ext = jnp.concatenate(before + [u], axis=0)
        tail_scr[:, cs] = jnp.concatenate(
            [group(u, seg - m)[SUBLANES - 1:] for m in range(ntaps, 0, -1)], axis=0)
        cw = cw_ref[:, cs]
        conv = cb_ref[:, cs]
        for tap in range(CONV_W):
            conv = conv + ext[tap * SUBLANES:tap * SUBLANES + tc] * cw[tap:tap + 1]
        yield

        a, b = _lru_gates(conv, wr_ref[n], br_ref[:, cs], wi_ref[n], bi_ref[:, cs], lam_ref[:, cs])
        yield

        h = b[:SUBLANES]
        acc = a[:SUBLANES]
        h_loc, a_cum = [h], [acc]
        for j in range(1, seg):
            sl = slice(j * SUBLANES, (j + 1) * SUBLANES)
            h = a[sl] * h + b[sl]
            acc = a[sl] * acc
            h_loc.append(h)
            a_cum.append(acc)

        step = 1
        while step < SUBLANES:
            keep = sub >= step
            h = jnp.where(keep, acc * pltpu.roll(h, step, axis=0) + h, h)
            acc = jnp.where(keep, acc * pltpu.roll(acc, step, axis=0), acc)
            step *= 2
        h_prev = h_scr[:, cs]
        after = h + acc * h_prev
        h_in = shift_in(after, h_prev)
        h_scr[:, cs] = after[SUBLANES - 1:]
        yield

        hs = jnp.concatenate([h_loc[j] + a_cum[j] * h_in for j in range(seg)], axis=0)
        y = (hs * _silu(ug_ref[:, LRU_WIDTH + n * bw:LRU_WIDTH + (n + 1) * bw])).astype(BF16)
        y_ref[:, cs] = jnp.dot(time_order, y, preferred_element_type=F32).astype(y_ref.dtype)
        yield


def _rglru_front_kernel(nchunk, npairs, chunks, xs_ref, x_ref, g_ref, w_ref, cprev_ref, h0_ref,
                        cw_ref, cb_ref, wr_ref, br_ref, wi_ref, bi_ref, lam_ref,
                        os_ref, y_even_ref, y_odd_ref, cnew_ref, hlast_ref,
                        w_scr, ug0_scr, ug1_scr, h_scr, tail_scr):
    i = pl.program_id(0)
    p = i - nchunk
    tc = SUB_ROWS
    lru = (cw_ref, cb_ref, wr_ref, br_ref, wi_ref, bi_ref, lam_ref, h_scr, tail_scr)

    @pl.when(i < nchunk)
    def _():
        wb = w_ref[...].astype(BF16)
        w_scr[i] = wb
        xs = xs_ref[...]
        xsn = (xs * _rms_scale(xs) * g_ref[...]).astype(BF16)
        os_ref[...] = jnp.dot(xsn, wb, preferred_element_type=F32)

    @pl.when(i == nchunk - 1)
    def _():
        ug1_scr[...] = jnp.zeros_like(ug1_scr)
        h_scr[...] = jnp.zeros_like(h_scr)
        tail_scr[...] = jnp.zeros_like(tail_scr)

    @pl.when(p >= 0)
    def _():
        _interleave((_in_proj_tile(x_ref, slice(0, tc), g_ref, w_scr, ug0_scr), 1),
                    (_rglru_tile(ug1_scr, y_odd_ref, *lru), 2))
        hlast_ref[0] = h_scr[...]
        cnew_ref[0] = tail_scr[...]

    @pl.when((p >= 0) & (p < npairs))
    def _():
        @pl.when((2 * p) % chunks == 0)
        def _():
            h_scr[...] = h0_ref[0]
            tail_scr[...] = cprev_ref[0]

        _interleave((_in_proj_tile(x_ref, slice(tc, 2 * tc), g_ref, w_scr, ug1_scr), 1),
                    (_rglru_tile(ug0_scr, y_even_ref, *lru), 2))


def rglru_front(xs, x, g, w_in, conv_prev, h0, conv_w, conv_b, w_r, b_r, w_i, b_i, lam, seq_len):
    m, d = x.shape
    w = w_in.shape[1] // 2
    tc = SUB_ROWS
    nchunk = w_in.shape[1] // W_CHUNK
    bsz = m // seq_len
    chunks = seq_len // tc
    npairs = m // (2 * tc)
    assert seq_len % (2 * tc) == 0 and tc % (SUBLANES * SUBLANES) == 0 and tc // SUBLANES > CONV_W
    pair = lambda i: jnp.clip(i - nchunk, 0, npairs - 1)
    last = npairs * 2 - 1
    seq_in = lambda i: (jnp.clip(2 * (i - nchunk), 0, last) // chunks, 0, 0)
    seq_out = lambda i: (jnp.clip(2 * (i - nchunk) - 1, 0, last) // chunks, 0, 0)
    state_in = lambda rows: pl.BlockSpec((1, rows, w), seq_in)
    state_out = lambda rows: pl.BlockSpec((1, rows, w), seq_out)
    chunk = lambda i: (0, jnp.minimum(i, nchunk - 1))
    return pl.pallas_call(
        functools.partial(_rglru_front_kernel, nchunk, npairs, chunks),
        grid=(nchunk + npairs + 1,),
        in_specs=[_resident(xs.shape), pl.BlockSpec((2 * tc, d), lambda i: (pair(i), 0)), _resident((1, d)),
                  pl.BlockSpec((d, W_CHUNK), chunk), state_in(CONV_W - 1), state_in(1),
                  _resident((CONV_W, w)), _resident((1, w)), _resident(w_r.shape), _resident((1, w)),
                  _resident(w_i.shape), _resident((1, w)), _resident((1, w))],
        out_specs=[
            pl.BlockSpec((xs.shape[0], W_CHUNK), chunk),
            pl.BlockSpec((tc, w), lambda i: (pair(i), 0)),
            pl.BlockSpec((tc, w), lambda i: (jnp.clip(i - nchunk - 1, 0, npairs - 1), 0)),
            state_out(CONV_W - 1), state_out(1),
        ],
        out_shape=[
            jax.ShapeDtypeStruct((xs.shape[0], 2 * w), F32),
            jax.ShapeDtypeStruct((m // 2, w), BF16),
            jax.ShapeDtypeStruct((m // 2, w), BF16),
            jax.ShapeDtypeStruct((bsz, CONV_W - 1, w), F32),
            jax.ShapeDtypeStruct((bsz, 1, w), F32),
        ],
        scratch_shapes=[pltpu.VMEM((nchunk, d, W_CHUNK), BF16),
                        pltpu.VMEM((tc, 2 * w), F32), pltpu.VMEM((tc, 2 * w), F32),
                        pltpu.VMEM((1, w), F32), pltpu.VMEM((CONV_W - 1, w), F32)],
        compiler_params=_params("arbitrary"),
        name="rglru_front",
    )(xs, x, g.reshape(1, d), w_in, conv_prev, h0.reshape(bsz, 1, w), conv_w, conv_b.reshape(1, w),
      w_r, b_r.reshape(1, w), w_i, b_i.reshape(1, w), lam.reshape(1, w))


def _rglru_step_kernel(u_ref, cprev_ref, h0_ref, cw_ref, cb_ref, wr_ref, br_ref, wi_ref, bi_ref,
                       lam_ref, h_ref, cnew_ref):
    u = u_ref[...]
    cw = cw_ref[...]
    conv = cb_ref[...]
    for tap in range(CONV_W - 1):
        conv = conv + cprev_ref[tap] * cw[tap:tap + 1]
        if tap > 0:
            cnew_ref[tap - 1] = cprev_ref[tap]
    conv = conv + u * cw[CONV_W - 1:]
    cnew_ref[CONV_W - 2] = u
    a, b = _lru_gates(conv, wr_ref[0], br_ref[...], wi_ref[0], bi_ref[...], lam_ref[...])
    h_ref[...] = a * h0_ref[...] + b


def rglru_step(u, conv_prev_t, h0, conv_w, conv_b, w_r, b_r, w_i, b_i, lam):
    bsz, w = u.shape
    bw = LRU_BLOCK_W
    rows = pl.BlockSpec((bsz, bw), lambda n: (0, n))
    taps = pl.BlockSpec((CONV_W - 1, bsz, bw), lambda n: (0, 0, n))
    chan = lambda r: pl.BlockSpec((r, bw), lambda n: (0, n))
    blockw = pl.BlockSpec((1, bw, bw), lambda n: (n, 0, 0))
    return pl.pallas_call(
        _rglru_step_kernel,
        grid=(LRU_BLOCKS,),
        in_specs=[rows, taps, rows, chan(CONV_W), chan(1), blockw, chan(1), blockw, chan(1), chan(1)],
        out_specs=[rows, taps],
        out_shape=[
            jax.ShapeDtypeStruct((bsz, w), F32),
            jax.ShapeDtypeStruct((CONV_W - 1, bsz, w), F32),
        ],
        compiler_params=_params("parallel"),
        name="rglru_step",
    )(u, conv_prev_t, h0, conv_w, conv_b.reshape(1, w), w_r, b_r.reshape(1, w),
      w_i, b_i.reshape(1, w), lam.reshape(1, w))


def _buckets(dist):
    n = jnp.maximum(dist, 0)
    max_exact = N_BUCKETS // 2
    nf = jnp.maximum(n, 1).astype(F32)
    large = max_exact + jnp.floor(jnp.log(nf / max_exact) / math.log(MAX_DISTANCE / max_exact)
                                  * (N_BUCKETS - max_exact)).astype(jnp.int32)
    large = jnp.minimum(large, N_BUCKETS - 1)
    return jnp.where(n < max_exact, n, large)


def _lookup(bucket, valid, table_ref, head):
    bias = jnp.zeros(bucket.shape, F32)
    for b in range(N_BUCKETS):
        bias = jnp.where(bucket == b, table_ref[b, head], bias)
    return jnp.where(valid, bias, NEG_INF)


def _bias_kernel(table_ref, sinks_ref, band_ref, sinkt_ref, past_ref, new_ref):
    hk = pl.program_id(0)
    span = 3 * BLOCK
    dist = (lax.broadcasted_iota(jnp.int32, (1, span), 1) + BLOCK) % span
    bucket = _buckets(dist)
    in_window = (dist >= 0) & (dist < WINDOW)
    no_prev = lax.broadcasted_iota(jnp.int32, (2 * BLOCK, BLOCK), 0) < BLOCK

    def band(head):
        row = _lookup(bucket, in_window, table_ref, head) * LOG2_E
        full = pltpu.roll(jnp.broadcast_to(row, (2 * BLOCK, span)), 0, axis=1, stride=1, stride_axis=0)
        return full[:, :BLOCK]

    rows = past_ref.shape[2]
    d_past = rows - lax.broadcasted_iota(jnp.int32, (1, rows), 1)
    b_past = _buckets(d_past)
    ok_past = (d_past >= 0) & (d_past < WINDOW)
    d_new = jnp.zeros((1, LANES), jnp.int32)
    b_new = _buckets(d_new)
    for par in range(HEADS_PER_TILE):
        for slab in range(SLABS):
            g = slab * HEADS_PER_TILE + par
            head = hk * GROUP + g
            rs = slice(par * 2 * BLOCK, (par + 1) * 2 * BLOCK)
            cs = slice(slab * BLOCK, (slab + 1) * BLOCK)
            bias = band(head)
            band_ref[0, 0, rs, cs] = bias
            band_ref[1, 0, rs, cs] = jnp.where(no_prev, NEG_INF, bias)
            sinkt_ref[0, par, :, cs] = jnp.full((1, BLOCK), sinks_ref[head] * LOG2_E, F32)
            past_ref[0, g:g + 1, :] = _lookup(b_past, ok_past, table_ref, head)
            new_ref[0, g:g + 1, :] = _lookup(b_new, d_new == 0, table_ref, head)


def bias_tables(table, sinks, past_rows):
    smem = pl.BlockSpec(memory_space=pltpu.SMEM)
    return pl.pallas_call(
        _bias_kernel,
        grid=(N_KV_HEADS,),
        in_specs=[smem, smem],
        out_specs=[
            pl.BlockSpec((2, 1, HEADS_PER_TILE * 2 * BLOCK, SLABS * BLOCK), lambda h: (0, h, 0, 0)),
            pl.BlockSpec((1, HEADS_PER_TILE, 1, SLABS * BLOCK), lambda h: (h, 0, 0, 0)),
            pl.BlockSpec((1, GROUP, past_rows), lambda h: (h, 0, 0)),
            pl.BlockSpec((1, GROUP, LANES), lambda h: (h, 0, 0)),
        ],
        out_shape=[
            jax.ShapeDtypeStruct((2, N_KV_HEADS, HEADS_PER_TILE * 2 * BLOCK, SLABS * BLOCK), F32),
            jax.ShapeDtypeStruct((N_KV_HEADS, HEADS_PER_TILE, 1, SLABS * BLOCK), F32),
            jax.ShapeDtypeStruct((N_KV_HEADS, GROUP, past_rows), F32),
            jax.ShapeDtypeStruct((N_KV_HEADS, GROUP, LANES), F32),
        ],
        compiler_params=_params("parallel"),
        name="bias_tables",
    )(table, sinks)


def _band_attn_kernel(q_ref, kp_ref, kc_ref, vp_ref, vc_ref, gate_ref, bias_ref, sink_ref, y_ref,
                      s_scr, p_scr):
    first_tile = (pl.program_id(1) == 0).astype(jnp.int32)
    nt = (((1,), (1,)), ((), ()))
    low = (lax.broadcasted_iota(jnp.int32, (1, LANES), 1) < HEAD_DIM)
    keep_low = low.astype(BF16)
    keep_high = 1 - keep_low
    nkeys = 2 * BLOCK
    zeros_v = jnp.zeros((HEAD_DIM, nkeys), BF16)
    ones_rows = jnp.where(
        lax.broadcasted_iota(jnp.int32, (2 * SUBLANES, HEADS_PER_TILE * nkeys), 0)
        == lax.broadcasted_iota(jnp.int32, (2 * SUBLANES, HEADS_PER_TILE * nkeys), 1) // nkeys,
        1.0, 0.0).astype(BF16)
    rows = ATT_ROWS
    items = [(blk, hk) for blk in range(q_ref.shape[0] // BLOCK) for hk in range(N_KV_HEADS)]

    def rows_of(blk):
        return slice(blk * BLOCK, (blk + 1) * BLOCK)

    def scores(idx):
        blk, hk = items[idx]
        cs = slice(hk * LANES, (hk + 1) * LANES)
        k_prev = kp_ref[:, cs] if blk == 0 else kc_ref[rows_of(blk - 1), cs]
        kd = jnp.concatenate([k_prev, kc_ref[rows_of(blk), cs]], axis=0)
        lhs = jnp.concatenate([kd * keep_low, kd * keep_high], axis=0)
        qs = jnp.concatenate([q_ref[rows_of(blk), (hk * SLABS + s) * LANES:(hk * SLABS + s + 1) * LANES]
                              for s in range(SLABS)], axis=0)
        s_scr[idx % 2] = lax.dot_general(lhs, qs, nt, preferred_element_type=F32)

    scores(0)
    for idx, (blk, hk) in enumerate(items):
        slot = idx % 2
        first = first_tile if blk == 0 else 0
        if idx + 1 < len(items):
            scores(idx + 1)
        sink_w = []
        for par in range(HEADS_PER_TILE):
            base = par * nkeys
            sink = sink_ref[hk, par]
            top = None
            for r in range(base, base + nkeys, rows):
                sb = s_scr[slot, r:r + rows, :] + bias_ref[first, hk, r:r + rows, :]
                s_scr[slot, r:r + rows, :] = sb
                top = sb if top is None else jnp.maximum(top, sb)
            m = jnp.maximum(jnp.max(top, axis=0, keepdims=True), sink)
            for r in range(base, base + nkeys, rows):
                p_scr[slot, r:r + rows, :] = jnp.exp2(s_scr[slot, r:r + rows, :] - m).astype(BF16)
            sink_w.append(jnp.exp2(sink - m))
        cs = slice(hk * LANES, (hk + 1) * LANES)
        v_prev = vp_ref[cs, :] if blk == 0 else vc_ref[cs, rows_of(blk - 1)]
        vt = jnp.concatenate([v_prev, vc_ref[cs, rows_of(blk)]], axis=1)
        lhs_v = jnp.concatenate([
            jnp.concatenate([vt[:HEAD_DIM], zeros_v, ones_rows[:, :nkeys]], axis=0),
            jnp.concatenate([zeros_v, vt[HEAD_DIM:], ones_rows[:, nkeys:]], axis=0)], axis=1)
        ot = jnp.dot(lhs_v, p_scr[slot], preferred_element_type=F32)
        inv = [1.0 / (ot[LANES + par:LANES + par + 1] + sink_w[par]) for par in range(HEADS_PER_TILE)]
        ot = jnp.concatenate([ot[:HEAD_DIM] * inv[0], ot[HEAD_DIM:LANES] * inv[1]], axis=0)
        o = ot.T
        for sl in range(SLABS):
            c0 = (hk * SLABS + sl) * LANES
            y_ref[rows_of(blk), c0:c0 + LANES] = (
                o[sl * BLOCK:(sl + 1) * BLOCK]
                * _silu(gate_ref[rows_of(blk), c0:c0 + LANES])).astype(y_ref.dtype)


def band_attention(q, kdup, vt, gate, bias_band, sink_t, bsz, t):
    m = q.shape[0]
    nblk = t // BLOCK
    per_tile = ATT_TILE // BLOCK
    ntile = t // ATT_TILE
    assert t % ATT_TILE == 0
    before = lambda b, i: b * nblk + jnp.maximum(per_tile * i - 1, 0)
    cur = lambda n: pl.BlockSpec((ATT_TILE, n), lambda b, i: (b * ntile + i, 0))
    prev = lambda n: pl.BlockSpec((BLOCK, n), lambda b, i: (before(b, i), 0))
    cur_t = pl.BlockSpec((2 * KV_WIDTH, ATT_TILE), lambda b, i: (0, b * ntile + i))
    prev_t = pl.BlockSpec((2 * KV_WIDTH, BLOCK), lambda b, i: (0, before(b, i)))
    score_tile = (HEADS_PER_TILE * 2 * BLOCK, SLABS * BLOCK)
    return pl.pallas_call(
        _band_attn_kernel,
        grid=(bsz, ntile),
        in_specs=[
            cur(ATT_WIDTH), prev(2 * KV_WIDTH), cur(2 * KV_WIDTH), prev_t, cur_t,
            cur(ATT_WIDTH), _resident(bias_band.shape), _resident(sink_t.shape),
        ],
        out_specs=cur(ATT_WIDTH),
        out_shape=jax.ShapeDtypeStruct((m, ATT_WIDTH), BF16),
        scratch_shapes=[pltpu.VMEM((2,) + score_tile, F32), pltpu.VMEM((2,) + score_tile, BF16)],
        compiler_params=_params("parallel", "parallel"),
        name="band_attention",
    )(q, kdup, kdup, vt, vt, gate, bias_band, sink_t)


def _cached_attn_kernel(q_ref, ckt_ref, cvt_ref, kn_ref, vn_ref, sinks_ref, bpast_ref, bnew_ref, o_ref):
    shape = (N_Q_HEADS, KV_WIDTH)
    lane_kv = lax.broadcasted_iota(jnp.int32, shape, 1) // HEAD_DIM
    row_kv = lax.broadcasted_iota(jnp.int32, shape, 0) // GROUP
    own = lane_kv == row_kv
    sink = sinks_ref[...]
    nt = (((1,), (1,)), ((), ()))
    for b in range(q_ref.shape[0]):
        q = q_ref[b]
        qt = jnp.concatenate([q] * N_KV_HEADS, axis=1)
        qm = jnp.where(own, qt, 0.0).astype(BF16)
        knew = kn_ref[b].astype(BF16).astype(F32)
        vnew = vn_ref[b].astype(BF16).astype(F32)
        s = jnp.dot(qm, ckt_ref[b].astype(BF16), preferred_element_type=F32) + bpast_ref[...]
        s_new = jnp.sum(qm.astype(F32) * knew, axis=-1, keepdims=True) + bnew_ref[:, :1]
        m = jnp.maximum(jnp.maximum(jnp.max(s, axis=-1, keepdims=True), s_new), sink)
        p = jnp.exp(s - m)
        p_new = jnp.exp(s_new - m)
        denom = jnp.sum(p, axis=-1, keepdims=True) + p_new + jnp.exp(sink - m)
        o_all = (lax.dot_general(p.astype(BF16), cvt_ref[b].astype(BF16), nt, preferred_element_type=F32)
                 + p_new.astype(BF16).astype(F32) * vnew)
        o_all = jnp.where(own, o_all, 0.0)
        o = o_all[:, :HEAD_DIM]
        for hk in range(1, N_KV_HEADS):
            o = o + o_all[:, hk * HEAD_DIM:(hk + 1) * HEAD_DIM]
        o_ref[b] = o / denom


def cached_attention(q, cache_kt, cache_vt, k_new, v_new, sinks, bias_past, bias_new):
    bsz, _, rows = cache_kt.shape
    nseq = math.gcd(bsz, SEQS_PER_STEP)
    per_seq = lambda r, n: pl.BlockSpec((nseq, r, n), lambda b: (b, 0, 0))
    return pl.pallas_call(
        _cached_attn_kernel,
        grid=(bsz // nseq,),
        in_specs=[
            per_seq(N_Q_HEADS, HEAD_DIM), per_seq(KV_WIDTH, rows), per_seq(KV_WIDTH, rows),
            per_seq(1, KV_WIDTH), per_seq(1, KV_WIDTH),
            _resident((N_Q_HEADS, 1)), _resident((N_Q_HEADS, rows)), _resident((N_Q_HEADS, LANES)),
        ],
        out_specs=per_seq(N_Q_HEADS, HEAD_DIM),
        out_shape=jax.ShapeDtypeStruct((bsz, N_Q_HEADS, HEAD_DIM), F32),
        compiler_params=_params("parallel"),
        name="cached_attention",
    )(q, cache_kt, cache_vt, k_new, v_new, sinks, bias_past, bias_new)


def kernel(x_prompt, x_sample, state_conv, state_h, cache_k, cache_v, a_norm_pre, a_norm_post,
           a_w_in, a_conv_w, a_conv_b, a_w_r, a_b_r, a_w_i, a_b_i, a_lambda, a_w_out, kv_norm, w_kv,
           b_norm_pre, b_norm_post, b_w_qg, b_sinks, b_w_out, rel_bias_table):
    bsz, t, d = x_prompt.shape
    dbsz, dt, _ = x_sample.shape
    assert a_w_in.shape[0] == 1 and b_w_qg.shape[0] == 1 and dt == 1
    assert t % BLOCK == 0 and t >= WINDOW
    past_rows = cache_k.shape[1]
    assert past_rows == min(WINDOW, PAST_LEN)

    w_r = (0.5 * a_w_r[0]).astype(BF16)
    w_i = (0.5 * a_w_i[0]).astype(BF16)
    sinks = b_sinks[0]
    bias_band, sink_t, bias_past, bias_new = bias_tables(rel_bias_table, sinks, past_rows)
    lru = (a_conv_w[0], a_conv_b[0], w_r, a_b_r[0], w_i, a_b_i[0], a_lambda[0])

    tm = 2 * SUB_ROWS
    xp = x_prompt.reshape(bsz * t, d)
    xs = x_sample.reshape(dbsz, d)

    conv0 = jnp.zeros((bsz, CONV_W - 1, LRU_WIDTH), F32)
    h0 = jnp.zeros((bsz, LRU_WIDTH), F32)
    ug_s, y_even, y_odd, p_conv, p_h = rglru_front(xs, xp, a_norm_pre[0], a_w_in[0], conv0, h0, *lru, seq_len=t)
    hs, s_conv_t = rglru_step(ug_s[:, :LRU_WIDTH], jnp.transpose(state_conv[0], (1, 0, 2)), state_h[0], *lru)
    xs1, x1 = proj_norm_res(hs, ug_s[:, LRU_WIDTH:], xs, (y_even, y_odd), a_w_out[0], a_norm_post[0], xp, tm)

    kv_s, qg_s, q, gate_b, kdup, vt, k_tail, v_tail = norm_proj_kvq(
        xs1, x1, kv_norm, b_norm_pre[0], w_kv, b_w_qg[0], SUB_ROWS, seq_len=t)
    ks, vs = kv_s[:, :KV_WIDTH], kv_s[:, KV_WIDTH:]
    cache_kt = jnp.transpose(cache_k, (0, 2, 3, 1)).reshape(dbsz, KV_WIDTH, past_rows)
    cache_vt = jnp.transpose(cache_v, (0, 2, 3, 1)).reshape(dbsz, KV_WIDTH, past_rows)
    os_ = cached_attention(qg_s[:, :ATT_WIDTH].reshape(dbsz, N_Q_HEADS, HEAD_DIM), cache_kt, cache_vt,
                           ks.reshape(dbsz, 1, KV_WIDTH), vs.reshape(dbsz, 1, KV_WIDTH),
                           sinks.reshape(N_Q_HEADS, 1), bias_past.reshape(N_Q_HEADS, past_rows),
                           bias_new.reshape(N_Q_HEADS, LANES))
    yb = band_attention(q, kdup, vt, gate_b, bias_band, sink_t, bsz, t)
    y_sample, y_prompt = proj_norm_res(os_.reshape(dbsz, ATT_WIDTH), qg_s[:, ATT_WIDTH:], xs1, (yb,),
                                       b_w_out[0], b_norm_post[0], x1, tm)
    y_prompt = y_prompt.reshape(bsz, t, d)
    p_k = jnp.transpose(k_tail.reshape(bsz, N_KV_HEADS, HEAD_DIM, WINDOW), (0, 3, 1, 2))
    p_v = jnp.transpose(v_tail.reshape(bsz, N_KV_HEADS, HEAD_DIM, WINDOW), (0, 3, 1, 2))

    return (y_prompt, y_sample.reshape(dbsz, 1, d),
            p_conv[None], p_h.reshape(1, bsz, LRU_WIDTH), p_k, p_v,
            jnp.transpose(s_conv_t, (1, 0, 2))[None], hs[None],
            ks.reshape(dbsz, 1, N_KV_HEADS, HEAD_DIM), vs.reshape(dbsz, 1, N_KV_HEADS, HEAD_DIM))
```

```python
import functools
import math

import jax
import jax.numpy as jnp
from jax import lax
from jax.experimental import pallas as pl
from jax.experimental.pallas import tpu as pltpu

F32 = jnp.float32
BF16 = jnp.bfloat16

D_MODEL = 2048
LRU_WIDTH = 2048
LRU_BLOCKS = 8
LRU_BLOCK_W = LRU_WIDTH // LRU_BLOCKS
CONV_W = 4
LRU_C = 8.0
HEAD_DIM = 64
N_Q_HEADS = 32
N_KV_HEADS = 8
GROUP = N_Q_HEADS // N_KV_HEADS
ATT_WIDTH = N_Q_HEADS * HEAD_DIM
KV_WIDTH = N_KV_HEADS * HEAD_DIM
WINDOW = 128
BLOCK = WINDOW
N_BUCKETS = 32
MAX_DISTANCE = 128
RMS_EPS = 1e-6
NEG_INF = -1e30
LOG2_E = 1.4426950408889634
PAST_LEN = 16384

V7X_VMEM_BYTES = 64 * 1024 * 1024
VMEM_LIMIT = V7X_VMEM_BYTES - 8 * 1024 * 1024
SUBLANES = 8
LANES = 128
HEADS_PER_TILE = LANES // HEAD_DIM
SLABS = GROUP // HEADS_PER_TILE
ATT_ROWS = 32
ATT_TILE = 2 * BLOCK
SUB_ROWS = 256
W_CHUNK = 512
SEQS_PER_STEP = 8
MXU_COLS = 256


def _params(*semantics):
    return pltpu.CompilerParams(dimension_semantics=semantics, vmem_limit_bytes=VMEM_LIMIT)


def _resident(shape):
    zeros = (0,) * len(shape)
    return pl.BlockSpec(shape, lambda *_: zeros, pipeline_mode=pl.Buffered(1))


def _rms_scale(x):
    return lax.rsqrt(jnp.mean(x * x, axis=-1, keepdims=True) + RMS_EPS)


def _silu(x):
    h = 0.5 * x
    return h * jnp.tanh(h) + h


def _segment_major(rows, inverse=False):
    seg = rows // SUBLANES
    r = lax.broadcasted_iota(jnp.int32, (rows, rows), 0)
    c = lax.broadcasted_iota(jnp.int32, (rows, rows), 1)
    if inverse:
        src = (r % seg) * SUBLANES + r // seg
    else:
        src = (r % SUBLANES) * seg + r // SUBLANES
    return jnp.where(c == src, 1.0, 0.0).astype(BF16)


def _phase_specs(nchunk, tm, k):
    chunk_w = pl.BlockSpec((k, W_CHUNK), lambda i: (0, jnp.minimum(i, nchunk - 1)))
    chunk_o = lambda rows: pl.BlockSpec((rows, W_CHUNK), lambda i: (0, jnp.minimum(i, nchunk - 1)))
    tile = lambda n: pl.BlockSpec((tm, n), lambda i: (jnp.maximum(i - nchunk, 0), 0))
    return chunk_w, chunk_o, tile


def _dup_heads(x):
    low = lax.broadcasted_iota(jnp.int32, (x.shape[0], LANES), 1) < HEAD_DIM
    out = []
    for c in range(x.shape[1] // LANES):
        col = x[:, c * LANES:(c + 1) * LANES]
        swapped = pltpu.roll(col, HEAD_DIM, axis=1)
        out += [jnp.where(low, col, swapped), jnp.where(low, swapped, col)]
    return jnp.concatenate(out, axis=1)


def _norm_proj_kvq_kernel(nkv, nqg, xs_ref, x_ref, gkv_ref, gq_ref, wkv_ref, wqg_ref,
                          kvs_ref, qgs_ref, q_ref, gate_ref, kdup_ref, vt_ref, ktail_ref, vtail_ref,
                          wkv_scr, wqg_scr):
    i = pl.program_id(0)
    nchunk = nkv + nqg
    q_chunks = ATT_WIDTH // W_CHUNK
    q_scale = 1.0 / math.sqrt(HEAD_DIM)
    q_scale_log2 = q_scale * LOG2_E

    def sample_rows(g_ref):
        xs = xs_ref[...]
        return (xs * _rms_scale(xs) * g_ref[...]).astype(BF16)

    @pl.when(i < nkv)
    def _():
        wb = wkv_ref[...].astype(BF16)
        wkv_scr[i] = wb
        kvs_ref[...] = jnp.dot(sample_rows(gkv_ref), wb, preferred_element_type=F32)

    @pl.when((i >= nkv) & (i < nchunk))
    def _():
        c = i - nkv
        wb = wqg_ref[...].astype(BF16)
        wqg_scr[c] = wb
        r = jnp.dot(sample_rows(gq_ref), wb, preferred_element_type=F32)
        qgs_ref[...] = r * jnp.where(c < q_chunks, q_scale, 1.0)

    @pl.when(i >= nchunk)
    def _():
        tm = x_ref.shape[0]
        for rs in _row_blocks(tm):
            x = x_ref[rs, :]
            xh = x * _rms_scale(x)
            xkv = (xh * gkv_ref[...]).astype(BF16)
            xq = (xh * gq_ref[...]).astype(BF16)
            k = jnp.dot(xkv, wkv_scr[0], preferred_element_type=F32)
            v = jnp.dot(xkv, wkv_scr[1], preferred_element_type=F32)
            kdup_ref[rs, :] = _dup_heads(k).astype(BF16)
            vt_ref[:, rs] = _dup_heads(v).T.astype(BF16)
            for c in range(nqg):
                r = jnp.dot(xq, wqg_scr[c], preferred_element_type=F32)
                if c < q_chunks:
                    q_ref[rs, c * W_CHUNK:(c + 1) * W_CHUNK] = (r * q_scale_log2).astype(q_ref.dtype)
                else:
                    cc = c - q_chunks
                    gate_ref[rs, cc * W_CHUNK:(cc + 1) * W_CHUNK] = r
        ktail_ref[0] = k[k.shape[0] - WINDOW:].T
        vtail_ref[0] = v[v.shape[0] - WINDOW:].T


def norm_proj_kvq(xs, x, g_kv, g_q, w_kv, w_qg, tm, seq_len):
    m, d = x.shape
    ns = xs.shape[0]
    assert w_kv.shape[1] == 2 * KV_WIDTH == 2 * W_CHUNK and seq_len % tm == 0 and tm >= WINDOW
    nkv, nqg = w_kv.shape[1] // W_CHUNK, w_qg.shape[1] // W_CHUNK
    nchunk = nkv + nqg
    tiles = seq_len // tm
    tile = lambda n: pl.BlockSpec((tm, n), lambda i: (jnp.maximum(i - nchunk, 0), 0))
    tail = pl.BlockSpec((1, KV_WIDTH, WINDOW), lambda i: (jnp.maximum(i - nchunk, 0) // tiles, 0, 0))
    kv_chunk = lambda i: (0, jnp.minimum(i, nkv - 1))
    qg_chunk = lambda i: (0, jnp.clip(i - nkv, 0, nqg - 1))
    return pl.pallas_call(
        functools.partial(_norm_proj_kvq_kernel, nkv, nqg),
        grid=(nchunk + m // tm,),
        in_specs=[
            _resident(xs.shape), tile(d), _resident((1, d)), _resident((1, d)),
            pl.BlockSpec((d, W_CHUNK), kv_chunk), pl.BlockSpec((d, W_CHUNK), qg_chunk),
        ],
        out_specs=[
            pl.BlockSpec((ns, W_CHUNK), kv_chunk), pl.BlockSpec((ns, W_CHUNK), qg_chunk),
            tile(ATT_WIDTH), tile(ATT_WIDTH), tile(2 * KV_WIDTH),
            pl.BlockSpec((2 * KV_WIDTH, tm), lambda i: (0, jnp.maximum(i - nchunk, 0))), tail, tail,
        ],
        out_shape=[
            jax.ShapeDtypeStruct((ns, w_kv.shape[1]), F32),
            jax.ShapeDtypeStruct((ns, w_qg.shape[1]), F32),
            jax.ShapeDtypeStruct((m, ATT_WIDTH), BF16),
            jax.ShapeDtypeStruct((m, ATT_WIDTH), F32),
            jax.ShapeDtypeStruct((m, 2 * KV_WIDTH), BF16),
            jax.ShapeDtypeStruct((2 * KV_WIDTH, m), BF16),
            jax.ShapeDtypeStruct((m // seq_len, KV_WIDTH, WINDOW), F32),
            jax.ShapeDtypeStruct((m // seq_len, KV_WIDTH, WINDOW), F32),
        ],
        scratch_shapes=[pltpu.VMEM((nkv, d, W_CHUNK), BF16), pltpu.VMEM((nqg, d, W_CHUNK), BF16)],
        compiler_params=_params("arbitrary"),
        name="norm_proj_kvq",
    )(xs, x, g_kv.reshape(1, d), g_q.reshape(1, d), w_kv, w_qg)


def _row_blocks(rows):
    sub = min(rows, SUB_ROWS)
    return [slice(r, r + sub) for r in range(0, rows, sub)]


def _proj_norm_res_kernel(nchunk, nparts, as_ref, gs_ref, xs_ref, *refs):
    y_refs = refs[:nparts]
    w_ref, g_ref, x_ref, os_ref, o_ref, w_scr, raw_scr = refs[nparts:]
    i = pl.program_id(0)

    @pl.when(i < nchunk)
    def _():
        wb = w_ref[...].astype(BF16)
        w_scr[i] = wb
        ys = (as_ref[...] * _silu(gs_ref[...])).astype(BF16)
        raw_scr[i] = jnp.dot(ys, wb, preferred_element_type=F32)

    @pl.when(i == nchunk - 1)
    def _():
        o = jnp.concatenate([raw_scr[c] for c in range(nchunk)], axis=1)
        os_ref[...] = xs_ref[...] + o * _rms_scale(o) * g_ref[...]

    @pl.when(i >= nchunk)
    def _():
        tm = x_ref.shape[0]
        part_rows = tm // nparts

        for rs in _row_blocks(tm):
            part, off = divmod(rs.start, part_rows)
            y = y_refs[part][off:off + rs.stop - rs.start, :]
            o = jnp.concatenate([jnp.dot(y, w_scr[c], preferred_element_type=F32)
                                 for c in range(nchunk)], axis=1)
            o_ref[rs, :] = x_ref[rs, :] + o * _rms_scale(o) * g_ref[...]


def proj_norm_res(a_s, gate_s, x_s, y_parts, w, g, x, tm):
    k, d = w.shape
    m = x.shape[0]
    nchunk = d // W_CHUNK
    nparts = len(y_parts)
    assert (tm // nparts) % min(tm, SUB_ROWS) == 0
    chunk_w, _, tile = _phase_specs(nchunk, tm, k)
    part = pl.BlockSpec((tm // nparts, k), lambda i: (jnp.maximum(i - nchunk, 0), 0))
    return pl.pallas_call(
        functools.partial(_proj_norm_res_kernel, nchunk, nparts),
        grid=(nchunk + m // tm,),
        in_specs=[_resident(a_s.shape), _resident(gate_s.shape), _resident(x_s.shape)]
        + [part] * nparts + [chunk_w, _resident((1, d)), tile(d)],
        out_specs=[pl.BlockSpec(x_s.shape, lambda i: (0, 0)), tile(d)],
        out_shape=[jax.ShapeDtypeStruct(x_s.shape, F32), jax.ShapeDtypeStruct((m, d), F32)],
        scratch_shapes=[pltpu.VMEM((nchunk, k, W_CHUNK), BF16),
                        pltpu.VMEM((nchunk, x_s.shape[0], W_CHUNK), F32)],
        compiler_params=_params("arbitrary"),
        name="proj_norm_res",
    )(a_s, gate_s, x_s, *y_parts, w, g.reshape(1, d), x)


def _lru_gate_dots(conv, wr_half, wi_half):
    cb = conv.astype(BF16)
    return (jnp.dot(cb, wr_half, preferred_element_type=F32),
            jnp.dot(cb, wi_half, preferred_element_type=F32))


def _lru_gates(conv, wr_half, br, wi_half, bi, lam):
    return _lru_gate_math(conv, _lru_gate_dots(conv, wr_half, wi_half), br, bi, lam)


def _lru_gate_math(conv, half_pre, br, bi, lam):
    th_r = jnp.tanh(half_pre[0] + 0.5 * br)
    th_i = jnp.tanh(half_pre[1] + 0.5 * bi)
    nl = -lam
    softplus = jnp.maximum(nl, 0.0) + jnp.log1p(jnp.exp(-jnp.abs(nl)))
    half = (0.5 * LRU_C) * softplus
    x = th_r * half + half
    a = jnp.exp2(x * -LOG2_E)
    z = jnp.tanh(x) * (a * a + 1.0)
    mult = z * lax.rsqrt(jnp.maximum(z, 1e-30))
    hc = 0.5 * conv
    return a, mult * (hc * th_i + hc)


def _interleave(*stages):
    live = [[stage, share] for stage, share in stages]
    while live:
        for entry in list(live):
            try:
                for _ in range(entry[1]):
                    next(entry[0])
            except StopIteration:
                live.remove(entry)


def _in_proj_tile(x_ref, rs, g_ref, w_scr, ug_ref):
    x = x_ref[rs, :]
    xn = (x * _rms_scale(x) * g_ref[...]).astype(BF16)
    xn = jnp.dot(_segment_major(xn.shape[0]), xn, preferred_element_type=F32).astype(BF16)
    for c in range(w_scr.shape[0]):
        for n0 in range(0, W_CHUNK, MXU_COLS):
            ug_ref[:, c * W_CHUNK + n0:c * W_CHUNK + n0 + MXU_COLS] = jnp.dot(
                xn, w_scr[c, :, n0:n0 + MXU_COLS], preferred_element_type=F32)
            yield


def _rglru_tile(ug_ref, y_ref, cw_ref, cb_ref, wr_ref, br_ref, wi_ref, bi_ref, lam_ref, h_scr, tail_scr):
    tc = ug_ref.shape[0]
    seg = tc // SUBLANES
    ntaps = CONV_W - 1
    bw = LRU_BLOCK_W
    sub = lax.broadcasted_iota(jnp.int32, (SUBLANES, bw), 0)
    first = sub == 0
    time_order = _segment_major(tc, inverse=True)

    def shift_in(x, row0):
        return jnp.where(first, row0, pltpu.roll(x, 1, axis=0))

    def group(x, j):
        return x[j * SUBLANES:(j + 1) * SUBLANES]

    def store_time_order(cols, y):
        y_ref[:, cols] = jnp.dot(time_order, y, preferred_element_type=F32).astype(y_ref.dtype)

    pending = None
    for n in range(LRU_BLOCKS):
        cs = slice(n * bw, (n + 1) * bw)
        u = ug_ref[:, cs]
        tail = tail_scr[:, cs]
        before = [shift_in(group(u, seg - m), tail[ntaps - m:ntaps - m + 1])
                  for m in range(ntaps, 0, -1)]
        ext = jnp.concatenate(before + [u], axis=0)
        tail_scr[:, cs] = jnp.concatenate(
            [group(u, seg - m)[SUBLANES - 1:] for m in range(ntaps, 0, -1)], axis=0)
        cw = cw_ref[:, cs]
        conv = cb_ref[:, cs]
        for tap in range(CONV_W):
            conv = conv + ext[tap * SUBLANES:tap * SUBLANES + tc] * cw[tap:tap + 1]
        yield

        half_pre = _lru_gate_dots(conv, wr_ref[n], wi_ref[n])
        yield

        if pending is not None:
            store_time_order(*pending)
        yield

        a, b = _lru_gate_math(conv, half_pre, br_ref[:, cs], bi_ref[:, cs], lam_ref[:, cs])

        h = b[:SUBLANES]
        acc = a[:SUBLANES]
        h_loc, a_cum = [h], [acc]
        for j in range(1, seg):
            sl = slice(j * SUBLANES, (j + 1) * SUBLANES)
            h = a[sl] * h + b[sl]
            acc = a[sl] * acc
            h_loc.append(h)
            a_cum.append(acc)

        step = 1
        while step < SUBLANES:
            keep = sub >= step
            h = jnp.where(keep, acc * pltpu.roll(h, step, axis=0) + h, h)
            acc = jnp.where(keep, acc * pltpu.roll(acc, step, axis=0), acc)
            step *= 2
        h_prev = h_scr[:, cs]
        after = h + acc * h_prev
        h_in = shift_in(after, h_prev)
        h_scr[:, cs] = after[SUBLANES - 1:]

        hs = jnp.concatenate([h_loc[j] + a_cum[j] * h_in for j in range(seg)], axis=0)
        y = (hs * _silu(ug_ref[:, LRU_WIDTH + n * bw:LRU_WIDTH + (n + 1) * bw])).astype(BF16)
        pending = (cs, y)
        yield

    store_time_order(*pending)
    yield


def _rglru_front_kernel(nchunk, npairs, chunks, xs_ref, x_ref, g_ref, w_ref, cprev_ref, h0_ref,
                        cw_ref, cb_ref, wr_ref, br_ref, wi_ref, bi_ref, lam_ref,
                        os_ref, y_even_ref, y_odd_ref, cnew_ref, hlast_ref,
                        w_scr, ug0_scr, ug1_scr, h_scr, tail_scr):
    i = pl.program_id(0)
    p = i - nchunk
    tc = SUB_ROWS
    lru = (cw_ref, cb_ref, wr_ref, br_ref, wi_ref, bi_ref, lam_ref, h_scr, tail_scr)

    @pl.when(i < nchunk)
    def _():
        wb = w_ref[...].astype(BF16)
        w_scr[i] = wb
        xs = xs_ref[...]
        xsn = (xs * _rms_scale(xs) * g_ref[...]).astype(BF16)
        os_ref[...] = jnp.dot(xsn, wb, preferred_element_type=F32)

    @pl.when(i == nchunk - 1)
    def _():
        ug1_scr[...] = jnp.zeros_like(ug1_scr)
        h_scr[...] = jnp.zeros_like(h_scr)
        tail_scr[...] = jnp.zeros_like(tail_scr)

    @pl.when(p >= 0)
    def _():
        _interleave((_in_proj_tile(x_ref, slice(0, tc), g_ref, w_scr, ug0_scr), 1),
                    (_rglru_tile(ug1_scr, y_odd_ref, *lru), 2))
        hlast_ref[0] = h_scr[...]
        cnew_ref[0] = tail_scr[...]

    @pl.when((p >= 0) & (p < npairs))
    def _():
        @pl.when((2 * p) % chunks == 0)
        def _():
            h_scr[...] = h0_ref[0]
            tail_scr[...] = cprev_ref[0]

        _interleave((_in_proj_tile(x_ref, slice(tc, 2 * tc), g_ref, w_scr, ug1_scr), 1),
                    (_rglru_tile(ug0_scr, y_even_ref, *lru), 2))


def rglru_front(xs, x, g, w_in, conv_prev, h0, conv_w, conv_b, w_r, b_r, w_i, b_i, lam, seq_len):
    m, d = x.shape
    w = w_in.shape[1] // 2
    tc = SUB_ROWS
    nchunk = w_in.shape[1] // W_CHUNK
    bsz = m // seq_len
    chunks = seq_len // tc
    npairs = m // (2 * tc)
    assert seq_len % (2 * tc) == 0 and tc % (SUBLANES * SUBLANES) == 0 and tc // SUBLANES > CONV_W
    pair = lambda i: jnp.clip(i - nchunk, 0, npairs - 1)
    last = npairs * 2 - 1
    seq_in = lambda i: (jnp.clip(2 * (i - nchunk), 0, last) // chunks, 0, 0)
    seq_out = lambda i: (jnp.clip(2 * (i - nchunk) - 1, 0, last) // chunks, 0, 0)
    state_in = lambda rows: pl.BlockSpec((1, rows, w), seq_in)
    state_out = lambda rows: pl.BlockSpec((1, rows, w), seq_out)
    chunk = lambda i: (0, jnp.minimum(i, nchunk - 1))
    return pl.pallas_call(
        functools.partial(_rglru_front_kernel, nchunk, npairs, chunks),
        grid=(nchunk + npairs + 1,),
        in_specs=[_resident(xs.shape), pl.BlockSpec((2 * tc, d), lambda i: (pair(i), 0)), _resident((1, d)),
                  pl.BlockSpec((d, W_CHUNK), chunk), state_in(CONV_W - 1), state_in(1),
                  _resident((CONV_W, w)), _resident((1, w)), _resident(w_r.shape), _resident((1, w)),
                  _resident(w_i.shape), _resident((1, w)), _resident((1, w))],
        out_specs=[
            pl.BlockSpec((xs.shape[0], W_CHUNK), chunk),
            pl.BlockSpec((tc, w), lambda i: (pair(i), 0)),
            pl.BlockSpec((tc, w), lambda i: (jnp.clip(i - nchunk - 1, 0, npairs - 1), 0)),
            state_out(CONV_W - 1), state_out(1),
        ],
        out_shape=[
            jax.ShapeDtypeStruct((xs.shape[0], 2 * w), F32),
            jax.ShapeDtypeStruct((m // 2, w), BF16),
            jax.ShapeDtypeStruct((m // 2, w), BF16),
            jax.ShapeDtypeStruct((bsz, CONV_W - 1, w), F32),
            jax.ShapeDtypeStruct((bsz, 1, w), F32),
        ],
        scratch_shapes=[pltpu.VMEM((nchunk, d, W_CHUNK), BF16),
                        pltpu.VMEM((tc, 2 * w), F32), pltpu.VMEM((tc, 2 * w), F32),
                        pltpu.VMEM((1, w), F32), pltpu.VMEM((CONV_W - 1, w), F32)],
        compiler_params=_params("arbitrary"),
        name="rglru_front",
    )(xs, x, g.reshape(1, d), w_in, conv_prev, h0.reshape(bsz, 1, w), conv_w, conv_b.reshape(1, w),
      w_r, b_r.reshape(1, w), w_i, b_i.reshape(1, w), lam.reshape(1, w))


def _rglru_step_kernel(u_ref, cprev_ref, h0_ref, cw_ref, cb_ref, wr_ref, br_ref, wi_ref, bi_ref,
                       lam_ref, h_ref, cnew_ref):
    u = u_ref[...]
    cw = cw_ref[...]
    conv = cb_ref[...]
    for tap in range(CONV_W - 1):
        conv = conv + cprev_ref[tap] * cw[tap:tap + 1]
        if tap > 0:
            cnew_ref[tap - 1] = cprev_ref[tap]
    conv = conv + u * cw[CONV_W - 1:]
    cnew_ref[CONV_W - 2] = u
    a, b = _lru_gates(conv, wr_ref[0], br_ref[...], wi_ref[0], bi_ref[...], lam_ref[...])
    h_ref[...] = a * h0_ref[...] + b


def rglru_step(u, conv_prev_t, h0, conv_w, conv_b, w_r, b_r, w_i, b_i, lam):
    bsz, w = u.shape
    bw = LRU_BLOCK_W
    rows = pl.BlockSpec((bsz, bw), lambda n: (0, n))
    taps = pl.BlockSpec((CONV_W - 1, bsz, bw), lambda n: (0, 0, n))
    chan = lambda r: pl.BlockSpec((r, bw), lambda n: (0, n))
    blockw = pl.BlockSpec((1, bw, bw), lambda n: (n, 0, 0))
    return pl.pallas_call(
        _rglru_step_kernel,
        grid=(LRU_BLOCKS,),
        in_specs=[rows, taps, rows, chan(CONV_W), chan(1), blockw, chan(1), blockw, chan(1), chan(1)],
        out_specs=[rows, taps],
        out_shape=[
            jax.ShapeDtypeStruct((bsz, w), F32),
            jax.ShapeDtypeStruct((CONV_W - 1, bsz, w), F32),
        ],
        compiler_params=_params("parallel"),
        name="rglru_step",
    )(u, conv_prev_t, h0, conv_w, conv_b.reshape(1, w), w_r, b_r.reshape(1, w),
      w_i, b_i.reshape(1, w), lam.reshape(1, w))


def _buckets(dist):
    n = jnp.maximum(dist, 0)
    max_exact = N_BUCKETS // 2
    nf = jnp.maximum(n, 1).astype(F32)
    large = max_exact + jnp.floor(jnp.log(nf / max_exact) / math.log(MAX_DISTANCE / max_exact)
                                  * (N_BUCKETS - max_exact)).astype(jnp.int32)
    large = jnp.minimum(large, N_BUCKETS - 1)
    return jnp.where(n < max_exact, n, large)


def _lookup(bucket, valid, table_ref, head):
    bias = jnp.zeros(bucket.shape, F32)
    for b in range(N_BUCKETS):
        bias = jnp.where(bucket == b, table_ref[b, head], bias)
    return jnp.where(valid, bias, NEG_INF)


def _bias_kernel(table_ref, sinks_ref, band_ref, sinkt_ref, past_ref, new_ref):
    hk = pl.program_id(0)
    span = 3 * BLOCK
    dist = (lax.broadcasted_iota(jnp.int32, (1, span), 1) + BLOCK) % span
    bucket = _buckets(dist)
    in_window = (dist >= 0) & (dist < WINDOW)
    no_prev = lax.broadcasted_iota(jnp.int32, (2 * BLOCK, BLOCK), 0) < BLOCK

    def band(head):
        row = _lookup(bucket, in_window, table_ref, head) * LOG2_E
        full = pltpu.roll(jnp.broadcast_to(row, (2 * BLOCK, span)), 0, axis=1, stride=1, stride_axis=0)
        return full[:, :BLOCK]

    rows = past_ref.shape[2]
    d_past = rows - lax.broadcasted_iota(jnp.int32, (1, rows), 1)
    b_past = _buckets(d_past)
    ok_past = (d_past >= 0) & (d_past < WINDOW)
    d_new = jnp.zeros((1, LANES), jnp.int32)
    b_new = _buckets(d_new)
    for par in range(HEADS_PER_TILE):
        for slab in range(SLABS):
            g = slab * HEADS_PER_TILE + par
            head = hk * GROUP + g
            rs = slice(par * 2 * BLOCK, (par + 1) * 2 * BLOCK)
            cs = slice(slab * BLOCK, (slab + 1) * BLOCK)
            bias = band(head)
            band_ref[0, 0, rs, cs] = bias
            band_ref[1, 0, rs, cs] = jnp.where(no_prev, NEG_INF, bias)
            sinkt_ref[0, par, :, cs] = jnp.full((1, BLOCK), sinks_ref[head] * LOG2_E, F32)
            past_ref[0, g:g + 1, :] = _lookup(b_past, ok_past, table_ref, head)
            new_ref[0, g:g + 1, :] = _lookup(b_new, d_new == 0, table_ref, head)


def bias_tables(table, sinks, past_rows):
    smem = pl.BlockSpec(memory_space=pltpu.SMEM)
    return pl.pallas_call(
        _bias_kernel,
        grid=(N_KV_HEADS,),
        in_specs=[smem, smem],
        out_specs=[
            pl.BlockSpec((2, 1, HEADS_PER_TILE * 2 * BLOCK, SLABS * BLOCK), lambda h: (0, h, 0, 0)),
            pl.BlockSpec((1, HEADS_PER_TILE, 1, SLABS * BLOCK), lambda h: (h, 0, 0, 0)),
            pl.BlockSpec((1, GROUP, past_rows), lambda h: (h, 0, 0)),
            pl.BlockSpec((1, GROUP, LANES), lambda h: (h, 0, 0)),
        ],
        out_shape=[
            jax.ShapeDtypeStruct((2, N_KV_HEADS, HEADS_PER_TILE * 2 * BLOCK, SLABS * BLOCK), F32),
            jax.ShapeDtypeStruct((N_KV_HEADS, HEADS_PER_TILE, 1, SLABS * BLOCK), F32),
            jax.ShapeDtypeStruct((N_KV_HEADS, GROUP, past_rows), F32),
            jax.ShapeDtypeStruct((N_KV_HEADS, GROUP, LANES), F32),
        ],
        compiler_params=_params("parallel"),
        name="bias_tables",
    )(table, sinks)


def _band_attn_kernel(q_ref, kp_ref, kc_ref, vp_ref, vc_ref, gate_ref, bias_ref, sink_ref, y_ref,
                      s_scr, p_scr):
    first_tile = (pl.program_id(1) == 0).astype(jnp.int32)
    nt = (((1,), (1,)), ((), ()))
    low = (lax.broadcasted_iota(jnp.int32, (1, LANES), 1) < HEAD_DIM)
    keep_low = low.astype(BF16)
    keep_high = 1 - keep_low
    nkeys = 2 * BLOCK
    zeros_v = jnp.zeros((HEAD_DIM, nkeys), BF16)
    ones_rows = jnp.where(
        lax.broadcasted_iota(jnp.int32, (2 * SUBLANES, HEADS_PER_TILE * nkeys), 0)
        == lax.broadcasted_iota(jnp.int32, (2 * SUBLANES, HEADS_PER_TILE * nkeys), 1) // nkeys,
        1.0, 0.0).astype(BF16)
    rows = ATT_ROWS
    items = [(blk, hk) for blk in range(q_ref.shape[0] // BLOCK) for hk in range(N_KV_HEADS)]

    def rows_of(blk):
        return slice(blk * BLOCK, (blk + 1) * BLOCK)

    def scores(idx):
        blk, hk = items[idx]
        cs = slice(hk * LANES, (hk + 1) * LANES)
        k_prev = kp_ref[:, cs] if blk == 0 else kc_ref[rows_of(blk - 1), cs]
        kd = jnp.concatenate([k_prev, kc_ref[rows_of(blk), cs]], axis=0)
        lhs = jnp.concatenate([kd * keep_low, kd * keep_high], axis=0)
        qs = jnp.concatenate([q_ref[rows_of(blk), (hk * SLABS + s) * LANES:(hk * SLABS + s + 1) * LANES]
                              for s in range(SLABS)], axis=0)
        s_scr[idx % 2] = lax.dot_general(lhs, qs, nt, preferred_element_type=F32)

    scores(0)
    for idx, (blk, hk) in enumerate(items):
        slot = idx % 2
        first = first_tile if blk == 0 else 0
        if idx + 1 < len(items):
            scores(idx + 1)
        sink_w = []
        for par in range(HEADS_PER_TILE):
            base = par * nkeys
            sink = sink_ref[hk, par]
            top = None
            for r in range(base, base + nkeys, rows):
                sb = s_scr[slot, r:r + rows, :] + bias_ref[first, hk, r:r + rows, :]
                s_scr[slot, r:r + rows, :] = sb
                top = sb if top is None else jnp.maximum(top, sb)
            m = jnp.maximum(jnp.max(top, axis=0, keepdims=True), sink)
            for r in range(base, base + nkeys, rows):
                p_scr[slot, r:r + rows, :] = jnp.exp2(s_scr[slot, r:r + rows, :] - m).astype(BF16)
            sink_w.append(jnp.exp2(sink - m))
        cs = slice(hk * LANES, (hk + 1) * LANES)
        v_prev = vp_ref[cs, :] if blk == 0 else vc_ref[cs, rows_of(blk - 1)]
        vt = jnp.concatenate([v_prev, vc_ref[cs, rows_of(blk)]], axis=1)
        lhs_v = jnp.concatenate([
            jnp.concatenate([vt[:HEAD_DIM], zeros_v, ones_rows[:, :nkeys]], axis=0),
            jnp.concatenate([zeros_v, vt[HEAD_DIM:], ones_rows[:, nkeys:]], axis=0)], axis=1)
        ot = jnp.dot(lhs_v, p_scr[slot], preferred_element_type=F32)
        inv = [1.0 / (ot[LANES + par:LANES + par + 1] + sink_w[par]) for par in range(HEADS_PER_TILE)]
        ot = jnp.concatenate([ot[:HEAD_DIM] * inv[0], ot[HEAD_DIM:LANES] * inv[1]], axis=0)
        o = ot.T
        for sl in range(SLABS):
            c0 = (hk * SLABS + sl) * LANES
            y_ref[rows_of(blk), c0:c0 + LANES] = (
                o[sl * BLOCK:(sl + 1) * BLOCK]
                * _silu(gate_ref[rows_of(blk), c0:c0 + LANES])).astype(y_ref.dtype)


def band_attention(q, kdup, vt, gate, bias_band, sink_t, bsz, t):
    m = q.shape[0]
    nblk = t // BLOCK
    per_tile = ATT_TILE // BLOCK
    ntile = t // ATT_TILE
    assert t % ATT_TILE == 0
    before = lambda b, i: b * nblk + jnp.maximum(per_tile * i - 1, 0)
    cur = lambda n: pl.BlockSpec((ATT_TILE, n), lambda b, i: (b * ntile + i, 0))
    prev = lambda n: pl.BlockSpec((BLOCK, n), lambda b, i: (before(b, i), 0))
    cur_t = pl.BlockSpec((2 * KV_WIDTH, ATT_TILE), lambda b, i: (0, b * ntile + i))
    prev_t = pl.BlockSpec((2 * KV_WIDTH, BLOCK), lambda b, i: (0, before(b, i)))
    score_tile = (HEADS_PER_TILE * 2 * BLOCK, SLABS * BLOCK)
    return pl.pallas_call(
        _band_attn_kernel,
        grid=(bsz, ntile),
        in_specs=[
            cur(ATT_WIDTH), prev(2 * KV_WIDTH), cur(2 * KV_WIDTH), prev_t, cur_t,
            cur(ATT_WIDTH), _resident(bias_band.shape), _resident(sink_t.shape),
        ],
        out_specs=cur(ATT_WIDTH),
        out_shape=jax.ShapeDtypeStruct((m, ATT_WIDTH), BF16),
        scratch_shapes=[pltpu.VMEM((2,) + score_tile, F32), pltpu.VMEM((2,) + score_tile, BF16)],
        compiler_params=_params("parallel", "parallel"),
        name="band_attention",
    )(q, kdup, kdup, vt, vt, gate, bias_band, sink_t)


def _cached_attn_kernel(q_ref, ckt_ref, cvt_ref, kn_ref, vn_ref, sinks_ref, bpast_ref, bnew_ref, o_ref):
    shape = (N_Q_HEADS, KV_WIDTH)
    lane_kv = lax.broadcasted_iota(jnp.int32, shape, 1) // HEAD_DIM
    row_kv = lax.broadcasted_iota(jnp.int32, shape, 0) // GROUP
    own = lane_kv == row_kv
    sink = sinks_ref[...]
    nt = (((1,), (1,)), ((), ()))
    for b in range(q_ref.shape[0]):
        q = q_ref[b]
        qt = jnp.concatenate([q] * N_KV_HEADS, axis=1)
        qm = jnp.where(own, qt, 0.0).astype(BF16)
        knew = kn_ref[b].astype(BF16).astype(F32)
        vnew = vn_ref[b].astype(BF16).astype(F32)
        s = jnp.dot(qm, ckt_ref[b].astype(BF16), preferred_element_type=F32) + bpast_ref[...]
        s_new = jnp.sum(qm.astype(F32) * knew, axis=-1, keepdims=True) + bnew_ref[:, :1]
        m = jnp.maximum(jnp.maximum(jnp.max(s, axis=-1, keepdims=True), s_new), sink)
        p = jnp.exp(s - m)
        p_new = jnp.exp(s_new - m)
        denom = jnp.sum(p, axis=-1, keepdims=True) + p_new + jnp.exp(sink - m)
        o_all = (lax.dot_general(p.astype(BF16), cvt_ref[b].astype(BF16), nt, preferred_element_type=F32)
                 + p_new.astype(BF16).astype(F32) * vnew)
        o_all = jnp.where(own, o_all, 0.0)
        o = o_all[:, :HEAD_DIM]
        for hk in range(1, N_KV_HEADS):
            o = o + o_all[:, hk * HEAD_DIM:(hk + 1) * HEAD_DIM]
        o_ref[b] = o / denom


def cached_attention(q, cache_kt, cache_vt, k_new, v_new, sinks, bias_past, bias_new):
    bsz, _, rows = cache_kt.shape
    nseq = math.gcd(bsz, SEQS_PER_STEP)
    per_seq = lambda r, n: pl.BlockSpec((nseq, r, n), lambda b: (b, 0, 0))
    return pl.pallas_call(
        _cached_attn_kernel,
        grid=(bsz // nseq,),
        in_specs=[
            per_seq(N_Q_HEADS, HEAD_DIM), per_seq(KV_WIDTH, rows), per_seq(KV_WIDTH, rows),
            per_seq(1, KV_WIDTH), per_seq(1, KV_WIDTH),
            _resident((N_Q_HEADS, 1)), _resident((N_Q_HEADS, rows)), _resident((N_Q_HEADS, LANES)),
        ],
        out_specs=per_seq(N_Q_HEADS, HEAD_DIM),
        out_shape=jax.ShapeDtypeStruct((bsz, N_Q_HEADS, HEAD_DIM), F32),
        compiler_params=_params("parallel"),
        name="cached_attention",
    )(q, cache_kt, cache_vt, k_new, v_new, sinks, bias_past, bias_new)


def kernel(x_prompt, x_sample, state_conv, state_h, cache_k, cache_v, a_norm_pre, a_norm_post,
           a_w_in, a_conv_w, a_conv_b, a_w_r, a_b_r, a_w_i, a_b_i, a_lambda, a_w_out, kv_norm, w_kv,
           b_norm_pre, b_norm_post, b_w_qg, b_sinks, b_w_out, rel_bias_table):
    bsz, t, d = x_prompt.shape
    dbsz, dt, _ = x_sample.shape
    assert a_w_in.shape[0] == 1 and b_w_qg.shape[0] == 1 and dt == 1
    assert t % BLOCK == 0 and t >= WINDOW
    past_rows = cache_k.shape[1]
    assert past_rows == min(WINDOW, PAST_LEN)

    w_r = (0.5 * a_w_r[0]).astype(BF16)
    w_i = (0.5 * a_w_i[0]).astype(BF16)
    sinks = b_sinks[0]
    bias_band, sink_t, bias_past, bias_new = bias_tables(rel_bias_table, sinks, past_rows)
    lru = (a_conv_w[0], a_conv_b[0], w_r, a_b_r[0], w_i, a_b_i[0], a_lambda[0])

    tm = 2 * SUB_ROWS
    xp = x_prompt.reshape(bsz * t, d)
    xs = x_sample.reshape(dbsz, d)

    conv0 = jnp.zeros((bsz, CONV_W - 1, LRU_WIDTH), F32)
    h0 = jnp.zeros((bsz, LRU_WIDTH), F32)
    ug_s, y_even, y_odd, p_conv, p_h = rglru_front(xs, xp, a_norm_pre[0], a_w_in[0], conv0, h0, *lru, seq_len=t)
    hs, s_conv_t = rglru_step(ug_s[:, :LRU_WIDTH], jnp.transpose(state_conv[0], (1, 0, 2)), state_h[0], *lru)
    xs1, x1 = proj_norm_res(hs, ug_s[:, LRU_WIDTH:], xs, (y_even, y_odd), a_w_out[0], a_norm_post[0], xp, tm)

    kv_s, qg_s, q, gate_b, kdup, vt, k_tail, v_tail = norm_proj_kvq(
        xs1, x1, kv_norm, b_norm_pre[0], w_kv, b_w_qg[0], SUB_ROWS, seq_len=t)
    ks, vs = kv_s[:, :KV_WIDTH], kv_s[:, KV_WIDTH:]
    cache_kt = jnp.transpose(cache_k, (0, 2, 3, 1)).reshape(dbsz, KV_WIDTH, past_rows)
    cache_vt = jnp.transpose(cache_v, (0, 2, 3, 1)).reshape(dbsz, KV_WIDTH, past_rows)
    os_ = cached_attention(qg_s[:, :ATT_WIDTH].reshape(dbsz, N_Q_HEADS, HEAD_DIM), cache_kt, cache_vt,
                           ks.reshape(dbsz, 1, KV_WIDTH), vs.reshape(dbsz, 1, KV_WIDTH),
                           sinks.reshape(N_Q_HEADS, 1), bias_past.reshape(N_Q_HEADS, past_rows),
                           bias_new.reshape(N_Q_HEADS, LANES))
    yb = band_attention(q, kdup, vt, gate_b, bias_band, sink_t, bsz, t)
    y_sample, y_prompt = proj_norm_res(os_.reshape(dbsz, ATT_WIDTH), qg_s[:, ATT_WIDTH:], xs1, (yb,),
                                       b_w_out[0], b_norm_post[0], x1, tm)
    y_prompt = y_prompt.reshape(bsz, t, d)
    p_k = jnp.transpose(k_tail.reshape(bsz, N_KV_HEADS, HEAD_DIM, WINDOW), (0, 3, 1, 2))
    p_v = jnp.transpose(v_tail.reshape(bsz, N_KV_HEADS, HEAD_DIM, WINDOW), (0, 3, 1, 2))

    return (y_prompt, y_sample.reshape(dbsz, 1, d),
            p_conv[None], p_h.reshape(1, bsz, LRU_WIDTH), p_k, p_v,
            jnp.transpose(s_conv_t, (1, 0, 2))[None], hs[None],
            ks.reshape(dbsz, 1, N_KV_HEADS, HEAD_DIM), vs.reshape(dbsz, 1, N_KV_HEADS, HEAD_DIM))
```

```python
import functools
import math

import jax
import jax.numpy as jnp
from jax import lax
from jax.experimental import pallas as pl
from jax.experimental.pallas import tpu as pltpu

F32 = jnp.float32
BF16 = jnp.bfloat16

D_MODEL = 2048
LRU_WIDTH = 2048
LRU_BLOCKS = 8
LRU_BLOCK_W = LRU_WIDTH // LRU_BLOCKS
CONV_W = 4
LRU_C = 8.0
HEAD_DIM = 64
N_Q_HEADS = 32
N_KV_HEADS = 8
GROUP = N_Q_HEADS // N_KV_HEADS
ATT_WIDTH = N_Q_HEADS * HEAD_DIM
KV_WIDTH = N_KV_HEADS * HEAD_DIM
WINDOW = 128
BLOCK = WINDOW
N_BUCKETS = 32
MAX_DISTANCE = 128
RMS_EPS = 1e-6
NEG_INF = -1e30
LOG2_E = 1.4426950408889634
PAST_LEN = 16384

V7X_VMEM_BYTES = 64 * 1024 * 1024
VMEM_LIMIT = V7X_VMEM_BYTES - 8 * 1024 * 1024
SUBLANES = 8
LANES = 128
HEADS_PER_TILE = LANES // HEAD_DIM
SLABS = GROUP // HEADS_PER_TILE
ATT_ROWS = 32
ATT_SLOTS = 3
ATT_TILE = 2 * BLOCK
SUB_ROWS = 256
W_CHUNK = 512
SEQS_PER_STEP = 8
MXU_COLS = 256


def _params(*semantics):
    return pltpu.CompilerParams(dimension_semantics=semantics, vmem_limit_bytes=VMEM_LIMIT)


def _resident(shape):
    zeros = (0,) * len(shape)
    return pl.BlockSpec(shape, lambda *_: zeros, pipeline_mode=pl.Buffered(1))


def _rms_scale(x):
    return lax.rsqrt(jnp.mean(x * x, axis=-1, keepdims=True) + RMS_EPS)


def _silu(x):
    h = 0.5 * x
    return h * jnp.tanh(h) + h


def _segment_major(rows, inverse=False):
    seg = rows // SUBLANES
    r = lax.broadcasted_iota(jnp.int32, (rows, rows), 0)
    c = lax.broadcasted_iota(jnp.int32, (rows, rows), 1)
    if inverse:
        src = (r % seg) * SUBLANES + r // seg
    else:
        src = (r % SUBLANES) * seg + r // SUBLANES
    return jnp.where(c == src, 1.0, 0.0).astype(BF16)


def _phase_specs(nchunk, tm, k):
    chunk_w = pl.BlockSpec((k, W_CHUNK), lambda i: (0, jnp.minimum(i, nchunk - 1)))
    chunk_o = lambda rows: pl.BlockSpec((rows, W_CHUNK), lambda i: (0, jnp.minimum(i, nchunk - 1)))
    tile = lambda n: pl.BlockSpec((tm, n), lambda i: (jnp.maximum(i - nchunk, 0), 0))
    return chunk_w, chunk_o, tile


def _dup_heads(x):
    low = lax.broadcasted_iota(jnp.int32, (x.shape[0], LANES), 1) < HEAD_DIM
    out = []
    for c in range(x.shape[1] // LANES):
        col = x[:, c * LANES:(c + 1) * LANES]
        swapped = pltpu.roll(col, HEAD_DIM, axis=1)
        out += [jnp.where(low, col, swapped), jnp.where(low, swapped, col)]
    return jnp.concatenate(out, axis=1)


def _norm_proj_kvq_kernel(nkv, nqg, xs_ref, x_ref, gkv_ref, gq_ref, wkv_ref, wqg_ref,
                          ks_ref, vs_ref, qs_ref, gates_ref,
                          q_ref, gate_ref, kdup_ref, vt_ref, ktail_ref, vtail_ref,
                          wkv_scr, wqg_scr):
    i = pl.program_id(0)
    nchunk = nkv + nqg
    q_chunks = ATT_WIDTH // W_CHUNK
    q_scale = 1.0 / math.sqrt(HEAD_DIM)
    q_scale_log2 = q_scale * LOG2_E

    def sample_rows(g_ref):
        xs = xs_ref[...]
        return (xs * _rms_scale(xs) * g_ref[...]).astype(BF16)

    @pl.when(i < nkv)
    def _():
        wb = wkv_ref[...].astype(BF16)
        wkv_scr[i] = wb
        r = jnp.dot(sample_rows(gkv_ref), wb, preferred_element_type=F32)

        @pl.when(i == 0)
        def _():
            ks_ref[...] = r

        @pl.when(i == 1)
        def _():
            vs_ref[...] = r

    @pl.when((i >= nkv) & (i < nchunk))
    def _():
        c = i - nkv
        wb = wqg_ref[...].astype(BF16)
        wqg_scr[c] = wb
        r = jnp.dot(sample_rows(gq_ref), wb, preferred_element_type=F32)

        @pl.when(c < q_chunks)
        def _():
            qs_ref[...] = r * q_scale

        @pl.when(c >= q_chunks)
        def _():
            gates_ref[...] = r

    @pl.when(i >= nchunk)
    def _():
        tm = x_ref.shape[0]
        for rs in _row_blocks(tm):
            x = x_ref[rs, :]
            xh = x * _rms_scale(x)
            xkv = (xh * gkv_ref[...]).astype(BF16)
            xq = (xh * gq_ref[...]).astype(BF16)
            k = jnp.dot(xkv, wkv_scr[0], preferred_element_type=F32)
            v = jnp.dot(xkv, wkv_scr[1], preferred_element_type=F32)
            kdup_ref[rs, :] = _dup_heads(k).astype(BF16)
            vt_ref[:, rs] = _dup_heads(v).T.astype(BF16)
            for c in range(nqg):
                r = jnp.dot(xq, wqg_scr[c], preferred_element_type=F32)
                if c < q_chunks:
                    q_ref[rs, c * W_CHUNK:(c + 1) * W_CHUNK] = (r * q_scale_log2).astype(q_ref.dtype)
                else:
                    cc = c - q_chunks
                    gate_ref[rs, cc * W_CHUNK:(cc + 1) * W_CHUNK] = r
        ktail_ref[0] = k[k.shape[0] - WINDOW:].T
        vtail_ref[0] = v[v.shape[0] - WINDOW:].T


def norm_proj_kvq(xs, x, g_kv, g_q, w_kv, w_qg, tm, seq_len):
    m, d = x.shape
    ns = xs.shape[0]
    assert w_kv.shape[1] == 2 * KV_WIDTH == 2 * W_CHUNK and seq_len % tm == 0 and tm >= WINDOW
    nkv, nqg = w_kv.shape[1] // W_CHUNK, w_qg.shape[1] // W_CHUNK
    nchunk = nkv + nqg
    tiles = seq_len // tm
    tile = lambda n: pl.BlockSpec((tm, n), lambda i: (jnp.maximum(i - nchunk, 0), 0))
    tail = pl.BlockSpec((1, KV_WIDTH, WINDOW), lambda i: (jnp.maximum(i - nchunk, 0) // tiles, 0, 0))
    kv_chunk = lambda i: (0, jnp.minimum(i, nkv - 1))
    qg_chunk = lambda i: (0, jnp.clip(i - nkv, 0, nqg - 1))
    q_chunks = ATT_WIDTH // W_CHUNK
    whole_s = lambda n: pl.BlockSpec((ns, n), lambda i: (0, 0))
    return pl.pallas_call(
        functools.partial(_norm_proj_kvq_kernel, nkv, nqg),
        grid=(nchunk + m // tm,),
        in_specs=[
            _resident(xs.shape), tile(d), _resident((1, d)), _resident((1, d)),
            pl.BlockSpec((d, W_CHUNK), kv_chunk), pl.BlockSpec((d, W_CHUNK), qg_chunk),
        ],
        out_specs=[
            whole_s(KV_WIDTH), whole_s(KV_WIDTH),
            pl.BlockSpec((ns, W_CHUNK), lambda i: (0, jnp.clip(i - nkv, 0, q_chunks - 1))),
            pl.BlockSpec((ns, W_CHUNK), lambda i: (0, jnp.clip(i - nkv - q_chunks, 0, nqg - q_chunks - 1))),
            tile(ATT_WIDTH), tile(ATT_WIDTH), tile(2 * KV_WIDTH),
            pl.BlockSpec((2 * KV_WIDTH, tm), lambda i: (0, jnp.maximum(i - nchunk, 0))), tail, tail,
        ],
        out_shape=[
            jax.ShapeDtypeStruct((ns, KV_WIDTH), F32),
            jax.ShapeDtypeStruct((ns, KV_WIDTH), F32),
            jax.ShapeDtypeStruct((ns, ATT_WIDTH), F32),
            jax.ShapeDtypeStruct((ns, w_qg.shape[1] - ATT_WIDTH), F32),
            jax.ShapeDtypeStruct((m, ATT_WIDTH), BF16),
            jax.ShapeDtypeStruct((m, ATT_WIDTH), F32),
            jax.ShapeDtypeStruct((m, 2 * KV_WIDTH), BF16),
            jax.ShapeDtypeStruct((2 * KV_WIDTH, m), BF16),
            jax.ShapeDtypeStruct((m // seq_len, KV_WIDTH, WINDOW), F32),
            jax.ShapeDtypeStruct((m // seq_len, KV_WIDTH, WINDOW), F32),
        ],
        scratch_shapes=[pltpu.VMEM((nkv, d, W_CHUNK), BF16), pltpu.VMEM((nqg, d, W_CHUNK), BF16)],
        compiler_params=_params("arbitrary"),
        name="norm_proj_kvq",
    )(xs, x, g_kv.reshape(1, d), g_q.reshape(1, d), w_kv, w_qg)


def _row_blocks(rows):
    sub = min(rows, SUB_ROWS)
    return [slice(r, r + sub) for r in range(0, rows, sub)]


def _proj_norm_res_kernel(nchunk, nparts, as_ref, gs_ref, xs_ref, *refs):
    y_refs = refs[:nparts]
    w_ref, g_ref, x_ref, os_ref, o_ref, w_scr, raw_scr = refs[nparts:]
    i = pl.program_id(0)

    @pl.when(i < nchunk)
    def _():
        wb = w_ref[...].astype(BF16)
        w_scr[i] = wb
        ys = (as_ref[...] * _silu(gs_ref[...])).astype(BF16)
        raw_scr[i] = jnp.dot(ys, wb, preferred_element_type=F32)

    @pl.when(i == nchunk - 1)
    def _():
        o = jnp.concatenate([raw_scr[c] for c in range(nchunk)], axis=1)
        os_ref[...] = xs_ref[...] + o * _rms_scale(o) * g_ref[...]

    @pl.when(i >= nchunk)
    def _():
        tm = x_ref.shape[0]
        part_rows = tm // nparts

        for rs in _row_blocks(tm):
            part, off = divmod(rs.start, part_rows)
            y = y_refs[part][off:off + rs.stop - rs.start, :]
            o = jnp.concatenate([jnp.dot(y, w_scr[c], preferred_element_type=F32)
                                 for c in range(nchunk)], axis=1)
            o_ref[rs, :] = x_ref[rs, :] + o * _rms_scale(o) * g_ref[...]


def proj_norm_res(a_s, gate_s, x_s, y_parts, w, g, x, tm):
    k, d = w.shape
    m = x.shape[0]
    nchunk = d // W_CHUNK
    nparts = len(y_parts)
    assert (tm // nparts) % min(tm, SUB_ROWS) == 0
    chunk_w, _, tile = _phase_specs(nchunk, tm, k)
    part = pl.BlockSpec((tm // nparts, k), lambda i: (jnp.maximum(i - nchunk, 0), 0))
    return pl.pallas_call(
        functools.partial(_proj_norm_res_kernel, nchunk, nparts),
        grid=(nchunk + m // tm,),
        in_specs=[_resident(a_s.shape), _resident(gate_s.shape), _resident(x_s.shape)]
        + [part] * nparts + [chunk_w, _resident((1, d)), tile(d)],
        out_specs=[pl.BlockSpec(x_s.shape, lambda i: (0, 0)), tile(d)],
        out_shape=[jax.ShapeDtypeStruct(x_s.shape, F32), jax.ShapeDtypeStruct((m, d), F32)],
        scratch_shapes=[pltpu.VMEM((nchunk, k, W_CHUNK), BF16),
                        pltpu.VMEM((nchunk, x_s.shape[0], W_CHUNK), F32)],
        compiler_params=_params("arbitrary"),
        name="proj_norm_res",
    )(a_s, gate_s, x_s, *y_parts, w, g.reshape(1, d), x)


def _lru_gate_dots(conv, wr_half, wi_half):
    cb = conv.astype(BF16)
    return (jnp.dot(cb, wr_half, preferred_element_type=F32),
            jnp.dot(cb, wi_half, preferred_element_type=F32))


def _lru_gates(conv, wr_half, br, wi_half, bi, lam):
    return _lru_gate_math(conv, _lru_gate_dots(conv, wr_half, wi_half), br, bi, lam)


def _lru_gate_math(conv, half_pre, br, bi, lam):
    th_r = jnp.tanh(half_pre[0] + 0.5 * br)
    th_i = jnp.tanh(half_pre[1] + 0.5 * bi)
    nl = -lam
    softplus = jnp.maximum(nl, 0.0) + jnp.log1p(jnp.exp(-jnp.abs(nl)))
    half = (0.5 * LRU_C) * softplus
    x = th_r * half + half
    a = jnp.exp2(x * -LOG2_E)
    z = jnp.tanh(x) * (a * a + 1.0)
    mult = z * lax.rsqrt(jnp.maximum(z, 1e-30))
    hc = 0.5 * conv
    return a, mult * (hc * th_i + hc)


def _interleave(*stages):
    live = [[stage, share] for stage, share in stages]
    while live:
        for entry in list(live):
            try:
                for _ in range(entry[1]):
                    next(entry[0])
            except StopIteration:
                live.remove(entry)


def _in_proj_tile(x_ref, rs, g_ref, w_scr, ug_ref):
    x = x_ref[rs, :]
    xn = (x * _rms_scale(x) * g_ref[...]).astype(BF16)
    xn = jnp.dot(_segment_major(xn.shape[0]), xn, preferred_element_type=F32).astype(BF16)
    for c in range(w_scr.shape[0]):
        for n0 in range(0, W_CHUNK, MXU_COLS):
            ug_ref[:, c * W_CHUNK + n0:c * W_CHUNK + n0 + MXU_COLS] = jnp.dot(
                xn, w_scr[c, :, n0:n0 + MXU_COLS], preferred_element_type=F32)
            yield


def _rglru_tile(ug_ref, y_ref, cw_ref, cb_ref, wr_ref, br_ref, wi_ref, bi_ref, lam_ref, h_scr, tail_scr):
    tc = ug_ref.shape[0]
    seg = tc // SUBLANES
    ntaps = CONV_W - 1
    bw = LRU_BLOCK_W
    sub = lax.broadcasted_iota(jnp.int32, (SUBLANES, bw), 0)
    first = sub == 0
    time_order = _segment_major(tc, inverse=True)

    def shift_in(x, row0):
        return jnp.where(first, row0, pltpu.roll(x, 1, axis=0))

    def group(x, j):
        return x[j * SUBLANES:(j + 1) * SUBLANES]

    def store_time_order(cols, y):
        y_ref[:, cols] = jnp.dot(time_order, y, preferred_element_type=F32).astype(y_ref.dtype)

    pending = None
    for n in range(LRU_BLOCKS):
        cs = slice(n * bw, (n + 1) * bw)
        u = ug_ref[:, cs]
        tail = tail_scr[:, cs]
        before = [shift_in(group(u, seg - m), tail[ntaps - m:ntaps - m + 1])
                  for m in range(ntaps, 0, -1)]
        ext = jnp.concatenate(before + [u], axis=0)
        tail_scr[:, cs] = jnp.concatenate(
            [group(u, seg - m)[SUBLANES - 1:] for m in range(ntaps, 0, -1)], axis=0)
        cw = cw_ref[:, cs]
        conv = cb_ref[:, cs]
        for tap in range(CONV_W):
            conv = conv + ext[tap * SUBLANES:tap * SUBLANES + tc] * cw[tap:tap + 1]
        yield

        half_pre = _lru_gate_dots(conv, wr_ref[n], wi_ref[n])
        yield

        if pending is not None:
            store_time_order(*pending)
        yield

        a, b = _lru_gate_math(conv, half_pre, br_ref[:, cs], bi_ref[:, cs], lam_ref[:, cs])

        h = b[:SUBLANES]
        acc = a[:SUBLANES]
        h_loc, a_cum = [h], [acc]
        for j in range(1, seg):
            sl = slice(j * SUBLANES, (j + 1) * SUBLANES)
            h = a[sl] * h + b[sl]
            acc = a[sl] * acc
            h_loc.append(h)
            a_cum.append(acc)

        step = 1
        while step < SUBLANES:
            keep = sub >= step
            h = jnp.where(keep, acc * pltpu.roll(h, step, axis=0) + h, h)
            acc = jnp.where(keep, acc * pltpu.roll(acc, step, axis=0), acc)
            step *= 2
        h_prev = h_scr[:, cs]
        after = h + acc * h_prev
        h_in = shift_in(after, h_prev)
        h_scr[:, cs] = after[SUBLANES - 1:]

        hs = jnp.concatenate([h_loc[j] + a_cum[j] * h_in for j in range(seg)], axis=0)
        y = (hs * _silu(ug_ref[:, LRU_WIDTH + n * bw:LRU_WIDTH + (n + 1) * bw])).astype(BF16)
        pending = (cs, y)
        yield

    store_time_order(*pending)
    yield


def _rglru_front_kernel(nchunk, npairs, chunks, xs_ref, x_ref, g_ref, w_ref, cprev_ref, h0_ref,
                        scprev_ref, sh0_ref, cw_ref, cb_ref, wr_ref, br_ref, wi_ref, bi_ref, lam_ref,
                        gs_ref, hs_ref, scnew_ref, y_even_ref, y_odd_ref, cnew_ref, hlast_ref,
                        w_scr, wr_scr, wi_scr, us_scr, ug0_scr, ug1_scr, h_scr, tail_scr):
    i = pl.program_id(0)
    p = i - nchunk
    tc = SUB_ROWS
    half = nchunk // 2
    lru = (cw_ref, cb_ref, wr_scr, br_ref, wi_scr, bi_ref, lam_ref, h_scr, tail_scr)

    @pl.when(i < nchunk)
    def _():
        wb = w_ref[...].astype(BF16)
        w_scr[i] = wb
        xs = xs_ref[...]
        xsn = (xs * _rms_scale(xs) * g_ref[...]).astype(BF16)
        r = jnp.dot(xsn, wb, preferred_element_type=F32)
        gs_ref[...] = r

        @pl.when(i < half)
        def _():
            us_scr[i] = r

    @pl.when(i == nchunk - 1)
    def _():
        wr_scr[...] = (0.5 * wr_ref[...]).astype(BF16)
        wi_scr[...] = (0.5 * wi_ref[...]).astype(BF16)
        bw = LRU_BLOCK_W
        for n in range(LRU_BLOCKS):
            cs = slice(n * bw, (n + 1) * bw)
            c, off = divmod(n * bw, W_CHUNK)
            u = us_scr[c, :, off:off + bw]
            cw = cw_ref[:, cs]
            conv = cb_ref[:, cs]
            for tap in range(CONV_W - 1):
                conv = conv + scprev_ref[tap, :, cs] * cw[tap:tap + 1]
                if tap > 0:
                    scnew_ref[tap - 1, :, cs] = scprev_ref[tap, :, cs]
            conv = conv + u * cw[CONV_W - 1:]
            scnew_ref[CONV_W - 2, :, cs] = u
            a, b = _lru_gates(conv, wr_scr[n], br_ref[:, cs], wi_scr[n], bi_ref[:, cs], lam_ref[:, cs])
            hs_ref[:, cs] = a * sh0_ref[:, cs] + b
        ug1_scr[...] = jnp.zeros_like(ug1_scr)
        h_scr[...] = jnp.zeros_like(h_scr)
        tail_scr[...] = jnp.zeros_like(tail_scr)

    @pl.when(p >= 0)
    def _():
        _interleave((_in_proj_tile(x_ref, slice(0, tc), g_ref, w_scr, ug0_scr), 1),
                    (_rglru_tile(ug1_scr, y_odd_ref, *lru), 2))
        hlast_ref[0] = h_scr[...]
        cnew_ref[0] = tail_scr[...]

    @pl.when((p >= 0) & (p < npairs))
    def _():
        @pl.when((2 * p) % chunks == 0)
        def _():
            h_scr[...] = h0_ref[0]
            tail_scr[...] = cprev_ref[0]

        _interleave((_in_proj_tile(x_ref, slice(tc, 2 * tc), g_ref, w_scr, ug1_scr), 1),
                    (_rglru_tile(ug0_scr, y_even_ref, *lru), 2))


def rglru_front(xs, s_conv_prev, s_h0, x, conv_prev, h0, g, w_in, conv_w, conv_b, w_r, b_r, w_i, b_i, lam,
                seq_len):
    m, d = x.shape
    ns = xs.shape[0]
    w = w_in.shape[1] // 2
    tc = SUB_ROWS
    nchunk = w_in.shape[1] // W_CHUNK
    half = nchunk // 2
    bsz = m // seq_len
    chunks = seq_len // tc
    npairs = m // (2 * tc)
    assert seq_len % (2 * tc) == 0 and tc % (SUBLANES * SUBLANES) == 0 and tc // SUBLANES > CONV_W
    pair = lambda i: jnp.clip(i - nchunk, 0, npairs - 1)
    last = npairs * 2 - 1
    seq_in = lambda i: (jnp.clip(2 * (i - nchunk), 0, last) // chunks, 0, 0)
    seq_out = lambda i: (jnp.clip(2 * (i - nchunk) - 1, 0, last) // chunks, 0, 0)
    state_in = lambda rows: pl.BlockSpec((1, rows, w), seq_in)
    state_out = lambda rows: pl.BlockSpec((1, rows, w), seq_out)
    chunk = lambda i: (0, jnp.minimum(i, nchunk - 1))
    return pl.pallas_call(
        functools.partial(_rglru_front_kernel, nchunk, npairs, chunks),
        grid=(nchunk + npairs + 1,),
        in_specs=[_resident(xs.shape), pl.BlockSpec((2 * tc, d), lambda i: (pair(i), 0)), _resident((1, d)),
                  pl.BlockSpec((d, W_CHUNK), chunk), state_in(CONV_W - 1), state_in(1),
                  _resident(s_conv_prev.shape), _resident(s_h0.shape),
                  _resident((CONV_W, w)), _resident((1, w)), _resident(w_r.shape), _resident((1, w)),
                  _resident(w_i.shape), _resident((1, w)), _resident((1, w))],
        out_specs=[
            pl.BlockSpec((ns, W_CHUNK), lambda i: (0, jnp.clip(i - half, 0, half - 1))),
            pl.BlockSpec((ns, w), lambda i: (0, 0)),
            pl.BlockSpec(s_conv_prev.shape, lambda i: (0, 0, 0)),
            pl.BlockSpec((tc, w), lambda i: (pair(i), 0)),
            pl.BlockSpec((tc, w), lambda i: (jnp.clip(i - nchunk - 1, 0, npairs - 1), 0)),
            state_out(CONV_W - 1), state_out(1),
        ],
        out_shape=[
            jax.ShapeDtypeStruct((ns, w), F32),
            jax.ShapeDtypeStruct((ns, w), F32),
            jax.ShapeDtypeStruct(s_conv_prev.shape, F32),
            jax.ShapeDtypeStruct((m // 2, w), BF16),
            jax.ShapeDtypeStruct((m // 2, w), BF16),
            jax.ShapeDtypeStruct((bsz, CONV_W - 1, w), F32),
            jax.ShapeDtypeStruct((bsz, 1, w), F32),
        ],
        scratch_shapes=[pltpu.VMEM((nchunk, d, W_CHUNK), BF16),
                        pltpu.VMEM(w_r.shape, BF16), pltpu.VMEM(w_i.shape, BF16),
                        pltpu.VMEM((half, ns, W_CHUNK), F32),
                        pltpu.VMEM((tc, 2 * w), F32), pltpu.VMEM((tc, 2 * w), F32),
                        pltpu.VMEM((1, w), F32), pltpu.VMEM((CONV_W - 1, w), F32)],
        compiler_params=_params("arbitrary"),
        name="rglru_front",
    )(xs, x, g.reshape(1, d), w_in, conv_prev, h0.reshape(bsz, 1, w), s_conv_prev, s_h0,
      conv_w, conv_b.reshape(1, w), w_r, b_r.reshape(1, w), w_i, b_i.reshape(1, w), lam.reshape(1, w))


def _buckets(dist):
    n = jnp.maximum(dist, 0)
    max_exact = N_BUCKETS // 2
    nf = jnp.maximum(n, 1).astype(F32)
    large = max_exact + jnp.floor(jnp.log(nf / max_exact) / math.log(MAX_DISTANCE / max_exact)
                                  * (N_BUCKETS - max_exact)).astype(jnp.int32)
    large = jnp.minimum(large, N_BUCKETS - 1)
    return jnp.where(n < max_exact, n, large)


def _lookup(bucket, valid, table_ref, head):
    bias = jnp.zeros(bucket.shape, F32)
    for b in range(N_BUCKETS):
        bias = jnp.where(bucket == b, table_ref[b, head], bias)
    return jnp.where(valid, bias, NEG_INF)


def _bias_kernel(table_ref, sinks_ref, band_ref, sinkt_ref, past_ref, new_ref):
    hk = pl.program_id(0)
    span = 3 * BLOCK
    dist = (lax.broadcasted_iota(jnp.int32, (1, span), 1) + BLOCK) % span
    bucket = _buckets(dist)
    in_window = (dist >= 0) & (dist < WINDOW)
    no_prev = lax.broadcasted_iota(jnp.int32, (2 * BLOCK, BLOCK), 0) < BLOCK

    def band(head):
        row = _lookup(bucket, in_window, table_ref, head) * LOG2_E
        full = pltpu.roll(jnp.broadcast_to(row, (2 * BLOCK, span)), 0, axis=1, stride=1, stride_axis=0)
        return full[:, :BLOCK]

    rows = past_ref.shape[2]
    d_past = rows - lax.broadcasted_iota(jnp.int32, (1, rows), 1)
    b_past = _buckets(d_past)
    ok_past = (d_past >= 0) & (d_past < WINDOW)
    d_new = jnp.zeros((1, LANES), jnp.int32)
    b_new = _buckets(d_new)
    for par in range(HEADS_PER_TILE):
        for slab in range(SLABS):
            g = slab * HEADS_PER_TILE + par
            head = hk * GROUP + g
            rs = slice(par * 2 * BLOCK, (par + 1) * 2 * BLOCK)
            cs = slice(slab * BLOCK, (slab + 1) * BLOCK)
            bias = band(head)
            band_ref[0, 0, rs, cs] = bias
            band_ref[1, 0, rs, cs] = jnp.where(no_prev, NEG_INF, bias)
            sinkt_ref[0, par, :, cs] = jnp.full((1, BLOCK), sinks_ref[head] * LOG2_E, F32)
            past_ref[0, g:g + 1, :] = _lookup(b_past, ok_past, table_ref, head)
            new_ref[0, g:g + 1, :] = _lookup(b_new, d_new == 0, table_ref, head)


def bias_tables(table, sinks, past_rows):
    smem = pl.BlockSpec(memory_space=pltpu.SMEM)
    return pl.pallas_call(
        _bias_kernel,
        grid=(N_KV_HEADS,),
        in_specs=[smem, smem],
        out_specs=[
            pl.BlockSpec((2, 1, HEADS_PER_TILE * 2 * BLOCK, SLABS * BLOCK), lambda h: (0, h, 0, 0)),
            pl.BlockSpec((1, HEADS_PER_TILE, 1, SLABS * BLOCK), lambda h: (h, 0, 0, 0)),
            pl.BlockSpec((1, GROUP, past_rows), lambda h: (h, 0, 0)),
            pl.BlockSpec((1, GROUP, LANES), lambda h: (h, 0, 0)),
        ],
        out_shape=[
            jax.ShapeDtypeStruct((2, N_KV_HEADS, HEADS_PER_TILE * 2 * BLOCK, SLABS * BLOCK), F32),
            jax.ShapeDtypeStruct((N_KV_HEADS, HEADS_PER_TILE, 1, SLABS * BLOCK), F32),
            jax.ShapeDtypeStruct((N_KV_HEADS, GROUP, past_rows), F32),
            jax.ShapeDtypeStruct((N_KV_HEADS, GROUP, LANES), F32),
        ],
        compiler_params=_params("parallel"),
        name="bias_tables",
    )(table, sinks)


def _band_attn_kernel(q_ref, kp_ref, kc_ref, vp_ref, vc_ref, gate_ref, bias_ref, sink_ref, y_ref,
                      s_scr, p_scr):
    first_tile = (pl.program_id(1) == 0).astype(jnp.int32)
    nt = (((1,), (1,)), ((), ()))
    low = (lax.broadcasted_iota(jnp.int32, (1, LANES), 1) < HEAD_DIM)
    keep_low = low.astype(BF16)
    keep_high = 1 - keep_low
    nkeys = 2 * BLOCK
    zeros_v = jnp.zeros((HEAD_DIM, nkeys), BF16)
    ones_rows = jnp.where(
        lax.broadcasted_iota(jnp.int32, (2 * SUBLANES, HEADS_PER_TILE * nkeys), 0)
        == lax.broadcasted_iota(jnp.int32, (2 * SUBLANES, HEADS_PER_TILE * nkeys), 1) // nkeys,
        1.0, 0.0).astype(BF16)
    rows = ATT_ROWS
    nslot = s_scr.shape[0]
    items = [(blk, hk) for blk in range(q_ref.shape[0] // BLOCK) for hk in range(N_KV_HEADS)]

    def rows_of(blk):
        return slice(blk * BLOCK, (blk + 1) * BLOCK)

    def scores(idx):
        blk, hk = items[idx]
        cs = slice(hk * LANES, (hk + 1) * LANES)
        k_prev = kp_ref[:, cs] if blk == 0 else kc_ref[rows_of(blk - 1), cs]
        kd = jnp.concatenate([k_prev, kc_ref[rows_of(blk), cs]], axis=0)
        lhs = jnp.concatenate([kd * keep_low, kd * keep_high], axis=0)
        qs = jnp.concatenate([q_ref[rows_of(blk), (hk * SLABS + s) * LANES:(hk * SLABS + s + 1) * LANES]
                              for s in range(SLABS)], axis=0)
        s_scr[idx % nslot] = lax.dot_general(lhs, qs, nt, preferred_element_type=F32)

    def softmax(idx):
        blk, hk = items[idx]
        slot = idx % nslot
        first = first_tile if blk == 0 else 0
        sink_w = []
        for par in range(HEADS_PER_TILE):
            base = par * nkeys
            sink = sink_ref[hk, par]
            top = None
            for r in range(base, base + nkeys, rows):
                sb = s_scr[slot, r:r + rows, :] + bias_ref[first, hk, r:r + rows, :]
                s_scr[slot, r:r + rows, :] = sb
                top = sb if top is None else jnp.maximum(top, sb)
            m = jnp.maximum(jnp.max(top, axis=0, keepdims=True), sink)
            for r in range(base, base + nkeys, rows):
                p_scr[slot, r:r + rows, :] = jnp.exp2(s_scr[slot, r:r + rows, :] - m).astype(BF16)
            sink_w.append(jnp.exp2(sink - m))
        return sink_w

    def weighted_values(idx):
        blk, hk = items[idx]
        cs = slice(hk * LANES, (hk + 1) * LANES)
        v_prev = vp_ref[cs, :] if blk == 0 else vc_ref[cs, rows_of(blk - 1)]
        vt = jnp.concatenate([v_prev, vc_ref[cs, rows_of(blk)]], axis=1)
        lhs_v = jnp.concatenate([
            jnp.concatenate([vt[:HEAD_DIM], zeros_v, ones_rows[:, :nkeys]], axis=0),
            jnp.concatenate([zeros_v, vt[HEAD_DIM:], ones_rows[:, nkeys:]], axis=0)], axis=1)
        return jnp.dot(lhs_v, p_scr[idx % nslot], preferred_element_type=F32)

    def finish(idx, ot, sink_w):
        blk, hk = items[idx]
        inv = [1.0 / (ot[LANES + par:LANES + par + 1] + sink_w[par]) for par in range(HEADS_PER_TILE)]
        ot = jnp.concatenate([ot[:HEAD_DIM] * inv[0], ot[HEAD_DIM:LANES] * inv[1]], axis=0)
        o = ot.T
        for sl in range(SLABS):
            c0 = (hk * SLABS + sl) * LANES
            y_ref[rows_of(blk), c0:c0 + LANES] = (
                o[sl * BLOCK:(sl + 1) * BLOCK]
                * _silu(gate_ref[rows_of(blk), c0:c0 + LANES])).astype(y_ref.dtype)

    n = len(items)
    sink_ws, outs = {}, {}
    for k in range(-2, n + 1):
        if 0 <= k + 2 < n:
            scores(k + 2)
        if 0 <= k + 1 < n:
            sink_ws[k + 1] = softmax(k + 1)
        if 0 <= k < n:
            outs[k] = weighted_values(k)
        if 0 <= k - 1 < n:
            finish(k - 1, outs.pop(k - 1), sink_ws.pop(k - 1))


def band_attention(q, kdup, vt, gate, bias_band, sink_t, bsz, t):
    m = q.shape[0]
    nblk = t // BLOCK
    per_tile = ATT_TILE // BLOCK
    ntile = t // ATT_TILE
    assert t % ATT_TILE == 0
    before = lambda b, i: b * nblk + jnp.maximum(per_tile * i - 1, 0)
    cur = lambda n: pl.BlockSpec((ATT_TILE, n), lambda b, i: (b * ntile + i, 0))
    prev = lambda n: pl.BlockSpec((BLOCK, n), lambda b, i: (before(b, i), 0))
    cur_t = pl.BlockSpec((2 * KV_WIDTH, ATT_TILE), lambda b, i: (0, b * ntile + i))
    prev_t = pl.BlockSpec((2 * KV_WIDTH, BLOCK), lambda b, i: (0, before(b, i)))
    score_tile = (HEADS_PER_TILE * 2 * BLOCK, SLABS * BLOCK)
    return pl.pallas_call(
        _band_attn_kernel,
        grid=(bsz, ntile),
        in_specs=[
            cur(ATT_WIDTH), prev(2 * KV_WIDTH), cur(2 * KV_WIDTH), prev_t, cur_t,
            cur(ATT_WIDTH), _resident(bias_band.shape), _resident(sink_t.shape),
        ],
        out_specs=cur(ATT_WIDTH),
        out_shape=jax.ShapeDtypeStruct((m, ATT_WIDTH), BF16),
        scratch_shapes=[pltpu.VMEM((ATT_SLOTS,) + score_tile, F32),
                        pltpu.VMEM((ATT_SLOTS,) + score_tile, BF16)],
        compiler_params=_params("parallel", "parallel"),
        name="band_attention",
    )(q, kdup, kdup, vt, vt, gate, bias_band, sink_t)


def _cached_attn_kernel(q_ref, ckt_ref, cvt_ref, kn_ref, vn_ref, sinks_ref, bpast_ref, bnew_ref, o_ref):
    shape = (N_Q_HEADS, KV_WIDTH)
    lane_kv = lax.broadcasted_iota(jnp.int32, shape, 1) // HEAD_DIM
    row_kv = lax.broadcasted_iota(jnp.int32, shape, 0) // GROUP
    own = lane_kv == row_kv
    sink = sinks_ref[...]
    nt = (((1,), (1,)), ((), ()))
    for b in range(q_ref.shape[0]):
        q = q_ref[b]
        qt = jnp.concatenate([q] * N_KV_HEADS, axis=1)
        qm = jnp.where(own, qt, 0.0).astype(BF16)
        knew = kn_ref[b].astype(BF16).astype(F32)
        vnew = vn_ref[b].astype(BF16).astype(F32)
        s = jnp.dot(qm, ckt_ref[b].astype(BF16), preferred_element_type=F32) + bpast_ref[...]
        s_new = jnp.sum(qm.astype(F32) * knew, axis=-1, keepdims=True) + bnew_ref[:, :1]
        m = jnp.maximum(jnp.maximum(jnp.max(s, axis=-1, keepdims=True), s_new), sink)
        p = jnp.exp(s - m)
        p_new = jnp.exp(s_new - m)
        denom = jnp.sum(p, axis=-1, keepdims=True) + p_new + jnp.exp(sink - m)
        o_all = (lax.dot_general(p.astype(BF16), cvt_ref[b].astype(BF16), nt, preferred_element_type=F32)
                 + p_new.astype(BF16).astype(F32) * vnew)
        o_all = jnp.where(own, o_all, 0.0)
        o = o_all[:, :HEAD_DIM]
        for hk in range(1, N_KV_HEADS):
            o = o + o_all[:, hk * HEAD_DIM:(hk + 1) * HEAD_DIM]
        o_ref[b] = o / denom


def cached_attention(q, cache_kt, cache_vt, k_new, v_new, sinks, bias_past, bias_new):
    bsz, _, rows = cache_kt.shape
    nseq = math.gcd(bsz, SEQS_PER_STEP)
    per_seq = lambda r, n: pl.BlockSpec((nseq, r, n), lambda b: (b, 0, 0))
    return pl.pallas_call(
        _cached_attn_kernel,
        grid=(bsz // nseq,),
        in_specs=[
            per_seq(N_Q_HEADS, HEAD_DIM), per_seq(KV_WIDTH, rows), per_seq(KV_WIDTH, rows),
            per_seq(1, KV_WIDTH), per_seq(1, KV_WIDTH),
            _resident((N_Q_HEADS, 1)), _resident((N_Q_HEADS, rows)), _resident((N_Q_HEADS, LANES)),
        ],
        out_specs=per_seq(N_Q_HEADS, HEAD_DIM),
        out_shape=jax.ShapeDtypeStruct((bsz, N_Q_HEADS, HEAD_DIM), F32),
        compiler_params=_params("parallel"),
        name="cached_attention",
    )(q, cache_kt, cache_vt, k_new, v_new, sinks, bias_past, bias_new)


def kernel(x_prompt, x_sample, state_conv, state_h, cache_k, cache_v, a_norm_pre, a_norm_post,
           a_w_in, a_conv_w, a_conv_b, a_w_r, a_b_r, a_w_i, a_b_i, a_lambda, a_w_out, kv_norm, w_kv,
           b_norm_pre, b_norm_post, b_w_qg, b_sinks, b_w_out, rel_bias_table):
    bsz, t, d = x_prompt.shape
    dbsz, dt, _ = x_sample.shape
    assert a_w_in.shape[0] == 1 and b_w_qg.shape[0] == 1 and dt == 1
    assert t % BLOCK == 0 and t >= WINDOW
    past_rows = cache_k.shape[1]
    assert past_rows == min(WINDOW, PAST_LEN)

    sinks = b_sinks[0]
    bias_band, sink_t, bias_past, bias_new = bias_tables(rel_bias_table, sinks, past_rows)

    tm = 2 * SUB_ROWS
    xp = x_prompt.reshape(bsz * t, d)
    xs = x_sample.reshape(dbsz, d)

    conv0 = jnp.zeros((bsz, CONV_W - 1, LRU_WIDTH), F32)
    h0 = jnp.zeros((bsz, LRU_WIDTH), F32)
    gate_s, hs, s_conv_t, y_even, y_odd, p_conv, p_h = rglru_front(
        xs, jnp.transpose(state_conv[0], (1, 0, 2)), state_h[0], xp, conv0, h0, a_norm_pre[0], a_w_in[0],
        a_conv_w[0], a_conv_b[0], a_w_r[0], a_b_r[0], a_w_i[0], a_b_i[0], a_lambda[0], seq_len=t)
    xs1, x1 = proj_norm_res(hs, gate_s, xs, (y_even, y_odd), a_w_out[0], a_norm_post[0], xp, tm)

    ks, vs, qs, gate_sb, q, gate_b, kdup, vt, k_tail, v_tail = norm_proj_kvq(
        xs1, x1, kv_norm, b_norm_pre[0], w_kv, b_w_qg[0], SUB_ROWS, seq_len=t)
    cache_kt = jnp.transpose(cache_k, (0, 2, 3, 1)).reshape(dbsz, KV_WIDTH, past_rows)
    cache_vt = jnp.transpose(cache_v, (0, 2, 3, 1)).reshape(dbsz, KV_WIDTH, past_rows)
    os_ = cached_attention(qs.reshape(dbsz, N_Q_HEADS, HEAD_DIM), cache_kt, cache_vt,
                           ks.reshape(dbsz, 1, KV_WIDTH), vs.reshape(dbsz, 1, KV_WIDTH),
                           sinks.reshape(N_Q_HEADS, 1), bias_past.reshape(N_Q_HEADS, past_rows),
                           bias_new.reshape(N_Q_HEADS, LANES))
    yb = band_attention(q, kdup, vt, gate_b, bias_band, sink_t, bsz, t)
    y_sample, y_prompt = proj_norm_res(os_.reshape(dbsz, ATT_WIDTH), gate_sb, xs1, (yb,),
                                       b_w_out[0], b_norm_post[0], x1, tm)
    y_prompt = y_prompt.reshape(bsz, t, d)
    p_k = jnp.transpose(k_tail.reshape(bsz, N_KV_HEADS, HEAD_DIM, WINDOW), (0, 3, 1, 2))
    p_v = jnp.transpose(v_tail.reshape(bsz, N_KV_HEADS, HEAD_DIM, WINDOW), (0, 3, 1, 2))

    return (y_prompt, y_sample.reshape(dbsz, 1, d),
            p_conv[None], p_h.reshape(1, bsz, LRU_WIDTH), p_k, p_v,
            jnp.transpose(s_conv_t, (1, 0, 2))[None], hs[None],
            ks.reshape(dbsz, 1, N_KV_HEADS, HEAD_DIM), vs.reshape(dbsz, 1, N_KV_HEADS, HEAD_DIM))
```

```python
import functools
import math

import jax
import jax.numpy as jnp
from jax import lax
from jax.experimental import pallas as pl
from jax.experimental.pallas import tpu as pltpu

F32 = jnp.float32
BF16 = jnp.bfloat16

D_MODEL = 2048
LRU_WIDTH = 2048
LRU_BLOCKS = 8
LRU_BLOCK_W = LRU_WIDTH // LRU_BLOCKS
CONV_W = 4
LRU_C = 8.0
HEAD_DIM = 64
N_Q_HEADS = 32
N_KV_HEADS = 8
GROUP = N_Q_HEADS // N_KV_HEADS
ATT_WIDTH = N_Q_HEADS * HEAD_DIM
KV_WIDTH = N_KV_HEADS * HEAD_DIM
WINDOW = 128
BLOCK = WINDOW
N_BUCKETS = 32
MAX_DISTANCE = 128
RMS_EPS = 1e-6
NEG_INF = -1e30
LOG2_E = 1.4426950408889634
PAST_LEN = 16384

V7X_VMEM_BYTES = 64 * 1024 * 1024
VMEM_LIMIT = V7X_VMEM_BYTES - 8 * 1024 * 1024
SUBLANES = 8
LANES = 128
HEADS_PER_TILE = LANES // HEAD_DIM
SLABS = GROUP // HEADS_PER_TILE
ATT_ROWS = 64
ATT_SLOTS = 3
ATT_TILE = 4 * BLOCK
SUB_ROWS = 256
W_CHUNK = 512
SEQS_PER_STEP = 8
MXU_COLS = 256


def _params(*semantics):
    return pltpu.CompilerParams(dimension_semantics=semantics, vmem_limit_bytes=VMEM_LIMIT)


def _resident(shape):
    zeros = (0,) * len(shape)
    return pl.BlockSpec(shape, lambda *_: zeros, pipeline_mode=pl.Buffered(1))


def _rms_scale(x):
    return lax.rsqrt(jnp.mean(x * x, axis=-1, keepdims=True) + RMS_EPS)


def _silu(x):
    h = 0.5 * x
    return h * jnp.tanh(h) + h


def _segment_major(rows, inverse=False):
    seg = rows // SUBLANES
    r = lax.broadcasted_iota(jnp.int32, (rows, rows), 0)
    c = lax.broadcasted_iota(jnp.int32, (rows, rows), 1)
    if inverse:
        src = (r % seg) * SUBLANES + r // seg
    else:
        src = (r % SUBLANES) * seg + r // SUBLANES
    return jnp.where(c == src, 1.0, 0.0).astype(BF16)


def _phase_specs(nchunk, tm, k):
    chunk_w = pl.BlockSpec((k, W_CHUNK), lambda i: (0, jnp.minimum(i, nchunk - 1)))
    chunk_o = lambda rows: pl.BlockSpec((rows, W_CHUNK), lambda i: (0, jnp.minimum(i, nchunk - 1)))
    tile = lambda n: pl.BlockSpec((tm, n), lambda i: (jnp.maximum(i - nchunk, 0), 0))
    return chunk_w, chunk_o, tile


def _dup_heads(x):
    low = lax.broadcasted_iota(jnp.int32, (x.shape[0], LANES), 1) < HEAD_DIM
    out = []
    for c in range(x.shape[1] // LANES):
        col = x[:, c * LANES:(c + 1) * LANES]
        swapped = pltpu.roll(col, HEAD_DIM, axis=1)
        out += [jnp.where(low, col, swapped), jnp.where(low, swapped, col)]
    return jnp.concatenate(out, axis=1)


def _norm_proj_kvq_kernel(nkv, nqg, xs_ref, x_ref, gkv_ref, gq_ref, wkv_ref, wqg_ref,
                          ks_ref, vs_ref, qs_ref, gates_ref,
                          q_ref, gate_ref, kdup_ref, vt_ref, ktail_ref, vtail_ref,
                          wkv_scr, wqg_scr):
    i = pl.program_id(0)
    nchunk = nkv + nqg
    q_chunks = ATT_WIDTH // W_CHUNK
    q_scale = 1.0 / math.sqrt(HEAD_DIM)
    q_scale_log2 = q_scale * LOG2_E

    def sample_rows(g_ref):
        xs = xs_ref[...]
        return (xs * _rms_scale(xs) * g_ref[...]).astype(BF16)

    @pl.when(i < nkv)
    def _():
        wb = wkv_ref[...].astype(BF16)
        wkv_scr[i] = wb
        r = jnp.dot(sample_rows(gkv_ref), wb, preferred_element_type=F32)

        @pl.when(i == 0)
        def _():
            ks_ref[...] = r

        @pl.when(i == 1)
        def _():
            vs_ref[...] = r

    @pl.when((i >= nkv) & (i < nchunk))
    def _():
        c = i - nkv
        wb = wqg_ref[...].astype(BF16)
        wqg_scr[c] = wb
        r = jnp.dot(sample_rows(gq_ref), wb, preferred_element_type=F32)

        @pl.when(c < q_chunks)
        def _():
            qs_ref[...] = r * q_scale

        @pl.when(c >= q_chunks)
        def _():
            gates_ref[...] = r

    @pl.when(i >= nchunk)
    def _():
        tm = x_ref.shape[0]
        for rs in _row_blocks(tm):
            x = x_ref[rs, :]
            xh = x * _rms_scale(x)
            xkv = (xh * gkv_ref[...]).astype(BF16)
            xq = (xh * gq_ref[...]).astype(BF16)
            k = jnp.dot(xkv, wkv_scr[0], preferred_element_type=F32)
            v = jnp.dot(xkv, wkv_scr[1], preferred_element_type=F32)
            kdup_ref[rs, :] = _dup_heads(k).astype(BF16)
            vt_ref[:, rs] = _dup_heads(v).T.astype(BF16)
            for c in range(nqg):
                r = jnp.dot(xq, wqg_scr[c], preferred_element_type=F32)
                if c < q_chunks:
                    q_ref[rs, c * W_CHUNK:(c + 1) * W_CHUNK] = (r * q_scale_log2).astype(q_ref.dtype)
                else:
                    cc = c - q_chunks
                    gate_ref[rs, cc * W_CHUNK:(cc + 1) * W_CHUNK] = r
        ktail_ref[0] = k[k.shape[0] - WINDOW:].T
        vtail_ref[0] = v[v.shape[0] - WINDOW:].T


def norm_proj_kvq(xs, x, g_kv, g_q, w_kv, w_qg, tm, seq_len):
    m, d = x.shape
    ns = xs.shape[0]
    assert w_kv.shape[1] == 2 * KV_WIDTH == 2 * W_CHUNK and seq_len % tm == 0 and tm >= WINDOW
    nkv, nqg = w_kv.shape[1] // W_CHUNK, w_qg.shape[1] // W_CHUNK
    nchunk = nkv + nqg
    tiles = seq_len // tm
    tile = lambda n: pl.BlockSpec((tm, n), lambda i: (jnp.maximum(i - nchunk, 0), 0))
    tail = pl.BlockSpec((1, KV_WIDTH, WINDOW), lambda i: (jnp.maximum(i - nchunk, 0) // tiles, 0, 0))
    kv_chunk = lambda i: (0, jnp.minimum(i, nkv - 1))
    qg_chunk = lambda i: (0, jnp.clip(i - nkv, 0, nqg - 1))
    q_chunks = ATT_WIDTH // W_CHUNK
    whole_s = lambda n: pl.BlockSpec((ns, n), lambda i: (0, 0))
    return pl.pallas_call(
        functools.partial(_norm_proj_kvq_kernel, nkv, nqg),
        grid=(nchunk + m // tm,),
        in_specs=[
            _resident(xs.shape), tile(d), _resident((1, d)), _resident((1, d)),
            pl.BlockSpec((d, W_CHUNK), kv_chunk), pl.BlockSpec((d, W_CHUNK), qg_chunk),
        ],
        out_specs=[
            whole_s(KV_WIDTH), whole_s(KV_WIDTH),
            pl.BlockSpec((ns, W_CHUNK), lambda i: (0, jnp.clip(i - nkv, 0, q_chunks - 1))),
            pl.BlockSpec((ns, W_CHUNK), lambda i: (0, jnp.clip(i - nkv - q_chunks, 0, nqg - q_chunks - 1))),
            tile(ATT_WIDTH), tile(ATT_WIDTH), tile(2 * KV_WIDTH),
            pl.BlockSpec((2 * KV_WIDTH, tm), lambda i: (0, jnp.maximum(i - nchunk, 0))), tail, tail,
        ],
        out_shape=[
            jax.ShapeDtypeStruct((ns, KV_WIDTH), F32),
            jax.ShapeDtypeStruct((ns, KV_WIDTH), F32),
            jax.ShapeDtypeStruct((ns, ATT_WIDTH), F32),
            jax.ShapeDtypeStruct((ns, w_qg.shape[1] - ATT_WIDTH), F32),
            jax.ShapeDtypeStruct((m, ATT_WIDTH), BF16),
            jax.ShapeDtypeStruct((m, ATT_WIDTH), F32),
            jax.ShapeDtypeStruct((m, 2 * KV_WIDTH), BF16),
            jax.ShapeDtypeStruct((2 * KV_WIDTH, m), BF16),
            jax.ShapeDtypeStruct((m // seq_len, KV_WIDTH, WINDOW), F32),
            jax.ShapeDtypeStruct((m // seq_len, KV_WIDTH, WINDOW), F32),
        ],
        scratch_shapes=[pltpu.VMEM((nkv, d, W_CHUNK), BF16), pltpu.VMEM((nqg, d, W_CHUNK), BF16)],
        compiler_params=_params("arbitrary"),
        name="norm_proj_kvq",
    )(xs, x, g_kv.reshape(1, d), g_q.reshape(1, d), w_kv, w_qg)


def _row_blocks(rows):
    sub = min(rows, SUB_ROWS)
    return [slice(r, r + sub) for r in range(0, rows, sub)]


def _proj_norm_res_kernel(nchunk, nparts, as_ref, gs_ref, xs_ref, *refs):
    y_refs = refs[:nparts]
    w_ref, g_ref, x_ref, os_ref, o_ref, w_scr, raw_scr = refs[nparts:]
    i = pl.program_id(0)

    @pl.when(i < nchunk)
    def _():
        wb = w_ref[...].astype(BF16)
        w_scr[i] = wb
        ys = (as_ref[...] * _silu(gs_ref[...])).astype(BF16)
        raw_scr[i] = jnp.dot(ys, wb, preferred_element_type=F32)

    @pl.when(i == nchunk - 1)
    def _():
        o = jnp.concatenate([raw_scr[c] for c in range(nchunk)], axis=1)
        os_ref[...] = xs_ref[...] + o * _rms_scale(o) * g_ref[...]

    @pl.when(i >= nchunk)
    def _():
        tm = x_ref.shape[0]
        part_rows = tm // nparts

        for rs in _row_blocks(tm):
            part, off = divmod(rs.start, part_rows)
            y = y_refs[part][off:off + rs.stop - rs.start, :]
            o = jnp.concatenate([jnp.dot(y, w_scr[c], preferred_element_type=F32)
                                 for c in range(nchunk)], axis=1)
            o_ref[rs, :] = x_ref[rs, :] + o * _rms_scale(o) * g_ref[...]


def proj_norm_res(a_s, gate_s, x_s, y_parts, w, g, x, tm):
    k, d = w.shape
    m = x.shape[0]
    nchunk = d // W_CHUNK
    nparts = len(y_parts)
    assert (tm // nparts) % min(tm, SUB_ROWS) == 0
    chunk_w, _, tile = _phase_specs(nchunk, tm, k)
    part = pl.BlockSpec((tm // nparts, k), lambda i: (jnp.maximum(i - nchunk, 0), 0))
    return pl.pallas_call(
        functools.partial(_proj_norm_res_kernel, nchunk, nparts),
        grid=(nchunk + m // tm,),
        in_specs=[_resident(a_s.shape), _resident(gate_s.shape), _resident(x_s.shape)]
        + [part] * nparts + [chunk_w, _resident((1, d)), tile(d)],
        out_specs=[pl.BlockSpec(x_s.shape, lambda i: (0, 0)), tile(d)],
        out_shape=[jax.ShapeDtypeStruct(x_s.shape, F32), jax.ShapeDtypeStruct((m, d), F32)],
        scratch_shapes=[pltpu.VMEM((nchunk, k, W_CHUNK), BF16),
                        pltpu.VMEM((nchunk, x_s.shape[0], W_CHUNK), F32)],
        compiler_params=_params("arbitrary"),
        name="proj_norm_res",
    )(a_s, gate_s, x_s, *y_parts, w, g.reshape(1, d), x)


def _lru_gate_dots(conv, wr_half, wi_half):
    cb = conv.astype(BF16)
    return (jnp.dot(cb, wr_half, preferred_element_type=F32),
            jnp.dot(cb, wi_half, preferred_element_type=F32))


def _lru_gates(conv, wr_half, br, wi_half, bi, lam):
    return _lru_gate_math(conv, _lru_gate_dots(conv, wr_half, wi_half), br, bi, lam)


def _lru_gate_math(conv, half_pre, br, bi, lam):
    th_r = jnp.tanh(half_pre[0] + 0.5 * br)
    th_i = jnp.tanh(half_pre[1] + 0.5 * bi)
    nl = -lam
    softplus = jnp.maximum(nl, 0.0) + jnp.log1p(jnp.exp(-jnp.abs(nl)))
    half = (0.5 * LRU_C) * softplus
    x = th_r * half + half
    a = jnp.exp2(x * -LOG2_E)
    z = jnp.tanh(x) * (a * a + 1.0)
    mult = z * lax.rsqrt(jnp.maximum(z, 1e-30))
    hc = 0.5 * conv
    return a, mult * (hc * th_i + hc)


def _interleave(*stages):
    live = [[stage, share] for stage, share in stages]
    while live:
        for entry in list(live):
            try:
                for _ in range(entry[1]):
                    next(entry[0])
            except StopIteration:
                live.remove(entry)


def _in_proj_tile(x_ref, rs, g_ref, w_scr, ug_ref):
    x = x_ref[rs, :]
    xn = (x * _rms_scale(x) * g_ref[...]).astype(BF16)
    xn = jnp.dot(_segment_major(xn.shape[0]), xn, preferred_element_type=F32).astype(BF16)
    for c in range(w_scr.shape[0]):
        for n0 in range(0, W_CHUNK, MXU_COLS):
            ug_ref[:, c * W_CHUNK + n0:c * W_CHUNK + n0 + MXU_COLS] = jnp.dot(
                xn, w_scr[c, :, n0:n0 + MXU_COLS], preferred_element_type=F32)
            yield


def _rglru_tile(ug_ref, y_ref, cw_ref, cb_ref, wr_ref, br_ref, wi_ref, bi_ref, lam_ref, h_scr, tail_scr):
    tc = ug_ref.shape[0]
    seg = tc // SUBLANES
    ntaps = CONV_W - 1
    bw = LRU_BLOCK_W
    sub = lax.broadcasted_iota(jnp.int32, (SUBLANES, bw), 0)
    first = sub == 0
    time_order = _segment_major(tc, inverse=True)

    def shift_in(x, row0):
        return jnp.where(first, row0, pltpu.roll(x, 1, axis=0))

    def group(x, j):
        return x[j * SUBLANES:(j + 1) * SUBLANES]

    def store_time_order(cols, y):
        y_ref[:, cols] = jnp.dot(time_order, y, preferred_element_type=F32).astype(y_ref.dtype)

    pending = None
    for n in range(LRU_BLOCKS):
        cs = slice(n * bw, (n + 1) * bw)
        u = ug_ref[:, cs]
        tail = tail_scr[:, cs]
        before = [shift_in(group(u, seg - m), tail[ntaps - m:ntaps - m + 1])
                  for m in range(ntaps, 0, -1)]
        ext = jnp.concatenate(before + [u], axis=0)
        tail_scr[:, cs] = jnp.concatenate(
            [group(u, seg - m)[SUBLANES - 1:] for m in range(ntaps, 0, -1)], axis=0)
        cw = cw_ref[:, cs]
        conv = cb_ref[:, cs]
        for tap in range(CONV_W):
            conv = conv + ext[tap * SUBLANES:tap * SUBLANES + tc] * cw[tap:tap + 1]
        yield

        half_pre = _lru_gate_dots(conv, wr_ref[n], wi_ref[n])
        yield

        if pending is not None:
            store_time_order(*pending)
        yield

        a, b = _lru_gate_math(conv, half_pre, br_ref[:, cs], bi_ref[:, cs], lam_ref[:, cs])

        h = b[:SUBLANES]
        acc = a[:SUBLANES]
        h_loc, a_cum = [h], [acc]
        for j in range(1, seg):
            sl = slice(j * SUBLANES, (j + 1) * SUBLANES)
            h = a[sl] * h + b[sl]
            acc = a[sl] * acc
            h_loc.append(h)
            a_cum.append(acc)

        step = 1
        while step < SUBLANES:
            keep = sub >= step
            h = jnp.where(keep, acc * pltpu.roll(h, step, axis=0) + h, h)
            acc = jnp.where(keep, acc * pltpu.roll(acc, step, axis=0), acc)
            step *= 2
        h_prev = h_scr[:, cs]
        after = h + acc * h_prev
        h_in = shift_in(after, h_prev)
        h_scr[:, cs] = after[SUBLANES - 1:]

        hs = jnp.concatenate([h_loc[j] + a_cum[j] * h_in for j in range(seg)], axis=0)
        y = (hs * _silu(ug_ref[:, LRU_WIDTH + n * bw:LRU_WIDTH + (n + 1) * bw])).astype(BF16)
        pending = (cs, y)
        yield

    store_time_order(*pending)
    yield


def _rglru_front_kernel(nchunk, npairs, chunks, xs_ref, x_ref, g_ref, w_ref, cprev_ref, h0_ref,
                        scprev_ref, sh0_ref, cw_ref, cb_ref, wr_ref, br_ref, wi_ref, bi_ref, lam_ref,
                        gs_ref, hs_ref, scnew_ref, y_even_ref, y_odd_ref, cnew_ref, hlast_ref,
                        w_scr, wr_scr, wi_scr, us_scr, ug0_scr, ug1_scr, h_scr, tail_scr):
    i = pl.program_id(0)
    p = i - nchunk
    tc = SUB_ROWS
    half = nchunk // 2
    lru = (cw_ref, cb_ref, wr_scr, br_ref, wi_scr, bi_ref, lam_ref, h_scr, tail_scr)

    @pl.when(i < nchunk)
    def _():
        wb = w_ref[...].astype(BF16)
        w_scr[i] = wb
        xs = xs_ref[...]
        xsn = (xs * _rms_scale(xs) * g_ref[...]).astype(BF16)
        r = jnp.dot(xsn, wb, preferred_element_type=F32)
        gs_ref[...] = r

        @pl.when(i < half)
        def _():
            us_scr[i] = r

    @pl.when(i == nchunk - 1)
    def _():
        wr_scr[...] = (0.5 * wr_ref[...]).astype(BF16)
        wi_scr[...] = (0.5 * wi_ref[...]).astype(BF16)
        bw = LRU_BLOCK_W
        for n in range(LRU_BLOCKS):
            cs = slice(n * bw, (n + 1) * bw)
            c, off = divmod(n * bw, W_CHUNK)
            u = us_scr[c, :, off:off + bw]
            cw = cw_ref[:, cs]
            conv = cb_ref[:, cs]
            for tap in range(CONV_W - 1):
                conv = conv + scprev_ref[tap, :, cs] * cw[tap:tap + 1]
                if tap > 0:
                    scnew_ref[tap - 1, :, cs] = scprev_ref[tap, :, cs]
            conv = conv + u * cw[CONV_W - 1:]
            scnew_ref[CONV_W - 2, :, cs] = u
            a, b = _lru_gates(conv, wr_scr[n], br_ref[:, cs], wi_scr[n], bi_ref[:, cs], lam_ref[:, cs])
            hs_ref[:, cs] = a * sh0_ref[:, cs] + b
        ug1_scr[...] = jnp.zeros_like(ug1_scr)
        h_scr[...] = jnp.zeros_like(h_scr)
        tail_scr[...] = jnp.zeros_like(tail_scr)

    @pl.when(p >= 0)
    def _():
        _interleave((_in_proj_tile(x_ref, slice(0, tc), g_ref, w_scr, ug0_scr), 1),
                    (_rglru_tile(ug1_scr, y_odd_ref, *lru), 2))
        hlast_ref[0] = h_scr[...]
        cnew_ref[0] = tail_scr[...]

    @pl.when((p >= 0) & (p < npairs))
    def _():
        @pl.when((2 * p) % chunks == 0)
        def _():
            h_scr[...] = h0_ref[0]
            tail_scr[...] = cprev_ref[0]

        _interleave((_in_proj_tile(x_ref, slice(tc, 2 * tc), g_ref, w_scr, ug1_scr), 1),
                    (_rglru_tile(ug0_scr, y_even_ref, *lru), 2))


def rglru_front(xs, s_conv_prev, s_h0, x, conv_prev, h0, g, w_in, conv_w, conv_b, w_r, b_r, w_i, b_i, lam,
                seq_len):
    m, d = x.shape
    ns = xs.shape[0]
    w = w_in.shape[1] // 2
    tc = SUB_ROWS
    nchunk = w_in.shape[1] // W_CHUNK
    half = nchunk // 2
    bsz = m // seq_len
    chunks = seq_len // tc
    npairs = m // (2 * tc)
    assert seq_len % (2 * tc) == 0 and tc % (SUBLANES * SUBLANES) == 0 and tc // SUBLANES > CONV_W
    pair = lambda i: jnp.clip(i - nchunk, 0, npairs - 1)
    last = npairs * 2 - 1
    seq_in = lambda i: (jnp.clip(2 * (i - nchunk), 0, last) // chunks, 0, 0)
    seq_out = lambda i: (jnp.clip(2 * (i - nchunk) - 1, 0, last) // chunks, 0, 0)
    state_in = lambda rows: pl.BlockSpec((1, rows, w), seq_in)
    state_out = lambda rows: pl.BlockSpec((1, rows, w), seq_out)
    chunk = lambda i: (0, jnp.minimum(i, nchunk - 1))
    return pl.pallas_call(
        functools.partial(_rglru_front_kernel, nchunk, npairs, chunks),
        grid=(nchunk + npairs + 1,),
        in_specs=[_resident(xs.shape), pl.BlockSpec((2 * tc, d), lambda i: (pair(i), 0)), _resident((1, d)),
                  pl.BlockSpec((d, W_CHUNK), chunk), state_in(CONV_W - 1), state_in(1),
                  _resident(s_conv_prev.shape), _resident(s_h0.shape),
                  _resident((CONV_W, w)), _resident((1, w)), _resident(w_r.shape), _resident((1, w)),
                  _resident(w_i.shape), _resident((1, w)), _resident((1, w))],
        out_specs=[
            pl.BlockSpec((ns, W_CHUNK), lambda i: (0, jnp.clip(i - half, 0, half - 1))),
            pl.BlockSpec((ns, w), lambda i: (0, 0)),
            pl.BlockSpec(s_conv_prev.shape, lambda i: (0, 0, 0)),
            pl.BlockSpec((tc, w), lambda i: (pair(i), 0)),
            pl.BlockSpec((tc, w), lambda i: (jnp.clip(i - nchunk - 1, 0, npairs - 1), 0)),
            state_out(CONV_W - 1), state_out(1),
        ],
        out_shape=[
            jax.ShapeDtypeStruct((ns, w), F32),
            jax.ShapeDtypeStruct((ns, w), F32),
            jax.ShapeDtypeStruct(s_conv_prev.shape, F32),
            jax.ShapeDtypeStruct((m // 2, w), BF16),
            jax.ShapeDtypeStruct((m // 2, w), BF16),
            jax.ShapeDtypeStruct((bsz, CONV_W - 1, w), F32),
            jax.ShapeDtypeStruct((bsz, 1, w), F32),
        ],
        scratch_shapes=[pltpu.VMEM((nchunk, d, W_CHUNK), BF16),
                        pltpu.VMEM(w_r.shape, BF16), pltpu.VMEM(w_i.shape, BF16),
                        pltpu.VMEM((half, ns, W_CHUNK), F32),
                        pltpu.VMEM((tc, 2 * w), F32), pltpu.VMEM((tc, 2 * w), F32),
                        pltpu.VMEM((1, w), F32), pltpu.VMEM((CONV_W - 1, w), F32)],
        compiler_params=_params("arbitrary"),
        name="rglru_front",
    )(xs, x, g.reshape(1, d), w_in, conv_prev, h0.reshape(bsz, 1, w), s_conv_prev, s_h0,
      conv_w, conv_b.reshape(1, w), w_r, b_r.reshape(1, w), w_i, b_i.reshape(1, w), lam.reshape(1, w))


def _buckets(dist):
    n = jnp.maximum(dist, 0)
    max_exact = N_BUCKETS // 2
    nf = jnp.maximum(n, 1).astype(F32)
    large = max_exact + jnp.floor(jnp.log(nf / max_exact) / math.log(MAX_DISTANCE / max_exact)
                                  * (N_BUCKETS - max_exact)).astype(jnp.int32)
    large = jnp.minimum(large, N_BUCKETS - 1)
    return jnp.where(n < max_exact, n, large)


def _lookup(bucket, valid, table_ref, head):
    bias = jnp.zeros(bucket.shape, F32)
    for b in range(N_BUCKETS):
        bias = jnp.where(bucket == b, table_ref[b, head], bias)
    return jnp.where(valid, bias, NEG_INF)


def _bias_kernel(table_ref, sinks_ref, band_ref, sinkt_ref, past_ref, new_ref):
    hk = pl.program_id(0)
    span = 3 * BLOCK
    dist = (lax.broadcasted_iota(jnp.int32, (1, span), 1) + BLOCK) % span
    bucket = _buckets(dist)
    in_window = (dist >= 0) & (dist < WINDOW)
    no_prev = lax.broadcasted_iota(jnp.int32, (2 * BLOCK, BLOCK), 0) < BLOCK

    def band(head):
        row = _lookup(bucket, in_window, table_ref, head) * LOG2_E
        full = pltpu.roll(jnp.broadcast_to(row, (2 * BLOCK, span)), 0, axis=1, stride=1, stride_axis=0)
        return full[:, :BLOCK]

    rows = past_ref.shape[2]
    d_past = rows - lax.broadcasted_iota(jnp.int32, (1, rows), 1)
    b_past = _buckets(d_past)
    ok_past = (d_past >= 0) & (d_past < WINDOW)
    d_new = jnp.zeros((1, LANES), jnp.int32)
    b_new = _buckets(d_new)
    for par in range(HEADS_PER_TILE):
        for slab in range(SLABS):
            g = slab * HEADS_PER_TILE + par
            head = hk * GROUP + g
            rs = slice(par * 2 * BLOCK, (par + 1) * 2 * BLOCK)
            cs = slice(slab * BLOCK, (slab + 1) * BLOCK)
            bias = band(head)
            band_ref[0, 0, rs, cs] = bias
            band_ref[1, 0, rs, cs] = jnp.where(no_prev, NEG_INF, bias)
            sinkt_ref[0, par, :, cs] = jnp.full((1, BLOCK), sinks_ref[head] * LOG2_E, F32)
            past_ref[0, g:g + 1, :] = _lookup(b_past, ok_past, table_ref, head)
            new_ref[0, g:g + 1, :] = _lookup(b_new, d_new == 0, table_ref, head)


def bias_tables(table, sinks, past_rows):
    smem = pl.BlockSpec(memory_space=pltpu.SMEM)
    return pl.pallas_call(
        _bias_kernel,
        grid=(N_KV_HEADS,),
        in_specs=[smem, smem],
        out_specs=[
            pl.BlockSpec((2, 1, HEADS_PER_TILE * 2 * BLOCK, SLABS * BLOCK), lambda h: (0, h, 0, 0)),
            pl.BlockSpec((1, HEADS_PER_TILE, 1, SLABS * BLOCK), lambda h: (h, 0, 0, 0)),
            pl.BlockSpec((1, GROUP, past_rows), lambda h: (h, 0, 0)),
            pl.BlockSpec((1, GROUP, LANES), lambda h: (h, 0, 0)),
        ],
        out_shape=[
            jax.ShapeDtypeStruct((2, N_KV_HEADS, HEADS_PER_TILE * 2 * BLOCK, SLABS * BLOCK), F32),
            jax.ShapeDtypeStruct((N_KV_HEADS, HEADS_PER_TILE, 1, SLABS * BLOCK), F32),
            jax.ShapeDtypeStruct((N_KV_HEADS, GROUP, past_rows), F32),
            jax.ShapeDtypeStruct((N_KV_HEADS, GROUP, LANES), F32),
        ],
        compiler_params=_params("parallel"),
        name="bias_tables",
    )(table, sinks)


def _band_attn_kernel(q_ref, kp_ref, kc_ref, vp_ref, vc_ref, gate_ref, bias_ref, sink_ref, y_ref,
                      s_scr, p_scr):
    first_tile = (pl.program_id(1) == 0).astype(jnp.int32)
    nt = (((1,), (1,)), ((), ()))
    low = (lax.broadcasted_iota(jnp.int32, (1, LANES), 1) < HEAD_DIM)
    keep_low = low.astype(BF16)
    keep_high = 1 - keep_low
    nkeys = 2 * BLOCK
    zeros_v = jnp.zeros((HEAD_DIM, nkeys), BF16)
    ones_rows = jnp.where(
        lax.broadcasted_iota(jnp.int32, (2 * SUBLANES, HEADS_PER_TILE * nkeys), 0)
        == lax.broadcasted_iota(jnp.int32, (2 * SUBLANES, HEADS_PER_TILE * nkeys), 1) // nkeys,
        1.0, 0.0).astype(BF16)
    rows = ATT_ROWS
    nslot = s_scr.shape[0]
    items = [(blk, hk) for blk in range(q_ref.shape[0] // BLOCK) for hk in range(N_KV_HEADS)]

    def rows_of(blk):
        return slice(blk * BLOCK, (blk + 1) * BLOCK)

    def scores(idx):
        blk, hk = items[idx]
        first = first_tile if blk == 0 else 0
        cs = slice(hk * LANES, (hk + 1) * LANES)
        k_prev = kp_ref[:, cs] if blk == 0 else kc_ref[rows_of(blk - 1), cs]
        kd = jnp.concatenate([k_prev, kc_ref[rows_of(blk), cs]], axis=0)
        qs = jnp.concatenate([q_ref[rows_of(blk), (hk * SLABS + s) * LANES:(hk * SLABS + s + 1) * LANES]
                              for s in range(SLABS)], axis=0)
        offsets = []
        for par, keep in enumerate((keep_low, keep_high)):
            base = par * nkeys
            s = (lax.dot_general(kd * keep, qs, nt, preferred_element_type=F32)
                 + bias_ref[first, hk, base:base + nkeys, :])
            s_scr[idx % nslot, base:base + nkeys, :] = s
            offsets.append(jnp.maximum(jnp.max(s, axis=0, keepdims=True), sink_ref[hk, par]))
        return offsets

    def softmax(idx, offsets):
        blk, hk = items[idx]
        slot = idx % nslot
        sink_w = []
        for par in range(HEADS_PER_TILE):
            base = par * nkeys
            m = offsets[par]
            for r in range(base, base + nkeys, rows):
                p_scr[slot, r:r + rows, :] = jnp.exp2(s_scr[slot, r:r + rows, :] - m).astype(BF16)
            sink_w.append(jnp.exp2(sink_ref[hk, par] - m))
        return sink_w

    def weighted_values(idx):
        blk, hk = items[idx]
        cs = slice(hk * LANES, (hk + 1) * LANES)
        v_prev = vp_ref[cs, :] if blk == 0 else vc_ref[cs, rows_of(blk - 1)]
        vt = jnp.concatenate([v_prev, vc_ref[cs, rows_of(blk)]], axis=1)
        lhs_v = jnp.concatenate([
            jnp.concatenate([vt[:HEAD_DIM], zeros_v, ones_rows[:, :nkeys]], axis=0),
            jnp.concatenate([zeros_v, vt[HEAD_DIM:], ones_rows[:, nkeys:]], axis=0)], axis=1)
        return jnp.dot(lhs_v, p_scr[idx % nslot], preferred_element_type=F32)

    def finish(idx, ot, sink_w):
        blk, hk = items[idx]
        inv = [1.0 / (ot[LANES + par:LANES + par + 1] + sink_w[par]) for par in range(HEADS_PER_TILE)]
        ot = jnp.concatenate([ot[:HEAD_DIM] * inv[0], ot[HEAD_DIM:LANES] * inv[1]], axis=0)
        o = ot.T
        for sl in range(SLABS):
            c0 = (hk * SLABS + sl) * LANES
            y_ref[rows_of(blk), c0:c0 + LANES] = (
                o[sl * BLOCK:(sl + 1) * BLOCK]
                * _silu(gate_ref[rows_of(blk), c0:c0 + LANES])).astype(y_ref.dtype)

    n = len(items)
    offs, sink_ws, outs = {}, {}, {}
    for k in range(-2, n + 1):
        if 0 <= k + 2 < n:
            offs[k + 2] = scores(k + 2)
        if 0 <= k + 1 < n:
            sink_ws[k + 1] = softmax(k + 1, offs.pop(k + 1))
        if 0 <= k < n:
            outs[k] = weighted_values(k)
        if 0 <= k - 1 < n:
            finish(k - 1, outs.pop(k - 1), sink_ws.pop(k - 1))


def band_attention(q, kdup, vt, gate, bias_band, sink_t, bsz, t):
    m = q.shape[0]
    nblk = t // BLOCK
    per_tile = ATT_TILE // BLOCK
    ntile = t // ATT_TILE
    assert t % ATT_TILE == 0
    before = lambda b, i: b * nblk + jnp.maximum(per_tile * i - 1, 0)
    cur = lambda n: pl.BlockSpec((ATT_TILE, n), lambda b, i: (b * ntile + i, 0))
    prev = lambda n: pl.BlockSpec((BLOCK, n), lambda b, i: (before(b, i), 0))
    cur_t = pl.BlockSpec((2 * KV_WIDTH, ATT_TILE), lambda b, i: (0, b * ntile + i))
    prev_t = pl.BlockSpec((2 * KV_WIDTH, BLOCK), lambda b, i: (0, before(b, i)))
    score_tile = (HEADS_PER_TILE * 2 * BLOCK, SLABS * BLOCK)
    return pl.pallas_call(
        _band_attn_kernel,
        grid=(bsz, ntile),
        in_specs=[
            cur(ATT_WIDTH), prev(2 * KV_WIDTH), cur(2 * KV_WIDTH), prev_t, cur_t,
            cur(ATT_WIDTH), _resident(bias_band.shape), _resident(sink_t.shape),
        ],
        out_specs=cur(ATT_WIDTH),
        out_shape=jax.ShapeDtypeStruct((m, ATT_WIDTH), BF16),
        scratch_shapes=[pltpu.VMEM((ATT_SLOTS,) + score_tile, F32),
                        pltpu.VMEM((ATT_SLOTS,) + score_tile, BF16)],
        compiler_params=_params("parallel", "parallel"),
        name="band_attention",
    )(q, kdup, kdup, vt, vt, gate, bias_band, sink_t)


def _cached_attn_kernel(q_ref, ckt_ref, cvt_ref, kn_ref, vn_ref, sinks_ref, bpast_ref, bnew_ref, o_ref):
    shape = (N_Q_HEADS, KV_WIDTH)
    lane_kv = lax.broadcasted_iota(jnp.int32, shape, 1) // HEAD_DIM
    row_kv = lax.broadcasted_iota(jnp.int32, shape, 0) // GROUP
    own = lane_kv == row_kv
    sink = sinks_ref[...]
    nt = (((1,), (1,)), ((), ()))
    for b in range(q_ref.shape[0]):
        q = q_ref[b]
        qt = jnp.concatenate([q] * N_KV_HEADS, axis=1)
        qm = jnp.where(own, qt, 0.0).astype(BF16)
        knew = kn_ref[b].astype(BF16).astype(F32)
        vnew = vn_ref[b].astype(BF16).astype(F32)
        s = jnp.dot(qm, ckt_ref[b].astype(BF16), preferred_element_type=F32) + bpast_ref[...]
        s_new = jnp.sum(qm.astype(F32) * knew, axis=-1, keepdims=True) + bnew_ref[:, :1]
        m = jnp.maximum(jnp.maximum(jnp.max(s, axis=-1, keepdims=True), s_new), sink)
        p = jnp.exp(s - m)
        p_new = jnp.exp(s_new - m)
        denom = jnp.sum(p, axis=-1, keepdims=True) + p_new + jnp.exp(sink - m)
        o_all = (lax.dot_general(p.astype(BF16), cvt_ref[b].astype(BF16), nt, preferred_element_type=F32)
                 + p_new.astype(BF16).astype(F32) * vnew)
        o_all = jnp.where(own, o_all, 0.0)
        o = o_all[:, :HEAD_DIM]
        for hk in range(1, N_KV_HEADS):
            o = o + o_all[:, hk * HEAD_DIM:(hk + 1) * HEAD_DIM]
        o_ref[b] = o / denom


def cached_attention(q, cache_kt, cache_vt, k_new, v_new, sinks, bias_past, bias_new):
    bsz, _, rows = cache_kt.shape
    nseq = math.gcd(bsz, SEQS_PER_STEP)
    per_seq = lambda r, n: pl.BlockSpec((nseq, r, n), lambda b: (b, 0, 0))
    return pl.pallas_call(
        _cached_attn_kernel,
        grid=(bsz // nseq,),
        in_specs=[
            per_seq(N_Q_HEADS, HEAD_DIM), per_seq(KV_WIDTH, rows), per_seq(KV_WIDTH, rows),
            per_seq(1, KV_WIDTH), per_seq(1, KV_WIDTH),
            _resident((N_Q_HEADS, 1)), _resident((N_Q_HEADS, rows)), _resident((N_Q_HEADS, LANES)),
        ],
        out_specs=per_seq(N_Q_HEADS, HEAD_DIM),
        out_shape=jax.ShapeDtypeStruct((bsz, N_Q_HEADS, HEAD_DIM), F32),
        compiler_params=_params("parallel"),
        name="cached_attention",
    )(q, cache_kt, cache_vt, k_new, v_new, sinks, bias_past, bias_new)


def kernel(x_prompt, x_sample, state_conv, state_h, cache_k, cache_v, a_norm_pre, a_norm_post,
           a_w_in, a_conv_w, a_conv_b, a_w_r, a_b_r, a_w_i, a_b_i, a_lambda, a_w_out, kv_norm, w_kv,
           b_norm_pre, b_norm_post, b_w_qg, b_sinks, b_w_out, rel_bias_table):
    bsz, t, d = x_prompt.shape
    dbsz, dt, _ = x_sample.shape
    assert a_w_in.shape[0] == 1 and b_w_qg.shape[0] == 1 and dt == 1
    assert t % BLOCK == 0 and t >= WINDOW
    past_rows = cache_k.shape[1]
    assert past_rows == min(WINDOW, PAST_LEN)

    sinks = b_sinks[0]
    bias_band, sink_t, bias_past, bias_new = bias_tables(rel_bias_table, sinks, past_rows)

    tm = 2 * SUB_ROWS
    xp = x_prompt.reshape(bsz * t, d)
    xs = x_sample.reshape(dbsz, d)

    conv0 = jnp.zeros((bsz, CONV_W - 1, LRU_WIDTH), F32)
    h0 = jnp.zeros((bsz, LRU_WIDTH), F32)
    gate_s, hs, s_conv_t, y_even, y_odd, p_conv, p_h = rglru_front(
        xs, jnp.transpose(state_conv[0], (1, 0, 2)), state_h[0], xp, conv0, h0, a_norm_pre[0], a_w_in[0],
        a_conv_w[0], a_conv_b[0], a_w_r[0], a_b_r[0], a_w_i[0], a_b_i[0], a_lambda[0], seq_len=t)
    xs1, x1 = proj_norm_res(hs, gate_s, xs, (y_even, y_odd), a_w_out[0], a_norm_post[0], xp, tm)

    ks, vs, qs, gate_sb, q, gate_b, kdup, vt, k_tail, v_tail = norm_proj_kvq(
        xs1, x1, kv_norm, b_norm_pre[0], w_kv, b_w_qg[0], SUB_ROWS, seq_len=t)
    cache_kt = jnp.transpose(cache_k, (0, 2, 3, 1)).reshape(dbsz, KV_WIDTH, past_rows)
    cache_vt = jnp.transpose(cache_v, (0, 2, 3, 1)).reshape(dbsz, KV_WIDTH, past_rows)
    os_ = cached_attention(qs.reshape(dbsz, N_Q_HEADS, HEAD_DIM), cache_kt, cache_vt,
                           ks.reshape(dbsz, 1, KV_WIDTH), vs.reshape(dbsz, 1, KV_WIDTH),
                           sinks.reshape(N_Q_HEADS, 1), bias_past.reshape(N_Q_HEADS, past_rows),
                           bias_new.reshape(N_Q_HEADS, LANES))
    yb = band_attention(q, kdup, vt, gate_b, bias_band, sink_t, bsz, t)
    y_sample, y_prompt = proj_norm_res(os_.reshape(dbsz, ATT_WIDTH), gate_sb, xs1, (yb,),
                                       b_w_out[0], b_norm_post[0], x1, tm)
    y_prompt = y_prompt.reshape(bsz, t, d)
    p_k = jnp.transpose(k_tail.reshape(bsz, N_KV_HEADS, HEAD_DIM, WINDOW), (0, 3, 1, 2))
    p_v = jnp.transpose(v_tail.reshape(bsz, N_KV_HEADS, HEAD_DIM, WINDOW), (0, 3, 1, 2))

    return (y_prompt, y_sample.reshape(dbsz, 1, d),
            p_conv[None], p_h.reshape(1, bsz, LRU_WIDTH), p_k, p_v,
            jnp.transpose(s_conv_t, (1, 0, 2))[None], hs[None],
            ks.reshape(dbsz, 1, N_KV_HEADS, HEAD_DIM), vs.reshape(dbsz, 1, N_KV_HEADS, HEAD_DIM))
```

```python
import functools
import math

import jax
import jax.numpy as jnp
from jax import lax
from jax.experimental import pallas as pl
from jax.experimental.pallas import tpu as pltpu

F32 = jnp.float32
BF16 = jnp.bfloat16

D_MODEL = 2048
LRU_WIDTH = 2048
LRU_BLOCKS = 8
LRU_BLOCK_W = LRU_WIDTH // LRU_BLOCKS
CONV_W = 4
LRU_C = 8.0
HEAD_DIM = 64
N_Q_HEADS = 32
N_KV_HEADS = 8
GROUP = N_Q_HEADS // N_KV_HEADS
ATT_WIDTH = N_Q_HEADS * HEAD_DIM
KV_WIDTH = N_KV_HEADS * HEAD_DIM
WINDOW = 128
BLOCK = WINDOW
N_BUCKETS = 32
MAX_DISTANCE = 128
RMS_EPS = 1e-6
NEG_INF = -1e30
LOG2_E = 1.4426950408889634
PAST_LEN = 16384

V7X_VMEM_BYTES = 64 * 1024 * 1024
VMEM_LIMIT = V7X_VMEM_BYTES - 8 * 1024 * 1024
SUBLANES = 8
LANES = 128
HEADS_PER_TILE = LANES // HEAD_DIM
SLABS = GROUP // HEADS_PER_TILE
ATT_ROWS = 64
ATT_SKEW = 2
ATT_SLOTS = 2 * ATT_SKEW + 2
HALF_Q = BLOCK // 2
ATT_KEYS = WINDOW + HALF_Q


def _head_order(half):
    heads = list(range(GROUP))
    return heads if half == 0 else [h ^ 1 for h in heads]
ATT_TILE = 4 * BLOCK
SUB_ROWS = 256
W_CHUNK = 512
SEQS_PER_STEP = 8
MXU_COLS = 256


def _params(*semantics):
    return pltpu.CompilerParams(dimension_semantics=semantics, vmem_limit_bytes=VMEM_LIMIT)


def _resident(shape):
    zeros = (0,) * len(shape)
    return pl.BlockSpec(shape, lambda *_: zeros, pipeline_mode=pl.Buffered(1))


def _rms_scale(x):
    return lax.rsqrt(jnp.mean(x * x, axis=-1, keepdims=True) + RMS_EPS)


def _silu(x):
    h = 0.5 * x
    return h * jnp.tanh(h) + h


def _segment_major(rows, inverse=False):
    seg = rows // SUBLANES
    r = lax.broadcasted_iota(jnp.int32, (rows, rows), 0)
    c = lax.broadcasted_iota(jnp.int32, (rows, rows), 1)
    if inverse:
        src = (r % seg) * SUBLANES + r // seg
    else:
        src = (r % SUBLANES) * seg + r // SUBLANES
    return jnp.where(c == src, 1.0, 0.0).astype(BF16)


def _phase_specs(nchunk, tm, k):
    chunk_w = pl.BlockSpec((k, W_CHUNK), lambda i: (0, jnp.minimum(i, nchunk - 1)))
    chunk_o = lambda rows: pl.BlockSpec((rows, W_CHUNK), lambda i: (0, jnp.minimum(i, nchunk - 1)))
    tile = lambda n: pl.BlockSpec((tm, n), lambda i: (jnp.maximum(i - nchunk, 0), 0))
    return chunk_w, chunk_o, tile


def _dup_heads(x):
    low = lax.broadcasted_iota(jnp.int32, (x.shape[0], LANES), 1) < HEAD_DIM
    out = []
    for c in range(x.shape[1] // LANES):
        col = x[:, c * LANES:(c + 1) * LANES]
        swapped = pltpu.roll(col, HEAD_DIM, axis=1)
        out += [jnp.where(low, col, swapped), jnp.where(low, swapped, col)]
    return jnp.concatenate(out, axis=1)


def _norm_proj_kvq_kernel(nkv, nqg, xs_ref, x_ref, gkv_ref, gq_ref, wkv_ref, wqg_ref,
                          ks_ref, vs_ref, qs_ref, gates_ref,
                          q_ref, gate_ref, kdup_ref, vt_ref, ktail_ref, vtail_ref,
                          wkv_scr, wqg_scr):
    i = pl.program_id(0)
    nchunk = nkv + nqg
    q_chunks = ATT_WIDTH // W_CHUNK
    q_scale = 1.0 / math.sqrt(HEAD_DIM)
    q_scale_log2 = q_scale * LOG2_E

    def sample_rows(g_ref):
        xs = xs_ref[...]
        return (xs * _rms_scale(xs) * g_ref[...]).astype(BF16)

    @pl.when(i < nkv)
    def _():
        wb = wkv_ref[...].astype(BF16)
        wkv_scr[i] = wb
        r = jnp.dot(sample_rows(gkv_ref), wb, preferred_element_type=F32)

        @pl.when(i == 0)
        def _():
            ks_ref[...] = r

        @pl.when(i == 1)
        def _():
            vs_ref[...] = r

    @pl.when((i >= nkv) & (i < nchunk))
    def _():
        c = i - nkv
        wb = wqg_ref[...].astype(BF16)
        wqg_scr[c] = wb
        r = jnp.dot(sample_rows(gq_ref), wb, preferred_element_type=F32)

        @pl.when(c < q_chunks)
        def _():
            qs_ref[...] = r * q_scale

        @pl.when(c >= q_chunks)
        def _():
            gates_ref[...] = r

    @pl.when(i >= nchunk)
    def _():
        tm = x_ref.shape[0]
        for rs in _row_blocks(tm):
            x = x_ref[rs, :]
            xh = x * _rms_scale(x)
            xkv = (xh * gkv_ref[...]).astype(BF16)
            xq = (xh * gq_ref[...]).astype(BF16)
            k = jnp.dot(xkv, wkv_scr[0], preferred_element_type=F32)
            v = jnp.dot(xkv, wkv_scr[1], preferred_element_type=F32)
            kdup_ref[rs, :] = _dup_heads(k).astype(BF16)
            vt_ref[:, rs] = _dup_heads(v).T.astype(BF16)
            for c in range(nqg):
                r = jnp.dot(xq, wqg_scr[c], preferred_element_type=F32)
                if c < q_chunks:
                    q_ref[rs, c * W_CHUNK:(c + 1) * W_CHUNK] = (r * q_scale_log2).astype(q_ref.dtype)
                else:
                    cc = c - q_chunks
                    gate_ref[rs, cc * W_CHUNK:(cc + 1) * W_CHUNK] = r
        ktail_ref[0] = k[k.shape[0] - WINDOW:].T
        vtail_ref[0] = v[v.shape[0] - WINDOW:].T


def norm_proj_kvq(xs, x, g_kv, g_q, w_kv, w_qg, tm, seq_len):
    m, d = x.shape
    ns = xs.shape[0]
    assert w_kv.shape[1] == 2 * KV_WIDTH == 2 * W_CHUNK and seq_len % tm == 0 and tm >= WINDOW
    nkv, nqg = w_kv.shape[1] // W_CHUNK, w_qg.shape[1] // W_CHUNK
    nchunk = nkv + nqg
    tiles = seq_len // tm
    tile = lambda n: pl.BlockSpec((tm, n), lambda i: (jnp.maximum(i - nchunk, 0), 0))
    tail = pl.BlockSpec((1, KV_WIDTH, WINDOW), lambda i: (jnp.maximum(i - nchunk, 0) // tiles, 0, 0))
    kv_chunk = lambda i: (0, jnp.minimum(i, nkv - 1))
    qg_chunk = lambda i: (0, jnp.clip(i - nkv, 0, nqg - 1))
    q_chunks = ATT_WIDTH // W_CHUNK
    whole_s = lambda n: pl.BlockSpec((ns, n), lambda i: (0, 0))
    return pl.pallas_call(
        functools.partial(_norm_proj_kvq_kernel, nkv, nqg),
        grid=(nchunk + m // tm,),
        in_specs=[
            _resident(xs.shape), tile(d), _resident((1, d)), _resident((1, d)),
            pl.BlockSpec((d, W_CHUNK), kv_chunk), pl.BlockSpec((d, W_CHUNK), qg_chunk),
        ],
        out_specs=[
            whole_s(KV_WIDTH), whole_s(KV_WIDTH),
            pl.BlockSpec((ns, W_CHUNK), lambda i: (0, jnp.clip(i - nkv, 0, q_chunks - 1))),
            pl.BlockSpec((ns, W_CHUNK), lambda i: (0, jnp.clip(i - nkv - q_chunks, 0, nqg - q_chunks - 1))),
            tile(ATT_WIDTH), tile(ATT_WIDTH), tile(2 * KV_WIDTH),
            pl.BlockSpec((2 * KV_WIDTH, tm), lambda i: (0, jnp.maximum(i - nchunk, 0))), tail, tail,
        ],
        out_shape=[
            jax.ShapeDtypeStruct((ns, KV_WIDTH), F32),
            jax.ShapeDtypeStruct((ns, KV_WIDTH), F32),
            jax.ShapeDtypeStruct((ns, ATT_WIDTH), F32),
            jax.ShapeDtypeStruct((ns, w_qg.shape[1] - ATT_WIDTH), F32),
            jax.ShapeDtypeStruct((m, ATT_WIDTH), BF16),
            jax.ShapeDtypeStruct((m, ATT_WIDTH), F32),
            jax.ShapeDtypeStruct((m, 2 * KV_WIDTH), BF16),
            jax.ShapeDtypeStruct((2 * KV_WIDTH, m), BF16),
            jax.ShapeDtypeStruct((m // seq_len, KV_WIDTH, WINDOW), F32),
            jax.ShapeDtypeStruct((m // seq_len, KV_WIDTH, WINDOW), F32),
        ],
        scratch_shapes=[pltpu.VMEM((nkv, d, W_CHUNK), BF16), pltpu.VMEM((nqg, d, W_CHUNK), BF16)],
        compiler_params=_params("arbitrary"),
        name="norm_proj_kvq",
    )(xs, x, g_kv.reshape(1, d), g_q.reshape(1, d), w_kv, w_qg)


def _row_blocks(rows):
    sub = min(rows, SUB_ROWS)
    return [slice(r, r + sub) for r in range(0, rows, sub)]


def _proj_norm_res_kernel(nchunk, nparts, as_ref, gs_ref, xs_ref, *refs):
    y_refs = refs[:nparts]
    w_ref, g_ref, x_ref, os_ref, o_ref, w_scr, raw_scr = refs[nparts:]
    i = pl.program_id(0)

    @pl.when(i < nchunk)
    def _():
        wb = w_ref[...].astype(BF16)
        w_scr[i] = wb
        ys = (as_ref[...] * _silu(gs_ref[...])).astype(BF16)
        raw_scr[i] = jnp.dot(ys, wb, preferred_element_type=F32)

    @pl.when(i == nchunk - 1)
    def _():
        o = jnp.concatenate([raw_scr[c] for c in range(nchunk)], axis=1)
        os_ref[...] = xs_ref[...] + o * _rms_scale(o) * g_ref[...]

    @pl.when(i >= nchunk)
    def _():
        tm = x_ref.shape[0]
        part_rows = tm // nparts

        for rs in _row_blocks(tm):
            part, off = divmod(rs.start, part_rows)
            y = y_refs[part][off:off + rs.stop - rs.start, :]
            o = jnp.concatenate([jnp.dot(y, w_scr[c], preferred_element_type=F32)
                                 for c in range(nchunk)], axis=1)
            o_ref[rs, :] = x_ref[rs, :] + o * _rms_scale(o) * g_ref[...]


def proj_norm_res(a_s, gate_s, x_s, y_parts, w, g, x, tm):
    k, d = w.shape
    m = x.shape[0]
    nchunk = d // W_CHUNK
    nparts = len(y_parts)
    assert (tm // nparts) % min(tm, SUB_ROWS) == 0
    chunk_w, _, tile = _phase_specs(nchunk, tm, k)
    part = pl.BlockSpec((tm // nparts, k), lambda i: (jnp.maximum(i - nchunk, 0), 0))
    return pl.pallas_call(
        functools.partial(_proj_norm_res_kernel, nchunk, nparts),
        grid=(nchunk + m // tm,),
        in_specs=[_resident(a_s.shape), _resident(gate_s.shape), _resident(x_s.shape)]
        + [part] * nparts + [chunk_w, _resident((1, d)), tile(d)],
        out_specs=[pl.BlockSpec(x_s.shape, lambda i: (0, 0)), tile(d)],
        out_shape=[jax.ShapeDtypeStruct(x_s.shape, F32), jax.ShapeDtypeStruct((m, d), F32)],
        scratch_shapes=[pltpu.VMEM((nchunk, k, W_CHUNK), BF16),
                        pltpu.VMEM((nchunk, x_s.shape[0], W_CHUNK), F32)],
        compiler_params=_params("arbitrary"),
        name="proj_norm_res",
    )(a_s, gate_s, x_s, *y_parts, w, g.reshape(1, d), x)


def _lru_gate_dots(conv, wr_half, wi_half):
    cb = conv.astype(BF16)
    return (jnp.dot(cb, wr_half, preferred_element_type=F32),
            jnp.dot(cb, wi_half, preferred_element_type=F32))


def _lru_gates(conv, wr_half, br, wi_half, bi, lam):
    return _lru_gate_math(conv, _lru_gate_dots(conv, wr_half, wi_half), br, bi, lam)


def _lru_gate_math(conv, half_pre, br, bi, lam):
    th_r = jnp.tanh(half_pre[0] + 0.5 * br)
    th_i = jnp.tanh(half_pre[1] + 0.5 * bi)
    nl = -lam
    softplus = jnp.maximum(nl, 0.0) + jnp.log1p(jnp.exp(-jnp.abs(nl)))
    half = (0.5 * LRU_C) * softplus
    x = th_r * half + half
    a = jnp.exp2(x * -LOG2_E)
    z = jnp.tanh(x) * (a * a + 1.0)
    mult = z * lax.rsqrt(jnp.maximum(z, 1e-30))
    hc = 0.5 * conv
    return a, mult * (hc * th_i + hc)


def _interleave(*stages):
    live = [[stage, share] for stage, share in stages]
    while live:
        for entry in list(live):
            try:
                for _ in range(entry[1]):
                    next(entry[0])
            except StopIteration:
                live.remove(entry)


def _in_proj_tile(x_ref, rs, g_ref, w_scr, ug_ref):
    x = x_ref[rs, :]
    xn = (x * _rms_scale(x) * g_ref[...]).astype(BF16)
    xn = jnp.dot(_segment_major(xn.shape[0]), xn, preferred_element_type=F32).astype(BF16)
    for c in range(w_scr.shape[0]):
        for n0 in range(0, W_CHUNK, MXU_COLS):
            ug_ref[:, c * W_CHUNK + n0:c * W_CHUNK + n0 + MXU_COLS] = jnp.dot(
                xn, w_scr[c, :, n0:n0 + MXU_COLS], preferred_element_type=F32)
            yield


def _rglru_tile(ug_ref, y_ref, cw_ref, cb_ref, wr_ref, br_ref, wi_ref, bi_ref, lam_ref, h_scr, tail_scr):
    tc = ug_ref.shape[0]
    seg = tc // SUBLANES
    ntaps = CONV_W - 1
    bw = LRU_BLOCK_W
    sub = lax.broadcasted_iota(jnp.int32, (SUBLANES, bw), 0)
    first = sub == 0
    time_order = _segment_major(tc, inverse=True)

    def shift_in(x, row0):
        return jnp.where(first, row0, pltpu.roll(x, 1, axis=0))

    def group(x, j):
        return x[j * SUBLANES:(j + 1) * SUBLANES]

    def store_time_order(cols, y):
        y_ref[:, cols] = jnp.dot(time_order, y, preferred_element_type=F32).astype(y_ref.dtype)

    pending = None
    for n in range(LRU_BLOCKS):
        cs = slice(n * bw, (n + 1) * bw)
        u = ug_ref[:, cs]
        tail = tail_scr[:, cs]
        before = [shift_in(group(u, seg - m), tail[ntaps - m:ntaps - m + 1])
                  for m in range(ntaps, 0, -1)]
        ext = jnp.concatenate(before + [u], axis=0)
        tail_scr[:, cs] = jnp.concatenate(
            [group(u, seg - m)[SUBLANES - 1:] for m in range(ntaps, 0, -1)], axis=0)
        cw = cw_ref[:, cs]
        conv = cb_ref[:, cs]
        for tap in range(CONV_W):
            conv = conv + ext[tap * SUBLANES:tap * SUBLANES + tc] * cw[tap:tap + 1]
        yield

        half_pre = _lru_gate_dots(conv, wr_ref[n], wi_ref[n])
        yield

        if pending is not None:
            store_time_order(*pending)
        yield

        a, b = _lru_gate_math(conv, half_pre, br_ref[:, cs], bi_ref[:, cs], lam_ref[:, cs])

        h = b[:SUBLANES]
        acc = a[:SUBLANES]
        h_loc, a_cum = [h], [acc]
        for j in range(1, seg):
            sl = slice(j * SUBLANES, (j + 1) * SUBLANES)
            h = a[sl] * h + b[sl]
            acc = a[sl] * acc
            h_loc.append(h)
            a_cum.append(acc)

        step = 1
        while step < SUBLANES:
            keep = sub >= step
            h = jnp.where(keep, acc * pltpu.roll(h, step, axis=0) + h, h)
            acc = jnp.where(keep, acc * pltpu.roll(acc, step, axis=0), acc)
            step *= 2
        h_prev = h_scr[:, cs]
        after = h + acc * h_prev
        h_in = shift_in(after, h_prev)
        h_scr[:, cs] = after[SUBLANES - 1:]

        hs = jnp.concatenate([h_loc[j] + a_cum[j] * h_in for j in range(seg)], axis=0)
        y = (hs * _silu(ug_ref[:, LRU_WIDTH + n * bw:LRU_WIDTH + (n + 1) * bw])).astype(BF16)
        pending = (cs, y)
        yield

    store_time_order(*pending)
    yield


def _rglru_front_kernel(nchunk, npairs, chunks, xs_ref, x_ref, g_ref, w_ref, cprev_ref, h0_ref,
                        scprev_ref, sh0_ref, cw_ref, cb_ref, wr_ref, br_ref, wi_ref, bi_ref, lam_ref,
                        gs_ref, hs_ref, scnew_ref, y_even_ref, y_odd_ref, cnew_ref, hlast_ref,
                        w_scr, wr_scr, wi_scr, us_scr, ug0_scr, ug1_scr, h_scr, tail_scr):
    i = pl.program_id(0)
    p = i - nchunk
    tc = SUB_ROWS
    half = nchunk // 2
    lru = (cw_ref, cb_ref, wr_scr, br_ref, wi_scr, bi_ref, lam_ref, h_scr, tail_scr)

    @pl.when(i < nchunk)
    def _():
        wb = w_ref[...].astype(BF16)
        w_scr[i] = wb
        xs = xs_ref[...]
        xsn = (xs * _rms_scale(xs) * g_ref[...]).astype(BF16)
        r = jnp.dot(xsn, wb, preferred_element_type=F32)
        gs_ref[...] = r

        @pl.when(i < half)
        def _():
            us_scr[i] = r

    @pl.when(i == nchunk - 1)
    def _():
        wr_scr[...] = (0.5 * wr_ref[...]).astype(BF16)
        wi_scr[...] = (0.5 * wi_ref[...]).astype(BF16)
        bw = LRU_BLOCK_W
        for n in range(LRU_BLOCKS):
            cs = slice(n * bw, (n + 1) * bw)
            c, off = divmod(n * bw, W_CHUNK)
            u = us_scr[c, :, off:off + bw]
            cw = cw_ref[:, cs]
            conv = cb_ref[:, cs]
            for tap in range(CONV_W - 1):
                conv = conv + scprev_ref[tap, :, cs] * cw[tap:tap + 1]
                if tap > 0:
                    scnew_ref[tap - 1, :, cs] = scprev_ref[tap, :, cs]
            conv = conv + u * cw[CONV_W - 1:]
            scnew_ref[CONV_W - 2, :, cs] = u
            a, b = _lru_gates(conv, wr_scr[n], br_ref[:, cs], wi_scr[n], bi_ref[:, cs], lam_ref[:, cs])
            hs_ref[:, cs] = a * sh0_ref[:, cs] + b
        ug1_scr[...] = jnp.zeros_like(ug1_scr)
        h_scr[...] = jnp.zeros_like(h_scr)
        tail_scr[...] = jnp.zeros_like(tail_scr)

    @pl.when(p >= 0)
    def _():
        _interleave((_in_proj_tile(x_ref, slice(0, tc), g_ref, w_scr, ug0_scr), 1),
                    (_rglru_tile(ug1_scr, y_odd_ref, *lru), 2))
        hlast_ref[0] = h_scr[...]
        cnew_ref[0] = tail_scr[...]

    @pl.when((p >= 0) & (p < npairs))
    def _():
        @pl.when((2 * p) % chunks == 0)
        def _():
            h_scr[...] = h0_ref[0]
            tail_scr[...] = cprev_ref[0]

        _interleave((_in_proj_tile(x_ref, slice(tc, 2 * tc), g_ref, w_scr, ug1_scr), 1),
                    (_rglru_tile(ug0_scr, y_even_ref, *lru), 2))


def rglru_front(xs, s_conv_prev, s_h0, x, conv_prev, h0, g, w_in, conv_w, conv_b, w_r, b_r, w_i, b_i, lam,
                seq_len):
    m, d = x.shape
    ns = xs.shape[0]
    w = w_in.shape[1] // 2
    tc = SUB_ROWS
    nchunk = w_in.shape[1] // W_CHUNK
    half = nchunk // 2
    bsz = m // seq_len
    chunks = seq_len // tc
    npairs = m // (2 * tc)
    assert seq_len % (2 * tc) == 0 and tc % (SUBLANES * SUBLANES) == 0 and tc // SUBLANES > CONV_W
    pair = lambda i: jnp.clip(i - nchunk, 0, npairs - 1)
    last = npairs * 2 - 1
    seq_in = lambda i: (jnp.clip(2 * (i - nchunk), 0, last) // chunks, 0, 0)
    seq_out = lambda i: (jnp.clip(2 * (i - nchunk) - 1, 0, last) // chunks, 0, 0)
    state_in = lambda rows: pl.BlockSpec((1, rows, w), seq_in)
    state_out = lambda rows: pl.BlockSpec((1, rows, w), seq_out)
    chunk = lambda i: (0, jnp.minimum(i, nchunk - 1))
    return pl.pallas_call(
        functools.partial(_rglru_front_kernel, nchunk, npairs, chunks),
        grid=(nchunk + npairs + 1,),
        in_specs=[_resident(xs.shape), pl.BlockSpec((2 * tc, d), lambda i: (pair(i), 0)), _resident((1, d)),
                  pl.BlockSpec((d, W_CHUNK), chunk), state_in(CONV_W - 1), state_in(1),
                  _resident(s_conv_prev.shape), _resident(s_h0.shape),
                  _resident((CONV_W, w)), _resident((1, w)), _resident(w_r.shape), _resident((1, w)),
                  _resident(w_i.shape), _resident((1, w)), _resident((1, w))],
        out_specs=[
            pl.BlockSpec((ns, W_CHUNK), lambda i: (0, jnp.clip(i - half, 0, half - 1))),
            pl.BlockSpec((ns, w), lambda i: (0, 0)),
            pl.BlockSpec(s_conv_prev.shape, lambda i: (0, 0, 0)),
            pl.BlockSpec((tc, w), lambda i: (pair(i), 0)),
            pl.BlockSpec((tc, w), lambda i: (jnp.clip(i - nchunk - 1, 0, npairs - 1), 0)),
            state_out(CONV_W - 1), state_out(1),
        ],
        out_shape=[
            jax.ShapeDtypeStruct((ns, w), F32),
            jax.ShapeDtypeStruct((ns, w), F32),
            jax.ShapeDtypeStruct(s_conv_prev.shape, F32),
            jax.ShapeDtypeStruct((m // 2, w), BF16),
            jax.ShapeDtypeStruct((m // 2, w), BF16),
            jax.ShapeDtypeStruct((bsz, CONV_W - 1, w), F32),
            jax.ShapeDtypeStruct((bsz, 1, w), F32),
        ],
        scratch_shapes=[pltpu.VMEM((nchunk, d, W_CHUNK), BF16),
                        pltpu.VMEM(w_r.shape, BF16), pltpu.VMEM(w_i.shape, BF16),
                        pltpu.VMEM((half, ns, W_CHUNK), F32),
                        pltpu.VMEM((tc, 2 * w), F32), pltpu.VMEM((tc, 2 * w), F32),
                        pltpu.VMEM((1, w), F32), pltpu.VMEM((CONV_W - 1, w), F32)],
        compiler_params=_params("arbitrary"),
        name="rglru_front",
    )(xs, x, g.reshape(1, d), w_in, conv_prev, h0.reshape(bsz, 1, w), s_conv_prev, s_h0,
      conv_w, conv_b.reshape(1, w), w_r, b_r.reshape(1, w), w_i, b_i.reshape(1, w), lam.reshape(1, w))


def _buckets(dist):
    n = jnp.maximum(dist, 0)
    max_exact = N_BUCKETS // 2
    nf = jnp.maximum(n, 1).astype(F32)
    large = max_exact + jnp.floor(jnp.log(nf / max_exact) / math.log(MAX_DISTANCE / max_exact)
                                  * (N_BUCKETS - max_exact)).astype(jnp.int32)
    large = jnp.minimum(large, N_BUCKETS - 1)
    return jnp.where(n < max_exact, n, large)


def _lookup(bucket, valid, table_ref, head):
    bias = jnp.zeros(bucket.shape, F32)
    for b in range(N_BUCKETS):
        bias = jnp.where(bucket == b, table_ref[b, head], bias)
    return jnp.where(valid, bias, NEG_INF)


def _bias_kernel(table_ref, sinks_ref, band_ref, sinkt_ref, past_ref, new_ref):
    hk = pl.program_id(0)
    span = 3 * BLOCK
    dist = (lax.broadcasted_iota(jnp.int32, (1, span), 1) + BLOCK) % span
    bucket = _buckets(dist)
    in_window = (dist >= 0) & (dist < WINDOW)
    key_row = lax.broadcasted_iota(jnp.int32, (ATT_KEYS, HALF_Q), 0)

    def band(head):
        row = _lookup(bucket, in_window, table_ref, head) * LOG2_E
        full = pltpu.roll(jnp.broadcast_to(row, (ATT_KEYS, span)), 0, axis=1, stride=1, stride_axis=0)
        return full[:, :HALF_Q]

    rows = past_ref.shape[2]
    d_past = rows - lax.broadcasted_iota(jnp.int32, (1, rows), 1)
    b_past = _buckets(d_past)
    ok_past = (d_past >= 0) & (d_past < WINDOW)
    d_new = jnp.zeros((1, LANES), jnp.int32)
    b_new = _buckets(d_new)
    for g in range(GROUP):
        head = hk * GROUP + g
        bias = band(head)
        for half in range(2):
            slot = _head_order(half).index(g)
            cs = slice(slot * HALF_Q, (slot + 1) * HALF_Q)
            band_ref[0, half, 0, :, cs] = bias
            prev_rows = BLOCK - half * HALF_Q
            band_ref[1, half, 0, :, cs] = jnp.where(key_row < prev_rows, NEG_INF, bias)
            sinkt_ref[half, 0, :, cs] = jnp.full((1, HALF_Q), sinks_ref[head] * LOG2_E, F32)
        past_ref[0, g:g + 1, :] = _lookup(b_past, ok_past, table_ref, head)
        new_ref[0, g:g + 1, :] = _lookup(b_new, d_new == 0, table_ref, head)


def bias_tables(table, sinks, past_rows):
    smem = pl.BlockSpec(memory_space=pltpu.SMEM)
    return pl.pallas_call(
        _bias_kernel,
        grid=(N_KV_HEADS,),
        in_specs=[smem, smem],
        out_specs=[
            pl.BlockSpec((2, 2, 1, ATT_KEYS, GROUP * HALF_Q), lambda h: (0, 0, h, 0, 0)),
            pl.BlockSpec((2, 1, 1, GROUP * HALF_Q), lambda h: (0, h, 0, 0)),
            pl.BlockSpec((1, GROUP, past_rows), lambda h: (h, 0, 0)),
            pl.BlockSpec((1, GROUP, LANES), lambda h: (h, 0, 0)),
        ],
        out_shape=[
            jax.ShapeDtypeStruct((2, 2, N_KV_HEADS, ATT_KEYS, GROUP * HALF_Q), F32),
            jax.ShapeDtypeStruct((2, N_KV_HEADS, 1, GROUP * HALF_Q), F32),
            jax.ShapeDtypeStruct((N_KV_HEADS, GROUP, past_rows), F32),
            jax.ShapeDtypeStruct((N_KV_HEADS, GROUP, LANES), F32),
        ],
        compiler_params=_params("parallel"),
        name="bias_tables",
    )(table, sinks)


def _band_attn_kernel(q_ref, kp_ref, kc_ref, vp_ref, vc_ref, gate_ref, bias_ref, sink_ref, y_ref,
                      s_scr, p_scr):
    first_tile = (pl.program_id(1) == 0).astype(jnp.int32)
    nt = (((1,), (1,)), ((), ()))
    low = (lax.broadcasted_iota(jnp.int32, (1, LANES), 1) < HEAD_DIM)
    keep_low = low.astype(BF16)
    keep_high = 1 - keep_low
    keep = (keep_low, keep_high)
    nkeys = 2 * BLOCK
    ones_rows = jnp.where(lax.broadcasted_iota(jnp.int32, (2 * SUBLANES, nkeys), 0) == 0,
                          1.0, 0.0).astype(BF16)
    rows = ATT_ROWS
    nslot = s_scr.shape[0]
    items = [(blk, hk, half) for blk in range(q_ref.shape[0] // BLOCK)
             for hk in range(N_KV_HEADS) for half in range(2)]
    assert nslot % 2 == 0

    def rows_of(blk):
        return slice(blk * BLOCK, (blk + 1) * BLOCK)

    def key_rows(half):
        return slice(half * HALF_Q, half * HALF_Q + ATT_KEYS)

    for slot in range(nslot):
        dead = slice(ATT_KEYS, nkeys) if slot % 2 == 0 else slice(0, HALF_Q)
        p_scr[slot, dead, :] = jnp.zeros((HALF_Q, p_scr.shape[2]), BF16)

    def scores(idx):
        blk, hk, half = items[idx]
        variant = first_tile if blk == 0 else 0
        cs = slice(hk * LANES, (hk + 1) * LANES)
        k_prev = kp_ref[:, cs] if blk == 0 else kc_ref[rows_of(blk - 1), cs]
        k_cur = kc_ref[rows_of(blk), cs]
        kd = (jnp.concatenate([k_prev, k_cur[:HALF_Q]], axis=0) if half == 0
              else jnp.concatenate([k_prev[HALF_Q:], k_cur], axis=0))
        q0 = blk * BLOCK + half * HALF_Q
        qs = jnp.concatenate(
            [q_ref[q0:q0 + HALF_Q, (hk * SLABS + h // HEADS_PER_TILE) * LANES:
                   (hk * SLABS + h // HEADS_PER_TILE + 1) * LANES] * keep[h % HEADS_PER_TILE]
             for h in _head_order(half)], axis=0)
        s = lax.dot_general(kd, qs, nt, preferred_element_type=F32) + bias_ref[variant, half, hk]
        s_scr[idx % nslot, key_rows(half), :] = s
        return jnp.maximum(jnp.max(s, axis=0, keepdims=True), sink_ref[half, hk])

    def softmax(idx, m):
        blk, hk, half = items[idx]
        slot = idx % nslot
        lo = half * HALF_Q
        for r in range(lo, lo + ATT_KEYS, rows):
            p_scr[slot, r:r + rows, :] = jnp.exp2(s_scr[slot, r:r + rows, :] - m).astype(BF16)
        return jnp.exp2(sink_ref[half, hk] - m)

    def weighted_values(idx):
        blk, hk, half = items[idx]
        vs = slice(hk * LANES, hk * LANES + HEAD_DIM)
        v_prev = vp_ref[vs, :] if blk == 0 else vc_ref[vs, rows_of(blk - 1)]
        vt = jnp.concatenate([v_prev, vc_ref[vs, rows_of(blk)]], axis=1)
        lhs_v = jnp.concatenate([vt, ones_rows], axis=0)
        return jnp.dot(lhs_v, p_scr[idx % nslot], preferred_element_type=F32)

    low_q = lax.broadcasted_iota(jnp.int32, (HEAD_DIM, LANES), 1) < HALF_Q

    def finish(blk, hk, ots, sink_ws):
        o = [ots[half][:HEAD_DIM] * (1.0 / (ots[half][HEAD_DIM:HEAD_DIM + 1] + sink_ws[half]))
             for half in range(2)]
        for sl in range(SLABS):
            a, b = (oh[:, sl * LANES:(sl + 1) * LANES] for oh in o)
            even = jnp.where(low_q, a, b)
            odd = pltpu.roll(jnp.where(low_q, b, a), HALF_Q, axis=1)
            pair = jnp.concatenate([even, odd], axis=0)
            c0 = (hk * SLABS + sl) * LANES
            y_ref[rows_of(blk), c0:c0 + LANES] = (
                pair.T * _silu(gate_ref[rows_of(blk), c0:c0 + LANES])).astype(y_ref.dtype)

    n = len(items)
    offs, sink_ws, outs = {}, {}, {}
    for k in range(-2 * ATT_SKEW, n + ATT_SKEW):
        if 0 <= k + 2 * ATT_SKEW < n:
            offs[k + 2 * ATT_SKEW] = scores(k + 2 * ATT_SKEW)
        if 0 <= k + ATT_SKEW < n:
            sink_ws[k + ATT_SKEW] = softmax(k + ATT_SKEW, offs.pop(k + ATT_SKEW))
        if 0 <= k < n:
            outs[k] = weighted_values(k)
        j = k - ATT_SKEW
        if 0 <= j < n and items[j][2] == 1:
            finish(items[j][0], items[j][1], [outs.pop(j - 1), outs.pop(j)],
                   [sink_ws.pop(j - 1), sink_ws.pop(j)])


def band_attention(q, kdup, vt, gate, bias_band, sink_t, bsz, t):
    m = q.shape[0]
    nblk = t // BLOCK
    per_tile = ATT_TILE // BLOCK
    ntile = t // ATT_TILE
    assert t % ATT_TILE == 0
    before = lambda b, i: b * nblk + jnp.maximum(per_tile * i - 1, 0)
    cur = lambda n: pl.BlockSpec((ATT_TILE, n), lambda b, i: (b * ntile + i, 0))
    prev = lambda n: pl.BlockSpec((BLOCK, n), lambda b, i: (before(b, i), 0))
    cur_t = pl.BlockSpec((2 * KV_WIDTH, ATT_TILE), lambda b, i: (0, b * ntile + i))
    prev_t = pl.BlockSpec((2 * KV_WIDTH, BLOCK), lambda b, i: (0, before(b, i)))
    score_tile = (2 * BLOCK, GROUP * HALF_Q)
    return pl.pallas_call(
        _band_attn_kernel,
        grid=(bsz, ntile),
        in_specs=[
            cur(ATT_WIDTH), prev(2 * KV_WIDTH), cur(2 * KV_WIDTH), prev_t, cur_t,
            cur(ATT_WIDTH), _resident(bias_band.shape), _resident(sink_t.shape),
        ],
        out_specs=cur(ATT_WIDTH),
        out_shape=jax.ShapeDtypeStruct((m, ATT_WIDTH), BF16),
        scratch_shapes=[pltpu.VMEM((ATT_SLOTS,) + score_tile, F32),
                        pltpu.VMEM((ATT_SLOTS,) + score_tile, BF16)],
        compiler_params=_params("parallel", "parallel"),
        name="band_attention",
    )(q, kdup, kdup, vt, vt, gate, bias_band, sink_t)


def _cached_attn_kernel(q_ref, ckt_ref, cvt_ref, kn_ref, vn_ref, sinks_ref, bpast_ref, bnew_ref, o_ref):
    shape = (N_Q_HEADS, KV_WIDTH)
    lane_kv = lax.broadcasted_iota(jnp.int32, shape, 1) // HEAD_DIM
    row_kv = lax.broadcasted_iota(jnp.int32, shape, 0) // GROUP
    own = lane_kv == row_kv
    sink = sinks_ref[...]
    nt = (((1,), (1,)), ((), ()))
    for b in range(q_ref.shape[0]):
        q = q_ref[b]
        qt = jnp.concatenate([q] * N_KV_HEADS, axis=1)
        qm = jnp.where(own, qt, 0.0).astype(BF16)
        knew = kn_ref[b].astype(BF16).astype(F32)
        vnew = vn_ref[b].astype(BF16).astype(F32)
        s = jnp.dot(qm, ckt_ref[b].astype(BF16), preferred_element_type=F32) + bpast_ref[...]
        s_new = jnp.sum(qm.astype(F32) * knew, axis=-1, keepdims=True) + bnew_ref[:, :1]
        m = jnp.maximum(jnp.maximum(jnp.max(s, axis=-1, keepdims=True), s_new), sink)
        p = jnp.exp(s - m)
        p_new = jnp.exp(s_new - m)
        denom = jnp.sum(p, axis=-1, keepdims=True) + p_new + jnp.exp(sink - m)
        o_all = (lax.dot_general(p.astype(BF16), cvt_ref[b].astype(BF16), nt, preferred_element_type=F32)
                 + p_new.astype(BF16).astype(F32) * vnew)
        o_all = jnp.where(own, o_all, 0.0)
        o = o_all[:, :HEAD_DIM]
        for hk in range(1, N_KV_HEADS):
            o = o + o_all[:, hk * HEAD_DIM:(hk + 1) * HEAD_DIM]
        o_ref[b] = o / denom


def cached_attention(q, cache_kt, cache_vt, k_new, v_new, sinks, bias_past, bias_new):
    bsz, _, rows = cache_kt.shape
    nseq = math.gcd(bsz, SEQS_PER_STEP)
    per_seq = lambda r, n: pl.BlockSpec((nseq, r, n), lambda b: (b, 0, 0))
    return pl.pallas_call(
        _cached_attn_kernel,
        grid=(bsz // nseq,),
        in_specs=[
            per_seq(N_Q_HEADS, HEAD_DIM), per_seq(KV_WIDTH, rows), per_seq(KV_WIDTH, rows),
            per_seq(1, KV_WIDTH), per_seq(1, KV_WIDTH),
            _resident((N_Q_HEADS, 1)), _resident((N_Q_HEADS, rows)), _resident((N_Q_HEADS, LANES)),
        ],
        out_specs=per_seq(N_Q_HEADS, HEAD_DIM),
        out_shape=jax.ShapeDtypeStruct((bsz, N_Q_HEADS, HEAD_DIM), F32),
        compiler_params=_params("parallel"),
        name="cached_attention",
    )(q, cache_kt, cache_vt, k_new, v_new, sinks, bias_past, bias_new)


def kernel(x_prompt, x_sample, state_conv, state_h, cache_k, cache_v, a_norm_pre, a_norm_post,
           a_w_in, a_conv_w, a_conv_b, a_w_r, a_b_r, a_w_i, a_b_i, a_lambda, a_w_out, kv_norm, w_kv,
           b_norm_pre, b_norm_post, b_w_qg, b_sinks, b_w_out, rel_bias_table):
    bsz, t, d = x_prompt.shape
    dbsz, dt, _ = x_sample.shape
    assert a_w_in.shape[0] == 1 and b_w_qg.shape[0] == 1 and dt == 1
    assert t % BLOCK == 0 and t >= WINDOW
    past_rows = cache_k.shape[1]
    assert past_rows == min(WINDOW, PAST_LEN)

    sinks = b_sinks[0]
    bias_band, sink_t, bias_past, bias_new = bias_tables(rel_bias_table, sinks, past_rows)

    tm = 2 * SUB_ROWS
    xp = x_prompt.reshape(bsz * t, d)
    xs = x_sample.reshape(dbsz, d)

    conv0 = jnp.zeros((bsz, CONV_W - 1, LRU_WIDTH), F32)
    h0 = jnp.zeros((bsz, LRU_WIDTH), F32)
    gate_s, hs, s_conv_t, y_even, y_odd, p_conv, p_h = rglru_front(
        xs, jnp.transpose(state_conv[0], (1, 0, 2)), state_h[0], xp, conv0, h0, a_norm_pre[0], a_w_in[0],
        a_conv_w[0], a_conv_b[0], a_w_r[0], a_b_r[0], a_w_i[0], a_b_i[0], a_lambda[0], seq_len=t)
    xs1, x1 = proj_norm_res(hs, gate_s, xs, (y_even, y_odd), a_w_out[0], a_norm_post[0], xp, tm)

    ks, vs, qs, gate_sb, q, gate_b, kdup, vt, k_tail, v_tail = norm_proj_kvq(
        xs1, x1, kv_norm, b_norm_pre[0], w_kv, b_w_qg[0], SUB_ROWS, seq_len=t)
    cache_kt = jnp.transpose(cache_k, (0, 2, 3, 1)).reshape(dbsz, KV_WIDTH, past_rows)
    cache_vt = jnp.transpose(cache_v, (0, 2, 3, 1)).reshape(dbsz, KV_WIDTH, past_rows)
    os_ = cached_attention(qs.reshape(dbsz, N_Q_HEADS, HEAD_DIM), cache_kt, cache_vt,
                           ks.reshape(dbsz, 1, KV_WIDTH), vs.reshape(dbsz, 1, KV_WIDTH),
                           sinks.reshape(N_Q_HEADS, 1), bias_past.reshape(N_Q_HEADS, past_rows),
                           bias_new.reshape(N_Q_HEADS, LANES))
    yb = band_attention(q, kdup, vt, gate_b, bias_band, sink_t, bsz, t)
    y_sample, y_prompt = proj_norm_res(os_.reshape(dbsz, ATT_WIDTH), gate_sb, xs1, (yb,),
                                       b_w_out[0], b_norm_post[0], x1, tm)
    y_prompt = y_prompt.reshape(bsz, t, d)
    p_k = jnp.transpose(k_tail.reshape(bsz, N_KV_HEADS, HEAD_DIM, WINDOW), (0, 3, 1, 2))
    p_v = jnp.transpose(v_tail.reshape(bsz, N_KV_HEADS, HEAD_DIM, WINDOW), (0, 3, 1, 2))

    return (y_prompt, y_sample.reshape(dbsz, 1, d),
            p_conv[None], p_h.reshape(1, bsz, LRU_WIDTH), p_k, p_v,
            jnp.transpose(s_conv_t, (1, 0, 2))[None], hs[None],
            ks.reshape(dbsz, 1, N_KV_HEADS, HEAD_DIM), vs.reshape(dbsz, 1, N_KV_HEADS, HEAD_DIM))
```

```python
import functools
import math

import jax
import jax.numpy as jnp
from jax import lax
from jax.experimental import pallas as pl
from jax.experimental.pallas import tpu as pltpu

F32 = jnp.float32
BF16 = jnp.bfloat16

D_MODEL = 2048
LRU_WIDTH = 2048
LRU_BLOCKS = 8
LRU_BLOCK_W = LRU_WIDTH // LRU_BLOCKS
CONV_W = 4
LRU_C = 8.0
HEAD_DIM = 64
N_Q_HEADS = 32
N_KV_HEADS = 8
GROUP = N_Q_HEADS // N_KV_HEADS
ATT_WIDTH = N_Q_HEADS * HEAD_DIM
KV_WIDTH = N_KV_HEADS * HEAD_DIM
WINDOW = 128
BLOCK = WINDOW
N_BUCKETS = 32
MAX_DISTANCE = 128
RMS_EPS = 1e-6
NEG_INF = -1e30
LOG2_E = 1.4426950408889634
PAST_LEN = 16384

V7X_VMEM_BYTES = 64 * 1024 * 1024
VMEM_LIMIT = V7X_VMEM_BYTES - 8 * 1024 * 1024
SUBLANES = 8
LANES = 128
HEADS_PER_TILE = LANES // HEAD_DIM
SLABS = GROUP // HEADS_PER_TILE
ATT_ROWS = 64
ATT_SKEW = 2
ATT_SLOTS = 2 * ATT_SKEW + 2
HALF_Q = BLOCK // 2
ATT_KEYS = WINDOW + HALF_Q


def _head_order(half):
    heads = list(range(GROUP))
    return heads if half == 0 else [h ^ 1 for h in heads]
ATT_TILE = 4 * BLOCK
SUB_ROWS = 256
W_CHUNK = 512
SEQS_PER_STEP = 8
MXU_COLS = 256


def _params(*semantics):
    return pltpu.CompilerParams(dimension_semantics=semantics, vmem_limit_bytes=VMEM_LIMIT)


def _resident(shape):
    zeros = (0,) * len(shape)
    return pl.BlockSpec(shape, lambda *_: zeros, pipeline_mode=pl.Buffered(1))


def _rms_scale(x):
    return lax.rsqrt(jnp.mean(x * x, axis=-1, keepdims=True) + RMS_EPS)


def _silu(x):
    h = 0.5 * x
    return h * jnp.tanh(h) + h


def _segment_major(rows, inverse=False):
    seg = rows // SUBLANES
    r = lax.broadcasted_iota(jnp.int32, (rows, rows), 0)
    c = lax.broadcasted_iota(jnp.int32, (rows, rows), 1)
    if inverse:
        src = (r % seg) * SUBLANES + r // seg
    else:
        src = (r % SUBLANES) * seg + r // SUBLANES
    return jnp.where(c == src, 1.0, 0.0).astype(BF16)


def _phase_specs(nchunk, tm, k):
    chunk_w = pl.BlockSpec((k, W_CHUNK), lambda i: (0, jnp.minimum(i, nchunk - 1)))
    chunk_o = lambda rows: pl.BlockSpec((rows, W_CHUNK), lambda i: (0, jnp.minimum(i, nchunk - 1)))
    tile = lambda n: pl.BlockSpec((tm, n), lambda i: (jnp.maximum(i - nchunk, 0), 0))
    return chunk_w, chunk_o, tile


def _dup_heads(x):
    low = lax.broadcasted_iota(jnp.int32, (x.shape[0], LANES), 1) < HEAD_DIM
    out = []
    for c in range(x.shape[1] // LANES):
        col = x[:, c * LANES:(c + 1) * LANES]
        swapped = pltpu.roll(col, HEAD_DIM, axis=1)
        out += [jnp.where(low, col, swapped), jnp.where(low, swapped, col)]
    return jnp.concatenate(out, axis=1)


def _norm_proj_kvq_kernel(nkv, nqg, xs_ref, x_ref, gkv_ref, gq_ref, wkv_ref, wqg_ref,
                          ks_ref, vs_ref, qs_ref, gates_ref,
                          q_ref, gate_ref, kdup_ref, vt_ref, ktail_ref, vtail_ref,
                          wkv_scr, wqg_scr):
    i = pl.program_id(0)
    nchunk = nkv + nqg
    q_chunks = ATT_WIDTH // W_CHUNK
    q_scale = 1.0 / math.sqrt(HEAD_DIM)
    q_scale_log2 = q_scale * LOG2_E

    def sample_rows(g_ref):
        xs = xs_ref[...]
        return (xs * _rms_scale(xs) * g_ref[...]).astype(BF16)

    @pl.when(i < nkv)
    def _():
        wb = wkv_ref[...].astype(BF16)
        wkv_scr[i] = wb
        r = jnp.dot(sample_rows(gkv_ref), wb, preferred_element_type=F32)

        @pl.when(i == 0)
        def _():
            ks_ref[...] = r

        @pl.when(i == 1)
        def _():
            vs_ref[...] = r

    @pl.when((i >= nkv) & (i < nchunk))
    def _():
        c = i - nkv
        wb = wqg_ref[...].astype(BF16)
        wqg_scr[c] = wb
        r = jnp.dot(sample_rows(gq_ref), wb, preferred_element_type=F32)

        @pl.when(c < q_chunks)
        def _():
            qs_ref[...] = r * q_scale

        @pl.when(c >= q_chunks)
        def _():
            gates_ref[...] = r

    @pl.when(i >= nchunk)
    def _():
        tm = x_ref.shape[0]
        for rs in _row_blocks(tm):
            x = x_ref[rs, :]
            xh = x * _rms_scale(x)
            xkv = (xh * gkv_ref[...]).astype(BF16)
            xq = (xh * gq_ref[...]).astype(BF16)
            k = jnp.dot(xkv, wkv_scr[0], preferred_element_type=F32)
            v = jnp.dot(xkv, wkv_scr[1], preferred_element_type=F32)
            kdup_ref[rs, :] = _dup_heads(k).astype(BF16)
            vt_ref[:, rs] = _dup_heads(v).T.astype(BF16)
            for c in range(nqg):
                r = jnp.dot(xq, wqg_scr[c], preferred_element_type=F32)
                if c < q_chunks:
                    q_ref[rs, c * W_CHUNK:(c + 1) * W_CHUNK] = (r * q_scale_log2).astype(q_ref.dtype)
                else:
                    cc = c - q_chunks
                    gate_ref[rs, cc * W_CHUNK:(cc + 1) * W_CHUNK] = r
        ktail_ref[0] = k[k.shape[0] - WINDOW:].T
        vtail_ref[0] = v[v.shape[0] - WINDOW:].T


def norm_proj_kvq(xs, x, g_kv, g_q, w_kv, w_qg, tm, seq_len):
    m, d = x.shape
    ns = xs.shape[0]
    assert w_kv.shape[1] == 2 * KV_WIDTH == 2 * W_CHUNK and seq_len % tm == 0 and tm >= WINDOW
    nkv, nqg = w_kv.shape[1] // W_CHUNK, w_qg.shape[1] // W_CHUNK
    nchunk = nkv + nqg
    tiles = seq_len // tm
    tile = lambda n: pl.BlockSpec((tm, n), lambda i: (jnp.maximum(i - nchunk, 0), 0))
    tail = pl.BlockSpec((1, KV_WIDTH, WINDOW), lambda i: (jnp.maximum(i - nchunk, 0) // tiles, 0, 0))
    kv_chunk = lambda i: (0, jnp.minimum(i, nkv - 1))
    qg_chunk = lambda i: (0, jnp.clip(i - nkv, 0, nqg - 1))
    q_chunks = ATT_WIDTH // W_CHUNK
    whole_s = lambda n: pl.BlockSpec((ns, n), lambda i: (0, 0))
    return pl.pallas_call(
        functools.partial(_norm_proj_kvq_kernel, nkv, nqg),
        grid=(nchunk + m // tm,),
        in_specs=[
            _resident(xs.shape), tile(d), _resident((1, d)), _resident((1, d)),
            pl.BlockSpec((d, W_CHUNK), kv_chunk), pl.BlockSpec((d, W_CHUNK), qg_chunk),
        ],
        out_specs=[
            whole_s(KV_WIDTH), whole_s(KV_WIDTH),
            pl.BlockSpec((ns, W_CHUNK), lambda i: (0, jnp.clip(i - nkv, 0, q_chunks - 1))),
            pl.BlockSpec((ns, W_CHUNK), lambda i: (0, jnp.clip(i - nkv - q_chunks, 0, nqg - q_chunks - 1))),
            tile(ATT_WIDTH), tile(ATT_WIDTH), tile(2 * KV_WIDTH),
            pl.BlockSpec((2 * KV_WIDTH, tm), lambda i: (0, jnp.maximum(i - nchunk, 0))), tail, tail,
        ],
        out_shape=[
            jax.ShapeDtypeStruct((ns, KV_WIDTH), F32),
            jax.ShapeDtypeStruct((ns, KV_WIDTH), F32),
            jax.ShapeDtypeStruct((ns, ATT_WIDTH), F32),
            jax.ShapeDtypeStruct((ns, w_qg.shape[1] - ATT_WIDTH), F32),
            jax.ShapeDtypeStruct((m, ATT_WIDTH), BF16),
            jax.ShapeDtypeStruct((m, ATT_WIDTH), F32),
            jax.ShapeDtypeStruct((m, 2 * KV_WIDTH), BF16),
            jax.ShapeDtypeStruct((2 * KV_WIDTH, m), BF16),
            jax.ShapeDtypeStruct((m // seq_len, KV_WIDTH, WINDOW), F32),
            jax.ShapeDtypeStruct((m // seq_len, KV_WIDTH, WINDOW), F32),
        ],
        scratch_shapes=[pltpu.VMEM((nkv, d, W_CHUNK), BF16), pltpu.VMEM((nqg, d, W_CHUNK), BF16)],
        compiler_params=_params("arbitrary"),
        name="norm_proj_kvq",
    )(xs, x, g_kv.reshape(1, d), g_q.reshape(1, d), w_kv, w_qg)


def _row_blocks(rows):
    sub = min(rows, SUB_ROWS)
    return [slice(r, r + sub) for r in range(0, rows, sub)]


def _proj_norm_res_kernel(nchunk, nparts, as_ref, gs_ref, xs_ref, *refs):
    y_refs = refs[:nparts]
    w_ref, g_ref, x_ref, os_ref, o_ref, w_scr, raw_scr = refs[nparts:]
    i = pl.program_id(0)

    @pl.when(i < nchunk)
    def _():
        wb = w_ref[...].astype(BF16)
        w_scr[i] = wb
        ys = (as_ref[...] * _silu(gs_ref[...])).astype(BF16)
        raw_scr[i] = jnp.dot(ys, wb, preferred_element_type=F32)

    @pl.when(i == nchunk - 1)
    def _():
        o = jnp.concatenate([raw_scr[c] for c in range(nchunk)], axis=1)
        os_ref[...] = xs_ref[...] + o * _rms_scale(o) * g_ref[...]

    @pl.when(i >= nchunk)
    def _():
        tm = x_ref.shape[0]
        part_rows = tm // nparts

        for rs in _row_blocks(tm):
            part, off = divmod(rs.start, part_rows)
            y = y_refs[part][off:off + rs.stop - rs.start, :]
            o = jnp.concatenate([jnp.dot(y, w_scr[c], preferred_element_type=F32)
                                 for c in range(nchunk)], axis=1)
            o_ref[rs, :] = x_ref[rs, :] + o * _rms_scale(o) * g_ref[...]


def proj_norm_res(a_s, gate_s, x_s, y_parts, w, g, x, tm):
    k, d = w.shape
    m = x.shape[0]
    nchunk = d // W_CHUNK
    nparts = len(y_parts)
    assert (tm // nparts) % min(tm, SUB_ROWS) == 0
    chunk_w, _, tile = _phase_specs(nchunk, tm, k)
    part = pl.BlockSpec((tm // nparts, k), lambda i: (jnp.maximum(i - nchunk, 0), 0))
    return pl.pallas_call(
        functools.partial(_proj_norm_res_kernel, nchunk, nparts),
        grid=(nchunk + m // tm,),
        in_specs=[_resident(a_s.shape), _resident(gate_s.shape), _resident(x_s.shape)]
        + [part] * nparts + [chunk_w, _resident((1, d)), tile(d)],
        out_specs=[pl.BlockSpec(x_s.shape, lambda i: (0, 0)), tile(d)],
        out_shape=[jax.ShapeDtypeStruct(x_s.shape, F32), jax.ShapeDtypeStruct((m, d), F32)],
        scratch_shapes=[pltpu.VMEM((nchunk, k, W_CHUNK), BF16),
                        pltpu.VMEM((nchunk, x_s.shape[0], W_CHUNK), F32)],
        compiler_params=_params("arbitrary"),
        name="proj_norm_res",
    )(a_s, gate_s, x_s, *y_parts, w, g.reshape(1, d), x)


def _lru_gate_dots(conv, wr_half, wi_half):
    cb = conv.astype(BF16)
    return (jnp.dot(cb, wr_half, preferred_element_type=F32),
            jnp.dot(cb, wi_half, preferred_element_type=F32))


def _lru_gates(conv, wr_half, br, wi_half, bi, lam):
    return _lru_gate_math(conv, _lru_gate_dots(conv, wr_half, wi_half), br, bi, lam)


def _lru_gate_math(conv, half_pre, br, bi, lam):
    th_r = jnp.tanh(half_pre[0] + 0.5 * br)
    th_i = jnp.tanh(half_pre[1] + 0.5 * bi)
    nl = -lam
    softplus = jnp.maximum(nl, 0.0) + jnp.log1p(jnp.exp(-jnp.abs(nl)))
    half = (0.5 * LRU_C) * softplus
    x = th_r * half + half
    a = jnp.exp2(x * -LOG2_E)
    z = jnp.tanh(x) * (a * a + 1.0)
    mult = z * lax.rsqrt(jnp.maximum(z, 1e-30))
    hc = 0.5 * conv
    return a, mult * (hc * th_i + hc)


def _interleave(*stages):
    live = [[stage, share] for stage, share in stages]
    while live:
        for entry in list(live):
            try:
                for _ in range(entry[1]):
                    next(entry[0])
            except StopIteration:
                live.remove(entry)


def _norm_tile(x_ref, g_ref, xn_ref):
    x = x_ref[...]
    xn = (x * _rms_scale(x) * g_ref[...]).astype(BF16)
    yield
    xn_ref[...] = jnp.dot(_segment_major(xn.shape[0]), xn, preferred_element_type=F32).astype(BF16)
    yield


def _in_proj_tile(xn_ref, w_scr, ug_ref):
    xn = xn_ref[...]
    for c in range(w_scr.shape[0]):
        for n0 in range(0, W_CHUNK, MXU_COLS):
            ug_ref[:, c * W_CHUNK + n0:c * W_CHUNK + n0 + MXU_COLS] = jnp.dot(
                xn, w_scr[c, :, n0:n0 + MXU_COLS], preferred_element_type=F32)
            yield


def _rglru_tile(ug_ref, y_ref, cw_ref, cb_ref, wr_ref, br_ref, wi_ref, bi_ref, lam_ref, h_scr, tail_scr):
    tc = ug_ref.shape[0]
    seg = tc // SUBLANES
    ntaps = CONV_W - 1
    bw = LRU_BLOCK_W
    sub = lax.broadcasted_iota(jnp.int32, (SUBLANES, bw), 0)
    first = sub == 0
    time_order = _segment_major(tc, inverse=True)

    def shift_in(x, row0):
        return jnp.where(first, row0, pltpu.roll(x, 1, axis=0))

    def group(x, j):
        return x[j * SUBLANES:(j + 1) * SUBLANES]

    def store_time_order(cols, y):
        y_ref[:, cols] = jnp.dot(time_order, y, preferred_element_type=F32).astype(y_ref.dtype)

    pending = None
    for n in range(LRU_BLOCKS):
        cs = slice(n * bw, (n + 1) * bw)
        u = ug_ref[:, cs]
        tail = tail_scr[:, cs]
        before = [shift_in(group(u, seg - m), tail[ntaps - m:ntaps - m + 1])
                  for m in range(ntaps, 0, -1)]
        ext = jnp.concatenate(before + [u], axis=0)
        tail_scr[:, cs] = jnp.concatenate(
            [group(u, seg - m)[SUBLANES - 1:] for m in range(ntaps, 0, -1)], axis=0)
        cw = cw_ref[:, cs]
        conv = cb_ref[:, cs]
        for tap in range(CONV_W):
            conv = conv + ext[tap * SUBLANES:tap * SUBLANES + tc] * cw[tap:tap + 1]
        yield

        half_pre = _lru_gate_dots(conv, wr_ref[n], wi_ref[n])
        yield

        if pending is not None:
            store_time_order(*pending)
        yield

        a, b = _lru_gate_math(conv, half_pre, br_ref[:, cs], bi_ref[:, cs], lam_ref[:, cs])

        h = b[:SUBLANES]
        acc = a[:SUBLANES]
        h_loc, a_cum = [h], [acc]
        for j in range(1, seg):
            sl = slice(j * SUBLANES, (j + 1) * SUBLANES)
            h = a[sl] * h + b[sl]
            acc = a[sl] * acc
            h_loc.append(h)
            a_cum.append(acc)

        step = 1
        while step < SUBLANES:
            keep = sub >= step
            h = jnp.where(keep, acc * pltpu.roll(h, step, axis=0) + h, h)
            acc = jnp.where(keep, acc * pltpu.roll(acc, step, axis=0), acc)
            step *= 2
        h_prev = h_scr[:, cs]
        after = h + acc * h_prev
        h_in = shift_in(after, h_prev)
        h_scr[:, cs] = after[SUBLANES - 1:]

        hs = jnp.concatenate([h_loc[j] + a_cum[j] * h_in for j in range(seg)], axis=0)
        y = (hs * _silu(ug_ref[:, LRU_WIDTH + n * bw:LRU_WIDTH + (n + 1) * bw])).astype(BF16)
        pending = (cs, y)
        yield

    store_time_order(*pending)
    yield


def _rglru_front_kernel(nchunk, npairs, chunks, xs_ref, xodd_ref, xnext_ref, g_ref, w_ref, cprev_ref, h0_ref,
                        scprev_ref, sh0_ref, cw_ref, cb_ref, wr_ref, br_ref, wi_ref, bi_ref, lam_ref,
                        gs_ref, hs_ref, scnew_ref, y_even_ref, y_odd_ref, cnew_ref, hlast_ref,
                        w_scr, wr_scr, wi_scr, us_scr, xn0_scr, xn1_scr, ug0_scr, ug1_scr, h_scr, tail_scr):
    i = pl.program_id(0)
    p = i - nchunk
    half = nchunk // 2
    lru = (cw_ref, cb_ref, wr_scr, br_ref, wi_scr, bi_ref, lam_ref, h_scr, tail_scr)

    @pl.when(i < nchunk)
    def _():
        wb = w_ref[...].astype(BF16)
        w_scr[i] = wb
        wr_scr[i] = (0.5 * wr_ref[0]).astype(BF16)
        wi_scr[i] = (0.5 * wi_ref[0]).astype(BF16)
        xs = xs_ref[...]
        xsn = (xs * _rms_scale(xs) * g_ref[...]).astype(BF16)
        r = jnp.dot(xsn, wb, preferred_element_type=F32)
        gs_ref[...] = r

        @pl.when(i < half)
        def _():
            us_scr[i] = r

    @pl.when(i == nchunk - 1)
    def _():
        for _ in _norm_tile(xnext_ref, g_ref, xn0_scr):
            pass
        bw = LRU_BLOCK_W
        for n in range(LRU_BLOCKS):
            cs = slice(n * bw, (n + 1) * bw)
            c, off = divmod(n * bw, W_CHUNK)
            u = us_scr[c, :, off:off + bw]
            cw = cw_ref[:, cs]
            conv = cb_ref[:, cs]
            for tap in range(CONV_W - 1):
                conv = conv + scprev_ref[tap, :, cs] * cw[tap:tap + 1]
                if tap > 0:
                    scnew_ref[tap - 1, :, cs] = scprev_ref[tap, :, cs]
            conv = conv + u * cw[CONV_W - 1:]
            scnew_ref[CONV_W - 2, :, cs] = u
            a, b = _lru_gates(conv, wr_scr[n], br_ref[:, cs], wi_scr[n], bi_ref[:, cs], lam_ref[:, cs])
            hs_ref[:, cs] = a * sh0_ref[:, cs] + b
        ug1_scr[...] = jnp.zeros_like(ug1_scr)
        h_scr[...] = jnp.zeros_like(h_scr)
        tail_scr[...] = jnp.zeros_like(tail_scr)

    @pl.when(p >= 0)
    def _():
        _interleave((_in_proj_tile(xn0_scr, w_scr, ug0_scr), 1),
                    (_rglru_tile(ug1_scr, y_odd_ref, *lru), 2),
                    (_norm_tile(xodd_ref, g_ref, xn1_scr), 1))
        hlast_ref[0] = h_scr[...]
        cnew_ref[0] = tail_scr[...]

    @pl.when((p >= 0) & (p < npairs))
    def _():
        @pl.when((2 * p) % chunks == 0)
        def _():
            h_scr[...] = h0_ref[0]
            tail_scr[...] = cprev_ref[0]

        _interleave((_in_proj_tile(xn1_scr, w_scr, ug1_scr), 1),
                    (_rglru_tile(ug0_scr, y_even_ref, *lru), 2),
                    (_norm_tile(xnext_ref, g_ref, xn0_scr), 1))


def rglru_front(xs, s_conv_prev, s_h0, x, conv_prev, h0, g, w_in, conv_w, conv_b, w_r, b_r, w_i, b_i, lam,
                seq_len):
    m, d = x.shape
    ns = xs.shape[0]
    w = w_in.shape[1] // 2
    tc = SUB_ROWS
    nchunk = w_in.shape[1] // W_CHUNK
    half = nchunk // 2
    bsz = m // seq_len
    chunks = seq_len // tc
    npairs = m // (2 * tc)
    assert seq_len % (2 * tc) == 0 and tc % (SUBLANES * SUBLANES) == 0 and tc // SUBLANES > CONV_W
    pair = lambda i: jnp.clip(i - nchunk, 0, npairs - 1)
    last = npairs * 2 - 1
    seq_in = lambda i: (jnp.clip(2 * (i - nchunk), 0, last) // chunks, 0, 0)
    seq_out = lambda i: (jnp.clip(2 * (i - nchunk) - 1, 0, last) // chunks, 0, 0)
    state_in = lambda rows: pl.BlockSpec((1, rows, w), seq_in)
    state_out = lambda rows: pl.BlockSpec((1, rows, w), seq_out)
    chunk = lambda i: (0, jnp.minimum(i, nchunk - 1))
    tile_ahead = lambda ahead: pl.BlockSpec(
        (tc, d), lambda i: (jnp.clip(2 * (i - nchunk) + ahead, 0, last), 0))
    assert nchunk == LRU_BLOCKS
    gate_w = pl.BlockSpec((1,) + w_r.shape[1:], lambda i: (jnp.minimum(i, nchunk - 1), 0, 0))
    return pl.pallas_call(
        functools.partial(_rglru_front_kernel, nchunk, npairs, chunks),
        grid=(nchunk + npairs + 1,),
        in_specs=[_resident(xs.shape), tile_ahead(1), tile_ahead(2), _resident((1, d)),
                  pl.BlockSpec((d, W_CHUNK), chunk), state_in(CONV_W - 1), state_in(1),
                  _resident(s_conv_prev.shape), _resident(s_h0.shape),
                  _resident((CONV_W, w)), _resident((1, w)), gate_w, _resident((1, w)),
                  gate_w, _resident((1, w)), _resident((1, w))],
        out_specs=[
            pl.BlockSpec((ns, W_CHUNK), lambda i: (0, jnp.clip(i - half, 0, half - 1))),
            pl.BlockSpec((ns, w), lambda i: (0, 0)),
            pl.BlockSpec(s_conv_prev.shape, lambda i: (0, 0, 0)),
            pl.BlockSpec((tc, w), lambda i: (pair(i), 0)),
            pl.BlockSpec((tc, w), lambda i: (jnp.clip(i - nchunk - 1, 0, npairs - 1), 0)),
            state_out(CONV_W - 1), state_out(1),
        ],
        out_shape=[
            jax.ShapeDtypeStruct((ns, w), F32),
            jax.ShapeDtypeStruct((ns, w), F32),
            jax.ShapeDtypeStruct(s_conv_prev.shape, F32),
            jax.ShapeDtypeStruct((m // 2, w), BF16),
            jax.ShapeDtypeStruct((m // 2, w), BF16),
            jax.ShapeDtypeStruct((bsz, CONV_W - 1, w), F32),
            jax.ShapeDtypeStruct((bsz, 1, w), F32),
        ],
        scratch_shapes=[pltpu.VMEM((nchunk, d, W_CHUNK), BF16),
                        pltpu.VMEM(w_r.shape, BF16), pltpu.VMEM(w_i.shape, BF16),
                        pltpu.VMEM((half, ns, W_CHUNK), F32),
                        pltpu.VMEM((tc, d), BF16), pltpu.VMEM((tc, d), BF16),
                        pltpu.VMEM((tc, 2 * w), F32), pltpu.VMEM((tc, 2 * w), F32),
                        pltpu.VMEM((1, w), F32), pltpu.VMEM((CONV_W - 1, w), F32)],
        compiler_params=_params("arbitrary"),
        name="rglru_front",
    )(xs, x, x, g.reshape(1, d), w_in, conv_prev, h0.reshape(bsz, 1, w), s_conv_prev, s_h0,
      conv_w, conv_b.reshape(1, w), w_r, b_r.reshape(1, w), w_i, b_i.reshape(1, w), lam.reshape(1, w))


def _buckets(dist):
    n = jnp.maximum(dist, 0)
    max_exact = N_BUCKETS // 2
    nf = jnp.maximum(n, 1).astype(F32)
    large = max_exact + jnp.floor(jnp.log(nf / max_exact) / math.log(MAX_DISTANCE / max_exact)
                                  * (N_BUCKETS - max_exact)).astype(jnp.int32)
    large = jnp.minimum(large, N_BUCKETS - 1)
    return jnp.where(n < max_exact, n, large)


def _lookup(bucket, valid, table_ref, head):
    bias = jnp.zeros(bucket.shape, F32)
    for b in range(N_BUCKETS):
        bias = jnp.where(bucket == b, table_ref[b, head], bias)
    return jnp.where(valid, bias, NEG_INF)


def _bias_kernel(table_ref, sinks_ref, band_ref, sinkt_ref, past_ref, new_ref):
    hk = pl.program_id(0)
    span = 3 * BLOCK
    dist = (lax.broadcasted_iota(jnp.int32, (1, span), 1) + BLOCK) % span
    bucket = _buckets(dist)
    in_window = (dist >= 0) & (dist < WINDOW)
    key_row = lax.broadcasted_iota(jnp.int32, (ATT_KEYS, HALF_Q), 0)

    def band(head):
        row = _lookup(bucket, in_window, table_ref, head) * LOG2_E
        full = pltpu.roll(jnp.broadcast_to(row, (ATT_KEYS, span)), 0, axis=1, stride=1, stride_axis=0)
        return full[:, :HALF_Q]

    rows = past_ref.shape[2]
    d_past = rows - lax.broadcasted_iota(jnp.int32, (1, rows), 1)
    b_past = _buckets(d_past)
    ok_past = (d_past >= 0) & (d_past < WINDOW)
    d_new = jnp.zeros((1, LANES), jnp.int32)
    b_new = _buckets(d_new)
    for g in range(GROUP):
        head = hk * GROUP + g
        bias = band(head)
        for half in range(2):
            slot = _head_order(half).index(g)
            cs = slice(slot * HALF_Q, (slot + 1) * HALF_Q)
            band_ref[0, half, 0, :, cs] = bias
            prev_rows = BLOCK - half * HALF_Q
            band_ref[1, half, 0, :, cs] = jnp.where(key_row < prev_rows, NEG_INF, bias)
            sinkt_ref[half, 0, :, cs] = jnp.full((1, HALF_Q), sinks_ref[head] * LOG2_E, F32)
        past_ref[0, g:g + 1, :] = _lookup(b_past, ok_past, table_ref, head)
        new_ref[0, g:g + 1, :] = _lookup(b_new, d_new == 0, table_ref, head)


def bias_tables(table, sinks, past_rows):
    smem = pl.BlockSpec(memory_space=pltpu.SMEM)
    return pl.pallas_call(
        _bias_kernel,
        grid=(N_KV_HEADS,),
        in_specs=[smem, smem],
        out_specs=[
            pl.BlockSpec((2, 2, 1, ATT_KEYS, GROUP * HALF_Q), lambda h: (0, 0, h, 0, 0)),
            pl.BlockSpec((2, 1, 1, GROUP * HALF_Q), lambda h: (0, h, 0, 0)),
            pl.BlockSpec((1, GROUP, past_rows), lambda h: (h, 0, 0)),
            pl.BlockSpec((1, GROUP, LANES), lambda h: (h, 0, 0)),
        ],
        out_shape=[
            jax.ShapeDtypeStruct((2, 2, N_KV_HEADS, ATT_KEYS, GROUP * HALF_Q), F32),
            jax.ShapeDtypeStruct((2, N_KV_HEADS, 1, GROUP * HALF_Q), F32),
            jax.ShapeDtypeStruct((N_KV_HEADS, GROUP, past_rows), F32),
            jax.ShapeDtypeStruct((N_KV_HEADS, GROUP, LANES), F32),
        ],
        compiler_params=_params("parallel"),
        name="bias_tables",
    )(table, sinks)


def _band_attn_kernel(q_ref, kp_ref, kc_ref, vp_ref, vc_ref, gate_ref, bias_ref, sink_ref, y_ref,
                      s_scr, p_scr):
    first_tile = (pl.program_id(1) == 0).astype(jnp.int32)
    nt = (((1,), (1,)), ((), ()))
    low = (lax.broadcasted_iota(jnp.int32, (1, LANES), 1) < HEAD_DIM)
    keep_low = low.astype(BF16)
    keep_high = 1 - keep_low
    keep = (keep_low, keep_high)
    nkeys = 2 * BLOCK
    ones_rows = jnp.where(lax.broadcasted_iota(jnp.int32, (2 * SUBLANES, nkeys), 0) == 0,
                          1.0, 0.0).astype(BF16)
    rows = ATT_ROWS
    nslot = s_scr.shape[0]
    items = [(blk, hk, half) for blk in range(q_ref.shape[0] // BLOCK)
             for hk in range(N_KV_HEADS) for half in range(2)]
    assert nslot % 2 == 0

    def rows_of(blk):
        return slice(blk * BLOCK, (blk + 1) * BLOCK)

    def key_rows(half):
        return slice(half * HALF_Q, half * HALF_Q + ATT_KEYS)

    for slot in range(nslot):
        dead = slice(ATT_KEYS, nkeys) if slot % 2 == 0 else slice(0, HALF_Q)
        p_scr[slot, dead, :] = jnp.zeros((HALF_Q, p_scr.shape[2]), BF16)

    def scores(idx):
        blk, hk, half = items[idx]
        variant = first_tile if blk == 0 else 0
        cs = slice(hk * LANES, (hk + 1) * LANES)
        k_prev = kp_ref[:, cs] if blk == 0 else kc_ref[rows_of(blk - 1), cs]
        k_cur = kc_ref[rows_of(blk), cs]
        kd = (jnp.concatenate([k_prev, k_cur[:HALF_Q]], axis=0) if half == 0
              else jnp.concatenate([k_prev[HALF_Q:], k_cur], axis=0))
        q0 = blk * BLOCK + half * HALF_Q
        qs = jnp.concatenate(
            [q_ref[q0:q0 + HALF_Q, (hk * SLABS + h // HEADS_PER_TILE) * LANES:
                   (hk * SLABS + h // HEADS_PER_TILE + 1) * LANES] * keep[h % HEADS_PER_TILE]
             for h in _head_order(half)], axis=0)
        s = lax.dot_general(kd, qs, nt, preferred_element_type=F32) + bias_ref[variant, half, hk]
        s_scr[idx % nslot, key_rows(half), :] = s
        return jnp.maximum(jnp.max(s, axis=0, keepdims=True), sink_ref[half, hk])

    def softmax(idx, m):
        blk, hk, half = items[idx]
        slot = idx % nslot
        lo = half * HALF_Q
        for r in range(lo, lo + ATT_KEYS, rows):
            p_scr[slot, r:r + rows, :] = jnp.exp2(s_scr[slot, r:r + rows, :] - m).astype(BF16)
        return jnp.exp2(sink_ref[half, hk] - m)

    def weighted_values(idx):
        blk, hk, half = items[idx]
        vs = slice(hk * LANES, hk * LANES + HEAD_DIM)
        v_prev = vp_ref[vs, :] if blk == 0 else vc_ref[vs, rows_of(blk - 1)]
        vt = jnp.concatenate([v_prev, vc_ref[vs, rows_of(blk)]], axis=1)
        lhs_v = jnp.concatenate([vt, ones_rows], axis=0)
        return jnp.dot(lhs_v, p_scr[idx % nslot], preferred_element_type=F32)

    low_q = lax.broadcasted_iota(jnp.int32, (HEAD_DIM, LANES), 1) < HALF_Q

    def finish(blk, hk, ots, sink_ws):
        o = [ots[half][:HEAD_DIM] * (1.0 / (ots[half][HEAD_DIM:HEAD_DIM + 1] + sink_ws[half]))
             for half in range(2)]
        for sl in range(SLABS):
            a, b = (oh[:, sl * LANES:(sl + 1) * LANES] for oh in o)
            even = jnp.where(low_q, a, b)
            odd = pltpu.roll(jnp.where(low_q, b, a), HALF_Q, axis=1)
            pair = jnp.concatenate([even, odd], axis=0)
            c0 = (hk * SLABS + sl) * LANES
            y_ref[rows_of(blk), c0:c0 + LANES] = (
                pair.T * _silu(gate_ref[rows_of(blk), c0:c0 + LANES])).astype(y_ref.dtype)

    n = len(items)
    offs, sink_ws, outs = {}, {}, {}
    for k in range(-2 * ATT_SKEW, n + ATT_SKEW):
        if 0 <= k + 2 * ATT_SKEW < n:
            offs[k + 2 * ATT_SKEW] = scores(k + 2 * ATT_SKEW)
        if 0 <= k + ATT_SKEW < n:
            sink_ws[k + ATT_SKEW] = softmax(k + ATT_SKEW, offs.pop(k + ATT_SKEW))
        if 0 <= k < n:
            outs[k] = weighted_values(k)
        j = k - ATT_SKEW
        if 0 <= j < n and items[j][2] == 1:
            finish(items[j][0], items[j][1], [outs.pop(j - 1), outs.pop(j)],
                   [sink_ws.pop(j - 1), sink_ws.pop(j)])


def band_attention(q, kdup, vt, gate, bias_band, sink_t, bsz, t):
    m = q.shape[0]
    nblk = t // BLOCK
    per_tile = ATT_TILE // BLOCK
    ntile = t // ATT_TILE
    assert t % ATT_TILE == 0
    before = lambda b, i: b * nblk + jnp.maximum(per_tile * i - 1, 0)
    cur = lambda n: pl.BlockSpec((ATT_TILE, n), lambda b, i: (b * ntile + i, 0))
    prev = lambda n: pl.BlockSpec((BLOCK, n), lambda b, i: (before(b, i), 0))
    cur_t = pl.BlockSpec((2 * KV_WIDTH, ATT_TILE), lambda b, i: (0, b * ntile + i))
    prev_t = pl.BlockSpec((2 * KV_WIDTH, BLOCK), lambda b, i: (0, before(b, i)))
    score_tile = (2 * BLOCK, GROUP * HALF_Q)
    return pl.pallas_call(
        _band_attn_kernel,
        grid=(bsz, ntile),
        in_specs=[
            cur(ATT_WIDTH), prev(2 * KV_WIDTH), cur(2 * KV_WIDTH), prev_t, cur_t,
            cur(ATT_WIDTH), _resident(bias_band.shape), _resident(sink_t.shape),
        ],
        out_specs=cur(ATT_WIDTH),
        out_shape=jax.ShapeDtypeStruct((m, ATT_WIDTH), BF16),
        scratch_shapes=[pltpu.VMEM((ATT_SLOTS,) + score_tile, F32),
                        pltpu.VMEM((ATT_SLOTS,) + score_tile, BF16)],
        compiler_params=_params("parallel", "parallel"),
        name="band_attention",
    )(q, kdup, kdup, vt, vt, gate, bias_band, sink_t)


def _cached_attn_kernel(q_ref, ckt_ref, cvt_ref, kn_ref, vn_ref, sinks_ref, bpast_ref, bnew_ref, o_ref):
    shape = (N_Q_HEADS, KV_WIDTH)
    lane_kv = lax.broadcasted_iota(jnp.int32, shape, 1) // HEAD_DIM
    row_kv = lax.broadcasted_iota(jnp.int32, shape, 0) // GROUP
    own = lane_kv == row_kv
    sink = sinks_ref[...]
    nt = (((1,), (1,)), ((), ()))
    for b in range(q_ref.shape[0]):
        q = q_ref[b]
        qt = jnp.concatenate([q] * N_KV_HEADS, axis=1)
        qm = jnp.where(own, qt, 0.0).astype(BF16)
        knew = kn_ref[b].astype(BF16).astype(F32)
        vnew = vn_ref[b].astype(BF16).astype(F32)
        s = jnp.dot(qm, ckt_ref[b].astype(BF16), preferred_element_type=F32) + bpast_ref[...]
        s_new = jnp.sum(qm.astype(F32) * knew, axis=-1, keepdims=True) + bnew_ref[:, :1]
        m = jnp.maximum(jnp.maximum(jnp.max(s, axis=-1, keepdims=True), s_new), sink)
        p = jnp.exp(s - m)
        p_new = jnp.exp(s_new - m)
        denom = jnp.sum(p, axis=-1, keepdims=True) + p_new + jnp.exp(sink - m)
        o_all = (lax.dot_general(p.astype(BF16), cvt_ref[b].astype(BF16), nt, preferred_element_type=F32)
                 + p_new.astype(BF16).astype(F32) * vnew)
        o_all = jnp.where(own, o_all, 0.0)
        o = o_all[:, :HEAD_DIM]
        for hk in range(1, N_KV_HEADS):
            o = o + o_all[:, hk * HEAD_DIM:(hk + 1) * HEAD_DIM]
        o_ref[b] = o / denom


def cached_attention(q, cache_kt, cache_vt, k_new, v_new, sinks, bias_past, bias_new):
    bsz, _, rows = cache_kt.shape
    nseq = math.gcd(bsz, SEQS_PER_STEP)
    per_seq = lambda r, n: pl.BlockSpec((nseq, r, n), lambda b: (b, 0, 0))
    return pl.pallas_call(
        _cached_attn_kernel,
        grid=(bsz // nseq,),
        in_specs=[
            per_seq(N_Q_HEADS, HEAD_DIM), per_seq(KV_WIDTH, rows), per_seq(KV_WIDTH, rows),
            per_seq(1, KV_WIDTH), per_seq(1, KV_WIDTH),
            _resident((N_Q_HEADS, 1)), _resident((N_Q_HEADS, rows)), _resident((N_Q_HEADS, LANES)),
        ],
        out_specs=per_seq(N_Q_HEADS, HEAD_DIM),
        out_shape=jax.ShapeDtypeStruct((bsz, N_Q_HEADS, HEAD_DIM), F32),
        compiler_params=_params("parallel"),
        name="cached_attention",
    )(q, cache_kt, cache_vt, k_new, v_new, sinks, bias_past, bias_new)


def kernel(x_prompt, x_sample, state_conv, state_h, cache_k, cache_v, a_norm_pre, a_norm_post,
           a_w_in, a_conv_w, a_conv_b, a_w_r, a_b_r, a_w_i, a_b_i, a_lambda, a_w_out, kv_norm, w_kv,
           b_norm_pre, b_norm_post, b_w_qg, b_sinks, b_w_out, rel_bias_table):
    bsz, t, d = x_prompt.shape
    dbsz, dt, _ = x_sample.shape
    assert a_w_in.shape[0] == 1 and b_w_qg.shape[0] == 1 and dt == 1
    assert t % BLOCK == 0 and t >= WINDOW
    past_rows = cache_k.shape[1]
    assert past_rows == min(WINDOW, PAST_LEN)

    sinks = b_sinks[0]
    bias_band, sink_t, bias_past, bias_new = bias_tables(rel_bias_table, sinks, past_rows)

    tm = 2 * SUB_ROWS
    xp = x_prompt.reshape(bsz * t, d)
    xs = x_sample.reshape(dbsz, d)

    conv0 = jnp.zeros((bsz, CONV_W - 1, LRU_WIDTH), F32)
    h0 = jnp.zeros((bsz, LRU_WIDTH), F32)
    gate_s, hs, s_conv_t, y_even, y_odd, p_conv, p_h = rglru_front(
        xs, jnp.transpose(state_conv[0], (1, 0, 2)), state_h[0], xp, conv0, h0, a_norm_pre[0], a_w_in[0],
        a_conv_w[0], a_conv_b[0], a_w_r[0], a_b_r[0], a_w_i[0], a_b_i[0], a_lambda[0], seq_len=t)
    xs1, x1 = proj_norm_res(hs, gate_s, xs, (y_even, y_odd), a_w_out[0], a_norm_post[0], xp, tm)

    ks, vs, qs, gate_sb, q, gate_b, kdup, vt, k_tail, v_tail = norm_proj_kvq(
        xs1, x1, kv_norm, b_norm_pre[0], w_kv, b_w_qg[0], SUB_ROWS, seq_len=t)
    cache_kt = jnp.transpose(cache_k, (0, 2, 3, 1)).reshape(dbsz, KV_WIDTH, past_rows)
    cache_vt = jnp.transpose(cache_v, (0, 2, 3, 1)).reshape(dbsz, KV_WIDTH, past_rows)
    os_ = cached_attention(qs.reshape(dbsz, N_Q_HEADS, HEAD_DIM), cache_kt, cache_vt,
                           ks.reshape(dbsz, 1, KV_WIDTH), vs.reshape(dbsz, 1, KV_WIDTH),
                           sinks.reshape(N_Q_HEADS, 1), bias_past.reshape(N_Q_HEADS, past_rows),
                           bias_new.reshape(N_Q_HEADS, LANES))
    yb = band_attention(q, kdup, vt, gate_b, bias_band, sink_t, bsz, t)
    y_sample, y_prompt = proj_norm_res(os_.reshape(dbsz, ATT_WIDTH), gate_sb, xs1, (yb,),
                                       b_w_out[0], b_norm_post[0], x1, tm)
    y_prompt = y_prompt.reshape(bsz, t, d)
    p_k = jnp.transpose(k_tail.reshape(bsz, N_KV_HEADS, HEAD_DIM, WINDOW), (0, 3, 1, 2))
    p_v = jnp.transpose(v_tail.reshape(bsz, N_KV_HEADS, HEAD_DIM, WINDOW), (0, 3, 1, 2))

    return (y_prompt, y_sample.reshape(dbsz, 1, d),
            p_conv[None], p_h.reshape(1, bsz, LRU_WIDTH), p_k, p_v,
            jnp.transpose(s_conv_t, (1, 0, 2))[None], hs[None],
            ks.reshape(dbsz, 1, N_KV_HEADS, HEAD_DIM), vs.reshape(dbsz, 1, N_KV_HEADS, HEAD_DIM))
```

```python
import functools
import math

import jax
import jax.numpy as jnp
from jax import lax
from jax.experimental import pallas as pl
from jax.experimental.pallas import tpu as pltpu

F32 = jnp.float32
BF16 = jnp.bfloat16

D_MODEL = 2048
LRU_WIDTH = 2048
LRU_BLOCKS = 8
LRU_BLOCK_W = LRU_WIDTH // LRU_BLOCKS
CONV_W = 4
LRU_C = 8.0
HEAD_DIM = 64
N_Q_HEADS = 32
N_KV_HEADS = 8
GROUP = N_Q_HEADS // N_KV_HEADS
ATT_WIDTH = N_Q_HEADS * HEAD_DIM
KV_WIDTH = N_KV_HEADS * HEAD_DIM
WINDOW = 128
BLOCK = WINDOW
N_BUCKETS = 32
MAX_DISTANCE = 128
RMS_EPS = 1e-6
NEG_INF = -1e30
LOG2_E = 1.4426950408889634
PAST_LEN = 16384

V7X_VMEM_BYTES = 64 * 1024 * 1024
VMEM_LIMIT = V7X_VMEM_BYTES - 8 * 1024 * 1024
SUBLANES = 8
LANES = 128
HEADS_PER_TILE = LANES // HEAD_DIM
SLABS = GROUP // HEADS_PER_TILE
MXU_COLS = 256
SUB_ROWS = 256
W_CHUNK = 512
ATT_TILE = 4 * BLOCK
ATT_ROWS = 64
ATT_SKEW = 2
ATT_SLOTS = 2 * ATT_SKEW + 2
HALF_Q = BLOCK // 2
ATT_KEYS = WINDOW + HALF_Q
SEQS_PER_STEP = 8
SQRT_FLOOR = 1e-30


def _head_order(half):
    heads = list(range(GROUP))
    return heads if half == 0 else [h ^ 1 for h in heads]


def _params(*semantics):
    return pltpu.CompilerParams(dimension_semantics=semantics, vmem_limit_bytes=VMEM_LIMIT)


def _resident(shape):
    zeros = (0,) * len(shape)
    return pl.BlockSpec(shape, lambda *_: zeros, pipeline_mode=pl.Buffered(1))


def _rms_scale(x):
    return lax.rsqrt(jnp.mean(x * x, axis=-1, keepdims=True) + RMS_EPS)


def _silu(x):
    h = 0.5 * x
    return h * jnp.tanh(h) + h


def _segment_major(rows, inverse=False):
    seg = rows // SUBLANES
    r = lax.broadcasted_iota(jnp.int32, (rows, rows), 0)
    c = lax.broadcasted_iota(jnp.int32, (rows, rows), 1)
    if inverse:
        src = (r % seg) * SUBLANES + r // seg
    else:
        src = (r % SUBLANES) * seg + r // SUBLANES
    return jnp.where(c == src, 1.0, 0.0).astype(BF16)


def _phase_specs(nchunk, tm, k):
    chunk_w = pl.BlockSpec((k, W_CHUNK), lambda i: (0, jnp.minimum(i, nchunk - 1)))
    tile = lambda n: pl.BlockSpec((tm, n), lambda i: (jnp.maximum(i - nchunk, 0), 0))
    return chunk_w, tile


def _dup_heads(x):
    low = lax.broadcasted_iota(jnp.int32, (x.shape[0], LANES), 1) < HEAD_DIM
    out = []
    for c in range(x.shape[1] // LANES):
        col = x[:, c * LANES:(c + 1) * LANES]
        swapped = pltpu.roll(col, HEAD_DIM, axis=1)
        out += [jnp.where(low, col, swapped), jnp.where(low, swapped, col)]
    return jnp.concatenate(out, axis=1)


def _norm_proj_kvq_kernel(nkv, nqg, xs_ref, x_ref, gkv_ref, gq_ref, wkv_ref, wqg_ref,
                          ks_ref, vs_ref, qs_ref, gates_ref,
                          q_ref, gate_ref, kdup_ref, vt_ref, ktail_ref, vtail_ref,
                          wkv_scr, wqg_scr):
    i = pl.program_id(0)
    nchunk = nkv + nqg
    q_chunks = ATT_WIDTH // W_CHUNK
    q_scale = 1.0 / math.sqrt(HEAD_DIM)
    q_scale_log2 = q_scale * LOG2_E

    def sample_rows(g_ref):
        xs = xs_ref[...]
        return (xs * _rms_scale(xs) * g_ref[...]).astype(BF16)

    @pl.when(i < nkv)
    def _():
        wb = wkv_ref[...].astype(BF16)
        wkv_scr[i] = wb
        r = jnp.dot(sample_rows(gkv_ref), wb, preferred_element_type=F32)

        @pl.when(i == 0)
        def _():
            ks_ref[...] = r

        @pl.when(i == 1)
        def _():
            vs_ref[...] = r

    @pl.when((i >= nkv) & (i < nchunk))
    def _():
        c = i - nkv
        wb = wqg_ref[...].astype(BF16)
        wqg_scr[c] = wb
        r = jnp.dot(sample_rows(gq_ref), wb, preferred_element_type=F32)

        @pl.when(c < q_chunks)
        def _():
            qs_ref[...] = r * q_scale

        @pl.when(c >= q_chunks)
        def _():
            gates_ref[...] = r

    @pl.when(i >= nchunk)
    def _():
        tm = x_ref.shape[0]
        for rs in _row_blocks(tm):
            x = x_ref[rs, :]
            xh = x * _rms_scale(x)
            xkv = (xh * gkv_ref[...]).astype(BF16)
            xq = (xh * gq_ref[...]).astype(BF16)
            k = jnp.dot(xkv, wkv_scr[0], preferred_element_type=F32)
            v = jnp.dot(xkv, wkv_scr[1], preferred_element_type=F32)
            kdup_ref[rs, :] = _dup_heads(k).astype(BF16)
            vt_ref[:, rs] = _dup_heads(v).T.astype(BF16)
            for c in range(nqg):
                r = jnp.dot(xq, wqg_scr[c], preferred_element_type=F32)
                if c < q_chunks:
                    q_ref[rs, c * W_CHUNK:(c + 1) * W_CHUNK] = (r * q_scale_log2).astype(q_ref.dtype)
                else:
                    cc = c - q_chunks
                    gate_ref[rs, cc * W_CHUNK:(cc + 1) * W_CHUNK] = r
        ktail_ref[0] = k[k.shape[0] - WINDOW:].T
        vtail_ref[0] = v[v.shape[0] - WINDOW:].T


def norm_proj_kvq(xs, x, g_kv, g_q, w_kv, w_qg, tm, seq_len):
    m, d = x.shape
    ns = xs.shape[0]
    assert w_kv.shape[1] == 2 * KV_WIDTH == 2 * W_CHUNK and seq_len % tm == 0 and tm >= WINDOW
    nkv, nqg = w_kv.shape[1] // W_CHUNK, w_qg.shape[1] // W_CHUNK
    nchunk = nkv + nqg
    tiles = seq_len // tm
    tile = lambda n: pl.BlockSpec((tm, n), lambda i: (jnp.maximum(i - nchunk, 0), 0))
    tail = pl.BlockSpec((1, KV_WIDTH, WINDOW), lambda i: (jnp.maximum(i - nchunk, 0) // tiles, 0, 0))
    kv_chunk = lambda i: (0, jnp.minimum(i, nkv - 1))
    qg_chunk = lambda i: (0, jnp.clip(i - nkv, 0, nqg - 1))
    q_chunks = ATT_WIDTH // W_CHUNK
    whole_s = lambda n: pl.BlockSpec((ns, n), lambda i: (0, 0))
    return pl.pallas_call(
        functools.partial(_norm_proj_kvq_kernel, nkv, nqg),
        grid=(nchunk + m // tm,),
        in_specs=[
            _resident(xs.shape), tile(d), _resident((1, d)), _resident((1, d)),
            pl.BlockSpec((d, W_CHUNK), kv_chunk), pl.BlockSpec((d, W_CHUNK), qg_chunk),
        ],
        out_specs=[
            whole_s(KV_WIDTH), whole_s(KV_WIDTH),
            pl.BlockSpec((ns, W_CHUNK), lambda i: (0, jnp.clip(i - nkv, 0, q_chunks - 1))),
            pl.BlockSpec((ns, W_CHUNK), lambda i: (0, jnp.clip(i - nkv - q_chunks, 0, nqg - q_chunks - 1))),
            tile(ATT_WIDTH), tile(ATT_WIDTH), tile(2 * KV_WIDTH),
            pl.BlockSpec((2 * KV_WIDTH, tm), lambda i: (0, jnp.maximum(i - nchunk, 0))), tail, tail,
        ],
        out_shape=[
            jax.ShapeDtypeStruct((ns, KV_WIDTH), F32),
            jax.ShapeDtypeStruct((ns, KV_WIDTH), F32),
            jax.ShapeDtypeStruct((ns, ATT_WIDTH), F32),
            jax.ShapeDtypeStruct((ns, w_qg.shape[1] - ATT_WIDTH), F32),
            jax.ShapeDtypeStruct((m, ATT_WIDTH), BF16),
            jax.ShapeDtypeStruct((m, ATT_WIDTH), F32),
            jax.ShapeDtypeStruct((m, 2 * KV_WIDTH), BF16),
            jax.ShapeDtypeStruct((2 * KV_WIDTH, m), BF16),
            jax.ShapeDtypeStruct((m // seq_len, KV_WIDTH, WINDOW), F32),
            jax.ShapeDtypeStruct((m // seq_len, KV_WIDTH, WINDOW), F32),
        ],
        scratch_shapes=[pltpu.VMEM((nkv, d, W_CHUNK), BF16), pltpu.VMEM((nqg, d, W_CHUNK), BF16)],
        compiler_params=_params("arbitrary"),
        name="norm_proj_kvq",
    )(xs, x, g_kv.reshape(1, d), g_q.reshape(1, d), w_kv, w_qg)


def _row_blocks(rows):
    sub = min(rows, SUB_ROWS)
    return [slice(r, r + sub) for r in range(0, rows, sub)]


def _proj_norm_res_kernel(nchunk, nparts, as_ref, gs_ref, xs_ref, *refs):
    y_refs = refs[:nparts]
    w_ref, g_ref, x_ref, os_ref, o_ref, w_scr, raw_scr = refs[nparts:]
    i = pl.program_id(0)

    @pl.when(i < nchunk)
    def _():
        wb = w_ref[...].astype(BF16)
        w_scr[i] = wb
        ys = (as_ref[...] * _silu(gs_ref[...])).astype(BF16)
        raw_scr[i] = jnp.dot(ys, wb, preferred_element_type=F32)

    @pl.when(i == nchunk - 1)
    def _():
        o = jnp.concatenate([raw_scr[c] for c in range(nchunk)], axis=1)
        os_ref[...] = xs_ref[...] + o * _rms_scale(o) * g_ref[...]

    @pl.when(i >= nchunk)
    def _():
        tm = x_ref.shape[0]
        part_rows = tm // nparts

        for rs in _row_blocks(tm):
            part, off = divmod(rs.start, part_rows)
            y = y_refs[part][off:off + rs.stop - rs.start, :]
            o = jnp.concatenate([jnp.dot(y, w_scr[c], preferred_element_type=F32)
                                 for c in range(nchunk)], axis=1)
            o_ref[rs, :] = x_ref[rs, :] + o * _rms_scale(o) * g_ref[...]


def proj_norm_res(a_s, gate_s, x_s, y_parts, w, g, x, tm):
    k, d = w.shape
    m = x.shape[0]
    nchunk = d // W_CHUNK
    nparts = len(y_parts)
    assert (tm // nparts) % min(tm, SUB_ROWS) == 0
    chunk_w, tile = _phase_specs(nchunk, tm, k)
    part = pl.BlockSpec((tm // nparts, k), lambda i: (jnp.maximum(i - nchunk, 0), 0))
    return pl.pallas_call(
        functools.partial(_proj_norm_res_kernel, nchunk, nparts),
        grid=(nchunk + m // tm,),
        in_specs=[_resident(a_s.shape), _resident(gate_s.shape), _resident(x_s.shape)]
        + [part] * nparts + [chunk_w, _resident((1, d)), tile(d)],
        out_specs=[pl.BlockSpec(x_s.shape, lambda i: (0, 0)), tile(d)],
        out_shape=[jax.ShapeDtypeStruct(x_s.shape, F32), jax.ShapeDtypeStruct((m, d), F32)],
        scratch_shapes=[pltpu.VMEM((nchunk, k, W_CHUNK), BF16),
                        pltpu.VMEM((nchunk, x_s.shape[0], W_CHUNK), F32)],
        compiler_params=_params("arbitrary"),
        name="proj_norm_res",
    )(a_s, gate_s, x_s, *y_parts, w, g.reshape(1, d), x)


def _lru_gate_dots(conv, wr_half, wi_half):
    cb = conv.astype(BF16)
    return (jnp.dot(cb, wr_half, preferred_element_type=F32),
            jnp.dot(cb, wi_half, preferred_element_type=F32))


def _lru_gates(conv, wr_half, br, wi_half, bi, lam):
    return _lru_gate_math(conv, _lru_gate_dots(conv, wr_half, wi_half), br, bi, lam)


def _lru_gate_math(conv, half_pre, br, bi, lam):
    th_r = jnp.tanh(half_pre[0] + 0.5 * br)
    th_i = jnp.tanh(half_pre[1] + 0.5 * bi)
    nl = -lam
    softplus = jnp.maximum(nl, 0.0) + jnp.log1p(jnp.exp(-jnp.abs(nl)))
    half = (0.5 * LRU_C) * softplus
    x = th_r * half + half
    a = jnp.exp2(x * -LOG2_E)
    z = jnp.tanh(x) * (a * a + 1.0)
    mult = z * lax.rsqrt(jnp.maximum(z, SQRT_FLOOR))
    hc = 0.5 * conv
    return a, mult * (hc * th_i + hc)


def _interleave(*stages):
    live = [[stage, share] for stage, share in stages]
    while live:
        for entry in list(live):
            try:
                for _ in range(entry[1]):
                    next(entry[0])
            except StopIteration:
                live.remove(entry)


def _in_proj_tile(x_ref, rs, g_ref, w_scr, ug_ref):
    x = x_ref[rs, :]
    xn = (x * _rms_scale(x) * g_ref[...]).astype(BF16)
    xn = jnp.dot(_segment_major(xn.shape[0]), xn, preferred_element_type=F32).astype(BF16)
    for c in range(w_scr.shape[0]):
        for n0 in range(0, W_CHUNK, MXU_COLS):
            ug_ref[:, c * W_CHUNK + n0:c * W_CHUNK + n0 + MXU_COLS] = jnp.dot(
                xn, w_scr[c, :, n0:n0 + MXU_COLS], preferred_element_type=F32)
            yield


def _rglru_tile(ug_ref, y_ref, cw_ref, cb_ref, wr_ref, br_ref, wi_ref, bi_ref, lam_ref, h_scr, tail_scr):
    tc = ug_ref.shape[0]
    seg = tc // SUBLANES
    ntaps = CONV_W - 1
    bw = LRU_BLOCK_W
    sub = lax.broadcasted_iota(jnp.int32, (SUBLANES, bw), 0)
    first = sub == 0
    time_order = _segment_major(tc, inverse=True)

    def shift_in(x, row0):
        return jnp.where(first, row0, pltpu.roll(x, 1, axis=0))

    def group(x, j):
        return x[j * SUBLANES:(j + 1) * SUBLANES]

    def store_time_order(cols, y):
        y_ref[:, cols] = jnp.dot(time_order, y, preferred_element_type=F32).astype(y_ref.dtype)

    pending = None
    for n in range(LRU_BLOCKS):
        cs = slice(n * bw, (n + 1) * bw)
        u = ug_ref[:, cs]
        tail = tail_scr[:, cs]
        before = [shift_in(group(u, seg - m), tail[ntaps - m:ntaps - m + 1])
                  for m in range(ntaps, 0, -1)]
        ext = jnp.concatenate(before + [u], axis=0)
        tail_scr[:, cs] = jnp.concatenate(
            [group(u, seg - m)[SUBLANES - 1:] for m in range(ntaps, 0, -1)], axis=0)
        cw = cw_ref[:, cs]
        conv = cb_ref[:, cs]
        for tap in range(CONV_W):
            conv = conv + ext[tap * SUBLANES:tap * SUBLANES + tc] * cw[tap:tap + 1]
        yield

        half_pre = _lru_gate_dots(conv, wr_ref[n], wi_ref[n])
        yield

        if pending is not None:
            store_time_order(*pending)
        yield

        a, b = _lru_gate_math(conv, half_pre, br_ref[:, cs], bi_ref[:, cs], lam_ref[:, cs])

        h = b[:SUBLANES]
        acc = a[:SUBLANES]
        h_loc, a_cum = [h], [acc]
        for j in range(1, seg):
            sl = slice(j * SUBLANES, (j + 1) * SUBLANES)
            h = a[sl] * h + b[sl]
            acc = a[sl] * acc
            h_loc.append(h)
            a_cum.append(acc)

        step = 1
        while step < SUBLANES:
            keep = sub >= step
            h = jnp.where(keep, acc * pltpu.roll(h, step, axis=0) + h, h)
            acc = jnp.where(keep, acc * pltpu.roll(acc, step, axis=0), acc)
            step *= 2
        h_prev = h_scr[:, cs]
        after = h + acc * h_prev
        h_in = shift_in(after, h_prev)
        h_scr[:, cs] = after[SUBLANES - 1:]

        hs = jnp.concatenate([h_loc[j] + a_cum[j] * h_in for j in range(seg)], axis=0)
        y = (hs * _silu(ug_ref[:, LRU_WIDTH + n * bw:LRU_WIDTH + (n + 1) * bw])).astype(BF16)
        pending = (cs, y)
        yield

    store_time_order(*pending)
    yield


def _rglru_front_kernel(nchunk, npairs, chunks, xs_ref, x_ref, g_ref, w_ref, cprev_ref, h0_ref,
                        scprev_ref, sh0_ref, cw_ref, cb_ref, wr_ref, br_ref, wi_ref, bi_ref, lam_ref,
                        gs_ref, hs_ref, scnew_ref, y_even_ref, y_odd_ref, cnew_ref, hlast_ref,
                        w_scr, wr_scr, wi_scr, us_scr, ug0_scr, ug1_scr, h_scr, tail_scr):
    i = pl.program_id(0)
    p = i - nchunk
    tc = SUB_ROWS
    half = nchunk // 2
    lru = (cw_ref, cb_ref, wr_scr, br_ref, wi_scr, bi_ref, lam_ref, h_scr, tail_scr)

    @pl.when(i < nchunk)
    def _():
        wb = w_ref[...].astype(BF16)
        w_scr[i] = wb
        xs = xs_ref[...]
        xsn = (xs * _rms_scale(xs) * g_ref[...]).astype(BF16)
        r = jnp.dot(xsn, wb, preferred_element_type=F32)
        gs_ref[...] = r

        @pl.when(i < half)
        def _():
            us_scr[i] = r

    @pl.when(i == nchunk - 1)
    def _():
        wr_scr[...] = (0.5 * wr_ref[...]).astype(BF16)
        wi_scr[...] = (0.5 * wi_ref[...]).astype(BF16)
        bw = LRU_BLOCK_W
        for n in range(LRU_BLOCKS):
            cs = slice(n * bw, (n + 1) * bw)
            c, off = divmod(n * bw, W_CHUNK)
            u = us_scr[c, :, off:off + bw]
            cw = cw_ref[:, cs]
            conv = cb_ref[:, cs]
            for tap in range(CONV_W - 1):
                conv = conv + scprev_ref[tap, :, cs] * cw[tap:tap + 1]
                if tap > 0:
                    scnew_ref[tap - 1, :, cs] = scprev_ref[tap, :, cs]
            conv = conv + u * cw[CONV_W - 1:]
            scnew_ref[CONV_W - 2, :, cs] = u
            a, b = _lru_gates(conv, wr_scr[n], br_ref[:, cs], wi_scr[n], bi_ref[:, cs], lam_ref[:, cs])
            hs_ref[:, cs] = a * sh0_ref[:, cs] + b
        ug1_scr[...] = jnp.zeros_like(ug1_scr)
        h_scr[...] = jnp.zeros_like(h_scr)
        tail_scr[...] = jnp.zeros_like(tail_scr)

    @pl.when(p >= 0)
    def _():
        _interleave((_in_proj_tile(x_ref, slice(0, tc), g_ref, w_scr, ug0_scr), 1),
                    (_rglru_tile(ug1_scr, y_odd_ref, *lru), 2))
        hlast_ref[0] = h_scr[...]
        cnew_ref[0] = tail_scr[...]

    @pl.when((p >= 0) & (p < npairs))
    def _():
        @pl.when((2 * p) % chunks == 0)
        def _():
            h_scr[...] = h0_ref[0]
            tail_scr[...] = cprev_ref[0]

        _interleave((_in_proj_tile(x_ref, slice(tc, 2 * tc), g_ref, w_scr, ug1_scr), 1),
                    (_rglru_tile(ug0_scr, y_even_ref, *lru), 2))


def rglru_front(xs, s_conv_prev, s_h0, x, conv_prev, h0, g, w_in, conv_w, conv_b, w_r, b_r, w_i, b_i, lam,
                seq_len):
    m, d = x.shape
    ns = xs.shape[0]
    w = w_in.shape[1] // 2
    tc = SUB_ROWS
    nchunk = w_in.shape[1] // W_CHUNK
    half = nchunk // 2
    bsz = m // seq_len
    chunks = seq_len // tc
    npairs = m // (2 * tc)
    assert seq_len % (2 * tc) == 0 and tc % (SUBLANES * SUBLANES) == 0 and tc // SUBLANES > CONV_W
    pair = lambda i: jnp.clip(i - nchunk, 0, npairs - 1)
    last = npairs * 2 - 1
    seq_in = lambda i: (jnp.clip(2 * (i - nchunk), 0, last) // chunks, 0, 0)
    seq_out = lambda i: (jnp.clip(2 * (i - nchunk) - 1, 0, last) // chunks, 0, 0)
    state_in = lambda rows: pl.BlockSpec((1, rows, w), seq_in)
    state_out = lambda rows: pl.BlockSpec((1, rows, w), seq_out)
    chunk = lambda i: (0, jnp.minimum(i, nchunk - 1))
    return pl.pallas_call(
        functools.partial(_rglru_front_kernel, nchunk, npairs, chunks),
        grid=(nchunk + npairs + 1,),
        in_specs=[_resident(xs.shape), pl.BlockSpec((2 * tc, d), lambda i: (pair(i), 0)), _resident((1, d)),
                  pl.BlockSpec((d, W_CHUNK), chunk), state_in(CONV_W - 1), state_in(1),
                  _resident(s_conv_prev.shape), _resident(s_h0.shape),
                  _resident((CONV_W, w)), _resident((1, w)), _resident(w_r.shape), _resident((1, w)),
                  _resident(w_i.shape), _resident((1, w)), _resident((1, w))],
        out_specs=[
            pl.BlockSpec((ns, W_CHUNK), lambda i: (0, jnp.clip(i - half, 0, half - 1))),
            pl.BlockSpec((ns, w), lambda i: (0, 0)),
            pl.BlockSpec(s_conv_prev.shape, lambda i: (0, 0, 0)),
            pl.BlockSpec((tc, w), lambda i: (pair(i), 0)),
            pl.BlockSpec((tc, w), lambda i: (jnp.clip(i - nchunk - 1, 0, npairs - 1), 0)),
            state_out(CONV_W - 1), state_out(1),
        ],
        out_shape=[
            jax.ShapeDtypeStruct((ns, w), F32),
            jax.ShapeDtypeStruct((ns, w), F32),
            jax.ShapeDtypeStruct(s_conv_prev.shape, F32),
            jax.ShapeDtypeStruct((m // 2, w), BF16),
            jax.ShapeDtypeStruct((m // 2, w), BF16),
            jax.ShapeDtypeStruct((bsz, CONV_W - 1, w), F32),
            jax.ShapeDtypeStruct((bsz, 1, w), F32),
        ],
        scratch_shapes=[pltpu.VMEM((nchunk, d, W_CHUNK), BF16),
                        pltpu.VMEM(w_r.shape, BF16), pltpu.VMEM(w_i.shape, BF16),
                        pltpu.VMEM((half, ns, W_CHUNK), F32),
                        pltpu.VMEM((tc, 2 * w), F32), pltpu.VMEM((tc, 2 * w), F32),
                        pltpu.VMEM((1, w), F32), pltpu.VMEM((CONV_W - 1, w), F32)],
        compiler_params=_params("arbitrary"),
        name="rglru_front",
    )(xs, x, g.reshape(1, d), w_in, conv_prev, h0.reshape(bsz, 1, w), s_conv_prev, s_h0,
      conv_w, conv_b.reshape(1, w), w_r, b_r.reshape(1, w), w_i, b_i.reshape(1, w), lam.reshape(1, w))


def _buckets(dist):
    n = jnp.maximum(dist, 0)
    max_exact = N_BUCKETS // 2
    nf = jnp.maximum(n, 1).astype(F32)
    large = max_exact + jnp.floor(jnp.log(nf / max_exact) / math.log(MAX_DISTANCE / max_exact)
                                  * (N_BUCKETS - max_exact)).astype(jnp.int32)
    large = jnp.minimum(large, N_BUCKETS - 1)
    return jnp.where(n < max_exact, n, large)


def _lookup(bucket, valid, table_ref, head):
    bias = jnp.zeros(bucket.shape, F32)
    for b in range(N_BUCKETS):
        bias = jnp.where(bucket == b, table_ref[b, head], bias)
    return jnp.where(valid, bias, NEG_INF)


def _bias_kernel(table_ref, sinks_ref, band_ref, sinkt_ref, past_ref, new_ref):
    hk = pl.program_id(0)
    span = 3 * BLOCK
    dist = (lax.broadcasted_iota(jnp.int32, (1, span), 1) + BLOCK) % span
    bucket = _buckets(dist)
    in_window = (dist >= 0) & (dist < WINDOW)
    key_row = lax.broadcasted_iota(jnp.int32, (ATT_KEYS, HALF_Q), 0)

    def band(head):
        row = _lookup(bucket, in_window, table_ref, head) * LOG2_E
        full = pltpu.roll(jnp.broadcast_to(row, (ATT_KEYS, span)), 0, axis=1, stride=1, stride_axis=0)
        return full[:, :HALF_Q]

    rows = past_ref.shape[2]
    d_past = rows - lax.broadcasted_iota(jnp.int32, (1, rows), 1)
    b_past = _buckets(d_past)
    ok_past = (d_past >= 0) & (d_past < WINDOW)
    d_new = jnp.zeros((1, LANES), jnp.int32)
    b_new = _buckets(d_new)
    for g in range(GROUP):
        head = hk * GROUP + g
        bias = band(head)
        for half in range(2):
            slot = _head_order(half).index(g)
            cs = slice(slot * HALF_Q, (slot + 1) * HALF_Q)
            band_ref[0, half, 0, :, cs] = bias
            prev_rows = BLOCK - half * HALF_Q
            band_ref[1, half, 0, :, cs] = jnp.where(key_row < prev_rows, NEG_INF, bias)
            sinkt_ref[half, 0, :, cs] = jnp.full((1, HALF_Q), sinks_ref[head] * LOG2_E, F32)
        past_ref[0, g:g + 1, :] = _lookup(b_past, ok_past, table_ref, head)
        new_ref[0, g:g + 1, :] = _lookup(b_new, d_new == 0, table_ref, head)


def bias_tables(table, sinks, past_rows):
    smem = pl.BlockSpec(memory_space=pltpu.SMEM)
    return pl.pallas_call(
        _bias_kernel,
        grid=(N_KV_HEADS,),
        in_specs=[smem, smem],
        out_specs=[
            pl.BlockSpec((2, 2, 1, ATT_KEYS, GROUP * HALF_Q), lambda h: (0, 0, h, 0, 0)),
            pl.BlockSpec((2, 1, 1, GROUP * HALF_Q), lambda h: (0, h, 0, 0)),
            pl.BlockSpec((1, GROUP, past_rows), lambda h: (h, 0, 0)),
            pl.BlockSpec((1, GROUP, LANES), lambda h: (h, 0, 0)),
        ],
        out_shape=[
            jax.ShapeDtypeStruct((2, 2, N_KV_HEADS, ATT_KEYS, GROUP * HALF_Q), F32),
            jax.ShapeDtypeStruct((2, N_KV_HEADS, 1, GROUP * HALF_Q), F32),
            jax.ShapeDtypeStruct((N_KV_HEADS, GROUP, past_rows), F32),
            jax.ShapeDtypeStruct((N_KV_HEADS, GROUP, LANES), F32),
        ],
        compiler_params=_params("parallel"),
        name="bias_tables",
    )(table, sinks)


def _band_attn_kernel(q_ref, kp_ref, kc_ref, vp_ref, vc_ref, gate_ref, bias_ref, sink_ref, y_ref,
                      s_scr, p_scr):
    first_tile = (pl.program_id(1) == 0).astype(jnp.int32)
    nt = (((1,), (1,)), ((), ()))
    low = (lax.broadcasted_iota(jnp.int32, (1, LANES), 1) < HEAD_DIM)
    keep_low = low.astype(BF16)
    keep_high = 1 - keep_low
    keep = (keep_low, keep_high)
    nkeys = 2 * BLOCK
    ones_rows = jnp.where(lax.broadcasted_iota(jnp.int32, (2 * SUBLANES, nkeys), 0) == 0,
                          1.0, 0.0).astype(BF16)
    rows = ATT_ROWS
    nslot = s_scr.shape[0]
    items = [(blk, hk, half) for blk in range(q_ref.shape[0] // BLOCK)
             for hk in range(N_KV_HEADS) for half in range(2)]
    assert nslot % 2 == 0

    def rows_of(blk):
        return slice(blk * BLOCK, (blk + 1) * BLOCK)

    def key_rows(half):
        return slice(half * HALF_Q, half * HALF_Q + ATT_KEYS)

    for slot in range(nslot):
        dead = slice(ATT_KEYS, nkeys) if slot % 2 == 0 else slice(0, HALF_Q)
        p_scr[slot, dead, :] = jnp.zeros((HALF_Q, p_scr.shape[2]), BF16)

    def scores(idx):
        blk, hk, half = items[idx]
        variant = first_tile if blk == 0 else 0
        cs = slice(hk * LANES, (hk + 1) * LANES)
        k_prev = kp_ref[:, cs] if blk == 0 else kc_ref[rows_of(blk - 1), cs]
        k_cur = kc_ref[rows_of(blk), cs]
        kd = (jnp.concatenate([k_prev, k_cur[:HALF_Q]], axis=0) if half == 0
              else jnp.concatenate([k_prev[HALF_Q:], k_cur], axis=0))
        q0 = blk * BLOCK + half * HALF_Q
        qs = jnp.concatenate(
            [q_ref[q0:q0 + HALF_Q, (hk * SLABS + h // HEADS_PER_TILE) * LANES:
                   (hk * SLABS + h // HEADS_PER_TILE + 1) * LANES] * keep[h % HEADS_PER_TILE]
             for h in _head_order(half)], axis=0)
        s = lax.dot_general(kd, qs, nt, preferred_element_type=F32) + bias_ref[variant, half, hk]
        s_scr[idx % nslot, key_rows(half), :] = s
        return jnp.maximum(jnp.max(s, axis=0, keepdims=True), sink_ref[half, hk])

    def softmax(idx, m):
        blk, hk, half = items[idx]
        slot = idx % nslot
        lo = half * HALF_Q
        for r in range(lo, lo + ATT_KEYS, rows):
            p_scr[slot, r:r + rows, :] = jnp.exp2(s_scr[slot, r:r + rows, :] - m).astype(BF16)
        return jnp.exp2(sink_ref[half, hk] - m)

    def weighted_values(idx):
        blk, hk, half = items[idx]
        vs = slice(hk * LANES, hk * LANES + HEAD_DIM)
        v_prev = vp_ref[vs, :] if blk == 0 else vc_ref[vs, rows_of(blk - 1)]
        vt = jnp.concatenate([v_prev, vc_ref[vs, rows_of(blk)]], axis=1)
        lhs_v = jnp.concatenate([vt, ones_rows], axis=0)
        return jnp.dot(lhs_v, p_scr[idx % nslot], preferred_element_type=F32)

    low_q = lax.broadcasted_iota(jnp.int32, (HEAD_DIM, LANES), 1) < HALF_Q

    def finish(blk, hk, ots, sink_ws):
        o = [ots[half][:HEAD_DIM] * (1.0 / (ots[half][HEAD_DIM:HEAD_DIM + 1] + sink_ws[half]))
             for half in range(2)]
        for sl in range(SLABS):
            a, b = (oh[:, sl * LANES:(sl + 1) * LANES] for oh in o)
            even = jnp.where(low_q, a, b)
            odd = pltpu.roll(jnp.where(low_q, b, a), HALF_Q, axis=1)
            pair = jnp.concatenate([even, odd], axis=0)
            c0 = (hk * SLABS + sl) * LANES
            y_ref[rows_of(blk), c0:c0 + LANES] = (
                pair.T * _silu(gate_ref[rows_of(blk), c0:c0 + LANES])).astype(y_ref.dtype)

    n = len(items)
    offs, sink_ws, outs = {}, {}, {}
    for k in range(-2 * ATT_SKEW, n + ATT_SKEW):
        if 0 <= k + 2 * ATT_SKEW < n:
            offs[k + 2 * ATT_SKEW] = scores(k + 2 * ATT_SKEW)
        if 0 <= k + ATT_SKEW < n:
            sink_ws[k + ATT_SKEW] = softmax(k + ATT_SKEW, offs.pop(k + ATT_SKEW))
        if 0 <= k < n:
            outs[k] = weighted_values(k)
        j = k - ATT_SKEW
        if 0 <= j < n and items[j][2] == 1:
            finish(items[j][0], items[j][1], [outs.pop(j - 1), outs.pop(j)],
                   [sink_ws.pop(j - 1), sink_ws.pop(j)])


def band_attention(q, kdup, vt, gate, bias_band, sink_t, bsz, t):
    m = q.shape[0]
    nblk = t // BLOCK
    per_tile = ATT_TILE // BLOCK
    ntile = t // ATT_TILE
    assert t % ATT_TILE == 0
    before = lambda b, i: b * nblk + jnp.maximum(per_tile * i - 1, 0)
    cur = lambda n: pl.BlockSpec((ATT_TILE, n), lambda b, i: (b * ntile + i, 0))
    prev = lambda n: pl.BlockSpec((BLOCK, n), lambda b, i: (before(b, i), 0))
    cur_t = pl.BlockSpec((2 * KV_WIDTH, ATT_TILE), lambda b, i: (0, b * ntile + i))
    prev_t = pl.BlockSpec((2 * KV_WIDTH, BLOCK), lambda b, i: (0, before(b, i)))
    score_tile = (2 * BLOCK, GROUP * HALF_Q)
    return pl.pallas_call(
        _band_attn_kernel,
        grid=(bsz, ntile),
        in_specs=[
            cur(ATT_WIDTH), prev(2 * KV_WIDTH), cur(2 * KV_WIDTH), prev_t, cur_t,
            cur(ATT_WIDTH), _resident(bias_band.shape), _resident(sink_t.shape),
        ],
        out_specs=cur(ATT_WIDTH),
        out_shape=jax.ShapeDtypeStruct((m, ATT_WIDTH), BF16),
        scratch_shapes=[pltpu.VMEM((ATT_SLOTS,) + score_tile, F32),
                        pltpu.VMEM((ATT_SLOTS,) + score_tile, BF16)],
        compiler_params=_params("parallel", "parallel"),
        name="band_attention",
    )(q, kdup, kdup, vt, vt, gate, bias_band, sink_t)


def _cached_attn_kernel(q_ref, ckt_ref, cvt_ref, kn_ref, vn_ref, sinks_ref, bpast_ref, bnew_ref, o_ref):
    shape = (N_Q_HEADS, KV_WIDTH)
    lane_kv = lax.broadcasted_iota(jnp.int32, shape, 1) // HEAD_DIM
    row_kv = lax.broadcasted_iota(jnp.int32, shape, 0) // GROUP
    own = lane_kv == row_kv
    sink = sinks_ref[...]
    nt = (((1,), (1,)), ((), ()))
    for b in range(q_ref.shape[0]):
        q = q_ref[b]
        qt = jnp.concatenate([q] * N_KV_HEADS, axis=1)
        qm = jnp.where(own, qt, 0.0).astype(BF16)
        knew = kn_ref[b].astype(BF16).astype(F32)
        vnew = vn_ref[b].astype(BF16).astype(F32)
        s = jnp.dot(qm, ckt_ref[b].astype(BF16), preferred_element_type=F32) + bpast_ref[...]
        s_new = jnp.sum(qm.astype(F32) * knew, axis=-1, keepdims=True) + bnew_ref[:, :1]
        m = jnp.maximum(jnp.maximum(jnp.max(s, axis=-1, keepdims=True), s_new), sink)
        p = jnp.exp(s - m)
        p_new = jnp.exp(s_new - m)
        denom = jnp.sum(p, axis=-1, keepdims=True) + p_new + jnp.exp(sink - m)
        o_all = (lax.dot_general(p.astype(BF16), cvt_ref[b].astype(BF16), nt, preferred_element_type=F32)
                 + p_new.astype(BF16).astype(F32) * vnew)
        o_all = jnp.where(own, o_all, 0.0)
        o = o_all[:, :HEAD_DIM]
        for hk in range(1, N_KV_HEADS):
            o = o + o_all[:, hk * HEAD_DIM:(hk + 1) * HEAD_DIM]
        o_ref[b] = o / denom


def cached_attention(q, cache_kt, cache_vt, k_new, v_new, sinks, bias_past, bias_new):
    bsz, _, rows = cache_kt.shape
    nseq = math.gcd(bsz, SEQS_PER_STEP)
    per_seq = lambda r, n: pl.BlockSpec((nseq, r, n), lambda b: (b, 0, 0))
    return pl.pallas_call(
        _cached_attn_kernel,
        grid=(bsz // nseq,),
        in_specs=[
            per_seq(N_Q_HEADS, HEAD_DIM), per_seq(KV_WIDTH, rows), per_seq(KV_WIDTH, rows),
            per_seq(1, KV_WIDTH), per_seq(1, KV_WIDTH),
            _resident((N_Q_HEADS, 1)), _resident((N_Q_HEADS, rows)), _resident((N_Q_HEADS, LANES)),
        ],
        out_specs=per_seq(N_Q_HEADS, HEAD_DIM),
        out_shape=jax.ShapeDtypeStruct((bsz, N_Q_HEADS, HEAD_DIM), F32),
        compiler_params=_params("parallel"),
        name="cached_attention",
    )(q, cache_kt, cache_vt, k_new, v_new, sinks, bias_past, bias_new)


def kernel(x_prompt, x_sample, state_conv, state_h, cache_k, cache_v, a_norm_pre, a_norm_post,
           a_w_in, a_conv_w, a_conv_b, a_w_r, a_b_r, a_w_i, a_b_i, a_lambda, a_w_out, kv_norm, w_kv,
           b_norm_pre, b_norm_post, b_w_qg, b_sinks, b_w_out, rel_bias_table):
    bsz, t, d = x_prompt.shape
    dbsz, dt, _ = x_sample.shape
    assert a_w_in.shape[0] == 1 and b_w_qg.shape[0] == 1 and dt == 1
    assert t % BLOCK == 0 and t >= WINDOW
    past_rows = cache_k.shape[1]
    assert past_rows == min(WINDOW, PAST_LEN)

    sinks = b_sinks[0]
    bias_band, sink_t, bias_past, bias_new = bias_tables(rel_bias_table, sinks, past_rows)

    tm = 2 * SUB_ROWS
    xp = x_prompt.reshape(bsz * t, d)
    xs = x_sample.reshape(dbsz, d)

    conv0 = jnp.zeros((bsz, CONV_W - 1, LRU_WIDTH), F32)
    h0 = jnp.zeros((bsz, LRU_WIDTH), F32)
    gate_s, hs, s_conv_t, y_even, y_odd, p_conv, p_h = rglru_front(
        xs, jnp.transpose(state_conv[0], (1, 0, 2)), state_h[0], xp, conv0, h0, a_norm_pre[0], a_w_in[0],
        a_conv_w[0], a_conv_b[0], a_w_r[0], a_b_r[0], a_w_i[0], a_b_i[0], a_lambda[0], seq_len=t)
    xs1, x1 = proj_norm_res(hs, gate_s, xs, (y_even, y_odd), a_w_out[0], a_norm_post[0], xp, tm)

    ks, vs, qs, gate_sb, q, gate_b, kdup, vt, k_tail, v_tail = norm_proj_kvq(
        xs1, x1, kv_norm, b_norm_pre[0], w_kv, b_w_qg[0], SUB_ROWS, seq_len=t)
    cache_kt = jnp.transpose(cache_k, (0, 2, 3, 1)).reshape(dbsz, KV_WIDTH, past_rows)
    cache_vt = jnp.transpose(cache_v, (0, 2, 3, 1)).reshape(dbsz, KV_WIDTH, past_rows)
    os_ = cached_attention(qs.reshape(dbsz, N_Q_HEADS, HEAD_DIM), cache_kt, cache_vt,
                           ks.reshape(dbsz, 1, KV_WIDTH), vs.reshape(dbsz, 1, KV_WIDTH),
                           sinks.reshape(N_Q_HEADS, 1), bias_past.reshape(N_Q_HEADS, past_rows),
                           bias_new.reshape(N_Q_HEADS, LANES))
    yb = band_attention(q, kdup, vt, gate_b, bias_band, sink_t, bsz, t)
    y_sample, y_prompt = proj_norm_res(os_.reshape(dbsz, ATT_WIDTH), gate_sb, xs1, (yb,),
                                       b_w_out[0], b_norm_post[0], x1, tm)
    y_prompt = y_prompt.reshape(bsz, t, d)
    p_k = jnp.transpose(k_tail.reshape(bsz, N_KV_HEADS, HEAD_DIM, WINDOW), (0, 3, 1, 2))
    p_v = jnp.transpose(v_tail.reshape(bsz, N_KV_HEADS, HEAD_DIM, WINDOW), (0, 3, 1, 2))

    return (y_prompt, y_sample.reshape(dbsz, 1, d),
            p_conv[None], p_h.reshape(1, bsz, LRU_WIDTH), p_k, p_v,
            jnp.transpose(s_conv_t, (1, 0, 2))[None], hs[None],
            ks.reshape(dbsz, 1, N_KV_HEADS, HEAD_DIM), vs.reshape(dbsz, 1, N_KV_HEADS, HEAD_DIM))
```

```python
import functools
import math

import jax
import jax.numpy as jnp
from jax import lax
from jax.experimental import pallas as pl
from jax.experimental.pallas import tpu as pltpu

F32 = jnp.float32
BF16 = jnp.bfloat16

D_MODEL = 2048
LRU_WIDTH = 2048
LRU_BLOCKS = 8
LRU_BLOCK_W = LRU_WIDTH // LRU_BLOCKS
CONV_W = 4
LRU_C = 8.0
HEAD_DIM = 64
N_Q_HEADS = 32
N_KV_HEADS = 8
GROUP = N_Q_HEADS // N_KV_HEADS
ATT_WIDTH = N_Q_HEADS * HEAD_DIM
KV_WIDTH = N_KV_HEADS * HEAD_DIM
WINDOW = 128
BLOCK = WINDOW
N_BUCKETS = 32
MAX_DISTANCE = 128
RMS_EPS = 1e-6
NEG_INF = -1e30
LOG2_E = 1.4426950408889634
PAST_LEN = 16384

V7X_VMEM_BYTES = 64 * 1024 * 1024
VMEM_LIMIT = V7X_VMEM_BYTES - 8 * 1024 * 1024
SUBLANES = 8
LANES = 128
HEADS_PER_TILE = LANES // HEAD_DIM
SLABS = GROUP // HEADS_PER_TILE
MXU_COLS = 256
SUB_ROWS = 256
W_CHUNK = 512
ATT_TILE = 4 * BLOCK
ATT_ROWS = 64
ATT_SKEW = 2
ATT_SLOTS = 2 * ATT_SKEW + 2
HALF_Q = BLOCK // 2
ATT_KEYS = WINDOW + HALF_Q
SEQS_PER_STEP = 8
SQRT_FLOOR = 1e-30


def _head_order(half):
    heads = list(range(GROUP))
    return heads if half == 0 else [h ^ 1 for h in heads]


def _params(*semantics):
    return pltpu.CompilerParams(dimension_semantics=semantics, vmem_limit_bytes=VMEM_LIMIT)


def _resident(shape):
    zeros = (0,) * len(shape)
    return pl.BlockSpec(shape, lambda *_: zeros, pipeline_mode=pl.Buffered(1))


def _rms_scale(x):
    return lax.rsqrt(jnp.mean(x * x, axis=-1, keepdims=True) + RMS_EPS)


def _silu(x):
    h = 0.5 * x
    return h * jnp.tanh(h) + h


def _segment_major(rows, inverse=False):
    seg = rows // SUBLANES
    r = lax.broadcasted_iota(jnp.int32, (rows, rows), 0)
    c = lax.broadcasted_iota(jnp.int32, (rows, rows), 1)
    if inverse:
        src = (r % seg) * SUBLANES + r // seg
    else:
        src = (r % SUBLANES) * seg + r // SUBLANES
    return jnp.where(c == src, 1.0, 0.0).astype(BF16)


def _phase_specs(nchunk, tm, k):
    chunk_w = pl.BlockSpec((k, W_CHUNK), lambda i: (0, jnp.minimum(i, nchunk - 1)))
    tile = lambda n: pl.BlockSpec((tm, n), lambda i: (jnp.maximum(i - nchunk, 0), 0))
    return chunk_w, tile


def _dup_heads(x):
    low = lax.broadcasted_iota(jnp.int32, (x.shape[0], LANES), 1) < HEAD_DIM
    out = []
    for c in range(x.shape[1] // LANES):
        col = x[:, c * LANES:(c + 1) * LANES]
        swapped = pltpu.roll(col, HEAD_DIM, axis=1)
        out += [jnp.where(low, col, swapped), jnp.where(low, swapped, col)]
    return jnp.concatenate(out, axis=1)


def _norm_proj_kvq_kernel(nkv, nqg, xs_ref, x_ref, gkv_ref, gq_ref, wkv_ref, wqg_ref,
                          ks_ref, vs_ref, qs_ref, gates_ref,
                          q_ref, gate_ref, kdup_ref, vt_ref, ktail_ref, vtail_ref,
                          wkv_scr, wqg_scr):
    i = pl.program_id(0)
    nchunk = nkv + nqg
    q_chunks = ATT_WIDTH // W_CHUNK
    q_scale = 1.0 / math.sqrt(HEAD_DIM)
    q_scale_log2 = q_scale * LOG2_E

    def sample_rows(g_ref):
        xs = xs_ref[...]
        return (xs * _rms_scale(xs) * g_ref[...]).astype(BF16)

    @pl.when(i < nkv)
    def _():
        wb = wkv_ref[...].astype(BF16)
        wkv_scr[i] = wb
        r = jnp.dot(sample_rows(gkv_ref), wb, preferred_element_type=F32)

        @pl.when(i == 0)
        def _():
            ks_ref[...] = r

        @pl.when(i == 1)
        def _():
            vs_ref[...] = r

    @pl.when((i >= nkv) & (i < nchunk))
    def _():
        c = i - nkv
        wb = wqg_ref[...].astype(BF16)
        wqg_scr[c] = wb
        r = jnp.dot(sample_rows(gq_ref), wb, preferred_element_type=F32)

        @pl.when(c < q_chunks)
        def _():
            qs_ref[...] = r * q_scale

        @pl.when(c >= q_chunks)
        def _():
            gates_ref[...] = r

    @pl.when(i >= nchunk)
    def _():
        tm = x_ref.shape[0]
        for rs in _row_blocks(tm):
            x = x_ref[rs, :]
            xh = x * _rms_scale(x)
            xkv = (xh * gkv_ref[...]).astype(BF16)
            xq = (xh * gq_ref[...]).astype(BF16)
            k = jnp.dot(xkv, wkv_scr[0], preferred_element_type=F32)
            v = jnp.dot(xkv, wkv_scr[1], preferred_element_type=F32)
            kdup_ref[rs, :] = _dup_heads(k).astype(BF16)
            vt_ref[:, rs] = _dup_heads(v).T.astype(BF16)
            for c in range(nqg):
                r = jnp.dot(xq, wqg_scr[c], preferred_element_type=F32)
                if c < q_chunks:
                    q_ref[rs, c * W_CHUNK:(c + 1) * W_CHUNK] = (r * q_scale_log2).astype(q_ref.dtype)
                else:
                    cc = c - q_chunks
                    gate_ref[rs, cc * W_CHUNK:(cc + 1) * W_CHUNK] = r
        ktail_ref[0] = k[k.shape[0] - WINDOW:].T
        vtail_ref[0] = v[v.shape[0] - WINDOW:].T


def norm_proj_kvq(xs, x, g_kv, g_q, w_kv, w_qg, tm, seq_len):
    m, d = x.shape
    ns = xs.shape[0]
    assert w_kv.shape[1] == 2 * KV_WIDTH == 2 * W_CHUNK and seq_len % tm == 0 and tm >= WINDOW
    nkv, nqg = w_kv.shape[1] // W_CHUNK, w_qg.shape[1] // W_CHUNK
    nchunk = nkv + nqg
    tiles = seq_len // tm
    tile = lambda n: pl.BlockSpec((tm, n), lambda i: (jnp.maximum(i - nchunk, 0), 0))
    tail = pl.BlockSpec((1, KV_WIDTH, WINDOW), lambda i: (jnp.maximum(i - nchunk, 0) // tiles, 0, 0))
    kv_chunk = lambda i: (0, jnp.minimum(i, nkv - 1))
    qg_chunk = lambda i: (0, jnp.clip(i - nkv, 0, nqg - 1))
    q_chunks = ATT_WIDTH // W_CHUNK
    whole_s = lambda n: pl.BlockSpec((ns, n), lambda i: (0, 0))
    return pl.pallas_call(
        functools.partial(_norm_proj_kvq_kernel, nkv, nqg),
        grid=(nchunk + m // tm,),
        in_specs=[
            _resident(xs.shape), tile(d), _resident((1, d)), _resident((1, d)),
            pl.BlockSpec((d, W_CHUNK), kv_chunk), pl.BlockSpec((d, W_CHUNK), qg_chunk),
        ],
        out_specs=[
            whole_s(KV_WIDTH), whole_s(KV_WIDTH),
            pl.BlockSpec((ns, W_CHUNK), lambda i: (0, jnp.clip(i - nkv, 0, q_chunks - 1))),
            pl.BlockSpec((ns, W_CHUNK), lambda i: (0, jnp.clip(i - nkv - q_chunks, 0, nqg - q_chunks - 1))),
            tile(ATT_WIDTH), tile(ATT_WIDTH), tile(2 * KV_WIDTH),
            pl.BlockSpec((2 * KV_WIDTH, tm), lambda i: (0, jnp.maximum(i - nchunk, 0))), tail, tail,
        ],
        out_shape=[
            jax.ShapeDtypeStruct((ns, KV_WIDTH), F32),
            jax.ShapeDtypeStruct((ns, KV_WIDTH), F32),
            jax.ShapeDtypeStruct((ns, ATT_WIDTH), F32),
            jax.ShapeDtypeStruct((ns, w_qg.shape[1] - ATT_WIDTH), F32),
            jax.ShapeDtypeStruct((m, ATT_WIDTH), BF16),
            jax.ShapeDtypeStruct((m, ATT_WIDTH), F32),
            jax.ShapeDtypeStruct((m, 2 * KV_WIDTH), BF16),
            jax.ShapeDtypeStruct((2 * KV_WIDTH, m), BF16),
            jax.ShapeDtypeStruct((m // seq_len, KV_WIDTH, WINDOW), F32),
            jax.ShapeDtypeStruct((m // seq_len, KV_WIDTH, WINDOW), F32),
        ],
        scratch_shapes=[pltpu.VMEM((nkv, d, W_CHUNK), BF16), pltpu.VMEM((nqg, d, W_CHUNK), BF16)],
        compiler_params=_params("arbitrary"),
        name="norm_proj_kvq",
    )(xs, x, g_kv.reshape(1, d), g_q.reshape(1, d), w_kv, w_qg)


def _row_blocks(rows):
    sub = min(rows, SUB_ROWS)
    return [slice(r, r + sub) for r in range(0, rows, sub)]


def _proj_norm_res_kernel(nchunk, nparts, as_ref, gs_ref, xs_ref, *refs):
    y_refs = refs[:nparts]
    w_ref, g_ref, x_ref, os_ref, o_ref, w_scr, raw_scr, raw0_scr = refs[nparts:]
    i = pl.program_id(0)
    tm = x_ref.shape[0]
    part_rows = tm // nparts

    def y_rows(rs):
        part, off = divmod(rs.start, part_rows)
        return y_refs[part][off:off + rs.stop - rs.start, :]

    def finish(rs, o):
        o_ref[rs, :] = x_ref[rs, :] + o * _rms_scale(o) * g_ref[...]

    @pl.when(i < nchunk)
    def _():
        wb = w_ref[...].astype(BF16)
        w_scr[i] = wb
        ys = (as_ref[...] * _silu(gs_ref[...])).astype(BF16)
        raw_scr[i] = jnp.dot(ys, wb, preferred_element_type=F32)
        for rs in _row_blocks(tm):
            raw0_scr[i, rs, :] = jnp.dot(y_rows(rs), wb, preferred_element_type=F32)

    @pl.when(i == nchunk - 1)
    def _():
        o = jnp.concatenate([raw_scr[c] for c in range(nchunk)], axis=1)
        os_ref[...] = xs_ref[...] + o * _rms_scale(o) * g_ref[...]
        for rs in _row_blocks(tm):
            finish(rs, jnp.concatenate([raw0_scr[c, rs, :] for c in range(nchunk)], axis=1))

    @pl.when(i >= nchunk)
    def _():
        for rs in _row_blocks(tm):
            y = y_rows(rs)
            finish(rs, jnp.concatenate([jnp.dot(y, w_scr[c], preferred_element_type=F32)
                                        for c in range(nchunk)], axis=1))


def proj_norm_res(a_s, gate_s, x_s, y_parts, w, g, x, tm):
    k, d = w.shape
    m = x.shape[0]
    nchunk = d // W_CHUNK
    nparts = len(y_parts)
    assert (tm // nparts) % min(tm, SUB_ROWS) == 0
    chunk_w, _ = _phase_specs(nchunk, tm, k)
    tile_of = lambda i: jnp.maximum(i - nchunk + 1, 0)
    tile = lambda n: pl.BlockSpec((tm, n), lambda i: (tile_of(i), 0))
    part = pl.BlockSpec((tm // nparts, k), lambda i: (tile_of(i), 0))
    return pl.pallas_call(
        functools.partial(_proj_norm_res_kernel, nchunk, nparts),
        grid=(nchunk + m // tm - 1,),
        in_specs=[_resident(a_s.shape), _resident(gate_s.shape), _resident(x_s.shape)]
        + [part] * nparts + [chunk_w, _resident((1, d)), tile(d)],
        out_specs=[pl.BlockSpec(x_s.shape, lambda i: (0, 0)), tile(d)],
        out_shape=[jax.ShapeDtypeStruct(x_s.shape, F32), jax.ShapeDtypeStruct((m, d), F32)],
        scratch_shapes=[pltpu.VMEM((nchunk, k, W_CHUNK), BF16),
                        pltpu.VMEM((nchunk, x_s.shape[0], W_CHUNK), F32),
                        pltpu.VMEM((nchunk, tm, W_CHUNK), F32)],
        compiler_params=_params("arbitrary"),
        name="proj_norm_res",
    )(a_s, gate_s, x_s, *y_parts, w, g.reshape(1, d), x)


def _lru_gate_dots(conv, wr_half, wi_half):
    cb = conv.astype(BF16)
    return (jnp.dot(cb, wr_half, preferred_element_type=F32),
            jnp.dot(cb, wi_half, preferred_element_type=F32))


def _lru_gates(conv, wr_half, br, wi_half, bi, lam):
    return _lru_gate_math(conv, _lru_gate_dots(conv, wr_half, wi_half), br, bi, lam)


def _lru_gate_math(conv, half_pre, br, bi, lam):
    th_r = jnp.tanh(half_pre[0] + 0.5 * br)
    th_i = jnp.tanh(half_pre[1] + 0.5 * bi)
    nl = -lam
    softplus = jnp.maximum(nl, 0.0) + jnp.log1p(jnp.exp(-jnp.abs(nl)))
    half = (0.5 * LRU_C) * softplus
    x = th_r * half + half
    a = jnp.exp2(x * -LOG2_E)
    z = jnp.tanh(x) * (a * a + 1.0)
    mult = z * lax.rsqrt(jnp.maximum(z, SQRT_FLOOR))
    hc = 0.5 * conv
    return a, mult * (hc * th_i + hc)


def _interleave(*stages):
    live = [[stage, share] for stage, share in stages]
    while live:
        for entry in list(live):
            try:
                for _ in range(entry[1]):
                    next(entry[0])
            except StopIteration:
                live.remove(entry)


def _in_proj_tile(x_ref, rs, g_ref, w_scr, ug_ref):
    x = x_ref[rs, :]
    xn = (x * _rms_scale(x) * g_ref[...]).astype(BF16)
    xn = jnp.dot(_segment_major(xn.shape[0]), xn, preferred_element_type=F32).astype(BF16)
    for c in range(w_scr.shape[0]):
        for n0 in range(0, W_CHUNK, MXU_COLS):
            ug_ref[:, c * W_CHUNK + n0:c * W_CHUNK + n0 + MXU_COLS] = jnp.dot(
                xn, w_scr[c, :, n0:n0 + MXU_COLS], preferred_element_type=F32)
            yield


def _rglru_tile(ug_ref, y_ref, cw_ref, cb_ref, wr_ref, br_ref, wi_ref, bi_ref, lam_ref, h_scr, tail_scr):
    tc = ug_ref.shape[0]
    seg = tc // SUBLANES
    ntaps = CONV_W - 1
    bw = LRU_BLOCK_W
    sub = lax.broadcasted_iota(jnp.int32, (SUBLANES, bw), 0)
    first = sub == 0
    time_order = _segment_major(tc, inverse=True)

    def shift_in(x, row0):
        return jnp.where(first, row0, pltpu.roll(x, 1, axis=0))

    def group(x, j):
        return x[j * SUBLANES:(j + 1) * SUBLANES]

    def store_time_order(cols, y):
        y_ref[:, cols] = jnp.dot(time_order, y, preferred_element_type=F32).astype(y_ref.dtype)

    pending = None
    for n in range(LRU_BLOCKS):
        cs = slice(n * bw, (n + 1) * bw)
        u = ug_ref[:, cs]
        tail = tail_scr[:, cs]
        before = [shift_in(group(u, seg - m), tail[ntaps - m:ntaps - m + 1])
                  for m in range(ntaps, 0, -1)]
        ext = jnp.concatenate(before + [u], axis=0)
        tail_scr[:, cs] = jnp.concatenate(
            [group(u, seg - m)[SUBLANES - 1:] for m in range(ntaps, 0, -1)], axis=0)
        cw = cw_ref[:, cs]
        conv = cb_ref[:, cs]
        for tap in range(CONV_W):
            conv = conv + ext[tap * SUBLANES:tap * SUBLANES + tc] * cw[tap:tap + 1]
        yield

        half_pre = _lru_gate_dots(conv, wr_ref[n], wi_ref[n])
        yield

        if pending is not None:
            store_time_order(*pending)
        yield

        a, b = _lru_gate_math(conv, half_pre, br_ref[:, cs], bi_ref[:, cs], lam_ref[:, cs])

        h = b[:SUBLANES]
        acc = a[:SUBLANES]
        h_loc, a_cum = [h], [acc]
        for j in range(1, seg):
            sl = slice(j * SUBLANES, (j + 1) * SUBLANES)
            h = a[sl] * h + b[sl]
            acc = a[sl] * acc
            h_loc.append(h)
            a_cum.append(acc)

        step = 1
        while step < SUBLANES:
            keep = sub >= step
            h = jnp.where(keep, acc * pltpu.roll(h, step, axis=0) + h, h)
            acc = jnp.where(keep, acc * pltpu.roll(acc, step, axis=0), acc)
            step *= 2
        h_prev = h_scr[:, cs]
        after = h + acc * h_prev
        h_in = shift_in(after, h_prev)
        h_scr[:, cs] = after[SUBLANES - 1:]

        hs = jnp.concatenate([h_loc[j] + a_cum[j] * h_in for j in range(seg)], axis=0)
        y = (hs * _silu(ug_ref[:, LRU_WIDTH + n * bw:LRU_WIDTH + (n + 1) * bw])).astype(BF16)
        pending = (cs, y)
        yield

    store_time_order(*pending)
    yield


def _rglru_front_kernel(nchunk, npairs, chunks, xs_ref, x_ref, g_ref, w_ref, cprev_ref, h0_ref,
                        scprev_ref, sh0_ref, cw_ref, cb_ref, wr_ref, br_ref, wi_ref, bi_ref, lam_ref,
                        gs_ref, hs_ref, scnew_ref, y_even_ref, y_odd_ref, cnew_ref, hlast_ref,
                        w_scr, wr_scr, wi_scr, us_scr, ug0_scr, ug1_scr, h_scr, tail_scr):
    i = pl.program_id(0)
    p = i - nchunk
    tc = SUB_ROWS
    half = nchunk // 2
    lru = (cw_ref, cb_ref, wr_scr, br_ref, wi_scr, bi_ref, lam_ref, h_scr, tail_scr)

    @pl.when(i < nchunk)
    def _():
        wb = w_ref[...].astype(BF16)
        w_scr[i] = wb
        xs = xs_ref[...]
        xsn = (xs * _rms_scale(xs) * g_ref[...]).astype(BF16)
        r = jnp.dot(xsn, wb, preferred_element_type=F32)
        gs_ref[...] = r

        @pl.when(i < half)
        def _():
            us_scr[i] = r

    @pl.when(i == nchunk - 1)
    def _():
        wr_scr[...] = (0.5 * wr_ref[...]).astype(BF16)
        wi_scr[...] = (0.5 * wi_ref[...]).astype(BF16)
        bw = LRU_BLOCK_W
        for n in range(LRU_BLOCKS):
            cs = slice(n * bw, (n + 1) * bw)
            c, off = divmod(n * bw, W_CHUNK)
            u = us_scr[c, :, off:off + bw]
            cw = cw_ref[:, cs]
            conv = cb_ref[:, cs]
            for tap in range(CONV_W - 1):
                conv = conv + scprev_ref[tap, :, cs] * cw[tap:tap + 1]
                if tap > 0:
                    scnew_ref[tap - 1, :, cs] = scprev_ref[tap, :, cs]
            conv = conv + u * cw[CONV_W - 1:]
            scnew_ref[CONV_W - 2, :, cs] = u
            a, b = _lru_gates(conv, wr_scr[n], br_ref[:, cs], wi_scr[n], bi_ref[:, cs], lam_ref[:, cs])
            hs_ref[:, cs] = a * sh0_ref[:, cs] + b
        ug1_scr[...] = jnp.zeros_like(ug1_scr)
        h_scr[...] = jnp.zeros_like(h_scr)
        tail_scr[...] = jnp.zeros_like(tail_scr)

    @pl.when(p >= 0)
    def _():
        _interleave((_in_proj_tile(x_ref, slice(0, tc), g_ref, w_scr, ug0_scr), 1),
                    (_rglru_tile(ug1_scr, y_odd_ref, *lru), 2))
        hlast_ref[0] = h_scr[...]
        cnew_ref[0] = tail_scr[...]

    @pl.when((p >= 0) & (p < npairs))
    def _():
        @pl.when((2 * p) % chunks == 0)
        def _():
            h_scr[...] = h0_ref[0]
            tail_scr[...] = cprev_ref[0]

        _interleave((_in_proj_tile(x_ref, slice(tc, 2 * tc), g_ref, w_scr, ug1_scr), 1),
                    (_rglru_tile(ug0_scr, y_even_ref, *lru), 2))


def rglru_front(xs, s_conv_prev, s_h0, x, conv_prev, h0, g, w_in, conv_w, conv_b, w_r, b_r, w_i, b_i, lam,
                seq_len):
    m, d = x.shape
    ns = xs.shape[0]
    w = w_in.shape[1] // 2
    tc = SUB_ROWS
    nchunk = w_in.shape[1] // W_CHUNK
    half = nchunk // 2
    bsz = m // seq_len
    chunks = seq_len // tc
    npairs = m // (2 * tc)
    assert seq_len % (2 * tc) == 0 and tc % (SUBLANES * SUBLANES) == 0 and tc // SUBLANES > CONV_W
    pair = lambda i: jnp.clip(i - nchunk, 0, npairs - 1)
    last = npairs * 2 - 1
    seq_in = lambda i: (jnp.clip(2 * (i - nchunk), 0, last) // chunks, 0, 0)
    seq_out = lambda i: (jnp.clip(2 * (i - nchunk) - 1, 0, last) // chunks, 0, 0)
    state_in = lambda rows: pl.BlockSpec((1, rows, w), seq_in)
    state_out = lambda rows: pl.BlockSpec((1, rows, w), seq_out)
    chunk = lambda i: (0, jnp.minimum(i, nchunk - 1))
    return pl.pallas_call(
        functools.partial(_rglru_front_kernel, nchunk, npairs, chunks),
        grid=(nchunk + npairs + 1,),
        in_specs=[_resident(xs.shape), pl.BlockSpec((2 * tc, d), lambda i: (pair(i), 0)), _resident((1, d)),
                  pl.BlockSpec((d, W_CHUNK), chunk), state_in(CONV_W - 1), state_in(1),
                  _resident(s_conv_prev.shape), _resident(s_h0.shape),
                  _resident((CONV_W, w)), _resident((1, w)), _resident(w_r.shape), _resident((1, w)),
                  _resident(w_i.shape), _resident((1, w)), _resident((1, w))],
        out_specs=[
            pl.BlockSpec((ns, W_CHUNK), lambda i: (0, jnp.clip(i - half, 0, half - 1))),
            pl.BlockSpec((ns, w), lambda i: (0, 0)),
            pl.BlockSpec(s_conv_prev.shape, lambda i: (0, 0, 0)),
            pl.BlockSpec((tc, w), lambda i: (pair(i), 0)),
            pl.BlockSpec((tc, w), lambda i: (jnp.clip(i - nchunk - 1, 0, npairs - 1), 0)),
            state_out(CONV_W - 1), state_out(1),
        ],
        out_shape=[
            jax.ShapeDtypeStruct((ns, w), F32),
            jax.ShapeDtypeStruct((ns, w), F32),
            jax.ShapeDtypeStruct(s_conv_prev.shape, F32),
            jax.ShapeDtypeStruct((m // 2, w), BF16),
            jax.ShapeDtypeStruct((m // 2, w), BF16),
            jax.ShapeDtypeStruct((bsz, CONV_W - 1, w), F32),
            jax.ShapeDtypeStruct((bsz, 1, w), F32),
        ],
        scratch_shapes=[pltpu.VMEM((nchunk, d, W_CHUNK), BF16),
                        pltpu.VMEM(w_r.shape, BF16), pltpu.VMEM(w_i.shape, BF16),
                        pltpu.VMEM((half, ns, W_CHUNK), F32),
                        pltpu.VMEM((tc, 2 * w), F32), pltpu.VMEM((tc, 2 * w), F32),
                        pltpu.VMEM((1, w), F32), pltpu.VMEM((CONV_W - 1, w), F32)],
        compiler_params=_params("arbitrary"),
        name="rglru_front",
    )(xs, x, g.reshape(1, d), w_in, conv_prev, h0.reshape(bsz, 1, w), s_conv_prev, s_h0,
      conv_w, conv_b.reshape(1, w), w_r, b_r.reshape(1, w), w_i, b_i.reshape(1, w), lam.reshape(1, w))


def _buckets(dist):
    n = jnp.maximum(dist, 0)
    max_exact = N_BUCKETS // 2
    nf = jnp.maximum(n, 1).astype(F32)
    large = max_exact + jnp.floor(jnp.log(nf / max_exact) / math.log(MAX_DISTANCE / max_exact)
                                  * (N_BUCKETS - max_exact)).astype(jnp.int32)
    large = jnp.minimum(large, N_BUCKETS - 1)
    return jnp.where(n < max_exact, n, large)


def _lookup(bucket, valid, table_ref, head):
    bias = jnp.zeros(bucket.shape, F32)
    for b in range(N_BUCKETS):
        bias = jnp.where(bucket == b, table_ref[b, head], bias)
    return jnp.where(valid, bias, NEG_INF)


def _bias_kernel(table_ref, sinks_ref, band_ref, sinkt_ref, past_ref, new_ref):
    hk = pl.program_id(0)
    span = 3 * BLOCK
    dist = (lax.broadcasted_iota(jnp.int32, (1, span), 1) + BLOCK) % span
    bucket = _buckets(dist)
    in_window = (dist >= 0) & (dist < WINDOW)
    key_row = lax.broadcasted_iota(jnp.int32, (ATT_KEYS, HALF_Q), 0)

    def band(head):
        row = _lookup(bucket, in_window, table_ref, head) * LOG2_E
        full = pltpu.roll(jnp.broadcast_to(row, (ATT_KEYS, span)), 0, axis=1, stride=1, stride_axis=0)
        return full[:, :HALF_Q]

    rows = past_ref.shape[2]
    d_past = rows - lax.broadcasted_iota(jnp.int32, (1, rows), 1)
    b_past = _buckets(d_past)
    ok_past = (d_past >= 0) & (d_past < WINDOW)
    d_new = jnp.zeros((1, LANES), jnp.int32)
    b_new = _buckets(d_new)
    for g in range(GROUP):
        head = hk * GROUP + g
        bias = band(head)
        for half in range(2):
            slot = _head_order(half).index(g)
            cs = slice(slot * HALF_Q, (slot + 1) * HALF_Q)
            band_ref[0, half, 0, :, cs] = bias
            prev_rows = BLOCK - half * HALF_Q
            band_ref[1, half, 0, :, cs] = jnp.where(key_row < prev_rows, NEG_INF, bias)
            sinkt_ref[half, 0, :, cs] = jnp.full((1, HALF_Q), sinks_ref[head] * LOG2_E, F32)
        past_ref[0, g:g + 1, :] = _lookup(b_past, ok_past, table_ref, head)
        new_ref[0, g:g + 1, :] = _lookup(b_new, d_new == 0, table_ref, head)


def bias_tables(table, sinks, past_rows):
    smem = pl.BlockSpec(memory_space=pltpu.SMEM)
    return pl.pallas_call(
        _bias_kernel,
        grid=(N_KV_HEADS,),
        in_specs=[smem, smem],
        out_specs=[
            pl.BlockSpec((2, 2, 1, ATT_KEYS, GROUP * HALF_Q), lambda h: (0, 0, h, 0, 0)),
            pl.BlockSpec((2, 1, 1, GROUP * HALF_Q), lambda h: (0, h, 0, 0)),
            pl.BlockSpec((1, GROUP, past_rows), lambda h: (h, 0, 0)),
            pl.BlockSpec((1, GROUP, LANES), lambda h: (h, 0, 0)),
        ],
        out_shape=[
            jax.ShapeDtypeStruct((2, 2, N_KV_HEADS, ATT_KEYS, GROUP * HALF_Q), F32),
            jax.ShapeDtypeStruct((2, N_KV_HEADS, 1, GROUP * HALF_Q), F32),
            jax.ShapeDtypeStruct((N_KV_HEADS, GROUP, past_rows), F32),
            jax.ShapeDtypeStruct((N_KV_HEADS, GROUP, LANES), F32),
        ],
        compiler_params=_params("parallel"),
        name="bias_tables",
    )(table, sinks)


def _band_attn_kernel(q_ref, kp_ref, kc_ref, vp_ref, vc_ref, gate_ref, bias_ref, sink_ref, y_ref,
                      s_scr, p_scr):
    first_tile = (pl.program_id(1) == 0).astype(jnp.int32)
    nt = (((1,), (1,)), ((), ()))
    low = (lax.broadcasted_iota(jnp.int32, (1, LANES), 1) < HEAD_DIM)
    keep_low = low.astype(BF16)
    keep_high = 1 - keep_low
    keep = (keep_low, keep_high)
    nkeys = 2 * BLOCK
    ones_rows = jnp.where(lax.broadcasted_iota(jnp.int32, (2 * SUBLANES, nkeys), 0) == 0,
                          1.0, 0.0).astype(BF16)
    rows = ATT_ROWS
    nslot = s_scr.shape[0]
    items = [(blk, hk, half) for blk in range(q_ref.shape[0] // BLOCK)
             for hk in range(N_KV_HEADS) for half in range(2)]
    assert nslot % 2 == 0

    def rows_of(blk):
        return slice(blk * BLOCK, (blk + 1) * BLOCK)

    def key_rows(half):
        return slice(half * HALF_Q, half * HALF_Q + ATT_KEYS)

    for slot in range(nslot):
        dead = slice(ATT_KEYS, nkeys) if slot % 2 == 0 else slice(0, HALF_Q)
        p_scr[slot, dead, :] = jnp.zeros((HALF_Q, p_scr.shape[2]), BF16)

    def scores(idx):
        blk, hk, half = items[idx]
        variant = first_tile if blk == 0 else 0
        cs = slice(hk * LANES, (hk + 1) * LANES)
        k_prev = kp_ref[:, cs] if blk == 0 else kc_ref[rows_of(blk - 1), cs]
        k_cur = kc_ref[rows_of(blk), cs]
        kd = (jnp.concatenate([k_prev, k_cur[:HALF_Q]], axis=0) if half == 0
              else jnp.concatenate([k_prev[HALF_Q:], k_cur], axis=0))
        q0 = blk * BLOCK + half * HALF_Q
        qs = jnp.concatenate(
            [q_ref[q0:q0 + HALF_Q, (hk * SLABS + h // HEADS_PER_TILE) * LANES:
                   (hk * SLABS + h // HEADS_PER_TILE + 1) * LANES] * keep[h % HEADS_PER_TILE]
             for h in _head_order(half)], axis=0)
        s = lax.dot_general(kd, qs, nt, preferred_element_type=F32) + bias_ref[variant, half, hk]
        s_scr[idx % nslot, key_rows(half), :] = s
        return jnp.maximum(jnp.max(s, axis=0, keepdims=True), sink_ref[half, hk])

    def softmax(idx, m):
        blk, hk, half = items[idx]
        slot = idx % nslot
        lo = half * HALF_Q
        for r in range(lo, lo + ATT_KEYS, rows):
            p_scr[slot, r:r + rows, :] = jnp.exp2(s_scr[slot, r:r + rows, :] - m).astype(BF16)
        return jnp.exp2(sink_ref[half, hk] - m)

    def weighted_values(idx):
        blk, hk, half = items[idx]
        vs = slice(hk * LANES, hk * LANES + HEAD_DIM)
        v_prev = vp_ref[vs, :] if blk == 0 else vc_ref[vs, rows_of(blk - 1)]
        vt = jnp.concatenate([v_prev, vc_ref[vs, rows_of(blk)]], axis=1)
        lhs_v = jnp.concatenate([vt, ones_rows], axis=0)
        return jnp.dot(lhs_v, p_scr[idx % nslot], preferred_element_type=F32)

    low_q = lax.broadcasted_iota(jnp.int32, (HEAD_DIM, LANES), 1) < HALF_Q

    def finish(blk, hk, ots, sink_ws):
        o = [ots[half][:HEAD_DIM] * (1.0 / (ots[half][HEAD_DIM:HEAD_DIM + 1] + sink_ws[half]))
             for half in range(2)]
        for sl in range(SLABS):
            a, b = (oh[:, sl * LANES:(sl + 1) * LANES] for oh in o)
            even = jnp.where(low_q, a, b)
            odd = pltpu.roll(jnp.where(low_q, b, a), HALF_Q, axis=1)
            pair = jnp.concatenate([even, odd], axis=0)
            c0 = (hk * SLABS + sl) * LANES
            y_ref[rows_of(blk), c0:c0 + LANES] = (
                pair.T * _silu(gate_ref[rows_of(blk), c0:c0 + LANES])).astype(y_ref.dtype)

    n = len(items)
    offs, sink_ws, outs = {}, {}, {}
    for k in range(-2 * ATT_SKEW, n + ATT_SKEW):
        if 0 <= k + 2 * ATT_SKEW < n:
            offs[k + 2 * ATT_SKEW] = scores(k + 2 * ATT_SKEW)
        if 0 <= k + ATT_SKEW < n:
            sink_ws[k + ATT_SKEW] = softmax(k + ATT_SKEW, offs.pop(k + ATT_SKEW))
        if 0 <= k < n:
            outs[k] = weighted_values(k)
        j = k - ATT_SKEW
        if 0 <= j < n and items[j][2] == 1:
            finish(items[j][0], items[j][1], [outs.pop(j - 1), outs.pop(j)],
                   [sink_ws.pop(j - 1), sink_ws.pop(j)])


def band_attention(q, kdup, vt, gate, bias_band, sink_t, bsz, t):
    m = q.shape[0]
    nblk = t // BLOCK
    per_tile = ATT_TILE // BLOCK
    ntile = t // ATT_TILE
    assert t % ATT_TILE == 0
    before = lambda b, i: b * nblk + jnp.maximum(per_tile * i - 1, 0)
    cur = lambda n: pl.BlockSpec((ATT_TILE, n), lambda b, i: (b * ntile + i, 0))
    prev = lambda n: pl.BlockSpec((BLOCK, n), lambda b, i: (before(b, i), 0))
    cur_t = pl.BlockSpec((2 * KV_WIDTH, ATT_TILE), lambda b, i: (0, b * ntile + i))
    prev_t = pl.BlockSpec((2 * KV_WIDTH, BLOCK), lambda b, i: (0, before(b, i)))
    score_tile = (2 * BLOCK, GROUP * HALF_Q)
    return pl.pallas_call(
        _band_attn_kernel,
        grid=(bsz, ntile),
        in_specs=[
            cur(ATT_WIDTH), prev(2 * KV_WIDTH), cur(2 * KV_WIDTH), prev_t, cur_t,
            cur(ATT_WIDTH), _resident(bias_band.shape), _resident(sink_t.shape),
        ],
        out_specs=cur(ATT_WIDTH),
        out_shape=jax.ShapeDtypeStruct((m, ATT_WIDTH), BF16),
        scratch_shapes=[pltpu.VMEM((ATT_SLOTS,) + score_tile, F32),
                        pltpu.VMEM((ATT_SLOTS,) + score_tile, BF16)],
        compiler_params=_params("parallel", "parallel"),
        name="band_attention",
    )(q, kdup, kdup, vt, vt, gate, bias_band, sink_t)


def _cached_attn_kernel(q_ref, ckt_ref, cvt_ref, kn_ref, vn_ref, sinks_ref, bpast_ref, bnew_ref, o_ref):
    shape = (N_Q_HEADS, KV_WIDTH)
    lane_kv = lax.broadcasted_iota(jnp.int32, shape, 1) // HEAD_DIM
    row_kv = lax.broadcasted_iota(jnp.int32, shape, 0) // GROUP
    own = lane_kv == row_kv
    sink = sinks_ref[...]
    nt = (((1,), (1,)), ((), ()))
    for b in range(q_ref.shape[0]):
        q = q_ref[b]
        qt = jnp.concatenate([q] * N_KV_HEADS, axis=1)
        qm = jnp.where(own, qt, 0.0).astype(BF16)
        knew = kn_ref[b].astype(BF16).astype(F32)
        vnew = vn_ref[b].astype(BF16).astype(F32)
        s = jnp.dot(qm, ckt_ref[b].astype(BF16), preferred_element_type=F32) + bpast_ref[...]
        s_new = jnp.sum(qm.astype(F32) * knew, axis=-1, keepdims=True) + bnew_ref[:, :1]
        m = jnp.maximum(jnp.maximum(jnp.max(s, axis=-1, keepdims=True), s_new), sink)
        p = jnp.exp(s - m)
        p_new = jnp.exp(s_new - m)
        denom = jnp.sum(p, axis=-1, keepdims=True) + p_new + jnp.exp(sink - m)
        o_all = (lax.dot_general(p.astype(BF16), cvt_ref[b].astype(BF16), nt, preferred_element_type=F32)
                 + p_new.astype(BF16).astype(F32) * vnew)
        o_all = jnp.where(own, o_all, 0.0)
        o = o_all[:, :HEAD_DIM]
        for hk in range(1, N_KV_HEADS):
            o = o + o_all[:, hk * HEAD_DIM:(hk + 1) * HEAD_DIM]
        o_ref[b] = o / denom


def cached_attention(q, cache_kt, cache_vt, k_new, v_new, sinks, bias_past, bias_new):
    bsz, _, rows = cache_kt.shape
    nseq = math.gcd(bsz, SEQS_PER_STEP)
    per_seq = lambda r, n: pl.BlockSpec((nseq, r, n), lambda b: (b, 0, 0))
    return pl.pallas_call(
        _cached_attn_kernel,
        grid=(bsz // nseq,),
        in_specs=[
            per_seq(N_Q_HEADS, HEAD_DIM), per_seq(KV_WIDTH, rows), per_seq(KV_WIDTH, rows),
            per_seq(1, KV_WIDTH), per_seq(1, KV_WIDTH),
            _resident((N_Q_HEADS, 1)), _resident((N_Q_HEADS, rows)), _resident((N_Q_HEADS, LANES)),
        ],
        out_specs=per_seq(N_Q_HEADS, HEAD_DIM),
        out_shape=jax.ShapeDtypeStruct((bsz, N_Q_HEADS, HEAD_DIM), F32),
        compiler_params=_params("parallel"),
        name="cached_attention",
    )(q, cache_kt, cache_vt, k_new, v_new, sinks, bias_past, bias_new)


def kernel(x_prompt, x_sample, state_conv, state_h, cache_k, cache_v, a_norm_pre, a_norm_post,
           a_w_in, a_conv_w, a_conv_b, a_w_r, a_b_r, a_w_i, a_b_i, a_lambda, a_w_out, kv_norm, w_kv,
           b_norm_pre, b_norm_post, b_w_qg, b_sinks, b_w_out, rel_bias_table):
    bsz, t, d = x_prompt.shape
    dbsz, dt, _ = x_sample.shape
    assert a_w_in.shape[0] == 1 and b_w_qg.shape[0] == 1 and dt == 1
    assert t % BLOCK == 0 and t >= WINDOW
    past_rows = cache_k.shape[1]
    assert past_rows == min(WINDOW, PAST_LEN)

    sinks = b_sinks[0]
    bias_band, sink_t, bias_past, bias_new = bias_tables(rel_bias_table, sinks, past_rows)

    tm = 2 * SUB_ROWS
    xp = x_prompt.reshape(bsz * t, d)
    xs = x_sample.reshape(dbsz, d)

    conv0 = jnp.zeros((bsz, CONV_W - 1, LRU_WIDTH), F32)
    h0 = jnp.zeros((bsz, LRU_WIDTH), F32)
    gate_s, hs, s_conv_t, y_even, y_odd, p_conv, p_h = rglru_front(
        xs, jnp.transpose(state_conv[0], (1, 0, 2)), state_h[0], xp, conv0, h0, a_norm_pre[0], a_w_in[0],
        a_conv_w[0], a_conv_b[0], a_w_r[0], a_b_r[0], a_w_i[0], a_b_i[0], a_lambda[0], seq_len=t)
    xs1, x1 = proj_norm_res(hs, gate_s, xs, (y_even, y_odd), a_w_out[0], a_norm_post[0], xp, tm)

    ks, vs, qs, gate_sb, q, gate_b, kdup, vt, k_tail, v_tail = norm_proj_kvq(
        xs1, x1, kv_norm, b_norm_pre[0], w_kv, b_w_qg[0], SUB_ROWS, seq_len=t)
    cache_kt = jnp.transpose(cache_k, (0, 2, 3, 1)).reshape(dbsz, KV_WIDTH, past_rows)
    cache_vt = jnp.transpose(cache_v, (0, 2, 3, 1)).reshape(dbsz, KV_WIDTH, past_rows)
    os_ = cached_attention(qs.reshape(dbsz, N_Q_HEADS, HEAD_DIM), cache_kt, cache_vt,
                           ks.reshape(dbsz, 1, KV_WIDTH), vs.reshape(dbsz, 1, KV_WIDTH),
                           sinks.reshape(N_Q_HEADS, 1), bias_past.reshape(N_Q_HEADS, past_rows),
                           bias_new.reshape(N_Q_HEADS, LANES))
    yb = band_attention(q, kdup, vt, gate_b, bias_band, sink_t, bsz, t)
    y_sample, y_prompt = proj_norm_res(os_.reshape(dbsz, ATT_WIDTH), gate_sb, xs1, (yb,),
                                       b_w_out[0], b_norm_post[0], x1, tm)
    y_prompt = y_prompt.reshape(bsz, t, d)
    p_k = jnp.transpose(k_tail.reshape(bsz, N_KV_HEADS, HEAD_DIM, WINDOW), (0, 3, 1, 2))
    p_v = jnp.transpose(v_tail.reshape(bsz, N_KV_HEADS, HEAD_DIM, WINDOW), (0, 3, 1, 2))

    return (y_prompt, y_sample.reshape(dbsz, 1, d),
            p_conv[None], p_h.reshape(1, bsz, LRU_WIDTH), p_k, p_v,
            jnp.transpose(s_conv_t, (1, 0, 2))[None], hs[None],
            ks.reshape(dbsz, 1, N_KV_HEADS, HEAD_DIM), vs.reshape(dbsz, 1, N_KV_HEADS, HEAD_DIM))
```

```python
import functools
import math

import jax
import jax.numpy as jnp
from jax import lax
from jax.experimental import pallas as pl
from jax.experimental.pallas import tpu as pltpu

F32 = jnp.float32
BF16 = jnp.bfloat16

D_MODEL = 2048
LRU_WIDTH = 2048
LRU_BLOCKS = 8
LRU_BLOCK_W = LRU_WIDTH // LRU_BLOCKS
CONV_W = 4
LRU_C = 8.0
HEAD_DIM = 64
N_Q_HEADS = 32
N_KV_HEADS = 8
GROUP = N_Q_HEADS // N_KV_HEADS
ATT_WIDTH = N_Q_HEADS * HEAD_DIM
KV_WIDTH = N_KV_HEADS * HEAD_DIM
WINDOW = 128
BLOCK = WINDOW
N_BUCKETS = 32
MAX_DISTANCE = 128
RMS_EPS = 1e-6
NEG_INF = -1e30
LOG2_E = 1.4426950408889634
PAST_LEN = 16384

V7X_VMEM_BYTES = 64 * 1024 * 1024
VMEM_LIMIT = V7X_VMEM_BYTES - 8 * 1024 * 1024
SUBLANES = 8
LANES = 128
HEADS_PER_TILE = LANES // HEAD_DIM
SLABS = GROUP // HEADS_PER_TILE
MXU_COLS = 256
SUB_ROWS = 256
W_CHUNK = 512
ATT_TILE = 4 * BLOCK
ATT_ROWS = 64
ATT_SKEW = 2
ATT_SLOTS = 2 * ATT_SKEW + 2
HALF_Q = BLOCK // 2
ATT_KEYS = WINDOW + HALF_Q
SEQS_PER_STEP = 8
ATT_PROJ_TILE = 2 * BLOCK
PROJ_TAIL_PIECES = 4
ATT_ROUNDS_PER_PROJ_PIECE = 3
SQRT_FLOOR = 1e-30


def _head_order(half):
    heads = list(range(GROUP))
    return heads if half == 0 else [h ^ 1 for h in heads]


def _params(*semantics):
    return pltpu.CompilerParams(dimension_semantics=semantics, vmem_limit_bytes=VMEM_LIMIT)


def _resident(shape):
    zeros = (0,) * len(shape)
    return pl.BlockSpec(shape, lambda *_: zeros, pipeline_mode=pl.Buffered(1))


def _rms_scale(x):
    return lax.rsqrt(jnp.mean(x * x, axis=-1, keepdims=True) + RMS_EPS)


def _silu(x):
    h = 0.5 * x
    return h * jnp.tanh(h) + h


def _segment_major(rows, inverse=False):
    seg = rows // SUBLANES
    r = lax.broadcasted_iota(jnp.int32, (rows, rows), 0)
    c = lax.broadcasted_iota(jnp.int32, (rows, rows), 1)
    if inverse:
        src = (r % seg) * SUBLANES + r // seg
    else:
        src = (r % SUBLANES) * seg + r // SUBLANES
    return jnp.where(c == src, 1.0, 0.0).astype(BF16)


def _phase_specs(nchunk, tm, k):
    chunk_w = pl.BlockSpec((k, W_CHUNK), lambda i: (0, jnp.minimum(i, nchunk - 1)))
    tile = lambda n: pl.BlockSpec((tm, n), lambda i: (jnp.maximum(i - nchunk, 0), 0))
    return chunk_w, tile


def _dup_heads(x):
    low = lax.broadcasted_iota(jnp.int32, (x.shape[0], LANES), 1) < HEAD_DIM
    out = []
    for c in range(x.shape[1] // LANES):
        col = x[:, c * LANES:(c + 1) * LANES]
        swapped = pltpu.roll(col, HEAD_DIM, axis=1)
        out += [jnp.where(low, col, swapped), jnp.where(low, swapped, col)]
    return jnp.concatenate(out, axis=1)


def _norm_proj_kvq_kernel(nkv, nqg, xs_ref, x_ref, gkv_ref, gq_ref, wkv_ref, wqg_ref,
                          ks_ref, vs_ref, qs_ref, gates_ref,
                          q_ref, gate_ref, kdup_ref, vt_ref, ktail_ref, vtail_ref,
                          wkv_scr, wqg_scr):
    i = pl.program_id(0)
    nchunk = nkv + nqg
    q_chunks = ATT_WIDTH // W_CHUNK
    q_scale = 1.0 / math.sqrt(HEAD_DIM)
    q_scale_log2 = q_scale * LOG2_E

    def sample_rows(g_ref):
        xs = xs_ref[...]
        return (xs * _rms_scale(xs) * g_ref[...]).astype(BF16)

    @pl.when(i < nkv)
    def _():
        wb = wkv_ref[...].astype(BF16)
        wkv_scr[i] = wb
        r = jnp.dot(sample_rows(gkv_ref), wb, preferred_element_type=F32)

        @pl.when(i == 0)
        def _():
            ks_ref[...] = r

        @pl.when(i == 1)
        def _():
            vs_ref[...] = r

    @pl.when((i >= nkv) & (i < nchunk))
    def _():
        c = i - nkv
        wb = wqg_ref[...].astype(BF16)
        wqg_scr[c] = wb
        r = jnp.dot(sample_rows(gq_ref), wb, preferred_element_type=F32)

        @pl.when(c < q_chunks)
        def _():
            qs_ref[...] = r * q_scale

        @pl.when(c >= q_chunks)
        def _():
            gates_ref[...] = r

    @pl.when(i >= nchunk)
    def _():
        tm = x_ref.shape[0]
        for rs in _row_blocks(tm):
            x = x_ref[rs, :]
            xh = x * _rms_scale(x)
            xkv = (xh * gkv_ref[...]).astype(BF16)
            xq = (xh * gq_ref[...]).astype(BF16)
            k = jnp.dot(xkv, wkv_scr[0], preferred_element_type=F32)
            v = jnp.dot(xkv, wkv_scr[1], preferred_element_type=F32)
            kdup_ref[rs, :] = _dup_heads(k).astype(BF16)
            vt_ref[:, rs] = _dup_heads(v).T.astype(BF16)
            for c in range(nqg):
                r = jnp.dot(xq, wqg_scr[c], preferred_element_type=F32)
                if c < q_chunks:
                    q_ref[rs, c * W_CHUNK:(c + 1) * W_CHUNK] = (r * q_scale_log2).astype(q_ref.dtype)
                else:
                    cc = c - q_chunks
                    gate_ref[rs, cc * W_CHUNK:(cc + 1) * W_CHUNK] = r
        ktail_ref[0] = k[k.shape[0] - WINDOW:].T
        vtail_ref[0] = v[v.shape[0] - WINDOW:].T


def norm_proj_kvq(xs, x, g_kv, g_q, w_kv, w_qg, tm, seq_len):
    m, d = x.shape
    ns = xs.shape[0]
    assert w_kv.shape[1] == 2 * KV_WIDTH == 2 * W_CHUNK and seq_len % tm == 0 and tm >= WINDOW
    nkv, nqg = w_kv.shape[1] // W_CHUNK, w_qg.shape[1] // W_CHUNK
    nchunk = nkv + nqg
    tiles = seq_len // tm
    tile = lambda n: pl.BlockSpec((tm, n), lambda i: (jnp.maximum(i - nchunk, 0), 0))
    tail = pl.BlockSpec((1, KV_WIDTH, WINDOW), lambda i: (jnp.maximum(i - nchunk, 0) // tiles, 0, 0))
    kv_chunk = lambda i: (0, jnp.minimum(i, nkv - 1))
    qg_chunk = lambda i: (0, jnp.clip(i - nkv, 0, nqg - 1))
    q_chunks = ATT_WIDTH // W_CHUNK
    whole_s = lambda n: pl.BlockSpec((ns, n), lambda i: (0, 0))
    return pl.pallas_call(
        functools.partial(_norm_proj_kvq_kernel, nkv, nqg),
        grid=(nchunk + m // tm,),
        in_specs=[
            _resident(xs.shape), tile(d), _resident((1, d)), _resident((1, d)),
            pl.BlockSpec((d, W_CHUNK), kv_chunk), pl.BlockSpec((d, W_CHUNK), qg_chunk),
        ],
        out_specs=[
            whole_s(KV_WIDTH), whole_s(KV_WIDTH),
            pl.BlockSpec((ns, W_CHUNK), lambda i: (0, jnp.clip(i - nkv, 0, q_chunks - 1))),
            pl.BlockSpec((ns, W_CHUNK), lambda i: (0, jnp.clip(i - nkv - q_chunks, 0, nqg - q_chunks - 1))),
            tile(ATT_WIDTH), tile(ATT_WIDTH), tile(2 * KV_WIDTH),
            pl.BlockSpec((2 * KV_WIDTH, tm), lambda i: (0, jnp.maximum(i - nchunk, 0))), tail, tail,
        ],
        out_shape=[
            jax.ShapeDtypeStruct((ns, KV_WIDTH), F32),
            jax.ShapeDtypeStruct((ns, KV_WIDTH), F32),
            jax.ShapeDtypeStruct((ns, ATT_WIDTH), F32),
            jax.ShapeDtypeStruct((ns, w_qg.shape[1] - ATT_WIDTH), F32),
            jax.ShapeDtypeStruct((m, ATT_WIDTH), BF16),
            jax.ShapeDtypeStruct((m, ATT_WIDTH), F32),
            jax.ShapeDtypeStruct((m, 2 * KV_WIDTH), BF16),
            jax.ShapeDtypeStruct((2 * KV_WIDTH, m), BF16),
            jax.ShapeDtypeStruct((m // seq_len, KV_WIDTH, WINDOW), F32),
            jax.ShapeDtypeStruct((m // seq_len, KV_WIDTH, WINDOW), F32),
        ],
        scratch_shapes=[pltpu.VMEM((nkv, d, W_CHUNK), BF16), pltpu.VMEM((nqg, d, W_CHUNK), BF16)],
        compiler_params=_params("arbitrary"),
        name="norm_proj_kvq",
    )(xs, x, g_kv.reshape(1, d), g_q.reshape(1, d), w_kv, w_qg)


def _row_blocks(rows):
    sub = min(rows, SUB_ROWS)
    return [slice(r, r + sub) for r in range(0, rows, sub)]


def _proj_norm_res_kernel(nchunk, nparts, as_ref, gs_ref, xs_ref, *refs):
    y_refs = refs[:nparts]
    w_ref, g_ref, x_ref, os_ref, o_ref, w_scr, raw_scr = refs[nparts:]
    i = pl.program_id(0)

    @pl.when(i < nchunk)
    def _():
        wb = w_ref[...].astype(BF16)
        w_scr[i] = wb
        ys = (as_ref[...] * _silu(gs_ref[...])).astype(BF16)
        raw_scr[i] = jnp.dot(ys, wb, preferred_element_type=F32)

    @pl.when(i == nchunk - 1)
    def _():
        o = jnp.concatenate([raw_scr[c] for c in range(nchunk)], axis=1)
        os_ref[...] = xs_ref[...] + o * _rms_scale(o) * g_ref[...]

    @pl.when(i >= nchunk)
    def _():
        tm = x_ref.shape[0]
        part_rows = tm // nparts

        for rs in _row_blocks(tm):
            part, off = divmod(rs.start, part_rows)
            y = y_refs[part][off:off + rs.stop - rs.start, :]
            o = jnp.concatenate([jnp.dot(y, w_scr[c], preferred_element_type=F32)
                                 for c in range(nchunk)], axis=1)
            o_ref[rs, :] = x_ref[rs, :] + o * _rms_scale(o) * g_ref[...]


def proj_norm_res(a_s, gate_s, x_s, y_parts, w, g, x, tm):
    k, d = w.shape
    m = x.shape[0]
    nchunk = d // W_CHUNK
    nparts = len(y_parts)
    assert (tm // nparts) % min(tm, SUB_ROWS) == 0
    chunk_w, tile = _phase_specs(nchunk, tm, k)
    part = pl.BlockSpec((tm // nparts, k), lambda i: (jnp.maximum(i - nchunk, 0), 0))
    return pl.pallas_call(
        functools.partial(_proj_norm_res_kernel, nchunk, nparts),
        grid=(nchunk + m // tm,),
        in_specs=[_resident(a_s.shape), _resident(gate_s.shape), _resident(x_s.shape)]
        + [part] * nparts + [chunk_w, _resident((1, d)), tile(d)],
        out_specs=[pl.BlockSpec(x_s.shape, lambda i: (0, 0)), tile(d)],
        out_shape=[jax.ShapeDtypeStruct(x_s.shape, F32), jax.ShapeDtypeStruct((m, d), F32)],
        scratch_shapes=[pltpu.VMEM((nchunk, k, W_CHUNK), BF16),
                        pltpu.VMEM((nchunk, x_s.shape[0], W_CHUNK), F32)],
        compiler_params=_params("arbitrary"),
        name="proj_norm_res",
    )(a_s, gate_s, x_s, *y_parts, w, g.reshape(1, d), x)


def _lru_gate_dots(conv, wr_half, wi_half):
    cb = conv.astype(BF16)
    return (jnp.dot(cb, wr_half, preferred_element_type=F32),
            jnp.dot(cb, wi_half, preferred_element_type=F32))


def _lru_gates(conv, wr_half, br, wi_half, bi, lam):
    return _lru_gate_math(conv, _lru_gate_dots(conv, wr_half, wi_half), br, bi, lam)


def _lru_gate_math(conv, half_pre, br, bi, lam):
    th_r = jnp.tanh(half_pre[0] + 0.5 * br)
    th_i = jnp.tanh(half_pre[1] + 0.5 * bi)
    nl = -lam
    softplus = jnp.maximum(nl, 0.0) + jnp.log1p(jnp.exp(-jnp.abs(nl)))
    half = (0.5 * LRU_C) * softplus
    x = th_r * half + half
    a = jnp.exp2(x * -LOG2_E)
    z = jnp.tanh(x) * (a * a + 1.0)
    mult = z * lax.rsqrt(jnp.maximum(z, SQRT_FLOOR))
    hc = 0.5 * conv
    return a, mult * (hc * th_i + hc)


def _interleave(*stages):
    live = [[stage, share] for stage, share in stages]
    while live:
        for entry in list(live):
            try:
                for _ in range(entry[1]):
                    next(entry[0])
            except StopIteration:
                live.remove(entry)


def _in_proj_tile(x_ref, rs, g_ref, w_scr, ug_ref):
    x = x_ref[rs, :]
    xn = (x * _rms_scale(x) * g_ref[...]).astype(BF16)
    xn = jnp.dot(_segment_major(xn.shape[0]), xn, preferred_element_type=F32).astype(BF16)
    for c in range(w_scr.shape[0]):
        for n0 in range(0, W_CHUNK, MXU_COLS):
            ug_ref[:, c * W_CHUNK + n0:c * W_CHUNK + n0 + MXU_COLS] = jnp.dot(
                xn, w_scr[c, :, n0:n0 + MXU_COLS], preferred_element_type=F32)
            yield


def _rglru_tile(ug_ref, y_ref, cw_ref, cb_ref, wr_ref, br_ref, wi_ref, bi_ref, lam_ref, h_scr, tail_scr):
    tc = ug_ref.shape[0]
    seg = tc // SUBLANES
    ntaps = CONV_W - 1
    bw = LRU_BLOCK_W
    sub = lax.broadcasted_iota(jnp.int32, (SUBLANES, bw), 0)
    first = sub == 0
    time_order = _segment_major(tc, inverse=True)

    def shift_in(x, row0):
        return jnp.where(first, row0, pltpu.roll(x, 1, axis=0))

    def group(x, j):
        return x[j * SUBLANES:(j + 1) * SUBLANES]

    def store_time_order(cols, y):
        y_ref[:, cols] = jnp.dot(time_order, y, preferred_element_type=F32).astype(y_ref.dtype)

    pending = None
    for n in range(LRU_BLOCKS):
        cs = slice(n * bw, (n + 1) * bw)
        u = ug_ref[:, cs]
        tail = tail_scr[:, cs]
        before = [shift_in(group(u, seg - m), tail[ntaps - m:ntaps - m + 1])
                  for m in range(ntaps, 0, -1)]
        ext = jnp.concatenate(before + [u], axis=0)
        tail_scr[:, cs] = jnp.concatenate(
            [group(u, seg - m)[SUBLANES - 1:] for m in range(ntaps, 0, -1)], axis=0)
        cw = cw_ref[:, cs]
        conv = cb_ref[:, cs]
        for tap in range(CONV_W):
            conv = conv + ext[tap * SUBLANES:tap * SUBLANES + tc] * cw[tap:tap + 1]
        yield

        half_pre = _lru_gate_dots(conv, wr_ref[n], wi_ref[n])
        yield

        if pending is not None:
            store_time_order(*pending)
        yield

        a, b = _lru_gate_math(conv, half_pre, br_ref[:, cs], bi_ref[:, cs], lam_ref[:, cs])

        h = b[:SUBLANES]
        acc = a[:SUBLANES]
        h_loc, a_cum = [h], [acc]
        for j in range(1, seg):
            sl = slice(j * SUBLANES, (j + 1) * SUBLANES)
            h = a[sl] * h + b[sl]
            acc = a[sl] * acc
            h_loc.append(h)
            a_cum.append(acc)

        step = 1
        while step < SUBLANES:
            keep = sub >= step
            h = jnp.where(keep, acc * pltpu.roll(h, step, axis=0) + h, h)
            acc = jnp.where(keep, acc * pltpu.roll(acc, step, axis=0), acc)
            step *= 2
        h_prev = h_scr[:, cs]
        after = h + acc * h_prev
        h_in = shift_in(after, h_prev)
        h_scr[:, cs] = after[SUBLANES - 1:]

        hs = jnp.concatenate([h_loc[j] + a_cum[j] * h_in for j in range(seg)], axis=0)
        y = (hs * _silu(ug_ref[:, LRU_WIDTH + n * bw:LRU_WIDTH + (n + 1) * bw])).astype(BF16)
        pending = (cs, y)
        yield

    store_time_order(*pending)
    yield


def _rglru_front_kernel(nchunk, npairs, chunks, xs_ref, x_ref, g_ref, w_ref, cprev_ref, h0_ref,
                        scprev_ref, sh0_ref, cw_ref, cb_ref, wr_ref, br_ref, wi_ref, bi_ref, lam_ref,
                        gs_ref, hs_ref, scnew_ref, y_even_ref, y_odd_ref, cnew_ref, hlast_ref,
                        w_scr, wr_scr, wi_scr, us_scr, ug0_scr, ug1_scr, h_scr, tail_scr):
    i = pl.program_id(0)
    p = i - nchunk
    tc = SUB_ROWS
    half = nchunk // 2
    lru = (cw_ref, cb_ref, wr_scr, br_ref, wi_scr, bi_ref, lam_ref, h_scr, tail_scr)

    @pl.when(i < nchunk)
    def _():
        wb = w_ref[...].astype(BF16)
        w_scr[i] = wb
        xs = xs_ref[...]
        xsn = (xs * _rms_scale(xs) * g_ref[...]).astype(BF16)
        r = jnp.dot(xsn, wb, preferred_element_type=F32)
        gs_ref[...] = r

        @pl.when(i < half)
        def _():
            us_scr[i] = r

    @pl.when(i == nchunk - 1)
    def _():
        wr_scr[...] = (0.5 * wr_ref[...]).astype(BF16)
        wi_scr[...] = (0.5 * wi_ref[...]).astype(BF16)
        bw = LRU_BLOCK_W
        for n in range(LRU_BLOCKS):
            cs = slice(n * bw, (n + 1) * bw)
            c, off = divmod(n * bw, W_CHUNK)
            u = us_scr[c, :, off:off + bw]
            cw = cw_ref[:, cs]
            conv = cb_ref[:, cs]
            for tap in range(CONV_W - 1):
                conv = conv + scprev_ref[tap, :, cs] * cw[tap:tap + 1]
                if tap > 0:
                    scnew_ref[tap - 1, :, cs] = scprev_ref[tap, :, cs]
            conv = conv + u * cw[CONV_W - 1:]
            scnew_ref[CONV_W - 2, :, cs] = u
            a, b = _lru_gates(conv, wr_scr[n], br_ref[:, cs], wi_scr[n], bi_ref[:, cs], lam_ref[:, cs])
            hs_ref[:, cs] = a * sh0_ref[:, cs] + b
        ug1_scr[...] = jnp.zeros_like(ug1_scr)
        h_scr[...] = jnp.zeros_like(h_scr)
        tail_scr[...] = jnp.zeros_like(tail_scr)

    @pl.when(p >= 0)
    def _():
        _interleave((_in_proj_tile(x_ref, slice(0, tc), g_ref, w_scr, ug0_scr), 1),
                    (_rglru_tile(ug1_scr, y_odd_ref, *lru), 2))
        hlast_ref[0] = h_scr[...]
        cnew_ref[0] = tail_scr[...]

    @pl.when((p >= 0) & (p < npairs))
    def _():
        @pl.when((2 * p) % chunks == 0)
        def _():
            h_scr[...] = h0_ref[0]
            tail_scr[...] = cprev_ref[0]

        _interleave((_in_proj_tile(x_ref, slice(tc, 2 * tc), g_ref, w_scr, ug1_scr), 1),
                    (_rglru_tile(ug0_scr, y_even_ref, *lru), 2))


def rglru_front(xs, s_conv_prev, s_h0, x, conv_prev, h0, g, w_in, conv_w, conv_b, w_r, b_r, w_i, b_i, lam,
                seq_len):
    m, d = x.shape
    ns = xs.shape[0]
    w = w_in.shape[1] // 2
    tc = SUB_ROWS
    nchunk = w_in.shape[1] // W_CHUNK
    half = nchunk // 2
    bsz = m // seq_len
    chunks = seq_len // tc
    npairs = m // (2 * tc)
    assert seq_len % (2 * tc) == 0 and tc % (SUBLANES * SUBLANES) == 0 and tc // SUBLANES > CONV_W
    pair = lambda i: jnp.clip(i - nchunk, 0, npairs - 1)
    last = npairs * 2 - 1
    seq_in = lambda i: (jnp.clip(2 * (i - nchunk), 0, last) // chunks, 0, 0)
    seq_out = lambda i: (jnp.clip(2 * (i - nchunk) - 1, 0, last) // chunks, 0, 0)
    state_in = lambda rows: pl.BlockSpec((1, rows, w), seq_in)
    state_out = lambda rows: pl.BlockSpec((1, rows, w), seq_out)
    chunk = lambda i: (0, jnp.minimum(i, nchunk - 1))
    return pl.pallas_call(
        functools.partial(_rglru_front_kernel, nchunk, npairs, chunks),
        grid=(nchunk + npairs + 1,),
        in_specs=[_resident(xs.shape), pl.BlockSpec((2 * tc, d), lambda i: (pair(i), 0)), _resident((1, d)),
                  pl.BlockSpec((d, W_CHUNK), chunk), state_in(CONV_W - 1), state_in(1),
                  _resident(s_conv_prev.shape), _resident(s_h0.shape),
                  _resident((CONV_W, w)), _resident((1, w)), _resident(w_r.shape), _resident((1, w)),
                  _resident(w_i.shape), _resident((1, w)), _resident((1, w))],
        out_specs=[
            pl.BlockSpec((ns, W_CHUNK), lambda i: (0, jnp.clip(i - half, 0, half - 1))),
            pl.BlockSpec((ns, w), lambda i: (0, 0)),
            pl.BlockSpec(s_conv_prev.shape, lambda i: (0, 0, 0)),
            pl.BlockSpec((tc, w), lambda i: (pair(i), 0)),
            pl.BlockSpec((tc, w), lambda i: (jnp.clip(i - nchunk - 1, 0, npairs - 1), 0)),
            state_out(CONV_W - 1), state_out(1),
        ],
        out_shape=[
            jax.ShapeDtypeStruct((ns, w), F32),
            jax.ShapeDtypeStruct((ns, w), F32),
            jax.ShapeDtypeStruct(s_conv_prev.shape, F32),
            jax.ShapeDtypeStruct((m // 2, w), BF16),
            jax.ShapeDtypeStruct((m // 2, w), BF16),
            jax.ShapeDtypeStruct((bsz, CONV_W - 1, w), F32),
            jax.ShapeDtypeStruct((bsz, 1, w), F32),
        ],
        scratch_shapes=[pltpu.VMEM((nchunk, d, W_CHUNK), BF16),
                        pltpu.VMEM(w_r.shape, BF16), pltpu.VMEM(w_i.shape, BF16),
                        pltpu.VMEM((half, ns, W_CHUNK), F32),
                        pltpu.VMEM((tc, 2 * w), F32), pltpu.VMEM((tc, 2 * w), F32),
                        pltpu.VMEM((1, w), F32), pltpu.VMEM((CONV_W - 1, w), F32)],
        compiler_params=_params("arbitrary"),
        name="rglru_front",
    )(xs, x, g.reshape(1, d), w_in, conv_prev, h0.reshape(bsz, 1, w), s_conv_prev, s_h0,
      conv_w, conv_b.reshape(1, w), w_r, b_r.reshape(1, w), w_i, b_i.reshape(1, w), lam.reshape(1, w))


def _buckets(dist):
    n = jnp.maximum(dist, 0)
    max_exact = N_BUCKETS // 2
    nf = jnp.maximum(n, 1).astype(F32)
    large = max_exact + jnp.floor(jnp.log(nf / max_exact) / math.log(MAX_DISTANCE / max_exact)
                                  * (N_BUCKETS - max_exact)).astype(jnp.int32)
    large = jnp.minimum(large, N_BUCKETS - 1)
    return jnp.where(n < max_exact, n, large)


def _lookup(bucket, valid, table_ref, head):
    bias = jnp.zeros(bucket.shape, F32)
    for b in range(N_BUCKETS):
        bias = jnp.where(bucket == b, table_ref[b, head], bias)
    return jnp.where(valid, bias, NEG_INF)


def _bias_kernel(table_ref, sinks_ref, band_ref, sinkt_ref, past_ref, new_ref):
    hk = pl.program_id(0)
    span = 3 * BLOCK
    dist = (lax.broadcasted_iota(jnp.int32, (1, span), 1) + BLOCK) % span
    bucket = _buckets(dist)
    in_window = (dist >= 0) & (dist < WINDOW)
    key_row = lax.broadcasted_iota(jnp.int32, (ATT_KEYS, HALF_Q), 0)

    def band(head):
        row = _lookup(bucket, in_window, table_ref, head) * LOG2_E
        full = pltpu.roll(jnp.broadcast_to(row, (ATT_KEYS, span)), 0, axis=1, stride=1, stride_axis=0)
        return full[:, :HALF_Q]

    rows = past_ref.shape[2]
    d_past = rows - lax.broadcasted_iota(jnp.int32, (1, rows), 1)
    b_past = _buckets(d_past)
    ok_past = (d_past >= 0) & (d_past < WINDOW)
    d_new = jnp.zeros((1, LANES), jnp.int32)
    b_new = _buckets(d_new)
    for g in range(GROUP):
        head = hk * GROUP + g
        bias = band(head)
        for half in range(2):
            slot = _head_order(half).index(g)
            cs = slice(slot * HALF_Q, (slot + 1) * HALF_Q)
            band_ref[0, half, 0, :, cs] = bias
            prev_rows = BLOCK - half * HALF_Q
            band_ref[1, half, 0, :, cs] = jnp.where(key_row < prev_rows, NEG_INF, bias)
            sinkt_ref[half, 0, :, cs] = jnp.full((1, HALF_Q), sinks_ref[head] * LOG2_E, F32)
        past_ref[0, g:g + 1, :] = _lookup(b_past, ok_past, table_ref, head)
        new_ref[0, g:g + 1, :] = _lookup(b_new, d_new == 0, table_ref, head)


def bias_tables(table, sinks, past_rows):
    smem = pl.BlockSpec(memory_space=pltpu.SMEM)
    return pl.pallas_call(
        _bias_kernel,
        grid=(N_KV_HEADS,),
        in_specs=[smem, smem],
        out_specs=[
            pl.BlockSpec((2, 2, 1, ATT_KEYS, GROUP * HALF_Q), lambda h: (0, 0, h, 0, 0)),
            pl.BlockSpec((2, 1, 1, GROUP * HALF_Q), lambda h: (0, h, 0, 0)),
            pl.BlockSpec((1, GROUP, past_rows), lambda h: (h, 0, 0)),
            pl.BlockSpec((1, GROUP, LANES), lambda h: (h, 0, 0)),
        ],
        out_shape=[
            jax.ShapeDtypeStruct((2, 2, N_KV_HEADS, ATT_KEYS, GROUP * HALF_Q), F32),
            jax.ShapeDtypeStruct((2, N_KV_HEADS, 1, GROUP * HALF_Q), F32),
            jax.ShapeDtypeStruct((N_KV_HEADS, GROUP, past_rows), F32),
            jax.ShapeDtypeStruct((N_KV_HEADS, GROUP, LANES), F32),
        ],
        compiler_params=_params("parallel"),
        name="bias_tables",
    )(table, sinks)


def _band_attn_rounds(first_tile, q_ref, kp_ref, kc_ref, vp_ref, vc_ref, gate_ref, bias_ref, sink_ref,
                      y_ref, s_scr, p_scr):
    nt = (((1,), (1,)), ((), ()))
    low = (lax.broadcasted_iota(jnp.int32, (1, LANES), 1) < HEAD_DIM)
    keep_low = low.astype(BF16)
    keep_high = 1 - keep_low
    keep = (keep_low, keep_high)
    nkeys = 2 * BLOCK
    ones_rows = jnp.where(lax.broadcasted_iota(jnp.int32, (2 * SUBLANES, nkeys), 0) == 0,
                          1.0, 0.0).astype(BF16)
    rows = ATT_ROWS
    nslot = s_scr.shape[0]
    items = [(blk, hk, half) for blk in range(q_ref.shape[0] // BLOCK)
             for hk in range(N_KV_HEADS) for half in range(2)]
    assert nslot % 2 == 0

    def rows_of(blk):
        return slice(blk * BLOCK, (blk + 1) * BLOCK)

    def key_rows(half):
        return slice(half * HALF_Q, half * HALF_Q + ATT_KEYS)

    for slot in range(nslot):
        dead = slice(ATT_KEYS, nkeys) if slot % 2 == 0 else slice(0, HALF_Q)
        p_scr[slot, dead, :] = jnp.zeros((HALF_Q, p_scr.shape[2]), BF16)

    def scores(idx):
        blk, hk, half = items[idx]
        variant = first_tile if blk == 0 else 0
        cs = slice(hk * LANES, (hk + 1) * LANES)
        k_prev = kp_ref[:, cs] if blk == 0 else kc_ref[rows_of(blk - 1), cs]
        k_cur = kc_ref[rows_of(blk), cs]
        kd = (jnp.concatenate([k_prev, k_cur[:HALF_Q]], axis=0) if half == 0
              else jnp.concatenate([k_prev[HALF_Q:], k_cur], axis=0))
        q0 = blk * BLOCK + half * HALF_Q
        qs = jnp.concatenate(
            [q_ref[q0:q0 + HALF_Q, (hk * SLABS + h // HEADS_PER_TILE) * LANES:
                   (hk * SLABS + h // HEADS_PER_TILE + 1) * LANES] * keep[h % HEADS_PER_TILE]
             for h in _head_order(half)], axis=0)
        s = lax.dot_general(kd, qs, nt, preferred_element_type=F32) + bias_ref[variant, half, hk]
        s_scr[idx % nslot, key_rows(half), :] = s
        return jnp.maximum(jnp.max(s, axis=0, keepdims=True), sink_ref[half, hk])

    def softmax(idx, m):
        blk, hk, half = items[idx]
        slot = idx % nslot
        lo = half * HALF_Q
        for r in range(lo, lo + ATT_KEYS, rows):
            p_scr[slot, r:r + rows, :] = jnp.exp2(s_scr[slot, r:r + rows, :] - m).astype(BF16)
        return jnp.exp2(sink_ref[half, hk] - m)

    def weighted_values(idx):
        blk, hk, half = items[idx]
        vs = slice(hk * LANES, hk * LANES + HEAD_DIM)
        v_prev = vp_ref[vs, :] if blk == 0 else vc_ref[vs, rows_of(blk - 1)]
        vt = jnp.concatenate([v_prev, vc_ref[vs, rows_of(blk)]], axis=1)
        lhs_v = jnp.concatenate([vt, ones_rows], axis=0)
        return jnp.dot(lhs_v, p_scr[idx % nslot], preferred_element_type=F32)

    low_q = lax.broadcasted_iota(jnp.int32, (HEAD_DIM, LANES), 1) < HALF_Q

    def finish(blk, hk, ots, sink_ws):
        o = [ots[half][:HEAD_DIM] * (1.0 / (ots[half][HEAD_DIM:HEAD_DIM + 1] + sink_ws[half]))
             for half in range(2)]
        for sl in range(SLABS):
            a, b = (oh[:, sl * LANES:(sl + 1) * LANES] for oh in o)
            even = jnp.where(low_q, a, b)
            odd = pltpu.roll(jnp.where(low_q, b, a), HALF_Q, axis=1)
            pair = jnp.concatenate([even, odd], axis=0)
            c0 = (hk * SLABS + sl) * LANES
            y_ref[rows_of(blk), c0:c0 + LANES] = (
                pair.T * _silu(gate_ref[rows_of(blk), c0:c0 + LANES])).astype(y_ref.dtype)

    n = len(items)
    offs, sink_ws, outs = {}, {}, {}
    for k in range(-2 * ATT_SKEW, n + ATT_SKEW):
        if 0 <= k + 2 * ATT_SKEW < n:
            offs[k + 2 * ATT_SKEW] = scores(k + 2 * ATT_SKEW)
        if 0 <= k + ATT_SKEW < n:
            sink_ws[k + ATT_SKEW] = softmax(k + ATT_SKEW, offs.pop(k + ATT_SKEW))
        if 0 <= k < n:
            outs[k] = weighted_values(k)
        j = k - ATT_SKEW
        if 0 <= j < n and items[j][2] == 1:
            finish(items[j][0], items[j][1], [outs.pop(j - 1), outs.pop(j)],
                   [sink_ws.pop(j - 1), sink_ws.pop(j)])
        yield


def _band_attn_kernel(q_ref, kp_ref, kc_ref, vp_ref, vc_ref, gate_ref, bias_ref, sink_ref, y_ref,
                      s_scr, p_scr):
    first_tile = (pl.program_id(1) == 0).astype(jnp.int32)
    for _ in _band_attn_rounds(first_tile, q_ref, kp_ref, kc_ref, vp_ref, vc_ref, gate_ref, bias_ref,
                               sink_ref, y_ref, s_scr, p_scr):
        pass


def band_attention(q, kdup, vt, gate, bias_band, sink_t, bsz, t):
    m = q.shape[0]
    nblk = t // BLOCK
    per_tile = ATT_TILE // BLOCK
    ntile = t // ATT_TILE
    assert t % ATT_TILE == 0
    before = lambda b, i: b * nblk + jnp.maximum(per_tile * i - 1, 0)
    cur = lambda n: pl.BlockSpec((ATT_TILE, n), lambda b, i: (b * ntile + i, 0))
    prev = lambda n: pl.BlockSpec((BLOCK, n), lambda b, i: (before(b, i), 0))
    cur_t = pl.BlockSpec((2 * KV_WIDTH, ATT_TILE), lambda b, i: (0, b * ntile + i))
    prev_t = pl.BlockSpec((2 * KV_WIDTH, BLOCK), lambda b, i: (0, before(b, i)))
    score_tile = (2 * BLOCK, GROUP * HALF_Q)
    return pl.pallas_call(
        _band_attn_kernel,
        grid=(bsz, ntile),
        in_specs=[
            cur(ATT_WIDTH), prev(2 * KV_WIDTH), cur(2 * KV_WIDTH), prev_t, cur_t,
            cur(ATT_WIDTH), _resident(bias_band.shape), _resident(sink_t.shape),
        ],
        out_specs=cur(ATT_WIDTH),
        out_shape=jax.ShapeDtypeStruct((m, ATT_WIDTH), BF16),
        scratch_shapes=[pltpu.VMEM((ATT_SLOTS,) + score_tile, F32),
                        pltpu.VMEM((ATT_SLOTS,) + score_tile, BF16)],
        compiler_params=_params("parallel", "parallel"),
        name="band_attention",
    )(q, kdup, kdup, vt, vt, gate, bias_band, sink_t)


def _proj_tile(y_scr, w_scr, g_ref, x_ref, o_ref, raw_scr):
    y = y_scr[...]
    for c in range(w_scr.shape[0]):
        for n0 in range(0, W_CHUNK, MXU_COLS):
            raw_scr[:, c * W_CHUNK + n0:c * W_CHUNK + n0 + MXU_COLS] = jnp.dot(
                y, w_scr[c, :, n0:n0 + MXU_COLS], preferred_element_type=F32)
            yield
    step = y_scr.shape[0] // PROJ_TAIL_PIECES
    for r0 in range(0, y_scr.shape[0], step):
        o = raw_scr[r0:r0 + step, :]
        o_ref[r0:r0 + step, :] = x_ref[r0:r0 + step, :] + o * _rms_scale(o) * g_ref[...]
        yield


def _attn_proj_kernel(nchunk, ntiles, tiles_per_seq, as_ref, gs_ref, xs_ref,
                      q_ref, kp_ref, kc_ref, vp_ref, vc_ref, gate_ref, bias_ref, sink_ref,
                      w_ref, g_ref, x_ref, os_ref, o_ref,
                      w_scr, raws_scr, s_scr, p_scr, ynew_scr, yold_scr, raw_scr):
    i = pl.program_id(0)
    p = i - nchunk

    @pl.when(i < nchunk)
    def _():
        wb = w_ref[...].astype(BF16)
        w_scr[i] = wb
        ys = (as_ref[...] * _silu(gs_ref[...])).astype(BF16)
        raws_scr[i] = jnp.dot(ys, wb, preferred_element_type=F32)

    @pl.when(i == nchunk - 1)
    def _():
        o = jnp.concatenate([raws_scr[c] for c in range(nchunk)], axis=1)
        os_ref[...] = xs_ref[...] + o * _rms_scale(o) * g_ref[...]
        yold_scr[...] = jnp.zeros_like(yold_scr)

    @pl.when(p >= 0)
    def _():
        tile = jnp.minimum(p, ntiles - 1)
        first = (tile % tiles_per_seq == 0).astype(jnp.int32)
        _interleave((_band_attn_rounds(first, q_ref, kp_ref, kc_ref, vp_ref, vc_ref, gate_ref, bias_ref,
                                       sink_ref, ynew_scr, s_scr, p_scr), ATT_ROUNDS_PER_PROJ_PIECE),
                    (_proj_tile(yold_scr, w_scr, g_ref, x_ref, o_ref, raw_scr), 1))
        yold_scr[...] = ynew_scr[...]


def attn_proj(a_s, gate_s, x_s, q, kdup, vt, gate, bias_band, sink_t, w, g, x, seq_len):
    k, d = w.shape
    m = x.shape[0]
    tm = ATT_PROJ_TILE
    nchunk = d // W_CHUNK
    ntiles = m // tm
    tiles_per_seq = seq_len // tm
    per_tile = tm // BLOCK
    assert seq_len % tm == 0 and tm % BLOCK == 0
    att_tile = lambda i: jnp.clip(i - nchunk, 0, ntiles - 1)
    proj_tile = lambda i: jnp.clip(i - nchunk - 1, 0, ntiles - 1)

    def before(i):
        t = att_tile(i)
        return per_tile * t - jnp.where(t % tiles_per_seq == 0, 0, 1)

    rows = lambda n: pl.BlockSpec((tm, n), lambda i: (att_tile(i), 0))
    score_tile = (2 * BLOCK, GROUP * HALF_Q)
    return pl.pallas_call(
        functools.partial(_attn_proj_kernel, nchunk, ntiles, tiles_per_seq),
        grid=(nchunk + ntiles + 1,),
        in_specs=[
            _resident(a_s.shape), _resident(gate_s.shape), _resident(x_s.shape),
            rows(ATT_WIDTH), pl.BlockSpec((BLOCK, 2 * KV_WIDTH), lambda i: (before(i), 0)),
            rows(2 * KV_WIDTH), pl.BlockSpec((2 * KV_WIDTH, BLOCK), lambda i: (0, before(i))),
            pl.BlockSpec((2 * KV_WIDTH, tm), lambda i: (0, att_tile(i))), rows(ATT_WIDTH),
            _resident(bias_band.shape), _resident(sink_t.shape),
            pl.BlockSpec((k, W_CHUNK), lambda i: (0, jnp.minimum(i, nchunk - 1))), _resident((1, d)),
            pl.BlockSpec((tm, d), lambda i: (proj_tile(i), 0)),
        ],
        out_specs=[pl.BlockSpec(x_s.shape, lambda i: (0, 0)),
                   pl.BlockSpec((tm, d), lambda i: (proj_tile(i), 0))],
        out_shape=[jax.ShapeDtypeStruct(x_s.shape, F32), jax.ShapeDtypeStruct((m, d), F32)],
        scratch_shapes=[pltpu.VMEM((nchunk, k, W_CHUNK), BF16),
                        pltpu.VMEM((nchunk, x_s.shape[0], W_CHUNK), F32),
                        pltpu.VMEM((ATT_SLOTS,) + score_tile, F32),
                        pltpu.VMEM((ATT_SLOTS,) + score_tile, BF16),
                        pltpu.VMEM((tm, k), BF16), pltpu.VMEM((tm, k), BF16),
                        pltpu.VMEM((tm, d), F32)],
        compiler_params=_params("arbitrary"),
        name="attn_proj",
    )(a_s, gate_s, x_s, q, kdup, kdup, vt, vt, gate, bias_band, sink_t, w, g.reshape(1, d), x)


def _cached_attn_kernel(q_ref, ckt_ref, cvt_ref, kn_ref, vn_ref, sinks_ref, bpast_ref, bnew_ref, o_ref):
    shape = (N_Q_HEADS, KV_WIDTH)
    lane_kv = lax.broadcasted_iota(jnp.int32, shape, 1) // HEAD_DIM
    row_kv = lax.broadcasted_iota(jnp.int32, shape, 0) // GROUP
    own = lane_kv == row_kv
    sink = sinks_ref[...]
    nt = (((1,), (1,)), ((), ()))
    for b in range(q_ref.shape[0]):
        q = q_ref[b]
        qt = jnp.concatenate([q] * N_KV_HEADS, axis=1)
        qm = jnp.where(own, qt, 0.0).astype(BF16)
        knew = kn_ref[b].astype(BF16).astype(F32)
        vnew = vn_ref[b].astype(BF16).astype(F32)
        s = jnp.dot(qm, ckt_ref[b].astype(BF16), preferred_element_type=F32) + bpast_ref[...]
        s_new = jnp.sum(qm.astype(F32) * knew, axis=-1, keepdims=True) + bnew_ref[:, :1]
        m = jnp.maximum(jnp.maximum(jnp.max(s, axis=-1, keepdims=True), s_new), sink)
        p = jnp.exp(s - m)
        p_new = jnp.exp(s_new - m)
        denom = jnp.sum(p, axis=-1, keepdims=True) + p_new + jnp.exp(sink - m)
        o_all = (lax.dot_general(p.astype(BF16), cvt_ref[b].astype(BF16), nt, preferred_element_type=F32)
                 + p_new.astype(BF16).astype(F32) * vnew)
        o_all = jnp.where(own, o_all, 0.0)
        o = o_all[:, :HEAD_DIM]
        for hk in range(1, N_KV_HEADS):
            o = o + o_all[:, hk * HEAD_DIM:(hk + 1) * HEAD_DIM]
        o_ref[b] = o / denom


def cached_attention(q, cache_kt, cache_vt, k_new, v_new, sinks, bias_past, bias_new):
    bsz, _, rows = cache_kt.shape
    nseq = math.gcd(bsz, SEQS_PER_STEP)
    per_seq = lambda r, n: pl.BlockSpec((nseq, r, n), lambda b: (b, 0, 0))
    return pl.pallas_call(
        _cached_attn_kernel,
        grid=(bsz // nseq,),
        in_specs=[
            per_seq(N_Q_HEADS, HEAD_DIM), per_seq(KV_WIDTH, rows), per_seq(KV_WIDTH, rows),
            per_seq(1, KV_WIDTH), per_seq(1, KV_WIDTH),
            _resident((N_Q_HEADS, 1)), _resident((N_Q_HEADS, rows)), _resident((N_Q_HEADS, LANES)),
        ],
        out_specs=per_seq(N_Q_HEADS, HEAD_DIM),
        out_shape=jax.ShapeDtypeStruct((bsz, N_Q_HEADS, HEAD_DIM), F32),
        compiler_params=_params("parallel"),
        name="cached_attention",
    )(q, cache_kt, cache_vt, k_new, v_new, sinks, bias_past, bias_new)


def kernel(x_prompt, x_sample, state_conv, state_h, cache_k, cache_v, a_norm_pre, a_norm_post,
           a_w_in, a_conv_w, a_conv_b, a_w_r, a_b_r, a_w_i, a_b_i, a_lambda, a_w_out, kv_norm, w_kv,
           b_norm_pre, b_norm_post, b_w_qg, b_sinks, b_w_out, rel_bias_table):
    bsz, t, d = x_prompt.shape
    dbsz, dt, _ = x_sample.shape
    assert a_w_in.shape[0] == 1 and b_w_qg.shape[0] == 1 and dt == 1
    assert t % BLOCK == 0 and t >= WINDOW
    past_rows = cache_k.shape[1]
    assert past_rows == min(WINDOW, PAST_LEN)

    sinks = b_sinks[0]
    bias_band, sink_t, bias_past, bias_new = bias_tables(rel_bias_table, sinks, past_rows)

    tm = 2 * SUB_ROWS
    xp = x_prompt.reshape(bsz * t, d)
    xs = x_sample.reshape(dbsz, d)

    conv0 = jnp.zeros((bsz, CONV_W - 1, LRU_WIDTH), F32)
    h0 = jnp.zeros((bsz, LRU_WIDTH), F32)
    gate_s, hs, s_conv_t, y_even, y_odd, p_conv, p_h = rglru_front(
        xs, jnp.transpose(state_conv[0], (1, 0, 2)), state_h[0], xp, conv0, h0, a_norm_pre[0], a_w_in[0],
        a_conv_w[0], a_conv_b[0], a_w_r[0], a_b_r[0], a_w_i[0], a_b_i[0], a_lambda[0], seq_len=t)
    xs1, x1 = proj_norm_res(hs, gate_s, xs, (y_even, y_odd), a_w_out[0], a_norm_post[0], xp, tm)

    ks, vs, qs, gate_sb, q, gate_b, kdup, vt, k_tail, v_tail = norm_proj_kvq(
        xs1, x1, kv_norm, b_norm_pre[0], w_kv, b_w_qg[0], SUB_ROWS, seq_len=t)
    cache_kt = jnp.transpose(cache_k, (0, 2, 3, 1)).reshape(dbsz, KV_WIDTH, past_rows)
    cache_vt = jnp.transpose(cache_v, (0, 2, 3, 1)).reshape(dbsz, KV_WIDTH, past_rows)
    os_ = cached_attention(qs.reshape(dbsz, N_Q_HEADS, HEAD_DIM), cache_kt, cache_vt,
                           ks.reshape(dbsz, 1, KV_WIDTH), vs.reshape(dbsz, 1, KV_WIDTH),
                           sinks.reshape(N_Q_HEADS, 1), bias_past.reshape(N_Q_HEADS, past_rows),
                           bias_new.reshape(N_Q_HEADS, LANES))
    y_sample, y_prompt = attn_proj(os_.reshape(dbsz, ATT_WIDTH), gate_sb, xs1, q, kdup, vt, gate_b,
                                   bias_band, sink_t, b_w_out[0], b_norm_post[0], x1, seq_len=t)
    y_prompt = y_prompt.reshape(bsz, t, d)
    p_k = jnp.transpose(k_tail.reshape(bsz, N_KV_HEADS, HEAD_DIM, WINDOW), (0, 3, 1, 2))
    p_v = jnp.transpose(v_tail.reshape(bsz, N_KV_HEADS, HEAD_DIM, WINDOW), (0, 3, 1, 2))

    return (y_prompt, y_sample.reshape(dbsz, 1, d),
            p_conv[None], p_h.reshape(1, bsz, LRU_WIDTH), p_k, p_v,
            jnp.transpose(s_conv_t, (1, 0, 2))[None], hs[None],
            ks.reshape(dbsz, 1, N_KV_HEADS, HEAD_DIM), vs.reshape(dbsz, 1, N_KV_HEADS, HEAD_DIM))
```

```python
import functools
import math

import jax
import jax.numpy as jnp
from jax import lax
from jax.experimental import pallas as pl
from jax.experimental.pallas import tpu as pltpu

F32 = jnp.float32
BF16 = jnp.bfloat16

D_MODEL = 2048
LRU_WIDTH = 2048
LRU_BLOCKS = 8
LRU_BLOCK_W = LRU_WIDTH // LRU_BLOCKS
CONV_W = 4
LRU_C = 8.0
HEAD_DIM = 64
N_Q_HEADS = 32
N_KV_HEADS = 8
GROUP = N_Q_HEADS // N_KV_HEADS
ATT_WIDTH = N_Q_HEADS * HEAD_DIM
KV_WIDTH = N_KV_HEADS * HEAD_DIM
WINDOW = 128
BLOCK = WINDOW
N_BUCKETS = 32
MAX_DISTANCE = 128
RMS_EPS = 1e-6
NEG_INF = -1e30
LOG2_E = 1.4426950408889634
PAST_LEN = 16384

V7X_VMEM_BYTES = 64 * 1024 * 1024
VMEM_LIMIT = V7X_VMEM_BYTES - 8 * 1024 * 1024
SUBLANES = 8
LANES = 128
HEADS_PER_TILE = LANES // HEAD_DIM
SLABS = GROUP // HEADS_PER_TILE
MXU_COLS = 256
SUB_ROWS = 256
W_CHUNK = 512
ATT_TILE = 4 * BLOCK
ATT_ROWS = 64
ATT_SKEW = 2
ATT_SLOTS = 2 * ATT_SKEW + 2
HALF_Q = BLOCK // 2
ATT_KEYS = WINDOW + HALF_Q
SEQS_PER_STEP = 8
ATT_PROJ_TILE = 2 * BLOCK
PROJ_TAIL_PIECES = 4
ATT_ROUNDS_PER_PROJ_PIECE = 4
SQRT_FLOOR = 1e-30


def _head_order(half):
    heads = list(range(GROUP))
    return heads if half == 0 else [h ^ 1 for h in heads]


def _params(*semantics):
    return pltpu.CompilerParams(dimension_semantics=semantics, vmem_limit_bytes=VMEM_LIMIT)


def _resident(shape):
    zeros = (0,) * len(shape)
    return pl.BlockSpec(shape, lambda *_: zeros, pipeline_mode=pl.Buffered(1))


def _rms_scale(x):
    return lax.rsqrt(jnp.mean(x * x, axis=-1, keepdims=True) + RMS_EPS)


def _silu(x):
    h = 0.5 * x
    return h * jnp.tanh(h) + h


def _segment_major(rows, inverse=False):
    seg = rows // SUBLANES
    r = lax.broadcasted_iota(jnp.int32, (rows, rows), 0)
    c = lax.broadcasted_iota(jnp.int32, (rows, rows), 1)
    if inverse:
        src = (r % seg) * SUBLANES + r // seg
    else:
        src = (r % SUBLANES) * seg + r // SUBLANES
    return jnp.where(c == src, 1.0, 0.0).astype(BF16)


def _phase_specs(nchunk, tm, k):
    chunk_w = pl.BlockSpec((k, W_CHUNK), lambda i: (0, jnp.minimum(i, nchunk - 1)))
    tile = lambda n: pl.BlockSpec((tm, n), lambda i: (jnp.maximum(i - nchunk, 0), 0))
    return chunk_w, tile


def _dup_heads(x):
    low = lax.broadcasted_iota(jnp.int32, (x.shape[0], LANES), 1) < HEAD_DIM
    out = []
    for c in range(x.shape[1] // LANES):
        col = x[:, c * LANES:(c + 1) * LANES]
        swapped = pltpu.roll(col, HEAD_DIM, axis=1)
        out += [jnp.where(low, col, swapped), jnp.where(low, swapped, col)]
    return jnp.concatenate(out, axis=1)


def _norm_proj_kvq_kernel(nkv, nqg, xs_ref, x_ref, gkv_ref, gq_ref, wkv_ref, wqg_ref,
                          ks_ref, vs_ref, qs_ref, gates_ref,
                          q_ref, gate_ref, kdup_ref, vt_ref, ktail_ref, vtail_ref,
                          wkv_scr, wqg_scr):
    i = pl.program_id(0)
    nchunk = nkv + nqg
    q_chunks = ATT_WIDTH // W_CHUNK
    q_scale = 1.0 / math.sqrt(HEAD_DIM)
    q_scale_log2 = q_scale * LOG2_E

    def sample_rows(g_ref):
        xs = xs_ref[...]
        return (xs * _rms_scale(xs) * g_ref[...]).astype(BF16)

    @pl.when(i < nkv)
    def _():
        wb = wkv_ref[...].astype(BF16)
        wkv_scr[i] = wb
        r = jnp.dot(sample_rows(gkv_ref), wb, preferred_element_type=F32)

        @pl.when(i == 0)
        def _():
            ks_ref[...] = r

        @pl.when(i == 1)
        def _():
            vs_ref[...] = r

    @pl.when((i >= nkv) & (i < nchunk))
    def _():
        c = i - nkv
        wb = wqg_ref[...].astype(BF16)
        wqg_scr[c] = wb
        r = jnp.dot(sample_rows(gq_ref), wb, preferred_element_type=F32)

        @pl.when(c < q_chunks)
        def _():
            qs_ref[...] = r * q_scale

        @pl.when(c >= q_chunks)
        def _():
            gates_ref[...] = r

    @pl.when(i >= nchunk)
    def _():
        tm = x_ref.shape[0]
        for rs in _row_blocks(tm):
            x = x_ref[rs, :]
            xh = x * _rms_scale(x)
            xkv = (xh * gkv_ref[...]).astype(BF16)
            xq = (xh * gq_ref[...]).astype(BF16)
            k = jnp.dot(xkv, wkv_scr[0], preferred_element_type=F32)
            v = jnp.dot(xkv, wkv_scr[1], preferred_element_type=F32)
            kdup_ref[rs, :] = _dup_heads(k).astype(BF16)
            vt_ref[:, rs] = _dup_heads(v).T.astype(BF16)
            for c in range(nqg):
                r = jnp.dot(xq, wqg_scr[c], preferred_element_type=F32)
                if c < q_chunks:
                    q_ref[rs, c * W_CHUNK:(c + 1) * W_CHUNK] = (r * q_scale_log2).astype(q_ref.dtype)
                else:
                    cc = c - q_chunks
                    gate_ref[rs, cc * W_CHUNK:(cc + 1) * W_CHUNK] = r
        ktail_ref[0] = k[k.shape[0] - WINDOW:].T
        vtail_ref[0] = v[v.shape[0] - WINDOW:].T


def norm_proj_kvq(xs, x, g_kv, g_q, w_kv, w_qg, tm, seq_len):
    m, d = x.shape
    ns = xs.shape[0]
    assert w_kv.shape[1] == 2 * KV_WIDTH == 2 * W_CHUNK and seq_len % tm == 0 and tm >= WINDOW
    nkv, nqg = w_kv.shape[1] // W_CHUNK, w_qg.shape[1] // W_CHUNK
    nchunk = nkv + nqg
    tiles = seq_len // tm
    tile = lambda n: pl.BlockSpec((tm, n), lambda i: (jnp.maximum(i - nchunk, 0), 0))
    tail = pl.BlockSpec((1, KV_WIDTH, WINDOW), lambda i: (jnp.maximum(i - nchunk, 0) // tiles, 0, 0))
    kv_chunk = lambda i: (0, jnp.minimum(i, nkv - 1))
    qg_chunk = lambda i: (0, jnp.clip(i - nkv, 0, nqg - 1))
    q_chunks = ATT_WIDTH // W_CHUNK
    whole_s = lambda n: pl.BlockSpec((ns, n), lambda i: (0, 0))
    return pl.pallas_call(
        functools.partial(_norm_proj_kvq_kernel, nkv, nqg),
        grid=(nchunk + m // tm,),
        in_specs=[
            _resident(xs.shape), tile(d), _resident((1, d)), _resident((1, d)),
            pl.BlockSpec((d, W_CHUNK), kv_chunk), pl.BlockSpec((d, W_CHUNK), qg_chunk),
        ],
        out_specs=[
            whole_s(KV_WIDTH), whole_s(KV_WIDTH),
            pl.BlockSpec((ns, W_CHUNK), lambda i: (0, jnp.clip(i - nkv, 0, q_chunks - 1))),
            pl.BlockSpec((ns, W_CHUNK), lambda i: (0, jnp.clip(i - nkv - q_chunks, 0, nqg - q_chunks - 1))),
            tile(ATT_WIDTH), tile(ATT_WIDTH), tile(2 * KV_WIDTH),
            pl.BlockSpec((2 * KV_WIDTH, tm), lambda i: (0, jnp.maximum(i - nchunk, 0))), tail, tail,
        ],
        out_shape=[
            jax.ShapeDtypeStruct((ns, KV_WIDTH), F32),
            jax.ShapeDtypeStruct((ns, KV_WIDTH), F32),
            jax.ShapeDtypeStruct((ns, ATT_WIDTH), F32),
            jax.ShapeDtypeStruct((ns, w_qg.shape[1] - ATT_WIDTH), F32),
            jax.ShapeDtypeStruct((m, ATT_WIDTH), BF16),
            jax.ShapeDtypeStruct((m, ATT_WIDTH), F32),
            jax.ShapeDtypeStruct((m, 2 * KV_WIDTH), BF16),
            jax.ShapeDtypeStruct((2 * KV_WIDTH, m), BF16),
            jax.ShapeDtypeStruct((m // seq_len, KV_WIDTH, WINDOW), F32),
            jax.ShapeDtypeStruct((m // seq_len, KV_WIDTH, WINDOW), F32),
        ],
        scratch_shapes=[pltpu.VMEM((nkv, d, W_CHUNK), BF16), pltpu.VMEM((nqg, d, W_CHUNK), BF16)],
        compiler_params=_params("arbitrary"),
        name="norm_proj_kvq",
    )(xs, x, g_kv.reshape(1, d), g_q.reshape(1, d), w_kv, w_qg)


def _row_blocks(rows):
    sub = min(rows, SUB_ROWS)
    return [slice(r, r + sub) for r in range(0, rows, sub)]


def _proj_norm_res_kernel(nchunk, nparts, as_ref, gs_ref, xs_ref, *refs):
    y_refs = refs[:nparts]
    w_ref, g_ref, x_ref, os_ref, o_ref, w_scr, raw_scr = refs[nparts:]
    i = pl.program_id(0)

    @pl.when(i < nchunk)
    def _():
        wb = w_ref[...].astype(BF16)
        w_scr[i] = wb
        ys = (as_ref[...] * _silu(gs_ref[...])).astype(BF16)
        raw_scr[i] = jnp.dot(ys, wb, preferred_element_type=F32)

    @pl.when(i == nchunk - 1)
    def _():
        o = jnp.concatenate([raw_scr[c] for c in range(nchunk)], axis=1)
        os_ref[...] = xs_ref[...] + o * _rms_scale(o) * g_ref[...]

    @pl.when(i >= nchunk)
    def _():
        tm = x_ref.shape[0]
        part_rows = tm // nparts

        for rs in _row_blocks(tm):
            part, off = divmod(rs.start, part_rows)
            y = y_refs[part][off:off + rs.stop - rs.start, :]
            o = jnp.concatenate([jnp.dot(y, w_scr[c], preferred_element_type=F32)
                                 for c in range(nchunk)], axis=1)
            o_ref[rs, :] = x_ref[rs, :] + o * _rms_scale(o) * g_ref[...]


def proj_norm_res(a_s, gate_s, x_s, y_parts, w, g, x, tm):
    k, d = w.shape
    m = x.shape[0]
    nchunk = d // W_CHUNK
    nparts = len(y_parts)
    assert (tm // nparts) % min(tm, SUB_ROWS) == 0
    chunk_w, tile = _phase_specs(nchunk, tm, k)
    part = pl.BlockSpec((tm // nparts, k), lambda i: (jnp.maximum(i - nchunk, 0), 0))
    return pl.pallas_call(
        functools.partial(_proj_norm_res_kernel, nchunk, nparts),
        grid=(nchunk + m // tm,),
        in_specs=[_resident(a_s.shape), _resident(gate_s.shape), _resident(x_s.shape)]
        + [part] * nparts + [chunk_w, _resident((1, d)), tile(d)],
        out_specs=[pl.BlockSpec(x_s.shape, lambda i: (0, 0)), tile(d)],
        out_shape=[jax.ShapeDtypeStruct(x_s.shape, F32), jax.ShapeDtypeStruct((m, d), F32)],
        scratch_shapes=[pltpu.VMEM((nchunk, k, W_CHUNK), BF16),
                        pltpu.VMEM((nchunk, x_s.shape[0], W_CHUNK), F32)],
        compiler_params=_params("arbitrary"),
        name="proj_norm_res",
    )(a_s, gate_s, x_s, *y_parts, w, g.reshape(1, d), x)


def _lru_gate_dots(conv, wr_half, wi_half):
    cb = conv.astype(BF16)
    return (jnp.dot(cb, wr_half, preferred_element_type=F32),
            jnp.dot(cb, wi_half, preferred_element_type=F32))


def _lru_gates(conv, wr_half, br, wi_half, bi, lam):
    return _lru_gate_math(conv, _lru_gate_dots(conv, wr_half, wi_half), br, bi, lam)


def _lru_gate_math(conv, half_pre, br, bi, lam):
    th_r = jnp.tanh(half_pre[0] + 0.5 * br)
    th_i = jnp.tanh(half_pre[1] + 0.5 * bi)
    nl = -lam
    softplus = jnp.maximum(nl, 0.0) + jnp.log1p(jnp.exp(-jnp.abs(nl)))
    half = (0.5 * LRU_C) * softplus
    x = th_r * half + half
    a = jnp.exp2(x * -LOG2_E)
    z = jnp.tanh(x) * (a * a + 1.0)
    mult = z * lax.rsqrt(jnp.maximum(z, SQRT_FLOOR))
    hc = 0.5 * conv
    return a, mult * (hc * th_i + hc)


def _interleave(*stages):
    live = [[stage, share] for stage, share in stages]
    while live:
        for entry in list(live):
            try:
                for _ in range(entry[1]):
                    next(entry[0])
            except StopIteration:
                live.remove(entry)


def _in_proj_tile(x_ref, rs, g_ref, w_scr, ug_ref):
    x = x_ref[rs, :]
    xn = (x * _rms_scale(x) * g_ref[...]).astype(BF16)
    xn = jnp.dot(_segment_major(xn.shape[0]), xn, preferred_element_type=F32).astype(BF16)
    for c in range(w_scr.shape[0]):
        for n0 in range(0, W_CHUNK, MXU_COLS):
            ug_ref[:, c * W_CHUNK + n0:c * W_CHUNK + n0 + MXU_COLS] = jnp.dot(
                xn, w_scr[c, :, n0:n0 + MXU_COLS], preferred_element_type=F32)
            yield


def _rglru_tile(ug_ref, y_ref, cw_ref, cb_ref, wr_ref, br_ref, wi_ref, bi_ref, lam_ref, h_scr, tail_scr):
    tc = ug_ref.shape[0]
    seg = tc // SUBLANES
    ntaps = CONV_W - 1
    bw = LRU_BLOCK_W
    sub = lax.broadcasted_iota(jnp.int32, (SUBLANES, bw), 0)
    first = sub == 0
    time_order = _segment_major(tc, inverse=True)

    def shift_in(x, row0):
        return jnp.where(first, row0, pltpu.roll(x, 1, axis=0))

    def group(x, j):
        return x[j * SUBLANES:(j + 1) * SUBLANES]

    def store_time_order(cols, y):
        y_ref[:, cols] = jnp.dot(time_order, y, preferred_element_type=F32).astype(y_ref.dtype)

    pending = None
    for n in range(LRU_BLOCKS):
        cs = slice(n * bw, (n + 1) * bw)
        u = ug_ref[:, cs]
        tail = tail_scr[:, cs]
        before = [shift_in(group(u, seg - m), tail[ntaps - m:ntaps - m + 1])
                  for m in range(ntaps, 0, -1)]
        ext = jnp.concatenate(before + [u], axis=0)
        tail_scr[:, cs] = jnp.concatenate(
            [group(u, seg - m)[SUBLANES - 1:] for m in range(ntaps, 0, -1)], axis=0)
        cw = cw_ref[:, cs]
        conv = cb_ref[:, cs]
        for tap in range(CONV_W):
            conv = conv + ext[tap * SUBLANES:tap * SUBLANES + tc] * cw[tap:tap + 1]
        yield

        half_pre = _lru_gate_dots(conv, wr_ref[n], wi_ref[n])
        yield

        if pending is not None:
            store_time_order(*pending)
        yield

        a, b = _lru_gate_math(conv, half_pre, br_ref[:, cs], bi_ref[:, cs], lam_ref[:, cs])

        h = b[:SUBLANES]
        acc = a[:SUBLANES]
        h_loc, a_cum = [h], [acc]
        for j in range(1, seg):
            sl = slice(j * SUBLANES, (j + 1) * SUBLANES)
            h = a[sl] * h + b[sl]
            acc = a[sl] * acc
            h_loc.append(h)
            a_cum.append(acc)

        step = 1
        while step < SUBLANES:
            keep = sub >= step
            h = jnp.where(keep, acc * pltpu.roll(h, step, axis=0) + h, h)
            acc = jnp.where(keep, acc * pltpu.roll(acc, step, axis=0), acc)
            step *= 2
        h_prev = h_scr[:, cs]
        after = h + acc * h_prev
        h_in = shift_in(after, h_prev)
        h_scr[:, cs] = after[SUBLANES - 1:]

        hs = jnp.concatenate([h_loc[j] + a_cum[j] * h_in for j in range(seg)], axis=0)
        y = (hs * _silu(ug_ref[:, LRU_WIDTH + n * bw:LRU_WIDTH + (n + 1) * bw])).astype(BF16)
        pending = (cs, y)
        yield

    store_time_order(*pending)
    yield


def _rglru_front_kernel(nchunk, npairs, chunks, xs_ref, x_ref, g_ref, w_ref, cprev_ref, h0_ref,
                        scprev_ref, sh0_ref, cw_ref, cb_ref, wr_ref, br_ref, wi_ref, bi_ref, lam_ref,
                        gs_ref, hs_ref, scnew_ref, y_even_ref, y_odd_ref, cnew_ref, hlast_ref,
                        w_scr, wr_scr, wi_scr, us_scr, ug0_scr, ug1_scr, h_scr, tail_scr):
    i = pl.program_id(0)
    p = i - nchunk
    tc = SUB_ROWS
    half = nchunk // 2
    lru = (cw_ref, cb_ref, wr_scr, br_ref, wi_scr, bi_ref, lam_ref, h_scr, tail_scr)

    @pl.when(i < nchunk)
    def _():
        wb = w_ref[...].astype(BF16)
        w_scr[i] = wb
        xs = xs_ref[...]
        xsn = (xs * _rms_scale(xs) * g_ref[...]).astype(BF16)
        r = jnp.dot(xsn, wb, preferred_element_type=F32)
        gs_ref[...] = r

        @pl.when(i < half)
        def _():
            us_scr[i] = r

    @pl.when(i == nchunk - 1)
    def _():
        wr_scr[...] = (0.5 * wr_ref[...]).astype(BF16)
        wi_scr[...] = (0.5 * wi_ref[...]).astype(BF16)
        bw = LRU_BLOCK_W
        for n in range(LRU_BLOCKS):
            cs = slice(n * bw, (n + 1) * bw)
            c, off = divmod(n * bw, W_CHUNK)
            u = us_scr[c, :, off:off + bw]
            cw = cw_ref[:, cs]
            conv = cb_ref[:, cs]
            for tap in range(CONV_W - 1):
                conv = conv + scprev_ref[tap, :, cs] * cw[tap:tap + 1]
                if tap > 0:
                    scnew_ref[tap - 1, :, cs] = scprev_ref[tap, :, cs]
            conv = conv + u * cw[CONV_W - 1:]
            scnew_ref[CONV_W - 2, :, cs] = u
            a, b = _lru_gates(conv, wr_scr[n], br_ref[:, cs], wi_scr[n], bi_ref[:, cs], lam_ref[:, cs])
            hs_ref[:, cs] = a * sh0_ref[:, cs] + b
        ug1_scr[...] = jnp.zeros_like(ug1_scr)
        h_scr[...] = jnp.zeros_like(h_scr)
        tail_scr[...] = jnp.zeros_like(tail_scr)

    @pl.when(p >= 0)
    def _():
        _interleave((_in_proj_tile(x_ref, slice(0, tc), g_ref, w_scr, ug0_scr), 1),
                    (_rglru_tile(ug1_scr, y_odd_ref, *lru), 2))
        hlast_ref[0] = h_scr[...]
        cnew_ref[0] = tail_scr[...]

    @pl.when((p >= 0) & (p < npairs))
    def _():
        @pl.when((2 * p) % chunks == 0)
        def _():
            h_scr[...] = h0_ref[0]
            tail_scr[...] = cprev_ref[0]

        _interleave((_in_proj_tile(x_ref, slice(tc, 2 * tc), g_ref, w_scr, ug1_scr), 1),
                    (_rglru_tile(ug0_scr, y_even_ref, *lru), 2))


def rglru_front(xs, s_conv_prev, s_h0, x, conv_prev, h0, g, w_in, conv_w, conv_b, w_r, b_r, w_i, b_i, lam,
                seq_len):
    m, d = x.shape
    ns = xs.shape[0]
    w = w_in.shape[1] // 2
    tc = SUB_ROWS
    nchunk = w_in.shape[1] // W_CHUNK
    half = nchunk // 2
    bsz = m // seq_len
    chunks = seq_len // tc
    npairs = m // (2 * tc)
    assert seq_len % (2 * tc) == 0 and tc % (SUBLANES * SUBLANES) == 0 and tc // SUBLANES > CONV_W
    pair = lambda i: jnp.clip(i - nchunk, 0, npairs - 1)
    last = npairs * 2 - 1
    seq_in = lambda i: (jnp.clip(2 * (i - nchunk), 0, last) // chunks, 0, 0)
    seq_out = lambda i: (jnp.clip(2 * (i - nchunk) - 1, 0, last) // chunks, 0, 0)
    state_in = lambda rows: pl.BlockSpec((1, rows, w), seq_in)
    state_out = lambda rows: pl.BlockSpec((1, rows, w), seq_out)
    chunk = lambda i: (0, jnp.minimum(i, nchunk - 1))
    return pl.pallas_call(
        functools.partial(_rglru_front_kernel, nchunk, npairs, chunks),
        grid=(nchunk + npairs + 1,),
        in_specs=[_resident(xs.shape), pl.BlockSpec((2 * tc, d), lambda i: (pair(i), 0)), _resident((1, d)),
                  pl.BlockSpec((d, W_CHUNK), chunk), state_in(CONV_W - 1), state_in(1),
                  _resident(s_conv_prev.shape), _resident(s_h0.shape),
                  _resident((CONV_W, w)), _resident((1, w)), _resident(w_r.shape), _resident((1, w)),
                  _resident(w_i.shape), _resident((1, w)), _resident((1, w))],
        out_specs=[
            pl.BlockSpec((ns, W_CHUNK), lambda i: (0, jnp.clip(i - half, 0, half - 1))),
            pl.BlockSpec((ns, w), lambda i: (0, 0)),
            pl.BlockSpec(s_conv_prev.shape, lambda i: (0, 0, 0)),
            pl.BlockSpec((tc, w), lambda i: (pair(i), 0)),
            pl.BlockSpec((tc, w), lambda i: (jnp.clip(i - nchunk - 1, 0, npairs - 1), 0)),
            state_out(CONV_W - 1), state_out(1),
        ],
        out_shape=[
            jax.ShapeDtypeStruct((ns, w), F32),
            jax.ShapeDtypeStruct((ns, w), F32),
            jax.ShapeDtypeStruct(s_conv_prev.shape, F32),
            jax.ShapeDtypeStruct((m // 2, w), BF16),
            jax.ShapeDtypeStruct((m // 2, w), BF16),
            jax.ShapeDtypeStruct((bsz, CONV_W - 1, w), F32),
            jax.ShapeDtypeStruct((bsz, 1, w), F32),
        ],
        scratch_shapes=[pltpu.VMEM((nchunk, d, W_CHUNK), BF16),
                        pltpu.VMEM(w_r.shape, BF16), pltpu.VMEM(w_i.shape, BF16),
                        pltpu.VMEM((half, ns, W_CHUNK), F32),
                        pltpu.VMEM((tc, 2 * w), F32), pltpu.VMEM((tc, 2 * w), F32),
                        pltpu.VMEM((1, w), F32), pltpu.VMEM((CONV_W - 1, w), F32)],
        compiler_params=_params("arbitrary"),
        name="rglru_front",
    )(xs, x, g.reshape(1, d), w_in, conv_prev, h0.reshape(bsz, 1, w), s_conv_prev, s_h0,
      conv_w, conv_b.reshape(1, w), w_r, b_r.reshape(1, w), w_i, b_i.reshape(1, w), lam.reshape(1, w))


def _buckets(dist):
    n = jnp.maximum(dist, 0)
    max_exact = N_BUCKETS // 2
    nf = jnp.maximum(n, 1).astype(F32)
    large = max_exact + jnp.floor(jnp.log(nf / max_exact) / math.log(MAX_DISTANCE / max_exact)
                                  * (N_BUCKETS - max_exact)).astype(jnp.int32)
    large = jnp.minimum(large, N_BUCKETS - 1)
    return jnp.where(n < max_exact, n, large)


def _lookup(bucket, valid, table_ref, head):
    bias = jnp.zeros(bucket.shape, F32)
    for b in range(N_BUCKETS):
        bias = jnp.where(bucket == b, table_ref[b, head], bias)
    return jnp.where(valid, bias, NEG_INF)


def _bias_kernel(table_ref, sinks_ref, band_ref, sinkt_ref, past_ref, new_ref):
    hk = pl.program_id(0)
    span = 3 * BLOCK
    dist = (lax.broadcasted_iota(jnp.int32, (1, span), 1) + BLOCK) % span
    bucket = _buckets(dist)
    in_window = (dist >= 0) & (dist < WINDOW)
    key_row = lax.broadcasted_iota(jnp.int32, (ATT_KEYS, HALF_Q), 0)

    def band(head):
        row = _lookup(bucket, in_window, table_ref, head) * LOG2_E
        full = pltpu.roll(jnp.broadcast_to(row, (ATT_KEYS, span)), 0, axis=1, stride=1, stride_axis=0)
        return full[:, :HALF_Q]

    rows = past_ref.shape[2]
    d_past = rows - lax.broadcasted_iota(jnp.int32, (1, rows), 1)
    b_past = _buckets(d_past)
    ok_past = (d_past >= 0) & (d_past < WINDOW)
    d_new = jnp.zeros((1, LANES), jnp.int32)
    b_new = _buckets(d_new)
    for g in range(GROUP):
        head = hk * GROUP + g
        bias = band(head)
        for half in range(2):
            slot = _head_order(half).index(g)
            cs = slice(slot * HALF_Q, (slot + 1) * HALF_Q)
            band_ref[0, half, 0, :, cs] = bias
            prev_rows = BLOCK - half * HALF_Q
            band_ref[1, half, 0, :, cs] = jnp.where(key_row < prev_rows, NEG_INF, bias)
            sinkt_ref[half, 0, :, cs] = jnp.full((1, HALF_Q), sinks_ref[head] * LOG2_E, F32)
        past_ref[0, g:g + 1, :] = _lookup(b_past, ok_past, table_ref, head)
        new_ref[0, g:g + 1, :] = _lookup(b_new, d_new == 0, table_ref, head)


def bias_tables(table, sinks, past_rows):
    smem = pl.BlockSpec(memory_space=pltpu.SMEM)
    return pl.pallas_call(
        _bias_kernel,
        grid=(N_KV_HEADS,),
        in_specs=[smem, smem],
        out_specs=[
            pl.BlockSpec((2, 2, 1, ATT_KEYS, GROUP * HALF_Q), lambda h: (0, 0, h, 0, 0)),
            pl.BlockSpec((2, 1, 1, GROUP * HALF_Q), lambda h: (0, h, 0, 0)),
            pl.BlockSpec((1, GROUP, past_rows), lambda h: (h, 0, 0)),
            pl.BlockSpec((1, GROUP, LANES), lambda h: (h, 0, 0)),
        ],
        out_shape=[
            jax.ShapeDtypeStruct((2, 2, N_KV_HEADS, ATT_KEYS, GROUP * HALF_Q), F32),
            jax.ShapeDtypeStruct((2, N_KV_HEADS, 1, GROUP * HALF_Q), F32),
            jax.ShapeDtypeStruct((N_KV_HEADS, GROUP, past_rows), F32),
            jax.ShapeDtypeStruct((N_KV_HEADS, GROUP, LANES), F32),
        ],
        compiler_params=_params("parallel"),
        name="bias_tables",
    )(table, sinks)


def _band_attn_rounds(first_tile, q_ref, kp_ref, kc_ref, vp_ref, vc_ref, gate_ref, bias_ref, sink_ref,
                      y_ref, s_scr, p_scr):
    nt = (((1,), (1,)), ((), ()))
    low = (lax.broadcasted_iota(jnp.int32, (1, LANES), 1) < HEAD_DIM)
    keep_low = low.astype(BF16)
    keep_high = 1 - keep_low
    keep = (keep_low, keep_high)
    nkeys = 2 * BLOCK
    ones_rows = jnp.where(lax.broadcasted_iota(jnp.int32, (2 * SUBLANES, nkeys), 0) == 0,
                          1.0, 0.0).astype(BF16)
    rows = ATT_ROWS
    nslot = s_scr.shape[0]
    items = [(blk, hk, half) for blk in range(q_ref.shape[0] // BLOCK)
             for hk in range(N_KV_HEADS) for half in range(2)]
    assert nslot % 2 == 0

    def rows_of(blk):
        return slice(blk * BLOCK, (blk + 1) * BLOCK)

    def key_rows(half):
        return slice(half * HALF_Q, half * HALF_Q + ATT_KEYS)

    for slot in range(nslot):
        dead = slice(ATT_KEYS, nkeys) if slot % 2 == 0 else slice(0, HALF_Q)
        p_scr[slot, dead, :] = jnp.zeros((HALF_Q, p_scr.shape[2]), BF16)

    def scores(idx):
        blk, hk, half = items[idx]
        variant = first_tile if blk == 0 else 0
        cs = slice(hk * LANES, (hk + 1) * LANES)
        k_prev = kp_ref[:, cs] if blk == 0 else kc_ref[rows_of(blk - 1), cs]
        k_cur = kc_ref[rows_of(blk), cs]
        kd = (jnp.concatenate([k_prev, k_cur[:HALF_Q]], axis=0) if half == 0
              else jnp.concatenate([k_prev[HALF_Q:], k_cur], axis=0))
        q0 = blk * BLOCK + half * HALF_Q
        qs = jnp.concatenate(
            [q_ref[q0:q0 + HALF_Q, (hk * SLABS + h // HEADS_PER_TILE) * LANES:
                   (hk * SLABS + h // HEADS_PER_TILE + 1) * LANES] * keep[h % HEADS_PER_TILE]
             for h in _head_order(half)], axis=0)
        s = lax.dot_general(kd, qs, nt, preferred_element_type=F32) + bias_ref[variant, half, hk]
        s_scr[idx % nslot, key_rows(half), :] = s
        return jnp.maximum(jnp.max(s, axis=0, keepdims=True), sink_ref[half, hk])

    def softmax(idx, m):
        blk, hk, half = items[idx]
        slot = idx % nslot
        lo = half * HALF_Q
        for r in range(lo, lo + ATT_KEYS, rows):
            p_scr[slot, r:r + rows, :] = jnp.exp2(s_scr[slot, r:r + rows, :] - m).astype(BF16)
        return jnp.exp2(sink_ref[half, hk] - m)

    def weighted_values(idx):
        blk, hk, half = items[idx]
        vs = slice(hk * LANES, hk * LANES + HEAD_DIM)
        v_prev = vp_ref[vs, :] if blk == 0 else vc_ref[vs, rows_of(blk - 1)]
        vt = jnp.concatenate([v_prev, vc_ref[vs, rows_of(blk)]], axis=1)
        lhs_v = jnp.concatenate([vt, ones_rows], axis=0)
        return jnp.dot(lhs_v, p_scr[idx % nslot], preferred_element_type=F32)

    low_q = lax.broadcasted_iota(jnp.int32, (HEAD_DIM, LANES), 1) < HALF_Q

    def finish(blk, hk, ots, sink_ws):
        o = [ots[half][:HEAD_DIM] * (1.0 / (ots[half][HEAD_DIM:HEAD_DIM + 1] + sink_ws[half]))
             for half in range(2)]
        for sl in range(SLABS):
            a, b = (oh[:, sl * LANES:(sl + 1) * LANES] for oh in o)
            even = jnp.where(low_q, a, b)
            odd = pltpu.roll(jnp.where(low_q, b, a), HALF_Q, axis=1)
            pair = jnp.concatenate([even, odd], axis=0)
            c0 = (hk * SLABS + sl) * LANES
            y_ref[rows_of(blk), c0:c0 + LANES] = (
                pair.T * _silu(gate_ref[rows_of(blk), c0:c0 + LANES])).astype(y_ref.dtype)

    n = len(items)
    offs, sink_ws, outs = {}, {}, {}
    for k in range(-2 * ATT_SKEW, n + ATT_SKEW):
        if 0 <= k + 2 * ATT_SKEW < n:
            offs[k + 2 * ATT_SKEW] = scores(k + 2 * ATT_SKEW)
        if 0 <= k + ATT_SKEW < n:
            sink_ws[k + ATT_SKEW] = softmax(k + ATT_SKEW, offs.pop(k + ATT_SKEW))
        if 0 <= k < n:
            outs[k] = weighted_values(k)
        j = k - ATT_SKEW
        if 0 <= j < n and items[j][2] == 1:
            finish(items[j][0], items[j][1], [outs.pop(j - 1), outs.pop(j)],
                   [sink_ws.pop(j - 1), sink_ws.pop(j)])
        yield


def _band_attn_kernel(q_ref, kp_ref, kc_ref, vp_ref, vc_ref, gate_ref, bias_ref, sink_ref, y_ref,
                      s_scr, p_scr):
    first_tile = (pl.program_id(1) == 0).astype(jnp.int32)
    for _ in _band_attn_rounds(first_tile, q_ref, kp_ref, kc_ref, vp_ref, vc_ref, gate_ref, bias_ref,
                               sink_ref, y_ref, s_scr, p_scr):
        pass


def band_attention(q, kdup, vt, gate, bias_band, sink_t, bsz, t):
    m = q.shape[0]
    nblk = t // BLOCK
    per_tile = ATT_TILE // BLOCK
    ntile = t // ATT_TILE
    assert t % ATT_TILE == 0
    before = lambda b, i: b * nblk + jnp.maximum(per_tile * i - 1, 0)
    cur = lambda n: pl.BlockSpec((ATT_TILE, n), lambda b, i: (b * ntile + i, 0))
    prev = lambda n: pl.BlockSpec((BLOCK, n), lambda b, i: (before(b, i), 0))
    cur_t = pl.BlockSpec((2 * KV_WIDTH, ATT_TILE), lambda b, i: (0, b * ntile + i))
    prev_t = pl.BlockSpec((2 * KV_WIDTH, BLOCK), lambda b, i: (0, before(b, i)))
    score_tile = (2 * BLOCK, GROUP * HALF_Q)
    return pl.pallas_call(
        _band_attn_kernel,
        grid=(bsz, ntile),
        in_specs=[
            cur(ATT_WIDTH), prev(2 * KV_WIDTH), cur(2 * KV_WIDTH), prev_t, cur_t,
            cur(ATT_WIDTH), _resident(bias_band.shape), _resident(sink_t.shape),
        ],
        out_specs=cur(ATT_WIDTH),
        out_shape=jax.ShapeDtypeStruct((m, ATT_WIDTH), BF16),
        scratch_shapes=[pltpu.VMEM((ATT_SLOTS,) + score_tile, F32),
                        pltpu.VMEM((ATT_SLOTS,) + score_tile, BF16)],
        compiler_params=_params("parallel", "parallel"),
        name="band_attention",
    )(q, kdup, kdup, vt, vt, gate, bias_band, sink_t)


def _proj_tile(y_scr, w_scr, g_ref, x_ref, o_ref, raw_scr):
    y = y_scr[...]
    for c in range(w_scr.shape[0]):
        for n0 in range(0, W_CHUNK, MXU_COLS):
            raw_scr[:, c * W_CHUNK + n0:c * W_CHUNK + n0 + MXU_COLS] = jnp.dot(
                y, w_scr[c, :, n0:n0 + MXU_COLS], preferred_element_type=F32)
            yield
    step = y_scr.shape[0] // PROJ_TAIL_PIECES
    for r0 in range(0, y_scr.shape[0], step):
        o = raw_scr[r0:r0 + step, :]
        o_ref[r0:r0 + step, :] = x_ref[r0:r0 + step, :] + o * _rms_scale(o) * g_ref[...]
        yield


def _attn_proj_kernel(nchunk, ntiles, tiles_per_seq, as_ref, gs_ref, xs_ref,
                      q_ref, kp_ref, kc_ref, vp_ref, vc_ref, gate_ref, bias_ref, sink_ref,
                      w_ref, g_ref, x_ref, os_ref, o_ref,
                      w_scr, raws_scr, s_scr, p_scr, ynew_scr, yold_scr, raw_scr):
    i = pl.program_id(0)
    p = i - nchunk

    @pl.when(i < nchunk)
    def _():
        wb = w_ref[...].astype(BF16)
        w_scr[i] = wb
        ys = (as_ref[...] * _silu(gs_ref[...])).astype(BF16)
        raws_scr[i] = jnp.dot(ys, wb, preferred_element_type=F32)

    @pl.when(i == nchunk - 1)
    def _():
        o = jnp.concatenate([raws_scr[c] for c in range(nchunk)], axis=1)
        os_ref[...] = xs_ref[...] + o * _rms_scale(o) * g_ref[...]
        yold_scr[...] = jnp.zeros_like(yold_scr)

    @pl.when(p >= 0)
    def _():
        tile = jnp.minimum(p, ntiles - 1)
        first = (tile % tiles_per_seq == 0).astype(jnp.int32)
        _interleave((_band_attn_rounds(first, q_ref, kp_ref, kc_ref, vp_ref, vc_ref, gate_ref, bias_ref,
                                       sink_ref, ynew_scr, s_scr, p_scr), ATT_ROUNDS_PER_PROJ_PIECE),
                    (_proj_tile(yold_scr, w_scr, g_ref, x_ref, o_ref, raw_scr), 1))
        yold_scr[...] = ynew_scr[...]


def attn_proj(a_s, gate_s, x_s, q, kdup, vt, gate, bias_band, sink_t, w, g, x, seq_len):
    k, d = w.shape
    m = x.shape[0]
    tm = ATT_PROJ_TILE
    nchunk = d // W_CHUNK
    ntiles = m // tm
    tiles_per_seq = seq_len // tm
    per_tile = tm // BLOCK
    assert seq_len % tm == 0 and tm % BLOCK == 0
    att_tile = lambda i: jnp.clip(i - nchunk, 0, ntiles - 1)
    proj_tile = lambda i: jnp.clip(i - nchunk - 1, 0, ntiles - 1)

    def before(i):
        t = att_tile(i)
        return per_tile * t - jnp.where(t % tiles_per_seq == 0, 0, 1)

    rows = lambda n: pl.BlockSpec((tm, n), lambda i: (att_tile(i), 0))
    score_tile = (2 * BLOCK, GROUP * HALF_Q)
    return pl.pallas_call(
        functools.partial(_attn_proj_kernel, nchunk, ntiles, tiles_per_seq),
        grid=(nchunk + ntiles + 1,),
        in_specs=[
            _resident(a_s.shape), _resident(gate_s.shape), _resident(x_s.shape),
            rows(ATT_WIDTH), pl.BlockSpec((BLOCK, 2 * KV_WIDTH), lambda i: (before(i), 0)),
            rows(2 * KV_WIDTH), pl.BlockSpec((2 * KV_WIDTH, BLOCK), lambda i: (0, before(i))),
            pl.BlockSpec((2 * KV_WIDTH, tm), lambda i: (0, att_tile(i))), rows(ATT_WIDTH),
            _resident(bias_band.shape), _resident(sink_t.shape),
            pl.BlockSpec((k, W_CHUNK), lambda i: (0, jnp.minimum(i, nchunk - 1))), _resident((1, d)),
            pl.BlockSpec((tm, d), lambda i: (proj_tile(i), 0)),
        ],
        out_specs=[pl.BlockSpec(x_s.shape, lambda i: (0, 0)),
                   pl.BlockSpec((tm, d), lambda i: (proj_tile(i), 0))],
        out_shape=[jax.ShapeDtypeStruct(x_s.shape, F32), jax.ShapeDtypeStruct((m, d), F32)],
        scratch_shapes=[pltpu.VMEM((nchunk, k, W_CHUNK), BF16),
                        pltpu.VMEM((nchunk, x_s.shape[0], W_CHUNK), F32),
                        pltpu.VMEM((ATT_SLOTS,) + score_tile, F32),
                        pltpu.VMEM((ATT_SLOTS,) + score_tile, BF16),
                        pltpu.VMEM((tm, k), BF16), pltpu.VMEM((tm, k), BF16),
                        pltpu.VMEM((tm, d), F32)],
        compiler_params=_params("arbitrary"),
        name="attn_proj",
    )(a_s, gate_s, x_s, q, kdup, kdup, vt, vt, gate, bias_band, sink_t, w, g.reshape(1, d), x)


def _cached_attn_kernel(q_ref, ckt_ref, cvt_ref, kn_ref, vn_ref, sinks_ref, bpast_ref, bnew_ref, o_ref):
    shape = (N_Q_HEADS, KV_WIDTH)
    lane_kv = lax.broadcasted_iota(jnp.int32, shape, 1) // HEAD_DIM
    row_kv = lax.broadcasted_iota(jnp.int32, shape, 0) // GROUP
    own = lane_kv == row_kv
    sink = sinks_ref[...]
    nt = (((1,), (1,)), ((), ()))
    for b in range(q_ref.shape[0]):
        q = q_ref[b]
        qt = jnp.concatenate([q] * N_KV_HEADS, axis=1)
        qm = jnp.where(own, qt, 0.0).astype(BF16)
        knew = kn_ref[b].astype(BF16).astype(F32)
        vnew = vn_ref[b].astype(BF16).astype(F32)
        s = jnp.dot(qm, ckt_ref[b].astype(BF16), preferred_element_type=F32) + bpast_ref[...]
        s_new = jnp.sum(qm.astype(F32) * knew, axis=-1, keepdims=True) + bnew_ref[:, :1]
        m = jnp.maximum(jnp.maximum(jnp.max(s, axis=-1, keepdims=True), s_new), sink)
        p = jnp.exp(s - m)
        p_new = jnp.exp(s_new - m)
        denom = jnp.sum(p, axis=-1, keepdims=True) + p_new + jnp.exp(sink - m)
        o_all = (lax.dot_general(p.astype(BF16), cvt_ref[b].astype(BF16), nt, preferred_element_type=F32)
                 + p_new.astype(BF16).astype(F32) * vnew)
        o_all = jnp.where(own, o_all, 0.0)
        o = o_all[:, :HEAD_DIM]
        for hk in range(1, N_KV_HEADS):
            o = o + o_all[:, hk * HEAD_DIM:(hk + 1) * HEAD_DIM]
        o_ref[b] = o / denom


def cached_attention(q, cache_kt, cache_vt, k_new, v_new, sinks, bias_past, bias_new):
    bsz, _, rows = cache_kt.shape
    nseq = math.gcd(bsz, SEQS_PER_STEP)
    per_seq = lambda r, n: pl.BlockSpec((nseq, r, n), lambda b: (b, 0, 0))
    return pl.pallas_call(
        _cached_attn_kernel,
        grid=(bsz // nseq,),
        in_specs=[
            per_seq(N_Q_HEADS, HEAD_DIM), per_seq(KV_WIDTH, rows), per_seq(KV_WIDTH, rows),
            per_seq(1, KV_WIDTH), per_seq(1, KV_WIDTH),
            _resident((N_Q_HEADS, 1)), _resident((N_Q_HEADS, rows)), _resident((N_Q_HEADS, LANES)),
        ],
        out_specs=per_seq(N_Q_HEADS, HEAD_DIM),
        out_shape=jax.ShapeDtypeStruct((bsz, N_Q_HEADS, HEAD_DIM), F32),
        compiler_params=_params("parallel"),
        name="cached_attention",
    )(q, cache_kt, cache_vt, k_new, v_new, sinks, bias_past, bias_new)


def kernel(x_prompt, x_sample, state_conv, state_h, cache_k, cache_v, a_norm_pre, a_norm_post,
           a_w_in, a_conv_w, a_conv_b, a_w_r, a_b_r, a_w_i, a_b_i, a_lambda, a_w_out, kv_norm, w_kv,
           b_norm_pre, b_norm_post, b_w_qg, b_sinks, b_w_out, rel_bias_table):
    bsz, t, d = x_prompt.shape
    dbsz, dt, _ = x_sample.shape
    assert a_w_in.shape[0] == 1 and b_w_qg.shape[0] == 1 and dt == 1
    assert t % BLOCK == 0 and t >= WINDOW
    past_rows = cache_k.shape[1]
    assert past_rows == min(WINDOW, PAST_LEN)

    sinks = b_sinks[0]
    bias_band, sink_t, bias_past, bias_new = bias_tables(rel_bias_table, sinks, past_rows)

    tm = 2 * SUB_ROWS
    xp = x_prompt.reshape(bsz * t, d)
    xs = x_sample.reshape(dbsz, d)

    conv0 = jnp.zeros((bsz, CONV_W - 1, LRU_WIDTH), F32)
    h0 = jnp.zeros((bsz, LRU_WIDTH), F32)
    gate_s, hs, s_conv_t, y_even, y_odd, p_conv, p_h = rglru_front(
        xs, jnp.transpose(state_conv[0], (1, 0, 2)), state_h[0], xp, conv0, h0, a_norm_pre[0], a_w_in[0],
        a_conv_w[0], a_conv_b[0], a_w_r[0], a_b_r[0], a_w_i[0], a_b_i[0], a_lambda[0], seq_len=t)
    xs1, x1 = proj_norm_res(hs, gate_s, xs, (y_even, y_odd), a_w_out[0], a_norm_post[0], xp, tm)

    ks, vs, qs, gate_sb, q, gate_b, kdup, vt, k_tail, v_tail = norm_proj_kvq(
        xs1, x1, kv_norm, b_norm_pre[0], w_kv, b_w_qg[0], SUB_ROWS, seq_len=t)
    cache_kt = jnp.transpose(cache_k, (0, 2, 3, 1)).reshape(dbsz, KV_WIDTH, past_rows)
    cache_vt = jnp.transpose(cache_v, (0, 2, 3, 1)).reshape(dbsz, KV_WIDTH, past_rows)
    os_ = cached_attention(qs.reshape(dbsz, N_Q_HEADS, HEAD_DIM), cache_kt, cache_vt,
                           ks.reshape(dbsz, 1, KV_WIDTH), vs.reshape(dbsz, 1, KV_WIDTH),
                           sinks.reshape(N_Q_HEADS, 1), bias_past.reshape(N_Q_HEADS, past_rows),
                           bias_new.reshape(N_Q_HEADS, LANES))
    y_sample, y_prompt = attn_proj(os_.reshape(dbsz, ATT_WIDTH), gate_sb, xs1, q, kdup, vt, gate_b,
                                   bias_band, sink_t, b_w_out[0], b_norm_post[0], x1, seq_len=t)
    y_prompt = y_prompt.reshape(bsz, t, d)
    p_k = jnp.transpose(k_tail.reshape(bsz, N_KV_HEADS, HEAD_DIM, WINDOW), (0, 3, 1, 2))
    p_v = jnp.transpose(v_tail.reshape(bsz, N_KV_HEADS, HEAD_DIM, WINDOW), (0, 3, 1, 2))

    return (y_prompt, y_sample.reshape(dbsz, 1, d),
            p_conv[None], p_h.reshape(1, bsz, LRU_WIDTH), p_k, p_v,
            jnp.transpose(s_conv_t, (1, 0, 2))[None], hs[None],
            ks.reshape(dbsz, 1, N_KV_HEADS, HEAD_DIM), vs.reshape(dbsz, 1, N_KV_HEADS, HEAD_DIM))
```

```python
import functools
import math

import jax
import jax.numpy as jnp
from jax import lax
from jax.experimental import pallas as pl
from jax.experimental.pallas import tpu as pltpu

F32 = jnp.float32
BF16 = jnp.bfloat16

D_MODEL = 2048
LRU_WIDTH = 2048
LRU_BLOCKS = 8
LRU_BLOCK_W = LRU_WIDTH // LRU_BLOCKS
CONV_W = 4
LRU_C = 8.0
HEAD_DIM = 64
N_Q_HEADS = 32
N_KV_HEADS = 8
GROUP = N_Q_HEADS // N_KV_HEADS
ATT_WIDTH = N_Q_HEADS * HEAD_DIM
KV_WIDTH = N_KV_HEADS * HEAD_DIM
WINDOW = 128
BLOCK = WINDOW
N_BUCKETS = 32
MAX_DISTANCE = 128
RMS_EPS = 1e-6
NEG_INF = -1e30
LOG2_E = 1.4426950408889634
PAST_LEN = 16384

V7X_VMEM_BYTES = 64 * 1024 * 1024
VMEM_LIMIT = V7X_VMEM_BYTES - 8 * 1024 * 1024
SUBLANES = 8
LANES = 128
HEADS_PER_TILE = LANES // HEAD_DIM
SLABS = GROUP // HEADS_PER_TILE
MXU_COLS = 256
SUB_ROWS = 256
W_CHUNK = 512
ATT_ROWS = 64
ATT_SKEW = 2
ATT_SLOTS = 2 * ATT_SKEW + 2
HALF_Q = BLOCK // 2
ATT_KEYS = WINDOW + HALF_Q
SEQS_PER_STEP = 8
SCAN_PIECES_PER_PROJ_PIECE = 2
ATT_PROJ_TILE = 2 * BLOCK
PROJ_TAIL_PIECES = 4
ATT_ROUNDS_PER_PROJ_PIECE = 4
SQRT_FLOOR = 1e-30


def _head_order(half):
    heads = list(range(GROUP))
    return heads if half == 0 else [h ^ 1 for h in heads]


def _params(*semantics):
    return pltpu.CompilerParams(dimension_semantics=semantics, vmem_limit_bytes=VMEM_LIMIT)


def _resident(shape):
    zeros = (0,) * len(shape)
    return pl.BlockSpec(shape, lambda *_: zeros, pipeline_mode=pl.Buffered(1))


def _rms_scale(x):
    return lax.rsqrt(jnp.mean(x * x, axis=-1, keepdims=True) + RMS_EPS)


def _silu(x):
    h = 0.5 * x
    return h * jnp.tanh(h) + h


def _segment_major(rows, inverse=False):
    seg = rows // SUBLANES
    r = lax.broadcasted_iota(jnp.int32, (rows, rows), 0)
    c = lax.broadcasted_iota(jnp.int32, (rows, rows), 1)
    if inverse:
        src = (r % seg) * SUBLANES + r // seg
    else:
        src = (r % SUBLANES) * seg + r // SUBLANES
    return jnp.where(c == src, 1.0, 0.0).astype(BF16)


def _phase_specs(nchunk, tm, k):
    chunk_w = pl.BlockSpec((k, W_CHUNK), lambda i: (0, jnp.minimum(i, nchunk - 1)))
    tile = lambda n: pl.BlockSpec((tm, n), lambda i: (jnp.maximum(i - nchunk, 0), 0))
    return chunk_w, tile


def _dup_heads(x):
    low = lax.broadcasted_iota(jnp.int32, (x.shape[0], LANES), 1) < HEAD_DIM
    out = []
    for c in range(x.shape[1] // LANES):
        col = x[:, c * LANES:(c + 1) * LANES]
        swapped = pltpu.roll(col, HEAD_DIM, axis=1)
        out += [jnp.where(low, col, swapped), jnp.where(low, swapped, col)]
    return jnp.concatenate(out, axis=1)


def _norm_proj_kvq_kernel(nkv, nqg, xs_ref, x_ref, gkv_ref, gq_ref, wkv_ref, wqg_ref,
                          ks_ref, vs_ref, qs_ref, gates_ref,
                          q_ref, gate_ref, kdup_ref, vt_ref, ktail_ref, vtail_ref,
                          wkv_scr, wqg_scr):
    i = pl.program_id(0)
    nchunk = nkv + nqg
    q_chunks = ATT_WIDTH // W_CHUNK
    q_scale = 1.0 / math.sqrt(HEAD_DIM)
    q_scale_log2 = q_scale * LOG2_E

    def sample_rows(g_ref):
        xs = xs_ref[...]
        return (xs * _rms_scale(xs) * g_ref[...]).astype(BF16)

    @pl.when(i < nkv)
    def _():
        wb = wkv_ref[...].astype(BF16)
        wkv_scr[i] = wb
        r = jnp.dot(sample_rows(gkv_ref), wb, preferred_element_type=F32)

        @pl.when(i == 0)
        def _():
            ks_ref[...] = r

        @pl.when(i == 1)
        def _():
            vs_ref[...] = r

    @pl.when((i >= nkv) & (i < nchunk))
    def _():
        c = i - nkv
        wb = wqg_ref[...].astype(BF16)
        wqg_scr[c] = wb
        r = jnp.dot(sample_rows(gq_ref), wb, preferred_element_type=F32)

        @pl.when(c < q_chunks)
        def _():
            qs_ref[...] = r * q_scale

        @pl.when(c >= q_chunks)
        def _():
            gates_ref[...] = r

    @pl.when(i >= nchunk)
    def _():
        tm = x_ref.shape[0]
        for rs in _row_blocks(tm):
            x = x_ref[rs, :]
            xh = x * _rms_scale(x)
            xkv = (xh * gkv_ref[...]).astype(BF16)
            xq = (xh * gq_ref[...]).astype(BF16)
            k = jnp.dot(xkv, wkv_scr[0], preferred_element_type=F32)
            v = jnp.dot(xkv, wkv_scr[1], preferred_element_type=F32)
            kdup_ref[rs, :] = _dup_heads(k).astype(BF16)
            vt_ref[:, rs] = _dup_heads(v).T.astype(BF16)
            for c in range(nqg):
                r = jnp.dot(xq, wqg_scr[c], preferred_element_type=F32)
                if c < q_chunks:
                    q_ref[rs, c * W_CHUNK:(c + 1) * W_CHUNK] = (r * q_scale_log2).astype(q_ref.dtype)
                else:
                    cc = c - q_chunks
                    gate_ref[rs, cc * W_CHUNK:(cc + 1) * W_CHUNK] = r
        ktail_ref[0] = k[k.shape[0] - WINDOW:].T
        vtail_ref[0] = v[v.shape[0] - WINDOW:].T


def norm_proj_kvq(xs, x, g_kv, g_q, w_kv, w_qg, tm, seq_len):
    m, d = x.shape
    ns = xs.shape[0]
    assert w_kv.shape[1] == 2 * KV_WIDTH == 2 * W_CHUNK and seq_len % tm == 0 and tm >= WINDOW
    nkv, nqg = w_kv.shape[1] // W_CHUNK, w_qg.shape[1] // W_CHUNK
    nchunk = nkv + nqg
    tiles = seq_len // tm
    tile = lambda n: pl.BlockSpec((tm, n), lambda i: (jnp.maximum(i - nchunk, 0), 0))
    tail = pl.BlockSpec((1, KV_WIDTH, WINDOW), lambda i: (jnp.maximum(i - nchunk, 0) // tiles, 0, 0))
    kv_chunk = lambda i: (0, jnp.minimum(i, nkv - 1))
    qg_chunk = lambda i: (0, jnp.clip(i - nkv, 0, nqg - 1))
    q_chunks = ATT_WIDTH // W_CHUNK
    whole_s = lambda n: pl.BlockSpec((ns, n), lambda i: (0, 0))
    return pl.pallas_call(
        functools.partial(_norm_proj_kvq_kernel, nkv, nqg),
        grid=(nchunk + m // tm,),
        in_specs=[
            _resident(xs.shape), tile(d), _resident((1, d)), _resident((1, d)),
            pl.BlockSpec((d, W_CHUNK), kv_chunk), pl.BlockSpec((d, W_CHUNK), qg_chunk),
        ],
        out_specs=[
            whole_s(KV_WIDTH), whole_s(KV_WIDTH),
            pl.BlockSpec((ns, W_CHUNK), lambda i: (0, jnp.clip(i - nkv, 0, q_chunks - 1))),
            pl.BlockSpec((ns, W_CHUNK), lambda i: (0, jnp.clip(i - nkv - q_chunks, 0, nqg - q_chunks - 1))),
            tile(ATT_WIDTH), tile(ATT_WIDTH), tile(2 * KV_WIDTH),
            pl.BlockSpec((2 * KV_WIDTH, tm), lambda i: (0, jnp.maximum(i - nchunk, 0))), tail, tail,
        ],
        out_shape=[
            jax.ShapeDtypeStruct((ns, KV_WIDTH), F32),
            jax.ShapeDtypeStruct((ns, KV_WIDTH), F32),
            jax.ShapeDtypeStruct((ns, ATT_WIDTH), F32),
            jax.ShapeDtypeStruct((ns, w_qg.shape[1] - ATT_WIDTH), F32),
            jax.ShapeDtypeStruct((m, ATT_WIDTH), BF16),
            jax.ShapeDtypeStruct((m, ATT_WIDTH), F32),
            jax.ShapeDtypeStruct((m, 2 * KV_WIDTH), BF16),
            jax.ShapeDtypeStruct((2 * KV_WIDTH, m), BF16),
            jax.ShapeDtypeStruct((m // seq_len, KV_WIDTH, WINDOW), F32),
            jax.ShapeDtypeStruct((m // seq_len, KV_WIDTH, WINDOW), F32),
        ],
        scratch_shapes=[pltpu.VMEM((nkv, d, W_CHUNK), BF16), pltpu.VMEM((nqg, d, W_CHUNK), BF16)],
        compiler_params=_params("arbitrary"),
        name="norm_proj_kvq",
    )(xs, x, g_kv.reshape(1, d), g_q.reshape(1, d), w_kv, w_qg)


def _row_blocks(rows):
    sub = min(rows, SUB_ROWS)
    return [slice(r, r + sub) for r in range(0, rows, sub)]


def _proj_norm_res_kernel(nchunk, nparts, as_ref, gs_ref, xs_ref, *refs):
    y_refs = refs[:nparts]
    w_ref, g_ref, x_ref, os_ref, o_ref, w_scr, raw_scr = refs[nparts:]
    i = pl.program_id(0)

    @pl.when(i < nchunk)
    def _():
        wb = w_ref[...].astype(BF16)
        w_scr[i] = wb
        ys = (as_ref[...] * _silu(gs_ref[...])).astype(BF16)
        raw_scr[i] = jnp.dot(ys, wb, preferred_element_type=F32)

    @pl.when(i == nchunk - 1)
    def _():
        o = jnp.concatenate([raw_scr[c] for c in range(nchunk)], axis=1)
        os_ref[...] = xs_ref[...] + o * _rms_scale(o) * g_ref[...]

    @pl.when(i >= nchunk)
    def _():
        tm = x_ref.shape[0]
        part_rows = tm // nparts

        for rs in _row_blocks(tm):
            part, off = divmod(rs.start, part_rows)
            y = y_refs[part][off:off + rs.stop - rs.start, :]
            o = jnp.concatenate([jnp.dot(y, w_scr[c], preferred_element_type=F32)
                                 for c in range(nchunk)], axis=1)
            o_ref[rs, :] = x_ref[rs, :] + o * _rms_scale(o) * g_ref[...]


def proj_norm_res(a_s, gate_s, x_s, y_parts, w, g, x, tm):
    k, d = w.shape
    m = x.shape[0]
    nchunk = d // W_CHUNK
    nparts = len(y_parts)
    assert (tm // nparts) % min(tm, SUB_ROWS) == 0
    chunk_w, tile = _phase_specs(nchunk, tm, k)
    part = pl.BlockSpec((tm // nparts, k), lambda i: (jnp.maximum(i - nchunk, 0), 0))
    return pl.pallas_call(
        functools.partial(_proj_norm_res_kernel, nchunk, nparts),
        grid=(nchunk + m // tm,),
        in_specs=[_resident(a_s.shape), _resident(gate_s.shape), _resident(x_s.shape)]
        + [part] * nparts + [chunk_w, _resident((1, d)), tile(d)],
        out_specs=[pl.BlockSpec(x_s.shape, lambda i: (0, 0)), tile(d)],
        out_shape=[jax.ShapeDtypeStruct(x_s.shape, F32), jax.ShapeDtypeStruct((m, d), F32)],
        scratch_shapes=[pltpu.VMEM((nchunk, k, W_CHUNK), BF16),
                        pltpu.VMEM((nchunk, x_s.shape[0], W_CHUNK), F32)],
        compiler_params=_params("arbitrary"),
        name="proj_norm_res",
    )(a_s, gate_s, x_s, *y_parts, w, g.reshape(1, d), x)


def _lru_gate_dots(conv, wr_half, wi_half):
    cb = conv.astype(BF16)
    return (jnp.dot(cb, wr_half, preferred_element_type=F32),
            jnp.dot(cb, wi_half, preferred_element_type=F32))


def _lru_gates(conv, wr_half, br, wi_half, bi, lam):
    return _lru_gate_math(conv, _lru_gate_dots(conv, wr_half, wi_half), br, bi, lam)


def _lru_gate_math(conv, half_pre, br, bi, lam):
    th_r = jnp.tanh(half_pre[0] + 0.5 * br)
    th_i = jnp.tanh(half_pre[1] + 0.5 * bi)
    nl = -lam
    softplus = jnp.maximum(nl, 0.0) + jnp.log1p(jnp.exp(-jnp.abs(nl)))
    half = (0.5 * LRU_C) * softplus
    x = th_r * half + half
    a = jnp.exp2(x * -LOG2_E)
    z = jnp.tanh(x) * (a * a + 1.0)
    mult = z * lax.rsqrt(jnp.maximum(z, SQRT_FLOOR))
    hc = 0.5 * conv
    return a, mult * (hc * th_i + hc)


def _interleave(*stages):
    live = [[stage, share] for stage, share in stages]
    while live:
        for entry in list(live):
            try:
                for _ in range(entry[1]):
                    next(entry[0])
            except StopIteration:
                live.remove(entry)


def _in_proj_tile(x_ref, rs, g_ref, w_scr, ug_ref):
    x = x_ref[rs, :]
    xn = (x * _rms_scale(x) * g_ref[...]).astype(BF16)
    xn = jnp.dot(_segment_major(xn.shape[0]), xn, preferred_element_type=F32).astype(BF16)
    for c in range(w_scr.shape[0]):
        for n0 in range(0, W_CHUNK, MXU_COLS):
            ug_ref[:, c * W_CHUNK + n0:c * W_CHUNK + n0 + MXU_COLS] = jnp.dot(
                xn, w_scr[c, :, n0:n0 + MXU_COLS], preferred_element_type=F32)
            yield


def _rglru_tile(ug_ref, y_ref, cw_ref, cb_ref, wr_ref, br_ref, wi_ref, bi_ref, lam_ref, h_scr, tail_scr):
    tc = ug_ref.shape[0]
    seg = tc // SUBLANES
    ntaps = CONV_W - 1
    bw = LRU_BLOCK_W
    sub = lax.broadcasted_iota(jnp.int32, (SUBLANES, bw), 0)
    first = sub == 0
    time_order = _segment_major(tc, inverse=True)

    def shift_in(x, row0):
        return jnp.where(first, row0, pltpu.roll(x, 1, axis=0))

    def group(x, j):
        return x[j * SUBLANES:(j + 1) * SUBLANES]

    def store_time_order(cols, y):
        y_ref[:, cols] = jnp.dot(time_order, y, preferred_element_type=F32).astype(y_ref.dtype)

    pending = None
    for n in range(LRU_BLOCKS):
        cs = slice(n * bw, (n + 1) * bw)
        u = ug_ref[:, cs]
        tail = tail_scr[:, cs]
        before = [shift_in(group(u, seg - m), tail[ntaps - m:ntaps - m + 1])
                  for m in range(ntaps, 0, -1)]
        ext = jnp.concatenate(before + [u], axis=0)
        tail_scr[:, cs] = jnp.concatenate(
            [group(u, seg - m)[SUBLANES - 1:] for m in range(ntaps, 0, -1)], axis=0)
        cw = cw_ref[:, cs]
        conv = cb_ref[:, cs]
        for tap in range(CONV_W):
            conv = conv + ext[tap * SUBLANES:tap * SUBLANES + tc] * cw[tap:tap + 1]
        yield

        half_pre = _lru_gate_dots(conv, wr_ref[n], wi_ref[n])
        yield

        if pending is not None:
            store_time_order(*pending)
        yield

        a, b = _lru_gate_math(conv, half_pre, br_ref[:, cs], bi_ref[:, cs], lam_ref[:, cs])

        h = b[:SUBLANES]
        acc = a[:SUBLANES]
        h_loc, a_cum = [h], [acc]
        for j in range(1, seg):
            sl = slice(j * SUBLANES, (j + 1) * SUBLANES)
            h = a[sl] * h + b[sl]
            acc = a[sl] * acc
            h_loc.append(h)
            a_cum.append(acc)

        step = 1
        while step < SUBLANES:
            keep = sub >= step
            h = jnp.where(keep, acc * pltpu.roll(h, step, axis=0) + h, h)
            acc = jnp.where(keep, acc * pltpu.roll(acc, step, axis=0), acc)
            step *= 2
        h_prev = h_scr[:, cs]
        after = h + acc * h_prev
        h_in = shift_in(after, h_prev)
        h_scr[:, cs] = after[SUBLANES - 1:]

        hs = jnp.concatenate([h_loc[j] + a_cum[j] * h_in for j in range(seg)], axis=0)
        y = (hs * _silu(ug_ref[:, LRU_WIDTH + n * bw:LRU_WIDTH + (n + 1) * bw])).astype(BF16)
        pending = (cs, y)
        yield

    store_time_order(*pending)
    yield


def _rglru_front_kernel(nchunk, npairs, chunks, xs_ref, x_ref, g_ref, w_ref, cprev_ref, h0_ref,
                        scprev_ref, sh0_ref, cw_ref, cb_ref, wr_ref, br_ref, wi_ref, bi_ref, lam_ref,
                        gs_ref, hs_ref, scnew_ref, y_even_ref, y_odd_ref, cnew_ref, hlast_ref,
                        w_scr, wr_scr, wi_scr, us_scr, ug0_scr, ug1_scr, h_scr, tail_scr):
    i = pl.program_id(0)
    p = i - nchunk
    tc = SUB_ROWS
    half = nchunk // 2
    lru = (cw_ref, cb_ref, wr_scr, br_ref, wi_scr, bi_ref, lam_ref, h_scr, tail_scr)

    @pl.when(i < nchunk)
    def _():
        wb = w_ref[...].astype(BF16)
        w_scr[i] = wb
        xs = xs_ref[...]
        xsn = (xs * _rms_scale(xs) * g_ref[...]).astype(BF16)
        r = jnp.dot(xsn, wb, preferred_element_type=F32)
        gs_ref[...] = r

        @pl.when(i < half)
        def _():
            us_scr[i] = r

    @pl.when(i == nchunk - 1)
    def _():
        wr_scr[...] = (0.5 * wr_ref[...]).astype(BF16)
        wi_scr[...] = (0.5 * wi_ref[...]).astype(BF16)
        bw = LRU_BLOCK_W
        for n in range(LRU_BLOCKS):
            cs = slice(n * bw, (n + 1) * bw)
            c, off = divmod(n * bw, W_CHUNK)
            u = us_scr[c, :, off:off + bw]
            cw = cw_ref[:, cs]
            conv = cb_ref[:, cs]
            for tap in range(CONV_W - 1):
                conv = conv + scprev_ref[tap, :, cs] * cw[tap:tap + 1]
                if tap > 0:
                    scnew_ref[tap - 1, :, cs] = scprev_ref[tap, :, cs]
            conv = conv + u * cw[CONV_W - 1:]
            scnew_ref[CONV_W - 2, :, cs] = u
            a, b = _lru_gates(conv, wr_scr[n], br_ref[:, cs], wi_scr[n], bi_ref[:, cs], lam_ref[:, cs])
            hs_ref[:, cs] = a * sh0_ref[:, cs] + b
        ug1_scr[...] = jnp.zeros_like(ug1_scr)
        h_scr[...] = jnp.zeros_like(h_scr)
        tail_scr[...] = jnp.zeros_like(tail_scr)

    @pl.when(p >= 0)
    def _():
        _interleave((_in_proj_tile(x_ref, slice(0, tc), g_ref, w_scr, ug0_scr), 1),
                    (_rglru_tile(ug1_scr, y_odd_ref, *lru), SCAN_PIECES_PER_PROJ_PIECE))
        hlast_ref[0] = h_scr[...]
        cnew_ref[0] = tail_scr[...]

    @pl.when((p >= 0) & (p < npairs))
    def _():
        @pl.when((2 * p) % chunks == 0)
        def _():
            h_scr[...] = h0_ref[0]
            tail_scr[...] = cprev_ref[0]

        _interleave((_in_proj_tile(x_ref, slice(tc, 2 * tc), g_ref, w_scr, ug1_scr), 1),
                    (_rglru_tile(ug0_scr, y_even_ref, *lru), SCAN_PIECES_PER_PROJ_PIECE))


def rglru_front(xs, s_conv_prev, s_h0, x, conv_prev, h0, g, w_in, conv_w, conv_b, w_r, b_r, w_i, b_i, lam,
                seq_len):
    m, d = x.shape
    ns = xs.shape[0]
    w = w_in.shape[1] // 2
    tc = SUB_ROWS
    nchunk = w_in.shape[1] // W_CHUNK
    half = nchunk // 2
    bsz = m // seq_len
    chunks = seq_len // tc
    npairs = m // (2 * tc)
    assert seq_len % (2 * tc) == 0 and tc % (SUBLANES * SUBLANES) == 0 and tc // SUBLANES > CONV_W
    pair = lambda i: jnp.clip(i - nchunk, 0, npairs - 1)
    last = npairs * 2 - 1
    seq_in = lambda i: (jnp.clip(2 * (i - nchunk), 0, last) // chunks, 0, 0)
    seq_out = lambda i: (jnp.clip(2 * (i - nchunk) - 1, 0, last) // chunks, 0, 0)
    state_in = lambda rows: pl.BlockSpec((1, rows, w), seq_in)
    state_out = lambda rows: pl.BlockSpec((1, rows, w), seq_out)
    chunk = lambda i: (0, jnp.minimum(i, nchunk - 1))
    return pl.pallas_call(
        functools.partial(_rglru_front_kernel, nchunk, npairs, chunks),
        grid=(nchunk + npairs + 1,),
        in_specs=[_resident(xs.shape), pl.BlockSpec((2 * tc, d), lambda i: (pair(i), 0)), _resident((1, d)),
                  pl.BlockSpec((d, W_CHUNK), chunk), state_in(CONV_W - 1), state_in(1),
                  _resident(s_conv_prev.shape), _resident(s_h0.shape),
                  _resident((CONV_W, w)), _resident((1, w)), _resident(w_r.shape), _resident((1, w)),
                  _resident(w_i.shape), _resident((1, w)), _resident((1, w))],
        out_specs=[
            pl.BlockSpec((ns, W_CHUNK), lambda i: (0, jnp.clip(i - half, 0, half - 1))),
            pl.BlockSpec((ns, w), lambda i: (0, 0)),
            pl.BlockSpec(s_conv_prev.shape, lambda i: (0, 0, 0)),
            pl.BlockSpec((tc, w), lambda i: (pair(i), 0)),
            pl.BlockSpec((tc, w), lambda i: (jnp.clip(i - nchunk - 1, 0, npairs - 1), 0)),
            state_out(CONV_W - 1), state_out(1),
        ],
        out_shape=[
            jax.ShapeDtypeStruct((ns, w), F32),
            jax.ShapeDtypeStruct((ns, w), F32),
            jax.ShapeDtypeStruct(s_conv_prev.shape, F32),
            jax.ShapeDtypeStruct((m // 2, w), BF16),
            jax.ShapeDtypeStruct((m // 2, w), BF16),
            jax.ShapeDtypeStruct((bsz, CONV_W - 1, w), F32),
            jax.ShapeDtypeStruct((bsz, 1, w), F32),
        ],
        scratch_shapes=[pltpu.VMEM((nchunk, d, W_CHUNK), BF16),
                        pltpu.VMEM(w_r.shape, BF16), pltpu.VMEM(w_i.shape, BF16),
                        pltpu.VMEM((half, ns, W_CHUNK), F32),
                        pltpu.VMEM((tc, 2 * w), F32), pltpu.VMEM((tc, 2 * w), F32),
                        pltpu.VMEM((1, w), F32), pltpu.VMEM((CONV_W - 1, w), F32)],
        compiler_params=_params("arbitrary"),
        name="rglru_front",
    )(xs, x, g.reshape(1, d), w_in, conv_prev, h0.reshape(bsz, 1, w), s_conv_prev, s_h0,
      conv_w, conv_b.reshape(1, w), w_r, b_r.reshape(1, w), w_i, b_i.reshape(1, w), lam.reshape(1, w))


def _buckets(dist):
    n = jnp.maximum(dist, 0)
    max_exact = N_BUCKETS // 2
    nf = jnp.maximum(n, 1).astype(F32)
    large = max_exact + jnp.floor(jnp.log(nf / max_exact) / math.log(MAX_DISTANCE / max_exact)
                                  * (N_BUCKETS - max_exact)).astype(jnp.int32)
    large = jnp.minimum(large, N_BUCKETS - 1)
    return jnp.where(n < max_exact, n, large)


def _lookup(bucket, valid, table_ref, head):
    bias = jnp.zeros(bucket.shape, F32)
    for b in range(N_BUCKETS):
        bias = jnp.where(bucket == b, table_ref[b, head], bias)
    return jnp.where(valid, bias, NEG_INF)


def _bias_kernel(table_ref, sinks_ref, band_ref, sinkt_ref, past_ref, new_ref):
    hk = pl.program_id(0)
    span = 3 * BLOCK
    dist = (lax.broadcasted_iota(jnp.int32, (1, span), 1) + BLOCK) % span
    bucket = _buckets(dist)
    in_window = (dist >= 0) & (dist < WINDOW)
    key_row = lax.broadcasted_iota(jnp.int32, (ATT_KEYS, HALF_Q), 0)

    def band(head):
        row = _lookup(bucket, in_window, table_ref, head) * LOG2_E
        full = pltpu.roll(jnp.broadcast_to(row, (ATT_KEYS, span)), 0, axis=1, stride=1, stride_axis=0)
        return full[:, :HALF_Q]

    rows = past_ref.shape[2]
    d_past = rows - lax.broadcasted_iota(jnp.int32, (1, rows), 1)
    b_past = _buckets(d_past)
    ok_past = (d_past >= 0) & (d_past < WINDOW)
    d_new = jnp.zeros((1, LANES), jnp.int32)
    b_new = _buckets(d_new)
    for g in range(GROUP):
        head = hk * GROUP + g
        bias = band(head)
        for half in range(2):
            slot = _head_order(half).index(g)
            cs = slice(slot * HALF_Q, (slot + 1) * HALF_Q)
            band_ref[0, half, 0, :, cs] = bias
            prev_rows = BLOCK - half * HALF_Q
            band_ref[1, half, 0, :, cs] = jnp.where(key_row < prev_rows, NEG_INF, bias)
            sinkt_ref[half, 0, :, cs] = jnp.full((1, HALF_Q), sinks_ref[head] * LOG2_E, F32)
        past_ref[0, g:g + 1, :] = _lookup(b_past, ok_past, table_ref, head)
        new_ref[0, g:g + 1, :] = _lookup(b_new, d_new == 0, table_ref, head)


def bias_tables(table, sinks, past_rows):
    smem = pl.BlockSpec(memory_space=pltpu.SMEM)
    return pl.pallas_call(
        _bias_kernel,
        grid=(N_KV_HEADS,),
        in_specs=[smem, smem],
        out_specs=[
            pl.BlockSpec((2, 2, 1, ATT_KEYS, GROUP * HALF_Q), lambda h: (0, 0, h, 0, 0)),
            pl.BlockSpec((2, 1, 1, GROUP * HALF_Q), lambda h: (0, h, 0, 0)),
            pl.BlockSpec((1, GROUP, past_rows), lambda h: (h, 0, 0)),
            pl.BlockSpec((1, GROUP, LANES), lambda h: (h, 0, 0)),
        ],
        out_shape=[
            jax.ShapeDtypeStruct((2, 2, N_KV_HEADS, ATT_KEYS, GROUP * HALF_Q), F32),
            jax.ShapeDtypeStruct((2, N_KV_HEADS, 1, GROUP * HALF_Q), F32),
            jax.ShapeDtypeStruct((N_KV_HEADS, GROUP, past_rows), F32),
            jax.ShapeDtypeStruct((N_KV_HEADS, GROUP, LANES), F32),
        ],
        compiler_params=_params("parallel"),
        name="bias_tables",
    )(table, sinks)


def _band_attn_rounds(first_tile, q_ref, kp_ref, kc_ref, vp_ref, vc_ref, gate_ref, bias_ref, sink_ref,
                      y_ref, s_scr, p_scr):
    nt = (((1,), (1,)), ((), ()))
    low = (lax.broadcasted_iota(jnp.int32, (1, LANES), 1) < HEAD_DIM)
    keep_low = low.astype(BF16)
    keep_high = 1 - keep_low
    keep = (keep_low, keep_high)
    nkeys = 2 * BLOCK
    ones_rows = jnp.where(lax.broadcasted_iota(jnp.int32, (2 * SUBLANES, nkeys), 0) == 0,
                          1.0, 0.0).astype(BF16)
    rows = ATT_ROWS
    nslot = s_scr.shape[0]
    items = [(blk, hk, half) for blk in range(q_ref.shape[0] // BLOCK)
             for hk in range(N_KV_HEADS) for half in range(2)]
    assert nslot % 2 == 0

    def rows_of(blk):
        return slice(blk * BLOCK, (blk + 1) * BLOCK)

    def key_rows(half):
        return slice(half * HALF_Q, half * HALF_Q + ATT_KEYS)

    for slot in range(nslot):
        dead = slice(ATT_KEYS, nkeys) if slot % 2 == 0 else slice(0, HALF_Q)
        p_scr[slot, dead, :] = jnp.zeros((HALF_Q, p_scr.shape[2]), BF16)

    def scores(idx):
        blk, hk, half = items[idx]
        variant = first_tile if blk == 0 else 0
        cs = slice(hk * LANES, (hk + 1) * LANES)
        k_prev = kp_ref[:, cs] if blk == 0 else kc_ref[rows_of(blk - 1), cs]
        k_cur = kc_ref[rows_of(blk), cs]
        kd = (jnp.concatenate([k_prev, k_cur[:HALF_Q]], axis=0) if half == 0
              else jnp.concatenate([k_prev[HALF_Q:], k_cur], axis=0))
        q0 = blk * BLOCK + half * HALF_Q
        qs = jnp.concatenate(
            [q_ref[q0:q0 + HALF_Q, (hk * SLABS + h // HEADS_PER_TILE) * LANES:
                   (hk * SLABS + h // HEADS_PER_TILE + 1) * LANES] * keep[h % HEADS_PER_TILE]
             for h in _head_order(half)], axis=0)
        s = lax.dot_general(kd, qs, nt, preferred_element_type=F32) + bias_ref[variant, half, hk]
        s_scr[idx % nslot, key_rows(half), :] = s
        return jnp.maximum(jnp.max(s, axis=0, keepdims=True), sink_ref[half, hk])

    def softmax(idx, m):
        blk, hk, half = items[idx]
        slot = idx % nslot
        lo = half * HALF_Q
        for r in range(lo, lo + ATT_KEYS, rows):
            p_scr[slot, r:r + rows, :] = jnp.exp2(s_scr[slot, r:r + rows, :] - m).astype(BF16)
        return jnp.exp2(sink_ref[half, hk] - m)

    def weighted_values(idx):
        blk, hk, half = items[idx]
        vs = slice(hk * LANES, hk * LANES + HEAD_DIM)
        v_prev = vp_ref[vs, :] if blk == 0 else vc_ref[vs, rows_of(blk - 1)]
        vt = jnp.concatenate([v_prev, vc_ref[vs, rows_of(blk)]], axis=1)
        lhs_v = jnp.concatenate([vt, ones_rows], axis=0)
        return jnp.dot(lhs_v, p_scr[idx % nslot], preferred_element_type=F32)

    low_q = lax.broadcasted_iota(jnp.int32, (HEAD_DIM, LANES), 1) < HALF_Q

    def finish(blk, hk, ots, sink_ws):
        o = [ots[half][:HEAD_DIM] * (1.0 / (ots[half][HEAD_DIM:HEAD_DIM + 1] + sink_ws[half]))
             for half in range(2)]
        for sl in range(SLABS):
            a, b = (oh[:, sl * LANES:(sl + 1) * LANES] for oh in o)
            even = jnp.where(low_q, a, b)
            odd = pltpu.roll(jnp.where(low_q, b, a), HALF_Q, axis=1)
            pair = jnp.concatenate([even, odd], axis=0)
            c0 = (hk * SLABS + sl) * LANES
            y_ref[rows_of(blk), c0:c0 + LANES] = (
                pair.T * _silu(gate_ref[rows_of(blk), c0:c0 + LANES])).astype(y_ref.dtype)

    n = len(items)
    offs, sink_ws, outs = {}, {}, {}
    for k in range(-2 * ATT_SKEW, n + ATT_SKEW):
        if 0 <= k + 2 * ATT_SKEW < n:
            offs[k + 2 * ATT_SKEW] = scores(k + 2 * ATT_SKEW)
        if 0 <= k + ATT_SKEW < n:
            sink_ws[k + ATT_SKEW] = softmax(k + ATT_SKEW, offs.pop(k + ATT_SKEW))
        if 0 <= k < n:
            outs[k] = weighted_values(k)
        j = k - ATT_SKEW
        if 0 <= j < n and items[j][2] == 1:
            finish(items[j][0], items[j][1], [outs.pop(j - 1), outs.pop(j)],
                   [sink_ws.pop(j - 1), sink_ws.pop(j)])
        yield


def _proj_tile(y_scr, w_scr, g_ref, x_ref, o_ref, raw_scr):
    y = y_scr[...]
    for c in range(w_scr.shape[0]):
        for n0 in range(0, W_CHUNK, MXU_COLS):
            raw_scr[:, c * W_CHUNK + n0:c * W_CHUNK + n0 + MXU_COLS] = jnp.dot(
                y, w_scr[c, :, n0:n0 + MXU_COLS], preferred_element_type=F32)
            yield
    step = y_scr.shape[0] // PROJ_TAIL_PIECES
    for r0 in range(0, y_scr.shape[0], step):
        o = raw_scr[r0:r0 + step, :]
        o_ref[r0:r0 + step, :] = x_ref[r0:r0 + step, :] + o * _rms_scale(o) * g_ref[...]
        yield


def _attn_proj_kernel(nchunk, ntiles, tiles_per_seq, as_ref, gs_ref, xs_ref,
                      q_ref, kp_ref, kc_ref, vp_ref, vc_ref, gate_ref, bias_ref, sink_ref,
                      w_ref, g_ref, x_ref, os_ref, o_ref,
                      w_scr, raws_scr, s_scr, p_scr, ynew_scr, yold_scr, raw_scr):
    i = pl.program_id(0)
    p = i - nchunk

    @pl.when(i < nchunk)
    def _():
        wb = w_ref[...].astype(BF16)
        w_scr[i] = wb
        ys = (as_ref[...] * _silu(gs_ref[...])).astype(BF16)
        raws_scr[i] = jnp.dot(ys, wb, preferred_element_type=F32)

    @pl.when(i == nchunk - 1)
    def _():
        o = jnp.concatenate([raws_scr[c] for c in range(nchunk)], axis=1)
        os_ref[...] = xs_ref[...] + o * _rms_scale(o) * g_ref[...]
        yold_scr[...] = jnp.zeros_like(yold_scr)

    @pl.when(p >= 0)
    def _():
        tile = jnp.minimum(p, ntiles - 1)
        first = (tile % tiles_per_seq == 0).astype(jnp.int32)
        _interleave((_band_attn_rounds(first, q_ref, kp_ref, kc_ref, vp_ref, vc_ref, gate_ref, bias_ref,
                                       sink_ref, ynew_scr, s_scr, p_scr), ATT_ROUNDS_PER_PROJ_PIECE),
                    (_proj_tile(yold_scr, w_scr, g_ref, x_ref, o_ref, raw_scr), 1))
        yold_scr[...] = ynew_scr[...]


def attn_proj(a_s, gate_s, x_s, q, kdup, vt, gate, bias_band, sink_t, w, g, x, seq_len):
    k, d = w.shape
    m = x.shape[0]
    tm = ATT_PROJ_TILE
    nchunk = d // W_CHUNK
    ntiles = m // tm
    tiles_per_seq = seq_len // tm
    per_tile = tm // BLOCK
    assert seq_len % tm == 0 and tm % BLOCK == 0
    att_tile = lambda i: jnp.clip(i - nchunk, 0, ntiles - 1)
    proj_tile = lambda i: jnp.clip(i - nchunk - 1, 0, ntiles - 1)

    def before(i):
        t = att_tile(i)
        return per_tile * t - jnp.where(t % tiles_per_seq == 0, 0, 1)

    rows = lambda n: pl.BlockSpec((tm, n), lambda i: (att_tile(i), 0))
    score_tile = (2 * BLOCK, GROUP * HALF_Q)
    return pl.pallas_call(
        functools.partial(_attn_proj_kernel, nchunk, ntiles, tiles_per_seq),
        grid=(nchunk + ntiles + 1,),
        in_specs=[
            _resident(a_s.shape), _resident(gate_s.shape), _resident(x_s.shape),
            rows(ATT_WIDTH), pl.BlockSpec((BLOCK, 2 * KV_WIDTH), lambda i: (before(i), 0)),
            rows(2 * KV_WIDTH), pl.BlockSpec((2 * KV_WIDTH, BLOCK), lambda i: (0, before(i))),
            pl.BlockSpec((2 * KV_WIDTH, tm), lambda i: (0, att_tile(i))), rows(ATT_WIDTH),
            _resident(bias_band.shape), _resident(sink_t.shape),
            pl.BlockSpec((k, W_CHUNK), lambda i: (0, jnp.minimum(i, nchunk - 1))), _resident((1, d)),
            pl.BlockSpec((tm, d), lambda i: (proj_tile(i), 0)),
        ],
        out_specs=[pl.BlockSpec(x_s.shape, lambda i: (0, 0)),
                   pl.BlockSpec((tm, d), lambda i: (proj_tile(i), 0))],
        out_shape=[jax.ShapeDtypeStruct(x_s.shape, F32), jax.ShapeDtypeStruct((m, d), F32)],
        scratch_shapes=[pltpu.VMEM((nchunk, k, W_CHUNK), BF16),
                        pltpu.VMEM((nchunk, x_s.shape[0], W_CHUNK), F32),
                        pltpu.VMEM((ATT_SLOTS,) + score_tile, F32),
                        pltpu.VMEM((ATT_SLOTS,) + score_tile, BF16),
                        pltpu.VMEM((tm, k), BF16), pltpu.VMEM((tm, k), BF16),
                        pltpu.VMEM((tm, d), F32)],
        compiler_params=_params("arbitrary"),
        name="attn_proj",
    )(a_s, gate_s, x_s, q, kdup, kdup, vt, vt, gate, bias_band, sink_t, w, g.reshape(1, d), x)


def _cached_attn_kernel(q_ref, ckt_ref, cvt_ref, kn_ref, vn_ref, sinks_ref, bpast_ref, bnew_ref, o_ref):
    shape = (N_Q_HEADS, KV_WIDTH)
    lane_kv = lax.broadcasted_iota(jnp.int32, shape, 1) // HEAD_DIM
    row_kv = lax.broadcasted_iota(jnp.int32, shape, 0) // GROUP
    own = lane_kv == row_kv
    sink = sinks_ref[...]
    nt = (((1,), (1,)), ((), ()))
    for b in range(q_ref.shape[0]):
        q = q_ref[b]
        qt = jnp.concatenate([q] * N_KV_HEADS, axis=1)
        qm = jnp.where(own, qt, 0.0).astype(BF16)
        knew = kn_ref[b].astype(BF16).astype(F32)
        vnew = vn_ref[b].astype(BF16).astype(F32)
        s = jnp.dot(qm, ckt_ref[b].astype(BF16), preferred_element_type=F32) + bpast_ref[...]
        s_new = jnp.sum(qm.astype(F32) * knew, axis=-1, keepdims=True) + bnew_ref[:, :1]
        m = jnp.maximum(jnp.maximum(jnp.max(s, axis=-1, keepdims=True), s_new), sink)
        p = jnp.exp(s - m)
        p_new = jnp.exp(s_new - m)
        denom = jnp.sum(p, axis=-1, keepdims=True) + p_new + jnp.exp(sink - m)
        o_all = (lax.dot_general(p.astype(BF16), cvt_ref[b].astype(BF16), nt, preferred_element_type=F32)
                 + p_new.astype(BF16).astype(F32) * vnew)
        o_all = jnp.where(own, o_all, 0.0)
        o = o_all[:, :HEAD_DIM]
        for hk in range(1, N_KV_HEADS):
            o = o + o_all[:, hk * HEAD_DIM:(hk + 1) * HEAD_DIM]
        o_ref[b] = o / denom


def cached_attention(q, cache_kt, cache_vt, k_new, v_new, sinks, bias_past, bias_new):
    bsz, _, rows = cache_kt.shape
    nseq = math.gcd(bsz, SEQS_PER_STEP)
    per_seq = lambda r, n: pl.BlockSpec((nseq, r, n), lambda b: (b, 0, 0))
    return pl.pallas_call(
        _cached_attn_kernel,
        grid=(bsz // nseq,),
        in_specs=[
            per_seq(N_Q_HEADS, HEAD_DIM), per_seq(KV_WIDTH, rows), per_seq(KV_WIDTH, rows),
            per_seq(1, KV_WIDTH), per_seq(1, KV_WIDTH),
            _resident((N_Q_HEADS, 1)), _resident((N_Q_HEADS, rows)), _resident((N_Q_HEADS, LANES)),
        ],
        out_specs=per_seq(N_Q_HEADS, HEAD_DIM),
        out_shape=jax.ShapeDtypeStruct((bsz, N_Q_HEADS, HEAD_DIM), F32),
        compiler_params=_params("parallel"),
        name="cached_attention",
    )(q, cache_kt, cache_vt, k_new, v_new, sinks, bias_past, bias_new)


def kernel(x_prompt, x_sample, state_conv, state_h, cache_k, cache_v, a_norm_pre, a_norm_post,
           a_w_in, a_conv_w, a_conv_b, a_w_r, a_b_r, a_w_i, a_b_i, a_lambda, a_w_out, kv_norm, w_kv,
           b_norm_pre, b_norm_post, b_w_qg, b_sinks, b_w_out, rel_bias_table):
    bsz, t, d = x_prompt.shape
    dbsz, dt, _ = x_sample.shape
    assert a_w_in.shape[0] == 1 and b_w_qg.shape[0] == 1 and dt == 1
    assert t % BLOCK == 0 and t >= WINDOW
    past_rows = cache_k.shape[1]
    assert past_rows == min(WINDOW, PAST_LEN)

    sinks = b_sinks[0]
    bias_band, sink_t, bias_past, bias_new = bias_tables(rel_bias_table, sinks, past_rows)

    tm = 2 * SUB_ROWS
    xp = x_prompt.reshape(bsz * t, d)
    xs = x_sample.reshape(dbsz, d)

    conv0 = jnp.zeros((bsz, CONV_W - 1, LRU_WIDTH), F32)
    h0 = jnp.zeros((bsz, LRU_WIDTH), F32)
    gate_s, hs, s_conv_t, y_even, y_odd, p_conv, p_h = rglru_front(
        xs, jnp.transpose(state_conv[0], (1, 0, 2)), state_h[0], xp, conv0, h0, a_norm_pre[0], a_w_in[0],
        a_conv_w[0], a_conv_b[0], a_w_r[0], a_b_r[0], a_w_i[0], a_b_i[0], a_lambda[0], seq_len=t)
    xs1, x1 = proj_norm_res(hs, gate_s, xs, (y_even, y_odd), a_w_out[0], a_norm_post[0], xp, tm)

    ks, vs, qs, gate_sb, q, gate_b, kdup, vt, k_tail, v_tail = norm_proj_kvq(
        xs1, x1, kv_norm, b_norm_pre[0], w_kv, b_w_qg[0], SUB_ROWS, seq_len=t)
    cache_kt = jnp.transpose(cache_k, (0, 2, 3, 1)).reshape(dbsz, KV_WIDTH, past_rows)
    cache_vt = jnp.transpose(cache_v, (0, 2, 3, 1)).reshape(dbsz, KV_WIDTH, past_rows)
    os_ = cached_attention(qs.reshape(dbsz, N_Q_HEADS, HEAD_DIM), cache_kt, cache_vt,
                           ks.reshape(dbsz, 1, KV_WIDTH), vs.reshape(dbsz, 1, KV_WIDTH),
                           sinks.reshape(N_Q_HEADS, 1), bias_past.reshape(N_Q_HEADS, past_rows),
                           bias_new.reshape(N_Q_HEADS, LANES))
    y_sample, y_prompt = attn_proj(os_.reshape(dbsz, ATT_WIDTH), gate_sb, xs1, q, kdup, vt, gate_b,
                                   bias_band, sink_t, b_w_out[0], b_norm_post[0], x1, seq_len=t)
    y_prompt = y_prompt.reshape(bsz, t, d)
    p_k = jnp.transpose(k_tail.reshape(bsz, N_KV_HEADS, HEAD_DIM, WINDOW), (0, 3, 1, 2))
    p_v = jnp.transpose(v_tail.reshape(bsz, N_KV_HEADS, HEAD_DIM, WINDOW), (0, 3, 1, 2))

    return (y_prompt, y_sample.reshape(dbsz, 1, d),
            p_conv[None], p_h.reshape(1, bsz, LRU_WIDTH), p_k, p_v,
            jnp.transpose(s_conv_t, (1, 0, 2))[None], hs[None],
            ks.reshape(dbsz, 1, N_KV_HEADS, HEAD_DIM), vs.reshape(dbsz, 1, N_KV_HEADS, HEAD_DIM))
```

```python
import functools
import math

import jax
import jax.numpy as jnp
from jax import lax
from jax.experimental import pallas as pl
from jax.experimental.pallas import tpu as pltpu

F32 = jnp.float32
BF16 = jnp.bfloat16

D_MODEL = 2048
LRU_WIDTH = 2048
LRU_BLOCKS = 8
LRU_BLOCK_W = LRU_WIDTH // LRU_BLOCKS
CONV_W = 4
LRU_C = 8.0
HEAD_DIM = 64
N_Q_HEADS = 32
N_KV_HEADS = 8
GROUP = N_Q_HEADS // N_KV_HEADS
ATT_WIDTH = N_Q_HEADS * HEAD_DIM
KV_WIDTH = N_KV_HEADS * HEAD_DIM
WINDOW = 128
BLOCK = WINDOW
N_BUCKETS = 32
MAX_DISTANCE = 128
RMS_EPS = 1e-6
NEG_INF = -1e30
LOG2_E = 1.4426950408889634
PAST_LEN = 16384

V7X_VMEM_BYTES = 64 * 1024 * 1024
VMEM_LIMIT = V7X_VMEM_BYTES - 8 * 1024 * 1024
SUBLANES = 8
LANES = 128
HEADS_PER_TILE = LANES // HEAD_DIM
SLABS = GROUP // HEADS_PER_TILE
MXU_COLS = 256
SUB_ROWS = 256
W_CHUNK = 512
ATT_ROWS = 64
ATT_SKEW = 2
ATT_SLOTS = 2 * ATT_SKEW + 2
HALF_Q = BLOCK // 2
ATT_KEYS = WINDOW + HALF_Q
SEQS_PER_STEP = 8
SCAN_PIECES_PER_PROJ_PIECE = 2
ATT_PROJ_TILE = 2 * BLOCK
PROJ_TAIL_PIECES = 4
ATT_ROUNDS_PER_PROJ_PIECE = 4
SQRT_FLOOR = 1e-30


def _head_order(half):
    heads = list(range(GROUP))
    return heads if half == 0 else [h ^ 1 for h in heads]


def _params(*semantics):
    return pltpu.CompilerParams(dimension_semantics=semantics, vmem_limit_bytes=VMEM_LIMIT)


def _resident(shape):
    zeros = (0,) * len(shape)
    return pl.BlockSpec(shape, lambda *_: zeros, pipeline_mode=pl.Buffered(1))


def _rms_scale(x):
    return lax.rsqrt(jnp.mean(x * x, axis=-1, keepdims=True) + RMS_EPS)


def _silu(x):
    h = 0.5 * x
    return h * jnp.tanh(h) + h


def _segment_major(rows, inverse=False):
    seg = rows // SUBLANES
    r = lax.broadcasted_iota(jnp.int32, (rows, rows), 0)
    c = lax.broadcasted_iota(jnp.int32, (rows, rows), 1)
    if inverse:
        src = (r % seg) * SUBLANES + r // seg
    else:
        src = (r % SUBLANES) * seg + r // SUBLANES
    return jnp.where(c == src, 1.0, 0.0).astype(BF16)


def _phase_specs(nchunk, tm, k):
    chunk_w = pl.BlockSpec((k, W_CHUNK), lambda i: (0, jnp.minimum(i, nchunk - 1)))
    tile = lambda n: pl.BlockSpec((tm, n), lambda i: (jnp.maximum(i - nchunk, 0), 0))
    return chunk_w, tile


def _dup_heads(x):
    low = lax.broadcasted_iota(jnp.int32, (x.shape[0], LANES), 1) < HEAD_DIM
    out = []
    for c in range(x.shape[1] // LANES):
        col = x[:, c * LANES:(c + 1) * LANES]
        swapped = pltpu.roll(col, HEAD_DIM, axis=1)
        out += [jnp.where(low, col, swapped), jnp.where(low, swapped, col)]
    return jnp.concatenate(out, axis=1)


def _norm_proj_kvq_kernel(nkv, nqg, xs_ref, x_ref, gkv_ref, gq_ref, wkv_ref, wqg_ref,
                          ks_ref, vs_ref, qs_ref, gates_ref,
                          q_ref, gate_ref, kdup_ref, vt_ref, ktail_ref, vtail_ref,
                          wkv_scr, wqg_scr):
    i = pl.program_id(0)
    nchunk = nkv + nqg
    q_chunks = ATT_WIDTH // W_CHUNK
    q_scale = 1.0 / math.sqrt(HEAD_DIM)
    q_scale_log2 = q_scale * LOG2_E

    def sample_rows(g_ref):
        xs = xs_ref[...]
        return (xs * _rms_scale(xs) * g_ref[...]).astype(BF16)

    @pl.when(i < nkv)
    def _():
        wb = wkv_ref[...].astype(BF16)
        wkv_scr[i] = wb
        r = jnp.dot(sample_rows(gkv_ref), wb, preferred_element_type=F32)

        @pl.when(i == 0)
        def _():
            ks_ref[...] = r

        @pl.when(i == 1)
        def _():
            vs_ref[...] = r

    @pl.when((i >= nkv) & (i < nchunk))
    def _():
        c = i - nkv
        wb = wqg_ref[...].astype(BF16)
        wqg_scr[c] = wb
        r = jnp.dot(sample_rows(gq_ref), wb, preferred_element_type=F32)

        @pl.when(c < q_chunks)
        def _():
            qs_ref[...] = r * q_scale

        @pl.when(c >= q_chunks)
        def _():
            gates_ref[...] = r

    @pl.when(i >= nchunk)
    def _():
        tm = x_ref.shape[0]
        for rs in _row_blocks(tm):
            x = x_ref[rs, :]
            xh = x * _rms_scale(x)
            xkv = (xh * gkv_ref[...]).astype(BF16)
            xq = (xh * gq_ref[...]).astype(BF16)
            k = jnp.dot(xkv, wkv_scr[0], preferred_element_type=F32)
            v = jnp.dot(xkv, wkv_scr[1], preferred_element_type=F32)
            kdup_ref[rs, :] = _dup_heads(k).astype(BF16)
            vt_ref[:, rs] = _dup_heads(v).T.astype(BF16)
            for c in range(nqg):
                r = jnp.dot(xq, wqg_scr[c], preferred_element_type=F32)
                if c < q_chunks:
                    q_ref[rs, c * W_CHUNK:(c + 1) * W_CHUNK] = (r * q_scale_log2).astype(q_ref.dtype)
                else:
                    cc = c - q_chunks
                    gate_ref[rs, cc * W_CHUNK:(cc + 1) * W_CHUNK] = r
        ktail_ref[0] = k[k.shape[0] - WINDOW:].T
        vtail_ref[0] = v[v.shape[0] - WINDOW:].T


def norm_proj_kvq(xs, x, g_kv, g_q, w_kv, w_qg, tm, seq_len):
    m, d = x.shape
    ns = xs.shape[0]
    assert w_kv.shape[1] == 2 * KV_WIDTH == 2 * W_CHUNK and seq_len % tm == 0 and tm >= WINDOW
    nkv, nqg = w_kv.shape[1] // W_CHUNK, w_qg.shape[1] // W_CHUNK
    nchunk = nkv + nqg
    tiles = seq_len // tm
    tile = lambda n: pl.BlockSpec((tm, n), lambda i: (jnp.maximum(i - nchunk, 0), 0))
    tail = pl.BlockSpec((1, KV_WIDTH, WINDOW), lambda i: (jnp.maximum(i - nchunk, 0) // tiles, 0, 0))
    kv_chunk = lambda i: (0, jnp.minimum(i, nkv - 1))
    qg_chunk = lambda i: (0, jnp.clip(i - nkv, 0, nqg - 1))
    q_chunks = ATT_WIDTH // W_CHUNK
    whole_s = lambda n: pl.BlockSpec((ns, n), lambda i: (0, 0))
    return pl.pallas_call(
        functools.partial(_norm_proj_kvq_kernel, nkv, nqg),
        grid=(nchunk + m // tm,),
        in_specs=[
            _resident(xs.shape), tile(d), _resident((1, d)), _resident((1, d)),
            pl.BlockSpec((d, W_CHUNK), kv_chunk), pl.BlockSpec((d, W_CHUNK), qg_chunk),
        ],
        out_specs=[
            whole_s(KV_WIDTH), whole_s(KV_WIDTH),
            pl.BlockSpec((ns, W_CHUNK), lambda i: (0, jnp.clip(i - nkv, 0, q_chunks - 1))),
            pl.BlockSpec((ns, W_CHUNK), lambda i: (0, jnp.clip(i - nkv - q_chunks, 0, nqg - q_chunks - 1))),
            tile(ATT_WIDTH), tile(ATT_WIDTH), tile(2 * KV_WIDTH),
            pl.BlockSpec((2 * KV_WIDTH, tm), lambda i: (0, jnp.maximum(i - nchunk, 0))), tail, tail,
        ],
        out_shape=[
            jax.ShapeDtypeStruct((ns, KV_WIDTH), F32),
            jax.ShapeDtypeStruct((ns, KV_WIDTH), F32),
            jax.ShapeDtypeStruct((ns, ATT_WIDTH), F32),
            jax.ShapeDtypeStruct((ns, w_qg.shape[1] - ATT_WIDTH), F32),
            jax.ShapeDtypeStruct((m, ATT_WIDTH), BF16),
            jax.ShapeDtypeStruct((m, ATT_WIDTH), F32),
            jax.ShapeDtypeStruct((m, 2 * KV_WIDTH), BF16),
            jax.ShapeDtypeStruct((2 * KV_WIDTH, m), BF16),
            jax.ShapeDtypeStruct((m // seq_len, KV_WIDTH, WINDOW), F32),
            jax.ShapeDtypeStruct((m // seq_len, KV_WIDTH, WINDOW), F32),
        ],
        scratch_shapes=[pltpu.VMEM((nkv, d, W_CHUNK), BF16), pltpu.VMEM((nqg, d, W_CHUNK), BF16)],
        compiler_params=_params("arbitrary"),
        name="norm_proj_kvq",
    )(xs, x, g_kv.reshape(1, d), g_q.reshape(1, d), w_kv, w_qg)


def _row_blocks(rows):
    sub = min(rows, SUB_ROWS)
    return [slice(r, r + sub) for r in range(0, rows, sub)]


def _proj_norm_res_kernel(nchunk, nparts, as_ref, gs_ref, xs_ref, *refs):
    y_refs = refs[:nparts]
    w_ref, g_ref, x_ref, os_ref, o_ref, w_scr, raw_scr = refs[nparts:]
    i = pl.program_id(0)

    @pl.when(i < nchunk)
    def _():
        wb = w_ref[...].astype(BF16)
        w_scr[i] = wb
        ys = (as_ref[...] * _silu(gs_ref[...])).astype(BF16)
        raw_scr[i] = jnp.dot(ys, wb, preferred_element_type=F32)

    @pl.when(i == nchunk - 1)
    def _():
        o = jnp.concatenate([raw_scr[c] for c in range(nchunk)], axis=1)
        os_ref[...] = xs_ref[...] + o * _rms_scale(o) * g_ref[...]

    @pl.when(i >= nchunk)
    def _():
        tm = x_ref.shape[0]
        part_rows = tm // nparts

        for rs in _row_blocks(tm):
            part, off = divmod(rs.start, part_rows)
            y = y_refs[part][off:off + rs.stop - rs.start, :]
            o = jnp.concatenate([jnp.dot(y, w_scr[c], preferred_element_type=F32)
                                 for c in range(nchunk)], axis=1)
            o_ref[rs, :] = x_ref[rs, :] + o * _rms_scale(o) * g_ref[...]


def proj_norm_res(a_s, gate_s, x_s, y_parts, w, g, x, tm):
    k, d = w.shape
    m = x.shape[0]
    nchunk = d // W_CHUNK
    nparts = len(y_parts)
    assert (tm // nparts) % min(tm, SUB_ROWS) == 0
    chunk_w, tile = _phase_specs(nchunk, tm, k)
    part = pl.BlockSpec((tm // nparts, k), lambda i: (jnp.maximum(i - nchunk, 0), 0))
    return pl.pallas_call(
        functools.partial(_proj_norm_res_kernel, nchunk, nparts),
        grid=(nchunk + m // tm,),
        in_specs=[_resident(a_s.shape), _resident(gate_s.shape), _resident(x_s.shape)]
        + [part] * nparts + [chunk_w, _resident((1, d)), tile(d)],
        out_specs=[pl.BlockSpec(x_s.shape, lambda i: (0, 0)), tile(d)],
        out_shape=[jax.ShapeDtypeStruct(x_s.shape, F32), jax.ShapeDtypeStruct((m, d), F32)],
        scratch_shapes=[pltpu.VMEM((nchunk, k, W_CHUNK), BF16),
                        pltpu.VMEM((nchunk, x_s.shape[0], W_CHUNK), F32)],
        compiler_params=_params("arbitrary"),
        name="proj_norm_res",
    )(a_s, gate_s, x_s, *y_parts, w, g.reshape(1, d), x)


def _lru_gate_dots(conv, wr_half, wi_half):
    cb = conv.astype(BF16)
    return (jnp.dot(cb, wr_half, preferred_element_type=F32),
            jnp.dot(cb, wi_half, preferred_element_type=F32))


def _lru_gates(conv, wr_half, br, wi_half, bi, lam):
    return _lru_gate_math(conv, _lru_gate_dots(conv, wr_half, wi_half), br, bi, lam)


def _lru_gate_math(conv, half_pre, br, bi, lam):
    th_r = jnp.tanh(half_pre[0] + 0.5 * br)
    th_i = jnp.tanh(half_pre[1] + 0.5 * bi)
    nl = -lam
    softplus = jnp.maximum(nl, 0.0) + jnp.log1p(jnp.exp(-jnp.abs(nl)))
    half = (0.5 * LRU_C) * softplus
    x = th_r * half + half
    a = jnp.exp2(x * -LOG2_E)
    z = jnp.tanh(x) * (a * a + 1.0)
    mult = z * lax.rsqrt(jnp.maximum(z, SQRT_FLOOR))
    hc = 0.5 * conv
    return a, mult * (hc * th_i + hc)


def _interleave(*stages):
    live = [[stage, share] for stage, share in stages]
    while live:
        for entry in list(live):
            try:
                for _ in range(entry[1]):
                    next(entry[0])
            except StopIteration:
                live.remove(entry)


def _in_proj_tile(x_ref, rs, g_ref, w_scr, ug_ref):
    x = x_ref[rs, :]
    xn = (x * _rms_scale(x) * g_ref[...]).astype(BF16)
    xn = jnp.dot(_segment_major(xn.shape[0]), xn, preferred_element_type=F32).astype(BF16)
    for c in range(w_scr.shape[0]):
        for n0 in range(0, W_CHUNK, MXU_COLS):
            ug_ref[:, c * W_CHUNK + n0:c * W_CHUNK + n0 + MXU_COLS] = jnp.dot(
                xn, w_scr[c, :, n0:n0 + MXU_COLS], preferred_element_type=F32)
            yield


def _rglru_tile(ug_ref, y_ref, cw_ref, cb_ref, wr_ref, br_ref, wi_ref, bi_ref, lam_ref, h_scr, tail_scr):
    tc = ug_ref.shape[0]
    seg = tc // SUBLANES
    ntaps = CONV_W - 1
    bw = LRU_BLOCK_W
    sub = lax.broadcasted_iota(jnp.int32, (SUBLANES, bw), 0)
    first = sub == 0
    time_order = _segment_major(tc, inverse=True)

    def shift_in(x, row0):
        return jnp.where(first, row0, pltpu.roll(x, 1, axis=0))

    def group(x, j):
        return x[j * SUBLANES:(j + 1) * SUBLANES]

    def store_time_order(cols, y):
        y_ref[:, cols] = jnp.dot(time_order, y, preferred_element_type=F32).astype(y_ref.dtype)

    pending = None
    for n in range(LRU_BLOCKS):
        cs = slice(n * bw, (n + 1) * bw)
        u = ug_ref[:, cs]
        tail = tail_scr[:, cs]
        before = [shift_in(group(u, seg - m), tail[ntaps - m:ntaps - m + 1])
                  for m in range(ntaps, 0, -1)]
        ext = jnp.concatenate(before + [u], axis=0)
        tail_scr[:, cs] = jnp.concatenate(
            [group(u, seg - m)[SUBLANES - 1:] for m in range(ntaps, 0, -1)], axis=0)
        cw = cw_ref[:, cs]
        conv = cb_ref[:, cs]
        for tap in range(CONV_W):
            conv = conv + ext[tap * SUBLANES:tap * SUBLANES + tc] * cw[tap:tap + 1]
        yield

        half_pre = _lru_gate_dots(conv, wr_ref[n], wi_ref[n])
        yield

        if pending is not None:
            store_time_order(*pending)
        yield

        a, b = _lru_gate_math(conv, half_pre, br_ref[:, cs], bi_ref[:, cs], lam_ref[:, cs])

        h = b[:SUBLANES]
        acc = a[:SUBLANES]
        h_loc, a_cum = [h], [acc]
        for j in range(1, seg):
            sl = slice(j * SUBLANES, (j + 1) * SUBLANES)
            h = a[sl] * h + b[sl]
            acc = a[sl] * acc
            h_loc.append(h)
            a_cum.append(acc)

        step = 1
        while step < SUBLANES:
            keep = sub >= step
            h = jnp.where(keep, acc * pltpu.roll(h, step, axis=0) + h, h)
            acc = jnp.where(keep, acc * pltpu.roll(acc, step, axis=0), acc)
            step *= 2
        h_prev = h_scr[:, cs]
        after = h + acc * h_prev
        h_in = shift_in(after, h_prev)
        h_scr[:, cs] = after[SUBLANES - 1:]

        hs = jnp.concatenate([h_loc[j] + a_cum[j] * h_in for j in range(seg)], axis=0)
        y = (hs * _silu(ug_ref[:, LRU_WIDTH + n * bw:LRU_WIDTH + (n + 1) * bw])).astype(BF16)
        pending = (cs, y)
        yield

    store_time_order(*pending)
    yield


def _rglru_front_kernel(nchunk, npairs, chunks, xs_ref, x_ref, g_ref, w_ref, cprev_ref, h0_ref,
                        scprev_ref, sh0_ref, cw_ref, cb_ref, wr_ref, br_ref, wi_ref, bi_ref, lam_ref,
                        gs_ref, hs_ref, scnew_ref, y_even_ref, y_odd_ref, cnew_ref, hlast_ref,
                        w_scr, wr_scr, wi_scr, us_scr, ug0_scr, ug1_scr, h_scr, tail_scr):
    i = pl.program_id(0)
    p = i - nchunk
    tc = SUB_ROWS
    half = nchunk // 2
    lru = (cw_ref, cb_ref, wr_scr, br_ref, wi_scr, bi_ref, lam_ref, h_scr, tail_scr)

    @pl.when(i < nchunk)
    def _():
        wb = w_ref[...].astype(BF16)
        w_scr[i] = wb
        xs = xs_ref[...]
        xsn = (xs * _rms_scale(xs) * g_ref[...]).astype(BF16)
        r = jnp.dot(xsn, wb, preferred_element_type=F32)
        gs_ref[...] = r

        @pl.when(i < half)
        def _():
            us_scr[i] = r

    @pl.when(i == nchunk - 1)
    def _():
        wr_scr[...] = (0.5 * wr_ref[...]).astype(BF16)
        wi_scr[...] = (0.5 * wi_ref[...]).astype(BF16)
        bw = LRU_BLOCK_W
        for n in range(LRU_BLOCKS):
            cs = slice(n * bw, (n + 1) * bw)
            c, off = divmod(n * bw, W_CHUNK)
            u = us_scr[c, :, off:off + bw]
            cw = cw_ref[:, cs]
            conv = cb_ref[:, cs]
            for tap in range(CONV_W - 1):
                conv = conv + scprev_ref[tap, :, cs] * cw[tap:tap + 1]
                if tap > 0:
                    scnew_ref[tap - 1, :, cs] = scprev_ref[tap, :, cs]
            conv = conv + u * cw[CONV_W - 1:]
            scnew_ref[CONV_W - 2, :, cs] = u
            a, b = _lru_gates(conv, wr_scr[n], br_ref[:, cs], wi_scr[n], bi_ref[:, cs], lam_ref[:, cs])
            hs_ref[:, cs] = a * sh0_ref[:, cs] + b

    def project_even():
        return _in_proj_tile(x_ref, slice(0, tc), g_ref, w_scr, ug0_scr)

    def scan_odd():
        return _rglru_tile(ug1_scr, y_odd_ref, *lru)

    @pl.when(p == 0)
    def _():
        _interleave((project_even(), 1))

    @pl.when((p > 0) & (p < npairs))
    def _():
        _interleave((project_even(), 1), (scan_odd(), SCAN_PIECES_PER_PROJ_PIECE))

    @pl.when(p == npairs)
    def _():
        _interleave((scan_odd(), 1))

    @pl.when(p > 0)
    def _():
        hlast_ref[0] = h_scr[...]
        cnew_ref[0] = tail_scr[...]

    @pl.when((p >= 0) & (p < npairs))
    def _():
        @pl.when((2 * p) % chunks == 0)
        def _():
            h_scr[...] = h0_ref[0]
            tail_scr[...] = cprev_ref[0]

        _interleave((_in_proj_tile(x_ref, slice(tc, 2 * tc), g_ref, w_scr, ug1_scr), 1),
                    (_rglru_tile(ug0_scr, y_even_ref, *lru), SCAN_PIECES_PER_PROJ_PIECE))


def rglru_front(xs, s_conv_prev, s_h0, x, conv_prev, h0, g, w_in, conv_w, conv_b, w_r, b_r, w_i, b_i, lam,
                seq_len):
    m, d = x.shape
    ns = xs.shape[0]
    w = w_in.shape[1] // 2
    tc = SUB_ROWS
    nchunk = w_in.shape[1] // W_CHUNK
    half = nchunk // 2
    bsz = m // seq_len
    chunks = seq_len // tc
    npairs = m // (2 * tc)
    assert seq_len % (2 * tc) == 0 and tc % (SUBLANES * SUBLANES) == 0 and tc // SUBLANES > CONV_W
    pair = lambda i: jnp.clip(i - nchunk, 0, npairs - 1)
    last = npairs * 2 - 1
    seq_in = lambda i: (jnp.clip(2 * (i - nchunk), 0, last) // chunks, 0, 0)
    seq_out = lambda i: (jnp.clip(2 * (i - nchunk) - 1, 0, last) // chunks, 0, 0)
    state_in = lambda rows: pl.BlockSpec((1, rows, w), seq_in)
    state_out = lambda rows: pl.BlockSpec((1, rows, w), seq_out)
    chunk = lambda i: (0, jnp.minimum(i, nchunk - 1))
    return pl.pallas_call(
        functools.partial(_rglru_front_kernel, nchunk, npairs, chunks),
        grid=(nchunk + npairs + 1,),
        in_specs=[_resident(xs.shape), pl.BlockSpec((2 * tc, d), lambda i: (pair(i), 0)), _resident((1, d)),
                  pl.BlockSpec((d, W_CHUNK), chunk), state_in(CONV_W - 1), state_in(1),
                  _resident(s_conv_prev.shape), _resident(s_h0.shape),
                  _resident((CONV_W, w)), _resident((1, w)), _resident(w_r.shape), _resident((1, w)),
                  _resident(w_i.shape), _resident((1, w)), _resident((1, w))],
        out_specs=[
            pl.BlockSpec((ns, W_CHUNK), lambda i: (0, jnp.clip(i - half, 0, half - 1))),
            pl.BlockSpec((ns, w), lambda i: (0, 0)),
            pl.BlockSpec(s_conv_prev.shape, lambda i: (0, 0, 0)),
            pl.BlockSpec((tc, w), lambda i: (pair(i), 0)),
            pl.BlockSpec((tc, w), lambda i: (jnp.clip(i - nchunk - 1, 0, npairs - 1), 0)),
            state_out(CONV_W - 1), state_out(1),
        ],
        out_shape=[
            jax.ShapeDtypeStruct((ns, w), F32),
            jax.ShapeDtypeStruct((ns, w), F32),
            jax.ShapeDtypeStruct(s_conv_prev.shape, F32),
            jax.ShapeDtypeStruct((m // 2, w), BF16),
            jax.ShapeDtypeStruct((m // 2, w), BF16),
            jax.ShapeDtypeStruct((bsz, CONV_W - 1, w), F32),
            jax.ShapeDtypeStruct((bsz, 1, w), F32),
        ],
        scratch_shapes=[pltpu.VMEM((nchunk, d, W_CHUNK), BF16),
                        pltpu.VMEM(w_r.shape, BF16), pltpu.VMEM(w_i.shape, BF16),
                        pltpu.VMEM((half, ns, W_CHUNK), F32),
                        pltpu.VMEM((tc, 2 * w), F32), pltpu.VMEM((tc, 2 * w), F32),
                        pltpu.VMEM((1, w), F32), pltpu.VMEM((CONV_W - 1, w), F32)],
        compiler_params=_params("arbitrary"),
        name="rglru_front",
    )(xs, x, g.reshape(1, d), w_in, conv_prev, h0.reshape(bsz, 1, w), s_conv_prev, s_h0,
      conv_w, conv_b.reshape(1, w), w_r, b_r.reshape(1, w), w_i, b_i.reshape(1, w), lam.reshape(1, w))


def _buckets(dist):
    n = jnp.maximum(dist, 0)
    max_exact = N_BUCKETS // 2
    nf = jnp.maximum(n, 1).astype(F32)
    large = max_exact + jnp.floor(jnp.log(nf / max_exact) / math.log(MAX_DISTANCE / max_exact)
                                  * (N_BUCKETS - max_exact)).astype(jnp.int32)
    large = jnp.minimum(large, N_BUCKETS - 1)
    return jnp.where(n < max_exact, n, large)


def _lookup(bucket, valid, table_ref, head):
    bias = jnp.zeros(bucket.shape, F32)
    for b in range(N_BUCKETS):
        bias = jnp.where(bucket == b, table_ref[b, head], bias)
    return jnp.where(valid, bias, NEG_INF)


def _bias_kernel(table_ref, sinks_ref, band_ref, sinkt_ref, past_ref, new_ref):
    hk = pl.program_id(0)
    span = 3 * BLOCK
    dist = (lax.broadcasted_iota(jnp.int32, (1, span), 1) + BLOCK) % span
    bucket = _buckets(dist)
    in_window = (dist >= 0) & (dist < WINDOW)
    key_row = lax.broadcasted_iota(jnp.int32, (ATT_KEYS, HALF_Q), 0)

    def band(head):
        row = _lookup(bucket, in_window, table_ref, head) * LOG2_E
        full = pltpu.roll(jnp.broadcast_to(row, (ATT_KEYS, span)), 0, axis=1, stride=1, stride_axis=0)
        return full[:, :HALF_Q]

    rows = past_ref.shape[2]
    d_past = rows - lax.broadcasted_iota(jnp.int32, (1, rows), 1)
    b_past = _buckets(d_past)
    ok_past = (d_past >= 0) & (d_past < WINDOW)
    d_new = jnp.zeros((1, LANES), jnp.int32)
    b_new = _buckets(d_new)
    for g in range(GROUP):
        head = hk * GROUP + g
        bias = band(head)
        for half in range(2):
            slot = _head_order(half).index(g)
            cs = slice(slot * HALF_Q, (slot + 1) * HALF_Q)
            band_ref[0, half, 0, :, cs] = bias
            prev_rows = BLOCK - half * HALF_Q
            band_ref[1, half, 0, :, cs] = jnp.where(key_row < prev_rows, NEG_INF, bias)
            sinkt_ref[half, 0, :, cs] = jnp.full((1, HALF_Q), sinks_ref[head] * LOG2_E, F32)
        past_ref[0, g:g + 1, :] = _lookup(b_past, ok_past, table_ref, head)
        new_ref[0, g:g + 1, :] = _lookup(b_new, d_new == 0, table_ref, head)


def bias_tables(table, sinks, past_rows):
    smem = pl.BlockSpec(memory_space=pltpu.SMEM)
    return pl.pallas_call(
        _bias_kernel,
        grid=(N_KV_HEADS,),
        in_specs=[smem, smem],
        out_specs=[
            pl.BlockSpec((2, 2, 1, ATT_KEYS, GROUP * HALF_Q), lambda h: (0, 0, h, 0, 0)),
            pl.BlockSpec((2, 1, 1, GROUP * HALF_Q), lambda h: (0, h, 0, 0)),
            pl.BlockSpec((1, GROUP, past_rows), lambda h: (h, 0, 0)),
            pl.BlockSpec((1, GROUP, LANES), lambda h: (h, 0, 0)),
        ],
        out_shape=[
            jax.ShapeDtypeStruct((2, 2, N_KV_HEADS, ATT_KEYS, GROUP * HALF_Q), F32),
            jax.ShapeDtypeStruct((2, N_KV_HEADS, 1, GROUP * HALF_Q), F32),
            jax.ShapeDtypeStruct((N_KV_HEADS, GROUP, past_rows), F32),
            jax.ShapeDtypeStruct((N_KV_HEADS, GROUP, LANES), F32),
        ],
        compiler_params=_params("parallel"),
        name="bias_tables",
    )(table, sinks)


def _band_attn_rounds(first_tile, q_ref, kp_ref, kc_ref, vp_ref, vc_ref, gate_ref, bias_ref, sink_ref,
                      y_ref, s_scr, p_scr):
    nt = (((1,), (1,)), ((), ()))
    low = (lax.broadcasted_iota(jnp.int32, (1, LANES), 1) < HEAD_DIM)
    keep_low = low.astype(BF16)
    keep_high = 1 - keep_low
    keep = (keep_low, keep_high)
    nkeys = 2 * BLOCK
    ones_rows = jnp.where(lax.broadcasted_iota(jnp.int32, (2 * SUBLANES, nkeys), 0) == 0,
                          1.0, 0.0).astype(BF16)
    rows = ATT_ROWS
    nslot = s_scr.shape[0]
    items = [(blk, hk, half) for blk in range(q_ref.shape[0] // BLOCK)
             for hk in range(N_KV_HEADS) for half in range(2)]
    assert nslot % 2 == 0

    def rows_of(blk):
        return slice(blk * BLOCK, (blk + 1) * BLOCK)

    def key_rows(half):
        return slice(half * HALF_Q, half * HALF_Q + ATT_KEYS)

    for slot in range(nslot):
        dead = slice(ATT_KEYS, nkeys) if slot % 2 == 0 else slice(0, HALF_Q)
        p_scr[slot, dead, :] = jnp.zeros((HALF_Q, p_scr.shape[2]), BF16)

    def scores(idx):
        blk, hk, half = items[idx]
        variant = first_tile if blk == 0 else 0
        cs = slice(hk * LANES, (hk + 1) * LANES)
        k_prev = kp_ref[:, cs] if blk == 0 else kc_ref[rows_of(blk - 1), cs]
        k_cur = kc_ref[rows_of(blk), cs]
        kd = (jnp.concatenate([k_prev, k_cur[:HALF_Q]], axis=0) if half == 0
              else jnp.concatenate([k_prev[HALF_Q:], k_cur], axis=0))
        q0 = blk * BLOCK + half * HALF_Q
        qs = jnp.concatenate(
            [q_ref[q0:q0 + HALF_Q, (hk * SLABS + h // HEADS_PER_TILE) * LANES:
                   (hk * SLABS + h // HEADS_PER_TILE + 1) * LANES] * keep[h % HEADS_PER_TILE]
             for h in _head_order(half)], axis=0)
        s = lax.dot_general(kd, qs, nt, preferred_element_type=F32) + bias_ref[variant, half, hk]
        s_scr[idx % nslot, key_rows(half), :] = s
        return jnp.maximum(jnp.max(s, axis=0, keepdims=True), sink_ref[half, hk])

    def softmax(idx, m):
        blk, hk, half = items[idx]
        slot = idx % nslot
        lo = half * HALF_Q
        for r in range(lo, lo + ATT_KEYS, rows):
            p_scr[slot, r:r + rows, :] = jnp.exp2(s_scr[slot, r:r + rows, :] - m).astype(BF16)
        return jnp.exp2(sink_ref[half, hk] - m)

    def weighted_values(idx):
        blk, hk, half = items[idx]
        vs = slice(hk * LANES, hk * LANES + HEAD_DIM)
        v_prev = vp_ref[vs, :] if blk == 0 else vc_ref[vs, rows_of(blk - 1)]
        vt = jnp.concatenate([v_prev, vc_ref[vs, rows_of(blk)]], axis=1)
        lhs_v = jnp.concatenate([vt, ones_rows], axis=0)
        return jnp.dot(lhs_v, p_scr[idx % nslot], preferred_element_type=F32)

    low_q = lax.broadcasted_iota(jnp.int32, (HEAD_DIM, LANES), 1) < HALF_Q

    def finish(blk, hk, ots, sink_ws):
        o = [ots[half][:HEAD_DIM] * (1.0 / (ots[half][HEAD_DIM:HEAD_DIM + 1] + sink_ws[half]))
             for half in range(2)]
        for sl in range(SLABS):
            a, b = (oh[:, sl * LANES:(sl + 1) * LANES] for oh in o)
            even = jnp.where(low_q, a, b)
            odd = pltpu.roll(jnp.where(low_q, b, a), HALF_Q, axis=1)
            pair = jnp.concatenate([even, odd], axis=0)
            c0 = (hk * SLABS + sl) * LANES
            y_ref[rows_of(blk), c0:c0 + LANES] = (
                pair.T * _silu(gate_ref[rows_of(blk), c0:c0 + LANES])).astype(y_ref.dtype)

    n = len(items)
    offs, sink_ws, outs = {}, {}, {}
    for k in range(-2 * ATT_SKEW, n + ATT_SKEW):
        if 0 <= k + 2 * ATT_SKEW < n:
            offs[k + 2 * ATT_SKEW] = scores(k + 2 * ATT_SKEW)
        if 0 <= k + ATT_SKEW < n:
            sink_ws[k + ATT_SKEW] = softmax(k + ATT_SKEW, offs.pop(k + ATT_SKEW))
        if 0 <= k < n:
            outs[k] = weighted_values(k)
        j = k - ATT_SKEW
        if 0 <= j < n and items[j][2] == 1:
            finish(items[j][0], items[j][1], [outs.pop(j - 1), outs.pop(j)],
                   [sink_ws.pop(j - 1), sink_ws.pop(j)])
        yield


def _proj_tile(y_scr, w_scr, g_ref, x_ref, o_ref, raw_scr):
    y = y_scr[...]
    for c in range(w_scr.shape[0]):
        for n0 in range(0, W_CHUNK, MXU_COLS):
            raw_scr[:, c * W_CHUNK + n0:c * W_CHUNK + n0 + MXU_COLS] = jnp.dot(
                y, w_scr[c, :, n0:n0 + MXU_COLS], preferred_element_type=F32)
            yield
    step = y_scr.shape[0] // PROJ_TAIL_PIECES
    for r0 in range(0, y_scr.shape[0], step):
        o = raw_scr[r0:r0 + step, :]
        o_ref[r0:r0 + step, :] = x_ref[r0:r0 + step, :] + o * _rms_scale(o) * g_ref[...]
        yield


def _attn_proj_kernel(nchunk, ntiles, tiles_per_seq, as_ref, gs_ref, xs_ref,
                      q_ref, kp_ref, kc_ref, vp_ref, vc_ref, gate_ref, bias_ref, sink_ref,
                      w_ref, g_ref, x_ref, os_ref, o_ref,
                      w_scr, raws_scr, s_scr, p_scr, ynew_scr, yold_scr, raw_scr):
    i = pl.program_id(0)
    p = i - nchunk

    @pl.when(i < nchunk)
    def _():
        wb = w_ref[...].astype(BF16)
        w_scr[i] = wb
        ys = (as_ref[...] * _silu(gs_ref[...])).astype(BF16)
        raws_scr[i] = jnp.dot(ys, wb, preferred_element_type=F32)

    @pl.when(i == nchunk - 1)
    def _():
        o = jnp.concatenate([raws_scr[c] for c in range(nchunk)], axis=1)
        os_ref[...] = xs_ref[...] + o * _rms_scale(o) * g_ref[...]

    def attend():
        first = (p % tiles_per_seq == 0).astype(jnp.int32)
        return _band_attn_rounds(first, q_ref, kp_ref, kc_ref, vp_ref, vc_ref, gate_ref, bias_ref,
                                 sink_ref, ynew_scr, s_scr, p_scr)

    def project():
        return _proj_tile(yold_scr, w_scr, g_ref, x_ref, o_ref, raw_scr)

    @pl.when(p == 0)
    def _():
        _interleave((attend(), 1))
        yold_scr[...] = ynew_scr[...]

    @pl.when((p > 0) & (p < ntiles))
    def _():
        _interleave((attend(), ATT_ROUNDS_PER_PROJ_PIECE), (project(), 1))
        yold_scr[...] = ynew_scr[...]

    @pl.when(p == ntiles)
    def _():
        _interleave((project(), 1))


def attn_proj(a_s, gate_s, x_s, q, kdup, vt, gate, bias_band, sink_t, w, g, x, seq_len):
    k, d = w.shape
    m = x.shape[0]
    tm = ATT_PROJ_TILE
    nchunk = d // W_CHUNK
    ntiles = m // tm
    tiles_per_seq = seq_len // tm
    per_tile = tm // BLOCK
    assert seq_len % tm == 0 and tm % BLOCK == 0
    att_tile = lambda i: jnp.clip(i - nchunk, 0, ntiles - 1)
    proj_tile = lambda i: jnp.clip(i - nchunk - 1, 0, ntiles - 1)

    def before(i):
        t = att_tile(i)
        return per_tile * t - jnp.where(t % tiles_per_seq == 0, 0, 1)

    rows = lambda n: pl.BlockSpec((tm, n), lambda i: (att_tile(i), 0))
    score_tile = (2 * BLOCK, GROUP * HALF_Q)
    return pl.pallas_call(
        functools.partial(_attn_proj_kernel, nchunk, ntiles, tiles_per_seq),
        grid=(nchunk + ntiles + 1,),
        in_specs=[
            _resident(a_s.shape), _resident(gate_s.shape), _resident(x_s.shape),
            rows(ATT_WIDTH), pl.BlockSpec((BLOCK, 2 * KV_WIDTH), lambda i: (before(i), 0)),
            rows(2 * KV_WIDTH), pl.BlockSpec((2 * KV_WIDTH, BLOCK), lambda i: (0, before(i))),
            pl.BlockSpec((2 * KV_WIDTH, tm), lambda i: (0, att_tile(i))), rows(ATT_WIDTH),
            _resident(bias_band.shape), _resident(sink_t.shape),
            pl.BlockSpec((k, W_CHUNK), lambda i: (0, jnp.minimum(i, nchunk - 1))), _resident((1, d)),
            pl.BlockSpec((tm, d), lambda i: (proj_tile(i), 0)),
        ],
        out_specs=[pl.BlockSpec(x_s.shape, lambda i: (0, 0)),
                   pl.BlockSpec((tm, d), lambda i: (proj_tile(i), 0))],
        out_shape=[jax.ShapeDtypeStruct(x_s.shape, F32), jax.ShapeDtypeStruct((m, d), F32)],
        scratch_shapes=[pltpu.VMEM((nchunk, k, W_CHUNK), BF16),
                        pltpu.VMEM((nchunk, x_s.shape[0], W_CHUNK), F32),
                        pltpu.VMEM((ATT_SLOTS,) + score_tile, F32),
                        pltpu.VMEM((ATT_SLOTS,) + score_tile, BF16),
                        pltpu.VMEM((tm, k), BF16), pltpu.VMEM((tm, k), BF16),
                        pltpu.VMEM((tm, d), F32)],
        compiler_params=_params("arbitrary"),
        name="attn_proj",
    )(a_s, gate_s, x_s, q, kdup, kdup, vt, vt, gate, bias_band, sink_t, w, g.reshape(1, d), x)


def _cached_attn_kernel(q_ref, ckt_ref, cvt_ref, kn_ref, vn_ref, sinks_ref, bpast_ref, bnew_ref, o_ref):
    shape = (N_Q_HEADS, KV_WIDTH)
    lane_kv = lax.broadcasted_iota(jnp.int32, shape, 1) // HEAD_DIM
    row_kv = lax.broadcasted_iota(jnp.int32, shape, 0) // GROUP
    own = lane_kv == row_kv
    sink = sinks_ref[...]
    nt = (((1,), (1,)), ((), ()))
    for b in range(q_ref.shape[0]):
        q = q_ref[b]
        qt = jnp.concatenate([q] * N_KV_HEADS, axis=1)
        qm = jnp.where(own, qt, 0.0).astype(BF16)
        knew = kn_ref[b].astype(BF16).astype(F32)
        vnew = vn_ref[b].astype(BF16).astype(F32)
        s = jnp.dot(qm, ckt_ref[b].astype(BF16), preferred_element_type=F32) + bpast_ref[...]
        s_new = jnp.sum(qm.astype(F32) * knew, axis=-1, keepdims=True) + bnew_ref[:, :1]
        m = jnp.maximum(jnp.maximum(jnp.max(s, axis=-1, keepdims=True), s_new), sink)
        p = jnp.exp(s - m)
        p_new = jnp.exp(s_new - m)
        denom = jnp.sum(p, axis=-1, keepdims=True) + p_new + jnp.exp(sink - m)
        o_all = (lax.dot_general(p.astype(BF16), cvt_ref[b].astype(BF16), nt, preferred_element_type=F32)
                 + p_new.astype(BF16).astype(F32) * vnew)
        o_all = jnp.where(own, o_all, 0.0)
        o = o_all[:, :HEAD_DIM]
        for hk in range(1, N_KV_HEADS):
            o = o + o_all[:, hk * HEAD_DIM:(hk + 1) * HEAD_DIM]
        o_ref[b] = o / denom


def cached_attention(q, cache_kt, cache_vt, k_new, v_new, sinks, bias_past, bias_new):
    bsz, _, rows = cache_kt.shape
    nseq = math.gcd(bsz, SEQS_PER_STEP)
    per_seq = lambda r, n: pl.BlockSpec((nseq, r, n), lambda b: (b, 0, 0))
    return pl.pallas_call(
        _cached_attn_kernel,
        grid=(bsz // nseq,),
        in_specs=[
            per_seq(N_Q_HEADS, HEAD_DIM), per_seq(KV_WIDTH, rows), per_seq(KV_WIDTH, rows),
            per_seq(1, KV_WIDTH), per_seq(1, KV_WIDTH),
            _resident((N_Q_HEADS, 1)), _resident((N_Q_HEADS, rows)), _resident((N_Q_HEADS, LANES)),
        ],
        out_specs=per_seq(N_Q_HEADS, HEAD_DIM),
        out_shape=jax.ShapeDtypeStruct((bsz, N_Q_HEADS, HEAD_DIM), F32),
        compiler_params=_params("parallel"),
        name="cached_attention",
    )(q, cache_kt, cache_vt, k_new, v_new, sinks, bias_past, bias_new)


def kernel(x_prompt, x_sample, state_conv, state_h, cache_k, cache_v, a_norm_pre, a_norm_post,
           a_w_in, a_conv_w, a_conv_b, a_w_r, a_b_r, a_w_i, a_b_i, a_lambda, a_w_out, kv_norm, w_kv,
           b_norm_pre, b_norm_post, b_w_qg, b_sinks, b_w_out, rel_bias_table):
    bsz, t, d = x_prompt.shape
    dbsz, dt, _ = x_sample.shape
    assert a_w_in.shape[0] == 1 and b_w_qg.shape[0] == 1 and dt == 1
    assert t % BLOCK == 0 and t >= WINDOW
    past_rows = cache_k.shape[1]
    assert past_rows == min(WINDOW, PAST_LEN)

    sinks = b_sinks[0]
    bias_band, sink_t, bias_past, bias_new = bias_tables(rel_bias_table, sinks, past_rows)

    tm = 2 * SUB_ROWS
    xp = x_prompt.reshape(bsz * t, d)
    xs = x_sample.reshape(dbsz, d)

    conv0 = jnp.zeros((bsz, CONV_W - 1, LRU_WIDTH), F32)
    h0 = jnp.zeros((bsz, LRU_WIDTH), F32)
    gate_s, hs, s_conv_t, y_even, y_odd, p_conv, p_h = rglru_front(
        xs, jnp.transpose(state_conv[0], (1, 0, 2)), state_h[0], xp, conv0, h0, a_norm_pre[0], a_w_in[0],
        a_conv_w[0], a_conv_b[0], a_w_r[0], a_b_r[0], a_w_i[0], a_b_i[0], a_lambda[0], seq_len=t)
    xs1, x1 = proj_norm_res(hs, gate_s, xs, (y_even, y_odd), a_w_out[0], a_norm_post[0], xp, tm)

    ks, vs, qs, gate_sb, q, gate_b, kdup, vt, k_tail, v_tail = norm_proj_kvq(
        xs1, x1, kv_norm, b_norm_pre[0], w_kv, b_w_qg[0], SUB_ROWS, seq_len=t)
    cache_kt = jnp.transpose(cache_k, (0, 2, 3, 1)).reshape(dbsz, KV_WIDTH, past_rows)
    cache_vt = jnp.transpose(cache_v, (0, 2, 3, 1)).reshape(dbsz, KV_WIDTH, past_rows)
    os_ = cached_attention(qs.reshape(dbsz, N_Q_HEADS, HEAD_DIM), cache_kt, cache_vt,
                           ks.reshape(dbsz, 1, KV_WIDTH), vs.reshape(dbsz, 1, KV_WIDTH),
                           sinks.reshape(N_Q_HEADS, 1), bias_past.reshape(N_Q_HEADS, past_rows),
                           bias_new.reshape(N_Q_HEADS, LANES))
    y_sample, y_prompt = attn_proj(os_.reshape(dbsz, ATT_WIDTH), gate_sb, xs1, q, kdup, vt, gate_b,
                                   bias_band, sink_t, b_w_out[0], b_norm_post[0], x1, seq_len=t)
    y_prompt = y_prompt.reshape(bsz, t, d)
    p_k = jnp.transpose(k_tail.reshape(bsz, N_KV_HEADS, HEAD_DIM, WINDOW), (0, 3, 1, 2))
    p_v = jnp.transpose(v_tail.reshape(bsz, N_KV_HEADS, HEAD_DIM, WINDOW), (0, 3, 1, 2))

    return (y_prompt, y_sample.reshape(dbsz, 1, d),
            p_conv[None], p_h.reshape(1, bsz, LRU_WIDTH), p_k, p_v,
            jnp.transpose(s_conv_t, (1, 0, 2))[None], hs[None],
            ks.reshape(dbsz, 1, N_KV_HEADS, HEAD_DIM), vs.reshape(dbsz, 1, N_KV_HEADS, HEAD_DIM))
```

```python
import functools
import math

import jax
import jax.numpy as jnp
from jax import lax
from jax.experimental import pallas as pl
from jax.experimental.pallas import tpu as pltpu

F32 = jnp.float32
BF16 = jnp.bfloat16

D_MODEL = 2048
LRU_WIDTH = 2048
LRU_BLOCKS = 8
LRU_BLOCK_W = LRU_WIDTH // LRU_BLOCKS
CONV_W = 4
LRU_C = 8.0
HEAD_DIM = 64
N_Q_HEADS = 32
N_KV_HEADS = 8
GROUP = N_Q_HEADS // N_KV_HEADS
ATT_WIDTH = N_Q_HEADS * HEAD_DIM
KV_WIDTH = N_KV_HEADS * HEAD_DIM
WINDOW = 128
BLOCK = WINDOW
N_BUCKETS = 32
MAX_DISTANCE = 128
RMS_EPS = 1e-6
NEG_INF = -1e30
LOG2_E = 1.4426950408889634
PAST_LEN = 16384

V7X_VMEM_BYTES = 64 * 1024 * 1024
VMEM_LIMIT = V7X_VMEM_BYTES - 8 * 1024 * 1024
SUBLANES = 8
LANES = 128
HEADS_PER_TILE = LANES // HEAD_DIM
SLABS = GROUP // HEADS_PER_TILE
MXU_COLS = 256
SUB_ROWS = 256
W_CHUNK = 512
ATT_ROWS = 64
ATT_SKEW = 2
ATT_SLOTS = 2 * ATT_SKEW + 2
HALF_Q = BLOCK // 2
ATT_KEYS = WINDOW + HALF_Q
SEQS_PER_STEP = 8
SCAN_PIECES_PER_PROJ_PIECE = 2
ATT_PROJ_TILE = 2 * BLOCK
PROJ_TAIL_PIECES = 4
ATT_ROUNDS_PER_PROJ_PIECE = 4
SQRT_FLOOR = 1e-30


def _head_order(half):
    heads = list(range(GROUP))
    return heads if half == 0 else [h ^ 1 for h in heads]


def _params(*semantics):
    return pltpu.CompilerParams(dimension_semantics=semantics, vmem_limit_bytes=VMEM_LIMIT)


def _resident(shape):
    zeros = (0,) * len(shape)
    return pl.BlockSpec(shape, lambda *_: zeros, pipeline_mode=pl.Buffered(1))


def _rms_scale(x):
    return lax.rsqrt(jnp.mean(x * x, axis=-1, keepdims=True) + RMS_EPS)


def _silu(x):
    h = 0.5 * x
    return h * jnp.tanh(h) + h


def _segment_major(rows, inverse=False):
    seg = rows // SUBLANES
    r = lax.broadcasted_iota(jnp.int32, (rows, rows), 0)
    c = lax.broadcasted_iota(jnp.int32, (rows, rows), 1)
    if inverse:
        src = (r % seg) * SUBLANES + r // seg
    else:
        src = (r % SUBLANES) * seg + r // SUBLANES
    return jnp.where(c == src, 1.0, 0.0).astype(BF16)


def _phase_specs(nchunk, tm, k):
    chunk_w = pl.BlockSpec((k, W_CHUNK), lambda i: (0, jnp.minimum(i, nchunk - 1)))
    tile = lambda n: pl.BlockSpec((tm, n), lambda i: (jnp.maximum(i - nchunk, 0), 0))
    return chunk_w, tile


def _dup_heads(x):
    low = lax.broadcasted_iota(jnp.int32, (x.shape[0], LANES), 1) < HEAD_DIM
    out = []
    for c in range(x.shape[1] // LANES):
        col = x[:, c * LANES:(c + 1) * LANES]
        swapped = pltpu.roll(col, HEAD_DIM, axis=1)
        out += [jnp.where(low, col, swapped), jnp.where(low, swapped, col)]
    return jnp.concatenate(out, axis=1)


def _norm_proj_kvq_kernel(nkv, nqg, xs_ref, x_ref, gkv_ref, gq_ref, wkv_ref, wqg_ref,
                          ks_ref, vs_ref, qs_ref, gates_ref,
                          q_ref, gate_ref, kdup_ref, vt_ref, ktail_ref, vtail_ref,
                          wkv_scr, wqg_scr):
    i = pl.program_id(0)
    nchunk = nkv + nqg
    q_chunks = ATT_WIDTH // W_CHUNK
    q_scale = 1.0 / math.sqrt(HEAD_DIM)
    q_scale_log2 = q_scale * LOG2_E

    def sample_rows(g_ref):
        xs = xs_ref[...]
        return (xs * _rms_scale(xs) * g_ref[...]).astype(BF16)

    @pl.when(i < nkv)
    def _():
        wb = wkv_ref[...].astype(BF16)
        wkv_scr[i] = wb
        r = jnp.dot(sample_rows(gkv_ref), wb, preferred_element_type=F32)

        @pl.when(i == 0)
        def _():
            ks_ref[...] = r

        @pl.when(i == 1)
        def _():
            vs_ref[...] = r

    @pl.when((i >= nkv) & (i < nchunk))
    def _():
        c = i - nkv
        wb = wqg_ref[...].astype(BF16)
        wqg_scr[c] = wb
        r = jnp.dot(sample_rows(gq_ref), wb, preferred_element_type=F32)

        @pl.when(c < q_chunks)
        def _():
            qs_ref[...] = r * q_scale

        @pl.when(c >= q_chunks)
        def _():
            gates_ref[...] = r

    @pl.when(i >= nchunk)
    def _():
        tm = x_ref.shape[0]
        for rs in _row_blocks(tm):
            x = x_ref[rs, :]
            xh = x * _rms_scale(x)
            xkv = (xh * gkv_ref[...]).astype(BF16)
            xq = (xh * gq_ref[...]).astype(BF16)
            k = jnp.dot(xkv, wkv_scr[0], preferred_element_type=F32)
            v = jnp.dot(xkv, wkv_scr[1], preferred_element_type=F32)
            kdup_ref[rs, :] = _dup_heads(k).astype(BF16)
            vt_ref[:, rs] = _dup_heads(v).T.astype(BF16)
            for c in range(nqg):
                r = jnp.dot(xq, wqg_scr[c], preferred_element_type=F32)
                if c < q_chunks:
                    q_ref[rs, c * W_CHUNK:(c + 1) * W_CHUNK] = (r * q_scale_log2).astype(q_ref.dtype)
                else:
                    cc = c - q_chunks
                    gate_ref[rs, cc * W_CHUNK:(cc + 1) * W_CHUNK] = r
        ktail_ref[0] = k[k.shape[0] - WINDOW:].T
        vtail_ref[0] = v[v.shape[0] - WINDOW:].T


def norm_proj_kvq(xs, x, g_kv, g_q, w_kv, w_qg, tm, seq_len):
    m, d = x.shape
    ns = xs.shape[0]
    assert w_kv.shape[1] == 2 * KV_WIDTH == 2 * W_CHUNK and seq_len % tm == 0 and tm >= WINDOW
    nkv, nqg = w_kv.shape[1] // W_CHUNK, w_qg.shape[1] // W_CHUNK
    nchunk = nkv + nqg
    tiles = seq_len // tm
    tile = lambda n: pl.BlockSpec((tm, n), lambda i: (jnp.maximum(i - nchunk, 0), 0))
    tail = pl.BlockSpec((1, KV_WIDTH, WINDOW), lambda i: (jnp.maximum(i - nchunk, 0) // tiles, 0, 0))
    kv_chunk = lambda i: (0, jnp.minimum(i, nkv - 1))
    qg_chunk = lambda i: (0, jnp.clip(i - nkv, 0, nqg - 1))
    q_chunks = ATT_WIDTH // W_CHUNK
    whole_s = lambda n: pl.BlockSpec((ns, n), lambda i: (0, 0))
    return pl.pallas_call(
        functools.partial(_norm_proj_kvq_kernel, nkv, nqg),
        grid=(nchunk + m // tm,),
        in_specs=[
            _resident(xs.shape), tile(d), _resident((1, d)), _resident((1, d)),
            pl.BlockSpec((d, W_CHUNK), kv_chunk), pl.BlockSpec((d, W_CHUNK), qg_chunk),
        ],
        out_specs=[
            whole_s(KV_WIDTH), whole_s(KV_WIDTH),
            pl.BlockSpec((ns, W_CHUNK), lambda i: (0, jnp.clip(i - nkv, 0, q_chunks - 1))),
            pl.BlockSpec((ns, W_CHUNK), lambda i: (0, jnp.clip(i - nkv - q_chunks, 0, nqg - q_chunks - 1))),
            tile(ATT_WIDTH), tile(ATT_WIDTH), tile(2 * KV_WIDTH),
            pl.BlockSpec((2 * KV_WIDTH, tm), lambda i: (0, jnp.maximum(i - nchunk, 0))), tail, tail,
        ],
        out_shape=[
            jax.ShapeDtypeStruct((ns, KV_WIDTH), F32),
            jax.ShapeDtypeStruct((ns, KV_WIDTH), F32),
            jax.ShapeDtypeStruct((ns, ATT_WIDTH), F32),
            jax.ShapeDtypeStruct((ns, w_qg.shape[1] - ATT_WIDTH), F32),
            jax.ShapeDtypeStruct((m, ATT_WIDTH), BF16),
            jax.ShapeDtypeStruct((m, ATT_WIDTH), F32),
            jax.ShapeDtypeStruct((m, 2 * KV_WIDTH), BF16),
            jax.ShapeDtypeStruct((2 * KV_WIDTH, m), BF16),
            jax.ShapeDtypeStruct((m // seq_len, KV_WIDTH, WINDOW), F32),
            jax.ShapeDtypeStruct((m // seq_len, KV_WIDTH, WINDOW), F32),
        ],
        scratch_shapes=[pltpu.VMEM((nkv, d, W_CHUNK), BF16), pltpu.VMEM((nqg, d, W_CHUNK), BF16)],
        compiler_params=_params("arbitrary"),
        name="norm_proj_kvq",
    )(xs, x, g_kv.reshape(1, d), g_q.reshape(1, d), w_kv, w_qg)


def _row_blocks(rows):
    sub = min(rows, SUB_ROWS)
    return [slice(r, r + sub) for r in range(0, rows, sub)]


def _proj_norm_res_kernel(nchunk, nparts, as_ref, gs_ref, xs_ref, *refs):
    y_refs = refs[:nparts]
    w_ref, g_ref, x_ref, os_ref, o_ref, w_scr, raw_scr = refs[nparts:]
    i = pl.program_id(0)

    @pl.when(i < nchunk)
    def _():
        wb = w_ref[...].astype(BF16)
        w_scr[i] = wb
        ys = (as_ref[...] * _silu(gs_ref[...])).astype(BF16)
        raw_scr[i] = jnp.dot(ys, wb, preferred_element_type=F32)

    @pl.when(i == nchunk - 1)
    def _():
        o = jnp.concatenate([raw_scr[c] for c in range(nchunk)], axis=1)
        os_ref[...] = xs_ref[...] + o * _rms_scale(o) * g_ref[...]

    @pl.when(i >= nchunk)
    def _():
        tm = x_ref.shape[0]
        part_rows = tm // nparts

        for rs in _row_blocks(tm):
            part, off = divmod(rs.start, part_rows)
            y = y_refs[part][off:off + rs.stop - rs.start, :]
            o = jnp.concatenate([jnp.dot(y, w_scr[c], preferred_element_type=F32)
                                 for c in range(nchunk)], axis=1)
            o_ref[rs, :] = x_ref[rs, :] + o * _rms_scale(o) * g_ref[...]


def proj_norm_res(a_s, gate_s, x_s, y_parts, w, g, x, tm):
    k, d = w.shape
    m = x.shape[0]
    nchunk = d // W_CHUNK
    nparts = len(y_parts)
    assert (tm // nparts) % min(tm, SUB_ROWS) == 0
    chunk_w, tile = _phase_specs(nchunk, tm, k)
    part = pl.BlockSpec((tm // nparts, k), lambda i: (jnp.maximum(i - nchunk, 0), 0))
    return pl.pallas_call(
        functools.partial(_proj_norm_res_kernel, nchunk, nparts),
        grid=(nchunk + m // tm,),
        in_specs=[_resident(a_s.shape), _resident(gate_s.shape), _resident(x_s.shape)]
        + [part] * nparts + [chunk_w, _resident((1, d)), tile(d)],
        out_specs=[pl.BlockSpec(x_s.shape, lambda i: (0, 0)), tile(d)],
        out_shape=[jax.ShapeDtypeStruct(x_s.shape, F32), jax.ShapeDtypeStruct((m, d), F32)],
        scratch_shapes=[pltpu.VMEM((nchunk, k, W_CHUNK), BF16),
                        pltpu.VMEM((nchunk, x_s.shape[0], W_CHUNK), F32)],
        compiler_params=_params("arbitrary"),
        name="proj_norm_res",
    )(a_s, gate_s, x_s, *y_parts, w, g.reshape(1, d), x)


def _lru_gate_dots(conv, wr_half, wi_half):
    cb = conv.astype(BF16)
    return (jnp.dot(cb, wr_half, preferred_element_type=F32),
            jnp.dot(cb, wi_half, preferred_element_type=F32))


def _lru_gates(conv, wr_half, br, wi_half, bi, lam):
    return _lru_gate_math(conv, _lru_gate_dots(conv, wr_half, wi_half), br, bi, lam)


def _lru_gate_math(conv, half_pre, br, bi, lam):
    th_r = jnp.tanh(half_pre[0] + 0.5 * br)
    th_i = jnp.tanh(half_pre[1] + 0.5 * bi)
    nl = -lam
    softplus = jnp.maximum(nl, 0.0) + jnp.log1p(jnp.exp(-jnp.abs(nl)))
    half = (0.5 * LRU_C) * softplus
    x = th_r * half + half
    a = jnp.exp2(x * -LOG2_E)
    z = jnp.tanh(x) * (a * a + 1.0)
    mult = z * lax.rsqrt(jnp.maximum(z, SQRT_FLOOR))
    hc = 0.5 * conv
    return a, mult * (hc * th_i + hc)


def _interleave(*stages):
    live = [[stage, share] for stage, share in stages]
    while live:
        for entry in list(live):
            try:
                for _ in range(entry[1]):
                    next(entry[0])
            except StopIteration:
                live.remove(entry)


def _in_proj_tile(x_ref, rs, g_ref, w_scr, ug_ref):
    x = x_ref[rs, :]
    xn = (x * _rms_scale(x) * g_ref[...]).astype(BF16)
    xn = jnp.dot(_segment_major(xn.shape[0]), xn, preferred_element_type=F32).astype(BF16)
    for c in range(w_scr.shape[0]):
        for n0 in range(0, W_CHUNK, MXU_COLS):
            ug_ref[:, c * W_CHUNK + n0:c * W_CHUNK + n0 + MXU_COLS] = jnp.dot(
                xn, w_scr[c, :, n0:n0 + MXU_COLS], preferred_element_type=F32)
            yield


def _rglru_tile(ug_ref, y_ref, cw_ref, cb_ref, wr_ref, br_ref, wi_ref, bi_ref, lam_ref, h_scr, tail_scr):
    tc = ug_ref.shape[0]
    seg = tc // SUBLANES
    ntaps = CONV_W - 1
    bw = LRU_BLOCK_W
    sub = lax.broadcasted_iota(jnp.int32, (SUBLANES, bw), 0)
    first = sub == 0
    time_order = _segment_major(tc, inverse=True)

    def shift_in(x, row0):
        return jnp.where(first, row0, pltpu.roll(x, 1, axis=0))

    def group(x, j):
        return x[j * SUBLANES:(j + 1) * SUBLANES]

    def store_time_order(cols, y):
        y_ref[:, cols] = jnp.dot(time_order, y, preferred_element_type=F32).astype(y_ref.dtype)

    pending = None
    for n in range(LRU_BLOCKS):
        cs = slice(n * bw, (n + 1) * bw)
        u = ug_ref[:, cs]
        tail = tail_scr[:, cs]
        before = [shift_in(group(u, seg - m), tail[ntaps - m:ntaps - m + 1])
                  for m in range(ntaps, 0, -1)]
        ext = jnp.concatenate(before + [u], axis=0)
        tail_scr[:, cs] = jnp.concatenate(
            [group(u, seg - m)[SUBLANES - 1:] for m in range(ntaps, 0, -1)], axis=0)
        cw = cw_ref[:, cs]
        conv = cb_ref[:, cs]
        for tap in range(CONV_W):
            conv = conv + ext[tap * SUBLANES:tap * SUBLANES + tc] * cw[tap:tap + 1]
        yield

        half_pre = _lru_gate_dots(conv, wr_ref[n], wi_ref[n])
        yield

        if pending is not None:
            store_time_order(*pending)
        yield

        a, b = _lru_gate_math(conv, half_pre, br_ref[:, cs], bi_ref[:, cs], lam_ref[:, cs])

        h = b[:SUBLANES]
        acc = a[:SUBLANES]
        h_loc, a_cum = [h], [acc]
        for j in range(1, seg):
            sl = slice(j * SUBLANES, (j + 1) * SUBLANES)
            h = a[sl] * h + b[sl]
            acc = a[sl] * acc
            h_loc.append(h)
            a_cum.append(acc)

        step = 1
        while step < SUBLANES:
            keep = sub >= step
            h = jnp.where(keep, acc * pltpu.roll(h, step, axis=0) + h, h)
            acc = jnp.where(keep, acc * pltpu.roll(acc, step, axis=0), acc)
            step *= 2
        h_prev = h_scr[:, cs]
        after = h + acc * h_prev
        h_in = shift_in(after, h_prev)
        h_scr[:, cs] = after[SUBLANES - 1:]

        hs = jnp.concatenate([h_loc[j] + a_cum[j] * h_in for j in range(seg)], axis=0)
        y = (hs * _silu(ug_ref[:, LRU_WIDTH + n * bw:LRU_WIDTH + (n + 1) * bw])).astype(BF16)
        pending = (cs, y)
        yield

    store_time_order(*pending)
    yield


def _rglru_front_kernel(nchunk, npairs, chunks, xs_ref, x_ref, g_ref, w_ref, cprev_ref, h0_ref,
                        scprev_ref, sh0_ref, cw_ref, cb_ref, wr_ref, br_ref, wi_ref, bi_ref, lam_ref,
                        gs_ref, hs_ref, scnew_ref, y_even_ref, y_odd_ref, cnew_ref, hlast_ref,
                        w_scr, wr_scr, wi_scr, us_scr, ug0_scr, ug1_scr, h_scr, tail_scr):
    i = pl.program_id(0)
    p = i - nchunk
    tc = SUB_ROWS
    half = nchunk // 2
    lru = (cw_ref, cb_ref, wr_scr, br_ref, wi_scr, bi_ref, lam_ref, h_scr, tail_scr)

    @pl.when(i < nchunk)
    def _():
        wb = w_ref[...].astype(BF16)
        w_scr[i] = wb
        xs = xs_ref[...]
        xsn = (xs * _rms_scale(xs) * g_ref[...]).astype(BF16)
        r = jnp.dot(xsn, wb, preferred_element_type=F32)
        gs_ref[...] = r

        @pl.when(i < half)
        def _():
            us_scr[i] = r

    @pl.when(i == nchunk - 1)
    def _():
        wr_scr[...] = (0.5 * wr_ref[...]).astype(BF16)
        wi_scr[...] = (0.5 * wi_ref[...]).astype(BF16)
        bw = LRU_BLOCK_W
        for n in range(LRU_BLOCKS):
            cs = slice(n * bw, (n + 1) * bw)
            c, off = divmod(n * bw, W_CHUNK)
            u = us_scr[c, :, off:off + bw]
            cw = cw_ref[:, cs]
            conv = cb_ref[:, cs]
            for tap in range(CONV_W - 1):
                conv = conv + scprev_ref[tap, :, cs] * cw[tap:tap + 1]
                if tap > 0:
                    scnew_ref[tap - 1, :, cs] = scprev_ref[tap, :, cs]
            conv = conv + u * cw[CONV_W - 1:]
            scnew_ref[CONV_W - 2, :, cs] = u
            a, b = _lru_gates(conv, wr_scr[n], br_ref[:, cs], wi_scr[n], bi_ref[:, cs], lam_ref[:, cs])
            hs_ref[:, cs] = a * sh0_ref[:, cs] + b

    def project_even():
        return _in_proj_tile(x_ref, slice(0, tc), g_ref, w_scr, ug0_scr)

    def scan_odd():
        return _rglru_tile(ug1_scr, y_odd_ref, *lru)

    @pl.when(p == 0)
    def _():
        _interleave((project_even(), 1))

    @pl.when((p > 0) & (p < npairs))
    def _():
        _interleave((project_even(), 1), (scan_odd(), SCAN_PIECES_PER_PROJ_PIECE))

    @pl.when(p == npairs)
    def _():
        _interleave((scan_odd(), 1))

    @pl.when(p > 0)
    def _():
        hlast_ref[0] = h_scr[...]
        cnew_ref[0] = tail_scr[...]

    @pl.when((p >= 0) & (p < npairs))
    def _():
        @pl.when((2 * p) % chunks == 0)
        def _():
            h_scr[...] = h0_ref[0]
            tail_scr[...] = cprev_ref[0]

        _interleave((_in_proj_tile(x_ref, slice(tc, 2 * tc), g_ref, w_scr, ug1_scr), 1),
                    (_rglru_tile(ug0_scr, y_even_ref, *lru), SCAN_PIECES_PER_PROJ_PIECE))


def rglru_front(xs, s_conv_prev, s_h0, x, conv_prev, h0, g, w_in, conv_w, conv_b, w_r, b_r, w_i, b_i, lam,
                seq_len):
    m, d = x.shape
    ns = xs.shape[0]
    w = w_in.shape[1] // 2
    tc = SUB_ROWS
    nchunk = w_in.shape[1] // W_CHUNK
    half = nchunk // 2
    bsz = m // seq_len
    chunks = seq_len // tc
    npairs = m // (2 * tc)
    assert seq_len % (2 * tc) == 0 and tc % (SUBLANES * SUBLANES) == 0 and tc // SUBLANES > CONV_W
    pair = lambda i: jnp.clip(i - nchunk, 0, npairs - 1)
    last = npairs * 2 - 1
    seq_in = lambda i: (jnp.clip(2 * (i - nchunk), 0, last) // chunks, 0, 0)
    seq_out = lambda i: (jnp.clip(2 * (i - nchunk) - 1, 0, last) // chunks, 0, 0)
    state_in = lambda rows: pl.BlockSpec((1, rows, w), seq_in)
    state_out = lambda rows: pl.BlockSpec((1, rows, w), seq_out)
    chunk = lambda i: (0, jnp.minimum(i, nchunk - 1))
    return pl.pallas_call(
        functools.partial(_rglru_front_kernel, nchunk, npairs, chunks),
        grid=(nchunk + npairs + 1,),
        in_specs=[_resident(xs.shape), pl.BlockSpec((2 * tc, d), lambda i: (pair(i), 0)), _resident((1, d)),
                  pl.BlockSpec((d, W_CHUNK), chunk), state_in(CONV_W - 1), state_in(1),
                  _resident(s_conv_prev.shape), _resident(s_h0.shape),
                  _resident((CONV_W, w)), _resident((1, w)), _resident(w_r.shape), _resident((1, w)),
                  _resident(w_i.shape), _resident((1, w)), _resident((1, w))],
        out_specs=[
            pl.BlockSpec((ns, W_CHUNK), lambda i: (0, jnp.clip(i - half, 0, half - 1))),
            pl.BlockSpec((ns, w), lambda i: (0, 0)),
            pl.BlockSpec(s_conv_prev.shape, lambda i: (0, 0, 0)),
            pl.BlockSpec((tc, w), lambda i: (pair(i), 0)),
            pl.BlockSpec((tc, w), lambda i: (jnp.clip(i - nchunk - 1, 0, npairs - 1), 0)),
            state_out(CONV_W - 1), state_out(1),
        ],
        out_shape=[
            jax.ShapeDtypeStruct((ns, w), F32),
            jax.ShapeDtypeStruct((ns, w), F32),
            jax.ShapeDtypeStruct(s_conv_prev.shape, F32),
            jax.ShapeDtypeStruct((m // 2, w), BF16),
            jax.ShapeDtypeStruct((m // 2, w), BF16),
            jax.ShapeDtypeStruct((bsz, CONV_W - 1, w), F32),
            jax.ShapeDtypeStruct((bsz, 1, w), F32),
        ],
        scratch_shapes=[pltpu.VMEM((nchunk, d, W_CHUNK), BF16),
                        pltpu.VMEM(w_r.shape, BF16), pltpu.VMEM(w_i.shape, BF16),
                        pltpu.VMEM((half, ns, W_CHUNK), F32),
                        pltpu.VMEM((tc, 2 * w), F32), pltpu.VMEM((tc, 2 * w), F32),
                        pltpu.VMEM((1, w), F32), pltpu.VMEM((CONV_W - 1, w), F32)],
        compiler_params=_params("arbitrary"),
        name="rglru_front",
    )(xs, x, g.reshape(1, d), w_in, conv_prev, h0.reshape(bsz, 1, w), s_conv_prev, s_h0,
      conv_w, conv_b.reshape(1, w), w_r, b_r.reshape(1, w), w_i, b_i.reshape(1, w), lam.reshape(1, w))


def _buckets(dist):
    n = jnp.maximum(dist, 0)
    max_exact = N_BUCKETS // 2
    nf = jnp.maximum(n, 1).astype(F32)
    large = max_exact + jnp.floor(jnp.log(nf / max_exact) / math.log(MAX_DISTANCE / max_exact)
                                  * (N_BUCKETS - max_exact)).astype(jnp.int32)
    large = jnp.minimum(large, N_BUCKETS - 1)
    return jnp.where(n < max_exact, n, large)


def _lookup(bucket, valid, table_ref, head):
    bias = jnp.zeros(bucket.shape, F32)
    for b in range(N_BUCKETS):
        bias = jnp.where(bucket == b, table_ref[b, head], bias)
    return jnp.where(valid, bias, NEG_INF)


def _bias_kernel(table_ref, sinks_ref, band_ref, sinkt_ref, past_ref, new_ref):
    hk = pl.program_id(0)
    span = 3 * BLOCK
    dist = (lax.broadcasted_iota(jnp.int32, (1, span), 1) + BLOCK) % span
    bucket = _buckets(dist)
    in_window = (dist >= 0) & (dist < WINDOW)
    key_row = lax.broadcasted_iota(jnp.int32, (ATT_KEYS, HALF_Q), 0)

    def band(head):
        row = _lookup(bucket, in_window, table_ref, head) * LOG2_E
        full = pltpu.roll(jnp.broadcast_to(row, (ATT_KEYS, span)), 0, axis=1, stride=1, stride_axis=0)
        return full[:, :HALF_Q]

    rows = past_ref.shape[2]
    d_past = rows - lax.broadcasted_iota(jnp.int32, (1, rows), 1)
    b_past = _buckets(d_past)
    ok_past = (d_past >= 0) & (d_past < WINDOW)
    d_new = jnp.zeros((1, LANES), jnp.int32)
    b_new = _buckets(d_new)
    for g in range(GROUP):
        head = hk * GROUP + g
        bias = band(head)
        for half in range(2):
            slot = _head_order(half).index(g)
            cs = slice(slot * HALF_Q, (slot + 1) * HALF_Q)
            band_ref[0, half, 0, :, cs] = bias
            prev_rows = BLOCK - half * HALF_Q
            band_ref[1, half, 0, :, cs] = jnp.where(key_row < prev_rows, NEG_INF, bias)
            sinkt_ref[half, 0, :, cs] = jnp.full((1, HALF_Q), sinks_ref[head] * LOG2_E, F32)
        past_ref[0, g:g + 1, :] = _lookup(b_past, ok_past, table_ref, head)
        new_ref[0, g:g + 1, :] = _lookup(b_new, d_new == 0, table_ref, head)


def bias_tables(table, sinks, past_rows):
    smem = pl.BlockSpec(memory_space=pltpu.SMEM)
    return pl.pallas_call(
        _bias_kernel,
        grid=(N_KV_HEADS,),
        in_specs=[smem, smem],
        out_specs=[
            pl.BlockSpec((2, 2, 1, ATT_KEYS, GROUP * HALF_Q), lambda h: (0, 0, h, 0, 0)),
            pl.BlockSpec((2, 1, 1, GROUP * HALF_Q), lambda h: (0, h, 0, 0)),
            pl.BlockSpec((1, GROUP, past_rows), lambda h: (h, 0, 0)),
            pl.BlockSpec((1, GROUP, LANES), lambda h: (h, 0, 0)),
        ],
        out_shape=[
            jax.ShapeDtypeStruct((2, 2, N_KV_HEADS, ATT_KEYS, GROUP * HALF_Q), F32),
            jax.ShapeDtypeStruct((2, N_KV_HEADS, 1, GROUP * HALF_Q), F32),
            jax.ShapeDtypeStruct((N_KV_HEADS, GROUP, past_rows), F32),
            jax.ShapeDtypeStruct((N_KV_HEADS, GROUP, LANES), F32),
        ],
        compiler_params=_params("parallel"),
        name="bias_tables",
    )(table, sinks)


def _band_attn_rounds(first_tile, q_ref, kp_ref, kc_ref, vp_ref, vc_ref, gate_ref, bias_ref, sink_ref,
                      y_ref, s_scr, p_scr):
    nt = (((1,), (1,)), ((), ()))
    low = (lax.broadcasted_iota(jnp.int32, (1, LANES), 1) < HEAD_DIM)
    keep_low = low.astype(BF16)
    keep_high = 1 - keep_low
    keep = (keep_low, keep_high)
    nkeys = 2 * BLOCK
    ones_rows = jnp.where(lax.broadcasted_iota(jnp.int32, (2 * SUBLANES, nkeys), 0) == 0,
                          1.0, 0.0).astype(BF16)
    rows = ATT_ROWS
    nslot = s_scr.shape[0]
    items = [(blk, hk, half) for blk in range(q_ref.shape[0] // BLOCK)
             for hk in range(N_KV_HEADS) for half in range(2)]
    assert nslot % 2 == 0

    def rows_of(blk):
        return slice(blk * BLOCK, (blk + 1) * BLOCK)

    def key_rows(half):
        return slice(half * HALF_Q, half * HALF_Q + ATT_KEYS)

    for slot in range(nslot):
        dead = slice(ATT_KEYS, nkeys) if slot % 2 == 0 else slice(0, HALF_Q)
        p_scr[slot, dead, :] = jnp.zeros((HALF_Q, p_scr.shape[2]), BF16)

    def scores(idx):
        blk, hk, half = items[idx]
        variant = first_tile if blk == 0 else 0
        cs = slice(hk * LANES, (hk + 1) * LANES)
        k_prev = kp_ref[:, cs] if blk == 0 else kc_ref[rows_of(blk - 1), cs]
        k_cur = kc_ref[rows_of(blk), cs]
        kd = (jnp.concatenate([k_prev, k_cur[:HALF_Q]], axis=0) if half == 0
              else jnp.concatenate([k_prev[HALF_Q:], k_cur], axis=0))
        q0 = blk * BLOCK + half * HALF_Q
        qs = jnp.concatenate(
            [q_ref[q0:q0 + HALF_Q, (hk * SLABS + h // HEADS_PER_TILE) * LANES:
                   (hk * SLABS + h // HEADS_PER_TILE + 1) * LANES] * keep[h % HEADS_PER_TILE]
             for h in _head_order(half)], axis=0)
        s = lax.dot_general(kd, qs, nt, preferred_element_type=F32) + bias_ref[variant, half, hk]
        s_scr[idx % nslot, key_rows(half), :] = s
        return jnp.maximum(jnp.max(s, axis=0, keepdims=True), sink_ref[half, hk])

    def softmax(idx, m):
        blk, hk, half = items[idx]
        slot = idx % nslot
        lo = half * HALF_Q
        for r in range(lo, lo + ATT_KEYS, rows):
            p_scr[slot, r:r + rows, :] = jnp.exp2(s_scr[slot, r:r + rows, :] - m).astype(BF16)
        return jnp.exp2(sink_ref[half, hk] - m)

    def weighted_values(idx):
        blk, hk, half = items[idx]
        vs = slice(hk * LANES, hk * LANES + HEAD_DIM)
        v_prev = vp_ref[vs, :] if blk == 0 else vc_ref[vs, rows_of(blk - 1)]
        vt = jnp.concatenate([v_prev, vc_ref[vs, rows_of(blk)]], axis=1)
        lhs_v = jnp.concatenate([vt, ones_rows], axis=0)
        return jnp.dot(lhs_v, p_scr[idx % nslot], preferred_element_type=F32)

    low_q = lax.broadcasted_iota(jnp.int32, (HEAD_DIM, LANES), 1) < HALF_Q

    def finish(blk, hk, ots, sink_ws):
        o = [ots[half][:HEAD_DIM] * (1.0 / (ots[half][HEAD_DIM:HEAD_DIM + 1] + sink_ws[half]))
             for half in range(2)]
        for sl in range(SLABS):
            a, b = (oh[:, sl * LANES:(sl + 1) * LANES] for oh in o)
            even = jnp.where(low_q, a, b)
            odd = pltpu.roll(jnp.where(low_q, b, a), HALF_Q, axis=1)
            pair = jnp.concatenate([even, odd], axis=0)
            c0 = (hk * SLABS + sl) * LANES
            y_ref[rows_of(blk), c0:c0 + LANES] = (
                pair.T * _silu(gate_ref[rows_of(blk), c0:c0 + LANES])).astype(y_ref.dtype)

    n = len(items)
    offs, sink_ws, outs = {}, {}, {}
    for k in range(-2 * ATT_SKEW, n + ATT_SKEW):
        if 0 <= k + 2 * ATT_SKEW < n:
            offs[k + 2 * ATT_SKEW] = scores(k + 2 * ATT_SKEW)
        if 0 <= k + ATT_SKEW < n:
            sink_ws[k + ATT_SKEW] = softmax(k + ATT_SKEW, offs.pop(k + ATT_SKEW))
        if 0 <= k < n:
            outs[k] = weighted_values(k)
        j = k - ATT_SKEW
        if 0 <= j < n and items[j][2] == 1:
            finish(items[j][0], items[j][1], [outs.pop(j - 1), outs.pop(j)],
                   [sink_ws.pop(j - 1), sink_ws.pop(j)])
        yield


def _proj_tile(y_scr, w_scr, g_ref, x_ref, o_ref, raw_scr):
    y = y_scr[...]
    for c in range(w_scr.shape[0]):
        for n0 in range(0, W_CHUNK, MXU_COLS):
            raw_scr[:, c * W_CHUNK + n0:c * W_CHUNK + n0 + MXU_COLS] = jnp.dot(
                y, w_scr[c, :, n0:n0 + MXU_COLS], preferred_element_type=F32)
            yield
    step = y_scr.shape[0] // PROJ_TAIL_PIECES
    for r0 in range(0, y_scr.shape[0], step):
        o = raw_scr[r0:r0 + step, :]
        o_ref[r0:r0 + step, :] = x_ref[r0:r0 + step, :] + o * _rms_scale(o) * g_ref[...]
        yield


def _attn_proj_kernel(nchunk, ntiles, tiles_per_seq, as_ref, gs_ref, xs_ref,
                      q_ref, kp_ref, kc_ref, vp_ref, vc_ref, gate_ref, bias_ref, sink_ref,
                      w_ref, g_ref, x_ref, os_ref, o_ref,
                      w_scr, raws_scr, s_scr, p_scr, ynew_scr, yold_scr, raw_scr):
    i = pl.program_id(0)
    p = i - nchunk

    @pl.when(i < nchunk)
    def _():
        wb = w_ref[...].astype(BF16)
        w_scr[i] = wb
        ys = (as_ref[...] * _silu(gs_ref[...])).astype(BF16)
        raws_scr[i] = jnp.dot(ys, wb, preferred_element_type=F32)

    @pl.when(i == nchunk - 1)
    def _():
        o = jnp.concatenate([raws_scr[c] for c in range(nchunk)], axis=1)
        os_ref[...] = xs_ref[...] + o * _rms_scale(o) * g_ref[...]

    def attend():
        first = (p % tiles_per_seq == 0).astype(jnp.int32)
        return _band_attn_rounds(first, q_ref, kp_ref, kc_ref, vp_ref, vc_ref, gate_ref, bias_ref,
                                 sink_ref, ynew_scr, s_scr, p_scr)

    def project():
        return _proj_tile(yold_scr, w_scr, g_ref, x_ref, o_ref, raw_scr)

    @pl.when(p == 0)
    def _():
        _interleave((attend(), 1))
        yold_scr[...] = ynew_scr[...]

    @pl.when((p > 0) & (p < ntiles))
    def _():
        _interleave((attend(), ATT_ROUNDS_PER_PROJ_PIECE), (project(), 1))
        yold_scr[...] = ynew_scr[...]

    @pl.when(p == ntiles)
    def _():
        _interleave((project(), 1))


def attn_proj(a_s, gate_s, x_s, q, kdup, vt, gate, bias_band, sink_t, w, g, x, seq_len):
    k, d = w.shape
    m = x.shape[0]
    tm = ATT_PROJ_TILE
    nchunk = d // W_CHUNK
    ntiles = m // tm
    tiles_per_seq = seq_len // tm
    per_tile = tm // BLOCK
    assert seq_len % tm == 0 and tm % BLOCK == 0
    att_tile = lambda i: jnp.clip(i - nchunk, 0, ntiles - 1)
    proj_tile = lambda i: jnp.clip(i - nchunk - 1, 0, ntiles - 1)

    def before(i):
        t = att_tile(i)
        return per_tile * t - jnp.where(t % tiles_per_seq == 0, 0, 1)

    rows = lambda n: pl.BlockSpec((tm, n), lambda i: (att_tile(i), 0))
    score_tile = (2 * BLOCK, GROUP * HALF_Q)
    return pl.pallas_call(
        functools.partial(_attn_proj_kernel, nchunk, ntiles, tiles_per_seq),
        grid=(nchunk + ntiles + 1,),
        in_specs=[
            _resident(a_s.shape), _resident(gate_s.shape), _resident(x_s.shape),
            rows(ATT_WIDTH), pl.BlockSpec((BLOCK, 2 * KV_WIDTH), lambda i: (before(i), 0)),
            rows(2 * KV_WIDTH), pl.BlockSpec((2 * KV_WIDTH, BLOCK), lambda i: (0, before(i))),
            pl.BlockSpec((2 * KV_WIDTH, tm), lambda i: (0, att_tile(i))), rows(ATT_WIDTH),
            _resident(bias_band.shape), _resident(sink_t.shape),
            pl.BlockSpec((k, W_CHUNK), lambda i: (0, jnp.minimum(i, nchunk - 1))), _resident((1, d)),
            pl.BlockSpec((tm, d), lambda i: (proj_tile(i), 0)),
        ],
        out_specs=[pl.BlockSpec(x_s.shape, lambda i: (0, 0)),
                   pl.BlockSpec((tm, d), lambda i: (proj_tile(i), 0))],
        out_shape=[jax.ShapeDtypeStruct(x_s.shape, F32), jax.ShapeDtypeStruct((m, d), F32)],
        scratch_shapes=[pltpu.VMEM((nchunk, k, W_CHUNK), BF16),
                        pltpu.VMEM((nchunk, x_s.shape[0], W_CHUNK), F32),
                        pltpu.VMEM((ATT_SLOTS,) + score_tile, F32),
                        pltpu.VMEM((ATT_SLOTS,) + score_tile, BF16),
                        pltpu.VMEM((tm, k), BF16), pltpu.VMEM((tm, k), BF16),
                        pltpu.VMEM((tm, d), F32)],
        compiler_params=_params("arbitrary"),
        name="attn_proj",
    )(a_s, gate_s, x_s, q, kdup, kdup, vt, vt, gate, bias_band, sink_t, w, g.reshape(1, d), x)


def _cached_attn_kernel(q_ref, ckt_ref, cvt_ref, kn_ref, vn_ref, sinks_ref, bpast_ref, bnew_ref, o_ref):
    shape = (N_Q_HEADS, KV_WIDTH)
    lane_kv = lax.broadcasted_iota(jnp.int32, shape, 1) // HEAD_DIM
    row_kv = lax.broadcasted_iota(jnp.int32, shape, 0) // GROUP
    own = lane_kv == row_kv
    sink = sinks_ref[...]
    nt = (((1,), (1,)), ((), ()))
    seqs = range(q_ref.shape[0])
    qm = []
    for b in seqs:
        q = q_ref[b]
        qt = jnp.concatenate([q] * N_KV_HEADS, axis=1)
        qm.append(jnp.where(own, qt, 0.0).astype(BF16))
    s = [jnp.dot(qm[b], ckt_ref[b].astype(BF16), preferred_element_type=F32) + bpast_ref[...] for b in seqs]
    s_new = [jnp.sum(qm[b].astype(F32) * kn_ref[b].astype(BF16).astype(F32), axis=-1, keepdims=True)
             + bnew_ref[:, :1] for b in seqs]
    m = [jnp.maximum(jnp.maximum(jnp.max(s[b], axis=-1, keepdims=True), s_new[b]), sink) for b in seqs]
    p = [jnp.exp(s[b] - m[b]) for b in seqs]
    p_new = [jnp.exp(s_new[b] - m[b]) for b in seqs]
    denom = [jnp.sum(p[b], axis=-1, keepdims=True) + p_new[b] + jnp.exp(sink - m[b]) for b in seqs]
    pv = [lax.dot_general(p[b].astype(BF16), cvt_ref[b].astype(BF16), nt, preferred_element_type=F32)
          for b in seqs]
    for b in seqs:
        o_all = pv[b] + p_new[b].astype(BF16).astype(F32) * vn_ref[b].astype(BF16).astype(F32)
        o_all = jnp.where(own, o_all, 0.0)
        o = o_all[:, :HEAD_DIM]
        for hk in range(1, N_KV_HEADS):
            o = o + o_all[:, hk * HEAD_DIM:(hk + 1) * HEAD_DIM]
        o_ref[b] = o / denom[b]


def cached_attention(q, cache_kt, cache_vt, k_new, v_new, sinks, bias_past, bias_new):
    bsz, _, rows = cache_kt.shape
    nseq = math.gcd(bsz, SEQS_PER_STEP)
    per_seq = lambda r, n: pl.BlockSpec((nseq, r, n), lambda b: (b, 0, 0))
    return pl.pallas_call(
        _cached_attn_kernel,
        grid=(bsz // nseq,),
        in_specs=[
            per_seq(N_Q_HEADS, HEAD_DIM), per_seq(KV_WIDTH, rows), per_seq(KV_WIDTH, rows),
            per_seq(1, KV_WIDTH), per_seq(1, KV_WIDTH),
            _resident((N_Q_HEADS, 1)), _resident((N_Q_HEADS, rows)), _resident((N_Q_HEADS, LANES)),
        ],
        out_specs=per_seq(N_Q_HEADS, HEAD_DIM),
        out_shape=jax.ShapeDtypeStruct((bsz, N_Q_HEADS, HEAD_DIM), F32),
        compiler_params=_params("parallel"),
        name="cached_attention",
    )(q, cache_kt, cache_vt, k_new, v_new, sinks, bias_past, bias_new)


def kernel(x_prompt, x_sample, state_conv, state_h, cache_k, cache_v, a_norm_pre, a_norm_post,
           a_w_in, a_conv_w, a_conv_b, a_w_r, a_b_r, a_w_i, a_b_i, a_lambda, a_w_out, kv_norm, w_kv,
           b_norm_pre, b_norm_post, b_w_qg, b_sinks, b_w_out, rel_bias_table):
    bsz, t, d = x_prompt.shape
    dbsz, dt, _ = x_sample.shape
    assert a_w_in.shape[0] == 1 and b_w_qg.shape[0] == 1 and dt == 1
    assert t % BLOCK == 0 and t >= WINDOW
    past_rows = cache_k.shape[1]
    assert past_rows == min(WINDOW, PAST_LEN)

    sinks = b_sinks[0]
    bias_band, sink_t, bias_past, bias_new = bias_tables(rel_bias_table, sinks, past_rows)

    tm = 2 * SUB_ROWS
    xp = x_prompt.reshape(bsz * t, d)
    xs = x_sample.reshape(dbsz, d)

    conv0 = jnp.zeros((bsz, CONV_W - 1, LRU_WIDTH), F32)
    h0 = jnp.zeros((bsz, LRU_WIDTH), F32)
    gate_s, hs, s_conv_t, y_even, y_odd, p_conv, p_h = rglru_front(
        xs, jnp.transpose(state_conv[0], (1, 0, 2)), state_h[0], xp, conv0, h0, a_norm_pre[0], a_w_in[0],
        a_conv_w[0], a_conv_b[0], a_w_r[0], a_b_r[0], a_w_i[0], a_b_i[0], a_lambda[0], seq_len=t)
    xs1, x1 = proj_norm_res(hs, gate_s, xs, (y_even, y_odd), a_w_out[0], a_norm_post[0], xp, tm)

    ks, vs, qs, gate_sb, q, gate_b, kdup, vt, k_tail, v_tail = norm_proj_kvq(
        xs1, x1, kv_norm, b_norm_pre[0], w_kv, b_w_qg[0], SUB_ROWS, seq_len=t)
    cache_kt = jnp.transpose(cache_k, (0, 2, 3, 1)).reshape(dbsz, KV_WIDTH, past_rows)
    cache_vt = jnp.transpose(cache_v, (0, 2, 3, 1)).reshape(dbsz, KV_WIDTH, past_rows)
    os_ = cached_attention(qs.reshape(dbsz, N_Q_HEADS, HEAD_DIM), cache_kt, cache_vt,
                           ks.reshape(dbsz, 1, KV_WIDTH), vs.reshape(dbsz, 1, KV_WIDTH),
                           sinks.reshape(N_Q_HEADS, 1), bias_past.reshape(N_Q_HEADS, past_rows),
                           bias_new.reshape(N_Q_HEADS, LANES))
    y_sample, y_prompt = attn_proj(os_.reshape(dbsz, ATT_WIDTH), gate_sb, xs1, q, kdup, vt, gate_b,
                                   bias_band, sink_t, b_w_out[0], b_norm_post[0], x1, seq_len=t)
    y_prompt = y_prompt.reshape(bsz, t, d)
    p_k = jnp.transpose(k_tail.reshape(bsz, N_KV_HEADS, HEAD_DIM, WINDOW), (0, 3, 1, 2))
    p_v = jnp.transpose(v_tail.reshape(bsz, N_KV_HEADS, HEAD_DIM, WINDOW), (0, 3, 1, 2))

    return (y_prompt, y_sample.reshape(dbsz, 1, d),
            p_conv[None], p_h.reshape(1, bsz, LRU_WIDTH), p_k, p_v,
            jnp.transpose(s_conv_t, (1, 0, 2))[None], hs[None],
            ks.reshape(dbsz, 1, N_KV_HEADS, HEAD_DIM), vs.reshape(dbsz, 1, N_KV_HEADS, HEAD_DIM))
```

```python
import functools
import math

import jax
import jax.numpy as jnp
from jax import lax
from jax.experimental import pallas as pl
from jax.experimental.pallas import tpu as pltpu

F32 = jnp.float32
BF16 = jnp.bfloat16

D_MODEL = 2048
LRU_WIDTH = 2048
LRU_BLOCKS = 8
LRU_BLOCK_W = LRU_WIDTH // LRU_BLOCKS
CONV_W = 4
LRU_C = 8.0
HEAD_DIM = 64
N_Q_HEADS = 32
N_KV_HEADS = 8
GROUP = N_Q_HEADS // N_KV_HEADS
ATT_WIDTH = N_Q_HEADS * HEAD_DIM
KV_WIDTH = N_KV_HEADS * HEAD_DIM
WINDOW = 128
BLOCK = WINDOW
N_BUCKETS = 32
MAX_DISTANCE = 128
RMS_EPS = 1e-6
NEG_INF = -1e30
LOG2_E = 1.4426950408889634
PAST_LEN = 16384

V7X_VMEM_BYTES = 64 * 1024 * 1024
VMEM_LIMIT = V7X_VMEM_BYTES - 8 * 1024 * 1024
SUBLANES = 8
LANES = 128
HEADS_PER_TILE = LANES // HEAD_DIM
SLABS = GROUP // HEADS_PER_TILE
MXU_COLS = 256
SUB_ROWS = 256
W_CHUNK = 512
ATT_ROWS = 64
ATT_SKEW = 2
ATT_SLOTS = 2 * ATT_SKEW + 2
HALF_Q = BLOCK // 2
ATT_KEYS = WINDOW + HALF_Q
SEQS_PER_STEP = 8
SCAN_PIECES_PER_PROJ_PIECE = 2
ATT_PROJ_TILE = 2 * BLOCK
PROJ_TAIL_PIECES = 4
ATT_ROUNDS_PER_PROJ_PIECE = 4
SQRT_FLOOR = 1e-30


def _head_order(half):
    heads = list(range(GROUP))
    return heads if half == 0 else [h ^ 1 for h in heads]


def _params(*semantics):
    return pltpu.CompilerParams(dimension_semantics=semantics, vmem_limit_bytes=VMEM_LIMIT)


def _resident(shape):
    zeros = (0,) * len(shape)
    return pl.BlockSpec(shape, lambda *_: zeros, pipeline_mode=pl.Buffered(1))


def _rms_scale(x):
    return lax.rsqrt(jnp.mean(x * x, axis=-1, keepdims=True) + RMS_EPS)


def _silu(x):
    h = 0.5 * x
    return h * jnp.tanh(h) + h


def _segment_major(rows, inverse=False):
    seg = rows // SUBLANES
    r = lax.broadcasted_iota(jnp.int32, (rows, rows), 0)
    c = lax.broadcasted_iota(jnp.int32, (rows, rows), 1)
    if inverse:
        src = (r % seg) * SUBLANES + r // seg
    else:
        src = (r % SUBLANES) * seg + r // SUBLANES
    return jnp.where(c == src, 1.0, 0.0).astype(BF16)


def _phase_specs(nchunk, tm, k):
    chunk_w = pl.BlockSpec((k, W_CHUNK), lambda i: (0, jnp.minimum(i, nchunk - 1)))
    tile = lambda n: pl.BlockSpec((tm, n), lambda i: (jnp.maximum(i - nchunk, 0), 0))
    return chunk_w, tile


def _dup_heads(x):
    low = lax.broadcasted_iota(jnp.int32, (x.shape[0], LANES), 1) < HEAD_DIM
    out = []
    for c in range(x.shape[1] // LANES):
        col = x[:, c * LANES:(c + 1) * LANES]
        swapped = pltpu.roll(col, HEAD_DIM, axis=1)
        out += [jnp.where(low, col, swapped), jnp.where(low, swapped, col)]
    return jnp.concatenate(out, axis=1)


def _norm_proj_kvq_kernel(nkv, nqg, xs_ref, x_ref, gkv_ref, gq_ref, wkv_ref, wqg_ref,
                          ks_ref, vs_ref, qs_ref, gates_ref,
                          q_ref, gate_ref, kdup_ref, vt_ref, ktail_ref, vtail_ref,
                          wkv_scr, wqg_scr):
    i = pl.program_id(0)
    nchunk = nkv + nqg
    q_chunks = ATT_WIDTH // W_CHUNK
    q_scale = 1.0 / math.sqrt(HEAD_DIM)
    q_scale_log2 = q_scale * LOG2_E

    def sample_rows(g_ref):
        xs = xs_ref[...]
        return (xs * _rms_scale(xs) * g_ref[...]).astype(BF16)

    @pl.when(i < nkv)
    def _():
        wb = wkv_ref[...].astype(BF16)
        wkv_scr[i] = wb
        r = jnp.dot(sample_rows(gkv_ref), wb, preferred_element_type=F32)

        @pl.when(i == 0)
        def _():
            ks_ref[...] = r

        @pl.when(i == 1)
        def _():
            vs_ref[...] = r

    @pl.when((i >= nkv) & (i < nchunk))
    def _():
        c = i - nkv
        wb = wqg_ref[...].astype(BF16)
        wqg_scr[c] = wb
        r = jnp.dot(sample_rows(gq_ref), wb, preferred_element_type=F32)

        @pl.when(c < q_chunks)
        def _():
            qs_ref[...] = r * q_scale

        @pl.when(c >= q_chunks)
        def _():
            gates_ref[...] = r

    @pl.when(i >= nchunk)
    def _():
        tm = x_ref.shape[0]
        for rs in _row_blocks(tm):
            x = x_ref[rs, :]
            xh = x * _rms_scale(x)
            xkv = (xh * gkv_ref[...]).astype(BF16)
            xq = (xh * gq_ref[...]).astype(BF16)
            k = jnp.dot(xkv, wkv_scr[0], preferred_element_type=F32)
            v = jnp.dot(xkv, wkv_scr[1], preferred_element_type=F32)
            kdup_ref[rs, :] = _dup_heads(k).astype(BF16)
            vt_ref[:, rs] = _dup_heads(v).T.astype(BF16)
            for c in range(nqg):
                r = jnp.dot(xq, wqg_scr[c], preferred_element_type=F32)
                if c < q_chunks:
                    q_ref[rs, c * W_CHUNK:(c + 1) * W_CHUNK] = (r * q_scale_log2).astype(q_ref.dtype)
                else:
                    cc = c - q_chunks
                    gate_ref[rs, cc * W_CHUNK:(cc + 1) * W_CHUNK] = r
        ktail_ref[0] = k[k.shape[0] - WINDOW:].T
        vtail_ref[0] = v[v.shape[0] - WINDOW:].T


def norm_proj_kvq(xs, x, g_kv, g_q, w_kv, w_qg, tm, seq_len):
    m, d = x.shape
    ns = xs.shape[0]
    assert w_kv.shape[1] == 2 * KV_WIDTH == 2 * W_CHUNK and seq_len % tm == 0 and tm >= WINDOW
    nkv, nqg = w_kv.shape[1] // W_CHUNK, w_qg.shape[1] // W_CHUNK
    nchunk = nkv + nqg
    tiles = seq_len // tm
    tile = lambda n: pl.BlockSpec((tm, n), lambda i: (jnp.maximum(i - nchunk, 0), 0))
    tail = pl.BlockSpec((1, KV_WIDTH, WINDOW), lambda i: (jnp.maximum(i - nchunk, 0) // tiles, 0, 0))
    kv_chunk = lambda i: (0, jnp.minimum(i, nkv - 1))
    qg_chunk = lambda i: (0, jnp.clip(i - nkv, 0, nqg - 1))
    q_chunks = ATT_WIDTH // W_CHUNK
    whole_s = lambda n: pl.BlockSpec((ns, n), lambda i: (0, 0))
    return pl.pallas_call(
        functools.partial(_norm_proj_kvq_kernel, nkv, nqg),
        grid=(nchunk + m // tm,),
        in_specs=[
            _resident(xs.shape), tile(d), _resident((1, d)), _resident((1, d)),
            pl.BlockSpec((d, W_CHUNK), kv_chunk), pl.BlockSpec((d, W_CHUNK), qg_chunk),
        ],
        out_specs=[
            whole_s(KV_WIDTH), whole_s(KV_WIDTH),
            pl.BlockSpec((ns, W_CHUNK), lambda i: (0, jnp.clip(i - nkv, 0, q_chunks - 1))),
            pl.BlockSpec((ns, W_CHUNK), lambda i: (0, jnp.clip(i - nkv - q_chunks, 0, nqg - q_chunks - 1))),
            tile(ATT_WIDTH), tile(ATT_WIDTH), tile(2 * KV_WIDTH),
            pl.BlockSpec((2 * KV_WIDTH, tm), lambda i: (0, jnp.maximum(i - nchunk, 0))), tail, tail,
        ],
        out_shape=[
            jax.ShapeDtypeStruct((ns, KV_WIDTH), F32),
            jax.ShapeDtypeStruct((ns, KV_WIDTH), F32),
            jax.ShapeDtypeStruct((ns, ATT_WIDTH), F32),
            jax.ShapeDtypeStruct((ns, w_qg.shape[1] - ATT_WIDTH), F32),
            jax.ShapeDtypeStruct((m, ATT_WIDTH), BF16),
            jax.ShapeDtypeStruct((m, ATT_WIDTH), F32),
            jax.ShapeDtypeStruct((m, 2 * KV_WIDTH), BF16),
            jax.ShapeDtypeStruct((2 * KV_WIDTH, m), BF16),
            jax.ShapeDtypeStruct((m // seq_len, KV_WIDTH, WINDOW), F32),
            jax.ShapeDtypeStruct((m // seq_len, KV_WIDTH, WINDOW), F32),
        ],
        scratch_shapes=[pltpu.VMEM((nkv, d, W_CHUNK), BF16), pltpu.VMEM((nqg, d, W_CHUNK), BF16)],
        compiler_params=_params("arbitrary"),
        name="norm_proj_kvq",
    )(xs, x, g_kv.reshape(1, d), g_q.reshape(1, d), w_kv, w_qg)


def _row_blocks(rows):
    sub = min(rows, SUB_ROWS)
    return [slice(r, r + sub) for r in range(0, rows, sub)]


def _proj_norm_res_kernel(nchunk, nparts, as_ref, gs_ref, xs_ref, *refs):
    y_refs = refs[:nparts]
    w_ref, g_ref, x_ref, os_ref, o_ref, w_scr, raw_scr = refs[nparts:]
    i = pl.program_id(0)

    @pl.when(i < nchunk)
    def _():
        wb = w_ref[...].astype(BF16)
        w_scr[i] = wb
        ys = (as_ref[...] * _silu(gs_ref[...])).astype(BF16)
        raw_scr[i] = jnp.dot(ys, wb, preferred_element_type=F32)

    @pl.when(i == nchunk - 1)
    def _():
        o = jnp.concatenate([raw_scr[c] for c in range(nchunk)], axis=1)
        os_ref[...] = xs_ref[...] + o * _rms_scale(o) * g_ref[...]

    @pl.when(i >= nchunk)
    def _():
        tm = x_ref.shape[0]
        part_rows = tm // nparts

        for rs in _row_blocks(tm):
            part, off = divmod(rs.start, part_rows)
            y = y_refs[part][off:off + rs.stop - rs.start, :]
            o = jnp.concatenate([jnp.dot(y, w_scr[c], preferred_element_type=F32)
                                 for c in range(nchunk)], axis=1)
            o_ref[rs, :] = x_ref[rs, :] + o * _rms_scale(o) * g_ref[...]


def proj_norm_res(a_s, gate_s, x_s, y_parts, w, g, x, tm):
    k, d = w.shape
    m = x.shape[0]
    nchunk = d // W_CHUNK
    nparts = len(y_parts)
    assert (tm // nparts) % min(tm, SUB_ROWS) == 0
    chunk_w, tile = _phase_specs(nchunk, tm, k)
    part = pl.BlockSpec((tm // nparts, k), lambda i: (jnp.maximum(i - nchunk, 0), 0))
    return pl.pallas_call(
        functools.partial(_proj_norm_res_kernel, nchunk, nparts),
        grid=(nchunk + m // tm,),
        in_specs=[_resident(a_s.shape), _resident(gate_s.shape), _resident(x_s.shape)]
        + [part] * nparts + [chunk_w, _resident((1, d)), tile(d)],
        out_specs=[pl.BlockSpec(x_s.shape, lambda i: (0, 0)), tile(d)],
        out_shape=[jax.ShapeDtypeStruct(x_s.shape, F32), jax.ShapeDtypeStruct((m, d), F32)],
        scratch_shapes=[pltpu.VMEM((nchunk, k, W_CHUNK), BF16),
                        pltpu.VMEM((nchunk, x_s.shape[0], W_CHUNK), F32)],
        compiler_params=_params("arbitrary"),
        name="proj_norm_res",
    )(a_s, gate_s, x_s, *y_parts, w, g.reshape(1, d), x)


def _lru_gate_dots(conv, wr_half, wi_half):
    cb = conv.astype(BF16)
    return (jnp.dot(cb, wr_half, preferred_element_type=F32),
            jnp.dot(cb, wi_half, preferred_element_type=F32))


def _lru_gates(conv, wr_half, br, wi_half, bi, lam):
    return _lru_gate_math(conv, _lru_gate_dots(conv, wr_half, wi_half), br, bi, lam)


def _lru_gate_math(conv, half_pre, br, bi, lam):
    th_r = jnp.tanh(half_pre[0] + 0.5 * br)
    th_i = jnp.tanh(half_pre[1] + 0.5 * bi)
    nl = -lam
    softplus = jnp.maximum(nl, 0.0) + jnp.log1p(jnp.exp(-jnp.abs(nl)))
    half = (0.5 * LRU_C) * softplus
    x = th_r * half + half
    a = jnp.exp2(x * -LOG2_E)
    z = jnp.tanh(x) * (a * a + 1.0)
    mult = z * lax.rsqrt(jnp.maximum(z, SQRT_FLOOR))
    hc = 0.5 * conv
    return a, mult * (hc * th_i + hc)


def _interleave(*stages):
    live = [[stage, share] for stage, share in stages]
    while live:
        for entry in list(live):
            try:
                for _ in range(entry[1]):
                    next(entry[0])
            except StopIteration:
                live.remove(entry)


def _in_proj_tile(x_ref, rs, g_ref, w_scr, ug_ref):
    x = x_ref[rs, :]
    xn = (x * _rms_scale(x) * g_ref[...]).astype(BF16)
    xn = jnp.dot(_segment_major(xn.shape[0]), xn, preferred_element_type=F32).astype(BF16)
    for c in range(w_scr.shape[0]):
        for n0 in range(0, W_CHUNK, MXU_COLS):
            ug_ref[:, c * W_CHUNK + n0:c * W_CHUNK + n0 + MXU_COLS] = jnp.dot(
                xn, w_scr[c, :, n0:n0 + MXU_COLS], preferred_element_type=F32)
            yield


def _rglru_tile(ug_ref, y_ref, cw_ref, cb_ref, wr_ref, br_ref, wi_ref, bi_ref, lam_ref, h_scr, tail_scr):
    tc = ug_ref.shape[0]
    seg = tc // SUBLANES
    ntaps = CONV_W - 1
    bw = LRU_BLOCK_W
    sub = lax.broadcasted_iota(jnp.int32, (SUBLANES, bw), 0)
    first = sub == 0
    time_order = _segment_major(tc, inverse=True)

    def shift_in(x, row0):
        return jnp.where(first, row0, pltpu.roll(x, 1, axis=0))

    def group(x, j):
        return x[j * SUBLANES:(j + 1) * SUBLANES]

    def store_time_order(cols, y):
        y_ref[:, cols] = jnp.dot(time_order, y, preferred_element_type=F32).astype(y_ref.dtype)

    pending = None
    for n in range(LRU_BLOCKS):
        cs = slice(n * bw, (n + 1) * bw)
        u = ug_ref[:, cs]
        tail = tail_scr[:, cs]
        before = [shift_in(group(u, seg - m), tail[ntaps - m:ntaps - m + 1])
                  for m in range(ntaps, 0, -1)]
        ext = jnp.concatenate(before + [u], axis=0)
        tail_scr[:, cs] = jnp.concatenate(
            [group(u, seg - m)[SUBLANES - 1:] for m in range(ntaps, 0, -1)], axis=0)
        cw = cw_ref[:, cs]
        conv = cb_ref[:, cs]
        for tap in range(CONV_W):
            conv = conv + ext[tap * SUBLANES:tap * SUBLANES + tc] * cw[tap:tap + 1]
        yield

        half_pre = _lru_gate_dots(conv, wr_ref[n], wi_ref[n])
        yield

        if pending is not None:
            store_time_order(*pending)
        yield

        a, b = _lru_gate_math(conv, half_pre, br_ref[:, cs], bi_ref[:, cs], lam_ref[:, cs])

        h = b[:SUBLANES]
        acc = a[:SUBLANES]
        h_loc, a_cum = [h], [acc]
        for j in range(1, seg):
            sl = slice(j * SUBLANES, (j + 1) * SUBLANES)
            h = a[sl] * h + b[sl]
            acc = a[sl] * acc
            h_loc.append(h)
            a_cum.append(acc)

        step = 1
        while step < SUBLANES:
            keep = sub >= step
            h = jnp.where(keep, acc * pltpu.roll(h, step, axis=0) + h, h)
            acc = jnp.where(keep, acc * pltpu.roll(acc, step, axis=0), acc)
            step *= 2
        h_prev = h_scr[:, cs]
        after = h + acc * h_prev
        h_in = shift_in(after, h_prev)
        h_scr[:, cs] = after[SUBLANES - 1:]

        hs = jnp.concatenate([h_loc[j] + a_cum[j] * h_in for j in range(seg)], axis=0)
        y = (hs * _silu(ug_ref[:, LRU_WIDTH + n * bw:LRU_WIDTH + (n + 1) * bw])).astype(BF16)
        pending = (cs, y)
        yield

    store_time_order(*pending)
    yield


def _rglru_front_kernel(nchunk, npairs, chunks, xs_ref, x_ref, g_ref, w_ref, cprev_ref, h0_ref,
                        scprev_ref, sh0_ref, cw_ref, cb_ref, wr_ref, br_ref, wi_ref, bi_ref, lam_ref,
                        gs_ref, hs_ref, scnew_ref, y_even_ref, y_odd_ref, cnew_ref, hlast_ref,
                        w_scr, wr_scr, wi_scr, us_scr, ug0_scr, ug1_scr, h_scr, tail_scr):
    i = pl.program_id(0)
    p = i - nchunk
    tc = SUB_ROWS
    half = nchunk // 2
    lru = (cw_ref, cb_ref, wr_scr, br_ref, wi_scr, bi_ref, lam_ref, h_scr, tail_scr)

    @pl.when(i < nchunk)
    def _():
        wb = w_ref[...].astype(BF16)
        w_scr[i] = wb
        xs = xs_ref[...]
        xsn = (xs * _rms_scale(xs) * g_ref[...]).astype(BF16)
        r = jnp.dot(xsn, wb, preferred_element_type=F32)
        gs_ref[...] = r

        @pl.when(i < half)
        def _():
            us_scr[i] = r

    @pl.when(i == nchunk - 1)
    def _():
        wr_scr[...] = (0.5 * wr_ref[...]).astype(BF16)
        wi_scr[...] = (0.5 * wi_ref[...]).astype(BF16)
        bw = LRU_BLOCK_W
        for n in range(LRU_BLOCKS):
            cs = slice(n * bw, (n + 1) * bw)
            c, off = divmod(n * bw, W_CHUNK)
            u = us_scr[c, :, off:off + bw]
            cw = cw_ref[:, cs]
            conv = cb_ref[:, cs]
            for tap in range(CONV_W - 1):
                conv = conv + scprev_ref[tap, :, cs] * cw[tap:tap + 1]
                if tap > 0:
                    scnew_ref[tap - 1, :, cs] = scprev_ref[tap, :, cs]
            conv = conv + u * cw[CONV_W - 1:]
            scnew_ref[CONV_W - 2, :, cs] = u
            a, b = _lru_gates(conv, wr_scr[n], br_ref[:, cs], wi_scr[n], bi_ref[:, cs], lam_ref[:, cs])
            hs_ref[:, cs] = a * sh0_ref[:, cs] + b

    def project_even():
        return _in_proj_tile(x_ref, slice(0, tc), g_ref, w_scr, ug0_scr)

    def scan_odd():
        return _rglru_tile(ug1_scr, y_odd_ref, *lru)

    @pl.when(p == 0)
    def _():
        _interleave((project_even(), 1))

    @pl.when((p > 0) & (p < npairs))
    def _():
        _interleave((project_even(), 1), (scan_odd(), SCAN_PIECES_PER_PROJ_PIECE))

    @pl.when(p == npairs)
    def _():
        _interleave((scan_odd(), 1))

    @pl.when(p > 0)
    def _():
        hlast_ref[0] = h_scr[...]
        cnew_ref[0] = tail_scr[...]

    @pl.when((p >= 0) & (p < npairs))
    def _():
        @pl.when((2 * p) % chunks == 0)
        def _():
            h_scr[...] = h0_ref[0]
            tail_scr[...] = cprev_ref[0]

        _interleave((_in_proj_tile(x_ref, slice(tc, 2 * tc), g_ref, w_scr, ug1_scr), 1),
                    (_rglru_tile(ug0_scr, y_even_ref, *lru), SCAN_PIECES_PER_PROJ_PIECE))


def rglru_front(xs, s_conv_prev, s_h0, x, conv_prev, h0, g, w_in, conv_w, conv_b, w_r, b_r, w_i, b_i, lam,
                seq_len):
    m, d = x.shape
    ns = xs.shape[0]
    w = w_in.shape[1] // 2
    tc = SUB_ROWS
    nchunk = w_in.shape[1] // W_CHUNK
    half = nchunk // 2
    bsz = m // seq_len
    chunks = seq_len // tc
    npairs = m // (2 * tc)
    assert seq_len % (2 * tc) == 0 and tc % (SUBLANES * SUBLANES) == 0 and tc // SUBLANES > CONV_W
    pair = lambda i: jnp.clip(i - nchunk, 0, npairs - 1)
    last = npairs * 2 - 1
    seq_in = lambda i: (jnp.clip(2 * (i - nchunk), 0, last) // chunks, 0, 0)
    seq_out = lambda i: (jnp.clip(2 * (i - nchunk) - 1, 0, last) // chunks, 0, 0)
    state_in = lambda rows: pl.BlockSpec((1, rows, w), seq_in)
    state_out = lambda rows: pl.BlockSpec((1, rows, w), seq_out)
    chunk = lambda i: (0, jnp.minimum(i, nchunk - 1))
    return pl.pallas_call(
        functools.partial(_rglru_front_kernel, nchunk, npairs, chunks),
        grid=(nchunk + npairs + 1,),
        in_specs=[_resident(xs.shape), pl.BlockSpec((2 * tc, d), lambda i: (pair(i), 0)), _resident((1, d)),
                  pl.BlockSpec((d, W_CHUNK), chunk), state_in(CONV_W - 1), state_in(1),
                  _resident(s_conv_prev.shape), _resident(s_h0.shape),
                  _resident((CONV_W, w)), _resident((1, w)), _resident(w_r.shape), _resident((1, w)),
                  _resident(w_i.shape), _resident((1, w)), _resident((1, w))],
        out_specs=[
            pl.BlockSpec((ns, W_CHUNK), lambda i: (0, jnp.clip(i - half, 0, half - 1))),
            pl.BlockSpec((ns, w), lambda i: (0, 0)),
            pl.BlockSpec(s_conv_prev.shape, lambda i: (0, 0, 0)),
            pl.BlockSpec((tc, w), lambda i: (pair(i), 0)),
            pl.BlockSpec((tc, w), lambda i: (jnp.clip(i - nchunk - 1, 0, npairs - 1), 0)),
            state_out(CONV_W - 1), state_out(1),
        ],
        out_shape=[
            jax.ShapeDtypeStruct((ns, w), F32),
            jax.ShapeDtypeStruct((ns, w), F32),
            jax.ShapeDtypeStruct(s_conv_prev.shape, F32),
            jax.ShapeDtypeStruct((m // 2, w), BF16),
            jax.ShapeDtypeStruct((m // 2, w), BF16),
            jax.ShapeDtypeStruct((bsz, CONV_W - 1, w), F32),
            jax.ShapeDtypeStruct((bsz, 1, w), F32),
        ],
        scratch_shapes=[pltpu.VMEM((nchunk, d, W_CHUNK), BF16),
                        pltpu.VMEM(w_r.shape, BF16), pltpu.VMEM(w_i.shape, BF16),
                        pltpu.VMEM((half, ns, W_CHUNK), F32),
                        pltpu.VMEM((tc, 2 * w), F32), pltpu.VMEM((tc, 2 * w), F32),
                        pltpu.VMEM((1, w), F32), pltpu.VMEM((CONV_W - 1, w), F32)],
        compiler_params=_params("arbitrary"),
        name="rglru_front",
    )(xs, x, g.reshape(1, d), w_in, conv_prev, h0.reshape(bsz, 1, w), s_conv_prev, s_h0,
      conv_w, conv_b.reshape(1, w), w_r, b_r.reshape(1, w), w_i, b_i.reshape(1, w), lam.reshape(1, w))


def _buckets(dist):
    n = jnp.maximum(dist, 0)
    max_exact = N_BUCKETS // 2
    nf = jnp.maximum(n, 1).astype(F32)
    large = max_exact + jnp.floor(jnp.log(nf / max_exact) / math.log(MAX_DISTANCE / max_exact)
                                  * (N_BUCKETS - max_exact)).astype(jnp.int32)
    large = jnp.minimum(large, N_BUCKETS - 1)
    return jnp.where(n < max_exact, n, large)


def _lookup(bucket, valid, table_ref, head):
    bias = jnp.zeros(bucket.shape, F32)
    for b in range(N_BUCKETS):
        bias = jnp.where(bucket == b, table_ref[b, head], bias)
    return jnp.where(valid, bias, NEG_INF)


def _bias_kernel(table_ref, sinks_ref, band_ref, sinkt_ref, past_ref, new_ref):
    hk = pl.program_id(0)
    dist = lax.broadcasted_iota(jnp.int32, (1, LANES), 1)
    bucket = _buckets(dist)
    key_row = lax.broadcasted_iota(jnp.int32, (ATT_KEYS, HALF_Q), 0)
    true_dist = lax.broadcasted_iota(jnp.int32, (ATT_KEYS, HALF_Q), 1) + BLOCK - key_row
    visible = (true_dist >= 0) & (true_dist < WINDOW)

    def band(head):
        row = _lookup(bucket, dist < WINDOW, table_ref, head) * LOG2_E
        full = pltpu.roll(jnp.broadcast_to(row, (ATT_KEYS, LANES)), 0, axis=1, stride=1, stride_axis=0)
        return jnp.where(visible, full[:, :HALF_Q], NEG_INF * LOG2_E)

    rows = past_ref.shape[2]
    d_past = rows - lax.broadcasted_iota(jnp.int32, (1, rows), 1)
    b_past = _buckets(d_past)
    ok_past = (d_past >= 0) & (d_past < WINDOW)
    d_new = jnp.zeros((1, LANES), jnp.int32)
    b_new = _buckets(d_new)
    for g in range(GROUP):
        head = hk * GROUP + g
        bias = band(head)
        for half in range(2):
            slot = _head_order(half).index(g)
            cs = slice(slot * HALF_Q, (slot + 1) * HALF_Q)
            band_ref[0, half, 0, :, cs] = bias
            prev_rows = BLOCK - half * HALF_Q
            band_ref[1, half, 0, :, cs] = jnp.where(key_row < prev_rows, NEG_INF, bias)
            sinkt_ref[half, 0, :, cs] = jnp.full((1, HALF_Q), sinks_ref[head] * LOG2_E, F32)
        past_ref[0, g:g + 1, :] = _lookup(b_past, ok_past, table_ref, head)
        new_ref[0, g:g + 1, :] = _lookup(b_new, d_new == 0, table_ref, head)


def bias_tables(table, sinks, past_rows):
    assert WINDOW <= LANES and BLOCK % LANES == 0
    smem = pl.BlockSpec(memory_space=pltpu.SMEM)
    return pl.pallas_call(
        _bias_kernel,
        grid=(N_KV_HEADS,),
        in_specs=[smem, smem],
        out_specs=[
            pl.BlockSpec((2, 2, 1, ATT_KEYS, GROUP * HALF_Q), lambda h: (0, 0, h, 0, 0)),
            pl.BlockSpec((2, 1, 1, GROUP * HALF_Q), lambda h: (0, h, 0, 0)),
            pl.BlockSpec((1, GROUP, past_rows), lambda h: (h, 0, 0)),
            pl.BlockSpec((1, GROUP, LANES), lambda h: (h, 0, 0)),
        ],
        out_shape=[
            jax.ShapeDtypeStruct((2, 2, N_KV_HEADS, ATT_KEYS, GROUP * HALF_Q), F32),
            jax.ShapeDtypeStruct((2, N_KV_HEADS, 1, GROUP * HALF_Q), F32),
            jax.ShapeDtypeStruct((N_KV_HEADS, GROUP, past_rows), F32),
            jax.ShapeDtypeStruct((N_KV_HEADS, GROUP, LANES), F32),
        ],
        compiler_params=_params("parallel"),
        name="bias_tables",
    )(table, sinks)


def _band_attn_rounds(first_tile, q_ref, kp_ref, kc_ref, vp_ref, vc_ref, gate_ref, bias_ref, sink_ref,
                      y_ref, s_scr, p_scr):
    nt = (((1,), (1,)), ((), ()))
    low = (lax.broadcasted_iota(jnp.int32, (1, LANES), 1) < HEAD_DIM)
    keep_low = low.astype(BF16)
    keep_high = 1 - keep_low
    keep = (keep_low, keep_high)
    nkeys = 2 * BLOCK
    ones_rows = jnp.where(lax.broadcasted_iota(jnp.int32, (2 * SUBLANES, nkeys), 0) == 0,
                          1.0, 0.0).astype(BF16)
    rows = ATT_ROWS
    nslot = s_scr.shape[0]
    items = [(blk, hk, half) for blk in range(q_ref.shape[0] // BLOCK)
             for hk in range(N_KV_HEADS) for half in range(2)]
    assert nslot % 2 == 0

    def rows_of(blk):
        return slice(blk * BLOCK, (blk + 1) * BLOCK)

    def key_rows(half):
        return slice(half * HALF_Q, half * HALF_Q + ATT_KEYS)

    for slot in range(nslot):
        dead = slice(ATT_KEYS, nkeys) if slot % 2 == 0 else slice(0, HALF_Q)
        p_scr[slot, dead, :] = jnp.zeros((HALF_Q, p_scr.shape[2]), BF16)

    def scores(idx):
        blk, hk, half = items[idx]
        variant = first_tile if blk == 0 else 0
        cs = slice(hk * LANES, (hk + 1) * LANES)
        k_prev = kp_ref[:, cs] if blk == 0 else kc_ref[rows_of(blk - 1), cs]
        k_cur = kc_ref[rows_of(blk), cs]
        kd = (jnp.concatenate([k_prev, k_cur[:HALF_Q]], axis=0) if half == 0
              else jnp.concatenate([k_prev[HALF_Q:], k_cur], axis=0))
        q0 = blk * BLOCK + half * HALF_Q
        qs = jnp.concatenate(
            [q_ref[q0:q0 + HALF_Q, (hk * SLABS + h // HEADS_PER_TILE) * LANES:
                   (hk * SLABS + h // HEADS_PER_TILE + 1) * LANES] * keep[h % HEADS_PER_TILE]
             for h in _head_order(half)], axis=0)
        s = lax.dot_general(kd, qs, nt, preferred_element_type=F32) + bias_ref[variant, half, hk]
        s_scr[idx % nslot, key_rows(half), :] = s
        return jnp.maximum(jnp.max(s, axis=0, keepdims=True), sink_ref[half, hk])

    def softmax(idx, m):
        blk, hk, half = items[idx]
        slot = idx % nslot
        lo = half * HALF_Q
        for r in range(lo, lo + ATT_KEYS, rows):
            p_scr[slot, r:r + rows, :] = jnp.exp2(s_scr[slot, r:r + rows, :] - m).astype(BF16)
        return jnp.exp2(sink_ref[half, hk] - m)

    def weighted_values(idx):
        blk, hk, half = items[idx]
        vs = slice(hk * LANES, hk * LANES + HEAD_DIM)
        v_prev = vp_ref[vs, :] if blk == 0 else vc_ref[vs, rows_of(blk - 1)]
        vt = jnp.concatenate([v_prev, vc_ref[vs, rows_of(blk)]], axis=1)
        lhs_v = jnp.concatenate([vt, ones_rows], axis=0)
        return jnp.dot(lhs_v, p_scr[idx % nslot], preferred_element_type=F32)

    low_q = lax.broadcasted_iota(jnp.int32, (HEAD_DIM, LANES), 1) < HALF_Q

    def finish(blk, hk, ots, sink_ws):
        o = [ots[half][:HEAD_DIM] * (1.0 / (ots[half][HEAD_DIM:HEAD_DIM + 1] + sink_ws[half]))
             for half in range(2)]
        for sl in range(SLABS):
            a, b = (oh[:, sl * LANES:(sl + 1) * LANES] for oh in o)
            even = jnp.where(low_q, a, b)
            odd = pltpu.roll(jnp.where(low_q, b, a), HALF_Q, axis=1)
            pair = jnp.concatenate([even, odd], axis=0)
            c0 = (hk * SLABS + sl) * LANES
            y_ref[rows_of(blk), c0:c0 + LANES] = (
                pair.T * _silu(gate_ref[rows_of(blk), c0:c0 + LANES])).astype(y_ref.dtype)

    n = len(items)
    offs, sink_ws, outs = {}, {}, {}
    for k in range(-2 * ATT_SKEW, n + ATT_SKEW):
        if 0 <= k + 2 * ATT_SKEW < n:
            offs[k + 2 * ATT_SKEW] = scores(k + 2 * ATT_SKEW)
        if 0 <= k + ATT_SKEW < n:
            sink_ws[k + ATT_SKEW] = softmax(k + ATT_SKEW, offs.pop(k + ATT_SKEW))
        if 0 <= k < n:
            outs[k] = weighted_values(k)
        j = k - ATT_SKEW
        if 0 <= j < n and items[j][2] == 1:
            finish(items[j][0], items[j][1], [outs.pop(j - 1), outs.pop(j)],
                   [sink_ws.pop(j - 1), sink_ws.pop(j)])
        yield


def _proj_tile(y_scr, w_scr, g_ref, x_ref, o_ref, raw_scr):
    y = y_scr[...]
    for c in range(w_scr.shape[0]):
        for n0 in range(0, W_CHUNK, MXU_COLS):
            raw_scr[:, c * W_CHUNK + n0:c * W_CHUNK + n0 + MXU_COLS] = jnp.dot(
                y, w_scr[c, :, n0:n0 + MXU_COLS], preferred_element_type=F32)
            yield
    step = y_scr.shape[0] // PROJ_TAIL_PIECES
    for r0 in range(0, y_scr.shape[0], step):
        o = raw_scr[r0:r0 + step, :]
        o_ref[r0:r0 + step, :] = x_ref[r0:r0 + step, :] + o * _rms_scale(o) * g_ref[...]
        yield


def _attn_proj_kernel(nchunk, ntiles, tiles_per_seq, as_ref, gs_ref, xs_ref,
                      q_ref, kp_ref, kc_ref, vp_ref, vc_ref, gate_ref, bias_ref, sink_ref,
                      w_ref, g_ref, x_ref, os_ref, o_ref,
                      w_scr, raws_scr, s_scr, p_scr, ynew_scr, yold_scr, raw_scr):
    i = pl.program_id(0)
    p = i - nchunk

    @pl.when(i < nchunk)
    def _():
        wb = w_ref[...].astype(BF16)
        w_scr[i] = wb
        ys = (as_ref[...] * _silu(gs_ref[...])).astype(BF16)
        raws_scr[i] = jnp.dot(ys, wb, preferred_element_type=F32)

    @pl.when(i == nchunk - 1)
    def _():
        o = jnp.concatenate([raws_scr[c] for c in range(nchunk)], axis=1)
        os_ref[...] = xs_ref[...] + o * _rms_scale(o) * g_ref[...]

    def attend():
        first = (p % tiles_per_seq == 0).astype(jnp.int32)
        return _band_attn_rounds(first, q_ref, kp_ref, kc_ref, vp_ref, vc_ref, gate_ref, bias_ref,
                                 sink_ref, ynew_scr, s_scr, p_scr)

    def project():
        return _proj_tile(yold_scr, w_scr, g_ref, x_ref, o_ref, raw_scr)

    @pl.when(p == 0)
    def _():
        _interleave((attend(), 1))
        yold_scr[...] = ynew_scr[...]

    @pl.when((p > 0) & (p < ntiles))
    def _():
        _interleave((attend(), ATT_ROUNDS_PER_PROJ_PIECE), (project(), 1))
        yold_scr[...] = ynew_scr[...]

    @pl.when(p == ntiles)
    def _():
        _interleave((project(), 1))


def attn_proj(a_s, gate_s, x_s, q, kdup, vt, gate, bias_band, sink_t, w, g, x, seq_len):
    k, d = w.shape
    m = x.shape[0]
    tm = ATT_PROJ_TILE
    nchunk = d // W_CHUNK
    ntiles = m // tm
    tiles_per_seq = seq_len // tm
    per_tile = tm // BLOCK
    assert seq_len % tm == 0 and tm % BLOCK == 0
    att_tile = lambda i: jnp.clip(i - nchunk, 0, ntiles - 1)
    proj_tile = lambda i: jnp.clip(i - nchunk - 1, 0, ntiles - 1)

    def before(i):
        t = att_tile(i)
        return per_tile * t - jnp.where(t % tiles_per_seq == 0, 0, 1)

    rows = lambda n: pl.BlockSpec((tm, n), lambda i: (att_tile(i), 0))
    score_tile = (2 * BLOCK, GROUP * HALF_Q)
    return pl.pallas_call(
        functools.partial(_attn_proj_kernel, nchunk, ntiles, tiles_per_seq),
        grid=(nchunk + ntiles + 1,),
        in_specs=[
            _resident(a_s.shape), _resident(gate_s.shape), _resident(x_s.shape),
            rows(ATT_WIDTH), pl.BlockSpec((BLOCK, 2 * KV_WIDTH), lambda i: (before(i), 0)),
            rows(2 * KV_WIDTH), pl.BlockSpec((2 * KV_WIDTH, BLOCK), lambda i: (0, before(i))),
            pl.BlockSpec((2 * KV_WIDTH, tm), lambda i: (0, att_tile(i))), rows(ATT_WIDTH),
            _resident(bias_band.shape), _resident(sink_t.shape),
            pl.BlockSpec((k, W_CHUNK), lambda i: (0, jnp.minimum(i, nchunk - 1))), _resident((1, d)),
            pl.BlockSpec((tm, d), lambda i: (proj_tile(i), 0)),
        ],
        out_specs=[pl.BlockSpec(x_s.shape, lambda i: (0, 0)),
                   pl.BlockSpec((tm, d), lambda i: (proj_tile(i), 0))],
        out_shape=[jax.ShapeDtypeStruct(x_s.shape, F32), jax.ShapeDtypeStruct((m, d), F32)],
        scratch_shapes=[pltpu.VMEM((nchunk, k, W_CHUNK), BF16),
                        pltpu.VMEM((nchunk, x_s.shape[0], W_CHUNK), F32),
                        pltpu.VMEM((ATT_SLOTS,) + score_tile, F32),
                        pltpu.VMEM((ATT_SLOTS,) + score_tile, BF16),
                        pltpu.VMEM((tm, k), BF16), pltpu.VMEM((tm, k), BF16),
                        pltpu.VMEM((tm, d), F32)],
        compiler_params=_params("arbitrary"),
        name="attn_proj",
    )(a_s, gate_s, x_s, q, kdup, kdup, vt, vt, gate, bias_band, sink_t, w, g.reshape(1, d), x)


def _cached_attn_kernel(q_ref, ckt_ref, cvt_ref, kn_ref, vn_ref, sinks_ref, bpast_ref, bnew_ref, o_ref):
    shape = (N_Q_HEADS, KV_WIDTH)
    lane_kv = lax.broadcasted_iota(jnp.int32, shape, 1) // HEAD_DIM
    row_kv = lax.broadcasted_iota(jnp.int32, shape, 0) // GROUP
    own = lane_kv == row_kv
    sink = sinks_ref[...]
    nt = (((1,), (1,)), ((), ()))
    seqs = range(q_ref.shape[0])
    qm = []
    for b in seqs:
        q = q_ref[b]
        qt = jnp.concatenate([q] * N_KV_HEADS, axis=1)
        qm.append(jnp.where(own, qt, 0.0).astype(BF16))
    s = [jnp.dot(qm[b], ckt_ref[b].astype(BF16), preferred_element_type=F32) + bpast_ref[...] for b in seqs]
    s_new = [jnp.sum(qm[b].astype(F32) * kn_ref[b:b + 1, :].astype(BF16).astype(F32), axis=-1, keepdims=True)
             + bnew_ref[:, :1] for b in seqs]
    m = [jnp.maximum(jnp.maximum(jnp.max(s[b], axis=-1, keepdims=True), s_new[b]), sink) for b in seqs]
    p = [jnp.exp(s[b] - m[b]) for b in seqs]
    p_new = [jnp.exp(s_new[b] - m[b]) for b in seqs]
    denom = [jnp.sum(p[b], axis=-1, keepdims=True) + p_new[b] + jnp.exp(sink - m[b]) for b in seqs]
    pv = [lax.dot_general(p[b].astype(BF16), cvt_ref[b].astype(BF16), nt, preferred_element_type=F32)
          for b in seqs]
    for b in seqs:
        o_all = pv[b] + p_new[b].astype(BF16).astype(F32) * vn_ref[b:b + 1, :].astype(BF16).astype(F32)
        o_all = jnp.where(own, o_all, 0.0)
        o = o_all[:, :HEAD_DIM]
        for hk in range(1, N_KV_HEADS):
            o = o + o_all[:, hk * HEAD_DIM:(hk + 1) * HEAD_DIM]
        o_ref[b] = o / denom[b]


def cached_attention(q, cache_kt, cache_vt, k_new, v_new, sinks, bias_past, bias_new):
    bsz, _, rows = cache_kt.shape
    nseq = math.gcd(bsz, SEQS_PER_STEP)
    per_seq = lambda r, n: pl.BlockSpec((nseq, r, n), lambda b: (b, 0, 0))
    new_row = pl.BlockSpec((nseq, KV_WIDTH), lambda b: (b, 0))
    return pl.pallas_call(
        _cached_attn_kernel,
        grid=(bsz // nseq,),
        in_specs=[
            per_seq(N_Q_HEADS, HEAD_DIM), per_seq(KV_WIDTH, rows), per_seq(KV_WIDTH, rows),
            new_row, new_row,
            _resident((N_Q_HEADS, 1)), _resident((N_Q_HEADS, rows)), _resident((N_Q_HEADS, LANES)),
        ],
        out_specs=per_seq(N_Q_HEADS, HEAD_DIM),
        out_shape=jax.ShapeDtypeStruct((bsz, N_Q_HEADS, HEAD_DIM), F32),
        compiler_params=_params("parallel"),
        name="cached_attention",
    )(q, cache_kt, cache_vt, k_new, v_new, sinks, bias_past, bias_new)


def kernel(x_prompt, x_sample, state_conv, state_h, cache_k, cache_v, a_norm_pre, a_norm_post,
           a_w_in, a_conv_w, a_conv_b, a_w_r, a_b_r, a_w_i, a_b_i, a_lambda, a_w_out, kv_norm, w_kv,
           b_norm_pre, b_norm_post, b_w_qg, b_sinks, b_w_out, rel_bias_table):
    bsz, t, d = x_prompt.shape
    dbsz, dt, _ = x_sample.shape
    assert a_w_in.shape[0] == 1 and b_w_qg.shape[0] == 1 and dt == 1
    assert t % BLOCK == 0 and t >= WINDOW
    past_rows = cache_k.shape[1]
    assert past_rows == min(WINDOW, PAST_LEN)

    sinks = b_sinks[0]
    bias_band, sink_t, bias_past, bias_new = bias_tables(rel_bias_table, sinks, past_rows)

    tm = 2 * SUB_ROWS
    xp = x_prompt.reshape(bsz * t, d)
    xs = x_sample.reshape(dbsz, d)

    conv0 = jnp.zeros((bsz, CONV_W - 1, LRU_WIDTH), F32)
    h0 = jnp.zeros((bsz, LRU_WIDTH), F32)
    gate_s, hs, s_conv_t, y_even, y_odd, p_conv, p_h = rglru_front(
        xs, jnp.transpose(state_conv[0], (1, 0, 2)), state_h[0], xp, conv0, h0, a_norm_pre[0], a_w_in[0],
        a_conv_w[0], a_conv_b[0], a_w_r[0], a_b_r[0], a_w_i[0], a_b_i[0], a_lambda[0], seq_len=t)
    xs1, x1 = proj_norm_res(hs, gate_s, xs, (y_even, y_odd), a_w_out[0], a_norm_post[0], xp, tm)

    ks, vs, qs, gate_sb, q, gate_b, kdup, vt, k_tail, v_tail = norm_proj_kvq(
        xs1, x1, kv_norm, b_norm_pre[0], w_kv, b_w_qg[0], SUB_ROWS, seq_len=t)
    cache_kt = jnp.transpose(cache_k, (0, 2, 3, 1)).reshape(dbsz, KV_WIDTH, past_rows)
    cache_vt = jnp.transpose(cache_v, (0, 2, 3, 1)).reshape(dbsz, KV_WIDTH, past_rows)
    os_ = cached_attention(qs.reshape(dbsz, N_Q_HEADS, HEAD_DIM), cache_kt, cache_vt,
                           ks, vs,
                           sinks.reshape(N_Q_HEADS, 1), bias_past.reshape(N_Q_HEADS, past_rows),
                           bias_new.reshape(N_Q_HEADS, LANES))
    y_sample, y_prompt = attn_proj(os_.reshape(dbsz, ATT_WIDTH), gate_sb, xs1, q, kdup, vt, gate_b,
                                   bias_band, sink_t, b_w_out[0], b_norm_post[0], x1, seq_len=t)
    y_prompt = y_prompt.reshape(bsz, t, d)
    p_k = jnp.transpose(k_tail.reshape(bsz, N_KV_HEADS, HEAD_DIM, WINDOW), (0, 3, 1, 2))
    p_v = jnp.transpose(v_tail.reshape(bsz, N_KV_HEADS, HEAD_DIM, WINDOW), (0, 3, 1, 2))

    return (y_prompt, y_sample.reshape(dbsz, 1, d),
            p_conv[None], p_h.reshape(1, bsz, LRU_WIDTH), p_k, p_v,
            jnp.transpose(s_conv_t, (1, 0, 2))[None], hs[None],
            ks.reshape(dbsz, 1, N_KV_HEADS, HEAD_DIM), vs.reshape(dbsz, 1, N_KV_HEADS, HEAD_DIM))
```

```python
import functools
import math

import jax
import jax.numpy as jnp
from jax import lax
from jax.experimental import pallas as pl
from jax.experimental.pallas import tpu as pltpu

F32 = jnp.float32
BF16 = jnp.bfloat16

D_MODEL = 2048
LRU_WIDTH = 2048
LRU_BLOCKS = 8
LRU_BLOCK_W = LRU_WIDTH // LRU_BLOCKS
CONV_W = 4
LRU_C = 8.0
HEAD_DIM = 64
N_Q_HEADS = 32
N_KV_HEADS = 8
GROUP = N_Q_HEADS // N_KV_HEADS
ATT_WIDTH = N_Q_HEADS * HEAD_DIM
KV_WIDTH = N_KV_HEADS * HEAD_DIM
WINDOW = 128
BLOCK = WINDOW
N_BUCKETS = 32
MAX_DISTANCE = 128
RMS_EPS = 1e-6
NEG_INF = -1e30
LOG2_E = 1.4426950408889634
PAST_LEN = 16384

V7X_VMEM_BYTES = 64 * 1024 * 1024
VMEM_LIMIT = V7X_VMEM_BYTES - 8 * 1024 * 1024
SUBLANES = 8
LANES = 128
HEADS_PER_TILE = LANES // HEAD_DIM
SLABS = GROUP // HEADS_PER_TILE
MXU_COLS = 256
SUB_ROWS = 256
W_CHUNK = 512
ATT_ROWS = 64
ATT_SKEW = 2
ATT_SLOTS = 2 * ATT_SKEW + 2
HALF_Q = BLOCK // 2
ATT_KEYS = WINDOW + HALF_Q
SEQS_PER_STEP = 8
SCAN_PIECES_PER_PROJ_PIECE = 2
ATT_PROJ_TILE = 2 * BLOCK
PROJ_TAIL_PIECES = 4
ATT_ROUNDS_PER_PROJ_PIECE = 4
SQRT_FLOOR = 1e-30


def _head_order(half):
    heads = list(range(GROUP))
    return heads if half == 0 else [h ^ 1 for h in heads]


def _params(*semantics):
    return pltpu.CompilerParams(dimension_semantics=semantics, vmem_limit_bytes=VMEM_LIMIT)


def _resident(shape):
    zeros = (0,) * len(shape)
    return pl.BlockSpec(shape, lambda *_: zeros, pipeline_mode=pl.Buffered(1))


def _rms_scale(x):
    return lax.rsqrt(jnp.mean(x * x, axis=-1, keepdims=True) + RMS_EPS)


def _silu(x):
    h = 0.5 * x
    return h * jnp.tanh(h) + h


def _segment_major(rows, inverse=False):
    seg = rows // SUBLANES
    r = lax.broadcasted_iota(jnp.int32, (rows, rows), 0)
    c = lax.broadcasted_iota(jnp.int32, (rows, rows), 1)
    if inverse:
        src = (r % seg) * SUBLANES + r // seg
    else:
        src = (r % SUBLANES) * seg + r // SUBLANES
    return jnp.where(c == src, 1.0, 0.0).astype(BF16)


def _phase_specs(nchunk, tm, k):
    chunk_w = pl.BlockSpec((k, W_CHUNK), lambda i: (0, jnp.minimum(i, nchunk - 1)))
    tile = lambda n: pl.BlockSpec((tm, n), lambda i: (jnp.maximum(i - nchunk, 0), 0))
    return chunk_w, tile


def _dup_heads(x):
    low = lax.broadcasted_iota(jnp.int32, (x.shape[0], LANES), 1) < HEAD_DIM
    out = []
    for c in range(x.shape[1] // LANES):
        col = x[:, c * LANES:(c + 1) * LANES]
        swapped = pltpu.roll(col, HEAD_DIM, axis=1)
        out += [jnp.where(low, col, swapped), jnp.where(low, swapped, col)]
    return jnp.concatenate(out, axis=1)


def _norm_proj_kvq_kernel(nkv, nqg, xs_ref, x_ref, gkv_ref, gq_ref, wkv_ref, wqg_ref,
                          ks_ref, vs_ref, qs_ref, gates_ref,
                          q_ref, gate_ref, kdup_ref, vt_ref, ktail_ref, vtail_ref,
                          wkv_scr, wqg_scr):
    i = pl.program_id(0)
    nchunk = nkv + nqg
    q_chunks = ATT_WIDTH // W_CHUNK
    q_scale = 1.0 / math.sqrt(HEAD_DIM)
    q_scale_log2 = q_scale * LOG2_E

    def sample_rows(g_ref):
        xs = xs_ref[...]
        return (xs * _rms_scale(xs) * g_ref[...]).astype(BF16)

    @pl.when(i < nkv)
    def _():
        wb = wkv_ref[...].astype(BF16)
        wkv_scr[i] = wb
        r = jnp.dot(sample_rows(gkv_ref), wb, preferred_element_type=F32)

        @pl.when(i == 0)
        def _():
            ks_ref[...] = r

        @pl.when(i == 1)
        def _():
            vs_ref[...] = r

    @pl.when((i >= nkv) & (i < nchunk))
    def _():
        c = i - nkv
        wb = wqg_ref[...].astype(BF16)
        wqg_scr[c] = wb
        r = jnp.dot(sample_rows(gq_ref), wb, preferred_element_type=F32)

        @pl.when(c < q_chunks)
        def _():
            qs_ref[...] = r * q_scale

        @pl.when(c >= q_chunks)
        def _():
            gates_ref[...] = r

    @pl.when(i >= nchunk)
    def _():
        tm = x_ref.shape[0]
        for rs in _row_blocks(tm):
            x = x_ref[rs, :]
            xh = x * _rms_scale(x)
            xkv = (xh * gkv_ref[...]).astype(BF16)
            xq = (xh * gq_ref[...]).astype(BF16)
            k = jnp.dot(xkv, wkv_scr[0], preferred_element_type=F32)
            v = jnp.dot(xkv, wkv_scr[1], preferred_element_type=F32)
            kdup_ref[rs, :] = _dup_heads(k).astype(BF16)
            vt_ref[:, rs] = _dup_heads(v).T.astype(BF16)
            for c in range(nqg):
                r = jnp.dot(xq, wqg_scr[c], preferred_element_type=F32)
                if c < q_chunks:
                    q_ref[rs, c * W_CHUNK:(c + 1) * W_CHUNK] = (r * q_scale_log2).astype(q_ref.dtype)
                else:
                    cc = c - q_chunks
                    gate_ref[rs, cc * W_CHUNK:(cc + 1) * W_CHUNK] = r
        ktail_ref[0] = k[k.shape[0] - WINDOW:].T
        vtail_ref[0] = v[v.shape[0] - WINDOW:].T


def norm_proj_kvq(xs, x, g_kv, g_q, w_kv, w_qg, tm, seq_len):
    m, d = x.shape
    ns = xs.shape[0]
    assert w_kv.shape[1] == 2 * KV_WIDTH == 2 * W_CHUNK and seq_len % tm == 0 and tm >= WINDOW
    nkv, nqg = w_kv.shape[1] // W_CHUNK, w_qg.shape[1] // W_CHUNK
    nchunk = nkv + nqg
    tiles = seq_len // tm
    tile = lambda n: pl.BlockSpec((tm, n), lambda i: (jnp.maximum(i - nchunk, 0), 0))
    tail = pl.BlockSpec((1, KV_WIDTH, WINDOW), lambda i: (jnp.maximum(i - nchunk, 0) // tiles, 0, 0))
    kv_chunk = lambda i: (0, jnp.minimum(i, nkv - 1))
    qg_chunk = lambda i: (0, jnp.clip(i - nkv, 0, nqg - 1))
    q_chunks = ATT_WIDTH // W_CHUNK
    whole_s = lambda n: pl.BlockSpec((ns, n), lambda i: (0, 0))
    return pl.pallas_call(
        functools.partial(_norm_proj_kvq_kernel, nkv, nqg),
        grid=(nchunk + m // tm,),
        in_specs=[
            _resident(xs.shape), tile(d), _resident((1, d)), _resident((1, d)),
            pl.BlockSpec((d, W_CHUNK), kv_chunk), pl.BlockSpec((d, W_CHUNK), qg_chunk),
        ],
        out_specs=[
            whole_s(KV_WIDTH), whole_s(KV_WIDTH),
            pl.BlockSpec((ns, W_CHUNK), lambda i: (0, jnp.clip(i - nkv, 0, q_chunks - 1))),
            pl.BlockSpec((ns, W_CHUNK), lambda i: (0, jnp.clip(i - nkv - q_chunks, 0, nqg - q_chunks - 1))),
            tile(ATT_WIDTH), tile(ATT_WIDTH), tile(2 * KV_WIDTH),
            pl.BlockSpec((2 * KV_WIDTH, tm), lambda i: (0, jnp.maximum(i - nchunk, 0))), tail, tail,
        ],
        out_shape=[
            jax.ShapeDtypeStruct((ns, KV_WIDTH), F32),
            jax.ShapeDtypeStruct((ns, KV_WIDTH), F32),
            jax.ShapeDtypeStruct((ns, ATT_WIDTH), F32),
            jax.ShapeDtypeStruct((ns, w_qg.shape[1] - ATT_WIDTH), F32),
            jax.ShapeDtypeStruct((m, ATT_WIDTH), BF16),
            jax.ShapeDtypeStruct((m, ATT_WIDTH), F32),
            jax.ShapeDtypeStruct((m, 2 * KV_WIDTH), BF16),
            jax.ShapeDtypeStruct((2 * KV_WIDTH, m), BF16),
            jax.ShapeDtypeStruct((m // seq_len, KV_WIDTH, WINDOW), F32),
            jax.ShapeDtypeStruct((m // seq_len, KV_WIDTH, WINDOW), F32),
        ],
        scratch_shapes=[pltpu.VMEM((nkv, d, W_CHUNK), BF16), pltpu.VMEM((nqg, d, W_CHUNK), BF16)],
        compiler_params=_params("arbitrary"),
        name="norm_proj_kvq",
    )(xs, x, g_kv.reshape(1, d), g_q.reshape(1, d), w_kv, w_qg)


def _row_blocks(rows):
    sub = min(rows, SUB_ROWS)
    return [slice(r, r + sub) for r in range(0, rows, sub)]


def _proj_norm_res_kernel(nchunk, nparts, as_ref, gs_ref, xs_ref, *refs):
    y_refs = refs[:nparts]
    w_ref, g_ref, x_ref, os_ref, o_ref, w_scr, raw_scr = refs[nparts:]
    i = pl.program_id(0)

    @pl.when(i < nchunk)
    def _():
        wb = w_ref[...].astype(BF16)
        w_scr[i] = wb
        ys = (as_ref[...] * _silu(gs_ref[...])).astype(BF16)
        raw_scr[i] = jnp.dot(ys, wb, preferred_element_type=F32)

    @pl.when(i == nchunk - 1)
    def _():
        o = jnp.concatenate([raw_scr[c] for c in range(nchunk)], axis=1)
        os_ref[...] = xs_ref[:, 0, :] + o * _rms_scale(o) * g_ref[...]

    @pl.when(i >= nchunk)
    def _():
        tm = x_ref.shape[0]
        part_rows = tm // nparts

        for rs in _row_blocks(tm):
            part, off = divmod(rs.start, part_rows)
            y = y_refs[part][off:off + rs.stop - rs.start, :]
            o = jnp.concatenate([jnp.dot(y, w_scr[c], preferred_element_type=F32)
                                 for c in range(nchunk)], axis=1)
            o_ref[rs, :] = x_ref[rs, :] + o * _rms_scale(o) * g_ref[...]


def proj_norm_res(a_s, gate_s, x_s, y_parts, w, g, x, tm):
    k, d = w.shape
    m = x.shape[0]
    ns = x_s.shape[0]
    nchunk = d // W_CHUNK
    nparts = len(y_parts)
    assert (tm // nparts) % min(tm, SUB_ROWS) == 0
    chunk_w, tile = _phase_specs(nchunk, tm, k)
    part = pl.BlockSpec((tm // nparts, k), lambda i: (jnp.maximum(i - nchunk, 0), 0))
    return pl.pallas_call(
        functools.partial(_proj_norm_res_kernel, nchunk, nparts),
        grid=(nchunk + m // tm,),
        in_specs=[_resident(a_s.shape), _resident(gate_s.shape), _resident(x_s.shape)]
        + [part] * nparts + [chunk_w, _resident((1, d)), tile(d)],
        out_specs=[pl.BlockSpec((ns, d), lambda i: (0, 0)), tile(d)],
        out_shape=[jax.ShapeDtypeStruct((ns, d), F32), jax.ShapeDtypeStruct((m, d), F32)],
        scratch_shapes=[pltpu.VMEM((nchunk, k, W_CHUNK), BF16),
                        pltpu.VMEM((nchunk, ns, W_CHUNK), F32)],
        compiler_params=_params("arbitrary"),
        name="proj_norm_res",
    )(a_s, gate_s, x_s, *y_parts, w, g.reshape(1, d), x)


def _lru_gate_dots(conv, wr_half, wi_half):
    cb = conv.astype(BF16)
    return (jnp.dot(cb, wr_half, preferred_element_type=F32),
            jnp.dot(cb, wi_half, preferred_element_type=F32))


def _lru_gates(conv, wr_half, br, wi_half, bi, lam):
    return _lru_gate_math(conv, _lru_gate_dots(conv, wr_half, wi_half), br, bi, lam)


def _lru_gate_math(conv, half_pre, br, bi, lam):
    th_r = jnp.tanh(half_pre[0] + 0.5 * br)
    th_i = jnp.tanh(half_pre[1] + 0.5 * bi)
    nl = -lam
    softplus = jnp.maximum(nl, 0.0) + jnp.log1p(jnp.exp(-jnp.abs(nl)))
    half = (0.5 * LRU_C) * softplus
    x = th_r * half + half
    a = jnp.exp2(x * -LOG2_E)
    z = jnp.tanh(x) * (a * a + 1.0)
    mult = z * lax.rsqrt(jnp.maximum(z, SQRT_FLOOR))
    hc = 0.5 * conv
    return a, mult * (hc * th_i + hc)


def _interleave(*stages):
    live = [[stage, share] for stage, share in stages]
    while live:
        for entry in list(live):
            try:
                for _ in range(entry[1]):
                    next(entry[0])
            except StopIteration:
                live.remove(entry)


def _in_proj_tile(x_ref, rs, g_ref, w_scr, ug_ref):
    x = x_ref[rs, :]
    xn = (x * _rms_scale(x) * g_ref[...]).astype(BF16)
    xn = jnp.dot(_segment_major(xn.shape[0]), xn, preferred_element_type=F32).astype(BF16)
    for c in range(w_scr.shape[0]):
        for n0 in range(0, W_CHUNK, MXU_COLS):
            ug_ref[:, c * W_CHUNK + n0:c * W_CHUNK + n0 + MXU_COLS] = jnp.dot(
                xn, w_scr[c, :, n0:n0 + MXU_COLS], preferred_element_type=F32)
            yield


def _rglru_tile(ug_ref, y_ref, cw_ref, cb_ref, wr_ref, br_ref, wi_ref, bi_ref, lam_ref, h_scr, tail_scr):
    tc = ug_ref.shape[0]
    seg = tc // SUBLANES
    ntaps = CONV_W - 1
    bw = LRU_BLOCK_W
    sub = lax.broadcasted_iota(jnp.int32, (SUBLANES, bw), 0)
    first = sub == 0
    time_order = _segment_major(tc, inverse=True)

    def shift_in(x, row0):
        return jnp.where(first, row0, pltpu.roll(x, 1, axis=0))

    def group(x, j):
        return x[j * SUBLANES:(j + 1) * SUBLANES]

    def store_time_order(cols, y):
        y_ref[:, cols] = jnp.dot(time_order, y, preferred_element_type=F32).astype(y_ref.dtype)

    pending = None
    for n in range(LRU_BLOCKS):
        cs = slice(n * bw, (n + 1) * bw)
        u = ug_ref[:, cs]
        tail = tail_scr[:, cs]
        before = [shift_in(group(u, seg - m), tail[ntaps - m:ntaps - m + 1])
                  for m in range(ntaps, 0, -1)]
        ext = jnp.concatenate(before + [u], axis=0)
        tail_scr[:, cs] = jnp.concatenate(
            [group(u, seg - m)[SUBLANES - 1:] for m in range(ntaps, 0, -1)], axis=0)
        cw = cw_ref[:, cs]
        conv = cb_ref[:, cs]
        for tap in range(CONV_W):
            conv = conv + ext[tap * SUBLANES:tap * SUBLANES + tc] * cw[tap:tap + 1]
        yield

        half_pre = _lru_gate_dots(conv, wr_ref[n], wi_ref[n])
        yield

        if pending is not None:
            store_time_order(*pending)
        yield

        a, b = _lru_gate_math(conv, half_pre, br_ref[:, cs], bi_ref[:, cs], lam_ref[:, cs])

        h = b[:SUBLANES]
        acc = a[:SUBLANES]
        h_loc, a_cum = [h], [acc]
        for j in range(1, seg):
            sl = slice(j * SUBLANES, (j + 1) * SUBLANES)
            h = a[sl] * h + b[sl]
            acc = a[sl] * acc
            h_loc.append(h)
            a_cum.append(acc)

        step = 1
        while step < SUBLANES:
            keep = sub >= step
            h = jnp.where(keep, acc * pltpu.roll(h, step, axis=0) + h, h)
            acc = jnp.where(keep, acc * pltpu.roll(acc, step, axis=0), acc)
            step *= 2
        h_prev = h_scr[:, cs]
        after = h + acc * h_prev
        h_in = shift_in(after, h_prev)
        h_scr[:, cs] = after[SUBLANES - 1:]

        hs = jnp.concatenate([h_loc[j] + a_cum[j] * h_in for j in range(seg)], axis=0)
        y = (hs * _silu(ug_ref[:, LRU_WIDTH + n * bw:LRU_WIDTH + (n + 1) * bw])).astype(BF16)
        pending = (cs, y)
        yield

    store_time_order(*pending)
    yield


def _rglru_front_kernel(nchunk, npairs, chunks, xs_ref, x_ref, g_ref, w_ref, cprev_ref, h0_ref,
                        scprev_ref, sh0_ref, cw_ref, cb_ref, wr_ref, br_ref, wi_ref, bi_ref, lam_ref,
                        gs_ref, hs_ref, scnew_ref, y_even_ref, y_odd_ref, cnew_ref, hlast_ref,
                        w_scr, wr_scr, wi_scr, us_scr, ug0_scr, ug1_scr, h_scr, tail_scr):
    i = pl.program_id(0)
    p = i - nchunk
    tc = SUB_ROWS
    half = nchunk // 2
    lru = (cw_ref, cb_ref, wr_scr, br_ref, wi_scr, bi_ref, lam_ref, h_scr, tail_scr)

    @pl.when(i < nchunk)
    def _():
        wb = w_ref[...].astype(BF16)
        w_scr[i] = wb
        xs = xs_ref[:, 0, :]
        xsn = (xs * _rms_scale(xs) * g_ref[...]).astype(BF16)
        r = jnp.dot(xsn, wb, preferred_element_type=F32)
        gs_ref[...] = r

        @pl.when(i < half)
        def _():
            us_scr[i] = r

    @pl.when(i == nchunk - 1)
    def _():
        wr_scr[...] = (0.5 * wr_ref[...]).astype(BF16)
        wi_scr[...] = (0.5 * wi_ref[...]).astype(BF16)
        bw = LRU_BLOCK_W
        for n in range(LRU_BLOCKS):
            cs = slice(n * bw, (n + 1) * bw)
            c, off = divmod(n * bw, W_CHUNK)
            u = us_scr[c, :, off:off + bw]
            cw = cw_ref[:, cs]
            conv = cb_ref[:, cs]
            for tap in range(CONV_W - 1):
                conv = conv + scprev_ref[tap, :, cs] * cw[tap:tap + 1]
                if tap > 0:
                    scnew_ref[tap - 1, :, cs] = scprev_ref[tap, :, cs]
            conv = conv + u * cw[CONV_W - 1:]
            scnew_ref[CONV_W - 2, :, cs] = u
            a, b = _lru_gates(conv, wr_scr[n], br_ref[:, cs], wi_scr[n], bi_ref[:, cs], lam_ref[:, cs])
            hs_ref[:, cs] = a * sh0_ref[:, cs] + b

    def project_even():
        return _in_proj_tile(x_ref, slice(0, tc), g_ref, w_scr, ug0_scr)

    def scan_odd():
        return _rglru_tile(ug1_scr, y_odd_ref, *lru)

    @pl.when(p == 0)
    def _():
        _interleave((project_even(), 1))

    @pl.when((p > 0) & (p < npairs))
    def _():
        _interleave((project_even(), 1), (scan_odd(), SCAN_PIECES_PER_PROJ_PIECE))

    @pl.when(p == npairs)
    def _():
        _interleave((scan_odd(), 1))

    @pl.when(p > 0)
    def _():
        hlast_ref[0] = h_scr[...]
        cnew_ref[0] = tail_scr[...]

    @pl.when((p >= 0) & (p < npairs))
    def _():
        @pl.when((2 * p) % chunks == 0)
        def _():
            h_scr[...] = h0_ref[0]
            tail_scr[...] = cprev_ref[0]

        _interleave((_in_proj_tile(x_ref, slice(tc, 2 * tc), g_ref, w_scr, ug1_scr), 1),
                    (_rglru_tile(ug0_scr, y_even_ref, *lru), SCAN_PIECES_PER_PROJ_PIECE))


def rglru_front(xs, s_conv_prev, s_h0, x, conv_prev, h0, g, w_in, conv_w, conv_b, w_r, b_r, w_i, b_i, lam,
                seq_len):
    m, d = x.shape
    ns = xs.shape[0]
    w = w_in.shape[1] // 2
    tc = SUB_ROWS
    nchunk = w_in.shape[1] // W_CHUNK
    half = nchunk // 2
    bsz = m // seq_len
    chunks = seq_len // tc
    npairs = m // (2 * tc)
    assert seq_len % (2 * tc) == 0 and tc % (SUBLANES * SUBLANES) == 0 and tc // SUBLANES > CONV_W
    pair = lambda i: jnp.clip(i - nchunk, 0, npairs - 1)
    last = npairs * 2 - 1
    seq_in = lambda i: (jnp.clip(2 * (i - nchunk), 0, last) // chunks, 0, 0)
    seq_out = lambda i: (jnp.clip(2 * (i - nchunk) - 1, 0, last) // chunks, 0, 0)
    state_in = lambda rows: pl.BlockSpec((1, rows, w), seq_in)
    state_out = lambda rows: pl.BlockSpec((1, rows, w), seq_out)
    chunk = lambda i: (0, jnp.minimum(i, nchunk - 1))
    return pl.pallas_call(
        functools.partial(_rglru_front_kernel, nchunk, npairs, chunks),
        grid=(nchunk + npairs + 1,),
        in_specs=[_resident(xs.shape), pl.BlockSpec((2 * tc, d), lambda i: (pair(i), 0)), _resident((1, d)),
                  pl.BlockSpec((d, W_CHUNK), chunk), state_in(CONV_W - 1), state_in(1),
                  _resident(s_conv_prev.shape), _resident(s_h0.shape),
                  _resident((CONV_W, w)), _resident((1, w)), _resident(w_r.shape), _resident((1, w)),
                  _resident(w_i.shape), _resident((1, w)), _resident((1, w))],
        out_specs=[
            pl.BlockSpec((ns, W_CHUNK), lambda i: (0, jnp.clip(i - half, 0, half - 1))),
            pl.BlockSpec((ns, w), lambda i: (0, 0)),
            pl.BlockSpec(s_conv_prev.shape, lambda i: (0, 0, 0)),
            pl.BlockSpec((tc, w), lambda i: (pair(i), 0)),
            pl.BlockSpec((tc, w), lambda i: (jnp.clip(i - nchunk - 1, 0, npairs - 1), 0)),
            state_out(CONV_W - 1), state_out(1),
        ],
        out_shape=[
            jax.ShapeDtypeStruct((ns, w), F32),
            jax.ShapeDtypeStruct((ns, w), F32),
            jax.ShapeDtypeStruct(s_conv_prev.shape, F32),
            jax.ShapeDtypeStruct((m // 2, w), BF16),
            jax.ShapeDtypeStruct((m // 2, w), BF16),
            jax.ShapeDtypeStruct((bsz, CONV_W - 1, w), F32),
            jax.ShapeDtypeStruct((bsz, 1, w), F32),
        ],
        scratch_shapes=[pltpu.VMEM((nchunk, d, W_CHUNK), BF16),
                        pltpu.VMEM(w_r.shape, BF16), pltpu.VMEM(w_i.shape, BF16),
                        pltpu.VMEM((half, ns, W_CHUNK), F32),
                        pltpu.VMEM((tc, 2 * w), F32), pltpu.VMEM((tc, 2 * w), F32),
                        pltpu.VMEM((1, w), F32), pltpu.VMEM((CONV_W - 1, w), F32)],
        compiler_params=_params("arbitrary"),
        name="rglru_front",
    )(xs, x, g.reshape(1, d), w_in, conv_prev, h0.reshape(bsz, 1, w), s_conv_prev, s_h0,
      conv_w, conv_b.reshape(1, w), w_r, b_r.reshape(1, w), w_i, b_i.reshape(1, w), lam.reshape(1, w))


def _buckets(dist):
    n = jnp.maximum(dist, 0)
    max_exact = N_BUCKETS // 2
    nf = jnp.maximum(n, 1).astype(F32)
    large = max_exact + jnp.floor(jnp.log(nf / max_exact) / math.log(MAX_DISTANCE / max_exact)
                                  * (N_BUCKETS - max_exact)).astype(jnp.int32)
    large = jnp.minimum(large, N_BUCKETS - 1)
    return jnp.where(n < max_exact, n, large)


def _lookup(bucket, valid, table_ref, head):
    bias = jnp.zeros(bucket.shape, F32)
    for b in range(N_BUCKETS):
        bias = jnp.where(bucket == b, table_ref[b, head], bias)
    return jnp.where(valid, bias, NEG_INF)


def _bias_kernel(table_ref, sinks_ref, band_ref, sinkt_ref, past_ref, new_ref):
    hk = pl.program_id(0)
    dist = lax.broadcasted_iota(jnp.int32, (1, LANES), 1)
    bucket = _buckets(dist)
    key_row = lax.broadcasted_iota(jnp.int32, (ATT_KEYS, HALF_Q), 0)
    true_dist = lax.broadcasted_iota(jnp.int32, (ATT_KEYS, HALF_Q), 1) + BLOCK - key_row
    visible = (true_dist >= 0) & (true_dist < WINDOW)

    def band(head):
        row = _lookup(bucket, dist < WINDOW, table_ref, head) * LOG2_E
        full = pltpu.roll(jnp.broadcast_to(row, (ATT_KEYS, LANES)), 0, axis=1, stride=1, stride_axis=0)
        return jnp.where(visible, full[:, :HALF_Q], NEG_INF * LOG2_E)

    rows = past_ref.shape[2]
    d_past = rows - lax.broadcasted_iota(jnp.int32, (1, rows), 1)
    b_past = _buckets(d_past)
    ok_past = (d_past >= 0) & (d_past < WINDOW)
    d_new = jnp.zeros((1, LANES), jnp.int32)
    b_new = _buckets(d_new)
    for g in range(GROUP):
        head = hk * GROUP + g
        bias = band(head)
        for half in range(2):
            slot = _head_order(half).index(g)
            cs = slice(slot * HALF_Q, (slot + 1) * HALF_Q)
            band_ref[0, half, 0, :, cs] = bias
            prev_rows = BLOCK - half * HALF_Q
            band_ref[1, half, 0, :, cs] = jnp.where(key_row < prev_rows, NEG_INF, bias)
            sinkt_ref[half, 0, :, cs] = jnp.full((1, HALF_Q), sinks_ref[head] * LOG2_E, F32)
        past_ref[0, g:g + 1, :] = _lookup(b_past, ok_past, table_ref, head)
        new_ref[0, g:g + 1, :] = _lookup(b_new, d_new == 0, table_ref, head)


def bias_tables(table, sinks, past_rows):
    assert WINDOW <= LANES and BLOCK % LANES == 0
    smem = pl.BlockSpec(memory_space=pltpu.SMEM)
    return pl.pallas_call(
        _bias_kernel,
        grid=(N_KV_HEADS,),
        in_specs=[smem, smem],
        out_specs=[
            pl.BlockSpec((2, 2, 1, ATT_KEYS, GROUP * HALF_Q), lambda h: (0, 0, h, 0, 0)),
            pl.BlockSpec((2, 1, 1, GROUP * HALF_Q), lambda h: (0, h, 0, 0)),
            pl.BlockSpec((1, GROUP, past_rows), lambda h: (h, 0, 0)),
            pl.BlockSpec((1, GROUP, LANES), lambda h: (h, 0, 0)),
        ],
        out_shape=[
            jax.ShapeDtypeStruct((2, 2, N_KV_HEADS, ATT_KEYS, GROUP * HALF_Q), F32),
            jax.ShapeDtypeStruct((2, N_KV_HEADS, 1, GROUP * HALF_Q), F32),
            jax.ShapeDtypeStruct((N_KV_HEADS, GROUP, past_rows), F32),
            jax.ShapeDtypeStruct((N_KV_HEADS, GROUP, LANES), F32),
        ],
        compiler_params=_params("parallel"),
        name="bias_tables",
    )(table, sinks)


def _band_attn_rounds(first_tile, q_ref, kp_ref, kc_ref, vp_ref, vc_ref, gate_ref, bias_ref, sink_ref,
                      y_ref, s_scr, p_scr):
    nt = (((1,), (1,)), ((), ()))
    low = (lax.broadcasted_iota(jnp.int32, (1, LANES), 1) < HEAD_DIM)
    keep_low = low.astype(BF16)
    keep_high = 1 - keep_low
    keep = (keep_low, keep_high)
    nkeys = 2 * BLOCK
    ones_rows = jnp.where(lax.broadcasted_iota(jnp.int32, (2 * SUBLANES, nkeys), 0) == 0,
                          1.0, 0.0).astype(BF16)
    rows = ATT_ROWS
    nslot = s_scr.shape[0]
    items = [(blk, hk, half) for blk in range(q_ref.shape[0] // BLOCK)
             for hk in range(N_KV_HEADS) for half in range(2)]
    assert nslot % 2 == 0

    def rows_of(blk):
        return slice(blk * BLOCK, (blk + 1) * BLOCK)

    def key_rows(half):
        return slice(half * HALF_Q, half * HALF_Q + ATT_KEYS)

    for slot in range(nslot):
        dead = slice(ATT_KEYS, nkeys) if slot % 2 == 0 else slice(0, HALF_Q)
        p_scr[slot, dead, :] = jnp.zeros((HALF_Q, p_scr.shape[2]), BF16)

    def scores(idx):
        blk, hk, half = items[idx]
        variant = first_tile if blk == 0 else 0
        cs = slice(hk * LANES, (hk + 1) * LANES)
        k_prev = kp_ref[:, cs] if blk == 0 else kc_ref[rows_of(blk - 1), cs]
        k_cur = kc_ref[rows_of(blk), cs]
        kd = (jnp.concatenate([k_prev, k_cur[:HALF_Q]], axis=0) if half == 0
              else jnp.concatenate([k_prev[HALF_Q:], k_cur], axis=0))
        q0 = blk * BLOCK + half * HALF_Q
        qs = jnp.concatenate(
            [q_ref[q0:q0 + HALF_Q, (hk * SLABS + h // HEADS_PER_TILE) * LANES:
                   (hk * SLABS + h // HEADS_PER_TILE + 1) * LANES] * keep[h % HEADS_PER_TILE]
             for h in _head_order(half)], axis=0)
        s = lax.dot_general(kd, qs, nt, preferred_element_type=F32) + bias_ref[variant, half, hk]
        s_scr[idx % nslot, key_rows(half), :] = s
        return jnp.maximum(jnp.max(s, axis=0, keepdims=True), sink_ref[half, hk])

    def softmax(idx, m):
        blk, hk, half = items[idx]
        slot = idx % nslot
        lo = half * HALF_Q
        for r in range(lo, lo + ATT_KEYS, rows):
            p_scr[slot, r:r + rows, :] = jnp.exp2(s_scr[slot, r:r + rows, :] - m).astype(BF16)
        return jnp.exp2(sink_ref[half, hk] - m)

    def weighted_values(idx):
        blk, hk, half = items[idx]
        vs = slice(hk * LANES, hk * LANES + HEAD_DIM)
        v_prev = vp_ref[vs, :] if blk == 0 else vc_ref[vs, rows_of(blk - 1)]
        vt = jnp.concatenate([v_prev, vc_ref[vs, rows_of(blk)]], axis=1)
        lhs_v = jnp.concatenate([vt, ones_rows], axis=0)
        return jnp.dot(lhs_v, p_scr[idx % nslot], preferred_element_type=F32)

    low_q = lax.broadcasted_iota(jnp.int32, (HEAD_DIM, LANES), 1) < HALF_Q

    def finish(blk, hk, ots, sink_ws):
        o = [ots[half][:HEAD_DIM] * (1.0 / (ots[half][HEAD_DIM:HEAD_DIM + 1] + sink_ws[half]))
             for half in range(2)]
        for sl in range(SLABS):
            a, b = (oh[:, sl * LANES:(sl + 1) * LANES] for oh in o)
            even = jnp.where(low_q, a, b)
            odd = pltpu.roll(jnp.where(low_q, b, a), HALF_Q, axis=1)
            pair = jnp.concatenate([even, odd], axis=0)
            c0 = (hk * SLABS + sl) * LANES
            y_ref[rows_of(blk), c0:c0 + LANES] = (
                pair.T * _silu(gate_ref[rows_of(blk), c0:c0 + LANES])).astype(y_ref.dtype)

    n = len(items)
    offs, sink_ws, outs = {}, {}, {}
    for k in range(-2 * ATT_SKEW, n + ATT_SKEW):
        if 0 <= k + 2 * ATT_SKEW < n:
            offs[k + 2 * ATT_SKEW] = scores(k + 2 * ATT_SKEW)
        if 0 <= k + ATT_SKEW < n:
            sink_ws[k + ATT_SKEW] = softmax(k + ATT_SKEW, offs.pop(k + ATT_SKEW))
        if 0 <= k < n:
            outs[k] = weighted_values(k)
        j = k - ATT_SKEW
        if 0 <= j < n and items[j][2] == 1:
            finish(items[j][0], items[j][1], [outs.pop(j - 1), outs.pop(j)],
                   [sink_ws.pop(j - 1), sink_ws.pop(j)])
        yield


def _proj_tile(y_scr, w_scr, g_ref, x_ref, o_ref, raw_scr):
    y = y_scr[...]
    for c in range(w_scr.shape[0]):
        for n0 in range(0, W_CHUNK, MXU_COLS):
            raw_scr[:, c * W_CHUNK + n0:c * W_CHUNK + n0 + MXU_COLS] = jnp.dot(
                y, w_scr[c, :, n0:n0 + MXU_COLS], preferred_element_type=F32)
            yield
    step = y_scr.shape[0] // PROJ_TAIL_PIECES
    for r0 in range(0, y_scr.shape[0], step):
        o = raw_scr[r0:r0 + step, :]
        o_ref[r0:r0 + step, :] = x_ref[r0:r0 + step, :] + o * _rms_scale(o) * g_ref[...]
        yield


def _attn_proj_kernel(nchunk, ntiles, tiles_per_seq, as_ref, gs_ref, xs_ref,
                      q_ref, kp_ref, kc_ref, vp_ref, vc_ref, gate_ref, bias_ref, sink_ref,
                      w_ref, g_ref, x_ref, os_ref, o_ref,
                      w_scr, raws_scr, s_scr, p_scr, ynew_scr, yold_scr, raw_scr):
    i = pl.program_id(0)
    p = i - nchunk

    @pl.when(i < nchunk)
    def _():
        wb = w_ref[...].astype(BF16)
        w_scr[i] = wb
        ys = (as_ref[...] * _silu(gs_ref[...])).astype(BF16)
        raws_scr[i] = jnp.dot(ys, wb, preferred_element_type=F32)

    @pl.when(i == nchunk - 1)
    def _():
        o = jnp.concatenate([raws_scr[c] for c in range(nchunk)], axis=1)
        os_ref[:, 0, :] = xs_ref[...] + o * _rms_scale(o) * g_ref[...]

    def attend():
        first = (p % tiles_per_seq == 0).astype(jnp.int32)
        return _band_attn_rounds(first, q_ref, kp_ref, kc_ref, vp_ref, vc_ref, gate_ref, bias_ref,
                                 sink_ref, ynew_scr, s_scr, p_scr)

    def project():
        return _proj_tile(yold_scr, w_scr, g_ref, x_ref, o_ref, raw_scr)

    @pl.when(p == 0)
    def _():
        _interleave((attend(), 1))
        yold_scr[...] = ynew_scr[...]

    @pl.when((p > 0) & (p < ntiles))
    def _():
        _interleave((attend(), ATT_ROUNDS_PER_PROJ_PIECE), (project(), 1))
        yold_scr[...] = ynew_scr[...]

    @pl.when(p == ntiles)
    def _():
        _interleave((project(), 1))


def attn_proj(a_s, gate_s, x_s, q, kdup, vt, gate, bias_band, sink_t, w, g, x, seq_len):
    k, d = w.shape
    m = x.shape[0]
    ns = x_s.shape[0]
    tm = ATT_PROJ_TILE
    nchunk = d // W_CHUNK
    ntiles = m // tm
    tiles_per_seq = seq_len // tm
    per_tile = tm // BLOCK
    assert seq_len % tm == 0 and tm % BLOCK == 0
    att_tile = lambda i: jnp.clip(i - nchunk, 0, ntiles - 1)
    proj_tile = lambda i: jnp.clip(i - nchunk - 1, 0, ntiles - 1)

    def before(i):
        t = att_tile(i)
        return per_tile * t - jnp.where(t % tiles_per_seq == 0, 0, 1)

    rows = lambda n: pl.BlockSpec((tm, n), lambda i: (att_tile(i), 0))
    score_tile = (2 * BLOCK, GROUP * HALF_Q)
    return pl.pallas_call(
        functools.partial(_attn_proj_kernel, nchunk, ntiles, tiles_per_seq),
        grid=(nchunk + ntiles + 1,),
        in_specs=[
            _resident(a_s.shape), _resident(gate_s.shape), _resident(x_s.shape),
            rows(ATT_WIDTH), pl.BlockSpec((BLOCK, 2 * KV_WIDTH), lambda i: (before(i), 0)),
            rows(2 * KV_WIDTH), pl.BlockSpec((2 * KV_WIDTH, BLOCK), lambda i: (0, before(i))),
            pl.BlockSpec((2 * KV_WIDTH, tm), lambda i: (0, att_tile(i))), rows(ATT_WIDTH),
            _resident(bias_band.shape), _resident(sink_t.shape),
            pl.BlockSpec((k, W_CHUNK), lambda i: (0, jnp.minimum(i, nchunk - 1))), _resident((1, d)),
            pl.BlockSpec((tm, d), lambda i: (proj_tile(i), 0)),
        ],
        out_specs=[pl.BlockSpec((ns, 1, d), lambda i: (0, 0, 0)),
                   pl.BlockSpec((tm, d), lambda i: (proj_tile(i), 0))],
        out_shape=[jax.ShapeDtypeStruct((ns, 1, d), F32), jax.ShapeDtypeStruct((m, d), F32)],
        scratch_shapes=[pltpu.VMEM((nchunk, k, W_CHUNK), BF16),
                        pltpu.VMEM((nchunk, x_s.shape[0], W_CHUNK), F32),
                        pltpu.VMEM((ATT_SLOTS,) + score_tile, F32),
                        pltpu.VMEM((ATT_SLOTS,) + score_tile, BF16),
                        pltpu.VMEM((tm, k), BF16), pltpu.VMEM((tm, k), BF16),
                        pltpu.VMEM((tm, d), F32)],
        compiler_params=_params("arbitrary"),
        name="attn_proj",
    )(a_s, gate_s, x_s, q, kdup, kdup, vt, vt, gate, bias_band, sink_t, w, g.reshape(1, d), x)


def _cached_attn_kernel(q_ref, ckt_ref, cvt_ref, kn_ref, vn_ref, sinks_ref, bpast_ref, bnew_ref, o_ref):
    shape = (N_Q_HEADS, KV_WIDTH)
    lane_kv = lax.broadcasted_iota(jnp.int32, shape, 1) // HEAD_DIM
    row_kv = lax.broadcasted_iota(jnp.int32, shape, 0) // GROUP
    own = lane_kv == row_kv
    sink = sinks_ref[...]
    nt = (((1,), (1,)), ((), ()))
    seqs = range(q_ref.shape[0])
    qm = []
    for b in seqs:
        q = q_ref[b]
        qt = jnp.concatenate([q] * N_KV_HEADS, axis=1)
        qm.append(jnp.where(own, qt, 0.0).astype(BF16))
    s = [jnp.dot(qm[b], ckt_ref[b].astype(BF16), preferred_element_type=F32) + bpast_ref[...] for b in seqs]
    s_new = [jnp.sum(qm[b].astype(F32) * kn_ref[b:b + 1, :].astype(BF16).astype(F32), axis=-1, keepdims=True)
             + bnew_ref[:, :1] for b in seqs]
    m = [jnp.maximum(jnp.maximum(jnp.max(s[b], axis=-1, keepdims=True), s_new[b]), sink) for b in seqs]
    p = [jnp.exp(s[b] - m[b]) for b in seqs]
    p_new = [jnp.exp(s_new[b] - m[b]) for b in seqs]
    denom = [jnp.sum(p[b], axis=-1, keepdims=True) + p_new[b] + jnp.exp(sink - m[b]) for b in seqs]
    pv = [lax.dot_general(p[b].astype(BF16), cvt_ref[b].astype(BF16), nt, preferred_element_type=F32)
          for b in seqs]
    for b in seqs:
        o_all = pv[b] + p_new[b].astype(BF16).astype(F32) * vn_ref[b:b + 1, :].astype(BF16).astype(F32)
        o_all = jnp.where(own, o_all, 0.0)
        o = o_all[:, :HEAD_DIM]
        for hk in range(1, N_KV_HEADS):
            o = o + o_all[:, hk * HEAD_DIM:(hk + 1) * HEAD_DIM]
        o_ref[b] = o / denom[b]


def cached_attention(q, cache_kt, cache_vt, k_new, v_new, sinks, bias_past, bias_new):
    bsz, _, rows = cache_kt.shape
    nseq = math.gcd(bsz, SEQS_PER_STEP)
    per_seq = lambda r, n: pl.BlockSpec((nseq, r, n), lambda b: (b, 0, 0))
    new_row = pl.BlockSpec((nseq, KV_WIDTH), lambda b: (b, 0))
    return pl.pallas_call(
        _cached_attn_kernel,
        grid=(bsz // nseq,),
        in_specs=[
            per_seq(N_Q_HEADS, HEAD_DIM), per_seq(KV_WIDTH, rows), per_seq(KV_WIDTH, rows),
            new_row, new_row,
            _resident((N_Q_HEADS, 1)), _resident((N_Q_HEADS, rows)), _resident((N_Q_HEADS, LANES)),
        ],
        out_specs=per_seq(N_Q_HEADS, HEAD_DIM),
        out_shape=jax.ShapeDtypeStruct((bsz, N_Q_HEADS, HEAD_DIM), F32),
        compiler_params=_params("parallel"),
        name="cached_attention",
    )(q, cache_kt, cache_vt, k_new, v_new, sinks, bias_past, bias_new)


def kernel(x_prompt, x_sample, state_conv, state_h, cache_k, cache_v, a_norm_pre, a_norm_post,
           a_w_in, a_conv_w, a_conv_b, a_w_r, a_b_r, a_w_i, a_b_i, a_lambda, a_w_out, kv_norm, w_kv,
           b_norm_pre, b_norm_post, b_w_qg, b_sinks, b_w_out, rel_bias_table):
    bsz, t, d = x_prompt.shape
    dbsz, dt, _ = x_sample.shape
    assert a_w_in.shape[0] == 1 and b_w_qg.shape[0] == 1 and dt == 1
    assert t % BLOCK == 0 and t >= WINDOW
    past_rows = cache_k.shape[1]
    assert past_rows == min(WINDOW, PAST_LEN)

    sinks = b_sinks[0]
    bias_band, sink_t, bias_past, bias_new = bias_tables(rel_bias_table, sinks, past_rows)

    tm = 2 * SUB_ROWS
    xp = x_prompt.reshape(bsz * t, d)
    xs = x_sample

    conv0 = jnp.zeros((bsz, CONV_W - 1, LRU_WIDTH), F32)
    h0 = jnp.zeros((bsz, LRU_WIDTH), F32)
    gate_s, hs, s_conv_t, y_even, y_odd, p_conv, p_h = rglru_front(
        xs, jnp.transpose(state_conv[0], (1, 0, 2)), state_h[0], xp, conv0, h0, a_norm_pre[0], a_w_in[0],
        a_conv_w[0], a_conv_b[0], a_w_r[0], a_b_r[0], a_w_i[0], a_b_i[0], a_lambda[0], seq_len=t)
    xs1, x1 = proj_norm_res(hs, gate_s, xs, (y_even, y_odd), a_w_out[0], a_norm_post[0], xp, tm)

    ks, vs, qs, gate_sb, q, gate_b, kdup, vt, k_tail, v_tail = norm_proj_kvq(
        xs1, x1, kv_norm, b_norm_pre[0], w_kv, b_w_qg[0], SUB_ROWS, seq_len=t)
    cache_kt = jnp.transpose(cache_k, (0, 2, 3, 1)).reshape(dbsz, KV_WIDTH, past_rows)
    cache_vt = jnp.transpose(cache_v, (0, 2, 3, 1)).reshape(dbsz, KV_WIDTH, past_rows)
    os_ = cached_attention(qs.reshape(dbsz, N_Q_HEADS, HEAD_DIM), cache_kt, cache_vt,
                           ks, vs,
                           sinks.reshape(N_Q_HEADS, 1), bias_past.reshape(N_Q_HEADS, past_rows),
                           bias_new.reshape(N_Q_HEADS, LANES))
    y_sample, y_prompt = attn_proj(os_.reshape(dbsz, ATT_WIDTH), gate_sb, xs1, q, kdup, vt, gate_b,
                                   bias_band, sink_t, b_w_out[0], b_norm_post[0], x1, seq_len=t)
    y_prompt = y_prompt.reshape(bsz, t, d)
    p_k = jnp.transpose(k_tail.reshape(bsz, N_KV_HEADS, HEAD_DIM, WINDOW), (0, 3, 1, 2))
    p_v = jnp.transpose(v_tail.reshape(bsz, N_KV_HEADS, HEAD_DIM, WINDOW), (0, 3, 1, 2))

    return (y_prompt, y_sample,
            p_conv[None], p_h.reshape(1, bsz, LRU_WIDTH), p_k, p_v,
            jnp.transpose(s_conv_t, (1, 0, 2))[None], hs[None],
            ks.reshape(dbsz, 1, N_KV_HEADS, HEAD_DIM), vs.reshape(dbsz, 1, N_KV_HEADS, HEAD_DIM))
```

```python
import functools
import math

import jax
import jax.numpy as jnp
from jax import lax
from jax.experimental import pallas as pl
from jax.experimental.pallas import tpu as pltpu

F32 = jnp.float32
BF16 = jnp.bfloat16

D_MODEL = 2048
LRU_WIDTH = 2048
LRU_BLOCKS = 8
LRU_BLOCK_W = LRU_WIDTH // LRU_BLOCKS
CONV_W = 4
LRU_C = 8.0
HEAD_DIM = 64
N_Q_HEADS = 32
N_KV_HEADS = 8
GROUP = N_Q_HEADS // N_KV_HEADS
ATT_WIDTH = N_Q_HEADS * HEAD_DIM
KV_WIDTH = N_KV_HEADS * HEAD_DIM
WINDOW = 128
BLOCK = WINDOW
N_BUCKETS = 32
MAX_DISTANCE = 128
RMS_EPS = 1e-6
NEG_INF = -1e30
LOG2_E = 1.4426950408889634
PAST_LEN = 16384

V7X_VMEM_BYTES = 64 * 1024 * 1024
VMEM_LIMIT = V7X_VMEM_BYTES - 8 * 1024 * 1024
SUBLANES = 8
LANES = 128
HEADS_PER_TILE = LANES // HEAD_DIM
SLABS = GROUP // HEADS_PER_TILE
MXU_COLS = 256
SUB_ROWS = 256
W_CHUNK = 512
ATT_ROWS = 64
ATT_SKEW = 2
ATT_SLOTS = 2 * ATT_SKEW + 2
HALF_Q = BLOCK // 2
ATT_KEYS = WINDOW + HALF_Q
SEQS_PER_STEP = 8
SCAN_PIECES_PER_PROJ_PIECE = 2
ATT_PROJ_TILE = 2 * BLOCK
PROJ_TAIL_PIECES = 4
ATT_ROUNDS_PER_PROJ_PIECE = 4
SQRT_FLOOR = 1e-30


def _head_order(half):
    heads = list(range(GROUP))
    return heads if half == 0 else [h ^ 1 for h in heads]


def _params(*semantics):
    return pltpu.CompilerParams(dimension_semantics=semantics, vmem_limit_bytes=VMEM_LIMIT)


def _resident(shape):
    zeros = (0,) * len(shape)
    return pl.BlockSpec(shape, lambda *_: zeros, pipeline_mode=pl.Buffered(1))


def _rms_scale(x):
    return lax.rsqrt(jnp.mean(x * x, axis=-1, keepdims=True) + RMS_EPS)


def _silu(x):
    h = 0.5 * x
    return h * jnp.tanh(h) + h


def _segment_major(rows, inverse=False):
    seg = rows // SUBLANES
    r = lax.broadcasted_iota(jnp.int32, (rows, rows), 0)
    c = lax.broadcasted_iota(jnp.int32, (rows, rows), 1)
    if inverse:
        src = (r % seg) * SUBLANES + r // seg
    else:
        src = (r % SUBLANES) * seg + r // SUBLANES
    return jnp.where(c == src, 1.0, 0.0).astype(BF16)


def _phase_specs(nchunk, tm, k):
    chunk_w = pl.BlockSpec((k, W_CHUNK), lambda i: (0, jnp.minimum(i, nchunk - 1)))
    tile = lambda n: pl.BlockSpec((tm, n), lambda i: (jnp.maximum(i - nchunk, 0), 0))
    return chunk_w, tile


def _dup_heads(x):
    low = lax.broadcasted_iota(jnp.int32, (x.shape[0], LANES), 1) < HEAD_DIM
    out = []
    for c in range(x.shape[1] // LANES):
        col = x[:, c * LANES:(c + 1) * LANES]
        swapped = pltpu.roll(col, HEAD_DIM, axis=1)
        out += [jnp.where(low, col, swapped), jnp.where(low, swapped, col)]
    return jnp.concatenate(out, axis=1)


def _norm_proj_kvq_kernel(nkv, nqg, xs_ref, x_ref, gkv_ref, gq_ref, wkv_ref, wqg_ref,
                          ks_ref, vs_ref, qs_ref, gates_ref,
                          q_ref, gate_ref, kdup_ref, vt_ref, ktail_ref, vtail_ref,
                          wkv_scr, wqg_scr):
    i = pl.program_id(0)
    nchunk = nkv + nqg
    q_chunks = ATT_WIDTH // W_CHUNK
    q_scale = 1.0 / math.sqrt(HEAD_DIM)
    q_scale_log2 = q_scale * LOG2_E

    def sample_rows(g_ref):
        xs = xs_ref[...]
        return (xs * _rms_scale(xs) * g_ref[...]).astype(BF16)

    @pl.when(i < nkv)
    def _():
        wb = wkv_ref[...].astype(BF16)
        wkv_scr[i] = wb
        r = jnp.dot(sample_rows(gkv_ref), wb, preferred_element_type=F32)

        @pl.when(i == 0)
        def _():
            ks_ref[...] = r

        @pl.when(i == 1)
        def _():
            vs_ref[...] = r

    @pl.when((i >= nkv) & (i < nchunk))
    def _():
        c = i - nkv
        wb = wqg_ref[...].astype(BF16)
        wqg_scr[c] = wb
        r = jnp.dot(sample_rows(gq_ref), wb, preferred_element_type=F32)

        @pl.when(c < q_chunks)
        def _():
            qs_ref[...] = r * q_scale

        @pl.when(c >= q_chunks)
        def _():
            gates_ref[...] = r

    @pl.when(i >= nchunk)
    def _():
        tm = x_ref.shape[0]
        for rs in _row_blocks(tm):
            x = x_ref[rs, :]
            xh = x * _rms_scale(x)
            xkv = (xh * gkv_ref[...]).astype(BF16)
            xq = (xh * gq_ref[...]).astype(BF16)
            k = jnp.dot(xkv, wkv_scr[0], preferred_element_type=F32)
            v = jnp.dot(xkv, wkv_scr[1], preferred_element_type=F32)
            kdup_ref[rs, :] = _dup_heads(k).astype(BF16)
            vt_ref[:, rs] = _dup_heads(v).T.astype(BF16)
            for c in range(nqg):
                r = jnp.dot(xq, wqg_scr[c], preferred_element_type=F32)
                if c < q_chunks:
                    q_ref[rs, c * W_CHUNK:(c + 1) * W_CHUNK] = (r * q_scale_log2).astype(q_ref.dtype)
                else:
                    cc = c - q_chunks
                    gate_ref[rs, cc * W_CHUNK:(cc + 1) * W_CHUNK] = r
        ktail_ref[0] = k[k.shape[0] - WINDOW:].T
        vtail_ref[0] = v[v.shape[0] - WINDOW:].T


def norm_proj_kvq(xs, x, g_kv, g_q, w_kv, w_qg, tm, seq_len):
    m, d = x.shape
    ns = xs.shape[0]
    assert w_kv.shape[1] == 2 * KV_WIDTH == 2 * W_CHUNK and seq_len % tm == 0 and tm >= WINDOW
    nkv, nqg = w_kv.shape[1] // W_CHUNK, w_qg.shape[1] // W_CHUNK
    nchunk = nkv + nqg
    tiles = seq_len // tm
    tile = lambda n: pl.BlockSpec((tm, n), lambda i: (jnp.maximum(i - nchunk, 0), 0))
    tail = pl.BlockSpec((1, KV_WIDTH, WINDOW), lambda i: (jnp.maximum(i - nchunk, 0) // tiles, 0, 0))
    kv_chunk = lambda i: (0, jnp.minimum(i, nkv - 1))
    qg_chunk = lambda i: (0, jnp.clip(i - nkv, 0, nqg - 1))
    q_chunks = ATT_WIDTH // W_CHUNK
    whole_s = lambda n: pl.BlockSpec((ns, n), lambda i: (0, 0))
    return pl.pallas_call(
        functools.partial(_norm_proj_kvq_kernel, nkv, nqg),
        grid=(nchunk + m // tm,),
        in_specs=[
            _resident(xs.shape), tile(d), _resident((1, d)), _resident((1, d)),
            pl.BlockSpec((d, W_CHUNK), kv_chunk), pl.BlockSpec((d, W_CHUNK), qg_chunk),
        ],
        out_specs=[
            whole_s(KV_WIDTH), whole_s(KV_WIDTH),
            pl.BlockSpec((ns, W_CHUNK), lambda i: (0, jnp.clip(i - nkv, 0, q_chunks - 1))),
            pl.BlockSpec((ns, W_CHUNK), lambda i: (0, jnp.clip(i - nkv - q_chunks, 0, nqg - q_chunks - 1))),
            tile(ATT_WIDTH), tile(ATT_WIDTH), tile(2 * KV_WIDTH),
            pl.BlockSpec((2 * KV_WIDTH, tm), lambda i: (0, jnp.maximum(i - nchunk, 0))), tail, tail,
        ],
        out_shape=[
            jax.ShapeDtypeStruct((ns, KV_WIDTH), F32),
            jax.ShapeDtypeStruct((ns, KV_WIDTH), F32),
            jax.ShapeDtypeStruct((ns, ATT_WIDTH), F32),
            jax.ShapeDtypeStruct((ns, w_qg.shape[1] - ATT_WIDTH), F32),
            jax.ShapeDtypeStruct((m, ATT_WIDTH), BF16),
            jax.ShapeDtypeStruct((m, ATT_WIDTH), F32),
            jax.ShapeDtypeStruct((m, 2 * KV_WIDTH), BF16),
            jax.ShapeDtypeStruct((2 * KV_WIDTH, m), BF16),
            jax.ShapeDtypeStruct((m // seq_len, KV_WIDTH, WINDOW), F32),
            jax.ShapeDtypeStruct((m // seq_len, KV_WIDTH, WINDOW), F32),
        ],
        scratch_shapes=[pltpu.VMEM((nkv, d, W_CHUNK), BF16), pltpu.VMEM((nqg, d, W_CHUNK), BF16)],
        compiler_params=_params("arbitrary"),
        name="norm_proj_kvq",
    )(xs, x, g_kv.reshape(1, d), g_q.reshape(1, d), w_kv, w_qg)


def _row_blocks(rows):
    sub = min(rows, SUB_ROWS)
    return [slice(r, r + sub) for r in range(0, rows, sub)]


def _proj_norm_res_kernel(nchunk, nparts, as_ref, gs_ref, xs_ref, *refs):
    y_refs = refs[:nparts]
    w_ref, g_ref, x_ref, os_ref, o_ref, w_scr, raw_scr = refs[nparts:]
    i = pl.program_id(0)

    @pl.when(i < nchunk)
    def _():
        wb = w_ref[...].astype(BF16)
        w_scr[i] = wb
        ys = (as_ref[...] * _silu(gs_ref[...])).astype(BF16)
        raw_scr[i] = jnp.dot(ys, wb, preferred_element_type=F32)

    @pl.when(i == nchunk - 1)
    def _():
        o = jnp.concatenate([raw_scr[c] for c in range(nchunk)], axis=1)
        os_ref[...] = xs_ref[:, 0, :] + o * _rms_scale(o) * g_ref[...]

    @pl.when(i >= nchunk)
    def _():
        tm = x_ref.shape[0]
        part_rows = tm // nparts

        for rs in _row_blocks(tm):
            part, off = divmod(rs.start, part_rows)
            y = y_refs[part][off:off + rs.stop - rs.start, :]
            o = jnp.concatenate([jnp.dot(y, w_scr[c], preferred_element_type=F32)
                                 for c in range(nchunk)], axis=1)
            o_ref[rs, :] = x_ref[rs, :] + o * _rms_scale(o) * g_ref[...]


def proj_norm_res(a_s, gate_s, x_s, y_parts, w, g, x, tm):
    k, d = w.shape
    m = x.shape[0]
    ns = x_s.shape[0]
    nchunk = d // W_CHUNK
    nparts = len(y_parts)
    assert (tm // nparts) % min(tm, SUB_ROWS) == 0
    chunk_w, tile = _phase_specs(nchunk, tm, k)
    part = pl.BlockSpec((tm // nparts, k), lambda i: (jnp.maximum(i - nchunk, 0), 0))
    return pl.pallas_call(
        functools.partial(_proj_norm_res_kernel, nchunk, nparts),
        grid=(nchunk + m // tm,),
        in_specs=[_resident(a_s.shape), _resident(gate_s.shape), _resident(x_s.shape)]
        + [part] * nparts + [chunk_w, _resident((1, d)), tile(d)],
        out_specs=[pl.BlockSpec((ns, d), lambda i: (0, 0)), tile(d)],
        out_shape=[jax.ShapeDtypeStruct((ns, d), F32), jax.ShapeDtypeStruct((m, d), F32)],
        scratch_shapes=[pltpu.VMEM((nchunk, k, W_CHUNK), BF16),
                        pltpu.VMEM((nchunk, ns, W_CHUNK), F32)],
        compiler_params=_params("arbitrary"),
        name="proj_norm_res",
    )(a_s, gate_s, x_s, *y_parts, w, g.reshape(1, d), x)


def _lru_gate_dots(conv, wr_half, wi_half):
    cb = conv.astype(BF16)
    return (jnp.dot(cb, wr_half, preferred_element_type=F32),
            jnp.dot(cb, wi_half, preferred_element_type=F32))


def _lru_gates(conv, wr_half, br, wi_half, bi, lam):
    return _lru_gate_math(conv, _lru_gate_dots(conv, wr_half, wi_half), br, bi, lam)


def _lru_gate_math(conv, half_pre, br, bi, lam):
    th_r = jnp.tanh(half_pre[0] + 0.5 * br)
    th_i = jnp.tanh(half_pre[1] + 0.5 * bi)
    nl = -lam
    softplus = jnp.maximum(nl, 0.0) + jnp.log1p(jnp.exp(-jnp.abs(nl)))
    half = (0.5 * LRU_C) * softplus
    x = th_r * half + half
    a = jnp.exp2(x * -LOG2_E)
    z = jnp.tanh(x) * (a * a + 1.0)
    mult = z * lax.rsqrt(jnp.maximum(z, SQRT_FLOOR))
    hc = 0.5 * conv
    return a, mult * (hc * th_i + hc)


def _interleave(*stages):
    live = [[stage, share] for stage, share in stages]
    while live:
        for entry in list(live):
            try:
                for _ in range(entry[1]):
                    next(entry[0])
            except StopIteration:
                live.remove(entry)


def _in_proj_tile(x_ref, rs, g_ref, w_scr, ug_ref):
    x = x_ref[rs, :]
    xn = (x * _rms_scale(x) * g_ref[...]).astype(BF16)
    xn = jnp.dot(_segment_major(xn.shape[0]), xn, preferred_element_type=F32).astype(BF16)
    for c in range(w_scr.shape[0]):
        for n0 in range(0, W_CHUNK, MXU_COLS):
            ug_ref[:, c * W_CHUNK + n0:c * W_CHUNK + n0 + MXU_COLS] = jnp.dot(
                xn, w_scr[c, :, n0:n0 + MXU_COLS], preferred_element_type=F32)
            yield


def _rglru_tile(ug_ref, y_ref, cw_ref, cb_ref, wr_ref, br_ref, wi_ref, bi_ref, lam_ref, h_scr, tail_scr):
    tc = ug_ref.shape[0]
    seg = tc // SUBLANES
    ntaps = CONV_W - 1
    bw = LRU_BLOCK_W
    sub = lax.broadcasted_iota(jnp.int32, (SUBLANES, bw), 0)
    first = sub == 0
    time_order = _segment_major(tc, inverse=True)

    def shift_in(x, row0):
        return jnp.where(first, row0, pltpu.roll(x, 1, axis=0))

    def group(x, j):
        return x[j * SUBLANES:(j + 1) * SUBLANES]

    def store_time_order(cols, y):
        y_ref[:, cols] = jnp.dot(time_order, y, preferred_element_type=F32).astype(y_ref.dtype)

    pending = None
    for n in range(LRU_BLOCKS):
        cs = slice(n * bw, (n + 1) * bw)
        u = ug_ref[:, cs]
        tail = tail_scr[:, cs]
        before = [shift_in(group(u, seg - m), tail[ntaps - m:ntaps - m + 1])
                  for m in range(ntaps, 0, -1)]
        ext = jnp.concatenate(before + [u], axis=0)
        tail_scr[:, cs] = jnp.concatenate(
            [group(u, seg - m)[SUBLANES - 1:] for m in range(ntaps, 0, -1)], axis=0)
        cw = cw_ref[:, cs]
        conv = cb_ref[:, cs]
        for tap in range(CONV_W):
            conv = conv + ext[tap * SUBLANES:tap * SUBLANES + tc] * cw[tap:tap + 1]
        yield

        half_pre = _lru_gate_dots(conv, wr_ref[n], wi_ref[n])
        yield

        if pending is not None:
            store_time_order(*pending)
        yield

        a, b = _lru_gate_math(conv, half_pre, br_ref[:, cs], bi_ref[:, cs], lam_ref[:, cs])

        h = b[:SUBLANES]
        acc = a[:SUBLANES]
        h_loc, a_cum = [h], [acc]
        for j in range(1, seg):
            sl = slice(j * SUBLANES, (j + 1) * SUBLANES)
            h = a[sl] * h + b[sl]
            acc = a[sl] * acc
            h_loc.append(h)
            a_cum.append(acc)

        step = 1
        while step < SUBLANES:
            keep = sub >= step
            h = jnp.where(keep, acc * pltpu.roll(h, step, axis=0) + h, h)
            acc = jnp.where(keep, acc * pltpu.roll(acc, step, axis=0), acc)
            step *= 2
        h_prev = h_scr[:, cs]
        after = h + acc * h_prev
        h_in = shift_in(after, h_prev)
        h_scr[:, cs] = after[SUBLANES - 1:]

        hs = jnp.concatenate([h_loc[j] + a_cum[j] * h_in for j in range(seg)], axis=0)
        y = (hs * _silu(ug_ref[:, LRU_WIDTH + n * bw:LRU_WIDTH + (n + 1) * bw])).astype(BF16)
        pending = (cs, y)
        yield

    store_time_order(*pending)
    yield


def _rglru_front_kernel(nchunk, npairs, chunks, xs_ref, x_ref, g_ref, w_ref,
                        scprev_ref, sh0_ref, cw_ref, cb_ref, wr_ref, br_ref, wi_ref, bi_ref, lam_ref,
                        gs_ref, hs_ref, scnew_ref, y_even_ref, y_odd_ref, cnew_ref, hlast_ref,
                        w_scr, wr_scr, wi_scr, us_scr, ug0_scr, ug1_scr, h_scr, tail_scr):
    i = pl.program_id(0)
    p = i - nchunk
    tc = SUB_ROWS
    half = nchunk // 2
    lru = (cw_ref, cb_ref, wr_scr, br_ref, wi_scr, bi_ref, lam_ref, h_scr, tail_scr)

    @pl.when(i < nchunk)
    def _():
        wb = w_ref[...].astype(BF16)
        w_scr[i] = wb
        xs = xs_ref[:, 0, :]
        xsn = (xs * _rms_scale(xs) * g_ref[...]).astype(BF16)
        r = jnp.dot(xsn, wb, preferred_element_type=F32)
        gs_ref[...] = r

        @pl.when(i < half)
        def _():
            us_scr[i] = r

    @pl.when(i == nchunk - 1)
    def _():
        wr_scr[...] = (0.5 * wr_ref[...]).astype(BF16)
        wi_scr[...] = (0.5 * wi_ref[...]).astype(BF16)
        bw = LRU_BLOCK_W
        for n in range(LRU_BLOCKS):
            cs = slice(n * bw, (n + 1) * bw)
            c, off = divmod(n * bw, W_CHUNK)
            u = us_scr[c, :, off:off + bw]
            cw = cw_ref[:, cs]
            conv = cb_ref[:, cs]
            for tap in range(CONV_W - 1):
                conv = conv + scprev_ref[tap, :, cs] * cw[tap:tap + 1]
                if tap > 0:
                    scnew_ref[tap - 1, :, cs] = scprev_ref[tap, :, cs]
            conv = conv + u * cw[CONV_W - 1:]
            scnew_ref[CONV_W - 2, :, cs] = u
            a, b = _lru_gates(conv, wr_scr[n], br_ref[:, cs], wi_scr[n], bi_ref[:, cs], lam_ref[:, cs])
            hs_ref[:, cs] = a * sh0_ref[:, cs] + b

    def project_even():
        return _in_proj_tile(x_ref, slice(0, tc), g_ref, w_scr, ug0_scr)

    def scan_odd():
        return _rglru_tile(ug1_scr, y_odd_ref, *lru)

    @pl.when(p == 0)
    def _():
        _interleave((project_even(), 1))

    @pl.when((p > 0) & (p < npairs))
    def _():
        _interleave((project_even(), 1), (scan_odd(), SCAN_PIECES_PER_PROJ_PIECE))

    @pl.when(p == npairs)
    def _():
        _interleave((scan_odd(), 1))

    @pl.when(p > 0)
    def _():
        b = (2 * p - 1) // chunks
        hlast_ref[pl.ds(b, 1), :] = h_scr[...]
        for tap in range(CONV_W - 1):
            cnew_ref[tap, pl.ds(b, 1), :] = tail_scr[tap:tap + 1, :]

    @pl.when((p >= 0) & (p < npairs))
    def _():
        @pl.when((2 * p) % chunks == 0)
        def _():
            h_scr[...] = jnp.zeros(h_scr.shape, F32)
            tail_scr[...] = jnp.zeros(tail_scr.shape, F32)

        _interleave((_in_proj_tile(x_ref, slice(tc, 2 * tc), g_ref, w_scr, ug1_scr), 1),
                    (_rglru_tile(ug0_scr, y_even_ref, *lru), SCAN_PIECES_PER_PROJ_PIECE))


def rglru_front(xs, s_conv_prev, s_h0, x, g, w_in, conv_w, conv_b, w_r, b_r, w_i, b_i, lam,
                seq_len):
    m, d = x.shape
    ns = xs.shape[0]
    w = w_in.shape[1] // 2
    tc = SUB_ROWS
    nchunk = w_in.shape[1] // W_CHUNK
    half = nchunk // 2
    bsz = m // seq_len
    chunks = seq_len // tc
    npairs = m // (2 * tc)
    assert seq_len % (2 * tc) == 0 and tc % (SUBLANES * SUBLANES) == 0 and tc // SUBLANES > CONV_W
    pair = lambda i: jnp.clip(i - nchunk, 0, npairs - 1)
    chunk =lambda i: (0, jnp.minimum(i, nchunk - 1))
    return pl.pallas_call(
        functools.partial(_rglru_front_kernel, nchunk, npairs, chunks),
        grid=(nchunk + npairs + 1,),
        in_specs=[_resident(xs.shape), pl.BlockSpec((2 * tc, d), lambda i: (pair(i), 0)), _resident((1, d)),
                  pl.BlockSpec((d, W_CHUNK), chunk),
                  _resident(s_conv_prev.shape), _resident(s_h0.shape),
                  _resident((CONV_W, w)), _resident((1, w)), _resident(w_r.shape), _resident((1, w)),
                  _resident(w_i.shape), _resident((1, w)), _resident((1, w))],
        out_specs=[
            pl.BlockSpec((ns, W_CHUNK), lambda i: (0, jnp.clip(i - half, 0, half - 1))),
            pl.BlockSpec((ns, w), lambda i: (0, 0)),
            pl.BlockSpec(s_conv_prev.shape, lambda i: (0, 0, 0)),
            pl.BlockSpec((tc, w), lambda i: (pair(i), 0)),
            pl.BlockSpec((tc, w), lambda i: (jnp.clip(i - nchunk - 1, 0, npairs - 1), 0)),
            pl.BlockSpec((CONV_W - 1, bsz, w), lambda i: (0, 0, 0)), pl.BlockSpec((bsz, w), lambda i: (0, 0)),
        ],
        out_shape=[
            jax.ShapeDtypeStruct((ns, w), F32),
            jax.ShapeDtypeStruct((ns, w), F32),
            jax.ShapeDtypeStruct(s_conv_prev.shape, F32),
            jax.ShapeDtypeStruct((m // 2, w), BF16),
            jax.ShapeDtypeStruct((m // 2, w), BF16),
            jax.ShapeDtypeStruct((CONV_W - 1, bsz, w), F32),
            jax.ShapeDtypeStruct((bsz, w), F32),
        ],
        scratch_shapes=[pltpu.VMEM((nchunk, d, W_CHUNK), BF16),
                        pltpu.VMEM(w_r.shape, BF16), pltpu.VMEM(w_i.shape, BF16),
                        pltpu.VMEM((half, ns, W_CHUNK), F32),
                        pltpu.VMEM((tc, 2 * w), F32), pltpu.VMEM((tc, 2 * w), F32),
                        pltpu.VMEM((1, w), F32), pltpu.VMEM((CONV_W - 1, w), F32)],
        compiler_params=_params("arbitrary"),
        name="rglru_front",
    )(xs, x, g.reshape(1, d), w_in, s_conv_prev, s_h0,
      conv_w, conv_b.reshape(1, w), w_r, b_r.reshape(1, w), w_i, b_i.reshape(1, w), lam.reshape(1, w))


def _buckets(dist):
    n = jnp.maximum(dist, 0)
    max_exact = N_BUCKETS // 2
    nf = jnp.maximum(n, 1).astype(F32)
    large = max_exact + jnp.floor(jnp.log(nf / max_exact) / math.log(MAX_DISTANCE / max_exact)
                                  * (N_BUCKETS - max_exact)).astype(jnp.int32)
    large = jnp.minimum(large, N_BUCKETS - 1)
    return jnp.where(n < max_exact, n, large)


def _lookup(bucket, valid, table_ref, head):
    bias = jnp.zeros(bucket.shape, F32)
    for b in range(N_BUCKETS):
        bias = jnp.where(bucket == b, table_ref[b, head], bias)
    return jnp.where(valid, bias, NEG_INF)


def _bias_kernel(table_ref, sinks_ref, band_ref, sinkt_ref, past_ref, new_ref, sinkcol_ref):
    hk = pl.program_id(0)
    dist = lax.broadcasted_iota(jnp.int32, (1, LANES), 1)
    bucket = _buckets(dist)
    key_row = lax.broadcasted_iota(jnp.int32, (ATT_KEYS, HALF_Q), 0)
    true_dist = lax.broadcasted_iota(jnp.int32, (ATT_KEYS, HALF_Q), 1) + BLOCK - key_row
    visible = (true_dist >= 0) & (true_dist < WINDOW)

    def band(head):
        row = _lookup(bucket, dist < WINDOW, table_ref, head) * LOG2_E
        full = pltpu.roll(jnp.broadcast_to(row, (ATT_KEYS, LANES)), 0, axis=1, stride=1, stride_axis=0)
        return jnp.where(visible, full[:, :HALF_Q], NEG_INF * LOG2_E)

    rows = past_ref.shape[2]
    d_past = rows - lax.broadcasted_iota(jnp.int32, (1, rows), 1)
    b_past = _buckets(d_past)
    ok_past = (d_past >= 0) & (d_past < WINDOW)
    d_new = jnp.zeros((1, LANES), jnp.int32)
    b_new = _buckets(d_new)
    for g in range(GROUP):
        head = hk * GROUP + g
        bias = band(head)
        for half in range(2):
            slot = _head_order(half).index(g)
            cs = slice(slot * HALF_Q, (slot + 1) * HALF_Q)
            band_ref[0, half, 0, :, cs] = bias
            prev_rows = BLOCK - half * HALF_Q
            band_ref[1, half, 0, :, cs] = jnp.where(key_row < prev_rows, NEG_INF, bias)
            sinkt_ref[half, 0, :, cs] = jnp.full((1, HALF_Q), sinks_ref[head] * LOG2_E, F32)
        past_ref[0, g:g + 1, :] = _lookup(b_past, ok_past, table_ref, head)
        new_ref[0, g:g + 1, :] = _lookup(b_new, d_new == 0, table_ref, head)
        sinkcol_ref[0, g:g + 1, :] = jnp.full((1, LANES), sinks_ref[head], F32)


def bias_tables(table, sinks, past_rows):
    assert WINDOW <= LANES and BLOCK % LANES == 0
    smem = pl.BlockSpec(memory_space=pltpu.SMEM)
    return pl.pallas_call(
        _bias_kernel,
        grid=(N_KV_HEADS,),
        in_specs=[smem, smem],
        out_specs=[
            pl.BlockSpec((2, 2, 1, ATT_KEYS, GROUP * HALF_Q), lambda h: (0, 0, h, 0, 0)),
            pl.BlockSpec((2, 1, 1, GROUP * HALF_Q), lambda h: (0, h, 0, 0)),
            pl.BlockSpec((1, GROUP, past_rows), lambda h: (h, 0, 0)),
            pl.BlockSpec((1, GROUP, LANES), lambda h: (h, 0, 0)),
            pl.BlockSpec((1, GROUP, LANES), lambda h: (h, 0, 0)),
        ],
        out_shape=[
            jax.ShapeDtypeStruct((2, 2, N_KV_HEADS, ATT_KEYS, GROUP * HALF_Q), F32),
            jax.ShapeDtypeStruct((2, N_KV_HEADS, 1, GROUP * HALF_Q), F32),
            jax.ShapeDtypeStruct((N_KV_HEADS, GROUP, past_rows), F32),
            jax.ShapeDtypeStruct((N_KV_HEADS, GROUP, LANES), F32),
            jax.ShapeDtypeStruct((N_KV_HEADS, GROUP, LANES), F32),
        ],
        compiler_params=_params("parallel"),
        name="bias_tables",
    )(table, sinks)


def _band_attn_rounds(first_tile, q_ref, kp_ref, kc_ref, vp_ref, vc_ref, gate_ref, bias_ref, sink_ref,
                      y_ref, s_scr, p_scr):
    nt = (((1,), (1,)), ((), ()))
    low = (lax.broadcasted_iota(jnp.int32, (1, LANES), 1) < HEAD_DIM)
    keep_low = low.astype(BF16)
    keep_high = 1 - keep_low
    keep = (keep_low, keep_high)
    nkeys = 2 * BLOCK
    ones_rows = jnp.where(lax.broadcasted_iota(jnp.int32, (2 * SUBLANES, nkeys), 0) == 0,
                          1.0, 0.0).astype(BF16)
    rows = ATT_ROWS
    nslot = s_scr.shape[0]
    items = [(blk, hk, half) for blk in range(q_ref.shape[0] // BLOCK)
             for hk in range(N_KV_HEADS) for half in range(2)]
    assert nslot % 2 == 0

    def rows_of(blk):
        return slice(blk * BLOCK, (blk + 1) * BLOCK)

    def key_rows(half):
        return slice(half * HALF_Q, half * HALF_Q + ATT_KEYS)

    for slot in range(nslot):
        dead = slice(ATT_KEYS, nkeys) if slot % 2 == 0 else slice(0, HALF_Q)
        p_scr[slot, dead, :] = jnp.zeros((HALF_Q, p_scr.shape[2]), BF16)

    def scores(idx):
        blk, hk, half = items[idx]
        variant = first_tile if blk == 0 else 0
        cs = slice(hk * LANES, (hk + 1) * LANES)
        k_prev = kp_ref[:, cs] if blk == 0 else kc_ref[rows_of(blk - 1), cs]
        k_cur = kc_ref[rows_of(blk), cs]
        kd = (jnp.concatenate([k_prev, k_cur[:HALF_Q]], axis=0) if half == 0
              else jnp.concatenate([k_prev[HALF_Q:], k_cur], axis=0))
        q0 = blk * BLOCK + half * HALF_Q
        qs = jnp.concatenate(
            [q_ref[q0:q0 + HALF_Q, (hk * SLABS + h // HEADS_PER_TILE) * LANES:
                   (hk * SLABS + h // HEADS_PER_TILE + 1) * LANES] * keep[h % HEADS_PER_TILE]
             for h in _head_order(half)], axis=0)
        s = lax.dot_general(kd, qs, nt, preferred_element_type=F32) + bias_ref[variant, half, hk]
        s_scr[idx % nslot, key_rows(half), :] = s
        return jnp.maximum(jnp.max(s, axis=0, keepdims=True), sink_ref[half, hk])

    def softmax(idx, m):
        blk, hk, half = items[idx]
        slot = idx % nslot
        lo = half * HALF_Q
        for r in range(lo, lo + ATT_KEYS, rows):
            p_scr[slot, r:r + rows, :] = jnp.exp2(s_scr[slot, r:r + rows, :] - m).astype(BF16)
        return jnp.exp2(sink_ref[half, hk] - m)

    def weighted_values(idx):
        blk, hk, half = items[idx]
        vs = slice(hk * LANES, hk * LANES + HEAD_DIM)
        v_prev = vp_ref[vs, :] if blk == 0 else vc_ref[vs, rows_of(blk - 1)]
        vt = jnp.concatenate([v_prev, vc_ref[vs, rows_of(blk)]], axis=1)
        lhs_v = jnp.concatenate([vt, ones_rows], axis=0)
        return jnp.dot(lhs_v, p_scr[idx % nslot], preferred_element_type=F32)

    low_q = lax.broadcasted_iota(jnp.int32, (HEAD_DIM, LANES), 1) < HALF_Q

    def finish(blk, hk, ots, sink_ws):
        o = [ots[half][:HEAD_DIM] * (1.0 / (ots[half][HEAD_DIM:HEAD_DIM + 1] + sink_ws[half]))
             for half in range(2)]
        for sl in range(SLABS):
            a, b = (oh[:, sl * LANES:(sl + 1) * LANES] for oh in o)
            even = jnp.where(low_q, a, b)
            odd = pltpu.roll(jnp.where(low_q, b, a), HALF_Q, axis=1)
            pair = jnp.concatenate([even, odd], axis=0)
            c0 = (hk * SLABS + sl) * LANES
            y_ref[rows_of(blk), c0:c0 + LANES] = (
                pair.T * _silu(gate_ref[rows_of(blk), c0:c0 + LANES])).astype(y_ref.dtype)

    n = len(items)
    offs, sink_ws, outs = {}, {}, {}
    for k in range(-2 * ATT_SKEW, n + ATT_SKEW):
        if 0 <= k + 2 * ATT_SKEW < n:
            offs[k + 2 * ATT_SKEW] = scores(k + 2 * ATT_SKEW)
        if 0 <= k + ATT_SKEW < n:
            sink_ws[k + ATT_SKEW] = softmax(k + ATT_SKEW, offs.pop(k + ATT_SKEW))
        if 0 <= k < n:
            outs[k] = weighted_values(k)
        j = k - ATT_SKEW
        if 0 <= j < n and items[j][2] == 1:
            finish(items[j][0], items[j][1], [outs.pop(j - 1), outs.pop(j)],
                   [sink_ws.pop(j - 1), sink_ws.pop(j)])
        yield


def _proj_tile(y_scr, w_scr, g_ref, x_ref, o_ref, raw_scr):
    y = y_scr[...]
    for c in range(w_scr.shape[0]):
        for n0 in range(0, W_CHUNK, MXU_COLS):
            raw_scr[:, c * W_CHUNK + n0:c * W_CHUNK + n0 + MXU_COLS] = jnp.dot(
                y, w_scr[c, :, n0:n0 + MXU_COLS], preferred_element_type=F32)
            yield
    step = y_scr.shape[0] // PROJ_TAIL_PIECES
    for r0 in range(0, y_scr.shape[0], step):
        o = raw_scr[r0:r0 + step, :]
        o_ref[r0:r0 + step, :] = x_ref[r0:r0 + step, :] + o * _rms_scale(o) * g_ref[...]
        yield


def _attn_proj_kernel(nchunk, ntiles, tiles_per_seq, as_ref, gs_ref, xs_ref,
                      q_ref, kp_ref, kc_ref, vp_ref, vc_ref, gate_ref, bias_ref, sink_ref,
                      w_ref, g_ref, x_ref, os_ref, o_ref,
                      w_scr, raws_scr, s_scr, p_scr, ynew_scr, yold_scr, raw_scr):
    i = pl.program_id(0)
    p = i - nchunk

    @pl.when(i < nchunk)
    def _():
        wb = w_ref[...].astype(BF16)
        w_scr[i] = wb
        ys = (as_ref[...] * _silu(gs_ref[...])).astype(BF16)
        raws_scr[i] = jnp.dot(ys, wb, preferred_element_type=F32)

    @pl.when(i == nchunk - 1)
    def _():
        o = jnp.concatenate([raws_scr[c] for c in range(nchunk)], axis=1)
        os_ref[:, 0, :] = xs_ref[...] + o * _rms_scale(o) * g_ref[...]

    def attend():
        first = (p % tiles_per_seq == 0).astype(jnp.int32)
        return _band_attn_rounds(first, q_ref, kp_ref, kc_ref, vp_ref, vc_ref, gate_ref, bias_ref,
                                 sink_ref, ynew_scr, s_scr, p_scr)

    def project():
        return _proj_tile(yold_scr, w_scr, g_ref, x_ref, o_ref, raw_scr)

    @pl.when(p == 0)
    def _():
        _interleave((attend(), 1))
        yold_scr[...] = ynew_scr[...]

    @pl.when((p > 0) & (p < ntiles))
    def _():
        _interleave((attend(), ATT_ROUNDS_PER_PROJ_PIECE), (project(), 1))
        yold_scr[...] = ynew_scr[...]

    @pl.when(p == ntiles)
    def _():
        _interleave((project(), 1))


def attn_proj(a_s, gate_s, x_s, q, kdup, vt, gate, bias_band, sink_t, w, g, x, seq_len):
    k, d = w.shape
    m = x.shape[0]
    ns = x_s.shape[0]
    tm = ATT_PROJ_TILE
    nchunk = d // W_CHUNK
    ntiles = m // tm
    tiles_per_seq = seq_len // tm
    per_tile = tm // BLOCK
    assert seq_len % tm == 0 and tm % BLOCK == 0
    att_tile = lambda i: jnp.clip(i - nchunk, 0, ntiles - 1)
    proj_tile = lambda i: jnp.clip(i - nchunk - 1, 0, ntiles - 1)

    def before(i):
        t = att_tile(i)
        return per_tile * t - jnp.where(t % tiles_per_seq == 0, 0, 1)

    rows = lambda n: pl.BlockSpec((tm, n), lambda i: (att_tile(i), 0))
    score_tile = (2 * BLOCK, GROUP * HALF_Q)
    return pl.pallas_call(
        functools.partial(_attn_proj_kernel, nchunk, ntiles, tiles_per_seq),
        grid=(nchunk + ntiles + 1,),
        in_specs=[
            _resident(a_s.shape), _resident(gate_s.shape), _resident(x_s.shape),
            rows(ATT_WIDTH), pl.BlockSpec((BLOCK, 2 * KV_WIDTH), lambda i: (before(i), 0)),
            rows(2 * KV_WIDTH), pl.BlockSpec((2 * KV_WIDTH, BLOCK), lambda i: (0, before(i))),
            pl.BlockSpec((2 * KV_WIDTH, tm), lambda i: (0, att_tile(i))), rows(ATT_WIDTH),
            _resident(bias_band.shape), _resident(sink_t.shape),
            pl.BlockSpec((k, W_CHUNK), lambda i: (0, jnp.minimum(i, nchunk - 1))), _resident((1, d)),
            pl.BlockSpec((tm, d), lambda i: (proj_tile(i), 0)),
        ],
        out_specs=[pl.BlockSpec((ns, 1, d), lambda i: (0, 0, 0)),
                   pl.BlockSpec((tm, d), lambda i: (proj_tile(i), 0))],
        out_shape=[jax.ShapeDtypeStruct((ns, 1, d), F32), jax.ShapeDtypeStruct((m, d), F32)],
        scratch_shapes=[pltpu.VMEM((nchunk, k, W_CHUNK), BF16),
                        pltpu.VMEM((nchunk, x_s.shape[0], W_CHUNK), F32),
                        pltpu.VMEM((ATT_SLOTS,) + score_tile, F32),
                        pltpu.VMEM((ATT_SLOTS,) + score_tile, BF16),
                        pltpu.VMEM((tm, k), BF16), pltpu.VMEM((tm, k), BF16),
                        pltpu.VMEM((tm, d), F32)],
        compiler_params=_params("arbitrary"),
        name="attn_proj",
    )(a_s, gate_s, x_s, q, kdup, kdup, vt, vt, gate, bias_band, sink_t, w, g.reshape(1, d), x)


def _cached_attn_kernel(q_ref, ckt_ref, cvt_ref, kn_ref, vn_ref, sinks_ref, bpast_ref, bnew_ref, o_ref):
    shape = (N_Q_HEADS, KV_WIDTH)
    lane_kv = lax.broadcasted_iota(jnp.int32, shape, 1) // HEAD_DIM
    row_kv = lax.broadcasted_iota(jnp.int32, shape, 0) // GROUP
    own = lane_kv == row_kv
    sink = sinks_ref[:, :1]
    nt = (((1,), (1,)), ((), ()))
    seqs = range(q_ref.shape[0])
    qm = []
    for b in seqs:
        q = q_ref[b]
        qt = jnp.concatenate([q] * N_KV_HEADS, axis=1)
        qm.append(jnp.where(own, qt, 0.0).astype(BF16))
    s = [jnp.dot(qm[b], ckt_ref[b].astype(BF16), preferred_element_type=F32) + bpast_ref[...] for b in seqs]
    s_new = [jnp.sum(qm[b].astype(F32) * kn_ref[b:b + 1, :].astype(BF16).astype(F32), axis=-1, keepdims=True)
             + bnew_ref[:, :1] for b in seqs]
    m = [jnp.maximum(jnp.maximum(jnp.max(s[b], axis=-1, keepdims=True), s_new[b]), sink) for b in seqs]
    p = [jnp.exp(s[b] - m[b]) for b in seqs]
    p_new = [jnp.exp(s_new[b] - m[b]) for b in seqs]
    denom = [jnp.sum(p[b], axis=-1, keepdims=True) + p_new[b] + jnp.exp(sink - m[b]) for b in seqs]
    pv = [lax.dot_general(p[b].astype(BF16), cvt_ref[b].astype(BF16), nt, preferred_element_type=F32)
          for b in seqs]
    for b in seqs:
        o_all = pv[b] + p_new[b].astype(BF16).astype(F32) * vn_ref[b:b + 1, :].astype(BF16).astype(F32)
        o_all = jnp.where(own, o_all, 0.0)
        o = o_all[:, :HEAD_DIM]
        for hk in range(1, N_KV_HEADS):
            o = o + o_all[:, hk * HEAD_DIM:(hk + 1) * HEAD_DIM]
        o_ref[b] = o / denom[b]


def cached_attention(q, cache_kt, cache_vt, k_new, v_new, sinks, bias_past, bias_new):
    bsz, _, rows = cache_kt.shape
    nseq = math.gcd(bsz, SEQS_PER_STEP)
    per_seq = lambda r, n: pl.BlockSpec((nseq, r, n), lambda b: (b, 0, 0))
    new_row = pl.BlockSpec((nseq, KV_WIDTH), lambda b: (b, 0))
    return pl.pallas_call(
        _cached_attn_kernel,
        grid=(bsz // nseq,),
        in_specs=[
            per_seq(N_Q_HEADS, HEAD_DIM), per_seq(KV_WIDTH, rows), per_seq(KV_WIDTH, rows),
            new_row, new_row,
            _resident((N_Q_HEADS, LANES)), _resident((N_Q_HEADS, rows)), _resident((N_Q_HEADS, LANES)),
        ],
        out_specs=per_seq(N_Q_HEADS, HEAD_DIM),
        out_shape=jax.ShapeDtypeStruct((bsz, N_Q_HEADS, HEAD_DIM), F32),
        compiler_params=_params("parallel"),
        name="cached_attention",
    )(q, cache_kt, cache_vt, k_new, v_new, sinks, bias_past, bias_new)


def kernel(x_prompt, x_sample, state_conv, state_h, cache_k, cache_v, a_norm_pre, a_norm_post,
           a_w_in, a_conv_w, a_conv_b, a_w_r, a_b_r, a_w_i, a_b_i, a_lambda, a_w_out, kv_norm, w_kv,
           b_norm_pre, b_norm_post, b_w_qg, b_sinks, b_w_out, rel_bias_table):
    bsz, t, d = x_prompt.shape
    dbsz, dt, _ = x_sample.shape
    assert a_w_in.shape[0] == 1 and b_w_qg.shape[0] == 1 and dt == 1
    assert t % BLOCK == 0 and t >= WINDOW
    past_rows = cache_k.shape[1]
    assert past_rows == min(WINDOW, PAST_LEN)

    sinks = b_sinks[0]
    bias_band, sink_t, bias_past, bias_new, sink_col = bias_tables(rel_bias_table, sinks, past_rows)

    tm = 2 * SUB_ROWS
    xp = x_prompt.reshape(bsz * t, d)
    xs = x_sample

    gate_s, hs, s_conv_t, y_even, y_odd, p_conv, p_h = rglru_front(
        xs, jnp.transpose(state_conv[0], (1, 0, 2)), state_h[0], xp, a_norm_pre[0], a_w_in[0],
        a_conv_w[0], a_conv_b[0], a_w_r[0], a_b_r[0], a_w_i[0], a_b_i[0], a_lambda[0], seq_len=t)
    xs1, x1 = proj_norm_res(hs, gate_s, xs, (y_even, y_odd), a_w_out[0], a_norm_post[0], xp, tm)

    ks, vs, qs, gate_sb, q, gate_b, kdup, vt, k_tail, v_tail = norm_proj_kvq(
        xs1, x1, kv_norm, b_norm_pre[0], w_kv, b_w_qg[0], SUB_ROWS, seq_len=t)
    cache_kt = jnp.transpose(cache_k, (0, 2, 3, 1)).reshape(dbsz, KV_WIDTH, past_rows)
    cache_vt = jnp.transpose(cache_v, (0, 2, 3, 1)).reshape(dbsz, KV_WIDTH, past_rows)
    os_ = cached_attention(qs.reshape(dbsz, N_Q_HEADS, HEAD_DIM), cache_kt, cache_vt,
                           ks, vs,
                           sink_col.reshape(N_Q_HEADS, LANES), bias_past.reshape(N_Q_HEADS, past_rows),
                           bias_new.reshape(N_Q_HEADS, LANES))
    y_sample, y_prompt = attn_proj(os_.reshape(dbsz, ATT_WIDTH), gate_sb, xs1, q, kdup, vt, gate_b,
                                   bias_band, sink_t, b_w_out[0], b_norm_post[0], x1, seq_len=t)
    y_prompt = y_prompt.reshape(bsz, t, d)
    p_k = jnp.transpose(k_tail.reshape(bsz, N_KV_HEADS, HEAD_DIM, WINDOW), (0, 3, 1, 2))
    p_v = jnp.transpose(v_tail.reshape(bsz, N_KV_HEADS, HEAD_DIM, WINDOW), (0, 3, 1, 2))

    return (y_prompt, y_sample,
            jnp.transpose(p_conv, (1, 0, 2))[None], p_h[None], p_k, p_v,
            jnp.transpose(s_conv_t, (1, 0, 2))[None], hs[None],
            ks.reshape(dbsz, 1, N_KV_HEADS, HEAD_DIM), vs.reshape(dbsz, 1, N_KV_HEADS, HEAD_DIM))
```

```python
import functools
import math

import jax
import jax.numpy as jnp
from jax import lax
from jax.experimental import pallas as pl
from jax.experimental.pallas import tpu as pltpu

F32 = jnp.float32
BF16 = jnp.bfloat16

D_MODEL = 2048
LRU_WIDTH = 2048
LRU_BLOCKS = 8
LRU_BLOCK_W = LRU_WIDTH // LRU_BLOCKS
CONV_W = 4
LRU_C = 8.0
HEAD_DIM = 64
N_Q_HEADS = 32
N_KV_HEADS = 8
GROUP = N_Q_HEADS // N_KV_HEADS
ATT_WIDTH = N_Q_HEADS * HEAD_DIM
KV_WIDTH = N_KV_HEADS * HEAD_DIM
WINDOW = 128
BLOCK = WINDOW
N_BUCKETS = 32
MAX_DISTANCE = 128
RMS_EPS = 1e-6
NEG_INF = -1e30
LOG2_E = 1.4426950408889634
PAST_LEN = 16384

V7X_VMEM_BYTES = 64 * 1024 * 1024
VMEM_LIMIT = V7X_VMEM_BYTES - 8 * 1024 * 1024
SUBLANES = 8
LANES = 128
HEADS_PER_TILE = LANES // HEAD_DIM
SLABS = GROUP // HEADS_PER_TILE
MXU_COLS = 256
SUB_ROWS = 256
W_CHUNK = 512
ATT_ROWS = 64
ATT_SKEW = 2
ATT_SLOTS = 2 * ATT_SKEW + 2
HALF_Q = BLOCK // 2
ATT_KEYS = WINDOW + HALF_Q
SEQS_PER_STEP = 8
SCAN_PIECES_PER_PROJ_PIECE = 2
ATT_PROJ_TILE = 2 * BLOCK
PROJ_TAIL_PIECES = 4
ATT_ROUNDS_PER_PROJ_PIECE = 4
SQRT_FLOOR = 1e-30


def _head_order(half):
    heads = list(range(GROUP))
    return heads if half == 0 else [h ^ 1 for h in heads]


def _params(*semantics):
    return pltpu.CompilerParams(dimension_semantics=semantics, vmem_limit_bytes=VMEM_LIMIT)


def _resident(shape):
    zeros = (0,) * len(shape)
    return pl.BlockSpec(shape, lambda *_: zeros, pipeline_mode=pl.Buffered(1))


def _rms_scale(x):
    return lax.rsqrt(jnp.mean(x * x, axis=-1, keepdims=True) + RMS_EPS)


def _silu(x):
    h = 0.5 * x
    return h * jnp.tanh(h) + h


def _segment_major(rows, inverse=False):
    seg = rows // SUBLANES
    r = lax.broadcasted_iota(jnp.int32, (rows, rows), 0)
    c = lax.broadcasted_iota(jnp.int32, (rows, rows), 1)
    if inverse:
        src = (r % seg) * SUBLANES + r // seg
    else:
        src = (r % SUBLANES) * seg + r // SUBLANES
    return jnp.where(c == src, 1.0, 0.0).astype(BF16)


def _phase_specs(nchunk, tm, k):
    chunk_w = pl.BlockSpec((k, W_CHUNK), lambda i: (0, jnp.minimum(i, nchunk - 1)))
    tile = lambda n: pl.BlockSpec((tm, n), lambda i: (jnp.maximum(i - nchunk, 0), 0))
    return chunk_w, tile


def _dup_heads(x):
    low = lax.broadcasted_iota(jnp.int32, (x.shape[0], LANES), 1) < HEAD_DIM
    out = []
    for c in range(x.shape[1] // LANES):
        col = x[:, c * LANES:(c + 1) * LANES]
        swapped = pltpu.roll(col, HEAD_DIM, axis=1)
        out += [jnp.where(low, col, swapped), jnp.where(low, swapped, col)]
    return jnp.concatenate(out, axis=1)


def _norm_proj_kvq_kernel(nkv, nqg, xs_ref, x_ref, gkv_ref, gq_ref, wkv_ref, wqg_ref,
                          ks_ref, vs_ref, qs_ref, gates_ref,
                          q_ref, gate_ref, kdup_ref, vt_ref, ktail_ref, vtail_ref,
                          wkv_scr, wqg_scr):
    i = pl.program_id(0)
    nchunk = nkv + nqg
    q_chunks = ATT_WIDTH // W_CHUNK
    q_scale = 1.0 / math.sqrt(HEAD_DIM)
    q_scale_log2 = q_scale * LOG2_E

    def sample_rows(g_ref):
        xs = xs_ref[...]
        return (xs * _rms_scale(xs) * g_ref[...]).astype(BF16)

    @pl.when(i < nkv)
    def _():
        wb = wkv_ref[...].astype(BF16)
        wkv_scr[i] = wb
        r = jnp.dot(sample_rows(gkv_ref), wb, preferred_element_type=F32)

        @pl.when(i == 0)
        def _():
            ks_ref[...] = r

        @pl.when(i == 1)
        def _():
            vs_ref[...] = r

    @pl.when((i >= nkv) & (i < nchunk))
    def _():
        c = i - nkv
        wb = wqg_ref[...].astype(BF16)
        wqg_scr[c] = wb
        r = jnp.dot(sample_rows(gq_ref), wb, preferred_element_type=F32)

        @pl.when(c < q_chunks)
        def _():
            qs_ref[...] = r * q_scale

        @pl.when(c >= q_chunks)
        def _():
            gates_ref[...] = r

    @pl.when(i >= nchunk)
    def _():
        tm = x_ref.shape[0]
        for rs in _row_blocks(tm):
            x = x_ref[rs, :]
            xh = x * _rms_scale(x)
            xkv = (xh * gkv_ref[...]).astype(BF16)
            xq = (xh * gq_ref[...]).astype(BF16)
            k = jnp.dot(xkv, wkv_scr[0], preferred_element_type=F32)
            v = jnp.dot(xkv, wkv_scr[1], preferred_element_type=F32)
            kdup_ref[rs, :] = _dup_heads(k).astype(BF16)
            vt_ref[:, rs] = _dup_heads(v).T.astype(BF16)
            for c in range(nqg):
                r = jnp.dot(xq, wqg_scr[c], preferred_element_type=F32)
                if c < q_chunks:
                    q_ref[rs, c * W_CHUNK:(c + 1) * W_CHUNK] = (r * q_scale_log2).astype(q_ref.dtype)
                else:
                    cc = c - q_chunks
                    gate_ref[rs, cc * W_CHUNK:(cc + 1) * W_CHUNK] = r
        ktail_ref[0] = k[k.shape[0] - WINDOW:].T
        vtail_ref[0] = v[v.shape[0] - WINDOW:].T


def norm_proj_kvq(xs, x, g_kv, g_q, w_kv, w_qg, tm, seq_len):
    m, d = x.shape
    ns = xs.shape[0]
    assert w_kv.shape[1] == 2 * KV_WIDTH == 2 * W_CHUNK and seq_len % tm == 0 and tm >= WINDOW
    nkv, nqg = w_kv.shape[1] // W_CHUNK, w_qg.shape[1] // W_CHUNK
    nchunk = nkv + nqg
    tiles = seq_len // tm
    tile = lambda n: pl.BlockSpec((tm, n), lambda i: (jnp.maximum(i - nchunk, 0), 0))
    tail = pl.BlockSpec((1, KV_WIDTH, WINDOW), lambda i: (jnp.maximum(i - nchunk, 0) // tiles, 0, 0))
    kv_chunk = lambda i: (0, jnp.minimum(i, nkv - 1))
    qg_chunk = lambda i: (0, jnp.clip(i - nkv, 0, nqg - 1))
    q_chunks = ATT_WIDTH // W_CHUNK
    whole_s = lambda n: pl.BlockSpec((ns, n), lambda i: (0, 0))
    return pl.pallas_call(
        functools.partial(_norm_proj_kvq_kernel, nkv, nqg),
        grid=(nchunk + m // tm,),
        in_specs=[
            _resident(xs.shape), tile(d), _resident((1, d)), _resident((1, d)),
            pl.BlockSpec((d, W_CHUNK), kv_chunk), pl.BlockSpec((d, W_CHUNK), qg_chunk),
        ],
        out_specs=[
            whole_s(KV_WIDTH), whole_s(KV_WIDTH),
            pl.BlockSpec((ns, W_CHUNK), lambda i: (0, jnp.clip(i - nkv, 0, q_chunks - 1))),
            pl.BlockSpec((ns, W_CHUNK), lambda i: (0, jnp.clip(i - nkv - q_chunks, 0, nqg - q_chunks - 1))),
            tile(ATT_WIDTH), tile(ATT_WIDTH), tile(2 * KV_WIDTH),
            pl.BlockSpec((2 * KV_WIDTH, tm), lambda i: (0, jnp.maximum(i - nchunk, 0))), tail, tail,
        ],
        out_shape=[
            jax.ShapeDtypeStruct((ns, KV_WIDTH), F32),
            jax.ShapeDtypeStruct((ns, KV_WIDTH), F32),
            jax.ShapeDtypeStruct((ns, ATT_WIDTH), F32),
            jax.ShapeDtypeStruct((ns, w_qg.shape[1] - ATT_WIDTH), F32),
            jax.ShapeDtypeStruct((m, ATT_WIDTH), BF16),
            jax.ShapeDtypeStruct((m, ATT_WIDTH), F32),
            jax.ShapeDtypeStruct((m, 2 * KV_WIDTH), BF16),
            jax.ShapeDtypeStruct((2 * KV_WIDTH, m), BF16),
            jax.ShapeDtypeStruct((m // seq_len, KV_WIDTH, WINDOW), F32),
            jax.ShapeDtypeStruct((m // seq_len, KV_WIDTH, WINDOW), F32),
        ],
        scratch_shapes=[pltpu.VMEM((nkv, d, W_CHUNK), BF16), pltpu.VMEM((nqg, d, W_CHUNK), BF16)],
        compiler_params=_params("arbitrary"),
        name="norm_proj_kvq",
    )(xs, x, g_kv.reshape(1, d), g_q.reshape(1, d), w_kv, w_qg)


def _row_blocks(rows):
    sub = min(rows, SUB_ROWS)
    return [slice(r, r + sub) for r in range(0, rows, sub)]


def _proj_norm_res_kernel(nchunk, nparts, as_ref, gs_ref, xs_ref, *refs):
    y_refs = refs[:nparts]
    w_ref, g_ref, x_ref, os_ref, o_ref, w_scr, raw_scr = refs[nparts:]
    i = pl.program_id(0)

    @pl.when(i < nchunk)
    def _():
        wb = w_ref[...].astype(BF16)
        w_scr[i] = wb
        ys = (as_ref[...] * _silu(gs_ref[...])).astype(BF16)
        raw_scr[i] = jnp.dot(ys, wb, preferred_element_type=F32)

    @pl.when(i == nchunk - 1)
    def _():
        o = jnp.concatenate([raw_scr[c] for c in range(nchunk)], axis=1)
        os_ref[...] = xs_ref[:, 0, :] + o * _rms_scale(o) * g_ref[...]

    @pl.when(i >= nchunk)
    def _():
        tm = x_ref.shape[0]
        part_rows = tm // nparts

        for rs in _row_blocks(tm):
            part, off = divmod(rs.start, part_rows)
            y = y_refs[part][off:off + rs.stop - rs.start, :]
            o = jnp.concatenate([jnp.dot(y, w_scr[c], preferred_element_type=F32)
                                 for c in range(nchunk)], axis=1)
            o_ref[rs, :] = x_ref[rs, :] + o * _rms_scale(o) * g_ref[...]


def proj_norm_res(a_s, gate_s, x_s, y_parts, w, g, x, tm):
    k, d = w.shape
    m = x.shape[0]
    ns = x_s.shape[0]
    nchunk = d // W_CHUNK
    nparts = len(y_parts)
    assert (tm // nparts) % min(tm, SUB_ROWS) == 0
    chunk_w, tile = _phase_specs(nchunk, tm, k)
    part = pl.BlockSpec((tm // nparts, k), lambda i: (jnp.maximum(i - nchunk, 0), 0))
    return pl.pallas_call(
        functools.partial(_proj_norm_res_kernel, nchunk, nparts),
        grid=(nchunk + m // tm,),
        in_specs=[_resident(a_s.shape), _resident(gate_s.shape), _resident(x_s.shape)]
        + [part] * nparts + [chunk_w, _resident((1, d)), tile(d)],
        out_specs=[pl.BlockSpec((ns, d), lambda i: (0, 0)), tile(d)],
        out_shape=[jax.ShapeDtypeStruct((ns, d), F32), jax.ShapeDtypeStruct((m, d), F32)],
        scratch_shapes=[pltpu.VMEM((nchunk, k, W_CHUNK), BF16),
                        pltpu.VMEM((nchunk, ns, W_CHUNK), F32)],
        compiler_params=_params("arbitrary"),
        name="proj_norm_res",
    )(a_s, gate_s, x_s, *y_parts, w, g.reshape(1, d), x)


def _lru_gate_dots(conv, wr_half, wi_half):
    cb = conv.astype(BF16)
    return (jnp.dot(cb, wr_half, preferred_element_type=F32),
            jnp.dot(cb, wi_half, preferred_element_type=F32))


def _lru_gates(conv, wr_half, br, wi_half, bi, lam):
    return _lru_gate_math(conv, _lru_gate_dots(conv, wr_half, wi_half), br, bi, lam)


def _lru_gate_math(conv, half_pre, br, bi, lam):
    th_r = jnp.tanh(half_pre[0] + 0.5 * br)
    th_i = jnp.tanh(half_pre[1] + 0.5 * bi)
    nl = -lam
    softplus = jnp.maximum(nl, 0.0) + jnp.log1p(jnp.exp(-jnp.abs(nl)))
    half = (0.5 * LRU_C) * softplus
    x = th_r * half + half
    a = jnp.exp2(x * -LOG2_E)
    z = jnp.tanh(x) * (a * a + 1.0)
    mult = z * lax.rsqrt(jnp.maximum(z, SQRT_FLOOR))
    hc = 0.5 * conv
    return a, mult * (hc * th_i + hc)


def _interleave(*stages):
    live = [[stage, share] for stage, share in stages]
    while live:
        for entry in list(live):
            try:
                for _ in range(entry[1]):
                    next(entry[0])
            except StopIteration:
                live.remove(entry)


def _in_proj_tile(x_ref, rs, g_ref, w_scr, ug_ref):
    x = x_ref[rs, :]
    xn = (x * _rms_scale(x) * g_ref[...]).astype(BF16)
    xn = jnp.dot(_segment_major(xn.shape[0]), xn, preferred_element_type=F32).astype(BF16)
    for c in range(w_scr.shape[0]):
        for n0 in range(0, W_CHUNK, MXU_COLS):
            ug_ref[:, c * W_CHUNK + n0:c * W_CHUNK + n0 + MXU_COLS] = jnp.dot(
                xn, w_scr[c, :, n0:n0 + MXU_COLS], preferred_element_type=F32)
            yield


def _rglru_tile(ug_ref, y_ref, cw_ref, cb_ref, wr_ref, br_ref, wi_ref, bi_ref, lam_ref, h_scr, tail_scr):
    tc = ug_ref.shape[0]
    seg = tc // SUBLANES
    ntaps = CONV_W - 1
    bw = LRU_BLOCK_W
    sub = lax.broadcasted_iota(jnp.int32, (SUBLANES, bw), 0)
    first = sub == 0
    time_order = _segment_major(tc, inverse=True)

    def shift_in(x, row0):
        return jnp.where(first, row0, pltpu.roll(x, 1, axis=0))

    def group(x, j):
        return x[j * SUBLANES:(j + 1) * SUBLANES]

    def store_time_order(cols, y):
        y_ref[:, cols] = jnp.dot(time_order, y, preferred_element_type=F32).astype(y_ref.dtype)

    pending = None
    for n in range(LRU_BLOCKS):
        cs = slice(n * bw, (n + 1) * bw)
        u = ug_ref[:, cs]
        tail = tail_scr[:, cs]
        before = [shift_in(group(u, seg - m), tail[ntaps - m:ntaps - m + 1])
                  for m in range(ntaps, 0, -1)]
        ext = jnp.concatenate(before + [u], axis=0)
        tail_scr[:, cs] = jnp.concatenate(
            [group(u, seg - m)[SUBLANES - 1:] for m in range(ntaps, 0, -1)], axis=0)
        cw = cw_ref[:, cs]
        conv = cb_ref[:, cs]
        for tap in range(CONV_W):
            conv = conv + ext[tap * SUBLANES:tap * SUBLANES + tc] * cw[tap:tap + 1]
        yield

        half_pre = _lru_gate_dots(conv, wr_ref[n], wi_ref[n])
        yield

        if pending is not None:
            store_time_order(*pending)
        yield

        a, b = _lru_gate_math(conv, half_pre, br_ref[:, cs], bi_ref[:, cs], lam_ref[:, cs])

        h = b[:SUBLANES]
        acc = a[:SUBLANES]
        h_loc, a_cum = [h], [acc]
        for j in range(1, seg):
            sl = slice(j * SUBLANES, (j + 1) * SUBLANES)
            h = a[sl] * h + b[sl]
            acc = a[sl] * acc
            h_loc.append(h)
            a_cum.append(acc)

        step = 1
        while step < SUBLANES:
            keep = sub >= step
            h = jnp.where(keep, acc * pltpu.roll(h, step, axis=0) + h, h)
            acc = jnp.where(keep, acc * pltpu.roll(acc, step, axis=0), acc)
            step *= 2
        h_prev = h_scr[:, cs]
        after = h + acc * h_prev
        h_in = shift_in(after, h_prev)
        h_scr[:, cs] = after[SUBLANES - 1:]

        hs = jnp.concatenate([h_loc[j] + a_cum[j] * h_in for j in range(seg)], axis=0)
        y = (hs * _silu(ug_ref[:, LRU_WIDTH + n * bw:LRU_WIDTH + (n + 1) * bw])).astype(BF16)
        pending = (cs, y)
        yield

    store_time_order(*pending)
    yield


def _rglru_front_kernel(nchunk, npairs, chunks, xs_ref, x_ref, g_ref, w_ref,
                        scprev_ref, sh0_ref, cw_ref, cb_ref, wr_ref, br_ref, wi_ref, bi_ref, lam_ref,
                        gs_ref, hs_ref, scnew_ref, y_even_ref, y_odd_ref, cnew_ref, hlast_ref,
                        w_scr, wr_scr, wi_scr, us_scr, ug0_scr, ug1_scr, h_scr, tail_scr):
    i = pl.program_id(0)
    p = i - nchunk
    tc = SUB_ROWS
    half = nchunk // 2
    lru = (cw_ref, cb_ref, wr_scr, br_ref, wi_scr, bi_ref, lam_ref, h_scr, tail_scr)

    @pl.when(i < nchunk)
    def _():
        wb = w_ref[...].astype(BF16)
        w_scr[i] = wb
        xs = xs_ref[:, 0, :]
        xsn = (xs * _rms_scale(xs) * g_ref[...]).astype(BF16)
        r = jnp.dot(xsn, wb, preferred_element_type=F32)
        gs_ref[...] = r

        @pl.when(i < half)
        def _():
            us_scr[i] = r

    @pl.when(i == nchunk - 1)
    def _():
        wr_scr[...] = (0.5 * wr_ref[...]).astype(BF16)
        wi_scr[...] = (0.5 * wi_ref[...]).astype(BF16)
        bw = LRU_BLOCK_W
        for n in range(LRU_BLOCKS):
            cs = slice(n * bw, (n + 1) * bw)
            c, off = divmod(n * bw, W_CHUNK)
            u = us_scr[c, :, off:off + bw]
            cw = cw_ref[:, cs]
            conv = cb_ref[:, cs]
            for tap in range(CONV_W - 1):
                conv = conv + scprev_ref[tap, :, cs] * cw[tap:tap + 1]
                if tap > 0:
                    scnew_ref[tap - 1, :, cs] = scprev_ref[tap, :, cs]
            conv = conv + u * cw[CONV_W - 1:]
            scnew_ref[CONV_W - 2, :, cs] = u
            a, b = _lru_gates(conv, wr_scr[n], br_ref[:, cs], wi_scr[n], bi_ref[:, cs], lam_ref[:, cs])
            hs_ref[:, cs] = a * sh0_ref[:, cs] + b

    def project_even():
        return _in_proj_tile(x_ref, slice(0, tc), g_ref, w_scr, ug0_scr)

    def scan_odd():
        return _rglru_tile(ug1_scr, y_odd_ref, *lru)

    @pl.when(p == 0)
    def _():
        _interleave((project_even(), 1))

    @pl.when((p > 0) & (p < npairs))
    def _():
        _interleave((project_even(), 1), (scan_odd(), SCAN_PIECES_PER_PROJ_PIECE))

    @pl.when(p == npairs)
    def _():
        _interleave((scan_odd(), 1))

    @pl.when(p > 0)
    def _():
        b = (2 * p - 1) // chunks
        hlast_ref[pl.ds(b, 1), :] = h_scr[...]
        for tap in range(CONV_W - 1):
            cnew_ref[tap, pl.ds(b, 1), :] = tail_scr[tap:tap + 1, :]

    @pl.when((p >= 0) & (p < npairs))
    def _():
        @pl.when((2 * p) % chunks == 0)
        def _():
            h_scr[...] = jnp.zeros(h_scr.shape, F32)
            tail_scr[...] = jnp.zeros(tail_scr.shape, F32)

        _interleave((_in_proj_tile(x_ref, slice(tc, 2 * tc), g_ref, w_scr, ug1_scr), 1),
                    (_rglru_tile(ug0_scr, y_even_ref, *lru), SCAN_PIECES_PER_PROJ_PIECE))


def rglru_front(xs, s_conv_prev, s_h0, x, g, w_in, conv_w, conv_b, w_r, b_r, w_i, b_i, lam,
                seq_len):
    m, d = x.shape
    ns = xs.shape[0]
    w = w_in.shape[1] // 2
    tc = SUB_ROWS
    nchunk = w_in.shape[1] // W_CHUNK
    half = nchunk // 2
    bsz = m // seq_len
    chunks = seq_len // tc
    npairs = m // (2 * tc)
    assert seq_len % (2 * tc) == 0 and tc % (SUBLANES * SUBLANES) == 0 and tc // SUBLANES > CONV_W
    pair = lambda i: jnp.clip(i - nchunk, 0, npairs - 1)
    chunk =lambda i: (0, jnp.minimum(i, nchunk - 1))
    return pl.pallas_call(
        functools.partial(_rglru_front_kernel, nchunk, npairs, chunks),
        grid=(nchunk + npairs + 1,),
        in_specs=[_resident(xs.shape), pl.BlockSpec((2 * tc, d), lambda i: (pair(i), 0)), _resident((1, d)),
                  pl.BlockSpec((d, W_CHUNK), chunk),
                  _resident(s_conv_prev.shape), _resident(s_h0.shape),
                  _resident((CONV_W, w)), _resident((1, w)), _resident(w_r.shape), _resident((1, w)),
                  _resident(w_i.shape), _resident((1, w)), _resident((1, w))],
        out_specs=[
            pl.BlockSpec((ns, W_CHUNK), lambda i: (0, jnp.clip(i - half, 0, half - 1))),
            pl.BlockSpec((ns, w), lambda i: (0, 0)),
            pl.BlockSpec(s_conv_prev.shape, lambda i: (0, 0, 0)),
            pl.BlockSpec((tc, w), lambda i: (pair(i), 0)),
            pl.BlockSpec((tc, w), lambda i: (jnp.clip(i - nchunk - 1, 0, npairs - 1), 0)),
            pl.BlockSpec((CONV_W - 1, bsz, w), lambda i: (0, 0, 0)), pl.BlockSpec((bsz, w), lambda i: (0, 0)),
        ],
        out_shape=[
            jax.ShapeDtypeStruct((ns, w), F32),
            jax.ShapeDtypeStruct((ns, w), F32),
            jax.ShapeDtypeStruct(s_conv_prev.shape, F32),
            jax.ShapeDtypeStruct((m // 2, w), BF16),
            jax.ShapeDtypeStruct((m // 2, w), BF16),
            jax.ShapeDtypeStruct((CONV_W - 1, bsz, w), F32),
            jax.ShapeDtypeStruct((bsz, w), F32),
        ],
        scratch_shapes=[pltpu.VMEM((nchunk, d, W_CHUNK), BF16),
                        pltpu.VMEM(w_r.shape, BF16), pltpu.VMEM(w_i.shape, BF16),
                        pltpu.VMEM((half, ns, W_CHUNK), F32),
                        pltpu.VMEM((tc, 2 * w), F32), pltpu.VMEM((tc, 2 * w), F32),
                        pltpu.VMEM((1, w), F32), pltpu.VMEM((CONV_W - 1, w), F32)],
        compiler_params=_params("arbitrary"),
        name="rglru_front",
    )(xs, x, g.reshape(1, d), w_in, s_conv_prev, s_h0,
      conv_w, conv_b.reshape(1, w), w_r, b_r.reshape(1, w), w_i, b_i.reshape(1, w), lam.reshape(1, w))


def _buckets(dist):
    n = jnp.maximum(dist, 0)
    max_exact = N_BUCKETS // 2
    nf = jnp.maximum(n, 1).astype(F32)
    large = max_exact + jnp.floor(jnp.log(nf / max_exact) / math.log(MAX_DISTANCE / max_exact)
                                  * (N_BUCKETS - max_exact)).astype(jnp.int32)
    large = jnp.minimum(large, N_BUCKETS - 1)
    return jnp.where(n < max_exact, n, large)


def _lookup(bucket, valid, table_ref, head):
    bias = jnp.zeros(bucket.shape, F32)
    for b in range(N_BUCKETS):
        bias = jnp.where(bucket == b, table_ref[b, head], bias)
    return jnp.where(valid, bias, NEG_INF)


def _bias_kernel(table_ref, sinks_ref, band_ref, sinkt_ref, past_ref, new_ref, sinkcol_ref):
    hk = pl.program_id(0)
    dist = lax.broadcasted_iota(jnp.int32, (1, LANES), 1)
    bucket = _buckets(dist)
    key_row = lax.broadcasted_iota(jnp.int32, (ATT_KEYS, HALF_Q), 0)
    true_dist = lax.broadcasted_iota(jnp.int32, (ATT_KEYS, HALF_Q), 1) + BLOCK - key_row
    visible = (true_dist >= 0) & (true_dist < WINDOW)

    def band(head):
        row = _lookup(bucket, dist < WINDOW, table_ref, head) * LOG2_E
        full = pltpu.roll(jnp.broadcast_to(row, (ATT_KEYS, LANES)), 0, axis=1, stride=1, stride_axis=0)
        return jnp.where(visible, full[:, :HALF_Q], NEG_INF * LOG2_E)

    rows = past_ref.shape[2]
    d_past = rows - lax.broadcasted_iota(jnp.int32, (1, rows), 1)
    b_past = _buckets(d_past)
    ok_past = (d_past >= 0) & (d_past < WINDOW)
    d_new = jnp.zeros((1, LANES), jnp.int32)
    b_new = _buckets(d_new)
    for g in range(GROUP):
        head = hk * GROUP + g
        bias = band(head)
        for half in range(2):
            slot = _head_order(half).index(g)
            cs = slice(slot * HALF_Q, (slot + 1) * HALF_Q)
            band_ref[0, half, 0, :, cs] = bias
            prev_rows = BLOCK - half * HALF_Q
            band_ref[1, half, 0, :, cs] = jnp.where(key_row < prev_rows, NEG_INF, bias)
            sinkt_ref[half, 0, :, cs] = jnp.full((1, HALF_Q), sinks_ref[head] * LOG2_E, F32)
        past_ref[0, g:g + 1, :] = _lookup(b_past, ok_past, table_ref, head)
        new_ref[0, g:g + 1, :] = _lookup(b_new, d_new == 0, table_ref, head)
        sinkcol_ref[0, g:g + 1, :] = jnp.full((1, LANES), sinks_ref[head], F32)


def bias_tables(table, sinks, past_rows):
    assert WINDOW <= LANES and BLOCK % LANES == 0
    smem = pl.BlockSpec(memory_space=pltpu.SMEM)
    return pl.pallas_call(
        _bias_kernel,
        grid=(N_KV_HEADS,),
        in_specs=[smem, smem],
        out_specs=[
            pl.BlockSpec((2, 2, 1, ATT_KEYS, GROUP * HALF_Q), lambda h: (0, 0, h, 0, 0)),
            pl.BlockSpec((2, 1, 1, GROUP * HALF_Q), lambda h: (0, h, 0, 0)),
            pl.BlockSpec((1, GROUP, past_rows), lambda h: (h, 0, 0)),
            pl.BlockSpec((1, GROUP, LANES), lambda h: (h, 0, 0)),
            pl.BlockSpec((1, GROUP, LANES), lambda h: (h, 0, 0)),
        ],
        out_shape=[
            jax.ShapeDtypeStruct((2, 2, N_KV_HEADS, ATT_KEYS, GROUP * HALF_Q), F32),
            jax.ShapeDtypeStruct((2, N_KV_HEADS, 1, GROUP * HALF_Q), F32),
            jax.ShapeDtypeStruct((N_KV_HEADS, GROUP, past_rows), F32),
            jax.ShapeDtypeStruct((N_KV_HEADS, GROUP, LANES), F32),
            jax.ShapeDtypeStruct((N_KV_HEADS, GROUP, LANES), F32),
        ],
        compiler_params=_params("parallel"),
        name="bias_tables",
    )(table, sinks)


def _band_attn_rounds(first_tile, q_ref, kp_ref, kc_ref, vp_ref, vc_ref, gate_ref, bias_ref, sink_ref,
                      y_ref, s_scr, p_scr):
    nt = (((1,), (1,)), ((), ()))
    low = (lax.broadcasted_iota(jnp.int32, (1, LANES), 1) < HEAD_DIM)
    keep_low = low.astype(BF16)
    keep_high = 1 - keep_low
    keep = (keep_low, keep_high)
    nkeys = 2 * BLOCK
    ones_rows = jnp.where(lax.broadcasted_iota(jnp.int32, (2 * SUBLANES, nkeys), 0) == 0,
                          1.0, 0.0).astype(BF16)
    rows = ATT_ROWS
    nslot = s_scr.shape[0]
    items = [(blk, hk, half) for blk in range(q_ref.shape[0] // BLOCK)
             for hk in range(N_KV_HEADS) for half in range(2)]
    assert nslot % 2 == 0

    def rows_of(blk):
        return slice(blk * BLOCK, (blk + 1) * BLOCK)

    def key_rows(half):
        return slice(half * HALF_Q, half * HALF_Q + ATT_KEYS)

    for slot in range(nslot):
        dead = slice(ATT_KEYS, nkeys) if slot % 2 == 0 else slice(0, HALF_Q)
        p_scr[slot, dead, :] = jnp.zeros((HALF_Q, p_scr.shape[2]), BF16)

    def scores(idx):
        blk, hk, half = items[idx]
        variant = first_tile if blk == 0 else 0
        cs = slice(hk * LANES, (hk + 1) * LANES)
        k_prev = kp_ref[:, cs] if blk == 0 else kc_ref[rows_of(blk - 1), cs]
        k_cur = kc_ref[rows_of(blk), cs]
        kd = (jnp.concatenate([k_prev, k_cur[:HALF_Q]], axis=0) if half == 0
              else jnp.concatenate([k_prev[HALF_Q:], k_cur], axis=0))
        q0 = blk * BLOCK + half * HALF_Q
        qs = jnp.concatenate(
            [q_ref[q0:q0 + HALF_Q, (hk * SLABS + h // HEADS_PER_TILE) * LANES:
                   (hk * SLABS + h // HEADS_PER_TILE + 1) * LANES] * keep[h % HEADS_PER_TILE]
             for h in _head_order(half)], axis=0)
        s = lax.dot_general(kd, qs, nt, preferred_element_type=F32) + bias_ref[variant, half, hk]
        s_scr[idx % nslot, key_rows(half), :] = s
        return jnp.maximum(jnp.max(s, axis=0, keepdims=True), sink_ref[half, hk])

    def softmax(idx, m):
        blk, hk, half = items[idx]
        slot = idx % nslot
        lo = half * HALF_Q
        for r in range(lo, lo + ATT_KEYS, rows):
            p_scr[slot, r:r + rows, :] = jnp.exp2(s_scr[slot, r:r + rows, :] - m).astype(BF16)
        return jnp.exp2(sink_ref[half, hk] - m)

    def weighted_values(idx):
        blk, hk, half = items[idx]
        vs = slice(hk * LANES, hk * LANES + HEAD_DIM)
        v_prev = vp_ref[vs, :] if blk == 0 else vc_ref[vs, rows_of(blk - 1)]
        vt = jnp.concatenate([v_prev, vc_ref[vs, rows_of(blk)]], axis=1)
        lhs_v = jnp.concatenate([vt, ones_rows], axis=0)
        return jnp.dot(lhs_v, p_scr[idx % nslot], preferred_element_type=F32)

    low_q = lax.broadcasted_iota(jnp.int32, (HEAD_DIM, LANES), 1) < HALF_Q

    def finish(blk, hk, ots, sink_ws):
        o = [ots[half][:HEAD_DIM] * (1.0 / (ots[half][HEAD_DIM:HEAD_DIM + 1] + sink_ws[half]))
             for half in range(2)]
        for sl in range(SLABS):
            a, b = (oh[:, sl * LANES:(sl + 1) * LANES] for oh in o)
            even = jnp.where(low_q, a, b)
            odd = pltpu.roll(jnp.where(low_q, b, a), HALF_Q, axis=1)
            pair = jnp.concatenate([even, odd], axis=0)
            c0 = (hk * SLABS + sl) * LANES
            y_ref[rows_of(blk), c0:c0 + LANES] = (
                pair.T * _silu(gate_ref[rows_of(blk), c0:c0 + LANES])).astype(y_ref.dtype)

    n = len(items)
    offs, sink_ws, outs = {}, {}, {}
    for k in range(-2 * ATT_SKEW, n + ATT_SKEW):
        if 0 <= k + 2 * ATT_SKEW < n:
            offs[k + 2 * ATT_SKEW] = scores(k + 2 * ATT_SKEW)
        if 0 <= k + ATT_SKEW < n:
            sink_ws[k + ATT_SKEW] = softmax(k + ATT_SKEW, offs.pop(k + ATT_SKEW))
        if 0 <= k < n:
            outs[k] = weighted_values(k)
        j = k - ATT_SKEW
        if 0 <= j < n and items[j][2] == 1:
            finish(items[j][0], items[j][1], [outs.pop(j - 1), outs.pop(j)],
                   [sink_ws.pop(j - 1), sink_ws.pop(j)])
        yield


def _proj_tile(y_scr, w_scr, g_ref, x_ref, o_ref, raw_scr):
    y = y_scr[...]
    for c in range(w_scr.shape[0]):
        for n0 in range(0, W_CHUNK, MXU_COLS):
            raw_scr[:, c * W_CHUNK + n0:c * W_CHUNK + n0 + MXU_COLS] = jnp.dot(
                y, w_scr[c, :, n0:n0 + MXU_COLS], preferred_element_type=F32)
            yield
    step = y_scr.shape[0] // PROJ_TAIL_PIECES
    for r0 in range(0, y_scr.shape[0], step):
        o = raw_scr[r0:r0 + step, :]
        o_ref[r0:r0 + step, :] = x_ref[r0:r0 + step, :] + o * _rms_scale(o) * g_ref[...]
        yield


def _attn_proj_kernel(nchunk, ntiles, tiles_per_seq, as_ref, gs_ref, xs_ref,
                      q_ref, kp_ref, kc_ref, vp_ref, vc_ref, gate_ref, bias_ref, sink_ref,
                      w_ref, g_ref, x_ref, os_ref, o_ref,
                      w_scr, raws_scr, s_scr, p_scr, ynew_scr, yold_scr, raw_scr):
    i = pl.program_id(0)
    p = i - nchunk

    @pl.when(i < nchunk)
    def _():
        wb = w_ref[...].astype(BF16)
        w_scr[i] = wb
        ys = (as_ref[...] * _silu(gs_ref[...])).astype(BF16)
        raws_scr[i] = jnp.dot(ys, wb, preferred_element_type=F32)

    @pl.when(i == nchunk - 1)
    def _():
        o = jnp.concatenate([raws_scr[c] for c in range(nchunk)], axis=1)
        os_ref[:, 0, :] = xs_ref[...] + o * _rms_scale(o) * g_ref[...]

    def attend():
        first = (p % tiles_per_seq == 0).astype(jnp.int32)
        return _band_attn_rounds(first, q_ref, kp_ref, kc_ref, vp_ref, vc_ref, gate_ref, bias_ref,
                                 sink_ref, ynew_scr, s_scr, p_scr)

    def project():
        return _proj_tile(yold_scr, w_scr, g_ref, x_ref, o_ref, raw_scr)

    @pl.when(p == 0)
    def _():
        _interleave((attend(), 1))
        yold_scr[...] = ynew_scr[...]

    @pl.when((p > 0) & (p < ntiles))
    def _():
        _interleave((attend(), ATT_ROUNDS_PER_PROJ_PIECE), (project(), 1))
        yold_scr[...] = ynew_scr[...]

    @pl.when(p == ntiles)
    def _():
        _interleave((project(), 1))


def attn_proj(a_s, gate_s, x_s, q, kdup, vt, gate, bias_band, sink_t, w, g, x, seq_len):
    k, d = w.shape
    m = x.shape[0]
    ns = x_s.shape[0]
    tm = ATT_PROJ_TILE
    nchunk = d // W_CHUNK
    ntiles = m // tm
    tiles_per_seq = seq_len // tm
    per_tile = tm // BLOCK
    assert seq_len % tm == 0 and tm % BLOCK == 0
    att_tile = lambda i: jnp.clip(i - nchunk, 0, ntiles - 1)
    proj_tile = lambda i: jnp.clip(i - nchunk - 1, 0, ntiles - 1)

    def before(i):
        t = att_tile(i)
        return per_tile * t - jnp.where(t % tiles_per_seq == 0, 0, 1)

    rows = lambda n: pl.BlockSpec((tm, n), lambda i: (att_tile(i), 0))
    score_tile = (2 * BLOCK, GROUP * HALF_Q)
    return pl.pallas_call(
        functools.partial(_attn_proj_kernel, nchunk, ntiles, tiles_per_seq),
        grid=(nchunk + ntiles + 1,),
        in_specs=[
            _resident(a_s.shape), _resident(gate_s.shape), _resident(x_s.shape),
            rows(ATT_WIDTH), pl.BlockSpec((BLOCK, 2 * KV_WIDTH), lambda i: (before(i), 0)),
            rows(2 * KV_WIDTH), pl.BlockSpec((2 * KV_WIDTH, BLOCK), lambda i: (0, before(i))),
            pl.BlockSpec((2 * KV_WIDTH, tm), lambda i: (0, att_tile(i))), rows(ATT_WIDTH),
            _resident(bias_band.shape), _resident(sink_t.shape),
            pl.BlockSpec((k, W_CHUNK), lambda i: (0, jnp.minimum(i, nchunk - 1))), _resident((1, d)),
            pl.BlockSpec((tm, d), lambda i: (proj_tile(i), 0)),
        ],
        out_specs=[pl.BlockSpec((ns, 1, d), lambda i: (0, 0, 0)),
                   pl.BlockSpec((tm, d), lambda i: (proj_tile(i), 0))],
        out_shape=[jax.ShapeDtypeStruct((ns, 1, d), F32), jax.ShapeDtypeStruct((m, d), F32)],
        scratch_shapes=[pltpu.VMEM((nchunk, k, W_CHUNK), BF16),
                        pltpu.VMEM((nchunk, x_s.shape[0], W_CHUNK), F32),
                        pltpu.VMEM((ATT_SLOTS,) + score_tile, F32),
                        pltpu.VMEM((ATT_SLOTS,) + score_tile, BF16),
                        pltpu.VMEM((tm, k), BF16), pltpu.VMEM((tm, k), BF16),
                        pltpu.VMEM((tm, d), F32)],
        compiler_params=_params("arbitrary"),
        name="attn_proj",
    )(a_s, gate_s, x_s, q, kdup, kdup, vt, vt, gate, bias_band, sink_t, w, g.reshape(1, d), x)


def _cached_attn_kernel(q_ref, ckt_ref, cvt_ref, kn_ref, vn_ref, sinks_ref, bpast_ref, bnew_ref, o_ref):
    pairs = N_Q_HEADS // 2
    shape = (N_Q_HEADS, KV_WIDTH)
    row = lax.broadcasted_iota(jnp.int32, shape, 0)
    row_head = 2 * (row % pairs) + row // pairs
    own = lax.broadcasted_iota(jnp.int32, shape, 1) // HEAD_DIM == row_head // GROUP
    low = lax.broadcasted_iota(jnp.int32, (pairs, LANES), 1) < HEAD_DIM

    def by_parity(ref):
        return jnp.concatenate([ref[pl.ds(par, pairs, stride=2), :] for par in range(2)], axis=0)

    sink = by_parity(sinks_ref)[:, :1]
    bias_past = by_parity(bpast_ref)
    bias_new = by_parity(bnew_ref)[:, :1]
    nt = (((1,), (1,)), ((), ()))
    seqs = range(q_ref.shape[0])
    qm = []
    for b in seqs:
        pr = jnp.concatenate([q_ref[b:b + 1, r * LANES:(r + 1) * LANES] for r in range(pairs)], axis=0)
        swapped = pltpu.roll(pr, HEAD_DIM, axis=1)
        q2 = jnp.concatenate([jnp.where(low, pr, swapped), jnp.where(low, swapped, pr)], axis=0)
        qt = jnp.concatenate([q2] * (KV_WIDTH // LANES), axis=1)
        qm.append(jnp.where(own, qt, 0.0).astype(BF16))
    s = [jnp.dot(qm[b], ckt_ref[b].astype(BF16), preferred_element_type=F32) + bias_past for b in seqs]
    s_new = [jnp.sum(qm[b].astype(F32) * kn_ref[b:b + 1, :].astype(BF16).astype(F32), axis=-1, keepdims=True)
             + bias_new for b in seqs]
    m = [jnp.maximum(jnp.maximum(jnp.max(s[b], axis=-1, keepdims=True), s_new[b]), sink) for b in seqs]
    p = [jnp.exp(s[b] - m[b]) for b in seqs]
    p_new = [jnp.exp(s_new[b] - m[b]) for b in seqs]
    denom = [jnp.sum(p[b], axis=-1, keepdims=True) + p_new[b] + jnp.exp(sink - m[b]) for b in seqs]
    pv = [lax.dot_general(p[b].astype(BF16), cvt_ref[b].astype(BF16), nt, preferred_element_type=F32)
          for b in seqs]
    for b in seqs:
        o_all = pv[b] + p_new[b].astype(BF16).astype(F32) * vn_ref[b:b + 1, :].astype(BF16).astype(F32)
        o_all = jnp.where(own, o_all, 0.0)
        o = o_all[:, :LANES]
        for k in range(1, KV_WIDTH // LANES):
            o = o + o_all[:, k * LANES:(k + 1) * LANES]
        o = o / denom[b]
        o = o + pltpu.roll(o, HEAD_DIM, axis=1)
        pr = jnp.where(low, o[:pairs], o[pairs:])
        for r in range(pairs):
            o_ref[b:b + 1, r * LANES:(r + 1) * LANES] = pr[r:r + 1, :]


def cached_attention(q, cache_kt, cache_vt, k_new, v_new, sinks, bias_past, bias_new):
    bsz, _, rows = cache_kt.shape
    assert LANES == 2 * HEAD_DIM and rows == LANES and N_Q_HEADS % 2 == 0
    nseq = math.gcd(bsz, SEQS_PER_STEP)
    cache = pl.BlockSpec((nseq, KV_WIDTH, rows), lambda b: (b, 0, 0))
    seq_rows = lambda n: pl.BlockSpec((nseq, n), lambda b: (b, 0))
    return pl.pallas_call(
        _cached_attn_kernel,
        grid=(bsz // nseq,),
        in_specs=[
            seq_rows(ATT_WIDTH), cache, cache, seq_rows(KV_WIDTH), seq_rows(KV_WIDTH),
            _resident((N_Q_HEADS, LANES)), _resident((N_Q_HEADS, rows)), _resident((N_Q_HEADS, LANES)),
        ],
        out_specs=seq_rows(ATT_WIDTH),
        out_shape=jax.ShapeDtypeStruct((bsz, ATT_WIDTH), F32),
        compiler_params=_params("parallel"),
        name="cached_attention",
    )(q, cache_kt, cache_vt, k_new, v_new, sinks, bias_past, bias_new)


def kernel(x_prompt, x_sample, state_conv, state_h, cache_k, cache_v, a_norm_pre, a_norm_post,
           a_w_in, a_conv_w, a_conv_b, a_w_r, a_b_r, a_w_i, a_b_i, a_lambda, a_w_out, kv_norm, w_kv,
           b_norm_pre, b_norm_post, b_w_qg, b_sinks, b_w_out, rel_bias_table):
    bsz, t, d = x_prompt.shape
    dbsz, dt, _ = x_sample.shape
    assert a_w_in.shape[0] == 1 and b_w_qg.shape[0] == 1 and dt == 1
    assert t % BLOCK == 0 and t >= WINDOW
    past_rows = cache_k.shape[1]
    assert past_rows == min(WINDOW, PAST_LEN)

    sinks = b_sinks[0]
    bias_band, sink_t, bias_past, bias_new, sink_col = bias_tables(rel_bias_table, sinks, past_rows)

    tm = 2 * SUB_ROWS
    xp = x_prompt.reshape(bsz * t, d)
    xs = x_sample

    gate_s, hs, s_conv_t, y_even, y_odd, p_conv, p_h = rglru_front(
        xs, jnp.transpose(state_conv[0], (1, 0, 2)), state_h[0], xp, a_norm_pre[0], a_w_in[0],
        a_conv_w[0], a_conv_b[0], a_w_r[0], a_b_r[0], a_w_i[0], a_b_i[0], a_lambda[0], seq_len=t)
    xs1, x1 = proj_norm_res(hs, gate_s, xs, (y_even, y_odd), a_w_out[0], a_norm_post[0], xp, tm)

    ks, vs, qs, gate_sb, q, gate_b, kdup, vt, k_tail, v_tail = norm_proj_kvq(
        xs1, x1, kv_norm, b_norm_pre[0], w_kv, b_w_qg[0], SUB_ROWS, seq_len=t)
    cache_kt = jnp.transpose(cache_k, (0, 2, 3, 1)).reshape(dbsz, KV_WIDTH, past_rows)
    cache_vt = jnp.transpose(cache_v, (0, 2, 3, 1)).reshape(dbsz, KV_WIDTH, past_rows)
    os_ = cached_attention(qs, cache_kt, cache_vt,
                           ks, vs,
                           sink_col.reshape(N_Q_HEADS, LANES), bias_past.reshape(N_Q_HEADS, past_rows),
                           bias_new.reshape(N_Q_HEADS, LANES))
    y_sample, y_prompt = attn_proj(os_, gate_sb, xs1, q, kdup, vt, gate_b,
                                   bias_band, sink_t, b_w_out[0], b_norm_post[0], x1, seq_len=t)
    y_prompt = y_prompt.reshape(bsz, t, d)
    p_k = jnp.transpose(k_tail.reshape(bsz, N_KV_HEADS, HEAD_DIM, WINDOW), (0, 3, 1, 2))
    p_v = jnp.transpose(v_tail.reshape(bsz, N_KV_HEADS, HEAD_DIM, WINDOW), (0, 3, 1, 2))

    return (y_prompt, y_sample,
            jnp.transpose(p_conv, (1, 0, 2))[None], p_h[None], p_k, p_v,
            jnp.transpose(s_conv_t, (1, 0, 2))[None], hs[None],
            ks.reshape(dbsz, 1, N_KV_HEADS, HEAD_DIM), vs.reshape(dbsz, 1, N_KV_HEADS, HEAD_DIM))
```

```python
import functools
import itertools
import math

import jax
import jax.numpy as jnp
from jax import lax
from jax.experimental import pallas as pl
from jax.experimental.pallas import tpu as pltpu

F32 = jnp.float32
BF16 = jnp.bfloat16

D_MODEL = 2048
LRU_WIDTH = 2048
LRU_BLOCKS = 8
LRU_BLOCK_W = LRU_WIDTH // LRU_BLOCKS
CONV_W = 4
LRU_C = 8.0
HEAD_DIM = 64
N_Q_HEADS = 32
N_KV_HEADS = 8
GROUP = N_Q_HEADS // N_KV_HEADS
ATT_WIDTH = N_Q_HEADS * HEAD_DIM
KV_WIDTH = N_KV_HEADS * HEAD_DIM
WINDOW = 128
BLOCK = WINDOW
N_BUCKETS = 32
MAX_DISTANCE = 128
RMS_EPS = 1e-6
NEG_INF = -1e30
LOG2_E = 1.4426950408889634
PAST_LEN = 16384

V7X_VMEM_BYTES = 64 * 1024 * 1024
VMEM_LIMIT = V7X_VMEM_BYTES - 8 * 1024 * 1024
SUBLANES = 8
LANES = 128
HEADS_PER_TILE = LANES // HEAD_DIM
SLABS = GROUP // HEADS_PER_TILE
MXU_COLS = 256
SUB_ROWS = 256
W_CHUNK = 512
ATT_ROWS = 64
ATT_SKEW = 2
ATT_SLOTS = 2 * ATT_SKEW + 2
HALF_Q = BLOCK // 2
ATT_KEYS = WINDOW + HALF_Q
SEQS_PER_STEP = 8
BIAS_KV_PER_STEP = 2
SCAN_PIECES_PER_PROJ_PIECE = 2
ATT_PROJ_TILE = 2 * BLOCK
PROJ_TAIL_PIECES = 4
ATT_ROUNDS_PER_PROJ_PIECE = 4
SQRT_FLOOR = 1e-30


def _head_order(half):
    heads = list(range(GROUP))
    return heads if half == 0 else [h ^ 1 for h in heads]


def _params(*semantics):
    return pltpu.CompilerParams(dimension_semantics=semantics, vmem_limit_bytes=VMEM_LIMIT)


def _resident(shape):
    zeros = (0,) * len(shape)
    return pl.BlockSpec(shape, lambda *_: zeros, pipeline_mode=pl.Buffered(1))


def _rms_scale(x):
    return lax.rsqrt(jnp.mean(x * x, axis=-1, keepdims=True) + RMS_EPS)


def _silu(x):
    h = 0.5 * x
    return h * jnp.tanh(h) + h


def _segment_major(rows, inverse=False):
    seg = rows // SUBLANES
    r = lax.broadcasted_iota(jnp.int32, (rows, rows), 0)
    c = lax.broadcasted_iota(jnp.int32, (rows, rows), 1)
    if inverse:
        src = (r % seg) * SUBLANES + r // seg
    else:
        src = (r % SUBLANES) * seg + r // SUBLANES
    return jnp.where(c == src, 1.0, 0.0).astype(BF16)


def _phase_specs(nchunk, tm, k):
    chunk_w = pl.BlockSpec((k, W_CHUNK), lambda i: (0, jnp.minimum(i, nchunk - 1)))
    tile = lambda n: pl.BlockSpec((tm, n), lambda i: (jnp.maximum(i - nchunk, 0), 0))
    return chunk_w, tile


def _dup_heads(x):
    low = lax.broadcasted_iota(jnp.int32, (x.shape[0], LANES), 1) < HEAD_DIM
    out = []
    for c in range(x.shape[1] // LANES):
        col = x[:, c * LANES:(c + 1) * LANES]
        swapped = pltpu.roll(col, HEAD_DIM, axis=1)
        out += [jnp.where(low, col, swapped), jnp.where(low, swapped, col)]
    return jnp.concatenate(out, axis=1)


def _norm_proj_kvq_kernel(nkv, nqg, xs_ref, x_ref, gkv_ref, gq_ref, wkv_ref, wqg_ref,
                          ks_ref, vs_ref, qs_ref, gates_ref,
                          q_ref, gate_ref, kdup_ref, vt_ref, ktail_ref, vtail_ref,
                          wkv_scr, wqg_scr):
    i = pl.program_id(0)
    nchunk = nkv + nqg
    q_chunks = ATT_WIDTH // W_CHUNK
    q_scale = 1.0 / math.sqrt(HEAD_DIM)
    q_scale_log2 = q_scale * LOG2_E

    def sample_rows(g_ref):
        xs = xs_ref[...]
        return (xs * _rms_scale(xs) * g_ref[...]).astype(BF16)

    @pl.when(i < nkv)
    def _():
        wb = wkv_ref[...].astype(BF16)
        wkv_scr[i] = wb
        r = jnp.dot(sample_rows(gkv_ref), wb, preferred_element_type=F32)

        @pl.when(i == 0)
        def _():
            ks_ref[...] = r

        @pl.when(i == 1)
        def _():
            vs_ref[...] = r

    @pl.when((i >= nkv) & (i < nchunk))
    def _():
        c = i - nkv
        wb = wqg_ref[...].astype(BF16)
        wqg_scr[c] = wb
        r = jnp.dot(sample_rows(gq_ref), wb, preferred_element_type=F32)

        @pl.when(c < q_chunks)
        def _():
            qs_ref[...] = r * q_scale

        @pl.when(c >= q_chunks)
        def _():
            gates_ref[...] = r

    @pl.when(i >= nchunk)
    def _():
        tm = x_ref.shape[0]
        for rs in _row_blocks(tm):
            x = x_ref[rs, :]
            xh = x * _rms_scale(x)
            xkv = (xh * gkv_ref[...]).astype(BF16)
            xq = (xh * gq_ref[...]).astype(BF16)
            k = jnp.dot(xkv, wkv_scr[0], preferred_element_type=F32)
            v = jnp.dot(xkv, wkv_scr[1], preferred_element_type=F32)
            kdup_ref[rs, :] = _dup_heads(k).astype(BF16)
            vt_ref[:, rs] = _dup_heads(v).T.astype(BF16)
            for c in range(nqg):
                r = jnp.dot(xq, wqg_scr[c], preferred_element_type=F32)
                if c < q_chunks:
                    q_ref[rs, c * W_CHUNK:(c + 1) * W_CHUNK] = (r * q_scale_log2).astype(q_ref.dtype)
                else:
                    cc = c - q_chunks
                    gate_ref[rs, cc * W_CHUNK:(cc + 1) * W_CHUNK] = r
        ktail_ref[0] = k[k.shape[0] - WINDOW:].T
        vtail_ref[0] = v[v.shape[0] - WINDOW:].T


def norm_proj_kvq(xs, x, g_kv, g_q, w_kv, w_qg, tm, seq_len):
    m, d = x.shape
    ns = xs.shape[0]
    assert w_kv.shape[1] == 2 * KV_WIDTH == 2 * W_CHUNK and seq_len % tm == 0 and tm >= WINDOW
    nkv, nqg = w_kv.shape[1] // W_CHUNK, w_qg.shape[1] // W_CHUNK
    nchunk = nkv + nqg
    tiles = seq_len // tm
    tile = lambda n: pl.BlockSpec((tm, n), lambda i: (jnp.maximum(i - nchunk, 0), 0))
    tail = pl.BlockSpec((1, KV_WIDTH, WINDOW), lambda i: (jnp.maximum(i - nchunk, 0) // tiles, 0, 0))
    kv_chunk = lambda i: (0, jnp.minimum(i, nkv - 1))
    qg_chunk = lambda i: (0, jnp.clip(i - nkv, 0, nqg - 1))
    q_chunks = ATT_WIDTH // W_CHUNK
    whole_s = lambda n: pl.BlockSpec((ns, n), lambda i: (0, 0))
    return pl.pallas_call(
        functools.partial(_norm_proj_kvq_kernel, nkv, nqg),
        grid=(nchunk + m // tm,),
        in_specs=[
            _resident(xs.shape), tile(d), _resident((1, d)), _resident((1, d)),
            pl.BlockSpec((d, W_CHUNK), kv_chunk), pl.BlockSpec((d, W_CHUNK), qg_chunk),
        ],
        out_specs=[
            whole_s(KV_WIDTH), whole_s(KV_WIDTH),
            pl.BlockSpec((ns, W_CHUNK), lambda i: (0, jnp.clip(i - nkv, 0, q_chunks - 1))),
            pl.BlockSpec((ns, W_CHUNK), lambda i: (0, jnp.clip(i - nkv - q_chunks, 0, nqg - q_chunks - 1))),
            tile(ATT_WIDTH), tile(ATT_WIDTH), tile(2 * KV_WIDTH),
            pl.BlockSpec((2 * KV_WIDTH, tm), lambda i: (0, jnp.maximum(i - nchunk, 0))), tail, tail,
        ],
        out_shape=[
            jax.ShapeDtypeStruct((ns, KV_WIDTH), F32),
            jax.ShapeDtypeStruct((ns, KV_WIDTH), F32),
            jax.ShapeDtypeStruct((ns, ATT_WIDTH), F32),
            jax.ShapeDtypeStruct((ns, w_qg.shape[1] - ATT_WIDTH), F32),
            jax.ShapeDtypeStruct((m, ATT_WIDTH), BF16),
            jax.ShapeDtypeStruct((m, ATT_WIDTH), F32),
            jax.ShapeDtypeStruct((m, 2 * KV_WIDTH), BF16),
            jax.ShapeDtypeStruct((2 * KV_WIDTH, m), BF16),
            jax.ShapeDtypeStruct((m // seq_len, KV_WIDTH, WINDOW), F32),
            jax.ShapeDtypeStruct((m // seq_len, KV_WIDTH, WINDOW), F32),
        ],
        scratch_shapes=[pltpu.VMEM((nkv, d, W_CHUNK), BF16), pltpu.VMEM((nqg, d, W_CHUNK), BF16)],
        compiler_params=_params("arbitrary"),
        name="norm_proj_kvq",
    )(xs, x, g_kv.reshape(1, d), g_q.reshape(1, d), w_kv, w_qg)


def _row_blocks(rows):
    sub = min(rows, SUB_ROWS)
    return [slice(r, r + sub) for r in range(0, rows, sub)]


def _proj_norm_res_kernel(nchunk, nparts, as_ref, gs_ref, xs_ref, *refs):
    y_refs = refs[:nparts]
    w_ref, g_ref, x_ref, os_ref, o_ref, w_scr, raw_scr = refs[nparts:]
    i = pl.program_id(0)

    @pl.when(i < nchunk)
    def _():
        wb = w_ref[...].astype(BF16)
        w_scr[i] = wb
        ys = (as_ref[...] * _silu(gs_ref[...])).astype(BF16)
        raw_scr[i] = jnp.dot(ys, wb, preferred_element_type=F32)

    @pl.when(i == nchunk - 1)
    def _():
        o = jnp.concatenate([raw_scr[c] for c in range(nchunk)], axis=1)
        os_ref[...] = xs_ref[:, 0, :] + o * _rms_scale(o) * g_ref[...]

    @pl.when(i >= nchunk)
    def _():
        tm = x_ref.shape[0]
        part_rows = tm // nparts

        for rs in _row_blocks(tm):
            part, off = divmod(rs.start, part_rows)
            y = y_refs[part][off:off + rs.stop - rs.start, :]
            o = jnp.concatenate([jnp.dot(y, w_scr[c], preferred_element_type=F32)
                                 for c in range(nchunk)], axis=1)
            o_ref[rs, :] = x_ref[rs, :] + o * _rms_scale(o) * g_ref[...]


def proj_norm_res(a_s, gate_s, x_s, y_parts, w, g, x, tm):
    k, d = w.shape
    m = x.shape[0]
    ns = x_s.shape[0]
    nchunk = d // W_CHUNK
    nparts = len(y_parts)
    assert (tm // nparts) % min(tm, SUB_ROWS) == 0
    chunk_w, tile = _phase_specs(nchunk, tm, k)
    part = pl.BlockSpec((tm // nparts, k), lambda i: (jnp.maximum(i - nchunk, 0), 0))
    return pl.pallas_call(
        functools.partial(_proj_norm_res_kernel, nchunk, nparts),
        grid=(nchunk + m // tm,),
        in_specs=[_resident(a_s.shape), _resident(gate_s.shape), _resident(x_s.shape)]
        + [part] * nparts + [chunk_w, _resident((1, d)), tile(d)],
        out_specs=[pl.BlockSpec((ns, d), lambda i: (0, 0)), tile(d)],
        out_shape=[jax.ShapeDtypeStruct((ns, d), F32), jax.ShapeDtypeStruct((m, d), F32)],
        scratch_shapes=[pltpu.VMEM((nchunk, k, W_CHUNK), BF16),
                        pltpu.VMEM((nchunk, ns, W_CHUNK), F32)],
        compiler_params=_params("arbitrary"),
        name="proj_norm_res",
    )(a_s, gate_s, x_s, *y_parts, w, g.reshape(1, d), x)


def _lru_gate_dots(conv, wr_half, wi_half):
    cb = conv.astype(BF16)
    return (jnp.dot(cb, wr_half, preferred_element_type=F32),
            jnp.dot(cb, wi_half, preferred_element_type=F32))


def _lru_gates(conv, wr_half, br, wi_half, bi, lam):
    return _lru_gate_math(conv, _lru_gate_dots(conv, wr_half, wi_half), br, bi, lam)


def _lru_gate_math(conv, half_pre, br, bi, lam):
    th_r = jnp.tanh(half_pre[0] + 0.5 * br)
    th_i = jnp.tanh(half_pre[1] + 0.5 * bi)
    nl = -lam
    softplus = jnp.maximum(nl, 0.0) + jnp.log1p(jnp.exp(-jnp.abs(nl)))
    half = (0.5 * LRU_C) * softplus
    x = th_r * half + half
    a = jnp.exp2(x * -LOG2_E)
    z = jnp.tanh(x) * (a * a + 1.0)
    mult = z * lax.rsqrt(jnp.maximum(z, SQRT_FLOOR))
    hc = 0.5 * conv
    return a, mult * (hc * th_i + hc)


def _interleave(*stages):
    live = [[stage, share] for stage, share in stages]
    while live:
        for entry in list(live):
            try:
                for _ in range(entry[1]):
                    next(entry[0])
            except StopIteration:
                live.remove(entry)


def _in_proj_tile(x_ref, rs, g_ref, w_scr, ug_ref):
    x = x_ref[rs, :]
    xn = (x * _rms_scale(x) * g_ref[...]).astype(BF16)
    xn = jnp.dot(_segment_major(xn.shape[0]), xn, preferred_element_type=F32).astype(BF16)
    for c in range(w_scr.shape[0]):
        for n0 in range(0, W_CHUNK, MXU_COLS):
            ug_ref[:, c * W_CHUNK + n0:c * W_CHUNK + n0 + MXU_COLS] = jnp.dot(
                xn, w_scr[c, :, n0:n0 + MXU_COLS], preferred_element_type=F32)
            yield


def _rglru_tile(ug_ref, y_ref, cw_ref, cb_ref, wr_ref, br_ref, wi_ref, bi_ref, lam_ref, h_scr, tail_scr):
    tc = ug_ref.shape[0]
    seg = tc // SUBLANES
    ntaps = CONV_W - 1
    bw = LRU_BLOCK_W
    sub = lax.broadcasted_iota(jnp.int32, (SUBLANES, bw), 0)
    first = sub == 0
    time_order = _segment_major(tc, inverse=True)

    def shift_in(x, row0):
        return jnp.where(first, row0, pltpu.roll(x, 1, axis=0))

    def group(x, j):
        return x[j * SUBLANES:(j + 1) * SUBLANES]

    def store_time_order(cols, y):
        y_ref[:, cols] = jnp.dot(time_order, y, preferred_element_type=F32).astype(y_ref.dtype)

    pending = None
    for n in range(LRU_BLOCKS):
        cs = slice(n * bw, (n + 1) * bw)
        u = ug_ref[:, cs]
        tail = tail_scr[:, cs]
        before = [shift_in(group(u, seg - m), tail[ntaps - m:ntaps - m + 1])
                  for m in range(ntaps, 0, -1)]
        ext = jnp.concatenate(before + [u], axis=0)
        tail_scr[:, cs] = jnp.concatenate(
            [group(u, seg - m)[SUBLANES - 1:] for m in range(ntaps, 0, -1)], axis=0)
        cw = cw_ref[:, cs]
        conv = cb_ref[:, cs]
        for tap in range(CONV_W):
            conv = conv + ext[tap * SUBLANES:tap * SUBLANES + tc] * cw[tap:tap + 1]
        yield

        half_pre = _lru_gate_dots(conv, wr_ref[n], wi_ref[n])
        yield

        if pending is not None:
            store_time_order(*pending)
        yield

        a, b = _lru_gate_math(conv, half_pre, br_ref[:, cs], bi_ref[:, cs], lam_ref[:, cs])

        h = b[:SUBLANES]
        acc = a[:SUBLANES]
        h_loc, a_cum = [h], [acc]
        for j in range(1, seg):
            sl = slice(j * SUBLANES, (j + 1) * SUBLANES)
            h = a[sl] * h + b[sl]
            acc = a[sl] * acc
            h_loc.append(h)
            a_cum.append(acc)

        step = 1
        while step < SUBLANES:
            keep = sub >= step
            h = jnp.where(keep, acc * pltpu.roll(h, step, axis=0) + h, h)
            acc = jnp.where(keep, acc * pltpu.roll(acc, step, axis=0), acc)
            step *= 2
        h_prev = h_scr[:, cs]
        after = h + acc * h_prev
        h_in = shift_in(after, h_prev)
        h_scr[:, cs] = after[SUBLANES - 1:]

        hs = jnp.concatenate([h_loc[j] + a_cum[j] * h_in for j in range(seg)], axis=0)
        y = (hs * _silu(ug_ref[:, LRU_WIDTH + n * bw:LRU_WIDTH + (n + 1) * bw])).astype(BF16)
        pending = (cs, y)
        yield

    store_time_order(*pending)
    yield


def _rglru_front_kernel(nchunk, npairs, chunks, xs_ref, x_ref, g_ref, w_ref,
                        scprev_ref, sh0_ref, cw_ref, cb_ref, wr_ref, br_ref, wi_ref, bi_ref, lam_ref,
                        gs_ref, hs_ref, scnew_ref, y_even_ref, y_odd_ref, cnew_ref, hlast_ref,
                        w_scr, wr_scr, wi_scr, us_scr, ug0_scr, ug1_scr, h_scr, tail_scr):
    i = pl.program_id(0)
    p = i - nchunk
    tc = SUB_ROWS
    half = nchunk // 2
    lru = (cw_ref, cb_ref, wr_scr, br_ref, wi_scr, bi_ref, lam_ref, h_scr, tail_scr)

    @pl.when(i < nchunk)
    def _():
        wb = w_ref[...].astype(BF16)
        w_scr[i] = wb
        xs = xs_ref[:, 0, :]
        xsn = (xs * _rms_scale(xs) * g_ref[...]).astype(BF16)
        r = jnp.dot(xsn, wb, preferred_element_type=F32)
        gs_ref[...] = r

        @pl.when(i < half)
        def _():
            us_scr[i] = r

    @pl.when(i == nchunk - 1)
    def _():
        wr_scr[...] = (0.5 * wr_ref[...]).astype(BF16)
        wi_scr[...] = (0.5 * wi_ref[...]).astype(BF16)
        bw = LRU_BLOCK_W
        for n in range(LRU_BLOCKS):
            cs = slice(n * bw, (n + 1) * bw)
            c, off = divmod(n * bw, W_CHUNK)
            u = us_scr[c, :, off:off + bw]
            cw = cw_ref[:, cs]
            conv = cb_ref[:, cs]
            for tap in range(CONV_W - 1):
                conv = conv + scprev_ref[tap, :, cs] * cw[tap:tap + 1]
                if tap > 0:
                    scnew_ref[tap - 1, :, cs] = scprev_ref[tap, :, cs]
            conv = conv + u * cw[CONV_W - 1:]
            scnew_ref[CONV_W - 2, :, cs] = u
            a, b = _lru_gates(conv, wr_scr[n], br_ref[:, cs], wi_scr[n], bi_ref[:, cs], lam_ref[:, cs])
            hs_ref[:, cs] = a * sh0_ref[:, cs] + b

    def project_even():
        return _in_proj_tile(x_ref, slice(0, tc), g_ref, w_scr, ug0_scr)

    def scan_odd():
        return _rglru_tile(ug1_scr, y_odd_ref, *lru)

    @pl.when(p == 0)
    def _():
        _interleave((project_even(), 1))

    @pl.when((p > 0) & (p < npairs))
    def _():
        _interleave((project_even(), 1), (scan_odd(), SCAN_PIECES_PER_PROJ_PIECE))

    @pl.when(p == npairs)
    def _():
        _interleave((scan_odd(), 1))

    @pl.when(p > 0)
    def _():
        b = (2 * p - 1) // chunks
        hlast_ref[pl.ds(b, 1), :] = h_scr[...]
        for tap in range(CONV_W - 1):
            cnew_ref[tap, pl.ds(b, 1), :] = tail_scr[tap:tap + 1, :]

    @pl.when((p >= 0) & (p < npairs))
    def _():
        @pl.when((2 * p) % chunks == 0)
        def _():
            h_scr[...] = jnp.zeros(h_scr.shape, F32)
            tail_scr[...] = jnp.zeros(tail_scr.shape, F32)

        _interleave((_in_proj_tile(x_ref, slice(tc, 2 * tc), g_ref, w_scr, ug1_scr), 1),
                    (_rglru_tile(ug0_scr, y_even_ref, *lru), SCAN_PIECES_PER_PROJ_PIECE))


def rglru_front(xs, s_conv_prev, s_h0, x, g, w_in, conv_w, conv_b, w_r, b_r, w_i, b_i, lam,
                seq_len):
    m, d = x.shape
    ns = xs.shape[0]
    w = w_in.shape[1] // 2
    tc = SUB_ROWS
    nchunk = w_in.shape[1] // W_CHUNK
    half = nchunk // 2
    bsz = m // seq_len
    chunks = seq_len // tc
    npairs = m // (2 * tc)
    assert seq_len % (2 * tc) == 0 and tc % (SUBLANES * SUBLANES) == 0 and tc // SUBLANES > CONV_W
    pair = lambda i: jnp.clip(i - nchunk, 0, npairs - 1)
    chunk =lambda i: (0, jnp.minimum(i, nchunk - 1))
    return pl.pallas_call(
        functools.partial(_rglru_front_kernel, nchunk, npairs, chunks),
        grid=(nchunk + npairs + 1,),
        in_specs=[_resident(xs.shape), pl.BlockSpec((2 * tc, d), lambda i: (pair(i), 0)), _resident((1, d)),
                  pl.BlockSpec((d, W_CHUNK), chunk),
                  _resident(s_conv_prev.shape), _resident(s_h0.shape),
                  _resident((CONV_W, w)), _resident((1, w)), _resident(w_r.shape), _resident((1, w)),
                  _resident(w_i.shape), _resident((1, w)), _resident((1, w))],
        out_specs=[
            pl.BlockSpec((ns, W_CHUNK), lambda i: (0, jnp.clip(i - half, 0, half - 1))),
            pl.BlockSpec((ns, w), lambda i: (0, 0)),
            pl.BlockSpec(s_conv_prev.shape, lambda i: (0, 0, 0)),
            pl.BlockSpec((tc, w), lambda i: (pair(i), 0)),
            pl.BlockSpec((tc, w), lambda i: (jnp.clip(i - nchunk - 1, 0, npairs - 1), 0)),
            pl.BlockSpec((CONV_W - 1, bsz, w), lambda i: (0, 0, 0)), pl.BlockSpec((bsz, w), lambda i: (0, 0)),
        ],
        out_shape=[
            jax.ShapeDtypeStruct((ns, w), F32),
            jax.ShapeDtypeStruct((ns, w), F32),
            jax.ShapeDtypeStruct(s_conv_prev.shape, F32),
            jax.ShapeDtypeStruct((m // 2, w), BF16),
            jax.ShapeDtypeStruct((m // 2, w), BF16),
            jax.ShapeDtypeStruct((CONV_W - 1, bsz, w), F32),
            jax.ShapeDtypeStruct((bsz, w), F32),
        ],
        scratch_shapes=[pltpu.VMEM((nchunk, d, W_CHUNK), BF16),
                        pltpu.VMEM(w_r.shape, BF16), pltpu.VMEM(w_i.shape, BF16),
                        pltpu.VMEM((half, ns, W_CHUNK), F32),
                        pltpu.VMEM((tc, 2 * w), F32), pltpu.VMEM((tc, 2 * w), F32),
                        pltpu.VMEM((1, w), F32), pltpu.VMEM((CONV_W - 1, w), F32)],
        compiler_params=_params("arbitrary"),
        name="rglru_front",
    )(xs, x, g.reshape(1, d), w_in, s_conv_prev, s_h0,
      conv_w, conv_b.reshape(1, w), w_r, b_r.reshape(1, w), w_i, b_i.reshape(1, w), lam.reshape(1, w))


def _buckets(dist):
    n = jnp.maximum(dist, 0)
    max_exact = N_BUCKETS // 2
    nf = jnp.maximum(n, 1).astype(F32)
    large = max_exact + jnp.floor(jnp.log(nf / max_exact) / math.log(MAX_DISTANCE / max_exact)
                                  * (N_BUCKETS - max_exact)).astype(jnp.int32)
    large = jnp.minimum(large, N_BUCKETS - 1)
    return jnp.where(n < max_exact, n, large)


def _lookup(bucket, valid, table_ref, head):
    bias = jnp.zeros(bucket.shape, F32)
    for b in range(N_BUCKETS):
        bias = jnp.where(bucket == b, table_ref[b, head], bias)
    return jnp.where(valid, bias, NEG_INF)


def _bias_kernel(table_ref, sinks_ref, band_ref, sinkt_ref, past_ref, new_ref, sinkcol_ref):
    hk = pl.program_id(0)
    key_row =lax.broadcasted_iota(jnp.int32, (ATT_KEYS, HALF_Q), 0)
    true_dist = lax.broadcasted_iota(jnp.int32, (ATT_KEYS, HALF_Q), 1) + BLOCK - key_row
    visible = (true_dist >= 0) & (true_dist < WINDOW)

    def band(row):
        full = pltpu.roll(jnp.broadcast_to(row * LOG2_E, (ATT_KEYS, LANES)), 0, axis=1, stride=1,
                          stride_axis=0)
        return jnp.where(visible, full[:, :HALF_Q], NEG_INF * LOG2_E)

    rows = past_ref.shape[2]
    lane = lax.broadcasted_iota(jnp.int32, (SUBLANES, LANES), 1)
    kind = lax.broadcasted_iota(jnp.int32, (SUBLANES, LANES), 0)
    dists = jnp.where(kind == 0, lane, jnp.where(kind == 1, rows - lane, 0))
    buckets = _buckets(dists)
    valid = (dists >= 0) & (dists < WINDOW)
    kvs = past_ref.shape[0]
    for k, g in itertools.product(range(kvs), range(GROUP)):
        head = (hk * kvs + k) * GROUP + g
        looked = _lookup(buckets, valid, table_ref, head)
        bias = band(looked[0:1])
        for half in range(2):
            slot = _head_order(half).index(g)
            cs = slice(slot * HALF_Q, (slot + 1) * HALF_Q)
            band_ref[0, half, k, :, cs] = bias
            prev_rows = BLOCK - half * HALF_Q
            band_ref[1, half, k, :, cs] = jnp.where(key_row < prev_rows, NEG_INF, bias)
            sinkt_ref[half, k, :, cs] = jnp.full((1, HALF_Q), sinks_ref[head] * LOG2_E, F32)
        past_ref[k, g:g + 1, :] = looked[1:2]
        new_ref[k, g:g + 1, :] = looked[2:3]
        sinkcol_ref[k, g:g + 1, :] = jnp.full((1, LANES), sinks_ref[head], F32)


def bias_tables(table, sinks, past_rows):
    assert WINDOW <= LANES and BLOCK % LANES == 0
    assert past_rows == LANES
    kvs = math.gcd(N_KV_HEADS, BIAS_KV_PER_STEP)
    smem = pl.BlockSpec(memory_space=pltpu.SMEM)
    per_head = pl.BlockSpec((kvs, GROUP, LANES), lambda h: (h, 0, 0))
    return pl.pallas_call(
        _bias_kernel,
        grid=(N_KV_HEADS // kvs,),
        in_specs=[smem, smem],
        out_specs=[
            pl.BlockSpec((2, 2, kvs, ATT_KEYS, GROUP * HALF_Q), lambda h: (0, 0, h, 0, 0)),
            pl.BlockSpec((2, kvs, 1, GROUP * HALF_Q), lambda h: (0, h, 0, 0)),
            per_head, per_head, per_head,
        ],
        out_shape=[
            jax.ShapeDtypeStruct((2, 2, N_KV_HEADS, ATT_KEYS, GROUP * HALF_Q), F32),
            jax.ShapeDtypeStruct((2, N_KV_HEADS, 1, GROUP * HALF_Q), F32),
            jax.ShapeDtypeStruct((N_KV_HEADS, GROUP, past_rows), F32),
            jax.ShapeDtypeStruct((N_KV_HEADS, GROUP, LANES), F32),
            jax.ShapeDtypeStruct((N_KV_HEADS, GROUP, LANES), F32),
        ],
        compiler_params=_params("parallel"),
        name="bias_tables",
    )(table, sinks)


def _band_attn_rounds(first_tile, q_ref, kp_ref, kc_ref, vp_ref, vc_ref, gate_ref, bias_ref, sink_ref,
                      y_ref, s_scr, p_scr):
    nt = (((1,), (1,)), ((), ()))
    low = (lax.broadcasted_iota(jnp.int32, (1, LANES), 1) < HEAD_DIM)
    keep_low = low.astype(BF16)
    keep_high = 1 - keep_low
    keep = (keep_low, keep_high)
    nkeys = 2 * BLOCK
    ones_rows = jnp.where(lax.broadcasted_iota(jnp.int32, (2 * SUBLANES, nkeys), 0) == 0,
                          1.0, 0.0).astype(BF16)
    rows = ATT_ROWS
    nslot = s_scr.shape[0]
    items = [(blk, hk, half) for blk in range(q_ref.shape[0] // BLOCK)
             for hk in range(N_KV_HEADS) for half in range(2)]
    assert nslot % 2 == 0

    def rows_of(blk):
        return slice(blk * BLOCK, (blk + 1) * BLOCK)

    def key_rows(half):
        return slice(half * HALF_Q, half * HALF_Q + ATT_KEYS)

    for slot in range(nslot):
        dead = slice(ATT_KEYS, nkeys) if slot % 2 == 0 else slice(0, HALF_Q)
        p_scr[slot, dead, :] = jnp.zeros((HALF_Q, p_scr.shape[2]), BF16)

    def scores(idx):
        blk, hk, half = items[idx]
        variant = first_tile if blk == 0 else 0
        cs = slice(hk * LANES, (hk + 1) * LANES)
        k_prev = kp_ref[:, cs] if blk == 0 else kc_ref[rows_of(blk - 1), cs]
        k_cur = kc_ref[rows_of(blk), cs]
        kd = (jnp.concatenate([k_prev, k_cur[:HALF_Q]], axis=0) if half == 0
              else jnp.concatenate([k_prev[HALF_Q:], k_cur], axis=0))
        q0 = blk * BLOCK + half * HALF_Q
        qs = jnp.concatenate(
            [q_ref[q0:q0 + HALF_Q, (hk * SLABS + h // HEADS_PER_TILE) * LANES:
                   (hk * SLABS + h // HEADS_PER_TILE + 1) * LANES] * keep[h % HEADS_PER_TILE]
             for h in _head_order(half)], axis=0)
        s = lax.dot_general(kd, qs, nt, preferred_element_type=F32) + bias_ref[variant, half, hk]
        s_scr[idx % nslot, key_rows(half), :] = s
        return jnp.maximum(jnp.max(s, axis=0, keepdims=True), sink_ref[half, hk])

    def softmax(idx, m):
        blk, hk, half = items[idx]
        slot = idx % nslot
        lo = half * HALF_Q
        for r in range(lo, lo + ATT_KEYS, rows):
            p_scr[slot, r:r + rows, :] = jnp.exp2(s_scr[slot, r:r + rows, :] - m).astype(BF16)
        return jnp.exp2(sink_ref[half, hk] - m)

    def weighted_values(idx):
        blk, hk, half = items[idx]
        vs = slice(hk * LANES, hk * LANES + HEAD_DIM)
        v_prev = vp_ref[vs, :] if blk == 0 else vc_ref[vs, rows_of(blk - 1)]
        vt = jnp.concatenate([v_prev, vc_ref[vs, rows_of(blk)]], axis=1)
        lhs_v = jnp.concatenate([vt, ones_rows], axis=0)
        return jnp.dot(lhs_v, p_scr[idx % nslot], preferred_element_type=F32)

    low_q = lax.broadcasted_iota(jnp.int32, (HEAD_DIM, LANES), 1) < HALF_Q

    def finish(blk, hk, ots, sink_ws):
        o = [ots[half][:HEAD_DIM] * (1.0 / (ots[half][HEAD_DIM:HEAD_DIM + 1] + sink_ws[half]))
             for half in range(2)]
        for sl in range(SLABS):
            a, b = (oh[:, sl * LANES:(sl + 1) * LANES] for oh in o)
            even = jnp.where(low_q, a, b)
            odd = pltpu.roll(jnp.where(low_q, b, a), HALF_Q, axis=1)
            pair = jnp.concatenate([even, odd], axis=0)
            c0 = (hk * SLABS + sl) * LANES
            y_ref[rows_of(blk), c0:c0 + LANES] = (
                pair.T * _silu(gate_ref[rows_of(blk), c0:c0 + LANES])).astype(y_ref.dtype)

    n = len(items)
    offs, sink_ws, outs = {}, {}, {}
    for k in range(-2 * ATT_SKEW, n + ATT_SKEW):
        if 0 <= k + 2 * ATT_SKEW < n:
            offs[k + 2 * ATT_SKEW] = scores(k + 2 * ATT_SKEW)
        if 0 <= k + ATT_SKEW < n:
            sink_ws[k + ATT_SKEW] = softmax(k + ATT_SKEW, offs.pop(k + ATT_SKEW))
        if 0 <= k < n:
            outs[k] = weighted_values(k)
        j = k - ATT_SKEW
        if 0 <= j < n and items[j][2] == 1:
            finish(items[j][0], items[j][1], [outs.pop(j - 1), outs.pop(j)],
                   [sink_ws.pop(j - 1), sink_ws.pop(j)])
        yield


def _proj_tile(y_scr, w_scr, g_ref, x_ref, o_ref, raw_scr):
    y = y_scr[...]
    for c in range(w_scr.shape[0]):
        for n0 in range(0, W_CHUNK, MXU_COLS):
            raw_scr[:, c * W_CHUNK + n0:c * W_CHUNK + n0 + MXU_COLS] = jnp.dot(
                y, w_scr[c, :, n0:n0 + MXU_COLS], preferred_element_type=F32)
            yield
    step = y_scr.shape[0] // PROJ_TAIL_PIECES
    for r0 in range(0, y_scr.shape[0], step):
        o = raw_scr[r0:r0 + step, :]
        o_ref[r0:r0 + step, :] = x_ref[r0:r0 + step, :] + o * _rms_scale(o) * g_ref[...]
        yield


def _attn_proj_kernel(nchunk, ntiles, tiles_per_seq, as_ref, gs_ref, xs_ref,
                      q_ref, kp_ref, kc_ref, vp_ref, vc_ref, gate_ref, bias_ref, sink_ref,
                      w_ref, g_ref, x_ref, os_ref, o_ref,
                      w_scr, raws_scr, s_scr, p_scr, ynew_scr, yold_scr, raw_scr):
    i = pl.program_id(0)
    p = i - nchunk

    @pl.when(i < nchunk)
    def _():
        wb = w_ref[...].astype(BF16)
        w_scr[i] = wb
        ys = (as_ref[...] * _silu(gs_ref[...])).astype(BF16)
        raws_scr[i] = jnp.dot(ys, wb, preferred_element_type=F32)

    @pl.when(i == nchunk - 1)
    def _():
        o = jnp.concatenate([raws_scr[c] for c in range(nchunk)], axis=1)
        os_ref[:, 0, :] = xs_ref[...] + o * _rms_scale(o) * g_ref[...]

    def attend():
        first = (p % tiles_per_seq == 0).astype(jnp.int32)
        return _band_attn_rounds(first, q_ref, kp_ref, kc_ref, vp_ref, vc_ref, gate_ref, bias_ref,
                                 sink_ref, ynew_scr, s_scr, p_scr)

    def project():
        return _proj_tile(yold_scr, w_scr, g_ref, x_ref, o_ref, raw_scr)

    @pl.when(p == 0)
    def _():
        _interleave((attend(), 1))
        yold_scr[...] = ynew_scr[...]

    @pl.when((p > 0) & (p < ntiles))
    def _():
        _interleave((attend(), ATT_ROUNDS_PER_PROJ_PIECE), (project(), 1))
        yold_scr[...] = ynew_scr[...]

    @pl.when(p == ntiles)
    def _():
        _interleave((project(), 1))


def attn_proj(a_s, gate_s, x_s, q, kdup, vt, gate, bias_band, sink_t, w, g, x, seq_len):
    k, d = w.shape
    m = x.shape[0]
    ns = x_s.shape[0]
    tm = ATT_PROJ_TILE
    nchunk = d // W_CHUNK
    ntiles = m // tm
    tiles_per_seq = seq_len // tm
    per_tile = tm // BLOCK
    assert seq_len % tm == 0 and tm % BLOCK == 0
    att_tile = lambda i: jnp.clip(i - nchunk, 0, ntiles - 1)
    proj_tile = lambda i: jnp.clip(i - nchunk - 1, 0, ntiles - 1)

    def before(i):
        t = att_tile(i)
        return per_tile * t - jnp.where(t % tiles_per_seq == 0, 0, 1)

    rows = lambda n: pl.BlockSpec((tm, n), lambda i: (att_tile(i), 0))
    score_tile = (2 * BLOCK, GROUP * HALF_Q)
    return pl.pallas_call(
        functools.partial(_attn_proj_kernel, nchunk, ntiles, tiles_per_seq),
        grid=(nchunk + ntiles + 1,),
        in_specs=[
            _resident(a_s.shape), _resident(gate_s.shape), _resident(x_s.shape),
            rows(ATT_WIDTH), pl.BlockSpec((BLOCK, 2 * KV_WIDTH), lambda i: (before(i), 0)),
            rows(2 * KV_WIDTH), pl.BlockSpec((2 * KV_WIDTH, BLOCK), lambda i: (0, before(i))),
            pl.BlockSpec((2 * KV_WIDTH, tm), lambda i: (0, att_tile(i))), rows(ATT_WIDTH),
            _resident(bias_band.shape), _resident(sink_t.shape),
            pl.BlockSpec((k, W_CHUNK), lambda i: (0, jnp.minimum(i, nchunk - 1))), _resident((1, d)),
            pl.BlockSpec((tm, d), lambda i: (proj_tile(i), 0)),
        ],
        out_specs=[pl.BlockSpec((ns, 1, d), lambda i: (0, 0, 0)),
                   pl.BlockSpec((tm, d), lambda i: (proj_tile(i), 0))],
        out_shape=[jax.ShapeDtypeStruct((ns, 1, d), F32), jax.ShapeDtypeStruct((m, d), F32)],
        scratch_shapes=[pltpu.VMEM((nchunk, k, W_CHUNK), BF16),
                        pltpu.VMEM((nchunk, x_s.shape[0], W_CHUNK), F32),
                        pltpu.VMEM((ATT_SLOTS,) + score_tile, F32),
                        pltpu.VMEM((ATT_SLOTS,) + score_tile, BF16),
                        pltpu.VMEM((tm, k), BF16), pltpu.VMEM((tm, k), BF16),
                        pltpu.VMEM((tm, d), F32)],
        compiler_params=_params("arbitrary"),
        name="attn_proj",
    )(a_s, gate_s, x_s, q, kdup, kdup, vt, vt, gate, bias_band, sink_t, w, g.reshape(1, d), x)


def _cached_attn_kernel(q_ref, ckt_ref, cvt_ref, kn_ref, vn_ref, sinks_ref, bpast_ref, bnew_ref, o_ref):
    pairs = N_Q_HEADS // 2
    shape = (N_Q_HEADS, KV_WIDTH)
    row = lax.broadcasted_iota(jnp.int32, shape, 0)
    row_head = 2 * (row % pairs) + row // pairs
    own = lax.broadcasted_iota(jnp.int32, shape, 1) // HEAD_DIM == row_head // GROUP
    low = lax.broadcasted_iota(jnp.int32, (pairs, LANES), 1) < HEAD_DIM

    def by_parity(ref):
        return jnp.concatenate([ref[pl.ds(par, pairs, stride=2), :] for par in range(2)], axis=0)

    sink = by_parity(sinks_ref)[:, :1]
    bias_past = by_parity(bpast_ref)
    bias_new = by_parity(bnew_ref)[:, :1]
    nt = (((1,), (1,)), ((), ()))
    seqs = range(q_ref.shape[0])
    qm = []
    for b in seqs:
        pr = jnp.concatenate([q_ref[b:b + 1, r * LANES:(r + 1) * LANES] for r in range(pairs)], axis=0)
        swapped = pltpu.roll(pr, HEAD_DIM, axis=1)
        q2 = jnp.concatenate([jnp.where(low, pr, swapped), jnp.where(low, swapped, pr)], axis=0)
        qt = jnp.concatenate([q2] * (KV_WIDTH // LANES), axis=1)
        qm.append(jnp.where(own, qt, 0.0).astype(BF16))
    s = [jnp.dot(qm[b], ckt_ref[b].astype(BF16), preferred_element_type=F32) + bias_past for b in seqs]
    s_new = [jnp.sum(qm[b].astype(F32) * kn_ref[b:b + 1, :].astype(BF16).astype(F32), axis=-1, keepdims=True)
             + bias_new for b in seqs]
    m = [jnp.maximum(jnp.maximum(jnp.max(s[b], axis=-1, keepdims=True), s_new[b]), sink) for b in seqs]
    p = [jnp.exp(s[b] - m[b]) for b in seqs]
    p_new = [jnp.exp(s_new[b] - m[b]) for b in seqs]
    denom = [jnp.sum(p[b], axis=-1, keepdims=True) + p_new[b] + jnp.exp(sink - m[b]) for b in seqs]
    pv = [lax.dot_general(p[b].astype(BF16), cvt_ref[b].astype(BF16), nt, preferred_element_type=F32)
          for b in seqs]
    for b in seqs:
        o_all = pv[b] + p_new[b].astype(BF16).astype(F32) * vn_ref[b:b + 1, :].astype(BF16).astype(F32)
        o_all = jnp.where(own, o_all, 0.0)
        o = o_all[:, :LANES]
        for k in range(1, KV_WIDTH // LANES):
            o = o + o_all[:, k * LANES:(k + 1) * LANES]
        o = o / denom[b]
        o = o + pltpu.roll(o, HEAD_DIM, axis=1)
        pr = jnp.where(low, o[:pairs], o[pairs:])
        for r in range(pairs):
            o_ref[b:b + 1, r * LANES:(r + 1) * LANES] = pr[r:r + 1, :]


def cached_attention(q, cache_kt, cache_vt, k_new, v_new, sinks, bias_past, bias_new):
    bsz, _, rows = cache_kt.shape
    assert LANES == 2 * HEAD_DIM and rows == LANES and N_Q_HEADS % 2 == 0
    nseq = math.gcd(bsz, SEQS_PER_STEP)
    cache = pl.BlockSpec((nseq, KV_WIDTH, rows), lambda b: (b, 0, 0))
    seq_rows = lambda n: pl.BlockSpec((nseq, n), lambda b: (b, 0))
    return pl.pallas_call(
        _cached_attn_kernel,
        grid=(bsz // nseq,),
        in_specs=[
            seq_rows(ATT_WIDTH), cache, cache, seq_rows(KV_WIDTH), seq_rows(KV_WIDTH),
            _resident((N_Q_HEADS, LANES)), _resident((N_Q_HEADS, rows)), _resident((N_Q_HEADS, LANES)),
        ],
        out_specs=seq_rows(ATT_WIDTH),
        out_shape=jax.ShapeDtypeStruct((bsz, ATT_WIDTH), F32),
        compiler_params=_params("parallel"),
        name="cached_attention",
    )(q, cache_kt, cache_vt, k_new, v_new, sinks, bias_past, bias_new)


def kernel(x_prompt, x_sample, state_conv, state_h, cache_k, cache_v, a_norm_pre, a_norm_post,
           a_w_in, a_conv_w, a_conv_b, a_w_r, a_b_r, a_w_i, a_b_i, a_lambda, a_w_out, kv_norm, w_kv,
           b_norm_pre, b_norm_post, b_w_qg, b_sinks, b_w_out, rel_bias_table):
    bsz, t, d = x_prompt.shape
    dbsz, dt, _ = x_sample.shape
    assert a_w_in.shape[0] == 1 and b_w_qg.shape[0] == 1 and dt == 1
    assert t % BLOCK == 0 and t >= WINDOW
    past_rows = cache_k.shape[1]
    assert past_rows == min(WINDOW, PAST_LEN)

    sinks = b_sinks[0]
    bias_band, sink_t, bias_past, bias_new, sink_col = bias_tables(rel_bias_table, sinks, past_rows)

    tm = 2 * SUB_ROWS
    xp = x_prompt.reshape(bsz * t, d)
    xs = x_sample

    gate_s, hs, s_conv_t, y_even, y_odd, p_conv, p_h = rglru_front(
        xs, jnp.transpose(state_conv[0], (1, 0, 2)), state_h[0], xp, a_norm_pre[0], a_w_in[0],
        a_conv_w[0], a_conv_b[0], a_w_r[0], a_b_r[0], a_w_i[0], a_b_i[0], a_lambda[0], seq_len=t)
    xs1, x1 = proj_norm_res(hs, gate_s, xs, (y_even, y_odd), a_w_out[0], a_norm_post[0], xp, tm)

    ks, vs, qs, gate_sb, q, gate_b, kdup, vt, k_tail, v_tail = norm_proj_kvq(
        xs1, x1, kv_norm, b_norm_pre[0], w_kv, b_w_qg[0], SUB_ROWS, seq_len=t)
    cache_kt = jnp.transpose(cache_k, (0, 2, 3, 1)).reshape(dbsz, KV_WIDTH, past_rows)
    cache_vt = jnp.transpose(cache_v, (0, 2, 3, 1)).reshape(dbsz, KV_WIDTH, past_rows)
    os_ = cached_attention(qs, cache_kt, cache_vt,
                           ks, vs,
                           sink_col.reshape(N_Q_HEADS, LANES), bias_past.reshape(N_Q_HEADS, past_rows),
                           bias_new.reshape(N_Q_HEADS, LANES))
    y_sample, y_prompt = attn_proj(os_, gate_sb, xs1, q, kdup, vt, gate_b,
                                   bias_band, sink_t, b_w_out[0], b_norm_post[0], x1, seq_len=t)
    y_prompt = y_prompt.reshape(bsz, t, d)
    p_k = jnp.transpose(k_tail.reshape(bsz, N_KV_HEADS, HEAD_DIM, WINDOW), (0, 3, 1, 2))
    p_v = jnp.transpose(v_tail.reshape(bsz, N_KV_HEADS, HEAD_DIM, WINDOW), (0, 3, 1, 2))

    return (y_prompt, y_sample,
            jnp.transpose(p_conv, (1, 0, 2))[None], p_h[None], p_k, p_v,
            jnp.transpose(s_conv_t, (1, 0, 2))[None], hs[None],
            ks.reshape(dbsz, 1, N_KV_HEADS, HEAD_DIM), vs.reshape(dbsz, 1, N_KV_HEADS, HEAD_DIM))
```

```python
import functools
import itertools
import math

import jax
import jax.numpy as jnp
from jax import lax
from jax.experimental import pallas as pl
from jax.experimental.pallas import tpu as pltpu

F32 = jnp.float32
BF16 = jnp.bfloat16

D_MODEL = 2048
LRU_WIDTH = 2048
LRU_BLOCKS = 8
LRU_BLOCK_W = LRU_WIDTH // LRU_BLOCKS
CONV_W = 4
LRU_C = 8.0
HEAD_DIM = 64
N_Q_HEADS = 32
N_KV_HEADS = 8
GROUP = N_Q_HEADS // N_KV_HEADS
ATT_WIDTH = N_Q_HEADS * HEAD_DIM
KV_WIDTH = N_KV_HEADS * HEAD_DIM
WINDOW = 128
BLOCK = WINDOW
N_BUCKETS = 32
MAX_DISTANCE = 128
RMS_EPS = 1e-6
NEG_INF = -1e30
LOG2_E = 1.4426950408889634
PAST_LEN = 16384

V7X_VMEM_BYTES = 64 * 1024 * 1024
VMEM_LIMIT = V7X_VMEM_BYTES - 8 * 1024 * 1024
SUBLANES = 8
LANES = 128
HEADS_PER_TILE = LANES // HEAD_DIM
SLABS = GROUP // HEADS_PER_TILE
MXU_COLS = 256
SUB_ROWS = 256
W_CHUNK = 512
ATT_ROWS = 64
ATT_SKEW = 2
ATT_SLOTS = 2 * ATT_SKEW + 2
HALF_Q = BLOCK // 2
ATT_KEYS = WINDOW + HALF_Q
SEQS_PER_STEP = 8
BIAS_KV_PER_STEP = 2
SCAN_PIECES_PER_PROJ_PIECE = 2
ATT_PROJ_TILE = 2 * BLOCK
PROJ_TAIL_PIECES = 4
ATT_ROUNDS_PER_PROJ_PIECE = 4
SQRT_FLOOR = 1e-30


def _head_order(half):
    heads = list(range(GROUP))
    return heads if half == 0 else [h ^ 1 for h in heads]


def _params(*semantics):
    return pltpu.CompilerParams(dimension_semantics=semantics, vmem_limit_bytes=VMEM_LIMIT)


def _resident(shape):
    zeros = (0,) * len(shape)
    return pl.BlockSpec(shape, lambda *_: zeros, pipeline_mode=pl.Buffered(1))


def _rms_scale(x):
    return lax.rsqrt(jnp.mean(x * x, axis=-1, keepdims=True) + RMS_EPS)


def _silu(x):
    h = 0.5 * x
    return h * jnp.tanh(h) + h


def _segment_major(rows, inverse=False):
    seg = rows // SUBLANES
    r = lax.broadcasted_iota(jnp.int32, (rows, rows), 0)
    c = lax.broadcasted_iota(jnp.int32, (rows, rows), 1)
    if inverse:
        src = (r % seg) * SUBLANES + r // seg
    else:
        src = (r % SUBLANES) * seg + r // SUBLANES
    return jnp.where(c == src, 1.0, 0.0).astype(BF16)


def _phase_specs(nchunk, tm, k):
    chunk_w = pl.BlockSpec((k, W_CHUNK), lambda i: (0, jnp.minimum(i, nchunk - 1)))
    tile = lambda n: pl.BlockSpec((tm, n), lambda i: (jnp.maximum(i - nchunk, 0), 0))
    return chunk_w, tile


def _dup_heads(x):
    low = lax.broadcasted_iota(jnp.int32, (x.shape[0], LANES), 1) < HEAD_DIM
    out = []
    for c in range(x.shape[1] // LANES):
        col = x[:, c * LANES:(c + 1) * LANES]
        swapped = pltpu.roll(col, HEAD_DIM, axis=1)
        out += [jnp.where(low, col, swapped), jnp.where(low, swapped, col)]
    return jnp.concatenate(out, axis=1)


def _norm_proj_kvq_kernel(nkv, nqg, xs_ref, x_ref, gkv_ref, gq_ref, wkv_ref, wqg_ref,
                          ks_ref, vs_ref, qs_ref, gates_ref,
                          q_ref, gate_ref, kdup_ref, vt_ref, ktail_ref, vtail_ref, ks4_ref, vs4_ref,
                          wkv_scr, wqg_scr):
    i = pl.program_id(0)
    nchunk = nkv + nqg
    q_chunks = ATT_WIDTH // W_CHUNK
    q_scale = 1.0 / math.sqrt(HEAD_DIM)
    q_scale_log2 = q_scale * LOG2_E

    def sample_rows(g_ref):
        xs = xs_ref[...]
        return (xs * _rms_scale(xs) * g_ref[...]).astype(BF16)

    @pl.when(i < nkv)
    def _():
        wb = wkv_ref[...].astype(BF16)
        wkv_scr[i] = wb
        r = jnp.dot(sample_rows(gkv_ref), wb, preferred_element_type=F32)

        def store(flat_ref, heads_ref):
            flat_ref[...] = r
            for g in range(N_KV_HEADS):
                heads_ref[:, 0, g, :] = r[:, g * HEAD_DIM:(g + 1) * HEAD_DIM]

        @pl.when(i == 0)
        def _():
            store(ks_ref, ks4_ref)

        @pl.when(i == 1)
        def _():
            store(vs_ref, vs4_ref)

    @pl.when((i >= nkv) & (i < nchunk))
    def _():
        c = i - nkv
        wb = wqg_ref[...].astype(BF16)
        wqg_scr[c] = wb
        r = jnp.dot(sample_rows(gq_ref), wb, preferred_element_type=F32)

        @pl.when(c < q_chunks)
        def _():
            qs_ref[...] = r * q_scale

        @pl.when(c >= q_chunks)
        def _():
            gates_ref[...] = r

    @pl.when(i >= nchunk)
    def _():
        tm = x_ref.shape[0]
        for rs in _row_blocks(tm):
            x = x_ref[rs, :]
            xh = x * _rms_scale(x)
            xkv = (xh * gkv_ref[...]).astype(BF16)
            xq = (xh * gq_ref[...]).astype(BF16)
            k = jnp.dot(xkv, wkv_scr[0], preferred_element_type=F32)
            v = jnp.dot(xkv, wkv_scr[1], preferred_element_type=F32)
            kdup_ref[rs, :] = _dup_heads(k).astype(BF16)
            vt_ref[:, rs] = _dup_heads(v).T.astype(BF16)
            for c in range(nqg):
                r = jnp.dot(xq, wqg_scr[c], preferred_element_type=F32)
                if c < q_chunks:
                    q_ref[rs, c * W_CHUNK:(c + 1) * W_CHUNK] = (r * q_scale_log2).astype(q_ref.dtype)
                else:
                    cc = c - q_chunks
                    gate_ref[rs, cc * W_CHUNK:(cc + 1) * W_CHUNK] = r
        ktail_ref[0] = k[k.shape[0] - WINDOW:].T
        vtail_ref[0] = v[v.shape[0] - WINDOW:].T


def norm_proj_kvq(xs, x, g_kv, g_q, w_kv, w_qg, tm, seq_len):
    m, d = x.shape
    ns = xs.shape[0]
    assert w_kv.shape[1] == 2 * KV_WIDTH == 2 * W_CHUNK and seq_len % tm == 0 and tm >= WINDOW
    nkv, nqg = w_kv.shape[1] // W_CHUNK, w_qg.shape[1] // W_CHUNK
    nchunk = nkv + nqg
    tiles = seq_len // tm
    tile = lambda n: pl.BlockSpec((tm, n), lambda i: (jnp.maximum(i - nchunk, 0), 0))
    tail = pl.BlockSpec((1, KV_WIDTH, WINDOW), lambda i: (jnp.maximum(i - nchunk, 0) // tiles, 0, 0))
    kv_chunk = lambda i: (0, jnp.minimum(i, nkv - 1))
    qg_chunk = lambda i: (0, jnp.clip(i - nkv, 0, nqg - 1))
    q_chunks = ATT_WIDTH // W_CHUNK
    whole_s = lambda n: pl.BlockSpec((ns, n), lambda i: (0, 0))
    new_heads = pl.BlockSpec((ns, 1, N_KV_HEADS, HEAD_DIM), lambda i: (0, 0, 0, 0))
    return pl.pallas_call(
        functools.partial(_norm_proj_kvq_kernel, nkv, nqg),
        grid=(nchunk + m // tm,),
        in_specs=[
            _resident(xs.shape), tile(d), _resident((1, d)), _resident((1, d)),
            pl.BlockSpec((d, W_CHUNK), kv_chunk), pl.BlockSpec((d, W_CHUNK), qg_chunk),
        ],
        out_specs=[
            whole_s(KV_WIDTH), whole_s(KV_WIDTH),
            pl.BlockSpec((ns, W_CHUNK), lambda i: (0, jnp.clip(i - nkv, 0, q_chunks - 1))),
            pl.BlockSpec((ns, W_CHUNK), lambda i: (0, jnp.clip(i - nkv - q_chunks, 0, nqg - q_chunks - 1))),
            tile(ATT_WIDTH), tile(ATT_WIDTH), tile(2 * KV_WIDTH),
            pl.BlockSpec((2 * KV_WIDTH, tm), lambda i: (0, jnp.maximum(i - nchunk, 0))), tail, tail,
            new_heads, new_heads,
        ],
        out_shape=[
            jax.ShapeDtypeStruct((ns, KV_WIDTH), F32),
            jax.ShapeDtypeStruct((ns, KV_WIDTH), F32),
            jax.ShapeDtypeStruct((ns, ATT_WIDTH), F32),
            jax.ShapeDtypeStruct((ns, w_qg.shape[1] - ATT_WIDTH), F32),
            jax.ShapeDtypeStruct((m, ATT_WIDTH), BF16),
            jax.ShapeDtypeStruct((m, ATT_WIDTH), F32),
            jax.ShapeDtypeStruct((m, 2 * KV_WIDTH), BF16),
            jax.ShapeDtypeStruct((2 * KV_WIDTH, m), BF16),
            jax.ShapeDtypeStruct((m // seq_len, KV_WIDTH, WINDOW), F32),
            jax.ShapeDtypeStruct((m // seq_len, KV_WIDTH, WINDOW), F32),
            jax.ShapeDtypeStruct((ns, 1, N_KV_HEADS, HEAD_DIM), F32),
            jax.ShapeDtypeStruct((ns, 1, N_KV_HEADS, HEAD_DIM), F32),
        ],
        scratch_shapes=[pltpu.VMEM((nkv, d, W_CHUNK), BF16), pltpu.VMEM((nqg, d, W_CHUNK), BF16)],
        compiler_params=_params("arbitrary"),
        name="norm_proj_kvq",
    )(xs, x, g_kv.reshape(1, d), g_q.reshape(1, d), w_kv, w_qg)


def _row_blocks(rows):
    sub = min(rows, SUB_ROWS)
    return [slice(r, r + sub) for r in range(0, rows, sub)]


def _proj_norm_res_kernel(nchunk, nparts, as_ref, gs_ref, xs_ref, *refs):
    y_refs = refs[:nparts]
    w_ref, g_ref, x_ref, os_ref, o_ref, w_scr, raw_scr = refs[nparts:]
    i = pl.program_id(0)

    @pl.when(i < nchunk)
    def _():
        wb = w_ref[...].astype(BF16)
        w_scr[i] = wb
        ys = (as_ref[...] * _silu(gs_ref[...])).astype(BF16)
        raw_scr[i] = jnp.dot(ys, wb, preferred_element_type=F32)

    @pl.when(i == nchunk - 1)
    def _():
        o = jnp.concatenate([raw_scr[c] for c in range(nchunk)], axis=1)
        os_ref[...] = xs_ref[:, 0, :] + o * _rms_scale(o) * g_ref[...]

    @pl.when(i >= nchunk)
    def _():
        tm = x_ref.shape[0]
        part_rows = tm // nparts

        for rs in _row_blocks(tm):
            part, off = divmod(rs.start, part_rows)
            y = y_refs[part][off:off + rs.stop - rs.start, :]
            o = jnp.concatenate([jnp.dot(y, w_scr[c], preferred_element_type=F32)
                                 for c in range(nchunk)], axis=1)
            o_ref[rs, :] = x_ref[rs, :] + o * _rms_scale(o) * g_ref[...]


def proj_norm_res(a_s, gate_s, x_s, y_parts, w, g, x, tm):
    k, d = w.shape
    m = x.shape[0]
    ns = x_s.shape[0]
    nchunk = d // W_CHUNK
    nparts = len(y_parts)
    assert (tm // nparts) % min(tm, SUB_ROWS) == 0
    chunk_w, tile = _phase_specs(nchunk, tm, k)
    part = pl.BlockSpec((tm // nparts, k), lambda i: (jnp.maximum(i - nchunk, 0), 0))
    return pl.pallas_call(
        functools.partial(_proj_norm_res_kernel, nchunk, nparts),
        grid=(nchunk + m // tm,),
        in_specs=[_resident(a_s.shape), _resident(gate_s.shape), _resident(x_s.shape)]
        + [part] * nparts + [chunk_w, _resident((1, d)), tile(d)],
        out_specs=[pl.BlockSpec((ns, d), lambda i: (0, 0)), tile(d)],
        out_shape=[jax.ShapeDtypeStruct((ns, d), F32), jax.ShapeDtypeStruct((m, d), F32)],
        scratch_shapes=[pltpu.VMEM((nchunk, k, W_CHUNK), BF16),
                        pltpu.VMEM((nchunk, ns, W_CHUNK), F32)],
        compiler_params=_params("arbitrary"),
        name="proj_norm_res",
    )(a_s, gate_s, x_s, *y_parts, w, g.reshape(1, d), x)


def _lru_gate_dots(conv, wr_half, wi_half):
    cb = conv.astype(BF16)
    return (jnp.dot(cb, wr_half, preferred_element_type=F32),
            jnp.dot(cb, wi_half, preferred_element_type=F32))


def _lru_gates(conv, wr_half, br, wi_half, bi, lam):
    return _lru_gate_math(conv, _lru_gate_dots(conv, wr_half, wi_half), br, bi, lam)


def _lru_gate_math(conv, half_pre, br, bi, lam):
    th_r = jnp.tanh(half_pre[0] + 0.5 * br)
    th_i = jnp.tanh(half_pre[1] + 0.5 * bi)
    nl = -lam
    softplus = jnp.maximum(nl, 0.0) + jnp.log1p(jnp.exp(-jnp.abs(nl)))
    half = (0.5 * LRU_C) * softplus
    x = th_r * half + half
    a = jnp.exp2(x * -LOG2_E)
    z = jnp.tanh(x) * (a * a + 1.0)
    mult = z * lax.rsqrt(jnp.maximum(z, SQRT_FLOOR))
    hc = 0.5 * conv
    return a, mult * (hc * th_i + hc)


def _interleave(*stages):
    live = [[stage, share] for stage, share in stages]
    while live:
        for entry in list(live):
            try:
                for _ in range(entry[1]):
                    next(entry[0])
            except StopIteration:
                live.remove(entry)


def _in_proj_tile(x_ref, rs, g_ref, w_scr, ug_ref):
    x = x_ref[rs, :]
    xn = (x * _rms_scale(x) * g_ref[...]).astype(BF16)
    xn = jnp.dot(_segment_major(xn.shape[0]), xn, preferred_element_type=F32).astype(BF16)
    for c in range(w_scr.shape[0]):
        for n0 in range(0, W_CHUNK, MXU_COLS):
            ug_ref[:, c * W_CHUNK + n0:c * W_CHUNK + n0 + MXU_COLS] = jnp.dot(
                xn, w_scr[c, :, n0:n0 + MXU_COLS], preferred_element_type=F32)
            yield


def _rglru_tile(ug_ref, y_ref, cw_ref, cb_ref, wr_ref, br_ref, wi_ref, bi_ref, lam_ref, h_scr, tail_scr):
    tc = ug_ref.shape[0]
    seg = tc // SUBLANES
    ntaps = CONV_W - 1
    bw = LRU_BLOCK_W
    sub = lax.broadcasted_iota(jnp.int32, (SUBLANES, bw), 0)
    first = sub == 0
    time_order = _segment_major(tc, inverse=True)

    def shift_in(x, row0):
        return jnp.where(first, row0, pltpu.roll(x, 1, axis=0))

    def group(x, j):
        return x[j * SUBLANES:(j + 1) * SUBLANES]

    def store_time_order(cols, y):
        y_ref[:, cols] = jnp.dot(time_order, y, preferred_element_type=F32).astype(y_ref.dtype)

    pending = None
    for n in range(LRU_BLOCKS):
        cs = slice(n * bw, (n + 1) * bw)
        u = ug_ref[:, cs]
        tail = tail_scr[:, cs]
        before = [shift_in(group(u, seg - m), tail[ntaps - m:ntaps - m + 1])
                  for m in range(ntaps, 0, -1)]
        ext = jnp.concatenate(before + [u], axis=0)
        tail_scr[:, cs] = jnp.concatenate(
            [group(u, seg - m)[SUBLANES - 1:] for m in range(ntaps, 0, -1)], axis=0)
        cw = cw_ref[:, cs]
        conv = cb_ref[:, cs]
        for tap in range(CONV_W):
            conv = conv + ext[tap * SUBLANES:tap * SUBLANES + tc] * cw[tap:tap + 1]
        yield

        half_pre = _lru_gate_dots(conv, wr_ref[n], wi_ref[n])
        yield

        if pending is not None:
            store_time_order(*pending)
        yield

        a, b = _lru_gate_math(conv, half_pre, br_ref[:, cs], bi_ref[:, cs], lam_ref[:, cs])

        h = b[:SUBLANES]
        acc = a[:SUBLANES]
        h_loc, a_cum = [h], [acc]
        for j in range(1, seg):
            sl = slice(j * SUBLANES, (j + 1) * SUBLANES)
            h = a[sl] * h + b[sl]
            acc = a[sl] * acc
            h_loc.append(h)
            a_cum.append(acc)

        step = 1
        while step < SUBLANES:
            keep = sub >= step
            h = jnp.where(keep, acc * pltpu.roll(h, step, axis=0) + h, h)
            acc = jnp.where(keep, acc * pltpu.roll(acc, step, axis=0), acc)
            step *= 2
        h_prev = h_scr[:, cs]
        after = h + acc * h_prev
        h_in = shift_in(after, h_prev)
        h_scr[:, cs] = after[SUBLANES - 1:]

        hs = jnp.concatenate([h_loc[j] + a_cum[j] * h_in for j in range(seg)], axis=0)
        y = (hs * _silu(ug_ref[:, LRU_WIDTH + n * bw:LRU_WIDTH + (n + 1) * bw])).astype(BF16)
        pending = (cs, y)
        yield

    store_time_order(*pending)
    yield


def _rglru_front_kernel(nchunk, npairs, chunks, xs_ref, x_ref, g_ref, w_ref,
                        scprev_ref, sh0_ref, cw_ref, cb_ref, wr_ref, br_ref, wi_ref, bi_ref, lam_ref,
                        gs_ref, hs_ref, scnew_ref, y_even_ref, y_odd_ref, cnew_ref, hlast_ref,
                        w_scr, wr_scr, wi_scr, us_scr, ug0_scr, ug1_scr, h_scr, tail_scr):
    i = pl.program_id(0)
    p = i - nchunk
    tc = SUB_ROWS
    half = nchunk // 2
    lru = (cw_ref, cb_ref, wr_scr, br_ref, wi_scr, bi_ref, lam_ref, h_scr, tail_scr)

    @pl.when(i < nchunk)
    def _():
        wb = w_ref[...].astype(BF16)
        w_scr[i] = wb
        xs = xs_ref[:, 0, :]
        xsn = (xs * _rms_scale(xs) * g_ref[...]).astype(BF16)
        r = jnp.dot(xsn, wb, preferred_element_type=F32)
        gs_ref[...] = r

        @pl.when(i < half)
        def _():
            us_scr[i] = r

    @pl.when(i == nchunk - 1)
    def _():
        wr_scr[...] = (0.5 * wr_ref[...]).astype(BF16)
        wi_scr[...] = (0.5 * wi_ref[...]).astype(BF16)
        bw = LRU_BLOCK_W
        for n in range(LRU_BLOCKS):
            cs = slice(n * bw, (n + 1) * bw)
            c, off = divmod(n * bw, W_CHUNK)
            u = us_scr[c, :, off:off + bw]
            cw = cw_ref[:, cs]
            conv = cb_ref[:, cs]
            for tap in range(CONV_W - 1):
                conv = conv + scprev_ref[tap, :, cs] * cw[tap:tap + 1]
                if tap > 0:
                    scnew_ref[tap - 1, :, cs] = scprev_ref[tap, :, cs]
            conv = conv + u * cw[CONV_W - 1:]
            scnew_ref[CONV_W - 2, :, cs] = u
            a, b = _lru_gates(conv, wr_scr[n], br_ref[:, cs], wi_scr[n], bi_ref[:, cs], lam_ref[:, cs])
            hs_ref[:, cs] = a * sh0_ref[:, cs] + b

    def project_even():
        return _in_proj_tile(x_ref, slice(0, tc), g_ref, w_scr, ug0_scr)

    def scan_odd():
        return _rglru_tile(ug1_scr, y_odd_ref, *lru)

    @pl.when(p == 0)
    def _():
        _interleave((project_even(), 1))

    @pl.when((p > 0) & (p < npairs))
    def _():
        _interleave((project_even(), 1), (scan_odd(), SCAN_PIECES_PER_PROJ_PIECE))

    @pl.when(p == npairs)
    def _():
        _interleave((scan_odd(), 1))

    @pl.when(p > 0)
    def _():
        b = (2 * p - 1) // chunks
        hlast_ref[pl.ds(b, 1), :] = h_scr[...]
        for tap in range(CONV_W - 1):
            cnew_ref[tap, pl.ds(b, 1), :] = tail_scr[tap:tap + 1, :]

    @pl.when((p >= 0) & (p < npairs))
    def _():
        @pl.when((2 * p) % chunks == 0)
        def _():
            h_scr[...] = jnp.zeros(h_scr.shape, F32)
            tail_scr[...] = jnp.zeros(tail_scr.shape, F32)

        _interleave((_in_proj_tile(x_ref, slice(tc, 2 * tc), g_ref, w_scr, ug1_scr), 1),
                    (_rglru_tile(ug0_scr, y_even_ref, *lru), SCAN_PIECES_PER_PROJ_PIECE))


def rglru_front(xs, s_conv_prev, s_h0, x, g, w_in, conv_w, conv_b, w_r, b_r, w_i, b_i, lam,
                seq_len):
    m, d = x.shape
    ns = xs.shape[0]
    w = w_in.shape[1] // 2
    tc = SUB_ROWS
    nchunk = w_in.shape[1] // W_CHUNK
    half = nchunk // 2
    bsz = m // seq_len
    chunks = seq_len // tc
    npairs = m // (2 * tc)
    assert seq_len % (2 * tc) == 0 and tc % (SUBLANES * SUBLANES) == 0 and tc // SUBLANES > CONV_W
    pair = lambda i: jnp.clip(i - nchunk, 0, npairs - 1)
    chunk =lambda i: (0, jnp.minimum(i, nchunk - 1))
    return pl.pallas_call(
        functools.partial(_rglru_front_kernel, nchunk, npairs, chunks),
        grid=(nchunk + npairs + 1,),
        in_specs=[_resident(xs.shape), pl.BlockSpec((2 * tc, d), lambda i: (pair(i), 0)), _resident((1, d)),
                  pl.BlockSpec((d, W_CHUNK), chunk),
                  _resident(s_conv_prev.shape), _resident(s_h0.shape),
                  _resident((CONV_W, w)), _resident((1, w)), _resident(w_r.shape), _resident((1, w)),
                  _resident(w_i.shape), _resident((1, w)), _resident((1, w))],
        out_specs=[
            pl.BlockSpec((ns, W_CHUNK), lambda i: (0, jnp.clip(i - half, 0, half - 1))),
            pl.BlockSpec((ns, w), lambda i: (0, 0)),
            pl.BlockSpec(s_conv_prev.shape, lambda i: (0, 0, 0)),
            pl.BlockSpec((tc, w), lambda i: (pair(i), 0)),
            pl.BlockSpec((tc, w), lambda i: (jnp.clip(i - nchunk - 1, 0, npairs - 1), 0)),
            pl.BlockSpec((CONV_W - 1, bsz, w), lambda i: (0, 0, 0)), pl.BlockSpec((bsz, w), lambda i: (0, 0)),
        ],
        out_shape=[
            jax.ShapeDtypeStruct((ns, w), F32),
            jax.ShapeDtypeStruct((ns, w), F32),
            jax.ShapeDtypeStruct(s_conv_prev.shape, F32),
            jax.ShapeDtypeStruct((m // 2, w), BF16),
            jax.ShapeDtypeStruct((m // 2, w), BF16),
            jax.ShapeDtypeStruct((CONV_W - 1, bsz, w), F32),
            jax.ShapeDtypeStruct((bsz, w), F32),
        ],
        scratch_shapes=[pltpu.VMEM((nchunk, d, W_CHUNK), BF16),
                        pltpu.VMEM(w_r.shape, BF16), pltpu.VMEM(w_i.shape, BF16),
                        pltpu.VMEM((half, ns, W_CHUNK), F32),
                        pltpu.VMEM((tc, 2 * w), F32), pltpu.VMEM((tc, 2 * w), F32),
                        pltpu.VMEM((1, w), F32), pltpu.VMEM((CONV_W - 1, w), F32)],
        compiler_params=_params("arbitrary"),
        name="rglru_front",
    )(xs, x, g.reshape(1, d), w_in, s_conv_prev, s_h0,
      conv_w, conv_b.reshape(1, w), w_r, b_r.reshape(1, w), w_i, b_i.reshape(1, w), lam.reshape(1, w))


def _buckets(dist):
    n = jnp.maximum(dist, 0)
    max_exact = N_BUCKETS // 2
    nf = jnp.maximum(n, 1).astype(F32)
    large = max_exact + jnp.floor(jnp.log(nf / max_exact) / math.log(MAX_DISTANCE / max_exact)
                                  * (N_BUCKETS - max_exact)).astype(jnp.int32)
    large = jnp.minimum(large, N_BUCKETS - 1)
    return jnp.where(n < max_exact, n, large)


def _lookup(bucket, valid, table_ref, head):
    bias = jnp.zeros(bucket.shape, F32)
    for b in range(N_BUCKETS):
        bias = jnp.where(bucket == b, table_ref[b, head], bias)
    return jnp.where(valid, bias, NEG_INF)


def _bias_kernel(table_ref, sinks_ref, band_ref, sinkt_ref, past_ref, new_ref, sinkcol_ref):
    hk = pl.program_id(0)
    key_row =lax.broadcasted_iota(jnp.int32, (ATT_KEYS, HALF_Q), 0)
    true_dist = lax.broadcasted_iota(jnp.int32, (ATT_KEYS, HALF_Q), 1) + BLOCK - key_row
    visible = (true_dist >= 0) & (true_dist < WINDOW)

    def band(row):
        full = pltpu.roll(jnp.broadcast_to(row * LOG2_E, (ATT_KEYS, LANES)), 0, axis=1, stride=1,
                          stride_axis=0)
        return jnp.where(visible, full[:, :HALF_Q], NEG_INF * LOG2_E)

    rows = past_ref.shape[2]
    lane = lax.broadcasted_iota(jnp.int32, (SUBLANES, LANES), 1)
    kind = lax.broadcasted_iota(jnp.int32, (SUBLANES, LANES), 0)
    dists = jnp.where(kind == 0, lane, jnp.where(kind == 1, rows - lane, 0))
    buckets = _buckets(dists)
    valid = (dists >= 0) & (dists < WINDOW)
    kvs = past_ref.shape[0]
    for k, g in itertools.product(range(kvs), range(GROUP)):
        head = (hk * kvs + k) * GROUP + g
        looked = _lookup(buckets, valid, table_ref, head)
        bias = band(looked[0:1])
        for half in range(2):
            slot = _head_order(half).index(g)
            cs = slice(slot * HALF_Q, (slot + 1) * HALF_Q)
            band_ref[0, half, k, :, cs] = bias
            prev_rows = BLOCK - half * HALF_Q
            band_ref[1, half, k, :, cs] = jnp.where(key_row < prev_rows, NEG_INF, bias)
            sinkt_ref[half, k, :, cs] = jnp.full((1, HALF_Q), sinks_ref[head] * LOG2_E, F32)
        past_ref[k, g:g + 1, :] = looked[1:2]
        new_ref[k, g:g + 1, :] = looked[2:3]
        sinkcol_ref[k, g:g + 1, :] = jnp.full((1, LANES), sinks_ref[head], F32)


def bias_tables(table, sinks, past_rows):
    assert WINDOW <= LANES and BLOCK % LANES == 0
    assert past_rows == LANES
    kvs = math.gcd(N_KV_HEADS, BIAS_KV_PER_STEP)
    smem = pl.BlockSpec(memory_space=pltpu.SMEM)
    per_head = pl.BlockSpec((kvs, GROUP, LANES), lambda h: (h, 0, 0))
    return pl.pallas_call(
        _bias_kernel,
        grid=(N_KV_HEADS // kvs,),
        in_specs=[smem, smem],
        out_specs=[
            pl.BlockSpec((2, 2, kvs, ATT_KEYS, GROUP * HALF_Q), lambda h: (0, 0, h, 0, 0)),
            pl.BlockSpec((2, kvs, 1, GROUP * HALF_Q), lambda h: (0, h, 0, 0)),
            per_head, per_head, per_head,
        ],
        out_shape=[
            jax.ShapeDtypeStruct((2, 2, N_KV_HEADS, ATT_KEYS, GROUP * HALF_Q), F32),
            jax.ShapeDtypeStruct((2, N_KV_HEADS, 1, GROUP * HALF_Q), F32),
            jax.ShapeDtypeStruct((N_KV_HEADS, GROUP, past_rows), F32),
            jax.ShapeDtypeStruct((N_KV_HEADS, GROUP, LANES), F32),
            jax.ShapeDtypeStruct((N_KV_HEADS, GROUP, LANES), F32),
        ],
        compiler_params=_params("parallel"),
        name="bias_tables",
    )(table, sinks)


def _band_attn_rounds(first_tile, q_ref, kp_ref, kc_ref, vp_ref, vc_ref, gate_ref, bias_ref, sink_ref,
                      y_ref, s_scr, p_scr):
    nt = (((1,), (1,)), ((), ()))
    low = (lax.broadcasted_iota(jnp.int32, (1, LANES), 1) < HEAD_DIM)
    keep_low = low.astype(BF16)
    keep_high = 1 - keep_low
    keep = (keep_low, keep_high)
    nkeys = 2 * BLOCK
    ones_rows = jnp.where(lax.broadcasted_iota(jnp.int32, (2 * SUBLANES, nkeys), 0) == 0,
                          1.0, 0.0).astype(BF16)
    rows = ATT_ROWS
    nslot = s_scr.shape[0]
    items = [(blk, hk, half) for blk in range(q_ref.shape[0] // BLOCK)
             for hk in range(N_KV_HEADS) for half in range(2)]
    assert nslot % 2 == 0

    def rows_of(blk):
        return slice(blk * BLOCK, (blk + 1) * BLOCK)

    def key_rows(half):
        return slice(half * HALF_Q, half * HALF_Q + ATT_KEYS)

    for slot in range(nslot):
        dead = slice(ATT_KEYS, nkeys) if slot % 2 == 0 else slice(0, HALF_Q)
        p_scr[slot, dead, :] = jnp.zeros((HALF_Q, p_scr.shape[2]), BF16)

    def scores(idx):
        blk, hk, half = items[idx]
        variant = first_tile if blk == 0 else 0
        cs = slice(hk * LANES, (hk + 1) * LANES)
        k_prev = kp_ref[:, cs] if blk == 0 else kc_ref[rows_of(blk - 1), cs]
        k_cur = kc_ref[rows_of(blk), cs]
        kd = (jnp.concatenate([k_prev, k_cur[:HALF_Q]], axis=0) if half == 0
              else jnp.concatenate([k_prev[HALF_Q:], k_cur], axis=0))
        q0 = blk * BLOCK + half * HALF_Q
        qs = jnp.concatenate(
            [q_ref[q0:q0 + HALF_Q, (hk * SLABS + h // HEADS_PER_TILE) * LANES:
                   (hk * SLABS + h // HEADS_PER_TILE + 1) * LANES] * keep[h % HEADS_PER_TILE]
             for h in _head_order(half)], axis=0)
        s = lax.dot_general(kd, qs, nt, preferred_element_type=F32) + bias_ref[variant, half, hk]
        s_scr[idx % nslot, key_rows(half), :] = s
        return jnp.maximum(jnp.max(s, axis=0, keepdims=True), sink_ref[half, hk])

    def softmax(idx, m):
        blk, hk, half = items[idx]
        slot = idx % nslot
        lo = half * HALF_Q
        for r in range(lo, lo + ATT_KEYS, rows):
            p_scr[slot, r:r + rows, :] = jnp.exp2(s_scr[slot, r:r + rows, :] - m).astype(BF16)
        return jnp.exp2(sink_ref[half, hk] - m)

    def weighted_values(idx):
        blk, hk, half = items[idx]
        vs = slice(hk * LANES, hk * LANES + HEAD_DIM)
        v_prev = vp_ref[vs, :] if blk == 0 else vc_ref[vs, rows_of(blk - 1)]
        vt = jnp.concatenate([v_prev, vc_ref[vs, rows_of(blk)]], axis=1)
        lhs_v = jnp.concatenate([vt, ones_rows], axis=0)
        return jnp.dot(lhs_v, p_scr[idx % nslot], preferred_element_type=F32)

    low_q = lax.broadcasted_iota(jnp.int32, (HEAD_DIM, LANES), 1) < HALF_Q

    def finish(blk, hk, ots, sink_ws):
        o = [ots[half][:HEAD_DIM] * (1.0 / (ots[half][HEAD_DIM:HEAD_DIM + 1] + sink_ws[half]))
             for half in range(2)]
        for sl in range(SLABS):
            a, b = (oh[:, sl * LANES:(sl + 1) * LANES] for oh in o)
            even = jnp.where(low_q, a, b)
            odd = pltpu.roll(jnp.where(low_q, b, a), HALF_Q, axis=1)
            pair = jnp.concatenate([even, odd], axis=0)
            c0 = (hk * SLABS + sl) * LANES
            y_ref[rows_of(blk), c0:c0 + LANES] = (
                pair.T * _silu(gate_ref[rows_of(blk), c0:c0 + LANES])).astype(y_ref.dtype)

    n = len(items)
    offs, sink_ws, outs = {}, {}, {}
    for k in range(-2 * ATT_SKEW, n + ATT_SKEW):
        if 0 <= k + 2 * ATT_SKEW < n:
            offs[k + 2 * ATT_SKEW] = scores(k + 2 * ATT_SKEW)
        if 0 <= k + ATT_SKEW < n:
            sink_ws[k + ATT_SKEW] = softmax(k + ATT_SKEW, offs.pop(k + ATT_SKEW))
        if 0 <= k < n:
            outs[k] = weighted_values(k)
        j = k - ATT_SKEW
        if 0 <= j < n and items[j][2] == 1:
            finish(items[j][0], items[j][1], [outs.pop(j - 1), outs.pop(j)],
                   [sink_ws.pop(j - 1), sink_ws.pop(j)])
        yield


def _proj_tile(y_scr, w_scr, g_ref, x_ref, o_ref, raw_scr):
    y = y_scr[...]
    for c in range(w_scr.shape[0]):
        for n0 in range(0, W_CHUNK, MXU_COLS):
            raw_scr[:, c * W_CHUNK + n0:c * W_CHUNK + n0 + MXU_COLS] = jnp.dot(
                y, w_scr[c, :, n0:n0 + MXU_COLS], preferred_element_type=F32)
            yield
    step = y_scr.shape[0] // PROJ_TAIL_PIECES
    for r0 in range(0, y_scr.shape[0], step):
        o = raw_scr[r0:r0 + step, :]
        o_ref[r0:r0 + step, :] = x_ref[r0:r0 + step, :] + o * _rms_scale(o) * g_ref[...]
        yield


def _attn_proj_kernel(nchunk, ntiles, tiles_per_seq, as_ref, gs_ref, xs_ref,
                      q_ref, kp_ref, kc_ref, vp_ref, vc_ref, gate_ref, bias_ref, sink_ref,
                      w_ref, g_ref, x_ref, os_ref, o_ref,
                      w_scr, raws_scr, s_scr, p_scr, ynew_scr, yold_scr, raw_scr):
    i = pl.program_id(0)
    p = i - nchunk

    @pl.when(i < nchunk)
    def _():
        wb = w_ref[...].astype(BF16)
        w_scr[i] = wb
        ys = (as_ref[...] * _silu(gs_ref[...])).astype(BF16)
        raws_scr[i] = jnp.dot(ys, wb, preferred_element_type=F32)

    @pl.when(i == nchunk - 1)
    def _():
        o = jnp.concatenate([raws_scr[c] for c in range(nchunk)], axis=1)
        os_ref[:, 0, :] = xs_ref[...] + o * _rms_scale(o) * g_ref[...]

    def attend():
        first = (p % tiles_per_seq == 0).astype(jnp.int32)
        return _band_attn_rounds(first, q_ref, kp_ref, kc_ref, vp_ref, vc_ref, gate_ref, bias_ref,
                                 sink_ref, ynew_scr, s_scr, p_scr)

    def project():
        return _proj_tile(yold_scr, w_scr, g_ref, x_ref, o_ref, raw_scr)

    @pl.when(p == 0)
    def _():
        _interleave((attend(), 1))
        yold_scr[...] = ynew_scr[...]

    @pl.when((p > 0) & (p < ntiles))
    def _():
        _interleave((attend(), ATT_ROUNDS_PER_PROJ_PIECE), (project(), 1))
        yold_scr[...] = ynew_scr[...]

    @pl.when(p == ntiles)
    def _():
        _interleave((project(), 1))


def attn_proj(a_s, gate_s, x_s, q, kdup, vt, gate, bias_band, sink_t, w, g, x, seq_len):
    k, d = w.shape
    m = x.shape[0]
    ns = x_s.shape[0]
    tm = ATT_PROJ_TILE
    nchunk = d // W_CHUNK
    ntiles = m // tm
    tiles_per_seq = seq_len // tm
    per_tile = tm // BLOCK
    assert seq_len % tm == 0 and tm % BLOCK == 0
    att_tile = lambda i: jnp.clip(i - nchunk, 0, ntiles - 1)
    proj_tile = lambda i: jnp.clip(i - nchunk - 1, 0, ntiles - 1)

    def before(i):
        t = att_tile(i)
        return per_tile * t - jnp.where(t % tiles_per_seq == 0, 0, 1)

    rows = lambda n: pl.BlockSpec((tm, n), lambda i: (att_tile(i), 0))
    score_tile = (2 * BLOCK, GROUP * HALF_Q)
    return pl.pallas_call(
        functools.partial(_attn_proj_kernel, nchunk, ntiles, tiles_per_seq),
        grid=(nchunk + ntiles + 1,),
        in_specs=[
            _resident(a_s.shape), _resident(gate_s.shape), _resident(x_s.shape),
            rows(ATT_WIDTH), pl.BlockSpec((BLOCK, 2 * KV_WIDTH), lambda i: (before(i), 0)),
            rows(2 * KV_WIDTH), pl.BlockSpec((2 * KV_WIDTH, BLOCK), lambda i: (0, before(i))),
            pl.BlockSpec((2 * KV_WIDTH, tm), lambda i: (0, att_tile(i))), rows(ATT_WIDTH),
            _resident(bias_band.shape), _resident(sink_t.shape),
            pl.BlockSpec((k, W_CHUNK), lambda i: (0, jnp.minimum(i, nchunk - 1))), _resident((1, d)),
            pl.BlockSpec((tm, d), lambda i: (proj_tile(i), 0)),
        ],
        out_specs=[pl.BlockSpec((ns, 1, d), lambda i: (0, 0, 0)),
                   pl.BlockSpec((tm, d), lambda i: (proj_tile(i), 0))],
        out_shape=[jax.ShapeDtypeStruct((ns, 1, d), F32), jax.ShapeDtypeStruct((m, d), F32)],
        scratch_shapes=[pltpu.VMEM((nchunk, k, W_CHUNK), BF16),
                        pltpu.VMEM((nchunk, x_s.shape[0], W_CHUNK), F32),
                        pltpu.VMEM((ATT_SLOTS,) + score_tile, F32),
                        pltpu.VMEM((ATT_SLOTS,) + score_tile, BF16),
                        pltpu.VMEM((tm, k), BF16), pltpu.VMEM((tm, k), BF16),
                        pltpu.VMEM((tm, d), F32)],
        compiler_params=_params("arbitrary"),
        name="attn_proj",
    )(a_s, gate_s, x_s, q, kdup, kdup, vt, vt, gate, bias_band, sink_t, w, g.reshape(1, d), x)


def _cached_attn_kernel(q_ref, ckt_ref, cvt_ref, kn_ref, vn_ref, sinks_ref, bpast_ref, bnew_ref, o_ref):
    pairs = N_Q_HEADS // 2
    shape = (N_Q_HEADS, KV_WIDTH)
    row = lax.broadcasted_iota(jnp.int32, shape, 0)
    row_head = 2 * (row % pairs) + row // pairs
    own = lax.broadcasted_iota(jnp.int32, shape, 1) // HEAD_DIM == row_head // GROUP
    low = lax.broadcasted_iota(jnp.int32, (pairs, LANES), 1) < HEAD_DIM

    def by_parity(ref):
        return jnp.concatenate([ref[pl.ds(par, pairs, stride=2), :] for par in range(2)], axis=0)

    sink = by_parity(sinks_ref)[:, :1]
    bias_past = by_parity(bpast_ref)
    bias_new = by_parity(bnew_ref)[:, :1]
    nt = (((1,), (1,)), ((), ()))
    seqs = range(q_ref.shape[0])
    qm = []
    for b in seqs:
        pr = jnp.concatenate([q_ref[b:b + 1, r * LANES:(r + 1) * LANES] for r in range(pairs)], axis=0)
        swapped = pltpu.roll(pr, HEAD_DIM, axis=1)
        q2 = jnp.concatenate([jnp.where(low, pr, swapped), jnp.where(low, swapped, pr)], axis=0)
        qt = jnp.concatenate([q2] * (KV_WIDTH // LANES), axis=1)
        qm.append(jnp.where(own, qt, 0.0).astype(BF16))
    s = [jnp.dot(qm[b], ckt_ref[b].astype(BF16), preferred_element_type=F32) + bias_past for b in seqs]
    s_new = [jnp.sum(qm[b].astype(F32) * kn_ref[b:b + 1, :].astype(BF16).astype(F32), axis=-1, keepdims=True)
             + bias_new for b in seqs]
    m = [jnp.maximum(jnp.maximum(jnp.max(s[b], axis=-1, keepdims=True), s_new[b]), sink) for b in seqs]
    p = [jnp.exp(s[b] - m[b]) for b in seqs]
    p_new = [jnp.exp(s_new[b] - m[b]) for b in seqs]
    denom = [jnp.sum(p[b], axis=-1, keepdims=True) + p_new[b] + jnp.exp(sink - m[b]) for b in seqs]
    pv = [lax.dot_general(p[b].astype(BF16), cvt_ref[b].astype(BF16), nt, preferred_element_type=F32)
          for b in seqs]
    for b in seqs:
        o_all = pv[b] + p_new[b].astype(BF16).astype(F32) * vn_ref[b:b + 1, :].astype(BF16).astype(F32)
        o_all = jnp.where(own, o_all, 0.0)
        o = o_all[:, :LANES]
        for k in range(1, KV_WIDTH // LANES):
            o = o + o_all[:, k * LANES:(k + 1) * LANES]
        o = o / denom[b]
        o = o + pltpu.roll(o, HEAD_DIM, axis=1)
        pr = jnp.where(low, o[:pairs], o[pairs:])
        for r in range(pairs):
            o_ref[b:b + 1, r * LANES:(r + 1) * LANES] = pr[r:r + 1, :]


def cached_attention(q, cache_kt, cache_vt, k_new, v_new, sinks, bias_past, bias_new):
    bsz, _, rows = cache_kt.shape
    assert LANES == 2 * HEAD_DIM and rows == LANES and N_Q_HEADS % 2 == 0
    nseq = math.gcd(bsz, SEQS_PER_STEP)
    cache = pl.BlockSpec((nseq, KV_WIDTH, rows), lambda b: (b, 0, 0))
    seq_rows = lambda n: pl.BlockSpec((nseq, n), lambda b: (b, 0))
    return pl.pallas_call(
        _cached_attn_kernel,
        grid=(bsz // nseq,),
        in_specs=[
            seq_rows(ATT_WIDTH), cache, cache, seq_rows(KV_WIDTH), seq_rows(KV_WIDTH),
            _resident((N_Q_HEADS, LANES)), _resident((N_Q_HEADS, rows)), _resident((N_Q_HEADS, LANES)),
        ],
        out_specs=seq_rows(ATT_WIDTH),
        out_shape=jax.ShapeDtypeStruct((bsz, ATT_WIDTH), F32),
        compiler_params=_params("parallel"),
        name="cached_attention",
    )(q, cache_kt, cache_vt, k_new, v_new, sinks, bias_past, bias_new)


def kernel(x_prompt, x_sample, state_conv, state_h, cache_k, cache_v, a_norm_pre, a_norm_post,
           a_w_in, a_conv_w, a_conv_b, a_w_r, a_b_r, a_w_i, a_b_i, a_lambda, a_w_out, kv_norm, w_kv,
           b_norm_pre, b_norm_post, b_w_qg, b_sinks, b_w_out, rel_bias_table):
    bsz, t, d = x_prompt.shape
    dbsz, dt, _ = x_sample.shape
    assert a_w_in.shape[0] == 1 and b_w_qg.shape[0] == 1 and dt == 1
    assert t % BLOCK == 0 and t >= WINDOW
    past_rows = cache_k.shape[1]
    assert past_rows == min(WINDOW, PAST_LEN)

    sinks = b_sinks[0]
    bias_band, sink_t, bias_past, bias_new, sink_col = bias_tables(rel_bias_table, sinks, past_rows)

    tm = 2 * SUB_ROWS
    xp = x_prompt.reshape(bsz * t, d)
    xs = x_sample

    gate_s, hs, s_conv_t, y_even, y_odd, p_conv, p_h = rglru_front(
        xs, jnp.transpose(state_conv[0], (1, 0, 2)), state_h[0], xp, a_norm_pre[0], a_w_in[0],
        a_conv_w[0], a_conv_b[0], a_w_r[0], a_b_r[0], a_w_i[0], a_b_i[0], a_lambda[0], seq_len=t)
    xs1, x1 = proj_norm_res(hs, gate_s, xs, (y_even, y_odd), a_w_out[0], a_norm_post[0], xp, tm)

    ks, vs, qs, gate_sb, q, gate_b, kdup, vt, k_tail, v_tail, s_k, s_v = norm_proj_kvq(
        xs1, x1, kv_norm, b_norm_pre[0], w_kv, b_w_qg[0], SUB_ROWS, seq_len=t)
    cache_kt = jnp.transpose(cache_k, (0, 2, 3, 1)).reshape(dbsz, KV_WIDTH, past_rows)
    cache_vt = jnp.transpose(cache_v, (0, 2, 3, 1)).reshape(dbsz, KV_WIDTH, past_rows)
    os_ = cached_attention(qs, cache_kt, cache_vt,
                           ks, vs,
                           sink_col.reshape(N_Q_HEADS, LANES), bias_past.reshape(N_Q_HEADS, past_rows),
                           bias_new.reshape(N_Q_HEADS, LANES))
    y_sample, y_prompt = attn_proj(os_, gate_sb, xs1, q, kdup, vt, gate_b,
                                   bias_band, sink_t, b_w_out[0], b_norm_post[0], x1, seq_len=t)
    y_prompt = y_prompt.reshape(bsz, t, d)
    p_k = jnp.transpose(k_tail.reshape(bsz, N_KV_HEADS, HEAD_DIM, WINDOW), (0, 3, 1, 2))
    p_v = jnp.transpose(v_tail.reshape(bsz, N_KV_HEADS, HEAD_DIM, WINDOW), (0, 3, 1, 2))

    return (y_prompt, y_sample,
            jnp.transpose(p_conv, (1, 0, 2))[None], p_h[None], p_k, p_v,
            jnp.transpose(s_conv_t, (1, 0, 2))[None], hs[None],
            s_k, s_v)
```

```python
import functools
import itertools
import math

import jax
import jax.numpy as jnp
from jax import lax
from jax.experimental import pallas as pl
from jax.experimental.pallas import tpu as pltpu

F32 = jnp.float32
BF16 = jnp.bfloat16

D_MODEL = 2048
LRU_WIDTH = 2048
LRU_BLOCKS = 8
LRU_BLOCK_W = LRU_WIDTH // LRU_BLOCKS
CONV_W = 4
LRU_C = 8.0
HEAD_DIM = 64
N_Q_HEADS = 32
N_KV_HEADS = 8
GROUP = N_Q_HEADS // N_KV_HEADS
ATT_WIDTH = N_Q_HEADS * HEAD_DIM
KV_WIDTH = N_KV_HEADS * HEAD_DIM
WINDOW = 128
BLOCK = WINDOW
N_BUCKETS = 32
MAX_DISTANCE = 128
RMS_EPS = 1e-6
NEG_INF = -1e30
LOG2_E = 1.4426950408889634
PAST_LEN = 16384

V7X_VMEM_BYTES = 64 * 1024 * 1024
VMEM_LIMIT = V7X_VMEM_BYTES - 8 * 1024 * 1024
SUBLANES = 8
LANES = 128
HEADS_PER_TILE = LANES // HEAD_DIM
SLABS = GROUP // HEADS_PER_TILE
MXU_COLS = 256
SUB_ROWS = 256
W_CHUNK = 512
ATT_ROWS = 64
ATT_SKEW = 2
ATT_SLOTS = 2 * ATT_SKEW + 2
HALF_Q = BLOCK // 2
ATT_KEYS = WINDOW + HALF_Q
SEQS_PER_STEP = 8
BIAS_KV_PER_STEP = 4
SCAN_PIECES_PER_PROJ_PIECE = 2
ATT_PROJ_TILE = 2 * BLOCK
PROJ_TAIL_PIECES = 4
ATT_ROUNDS_PER_PROJ_PIECE = 4
SQRT_FLOOR = 1e-30


def _head_order(half):
    heads = list(range(GROUP))
    return heads if half == 0 else [h ^ 1 for h in heads]


def _params(*semantics):
    return pltpu.CompilerParams(dimension_semantics=semantics, vmem_limit_bytes=VMEM_LIMIT)


def _resident(shape):
    zeros = (0,) * len(shape)
    return pl.BlockSpec(shape, lambda *_: zeros, pipeline_mode=pl.Buffered(1))


def _rms_scale(x):
    return lax.rsqrt(jnp.mean(x * x, axis=-1, keepdims=True) + RMS_EPS)


def _silu(x):
    h = 0.5 * x
    return h * jnp.tanh(h) + h


def _segment_major(rows, inverse=False):
    seg = rows // SUBLANES
    r = lax.broadcasted_iota(jnp.int32, (rows, rows), 0)
    c = lax.broadcasted_iota(jnp.int32, (rows, rows), 1)
    if inverse:
        src = (r % seg) * SUBLANES + r // seg
    else:
        src = (r % SUBLANES) * seg + r // SUBLANES
    return jnp.where(c == src, 1.0, 0.0).astype(BF16)


def _phase_specs(nchunk, tm, k):
    chunk_w = pl.BlockSpec((k, W_CHUNK), lambda i: (0, jnp.minimum(i, nchunk - 1)))
    tile = lambda n: pl.BlockSpec((tm, n), lambda i: (jnp.maximum(i - nchunk, 0), 0))
    return chunk_w, tile


def _dup_heads(x):
    low = lax.broadcasted_iota(jnp.int32, (x.shape[0], LANES), 1) < HEAD_DIM
    out = []
    for c in range(x.shape[1] // LANES):
        col = x[:, c * LANES:(c + 1) * LANES]
        swapped = pltpu.roll(col, HEAD_DIM, axis=1)
        out += [jnp.where(low, col, swapped), jnp.where(low, swapped, col)]
    return jnp.concatenate(out, axis=1)


def _norm_proj_kvq_kernel(nkv, nqg, xs_ref, x_ref, gkv_ref, gq_ref, wkv_ref, wqg_ref,
                          ks_ref, vs_ref, qs_ref, gates_ref,
                          q_ref, gate_ref, kdup_ref, vt_ref, ktail_ref, vtail_ref, ks4_ref, vs4_ref,
                          wkv_scr, wqg_scr):
    i = pl.program_id(0)
    nchunk = nkv + nqg
    q_chunks = ATT_WIDTH // W_CHUNK
    q_scale = 1.0 / math.sqrt(HEAD_DIM)
    q_scale_log2 = q_scale * LOG2_E

    def sample_rows(g_ref):
        xs = xs_ref[...]
        return (xs * _rms_scale(xs) * g_ref[...]).astype(BF16)

    @pl.when(i < nkv)
    def _():
        wb = wkv_ref[...].astype(BF16)
        wkv_scr[i] = wb
        r = jnp.dot(sample_rows(gkv_ref), wb, preferred_element_type=F32)

        def store(flat_ref, heads_ref):
            flat_ref[...] = r
            for g in range(N_KV_HEADS):
                heads_ref[:, 0, g, :] = r[:, g * HEAD_DIM:(g + 1) * HEAD_DIM]

        @pl.when(i == 0)
        def _():
            store(ks_ref, ks4_ref)

        @pl.when(i == 1)
        def _():
            store(vs_ref, vs4_ref)

    @pl.when((i >= nkv) & (i < nchunk))
    def _():
        c = i - nkv
        wb = wqg_ref[...].astype(BF16)
        wqg_scr[c] = wb
        r = jnp.dot(sample_rows(gq_ref), wb, preferred_element_type=F32)

        @pl.when(c < q_chunks)
        def _():
            qs_ref[...] = r * q_scale

        @pl.when(c >= q_chunks)
        def _():
            gates_ref[...] = r

    @pl.when(i >= nchunk)
    def _():
        tm = x_ref.shape[0]
        for rs in _row_blocks(tm):
            x = x_ref[rs, :]
            xh = x * _rms_scale(x)
            xkv = (xh * gkv_ref[...]).astype(BF16)
            xq = (xh * gq_ref[...]).astype(BF16)
            k = jnp.dot(xkv, wkv_scr[0], preferred_element_type=F32)
            v = jnp.dot(xkv, wkv_scr[1], preferred_element_type=F32)
            kdup_ref[rs, :] = _dup_heads(k).astype(BF16)
            vt_ref[:, rs] = _dup_heads(v).T.astype(BF16)
            for c in range(nqg):
                r = jnp.dot(xq, wqg_scr[c], preferred_element_type=F32)
                if c < q_chunks:
                    q_ref[rs, c * W_CHUNK:(c + 1) * W_CHUNK] = (r * q_scale_log2).astype(q_ref.dtype)
                else:
                    cc = c - q_chunks
                    gate_ref[rs, cc * W_CHUNK:(cc + 1) * W_CHUNK] = r
        ktail_ref[0] = k[k.shape[0] - WINDOW:].T
        vtail_ref[0] = v[v.shape[0] - WINDOW:].T


def norm_proj_kvq(xs, x, g_kv, g_q, w_kv, w_qg, tm, seq_len):
    m, d = x.shape
    ns = xs.shape[0]
    assert w_kv.shape[1] == 2 * KV_WIDTH == 2 * W_CHUNK and seq_len % tm == 0 and tm >= WINDOW
    nkv, nqg = w_kv.shape[1] // W_CHUNK, w_qg.shape[1] // W_CHUNK
    nchunk = nkv + nqg
    tiles = seq_len // tm
    tile = lambda n: pl.BlockSpec((tm, n), lambda i: (jnp.maximum(i - nchunk, 0), 0))
    tail = pl.BlockSpec((1, KV_WIDTH, WINDOW), lambda i: (jnp.maximum(i - nchunk, 0) // tiles, 0, 0))
    kv_chunk = lambda i: (0, jnp.minimum(i, nkv - 1))
    qg_chunk = lambda i: (0, jnp.clip(i - nkv, 0, nqg - 1))
    q_chunks = ATT_WIDTH // W_CHUNK
    whole_s = lambda n: pl.BlockSpec((ns, n), lambda i: (0, 0))
    new_heads = pl.BlockSpec((ns, 1, N_KV_HEADS, HEAD_DIM), lambda i: (0, 0, 0, 0))
    return pl.pallas_call(
        functools.partial(_norm_proj_kvq_kernel, nkv, nqg),
        grid=(nchunk + m // tm,),
        in_specs=[
            _resident(xs.shape), tile(d), _resident((1, d)), _resident((1, d)),
            pl.BlockSpec((d, W_CHUNK), kv_chunk), pl.BlockSpec((d, W_CHUNK), qg_chunk),
        ],
        out_specs=[
            whole_s(KV_WIDTH), whole_s(KV_WIDTH),
            pl.BlockSpec((ns, W_CHUNK), lambda i: (0, jnp.clip(i - nkv, 0, q_chunks - 1))),
            pl.BlockSpec((ns, W_CHUNK), lambda i: (0, jnp.clip(i - nkv - q_chunks, 0, nqg - q_chunks - 1))),
            tile(ATT_WIDTH), tile(ATT_WIDTH), tile(2 * KV_WIDTH),
            pl.BlockSpec((2 * KV_WIDTH, tm), lambda i: (0, jnp.maximum(i - nchunk, 0))), tail, tail,
            new_heads, new_heads,
        ],
        out_shape=[
            jax.ShapeDtypeStruct((ns, KV_WIDTH), F32),
            jax.ShapeDtypeStruct((ns, KV_WIDTH), F32),
            jax.ShapeDtypeStruct((ns, ATT_WIDTH), F32),
            jax.ShapeDtypeStruct((ns, w_qg.shape[1] - ATT_WIDTH), F32),
            jax.ShapeDtypeStruct((m, ATT_WIDTH), BF16),
            jax.ShapeDtypeStruct((m, ATT_WIDTH), F32),
            jax.ShapeDtypeStruct((m, 2 * KV_WIDTH), BF16),
            jax.ShapeDtypeStruct((2 * KV_WIDTH, m), BF16),
            jax.ShapeDtypeStruct((m // seq_len, KV_WIDTH, WINDOW), F32),
            jax.ShapeDtypeStruct((m // seq_len, KV_WIDTH, WINDOW), F32),
            jax.ShapeDtypeStruct((ns, 1, N_KV_HEADS, HEAD_DIM), F32),
            jax.ShapeDtypeStruct((ns, 1, N_KV_HEADS, HEAD_DIM), F32),
        ],
        scratch_shapes=[pltpu.VMEM((nkv, d, W_CHUNK), BF16), pltpu.VMEM((nqg, d, W_CHUNK), BF16)],
        compiler_params=_params("arbitrary"),
        name="norm_proj_kvq",
    )(xs, x, g_kv.reshape(1, d), g_q.reshape(1, d), w_kv, w_qg)


def _row_blocks(rows):
    sub = min(rows, SUB_ROWS)
    return [slice(r, r + sub) for r in range(0, rows, sub)]


def _proj_norm_res_kernel(nchunk, nparts, as_ref, gs_ref, xs_ref, *refs):
    y_refs = refs[:nparts]
    w_ref, g_ref, x_ref, os_ref, o_ref, w_scr, raw_scr = refs[nparts:]
    i = pl.program_id(0)

    @pl.when(i < nchunk)
    def _():
        wb = w_ref[...].astype(BF16)
        w_scr[i] = wb
        ys = (as_ref[...] * _silu(gs_ref[...])).astype(BF16)
        raw_scr[i] = jnp.dot(ys, wb, preferred_element_type=F32)

    @pl.when(i == nchunk - 1)
    def _():
        o = jnp.concatenate([raw_scr[c] for c in range(nchunk)], axis=1)
        os_ref[...] = xs_ref[:, 0, :] + o * _rms_scale(o) * g_ref[...]

    @pl.when(i >= nchunk)
    def _():
        tm = x_ref.shape[0]
        part_rows = tm // nparts

        for rs in _row_blocks(tm):
            part, off = divmod(rs.start, part_rows)
            y = y_refs[part][off:off + rs.stop - rs.start, :]
            o = jnp.concatenate([jnp.dot(y, w_scr[c], preferred_element_type=F32)
                                 for c in range(nchunk)], axis=1)
            o_ref[rs, :] = x_ref[rs, :] + o * _rms_scale(o) * g_ref[...]


def proj_norm_res(a_s, gate_s, x_s, y_parts, w, g, x, tm):
    k, d = w.shape
    m = x.shape[0]
    ns = x_s.shape[0]
    nchunk = d // W_CHUNK
    nparts = len(y_parts)
    assert (tm // nparts) % min(tm, SUB_ROWS) == 0
    chunk_w, tile = _phase_specs(nchunk, tm, k)
    part = pl.BlockSpec((tm // nparts, k), lambda i: (jnp.maximum(i - nchunk, 0), 0))
    return pl.pallas_call(
        functools.partial(_proj_norm_res_kernel, nchunk, nparts),
        grid=(nchunk + m // tm,),
        in_specs=[_resident(a_s.shape), _resident(gate_s.shape), _resident(x_s.shape)]
        + [part] * nparts + [chunk_w, _resident((1, d)), tile(d)],
        out_specs=[pl.BlockSpec((ns, d), lambda i: (0, 0)), tile(d)],
        out_shape=[jax.ShapeDtypeStruct((ns, d), F32), jax.ShapeDtypeStruct((m, d), F32)],
        scratch_shapes=[pltpu.VMEM((nchunk, k, W_CHUNK), BF16),
                        pltpu.VMEM((nchunk, ns, W_CHUNK), F32)],
        compiler_params=_params("arbitrary"),
        name="proj_norm_res",
    )(a_s, gate_s, x_s, *y_parts, w, g.reshape(1, d), x)


def _lru_gate_dots(conv, wr_half, wi_half):
    cb = conv.astype(BF16)
    return (jnp.dot(cb, wr_half, preferred_element_type=F32),
            jnp.dot(cb, wi_half, preferred_element_type=F32))


def _lru_gates(conv, wr_half, br, wi_half, bi, lam):
    return _lru_gate_math(conv, _lru_gate_dots(conv, wr_half, wi_half), br, bi, lam)


def _lru_gate_math(conv, half_pre, br, bi, lam):
    th_r = jnp.tanh(half_pre[0] + 0.5 * br)
    th_i = jnp.tanh(half_pre[1] + 0.5 * bi)
    nl = -lam
    softplus = jnp.maximum(nl, 0.0) + jnp.log1p(jnp.exp(-jnp.abs(nl)))
    half = (0.5 * LRU_C) * softplus
    x = th_r * half + half
    a = jnp.exp2(x * -LOG2_E)
    z = jnp.tanh(x) * (a * a + 1.0)
    mult = z * lax.rsqrt(jnp.maximum(z, SQRT_FLOOR))
    hc = 0.5 * conv
    return a, mult * (hc * th_i + hc)


def _interleave(*stages):
    live = [[stage, share] for stage, share in stages]
    while live:
        for entry in list(live):
            try:
                for _ in range(entry[1]):
                    next(entry[0])
            except StopIteration:
                live.remove(entry)


def _in_proj_tile(x_ref, rs, g_ref, w_scr, ug_ref):
    x = x_ref[rs, :]
    xn = (x * _rms_scale(x) * g_ref[...]).astype(BF16)
    xn = jnp.dot(_segment_major(xn.shape[0]), xn, preferred_element_type=F32).astype(BF16)
    for c in range(w_scr.shape[0]):
        for n0 in range(0, W_CHUNK, MXU_COLS):
            ug_ref[:, c * W_CHUNK + n0:c * W_CHUNK + n0 + MXU_COLS] = jnp.dot(
                xn, w_scr[c, :, n0:n0 + MXU_COLS], preferred_element_type=F32)
            yield


def _rglru_tile(ug_ref, y_ref, cw_ref, cb_ref, wr_ref, br_ref, wi_ref, bi_ref, lam_ref, h_scr, tail_scr):
    tc = ug_ref.shape[0]
    seg = tc // SUBLANES
    ntaps = CONV_W - 1
    bw = LRU_BLOCK_W
    sub = lax.broadcasted_iota(jnp.int32, (SUBLANES, bw), 0)
    first = sub == 0
    time_order = _segment_major(tc, inverse=True)

    def shift_in(x, row0):
        return jnp.where(first, row0, pltpu.roll(x, 1, axis=0))

    def group(x, j):
        return x[j * SUBLANES:(j + 1) * SUBLANES]

    def store_time_order(cols, y):
        y_ref[:, cols] = jnp.dot(time_order, y, preferred_element_type=F32).astype(y_ref.dtype)

    pending = None
    for n in range(LRU_BLOCKS):
        cs = slice(n * bw, (n + 1) * bw)
        u = ug_ref[:, cs]
        tail = tail_scr[:, cs]
        before = [shift_in(group(u, seg - m), tail[ntaps - m:ntaps - m + 1])
                  for m in range(ntaps, 0, -1)]
        ext = jnp.concatenate(before + [u], axis=0)
        tail_scr[:, cs] = jnp.concatenate(
            [group(u, seg - m)[SUBLANES - 1:] for m in range(ntaps, 0, -1)], axis=0)
        cw = cw_ref[:, cs]
        conv = cb_ref[:, cs]
        for tap in range(CONV_W):
            conv = conv + ext[tap * SUBLANES:tap * SUBLANES + tc] * cw[tap:tap + 1]
        yield

        half_pre = _lru_gate_dots(conv, wr_ref[n], wi_ref[n])
        yield

        if pending is not None:
            store_time_order(*pending)
        yield

        a, b = _lru_gate_math(conv, half_pre, br_ref[:, cs], bi_ref[:, cs], lam_ref[:, cs])

        h = b[:SUBLANES]
        acc = a[:SUBLANES]
        h_loc, a_cum = [h], [acc]
        for j in range(1, seg):
            sl = slice(j * SUBLANES, (j + 1) * SUBLANES)
            h = a[sl] * h + b[sl]
            acc = a[sl] * acc
            h_loc.append(h)
            a_cum.append(acc)

        step = 1
        while step < SUBLANES:
            keep = sub >= step
            h = jnp.where(keep, acc * pltpu.roll(h, step, axis=0) + h, h)
            acc = jnp.where(keep, acc * pltpu.roll(acc, step, axis=0), acc)
            step *= 2
        h_prev = h_scr[:, cs]
        after = h + acc * h_prev
        h_in = shift_in(after, h_prev)
        h_scr[:, cs] = after[SUBLANES - 1:]

        hs = jnp.concatenate([h_loc[j] + a_cum[j] * h_in for j in range(seg)], axis=0)
        y = (hs * _silu(ug_ref[:, LRU_WIDTH + n * bw:LRU_WIDTH + (n + 1) * bw])).astype(BF16)
        pending = (cs, y)
        yield

    store_time_order(*pending)
    yield


def _rglru_front_kernel(nchunk, npairs, chunks, xs_ref, x_ref, g_ref, w_ref,
                        scprev_ref, sh0_ref, cw_ref, cb_ref, wr_ref, br_ref, wi_ref, bi_ref, lam_ref,
                        gs_ref, hs_ref, scnew_ref, y_even_ref, y_odd_ref, cnew_ref, hlast_ref,
                        w_scr, wr_scr, wi_scr, us_scr, ug0_scr, ug1_scr, h_scr, tail_scr):
    i = pl.program_id(0)
    p = i - nchunk
    tc = SUB_ROWS
    half = nchunk // 2
    lru = (cw_ref, cb_ref, wr_scr, br_ref, wi_scr, bi_ref, lam_ref, h_scr, tail_scr)

    @pl.when(i < nchunk)
    def _():
        wb = w_ref[...].astype(BF16)
        w_scr[i] = wb
        xs = xs_ref[:, 0, :]
        xsn = (xs * _rms_scale(xs) * g_ref[...]).astype(BF16)
        r = jnp.dot(xsn, wb, preferred_element_type=F32)
        gs_ref[...] = r

        @pl.when(i < half)
        def _():
            us_scr[i] = r

    @pl.when(i == nchunk - 1)
    def _():
        wr_scr[...] = (0.5 * wr_ref[...]).astype(BF16)
        wi_scr[...] = (0.5 * wi_ref[...]).astype(BF16)
        bw = LRU_BLOCK_W
        for n in range(LRU_BLOCKS):
            cs = slice(n * bw, (n + 1) * bw)
            c, off = divmod(n * bw, W_CHUNK)
            u = us_scr[c, :, off:off + bw]
            cw = cw_ref[:, cs]
            conv = cb_ref[:, cs]
            for tap in range(CONV_W - 1):
                conv = conv + scprev_ref[tap, :, cs] * cw[tap:tap + 1]
                if tap > 0:
                    scnew_ref[tap - 1, :, cs] = scprev_ref[tap, :, cs]
            conv = conv + u * cw[CONV_W - 1:]
            scnew_ref[CONV_W - 2, :, cs] = u
            a, b = _lru_gates(conv, wr_scr[n], br_ref[:, cs], wi_scr[n], bi_ref[:, cs], lam_ref[:, cs])
            hs_ref[:, cs] = a * sh0_ref[:, cs] + b

    def project_even():
        return _in_proj_tile(x_ref, slice(0, tc), g_ref, w_scr, ug0_scr)

    def scan_odd():
        return _rglru_tile(ug1_scr, y_odd_ref, *lru)

    @pl.when(p == 0)
    def _():
        _interleave((project_even(), 1))

    @pl.when((p > 0) & (p < npairs))
    def _():
        _interleave((project_even(), 1), (scan_odd(), SCAN_PIECES_PER_PROJ_PIECE))

    @pl.when(p == npairs)
    def _():
        _interleave((scan_odd(), 1))

    @pl.when(p > 0)
    def _():
        b = (2 * p - 1) // chunks
        hlast_ref[pl.ds(b, 1), :] = h_scr[...]
        for tap in range(CONV_W - 1):
            cnew_ref[tap, pl.ds(b, 1), :] = tail_scr[tap:tap + 1, :]

    @pl.when((p >= 0) & (p < npairs))
    def _():
        @pl.when((2 * p) % chunks == 0)
        def _():
            h_scr[...] = jnp.zeros(h_scr.shape, F32)
            tail_scr[...] = jnp.zeros(tail_scr.shape, F32)

        _interleave((_in_proj_tile(x_ref, slice(tc, 2 * tc), g_ref, w_scr, ug1_scr), 1),
                    (_rglru_tile(ug0_scr, y_even_ref, *lru), SCAN_PIECES_PER_PROJ_PIECE))


def rglru_front(xs, s_conv_prev, s_h0, x, g, w_in, conv_w, conv_b, w_r, b_r, w_i, b_i, lam,
                seq_len):
    m, d = x.shape
    ns = xs.shape[0]
    w = w_in.shape[1] // 2
    tc = SUB_ROWS
    nchunk = w_in.shape[1] // W_CHUNK
    half = nchunk // 2
    bsz = m // seq_len
    chunks = seq_len // tc
    npairs = m // (2 * tc)
    assert seq_len % (2 * tc) == 0 and tc % (SUBLANES * SUBLANES) == 0 and tc // SUBLANES > CONV_W
    pair = lambda i: jnp.clip(i - nchunk, 0, npairs - 1)
    chunk =lambda i: (0, jnp.minimum(i, nchunk - 1))
    return pl.pallas_call(
        functools.partial(_rglru_front_kernel, nchunk, npairs, chunks),
        grid=(nchunk + npairs + 1,),
        in_specs=[_resident(xs.shape), pl.BlockSpec((2 * tc, d), lambda i: (pair(i), 0)), _resident((1, d)),
                  pl.BlockSpec((d, W_CHUNK), chunk),
                  _resident(s_conv_prev.shape), _resident(s_h0.shape),
                  _resident((CONV_W, w)), _resident((1, w)), _resident(w_r.shape), _resident((1, w)),
                  _resident(w_i.shape), _resident((1, w)), _resident((1, w))],
        out_specs=[
            pl.BlockSpec((ns, W_CHUNK), lambda i: (0, jnp.clip(i - half, 0, half - 1))),
            pl.BlockSpec((ns, w), lambda i: (0, 0)),
            pl.BlockSpec(s_conv_prev.shape, lambda i: (0, 0, 0)),
            pl.BlockSpec((tc, w), lambda i: (pair(i), 0)),
            pl.BlockSpec((tc, w), lambda i: (jnp.clip(i - nchunk - 1, 0, npairs - 1), 0)),
            pl.BlockSpec((CONV_W - 1, bsz, w), lambda i: (0, 0, 0)), pl.BlockSpec((bsz, w), lambda i: (0, 0)),
        ],
        out_shape=[
            jax.ShapeDtypeStruct((ns, w), F32),
            jax.ShapeDtypeStruct((ns, w), F32),
            jax.ShapeDtypeStruct(s_conv_prev.shape, F32),
            jax.ShapeDtypeStruct((m // 2, w), BF16),
            jax.ShapeDtypeStruct((m // 2, w), BF16),
            jax.ShapeDtypeStruct((CONV_W - 1, bsz, w), F32),
            jax.ShapeDtypeStruct((bsz, w), F32),
        ],
        scratch_shapes=[pltpu.VMEM((nchunk, d, W_CHUNK), BF16),
                        pltpu.VMEM(w_r.shape, BF16), pltpu.VMEM(w_i.shape, BF16),
                        pltpu.VMEM((half, ns, W_CHUNK), F32),
                        pltpu.VMEM((tc, 2 * w), F32), pltpu.VMEM((tc, 2 * w), F32),
                        pltpu.VMEM((1, w), F32), pltpu.VMEM((CONV_W - 1, w), F32)],
        compiler_params=_params("arbitrary"),
        name="rglru_front",
    )(xs, x, g.reshape(1, d), w_in, s_conv_prev, s_h0,
      conv_w, conv_b.reshape(1, w), w_r, b_r.reshape(1, w), w_i, b_i.reshape(1, w), lam.reshape(1, w))


def _buckets(dist):
    n = jnp.maximum(dist, 0)
    max_exact = N_BUCKETS // 2
    nf = jnp.maximum(n, 1).astype(F32)
    large = max_exact + jnp.floor(jnp.log(nf / max_exact) / math.log(MAX_DISTANCE / max_exact)
                                  * (N_BUCKETS - max_exact)).astype(jnp.int32)
    large = jnp.minimum(large, N_BUCKETS - 1)
    return jnp.where(n < max_exact, n, large)


def _lookup(bucket, valid, table_ref, head):
    bias = jnp.zeros(bucket.shape, F32)
    for b in range(N_BUCKETS):
        bias = jnp.where(bucket == b, table_ref[b, head], bias)
    return jnp.where(valid, bias, NEG_INF)


def _bias_kernel(table_ref, sinks_ref, band_ref, sinkt_ref, past_ref, new_ref, sinkcol_ref):
    hk = pl.program_id(0)
    key_row =lax.broadcasted_iota(jnp.int32, (ATT_KEYS, HALF_Q), 0)
    true_dist = lax.broadcasted_iota(jnp.int32, (ATT_KEYS, HALF_Q), 1) + BLOCK - key_row
    visible = (true_dist >= 0) & (true_dist < WINDOW)

    def band(row):
        full = pltpu.roll(jnp.broadcast_to(row * LOG2_E, (ATT_KEYS, LANES)), 0, axis=1, stride=1,
                          stride_axis=0)
        return jnp.where(visible, full[:, :HALF_Q], NEG_INF * LOG2_E)

    rows = past_ref.shape[2]
    lane = lax.broadcasted_iota(jnp.int32, (SUBLANES, LANES), 1)
    kind = lax.broadcasted_iota(jnp.int32, (SUBLANES, LANES), 0)
    dists = jnp.where(kind == 0, lane, jnp.where(kind == 1, rows - lane, 0))
    buckets = _buckets(dists)
    valid = (dists >= 0) & (dists < WINDOW)
    kvs = past_ref.shape[0]
    for k, g in itertools.product(range(kvs), range(GROUP)):
        head = (hk * kvs + k) * GROUP + g
        looked = _lookup(buckets, valid, table_ref, head)
        bias = band(looked[0:1])
        for half in range(2):
            slot = _head_order(half).index(g)
            cs = slice(slot * HALF_Q, (slot + 1) * HALF_Q)
            band_ref[0, half, k, :, cs] = bias
            prev_rows = BLOCK - half * HALF_Q
            band_ref[1, half, k, :, cs] = jnp.where(key_row < prev_rows, NEG_INF, bias)
            sinkt_ref[half, k, :, cs] = jnp.full((1, HALF_Q), sinks_ref[head] * LOG2_E, F32)
        past_ref[k, g:g + 1, :] = looked[1:2]
        new_ref[k, g:g + 1, :] = looked[2:3]
        sinkcol_ref[k, g:g + 1, :] = jnp.full((1, LANES), sinks_ref[head], F32)


def bias_tables(table, sinks, past_rows):
    assert WINDOW <= LANES and BLOCK % LANES == 0
    assert past_rows == LANES
    kvs = math.gcd(N_KV_HEADS, BIAS_KV_PER_STEP)
    smem = pl.BlockSpec(memory_space=pltpu.SMEM)
    per_head = pl.BlockSpec((kvs, GROUP, LANES), lambda h: (h, 0, 0))
    return pl.pallas_call(
        _bias_kernel,
        grid=(N_KV_HEADS // kvs,),
        in_specs=[smem, smem],
        out_specs=[
            pl.BlockSpec((2, 2, kvs, ATT_KEYS, GROUP * HALF_Q), lambda h: (0, 0, h, 0, 0)),
            pl.BlockSpec((2, kvs, 1, GROUP * HALF_Q), lambda h: (0, h, 0, 0)),
            per_head, per_head, per_head,
        ],
        out_shape=[
            jax.ShapeDtypeStruct((2, 2, N_KV_HEADS, ATT_KEYS, GROUP * HALF_Q), F32),
            jax.ShapeDtypeStruct((2, N_KV_HEADS, 1, GROUP * HALF_Q), F32),
            jax.ShapeDtypeStruct((N_KV_HEADS, GROUP, past_rows), F32),
            jax.ShapeDtypeStruct((N_KV_HEADS, GROUP, LANES), F32),
            jax.ShapeDtypeStruct((N_KV_HEADS, GROUP, LANES), F32),
        ],
        compiler_params=_params("parallel"),
        name="bias_tables",
    )(table, sinks)


def _band_attn_rounds(first_tile, q_ref, kp_ref, kc_ref, vp_ref, vc_ref, gate_ref, bias_ref, sink_ref,
                      y_ref, s_scr, p_scr):
    nt = (((1,), (1,)), ((), ()))
    low = (lax.broadcasted_iota(jnp.int32, (1, LANES), 1) < HEAD_DIM)
    keep_low = low.astype(BF16)
    keep_high = 1 - keep_low
    keep = (keep_low, keep_high)
    nkeys = 2 * BLOCK
    ones_rows = jnp.where(lax.broadcasted_iota(jnp.int32, (2 * SUBLANES, nkeys), 0) == 0,
                          1.0, 0.0).astype(BF16)
    rows = ATT_ROWS
    nslot = s_scr.shape[0]
    items = [(blk, hk, half) for blk in range(q_ref.shape[0] // BLOCK)
             for hk in range(N_KV_HEADS) for half in range(2)]
    assert nslot % 2 == 0

    def rows_of(blk):
        return slice(blk * BLOCK, (blk + 1) * BLOCK)

    def key_rows(half):
        return slice(half * HALF_Q, half * HALF_Q + ATT_KEYS)

    for slot in range(nslot):
        dead = slice(ATT_KEYS, nkeys) if slot % 2 == 0 else slice(0, HALF_Q)
        p_scr[slot, dead, :] = jnp.zeros((HALF_Q, p_scr.shape[2]), BF16)

    def scores(idx):
        blk, hk, half = items[idx]
        variant = first_tile if blk == 0 else 0
        cs = slice(hk * LANES, (hk + 1) * LANES)
        k_prev = kp_ref[:, cs] if blk == 0 else kc_ref[rows_of(blk - 1), cs]
        k_cur = kc_ref[rows_of(blk), cs]
        kd = (jnp.concatenate([k_prev, k_cur[:HALF_Q]], axis=0) if half == 0
              else jnp.concatenate([k_prev[HALF_Q:], k_cur], axis=0))
        q0 = blk * BLOCK + half * HALF_Q
        qs = jnp.concatenate(
            [q_ref[q0:q0 + HALF_Q, (hk * SLABS + h // HEADS_PER_TILE) * LANES:
                   (hk * SLABS + h // HEADS_PER_TILE + 1) * LANES] * keep[h % HEADS_PER_TILE]
             for h in _head_order(half)], axis=0)
        s = lax.dot_general(kd, qs, nt, preferred_element_type=F32) + bias_ref[variant, half, hk]
        s_scr[idx % nslot, key_rows(half), :] = s
        return jnp.maximum(jnp.max(s, axis=0, keepdims=True), sink_ref[half, hk])

    def softmax(idx, m):
        blk, hk, half = items[idx]
        slot = idx % nslot
        lo = half * HALF_Q
        for r in range(lo, lo + ATT_KEYS, rows):
            p_scr[slot, r:r + rows, :] = jnp.exp2(s_scr[slot, r:r + rows, :] - m).astype(BF16)
        return jnp.exp2(sink_ref[half, hk] - m)

    def weighted_values(idx):
        blk, hk, half = items[idx]
        vs = slice(hk * LANES, hk * LANES + HEAD_DIM)
        v_prev = vp_ref[vs, :] if blk == 0 else vc_ref[vs, rows_of(blk - 1)]
        vt = jnp.concatenate([v_prev, vc_ref[vs, rows_of(blk)]], axis=1)
        lhs_v = jnp.concatenate([vt, ones_rows], axis=0)
        return jnp.dot(lhs_v, p_scr[idx % nslot], preferred_element_type=F32)

    low_q = lax.broadcasted_iota(jnp.int32, (HEAD_DIM, LANES), 1) < HALF_Q

    def finish(blk, hk, ots, sink_ws):
        o = [ots[half][:HEAD_DIM] * (1.0 / (ots[half][HEAD_DIM:HEAD_DIM + 1] + sink_ws[half]))
             for half in range(2)]
        for sl in range(SLABS):
            a, b = (oh[:, sl * LANES:(sl + 1) * LANES] for oh in o)
            even = jnp.where(low_q, a, b)
            odd = pltpu.roll(jnp.where(low_q, b, a), HALF_Q, axis=1)
            pair = jnp.concatenate([even, odd], axis=0)
            c0 = (hk * SLABS + sl) * LANES
            y_ref[rows_of(blk), c0:c0 + LANES] = (
                pair.T * _silu(gate_ref[rows_of(blk), c0:c0 + LANES])).astype(y_ref.dtype)

    n = len(items)
    offs, sink_ws, outs = {}, {}, {}
    for k in range(-2 * ATT_SKEW, n + ATT_SKEW):
        if 0 <= k + 2 * ATT_SKEW < n:
            offs[k + 2 * ATT_SKEW] = scores(k + 2 * ATT_SKEW)
        if 0 <= k + ATT_SKEW < n:
            sink_ws[k + ATT_SKEW] = softmax(k + ATT_SKEW, offs.pop(k + ATT_SKEW))
        if 0 <= k < n:
            outs[k] = weighted_values(k)
        j = k - ATT_SKEW
        if 0 <= j < n and items[j][2] == 1:
            finish(items[j][0], items[j][1], [outs.pop(j - 1), outs.pop(j)],
                   [sink_ws.pop(j - 1), sink_ws.pop(j)])
        yield


def _proj_tile(y_scr, w_scr, g_ref, x_ref, o_ref, raw_scr):
    y = y_scr[...]
    for c in range(w_scr.shape[0]):
        for n0 in range(0, W_CHUNK, MXU_COLS):
            raw_scr[:, c * W_CHUNK + n0:c * W_CHUNK + n0 + MXU_COLS] = jnp.dot(
                y, w_scr[c, :, n0:n0 + MXU_COLS], preferred_element_type=F32)
            yield
    step = y_scr.shape[0] // PROJ_TAIL_PIECES
    for r0 in range(0, y_scr.shape[0], step):
        o = raw_scr[r0:r0 + step, :]
        o_ref[r0:r0 + step, :] = x_ref[r0:r0 + step, :] + o * _rms_scale(o) * g_ref[...]
        yield


def _attn_proj_kernel(nchunk, ntiles, tiles_per_seq, as_ref, gs_ref, xs_ref,
                      q_ref, kp_ref, kc_ref, vp_ref, vc_ref, gate_ref, bias_ref, sink_ref,
                      w_ref, g_ref, x_ref, os_ref, o_ref,
                      w_scr, raws_scr, s_scr, p_scr, ynew_scr, yold_scr, raw_scr):
    i = pl.program_id(0)
    p = i - nchunk

    @pl.when(i < nchunk)
    def _():
        wb = w_ref[...].astype(BF16)
        w_scr[i] = wb
        ys = (as_ref[...] * _silu(gs_ref[...])).astype(BF16)
        raws_scr[i] = jnp.dot(ys, wb, preferred_element_type=F32)

    @pl.when(i == nchunk - 1)
    def _():
        o = jnp.concatenate([raws_scr[c] for c in range(nchunk)], axis=1)
        os_ref[:, 0, :] = xs_ref[...] + o * _rms_scale(o) * g_ref[...]

    def attend():
        first = (p % tiles_per_seq == 0).astype(jnp.int32)
        return _band_attn_rounds(first, q_ref, kp_ref, kc_ref, vp_ref, vc_ref, gate_ref, bias_ref,
                                 sink_ref, ynew_scr, s_scr, p_scr)

    def project():
        return _proj_tile(yold_scr, w_scr, g_ref, x_ref, o_ref, raw_scr)

    @pl.when(p == 0)
    def _():
        _interleave((attend(), 1))
        yold_scr[...] = ynew_scr[...]

    @pl.when((p > 0) & (p < ntiles))
    def _():
        _interleave((attend(), ATT_ROUNDS_PER_PROJ_PIECE), (project(), 1))
        yold_scr[...] = ynew_scr[...]

    @pl.when(p == ntiles)
    def _():
        _interleave((project(), 1))


def attn_proj(a_s, gate_s, x_s, q, kdup, vt, gate, bias_band, sink_t, w, g, x, seq_len):
    k, d = w.shape
    m = x.shape[0]
    ns = x_s.shape[0]
    tm = ATT_PROJ_TILE
    nchunk = d // W_CHUNK
    ntiles = m // tm
    tiles_per_seq = seq_len // tm
    per_tile = tm // BLOCK
    assert seq_len % tm == 0 and tm % BLOCK == 0
    att_tile = lambda i: jnp.clip(i - nchunk, 0, ntiles - 1)
    proj_tile = lambda i: jnp.clip(i - nchunk - 1, 0, ntiles - 1)

    def before(i):
        t = att_tile(i)
        return per_tile * t - jnp.where(t % tiles_per_seq == 0, 0, 1)

    rows = lambda n: pl.BlockSpec((tm, n), lambda i: (att_tile(i), 0))
    score_tile = (2 * BLOCK, GROUP * HALF_Q)
    return pl.pallas_call(
        functools.partial(_attn_proj_kernel, nchunk, ntiles, tiles_per_seq),
        grid=(nchunk + ntiles + 1,),
        in_specs=[
            _resident(a_s.shape), _resident(gate_s.shape), _resident(x_s.shape),
            rows(ATT_WIDTH), pl.BlockSpec((BLOCK, 2 * KV_WIDTH), lambda i: (before(i), 0)),
            rows(2 * KV_WIDTH), pl.BlockSpec((2 * KV_WIDTH, BLOCK), lambda i: (0, before(i))),
            pl.BlockSpec((2 * KV_WIDTH, tm), lambda i: (0, att_tile(i))), rows(ATT_WIDTH),
            _resident(bias_band.shape), _resident(sink_t.shape),
            pl.BlockSpec((k, W_CHUNK), lambda i: (0, jnp.minimum(i, nchunk - 1))), _resident((1, d)),
            pl.BlockSpec((tm, d), lambda i: (proj_tile(i), 0)),
        ],
        out_specs=[pl.BlockSpec((ns, 1, d), lambda i: (0, 0, 0)),
                   pl.BlockSpec((tm, d), lambda i: (proj_tile(i), 0))],
        out_shape=[jax.ShapeDtypeStruct((ns, 1, d), F32), jax.ShapeDtypeStruct((m, d), F32)],
        scratch_shapes=[pltpu.VMEM((nchunk, k, W_CHUNK), BF16),
                        pltpu.VMEM((nchunk, x_s.shape[0], W_CHUNK), F32),
                        pltpu.VMEM((ATT_SLOTS,) + score_tile, F32),
                        pltpu.VMEM((ATT_SLOTS,) + score_tile, BF16),
                        pltpu.VMEM((tm, k), BF16), pltpu.VMEM((tm, k), BF16),
                        pltpu.VMEM((tm, d), F32)],
        compiler_params=_params("arbitrary"),
        name="attn_proj",
    )(a_s, gate_s, x_s, q, kdup, kdup, vt, vt, gate, bias_band, sink_t, w, g.reshape(1, d), x)


def _cached_attn_kernel(q_ref, ckt_ref, cvt_ref, kn_ref, vn_ref, sinks_ref, bpast_ref, bnew_ref, o_ref):
    pairs = N_Q_HEADS // 2
    shape = (N_Q_HEADS, KV_WIDTH)
    row = lax.broadcasted_iota(jnp.int32, shape, 0)
    row_head = 2 * (row % pairs) + row // pairs
    own = lax.broadcasted_iota(jnp.int32, shape, 1) // HEAD_DIM == row_head // GROUP
    low = lax.broadcasted_iota(jnp.int32, (pairs, LANES), 1) < HEAD_DIM

    def by_parity(ref):
        return jnp.concatenate([ref[pl.ds(par, pairs, stride=2), :] for par in range(2)], axis=0)

    sink = by_parity(sinks_ref)[:, :1]
    bias_past = by_parity(bpast_ref)
    bias_new = by_parity(bnew_ref)[:, :1]
    nt = (((1,), (1,)), ((), ()))
    seqs = range(q_ref.shape[0])
    qm = []
    for b in seqs:
        pr = jnp.concatenate([q_ref[b:b + 1, r * LANES:(r + 1) * LANES] for r in range(pairs)], axis=0)
        swapped = pltpu.roll(pr, HEAD_DIM, axis=1)
        q2 = jnp.concatenate([jnp.where(low, pr, swapped), jnp.where(low, swapped, pr)], axis=0)
        qt = jnp.concatenate([q2] * (KV_WIDTH // LANES), axis=1)
        qm.append(jnp.where(own, qt, 0.0).astype(BF16))
    s = [jnp.dot(qm[b], ckt_ref[b].astype(BF16), preferred_element_type=F32) + bias_past for b in seqs]
    s_new = [jnp.sum(qm[b].astype(F32) * kn_ref[b:b + 1, :].astype(BF16).astype(F32), axis=-1, keepdims=True)
             + bias_new for b in seqs]
    m = [jnp.maximum(jnp.maximum(jnp.max(s[b], axis=-1, keepdims=True), s_new[b]), sink) for b in seqs]
    p = [jnp.exp(s[b] - m[b]) for b in seqs]
    p_new = [jnp.exp(s_new[b] - m[b]) for b in seqs]
    denom = [jnp.sum(p[b], axis=-1, keepdims=True) + p_new[b] + jnp.exp(sink - m[b]) for b in seqs]
    pv = [lax.dot_general(p[b].astype(BF16), cvt_ref[b].astype(BF16), nt, preferred_element_type=F32)
          for b in seqs]
    for b in seqs:
        o_all = pv[b] + p_new[b].astype(BF16).astype(F32) * vn_ref[b:b + 1, :].astype(BF16).astype(F32)
        o_all = jnp.where(own, o_all, 0.0)
        o = o_all[:, :LANES]
        for k in range(1, KV_WIDTH // LANES):
            o = o + o_all[:, k * LANES:(k + 1) * LANES]
        o = o / denom[b]
        o = o + pltpu.roll(o, HEAD_DIM, axis=1)
        pr = jnp.where(low, o[:pairs], o[pairs:])
        for r in range(pairs):
            o_ref[b:b + 1, r * LANES:(r + 1) * LANES] = pr[r:r + 1, :]


def cached_attention(q, cache_kt, cache_vt, k_new, v_new, sinks, bias_past, bias_new):
    bsz, _, rows = cache_kt.shape
    assert LANES == 2 * HEAD_DIM and rows == LANES and N_Q_HEADS % 2 == 0
    nseq = math.gcd(bsz, SEQS_PER_STEP)
    cache = pl.BlockSpec((nseq, KV_WIDTH, rows), lambda b: (b, 0, 0))
    seq_rows = lambda n: pl.BlockSpec((nseq, n), lambda b: (b, 0))
    return pl.pallas_call(
        _cached_attn_kernel,
        grid=(bsz // nseq,),
        in_specs=[
            seq_rows(ATT_WIDTH), cache, cache, seq_rows(KV_WIDTH), seq_rows(KV_WIDTH),
            _resident((N_Q_HEADS, LANES)), _resident((N_Q_HEADS, rows)), _resident((N_Q_HEADS, LANES)),
        ],
        out_specs=seq_rows(ATT_WIDTH),
        out_shape=jax.ShapeDtypeStruct((bsz, ATT_WIDTH), F32),
        compiler_params=_params("parallel"),
        name="cached_attention",
    )(q, cache_kt, cache_vt, k_new, v_new, sinks, bias_past, bias_new)


def kernel(x_prompt, x_sample, state_conv, state_h, cache_k, cache_v, a_norm_pre, a_norm_post,
           a_w_in, a_conv_w, a_conv_b, a_w_r, a_b_r, a_w_i, a_b_i, a_lambda, a_w_out, kv_norm, w_kv,
           b_norm_pre, b_norm_post, b_w_qg, b_sinks, b_w_out, rel_bias_table):
    bsz, t, d = x_prompt.shape
    dbsz, dt, _ = x_sample.shape
    assert a_w_in.shape[0] == 1 and b_w_qg.shape[0] == 1 and dt == 1
    assert t % BLOCK == 0 and t >= WINDOW
    past_rows = cache_k.shape[1]
    assert past_rows == min(WINDOW, PAST_LEN)

    sinks = b_sinks[0]
    bias_band, sink_t, bias_past, bias_new, sink_col = bias_tables(rel_bias_table, sinks, past_rows)

    tm = 2 * SUB_ROWS
    xp = x_prompt.reshape(bsz * t, d)
    xs = x_sample

    gate_s, hs, s_conv_t, y_even, y_odd, p_conv, p_h = rglru_front(
        xs, jnp.transpose(state_conv[0], (1, 0, 2)), state_h[0], xp, a_norm_pre[0], a_w_in[0],
        a_conv_w[0], a_conv_b[0], a_w_r[0], a_b_r[0], a_w_i[0], a_b_i[0], a_lambda[0], seq_len=t)
    xs1, x1 = proj_norm_res(hs, gate_s, xs, (y_even, y_odd), a_w_out[0], a_norm_post[0], xp, tm)

    ks, vs, qs, gate_sb, q, gate_b, kdup, vt, k_tail, v_tail, s_k, s_v = norm_proj_kvq(
        xs1, x1, kv_norm, b_norm_pre[0], w_kv, b_w_qg[0], SUB_ROWS, seq_len=t)
    cache_kt = jnp.transpose(cache_k, (0, 2, 3, 1)).reshape(dbsz, KV_WIDTH, past_rows)
    cache_vt = jnp.transpose(cache_v, (0, 2, 3, 1)).reshape(dbsz, KV_WIDTH, past_rows)
    os_ = cached_attention(qs, cache_kt, cache_vt,
                           ks, vs,
                           sink_col.reshape(N_Q_HEADS, LANES), bias_past.reshape(N_Q_HEADS, past_rows),
                           bias_new.reshape(N_Q_HEADS, LANES))
    y_sample, y_prompt = attn_proj(os_, gate_sb, xs1, q, kdup, vt, gate_b,
                                   bias_band, sink_t, b_w_out[0], b_norm_post[0], x1, seq_len=t)
    y_prompt = y_prompt.reshape(bsz, t, d)
    p_k = jnp.transpose(k_tail.reshape(bsz, N_KV_HEADS, HEAD_DIM, WINDOW), (0, 3, 1, 2))
    p_v = jnp.transpose(v_tail.reshape(bsz, N_KV_HEADS, HEAD_DIM, WINDOW), (0, 3, 1, 2))

    return (y_prompt, y_sample,
            jnp.transpose(p_conv, (1, 0, 2))[None], p_h[None], p_k, p_v,
            jnp.transpose(s_conv_t, (1, 0, 2))[None], hs[None],
            s_k, s_v)
```
